```python
import math
import jax, jax.numpy as jnp
from jax import lax
import numpy as np

D_MODEL = 2048
BATCH = 8
SEQ = 4096
DEPTH = 4

N_META = 16
HEAD_DIM = 64
ATTN_WIDTH = D_MODEL // 2
N_HEADS = ATTN_WIDTH // HEAD_DIM
N_KV_HEADS = N_HEADS // 4
KV_GROUP = N_HEADS // N_KV_HEADS
KV_WIDTH = N_KV_HEADS * HEAD_DIM
SSM_WIDTH = D_MODEL - ATTN_WIDTH
SSM_GROUP_CH = 16
SSM_GROUPS = SSM_WIDTH // SSM_GROUP_CH
SSM_STATE = 64
WINDOW = 128
BLOCK = 128
PAD = BLOCK - N_META
D_FF = 4 * D_MODEL
IN_WIDTH = ATTN_WIDTH + 2 * KV_WIDTH + SSM_WIDTH
NORM_EPS = 1e-6
NEG_INF = -1e30
STEP_MIN = 1e-3
STEP_MAX = 1e-1

kernel_name = "hymba_s5_swa_alibi_trunk"


def _rmsnorm(x, g):
    xf = x.astype(jnp.float32)
    y = xf * lax.rsqrt(jnp.mean(xf * xf, axis=-1, keepdims=True) + NORM_EPS)
    return (y * g.astype(jnp.float32)).astype(x.dtype)


def _alibi_slopes():
    h = jnp.arange(1, N_HEADS + 1, dtype=jnp.float32)
    return jnp.exp2(-8.0 * h / N_HEADS)


def _sliding_window_attention(q, k, v, sinks):
    b, L = q.shape[0], q.shape[1]
    dtype = q.dtype
    Lp = L + PAD
    nb = Lp // BLOCK
    qf = q.astype(jnp.float32)
    kf = k.astype(jnp.float32)
    vf = v.astype(jnp.float32)
    pad4 = ((0, 0), (PAD, 0), (0, 0), (0, 0))
    qp = jnp.pad(qf, pad4).reshape(b, nb, BLOCK, N_KV_HEADS, KV_GROUP, HEAD_DIM)
    kp = jnp.pad(kf, pad4).reshape(b, nb, BLOCK, N_KV_HEADS, HEAD_DIM)
    vp = jnp.pad(vf, pad4).reshape(b, nb, BLOCK, N_KV_HEADS, HEAD_DIM)
    pad5 = ((0, 0), (1, 0), (0, 0), (0, 0), (0, 0))
    k_band = jnp.concatenate([jnp.pad(kp, pad5)[:, :-1], kp], axis=2)
    v_band = jnp.concatenate([jnp.pad(vp, pad5)[:, :-1], vp], axis=2)
    k_meta = kf[:, :N_META]
    v_meta = vf[:, :N_META]

    n_idx = jnp.arange(nb)[:, None, None]
    i_idx = jnp.arange(BLOCK)[None, :, None]
    j_idx = jnp.arange(2 * BLOCK)[None, None, :]
    t_pos = n_idx * BLOCK + i_idx - PAD
    s_pos = (n_idx - 1) * BLOCK + j_idx - PAD
    band_mask = (s_pos >= N_META) & (s_pos <= t_pos) & (t_pos - s_pos < WINDOW)
    band_dist = jnp.abs(t_pos - s_pos).astype(jnp.float32)
    m_pos = jnp.arange(N_META)[None, None, :]
    meta_mask = m_pos <= t_pos
    meta_dist = jnp.abs(t_pos - m_pos).astype(jnp.float32)

    slopes = _alibi_slopes().reshape(N_KV_HEADS, KV_GROUP, 1, 1)
    scale = 1.0 / math.sqrt(HEAD_DIM)
    s_band = jnp.einsum('bnqkgd,bnskd->bnkgqs', qp, k_band) * scale
    s_band = s_band - slopes * band_dist[:, None, None]
    s_band = jnp.where(band_mask[:, None, None], s_band, NEG_INF)
    s_meta = jnp.einsum('bnqkgd,bmkd->bnkgqm', qp, k_meta) * scale
    s_meta = s_meta - slopes * meta_dist[:, None, None]
    s_meta = jnp.where(meta_mask[:, None, None], s_meta, NEG_INF)
    sink = jnp.broadcast_to(sinks.astype(jnp.float32).reshape(N_KV_HEADS, KV_GROUP, 1, 1),
                            s_band.shape[:-1] + (1,))
    probs = jax.nn.softmax(jnp.concatenate([s_band, s_meta, sink], axis=-1), axis=-1)
    p_band = probs[..., :2 * BLOCK]
    p_meta = probs[..., 2 * BLOCK:2 * BLOCK + N_META]
    out = (jnp.einsum('bnkgqs,bnskd->bnqkgd', p_band, v_band)
           + jnp.einsum('bnkgqm,bmkd->bnqkgd', p_meta, v_meta))
    out = out.reshape(b, Lp, ATTN_WIDTH)[:, PAD:]
    return out.astype(dtype)


def _ssm_combine(e_i, e_j):
    a_i, b_i = e_i
    a_j, b_j = e_j
    return a_j * a_i, a_j * b_i + b_j


def _s5_mixer(u, lam_re, lam_im, log_step, b_re, b_im, c_re, c_im, d, w_glu, b_glu):
    dtype = u.dtype
    b, L = u.shape[0], u.shape[1]
    ul = jnp.moveaxis(u.astype(jnp.float32).reshape(b, L, SSM_GROUPS, SSM_GROUP_CH), 1, 0)
    lam = lax.complex(lam_re.astype(jnp.float32), lam_im.astype(jnp.float32))
    delta = jnp.exp(log_step.astype(jnp.float32))[:, None]
    lam_bar = jnp.exp(lam * delta)
    b_c = lax.complex(b_re.astype(jnp.float32), b_im.astype(jnp.float32))
    b_bar = ((lam_bar - 1.0) / lam)[..., None] * b_c
    c_c = lax.complex(c_re.astype(jnp.float32), c_im.astype(jnp.float32))
    bu = jnp.einsum('lbgh,gph->lbgp', ul.astype(jnp.complex64), b_bar)
    a = jnp.broadcast_to(lam_bar, (L, 1, SSM_GROUPS, SSM_STATE))
    _, states = lax.associative_scan(_ssm_combine, (a, bu), axis=0)
    y = jnp.real(jnp.einsum('lbgp,ghp->lbgh', states, c_c))
    y = y + d.astype(jnp.float32).reshape(SSM_GROUPS, SSM_GROUP_CH) * ul
    y = jnp.moveaxis(y, 0, 1).reshape(b, L, SSM_WIDTH)
    g = jax.nn.gelu(y)
    out = g * jax.nn.sigmoid(g @ w_glu.astype(jnp.float32) + b_glu.astype(jnp.float32))
    return out.astype(dtype)


def _fwd_setup_inputs(seed: int = 0) -> dict:
    key = jax.random.key(seed)
    ks = jax.random.split(key, 24)
    f32 = jnp.float32
    nrm = lambda k, shape, s: jax.random.normal(k, shape, f32) * s
    x = jax.random.normal(ks[0], (BATCH, SEQ, D_MODEL), f32)
    meta_tokens = nrm(ks[1], (N_META, D_MODEL), 1.0)
    norm_mix_g = 1.0 + nrm(ks[2], (DEPTH, D_MODEL), 0.02)
    w_in = nrm(ks[3], (DEPTH, D_MODEL, IN_WIDTH), D_MODEL ** -0.5)
    q_norm_g = 1.0 + nrm(ks[4], (DEPTH, HEAD_DIM), 0.02)
    k_norm_g = 1.0 + nrm(ks[5], (DEPTH, HEAD_DIM), 0.02)
    attn_sinks = nrm(ks[6], (DEPTH, N_HEADS), 0.5)
    n = jnp.arange(SSM_STATE, dtype=f32)
    ssm_lambda_re = -0.5 + nrm(ks[7], (DEPTH, SSM_GROUPS, SSM_STATE), 1e-3)
    ssm_lambda_im = math.pi * n + nrm(ks[8], (DEPTH, SSM_GROUPS, SSM_STATE), 1e-3)
    ssm_log_step = jax.random.uniform(ks[9], (DEPTH, SSM_GROUPS), f32,
                                      math.log(STEP_MIN), math.log(STEP_MAX))
    bs = (SSM_GROUP_CH ** -0.5) / math.sqrt(2.0)
    cs = (SSM_STATE ** -0.5) / math.sqrt(2.0)
    ssm_b_re = nrm(ks[10], (DEPTH, SSM_GROUPS, SSM_STATE, SSM_GROUP_CH), bs)
    ssm_b_im = nrm(ks[11], (DEPTH, SSM_GROUPS, SSM_STATE, SSM_GROUP_CH), bs)
    ssm_c_re = nrm(ks[12], (DEPTH, SSM_GROUPS, SSM_GROUP_CH, SSM_STATE), cs)
    ssm_c_im = nrm(ks[13], (DEPTH, SSM_GROUPS, SSM_GROUP_CH, SSM_STATE), cs)
    ssm_d = nrm(ks[14], (DEPTH, SSM_WIDTH), 1.0)
    w_glu = nrm(ks[15], (DEPTH, SSM_WIDTH, SSM_WIDTH), SSM_WIDTH ** -0.5)
    b_glu = nrm(ks[16], (DEPTH, SSM_WIDTH), 0.01)
    attn_out_g = 1.0 + nrm(ks[17], (DEPTH, ATTN_WIDTH), 0.02)
    ssm_out_g = 1.0 + nrm(ks[18], (DEPTH, SSM_WIDTH), 0.02)
    w_out = nrm(ks[19], (DEPTH, D_MODEL, D_MODEL), D_MODEL ** -0.5)
    norm_mlp_g = 1.0 + nrm(ks[20], (DEPTH, D_MODEL), 0.02)
    w_up = nrm(ks[21], (DEPTH, D_MODEL, D_FF), D_MODEL ** -0.5)
    w_down = nrm(ks[22], (DEPTH, D_FF, D_MODEL), D_FF ** -0.5)
    return {"x": x, "meta_tokens": meta_tokens, "norm_mix_g": norm_mix_g, "w_in": w_in,
            "q_norm_g": q_norm_g, "k_norm_g": k_norm_g, "attn_sinks": attn_sinks,
            "ssm_lambda_re": ssm_lambda_re, "ssm_lambda_im": ssm_lambda_im,
            "ssm_log_step": ssm_log_step, "ssm_b_re": ssm_b_re, "ssm_b_im": ssm_b_im,
            "ssm_c_re": ssm_c_re, "ssm_c_im": ssm_c_im, "ssm_d": ssm_d,
            "w_glu": w_glu, "b_glu": b_glu, "attn_out_g": attn_out_g, "ssm_out_g": ssm_out_g,
            "w_out": w_out, "norm_mlp_g": norm_mlp_g, "w_up": w_up, "w_down": w_down}


def _fwd_reference(x, meta_tokens, norm_mix_g, w_in, q_norm_g, k_norm_g, attn_sinks,
              ssm_lambda_re, ssm_lambda_im, ssm_log_step, ssm_b_re, ssm_b_im,
              ssm_c_re, ssm_c_im, ssm_d, w_glu, b_glu, attn_out_g, ssm_out_g,
              w_out, norm_mlp_g, w_up, w_down):
    b = x.shape[0]
    meta = jnp.broadcast_to(meta_tokens.astype(x.dtype)[None], (b, N_META, D_MODEL))
    h_res = jnp.concatenate([meta, x], axis=1)
    L = h_res.shape[1]
    for l in range(DEPTH):
        h = _rmsnorm(h_res, norm_mix_g[l])
        proj = h @ w_in[l]
        q = proj[..., :ATTN_WIDTH].reshape(b, L, N_HEADS, HEAD_DIM)
        k = proj[..., ATTN_WIDTH:ATTN_WIDTH + KV_WIDTH].reshape(b, L, N_KV_HEADS, HEAD_DIM)
        v = proj[..., ATTN_WIDTH + KV_WIDTH:ATTN_WIDTH + 2 * KV_WIDTH].reshape(b, L, N_KV_HEADS, HEAD_DIM)
        u = proj[..., ATTN_WIDTH + 2 * KV_WIDTH:]
        q = _rmsnorm(q, q_norm_g[l])
        k = _rmsnorm(k, k_norm_g[l])
        attn = _sliding_window_attention(q, k, v, attn_sinks[l])
        ssm = _s5_mixer(u, ssm_lambda_re[l], ssm_lambda_im[l], ssm_log_step[l],
                        ssm_b_re[l], ssm_b_im[l], ssm_c_re[l], ssm_c_im[l], ssm_d[l],
                        w_glu[l], b_glu[l])
        mix = jnp.concatenate([_rmsnorm(attn, attn_out_g[l]), _rmsnorm(ssm, ssm_out_g[l])], axis=-1)
        h_res = h_res + mix @ w_out[l]
        h2 = _rmsnorm(h_res, norm_mlp_g[l])
        h_res = h_res + jnp.square(jax.nn.relu(h2 @ w_up[l])) @ w_down[l]
    return h_res[:, N_META:]


import jax as _jax
import jax.numpy as _jnp

TWIN_FORMAT = 'train_step'
FWD_PARAMS = ['x', 'meta_tokens', 'norm_mix_g', 'w_in', 'q_norm_g', 'k_norm_g', 'attn_sinks', 'ssm_lambda_re', 'ssm_lambda_im', 'ssm_log_step', 'ssm_b_re', 'ssm_b_im', 'ssm_c_re', 'ssm_c_im', 'ssm_d', 'w_glu', 'b_glu', 'attn_out_g', 'ssm_out_g', 'w_out', 'norm_mlp_g', 'w_up', 'w_down']
TWIN_WEIGHTS = ['meta_tokens', 'norm_mix_g', 'w_in', 'q_norm_g', 'k_norm_g', 'attn_sinks', 'ssm_lambda_re', 'ssm_lambda_im', 'ssm_log_step', 'ssm_b_re', 'ssm_b_im', 'ssm_c_re', 'ssm_c_im', 'ssm_d', 'w_glu', 'b_glu', 'attn_out_g', 'ssm_out_g', 'w_out', 'norm_mlp_g', 'w_up', 'w_down']
TWIN_DIFF_INPUT = 'x'
TWIN_INPUTS = ['x', 'meta_tokens', 'norm_mix_g', 'w_in', 'q_norm_g', 'k_norm_g', 'attn_sinks', 'ssm_lambda_re', 'ssm_lambda_im', 'ssm_log_step', 'ssm_b_re', 'ssm_b_im', 'ssm_c_re', 'ssm_c_im', 'ssm_d', 'w_glu', 'b_glu', 'attn_out_g', 'ssm_out_g', 'w_out', 'norm_mlp_g', 'w_up', 'w_down', 'loss_target', 'm_meta_tokens', 'm_norm_mix_g', 'm_w_in', 'm_q_norm_g', 'm_k_norm_g', 'm_attn_sinks', 'm_ssm_lambda_re', 'm_ssm_lambda_im', 'm_ssm_log_step', 'm_ssm_b_re', 'm_ssm_b_im', 'm_ssm_c_re', 'm_ssm_c_im', 'm_ssm_d', 'm_w_glu', 'm_b_glu', 'm_attn_out_g', 'm_ssm_out_g', 'm_w_out', 'm_norm_mlp_g', 'm_w_up', 'm_w_down', 'v_meta_tokens', 'v_norm_mix_g', 'v_w_in', 'v_q_norm_g', 'v_k_norm_g', 'v_attn_sinks', 'v_ssm_lambda_re', 'v_ssm_lambda_im', 'v_ssm_log_step', 'v_ssm_b_re', 'v_ssm_b_im', 'v_ssm_c_re', 'v_ssm_c_im', 'v_ssm_d', 'v_w_glu', 'v_b_glu', 'v_attn_out_g', 'v_ssm_out_g', 'v_w_out', 'v_norm_mlp_g', 'v_w_up', 'v_w_down']
TWIN_OUTPUTS = ['loss', 'grad_x', 'grad_meta_tokens', 'grad_norm_mix_g', 'grad_w_in', 'grad_q_norm_g', 'grad_k_norm_g', 'grad_attn_sinks', 'grad_ssm_lambda_re', 'grad_ssm_lambda_im', 'grad_ssm_log_step', 'grad_ssm_b_re', 'grad_ssm_b_im', 'grad_ssm_c_re', 'grad_ssm_c_im', 'grad_ssm_d', 'grad_w_glu', 'grad_b_glu', 'grad_attn_out_g', 'grad_ssm_out_g', 'grad_w_out', 'grad_norm_mlp_g', 'grad_w_up', 'grad_w_down', 'delta_meta_tokens', 'delta_norm_mix_g', 'delta_w_in', 'delta_q_norm_g', 'delta_k_norm_g', 'delta_attn_sinks', 'delta_ssm_lambda_re', 'delta_ssm_lambda_im', 'delta_ssm_log_step', 'delta_ssm_b_re', 'delta_ssm_b_im', 'delta_ssm_c_re', 'delta_ssm_c_im', 'delta_ssm_d', 'delta_w_glu', 'delta_b_glu', 'delta_attn_out_g', 'delta_ssm_out_g', 'delta_w_out', 'delta_norm_mlp_g', 'delta_w_up', 'delta_w_down', 'new_m_meta_tokens', 'new_m_norm_mix_g', 'new_m_w_in', 'new_m_q_norm_g', 'new_m_k_norm_g', 'new_m_attn_sinks', 'new_m_ssm_lambda_re', 'new_m_ssm_lambda_im', 'new_m_ssm_log_step', 'new_m_ssm_b_re', 'new_m_ssm_b_im', 'new_m_ssm_c_re', 'new_m_ssm_c_im', 'new_m_ssm_d', 'new_m_w_glu', 'new_m_b_glu', 'new_m_attn_out_g', 'new_m_ssm_out_g', 'new_m_w_out', 'new_m_norm_mlp_g', 'new_m_w_up', 'new_m_w_down', 'new_v_meta_tokens', 'new_v_norm_mix_g', 'new_v_w_in', 'new_v_q_norm_g', 'new_v_k_norm_g', 'new_v_attn_sinks', 'new_v_ssm_lambda_re', 'new_v_ssm_lambda_im', 'new_v_ssm_log_step', 'new_v_ssm_b_re', 'new_v_ssm_b_im', 'new_v_ssm_c_re', 'new_v_ssm_c_im', 'new_v_ssm_d', 'new_v_w_glu', 'new_v_b_glu', 'new_v_attn_out_g', 'new_v_ssm_out_g', 'new_v_w_out', 'new_v_norm_mlp_g', 'new_v_w_up', 'new_v_w_down']
TWIN_LEAF_KINDS = {'loss': 'loss', 'grad_x': 'grad_x', 'grad_meta_tokens': 'grad_w', 'grad_norm_mix_g': 'grad_w', 'grad_w_in': 'grad_w', 'grad_q_norm_g': 'grad_w', 'grad_k_norm_g': 'grad_w', 'grad_attn_sinks': 'grad_w', 'grad_ssm_lambda_re': 'grad_w', 'grad_ssm_lambda_im': 'grad_w', 'grad_ssm_log_step': 'grad_w', 'grad_ssm_b_re': 'grad_w', 'grad_ssm_b_im': 'grad_w', 'grad_ssm_c_re': 'grad_w', 'grad_ssm_c_im': 'grad_w', 'grad_ssm_d': 'grad_w', 'grad_w_glu': 'grad_w', 'grad_b_glu': 'grad_w', 'grad_attn_out_g': 'grad_w', 'grad_ssm_out_g': 'grad_w', 'grad_w_out': 'grad_w', 'grad_norm_mlp_g': 'grad_w', 'grad_w_up': 'grad_w', 'grad_w_down': 'grad_w', 'delta_meta_tokens': 'delta_w', 'delta_norm_mix_g': 'delta_w', 'delta_w_in': 'delta_w', 'delta_q_norm_g': 'delta_w', 'delta_k_norm_g': 'delta_w', 'delta_attn_sinks': 'delta_w', 'delta_ssm_lambda_re': 'delta_w', 'delta_ssm_lambda_im': 'delta_w', 'delta_ssm_log_step': 'delta_w', 'delta_ssm_b_re': 'delta_w', 'delta_ssm_b_im': 'delta_w', 'delta_ssm_c_re': 'delta_w', 'delta_ssm_c_im': 'delta_w', 'delta_ssm_d': 'delta_w', 'delta_w_glu': 'delta_w', 'delta_b_glu': 'delta_w', 'delta_attn_out_g': 'delta_w', 'delta_ssm_out_g': 'delta_w', 'delta_w_out': 'delta_w', 'delta_norm_mlp_g': 'delta_w', 'delta_w_up': 'delta_w', 'delta_w_down': 'delta_w', 'new_m_meta_tokens': 'new_m', 'new_m_norm_mix_g': 'new_m', 'new_m_w_in': 'new_m', 'new_m_q_norm_g': 'new_m', 'new_m_k_norm_g': 'new_m', 'new_m_attn_sinks': 'new_m', 'new_m_ssm_lambda_re': 'new_m', 'new_m_ssm_lambda_im': 'new_m', 'new_m_ssm_log_step': 'new_m', 'new_m_ssm_b_re': 'new_m', 'new_m_ssm_b_im': 'new_m', 'new_m_ssm_c_re': 'new_m', 'new_m_ssm_c_im': 'new_m', 'new_m_ssm_d': 'new_m', 'new_m_w_glu': 'new_m', 'new_m_b_glu': 'new_m', 'new_m_attn_out_g': 'new_m', 'new_m_ssm_out_g': 'new_m', 'new_m_w_out': 'new_m', 'new_m_norm_mlp_g': 'new_m', 'new_m_w_up': 'new_m', 'new_m_w_down': 'new_m', 'new_v_meta_tokens': 'new_v', 'new_v_norm_mix_g': 'new_v', 'new_v_w_in': 'new_v', 'new_v_q_norm_g': 'new_v', 'new_v_k_norm_g': 'new_v', 'new_v_attn_sinks': 'new_v', 'new_v_ssm_lambda_re': 'new_v', 'new_v_ssm_lambda_im': 'new_v', 'new_v_ssm_log_step': 'new_v', 'new_v_ssm_b_re': 'new_v', 'new_v_ssm_b_im': 'new_v', 'new_v_ssm_c_re': 'new_v', 'new_v_ssm_c_im': 'new_v', 'new_v_ssm_d': 'new_v', 'new_v_w_glu': 'new_v', 'new_v_b_glu': 'new_v', 'new_v_attn_out_g': 'new_v', 'new_v_ssm_out_g': 'new_v', 'new_v_w_out': 'new_v', 'new_v_norm_mlp_g': 'new_v', 'new_v_w_up': 'new_v', 'new_v_w_down': 'new_v'}


def _forward(args):
    return _fwd_reference(*[args[k] for k in FWD_PARAMS])


def _output_shape():
    def fwd():
        inp = _fwd_setup_inputs(0)
        return _fwd_reference(*[inp[k] for k in FWD_PARAMS])
    out = _jax.eval_shape(fwd)
    return out.shape, out.dtype

N_MICROBATCH = 1
ADAM_LR = 0.001
ADAM_B1 = 0.9
ADAM_B2 = 0.999
ADAM_EPS = 1e-08
ADAM_WD = 0.01
ADAM_STEP = 10
PER_EXAMPLE_BATCH_AXIS = {'x': 0, 'loss_target': 0}
SHARED_INPUTS = []
_WEIGHT_DTYPES = {'meta_tokens': _jnp.float32, 'norm_mix_g': _jnp.float32, 'w_in': _jnp.float32, 'q_norm_g': _jnp.float32, 'k_norm_g': _jnp.float32, 'attn_sinks': _jnp.float32, 'ssm_lambda_re': _jnp.float32, 'ssm_lambda_im': _jnp.float32, 'ssm_log_step': _jnp.float32, 'ssm_b_re': _jnp.float32, 'ssm_b_im': _jnp.float32, 'ssm_c_re': _jnp.float32, 'ssm_c_im': _jnp.float32, 'ssm_d': _jnp.float32, 'w_glu': _jnp.float32, 'b_glu': _jnp.float32, 'attn_out_g': _jnp.float32, 'ssm_out_g': _jnp.float32, 'w_out': _jnp.float32, 'norm_mlp_g': _jnp.float32, 'w_up': _jnp.float32, 'w_down': _jnp.float32}
MOMENT_SCALE = {'meta_tokens': 3.975366e-01, 'norm_mix_g': 1.922782e+01, 'w_in': 1.711809e+01, 'q_norm_g': 2.923476e+00, 'k_norm_g': 2.936343e+00, 'attn_sinks': 1.810740e+01, 'ssm_lambda_re': 6.531403e-01, 'ssm_lambda_im': 6.402935e-01, 'ssm_log_step': 2.583444e+01, 'ssm_b_re': 5.409113e-01, 'ssm_b_im': 5.489312e-01, 'ssm_c_re': 1.016381e+00, 'ssm_c_im': 9.845825e-01, 'ssm_d': 2.544158e+01, 'w_glu': 3.923010e+00, 'b_glu': 1.069171e+01, 'attn_out_g': 3.324728e+01, 'ssm_out_g': 4.149446e+01, 'w_out': 2.452004e+01, 'norm_mlp_g': 5.263381e+01, 'w_up': 7.947960e+00, 'w_down': 2.516365e+01}


def _to_microbatches(a, axis):
    t = _jnp.moveaxis(a, axis, 0)
    t = t.reshape((N_MICROBATCH, t.shape[0] // N_MICROBATCH) + t.shape[1:])
    return _jnp.moveaxis(t, 1, axis + 1)


def setup_inputs(seed: int = 0) -> dict:
    inp = _fwd_setup_inputs(seed)
    key = _jax.random.fold_in(_jax.random.key(seed), 7919)
    shape, _ = _output_shape()
    out = dict(inp)
    out["loss_target"] = _jax.random.normal(_jax.random.fold_in(key, 0), shape, _jnp.float32)
    for i, name in enumerate(TWIN_WEIGHTS):
        w = inp[name].astype(_jnp.float32)
        if MOMENT_SCALE is None:
            s = _jnp.sqrt(_jnp.mean(_jnp.square(w)) + 1e-30)
        else:
            s = MOMENT_SCALE[name]
        km, kv = _jax.random.split(_jax.random.fold_in(key, i + 1))
        out[name] = w
        out["m_" + name] = s * _jax.random.normal(km, w.shape, _jnp.float32)
        out["v_" + name] = (s * s) * _jax.random.uniform(kv, w.shape, _jnp.float32, 0.5, 1.5)
    if N_MICROBATCH > 1:
        for name, axis in PER_EXAMPLE_BATCH_AXIS.items():
            out[name] = _to_microbatches(out[name], axis)
    return {'x': out['x'], 'meta_tokens': out['meta_tokens'], 'norm_mix_g': out['norm_mix_g'], 'w_in': out['w_in'], 'q_norm_g': out['q_norm_g'], 'k_norm_g': out['k_norm_g'], 'attn_sinks': out['attn_sinks'], 'ssm_lambda_re': out['ssm_lambda_re'], 'ssm_lambda_im': out['ssm_lambda_im'], 'ssm_log_step': out['ssm_log_step'], 'ssm_b_re': out['ssm_b_re'], 'ssm_b_im': out['ssm_b_im'], 'ssm_c_re': out['ssm_c_re'], 'ssm_c_im': out['ssm_c_im'], 'ssm_d': out['ssm_d'], 'w_glu': out['w_glu'], 'b_glu': out['b_glu'], 'attn_out_g': out['attn_out_g'], 'ssm_out_g': out['ssm_out_g'], 'w_out': out['w_out'], 'norm_mlp_g': out['norm_mlp_g'], 'w_up': out['w_up'], 'w_down': out['w_down'], 'loss_target': out['loss_target'], 'm_meta_tokens': out['m_meta_tokens'], 'm_norm_mix_g': out['m_norm_mix_g'], 'm_w_in': out['m_w_in'], 'm_q_norm_g': out['m_q_norm_g'], 'm_k_norm_g': out['m_k_norm_g'], 'm_attn_sinks': out['m_attn_sinks'], 'm_ssm_lambda_re': out['m_ssm_lambda_re'], 'm_ssm_lambda_im': out['m_ssm_lambda_im'], 'm_ssm_log_step': out['m_ssm_log_step'], 'm_ssm_b_re': out['m_ssm_b_re'], 'm_ssm_b_im': out['m_ssm_b_im'], 'm_ssm_c_re': out['m_ssm_c_re'], 'm_ssm_c_im': out['m_ssm_c_im'], 'm_ssm_d': out['m_ssm_d'], 'm_w_glu': out['m_w_glu'], 'm_b_glu': out['m_b_glu'], 'm_attn_out_g': out['m_attn_out_g'], 'm_ssm_out_g': out['m_ssm_out_g'], 'm_w_out': out['m_w_out'], 'm_norm_mlp_g': out['m_norm_mlp_g'], 'm_w_up': out['m_w_up'], 'm_w_down': out['m_w_down'], 'v_meta_tokens': out['v_meta_tokens'], 'v_norm_mix_g': out['v_norm_mix_g'], 'v_w_in': out['v_w_in'], 'v_q_norm_g': out['v_q_norm_g'], 'v_k_norm_g': out['v_k_norm_g'], 'v_attn_sinks': out['v_attn_sinks'], 'v_ssm_lambda_re': out['v_ssm_lambda_re'], 'v_ssm_lambda_im': out['v_ssm_lambda_im'], 'v_ssm_log_step': out['v_ssm_log_step'], 'v_ssm_b_re': out['v_ssm_b_re'], 'v_ssm_b_im': out['v_ssm_b_im'], 'v_ssm_c_re': out['v_ssm_c_re'], 'v_ssm_c_im': out['v_ssm_c_im'], 'v_ssm_d': out['v_ssm_d'], 'v_w_glu': out['v_w_glu'], 'v_b_glu': out['v_b_glu'], 'v_attn_out_g': out['v_attn_out_g'], 'v_ssm_out_g': out['v_ssm_out_g'], 'v_w_out': out['v_w_out'], 'v_norm_mlp_g': out['v_norm_mlp_g'], 'v_w_up': out['v_w_up'], 'v_w_down': out['v_w_down']}


def _loss(weights, diff, rest, loss_target):
    with _jax.named_scope("forward"):
        args = {**rest, TWIN_DIFF_INPUT: diff, **{k: w.astype(_WEIGHT_DTYPES[k]) for k, w in weights.items()}}
        y = _forward(args)
    with _jax.named_scope("loss_head"):
        err = _jnp.square(y.astype(_jnp.float32) - loss_target)
        return 0.5 * _jnp.sum(_jnp.mean(err, axis=-1)) if err.ndim else 0.5 * err


def _adamw(w, g, m, v):
    m = ADAM_B1 * m + (1.0 - ADAM_B1) * g
    v = ADAM_B2 * v + (1.0 - ADAM_B2) * _jnp.square(g)
    m_hat = m / (1.0 - ADAM_B1 ** ADAM_STEP)
    v_hat = v / (1.0 - ADAM_B2 ** ADAM_STEP)
    delta = -ADAM_LR * (m_hat / (_jnp.sqrt(v_hat) + ADAM_EPS) + ADAM_WD * w)
    return delta, m, v


def reference(x, meta_tokens, norm_mix_g, w_in, q_norm_g, k_norm_g, attn_sinks, ssm_lambda_re, ssm_lambda_im, ssm_log_step, ssm_b_re, ssm_b_im, ssm_c_re, ssm_c_im, ssm_d, w_glu, b_glu, attn_out_g, ssm_out_g, w_out, norm_mlp_g, w_up, w_down, loss_target, m_meta_tokens, m_norm_mix_g, m_w_in, m_q_norm_g, m_k_norm_g, m_attn_sinks, m_ssm_lambda_re, m_ssm_lambda_im, m_ssm_log_step, m_ssm_b_re, m_ssm_b_im, m_ssm_c_re, m_ssm_c_im, m_ssm_d, m_w_glu, m_b_glu, m_attn_out_g, m_ssm_out_g, m_w_out, m_norm_mlp_g, m_w_up, m_w_down, v_meta_tokens, v_norm_mix_g, v_w_in, v_q_norm_g, v_k_norm_g, v_attn_sinks, v_ssm_lambda_re, v_ssm_lambda_im, v_ssm_log_step, v_ssm_b_re, v_ssm_b_im, v_ssm_c_re, v_ssm_c_im, v_ssm_d, v_w_glu, v_b_glu, v_attn_out_g, v_ssm_out_g, v_w_out, v_norm_mlp_g, v_w_up, v_w_down):
    given = dict(x=x, meta_tokens=meta_tokens, norm_mix_g=norm_mix_g, w_in=w_in, q_norm_g=q_norm_g, k_norm_g=k_norm_g, attn_sinks=attn_sinks, ssm_lambda_re=ssm_lambda_re, ssm_lambda_im=ssm_lambda_im, ssm_log_step=ssm_log_step, ssm_b_re=ssm_b_re, ssm_b_im=ssm_b_im, ssm_c_re=ssm_c_re, ssm_c_im=ssm_c_im, ssm_d=ssm_d, w_glu=w_glu, b_glu=b_glu, attn_out_g=attn_out_g, ssm_out_g=ssm_out_g, w_out=w_out, norm_mlp_g=norm_mlp_g, w_up=w_up, w_down=w_down, loss_target=loss_target, m_meta_tokens=m_meta_tokens, m_norm_mix_g=m_norm_mix_g, m_w_in=m_w_in, m_q_norm_g=m_q_norm_g, m_k_norm_g=m_k_norm_g, m_attn_sinks=m_attn_sinks, m_ssm_lambda_re=m_ssm_lambda_re, m_ssm_lambda_im=m_ssm_lambda_im, m_ssm_log_step=m_ssm_log_step, m_ssm_b_re=m_ssm_b_re, m_ssm_b_im=m_ssm_b_im, m_ssm_c_re=m_ssm_c_re, m_ssm_c_im=m_ssm_c_im, m_ssm_d=m_ssm_d, m_w_glu=m_w_glu, m_b_glu=m_b_glu, m_attn_out_g=m_attn_out_g, m_ssm_out_g=m_ssm_out_g, m_w_out=m_w_out, m_norm_mlp_g=m_norm_mlp_g, m_w_up=m_w_up, m_w_down=m_w_down, v_meta_tokens=v_meta_tokens, v_norm_mix_g=v_norm_mix_g, v_w_in=v_w_in, v_q_norm_g=v_q_norm_g, v_k_norm_g=v_k_norm_g, v_attn_sinks=v_attn_sinks, v_ssm_lambda_re=v_ssm_lambda_re, v_ssm_lambda_im=v_ssm_lambda_im, v_ssm_log_step=v_ssm_log_step, v_ssm_b_re=v_ssm_b_re, v_ssm_b_im=v_ssm_b_im, v_ssm_c_re=v_ssm_c_re, v_ssm_c_im=v_ssm_c_im, v_ssm_d=v_ssm_d, v_w_glu=v_w_glu, v_b_glu=v_b_glu, v_attn_out_g=v_attn_out_g, v_ssm_out_g=v_ssm_out_g, v_w_out=v_w_out, v_norm_mlp_g=v_norm_mlp_g, v_w_up=v_w_up, v_w_down=v_w_down)
    weights = {n: given[n] for n in TWIN_WEIGHTS}
    shared = {n: given[n] for n in SHARED_INPUTS}
    per_example = {n: given[n] for n in ['x']}
    grad_fn = _jax.value_and_grad(_loss, argnums=(0, 1))

    def one_microbatch(ex, loss_target):
        ex = dict(ex)
        diff = ex.pop(TWIN_DIFF_INPUT)
        return grad_fn(weights, diff, {**shared, **ex}, loss_target)

    if N_MICROBATCH == 1:
        loss, (grad_w, grad_x) = one_microbatch(per_example, given["loss_target"])
    else:
        def body(carry, xs):
            loss_sum, grad_sum = carry
            l_k, (gw_k, gx_k) = one_microbatch(xs[0], xs[1])
            with _jax.named_scope("update"):
                return (loss_sum + l_k, _jax.tree.map(_jnp.add, grad_sum, gw_k)), gx_k

        init = (_jnp.zeros((), _jnp.float32), _jax.tree.map(_jnp.zeros_like, weights))
        (loss, grad_w), grad_x = _jax.lax.scan(body, init, (per_example, given["loss_target"]))
    with _jax.named_scope("update"):
        delta_w, new_m, new_v = {}, {}, {}
        for n in TWIN_WEIGHTS:
            delta_w[n], new_m[n], new_v[n] = _adamw(weights[n], grad_w[n], given["m_" + n], given["v_" + n])
    return (loss, grad_x, *[grad_w[n] for n in TWIN_WEIGHTS], *[delta_w[n] for n in TWIN_WEIGHTS],
            *[new_m[n] for n in TWIN_WEIGHTS], *[new_v[n] for n in TWIN_WEIGHTS])
```

```python
import functools
import math

import numpy as np
import jax
import jax.numpy as jnp
from jax import lax
from jax.experimental import pallas as pl
from jax.experimental.pallas import tpu as pltpu

F32 = jnp.float32
BF16 = jnp.bfloat16
MESH = pl.DeviceIdType.MESH

N_META = 16
HEAD_DIM = 64
KV_GROUP = 4
GROUP_CH = 16
STATE = 64
BLOCK = 128
PAD = BLOCK - N_META
NORM_EPS = 1e-6
NEG_INF = -1e30
LANES = 128
V7X_VMEM_LIMIT_BYTES = 56 * 1024 * 1024

ADAM_LR, ADAM_B1, ADAM_B2, ADAM_EPS, ADAM_WD, ADAM_STEP = 0.001, 0.9, 0.999, 1e-08, 0.01, 10

WEIGHTS = ['meta_tokens', 'norm_mix_g', 'w_in', 'q_norm_g', 'k_norm_g', 'attn_sinks', 'ssm_lambda_re',
           'ssm_lambda_im', 'ssm_log_step', 'ssm_b_re', 'ssm_b_im', 'ssm_c_re', 'ssm_c_im', 'ssm_d', 'w_glu',
           'b_glu', 'attn_out_g', 'ssm_out_g', 'w_out', 'norm_mlp_g', 'w_up', 'w_down']
BIG = ['w_in', 'w_glu', 'w_out', 'w_up', 'w_down']
COL_SHARDED = {'w_in': True, 'w_glu': False, 'w_out': False, 'w_up': True, 'w_down': False}
N_CHIPS = 4


def _pick(n, cands):
    for c in cands:
        if c <= n and n % c == 0:
            return c
    return n


def _params(sem):
    return pltpu.CompilerParams(dimension_semantics=sem, vmem_limit_bytes=V7X_VMEM_LIMIT_BYTES)


def _pcall(body, **kw):
    return pl.pallas_call(body, **kw)


def _sds(shape, dtype):
    return jax.ShapeDtypeStruct(shape, dtype)


_DIMS = {'nn': ((1,), (0,)), 'nt': ((1,), (1,)), 'tn': ((0,), (0,))}


def _mm(name, a, b, mode, out_dtypes, extras=(), epi=None):
    if mode == 'nn':
        (M, K), (_, N) = a.shape, b.shape
    elif mode == 'nt':
        (M, K), (N, _) = a.shape, b.shape
    else:
        (K, M), (_, N) = a.shape, b.shape
    if mode == 'tn':
        tm, tn, tk = _pick(M, (1024, 512, 256, 128)), _pick(N, (512, 640, 256, 128)), _pick(K, (1408, 704, 384, 128))
    else:
        tm, tn, tk = _pick(M, (1408, 704, 384, 128)), _pick(N, (512, 640, 256, 128)), _pick(K, (512, 256, 128))
    nk = K // tk
    a_spec = pl.BlockSpec((tk, tm), lambda i, j, k: (k, i)) if mode == 'tn' else pl.BlockSpec((tm, tk), lambda i, j, k: (i, k))
    b_spec = pl.BlockSpec((tn, tk), lambda i, j, k: (j, k)) if mode == 'nt' else pl.BlockSpec((tk, tn), lambda i, j, k: (k, j))
    ex_specs = [pl.BlockSpec((tm, tn), lambda i, j, k: (i, j)) if kind == 'tile' else pl.BlockSpec((1, tn), lambda i, j, k: (0, j))
                for kind, _ in extras]
    ne, no = len(extras), len(out_dtypes)
    dims = (_DIMS[mode], ((), ()))

    def body(a_ref, b_ref, *rest):
        ex, outs, acc = rest[:ne], rest[ne:ne + no], rest[ne + no]
        k = pl.program_id(2)

        @pl.when(k == 0)
        def _():
            acc[...] = jnp.zeros_like(acc)

        acc[...] += lax.dot_general(a_ref[...].astype(BF16), b_ref[...].astype(BF16), dims, preferred_element_type=F32)

        @pl.when(k == nk - 1)
        def _():
            r = acc[...]
            res = epi(r, *[e[...] for e in ex]) if epi is not None else (r,)
            for o, v in zip(outs, res):
                o[...] = v.astype(o.dtype)

    outs = _pcall(
        body, name=name, grid=(M // tm, N // tn, nk),
        in_specs=[a_spec, b_spec] + ex_specs,
        out_specs=[pl.BlockSpec((tm, tn), lambda i, j, k: (i, j)) for _ in out_dtypes],
        out_shape=[_sds((M, N), d) for d in out_dtypes],
        scratch_shapes=[pltpu.VMEM((tm, tn), F32)],
        compiler_params=_params(("parallel", "parallel", "arbitrary")),
    )(a, b, *[e for _, e in extras])
    return outs


def _ew(name, fn, ins, out_dtypes):
    R, C = ins[0].shape
    tr = _pick(R, tuple(t for t in (1024, 512, 256, 128, 64, 32, 16, 8) if t * C <= 512 * 1024) or (8,))
    n_in = len(ins)

    def body(*refs):
        res = fn(*[r[...] for r in refs[:n_in]])
        for o, v in zip(refs[n_in:], res):
            o[...] = v.astype(o.dtype)

    spec = pl.BlockSpec((tr, C), lambda i: (i, 0))
    return _pcall(body, name=name, grid=(R // tr,), in_specs=[spec] * n_in, out_specs=[spec] * len(out_dtypes),
                  out_shape=[_sds((R, C), d) for d in out_dtypes], compiler_params=_params(("parallel",)))(*ins)


def _adam_math(w, g, m, v):
    m = ADAM_B1 * m + (1.0 - ADAM_B1) * g
    v = ADAM_B2 * v + (1.0 - ADAM_B2) * (g * g)
    m_hat = m / (1.0 - ADAM_B1 ** ADAM_STEP)
    v_hat = v / (1.0 - ADAM_B2 ** ADAM_STEP)
    delta = -ADAM_LR * (m_hat / (jnp.sqrt(v_hat) + ADAM_EPS) + ADAM_WD * w)
    return delta, m, v


def _adamw_stacked(name, g_layers, w, m, v):
    depth, R, C = w.shape
    tr = _pick(R, tuple(t for t in (512, 256, 128, 64, 32, 16, 8) if t * C <= 256 * 1024) or (8,))
    nb = R // tr

    def g_spec(kk):
        return pl.BlockSpec((tr, C), lambda l, i: (jnp.where(l == kk, i, jnp.where(l < kk, 0, nb - 1)), 0))

    st_spec = pl.BlockSpec((None, tr, C), lambda l, i: (l, i, 0))

    def body(*refs):
        g_refs = refs[:depth]
        w_ref, m_ref, v_ref, go_ref, d_ref, mo_ref, vo_ref = refs[depth:]
        l = pl.program_id(0)
        for kk in range(depth):
            @pl.when(l == kk)
            def _(kk=kk):
                g = g_refs[kk][...]
                d, mn, vn = _adam_math(w_ref[...], g, m_ref[...], v_ref[...])
                go_ref[...] = g
                d_ref[...] = d
                mo_ref[...] = mn
                vo_ref[...] = vn

    return _pcall(body, name=name, grid=(depth, nb), in_specs=[g_spec(kk) for kk in range(depth)] + [st_spec] * 3,
                  out_specs=[st_spec] * 4, out_shape=[_sds(w.shape, F32)] * 4,
                  compiler_params=_params(("arbitrary", "arbitrary")))(*g_layers, w, m, v)


def _rms_fwd(name, xs, gs, out_dtype):
    L = xs[0].shape[0]
    ws = [x.shape[1] for x in xs]
    n = len(xs)
    tr = _pick(L, (384, 256, 128))

    def body(*refs):
        o = refs[2 * n]
        off = 0
        for i in range(n):
            x = refs[i][...]
            r = lax.rsqrt(jnp.mean(x * x, axis=-1, keepdims=True) + NORM_EPS)
            o[:, off:off + ws[i]] = ((x * r) * refs[n + i][...]).astype(o.dtype)
            off += ws[i]

    return _pcall(body, name=name, grid=(L // tr,),
                  in_specs=[pl.BlockSpec((tr, w), lambda i: (i, 0)) for w in ws] + [pl.BlockSpec((1, w), lambda i: (0, 0)) for w in ws],
                  out_specs=pl.BlockSpec((tr, sum(ws)), lambda i: (i, 0)), out_shape=_sds((L, sum(ws)), out_dtype),
                  compiler_params=_params(("parallel",)))(*xs, *gs)


def _rms_bwd(name, xs, gs, dy, resid=None):
    L = xs[0].shape[0]
    ws = [x.shape[1] for x in xs]
    n = len(xs)
    tr = _pick(L, (384, 256, 128))
    has_res = resid is not None

    def body(*refs):
        x_refs, g_refs, dy_ref = refs[:n], refs[n:2 * n], refs[2 * n]
        p = 2 * n + 1
        res_ref = refs[p] if has_res else None
        p += 1 if has_res else 0
        dx_refs, dg_refs = refs[p:p + n], refs[p + n:p + 2 * n]
        first = pl.program_id(0) == 0
        off = 0
        for i in range(n):
            x = x_refs[i][...]
            d = dy_ref[:, off:off + ws[i]]
            r = lax.rsqrt(jnp.mean(x * x, axis=-1, keepdims=True) + NORM_EPS)
            xh = x * r
            dg = jnp.sum(d * xh, axis=0, keepdims=True)

            @pl.when(first)
            def _(i=i):
                dg_refs[i][...] = jnp.zeros_like(dg_refs[i])

            dg_refs[i][...] += dg
            dyg = d * g_refs[i][...]
            dx = r * (dyg - xh * jnp.mean(dyg * xh, axis=-1, keepdims=True))
            if has_res:
                dx = dx + res_ref[...]
            dx_refs[i][...] = dx
            off += ws[i]

    in_specs = ([pl.BlockSpec((tr, w), lambda i: (i, 0)) for w in ws] + [pl.BlockSpec((1, w), lambda i: (0, 0)) for w in ws]
                + [pl.BlockSpec((tr, sum(ws)), lambda i: (i, 0))])
    ins = list(xs) + list(gs) + [dy]
    if has_res:
        in_specs.append(pl.BlockSpec((tr, ws[0]), lambda i: (i, 0)))
        ins.append(resid)
    outs = _pcall(body, name=name, grid=(L // tr,), in_specs=in_specs,
                  out_specs=[pl.BlockSpec((tr, w), lambda i: (i, 0)) for w in ws] + [pl.BlockSpec((1, w), lambda i: (0, 0)) for w in ws],
                  out_shape=[_sds((L, w), F32) for w in ws] + [_sds((1, w), F32) for w in ws],
                  compiler_params=_params(("arbitrary",)))(*ins)
    return outs[:n], outs[n:]


def _loss(xl, target):
    Lp, D = xl.shape

    def body(x_ref, t_ref, dy_ref, loss_ref):
        n = pl.program_id(0)

        @pl.when(n == 0)
        def _():
            loss_ref[...] = jnp.zeros_like(loss_ref)
            dy_ref[...] = jnp.zeros_like(dy_ref)

        @pl.when(n > 0)
        def _():
            err = x_ref[...] - t_ref[...]
            dy_ref[...] = err * (1.0 / D)
            loss_ref[...] += jnp.sum(err * err) * (0.5 / D)

    dy, loss = _pcall(body, name="loss_head", grid=(Lp // BLOCK,),
                      in_specs=[pl.BlockSpec((BLOCK, D), lambda n: (n, 0)), pl.BlockSpec((BLOCK, D), lambda n: (jnp.maximum(n - 1, 0), 0))],
                      out_specs=[pl.BlockSpec((BLOCK, D), lambda n: (n, 0)), pl.BlockSpec((8, LANES), lambda n: (0, 0))],
                      out_shape=[_sds((Lp, D), F32), _sds((8, LANES), F32)],
                      compiler_params=_params(("arbitrary",)))(xl, target)
    return loss[0, 0], dy


def _attn_mask_dist(n):
    i = lax.broadcasted_iota(jnp.int32, (BLOCK, 3 * BLOCK), 0)
    j = lax.broadcasted_iota(jnp.int32, (BLOCK, 3 * BLOCK), 1)
    in_band = j < 2 * BLOCK
    band = in_band & (j > i) & (j <= i + BLOCK) & (j >= 2 * BLOCK - BLOCK * n)
    jm = j - 2 * BLOCK
    meta = (~in_band) & (jm >= PAD) & (jm <= BLOCK * n + i)
    dist = jnp.where(in_band, BLOCK + i - j, BLOCK * n + i - jm).astype(F32)
    return band | meta, dist


def _head_norm(x, g):
    r = lax.rsqrt(jnp.mean(x * x, axis=-1, keepdims=True) + NORM_EPS)
    return (x * r) * g, r


def _attn_specs(attn_w, kv_w):
    kb = attn_w // kv_w
    q_spec = pl.BlockSpec((BLOCK, attn_w), lambda n: (n, 0))

    def kv(col):
        return [pl.BlockSpec((BLOCK, kv_w), lambda n: (jnp.maximum(n - 1, 0), col)),
                pl.BlockSpec((BLOCK, kv_w), lambda n: (n, col)),
                pl.BlockSpec((BLOCK, kv_w), lambda n: (0, col))]

    return q_spec, kv(kb), kv(kb + 1)


def _slopes(n_heads):
    return [2.0 ** (-8.0 * (h + 1) / n_heads) for h in range(n_heads)]


def _scores(qn, kn, slope, sink, mask, dist):
    s = lax.dot_general(qn.astype(BF16), kn.astype(BF16), (((1,), (1,)), ((), ())), preferred_element_type=F32)
    s = s * (1.0 / math.sqrt(HEAD_DIM)) - slope * dist
    s = jnp.where(mask, s, NEG_INF)
    m = jnp.maximum(jnp.max(s, axis=-1, keepdims=True), sink)
    p = jnp.exp(s - m)
    ps = jnp.exp(sink - m)
    inv = 1.0 / (jnp.sum(p, axis=-1, keepdims=True) + ps)
    return p * inv, ps * inv


def _attn_fwd(proj, gq, gk, sinks, attn_w, kv_w):
    Lp = proj.shape[0]
    n_heads, n_kv = attn_w // HEAD_DIM, kv_w // HEAD_DIM
    slopes = _slopes(n_heads)
    q_spec, k_specs, v_specs = _attn_specs(attn_w, kv_w)

    def body(q_ref, kp, kc, km, vp, vc, vm, gq_ref, gk_ref, sk_ref, o_ref):
        mask, dist = _attn_mask_dist(pl.program_id(0))
        for kh in range(n_kv):
            cs = slice(kh * HEAD_DIM, (kh + 1) * HEAD_DIM)
            kn, _ = _head_norm(jnp.concatenate([kp[:, cs], kc[:, cs], km[:, cs]], axis=0), gk_ref[...])
            vcat = jnp.concatenate([vp[:, cs], vc[:, cs], vm[:, cs]], axis=0).astype(BF16)
            for g in range(KV_GROUP):
                h = kh * KV_GROUP + g
                hs = slice(h * HEAD_DIM, (h + 1) * HEAD_DIM)
                qn, _ = _head_norm(q_ref[:, hs], gq_ref[...])
                p, _ = _scores(qn, kn, slopes[h], sk_ref[0:1, h:h + 1], mask, dist)
                o_ref[:, hs] = jnp.dot(p.astype(BF16), vcat, preferred_element_type=F32)

    small = lambda w: pl.BlockSpec((1, w), lambda n: (0, 0))
    return _pcall(body, name="attn_fwd", grid=(Lp // BLOCK,),
                  in_specs=[q_spec] + k_specs + v_specs + [small(HEAD_DIM), small(HEAD_DIM), small(n_heads)],
                  out_specs=pl.BlockSpec((BLOCK, attn_w), lambda n: (n, 0)), out_shape=_sds((Lp, attn_w), F32),
                  compiler_params=_params(("parallel",)))(proj, proj, proj, proj, proj, proj, proj, gq, gk, sinks)


def _attn_bwd(proj, attn, dattn, gq, gk, sinks, attn_w, kv_w):
    Lp = proj.shape[0]
    n_heads, n_kv = attn_w // HEAD_DIM, kv_w // HEAD_DIM
    slopes = _slopes(n_heads)
    q_spec, k_specs, v_specs = _attn_specs(attn_w, kv_w)
    scale = 1.0 / math.sqrt(HEAD_DIM)
    tn_dims = (((0,), (0,)), ((), ()))

    def body(q_ref, kp, kc, km, vp, vc, vm, o_ref, do_ref, gq_ref, gk_ref, sk_ref, dq_ref, dk_ref, dv_ref, dgq_ref, dsk_ref):
        n = pl.program_id(0)

        @pl.when(n == 0)
        def _():
            dk_ref[...] = jnp.zeros_like(dk_ref)
            dv_ref[...] = jnp.zeros_like(dv_ref)
            dgq_ref[...] = jnp.zeros_like(dgq_ref)
            dsk_ref[...] = jnp.zeros_like(dsk_ref)

        mask, dist = _attn_mask_dist(n)
        lane = lax.broadcasted_iota(jnp.int32, (1, n_heads), 1)
        rows_prev = pl.ds(pl.multiple_of(jnp.maximum(n - 1, 0) * BLOCK, BLOCK), BLOCK)
        rows_cur = pl.ds(pl.multiple_of(n * BLOCK, BLOCK), BLOCK)
        rows_meta = pl.ds(0, BLOCK)
        dgq = jnp.zeros((1, HEAD_DIM), F32)
        dsk = jnp.zeros((1, n_heads), F32)
        for kh in range(n_kv):
            cs = slice(kh * HEAD_DIM, (kh + 1) * HEAD_DIM)
            kn, _ = _head_norm(jnp.concatenate([kp[:, cs], kc[:, cs], km[:, cs]], axis=0), gk_ref[...])
            kn16 = kn.astype(BF16)
            vcat = jnp.concatenate([vp[:, cs], vc[:, cs], vm[:, cs]], axis=0).astype(BF16)
            dkn = jnp.zeros((3 * BLOCK, HEAD_DIM), F32)
            dvc = jnp.zeros((3 * BLOCK, HEAD_DIM), F32)
            for g in range(KV_GROUP):
                h = kh * KV_GROUP + g
                hs = slice(h * HEAD_DIM, (h + 1) * HEAD_DIM)
                q = q_ref[:, hs]
                qn, rq = _head_norm(q, gq_ref[...])
                p, ps = _scores(qn, kn, slopes[h], sk_ref[0:1, h:h + 1], mask, dist)
                do = do_ref[:, hs]
                dd = jnp.sum(do * o_ref[:, hs], axis=-1, keepdims=True)
                do16 = do.astype(BF16)
                dp = lax.dot_general(do16, vcat, (((1,), (1,)), ((), ())), preferred_element_type=F32)
                ds16 = (p * (dp - dd)).astype(BF16)
                dsk = dsk + jnp.where(lane == h, jnp.sum(-ps * dd), 0.0)
                dqn = jnp.dot(ds16, kn16, preferred_element_type=F32) * scale
                dkn = dkn + lax.dot_general(ds16, qn.astype(BF16), tn_dims, preferred_element_type=F32) * scale
                dvc = dvc + lax.dot_general(p.astype(BF16), do16, tn_dims, preferred_element_type=F32)
                xh = q * rq
                dgq = dgq + jnp.sum(dqn * xh, axis=0, keepdims=True)
                dyg = dqn * gq_ref[...]
                dq_ref[:, hs] = rq * (dyg - xh * jnp.mean(dyg * xh, axis=-1, keepdims=True))
            for part, rows in enumerate((rows_prev, rows_cur, rows_meta)):
                ps_ = slice(part * BLOCK, (part + 1) * BLOCK)
                dk_ref[rows, cs] += dkn[ps_]
                dv_ref[rows, cs] += dvc[ps_]
        dgq_ref[...] += dgq
        dsk_ref[...] += dsk

    small = lambda w: pl.BlockSpec((1, w), lambda n: (0, 0))
    blk = pl.BlockSpec((BLOCK, attn_w), lambda n: (n, 0))
    whole = pl.BlockSpec((Lp, kv_w), lambda n: (0, 0))
    return _pcall(body, name="attn_bwd", grid=(Lp // BLOCK,),
                  in_specs=[q_spec] + k_specs + v_specs + [blk, blk, small(HEAD_DIM), small(HEAD_DIM), small(n_heads)],
                  out_specs=[blk, whole, whole, small(HEAD_DIM), small(n_heads)],
                  out_shape=[_sds((Lp, attn_w), F32), _sds((Lp, kv_w), F32), _sds((Lp, kv_w), F32),
                             _sds((1, HEAD_DIM), F32), _sds((1, n_heads), F32)],
                  compiler_params=_params(("arbitrary",)))(proj, proj, proj, proj, proj, proj, proj, attn, dattn, gq, gk, sinks)


def _knorm_bwd(proj, dkn, gk, attn_w, kv_w):
    Lp = proj.shape[0]
    n_kv = kv_w // HEAD_DIM
    tr = _pick(Lp, (384, 256, 128))

    def body(k_ref, d_ref, g_ref, dk_ref, dg_ref):
        @pl.when(pl.program_id(0) == 0)
        def _():
            dg_ref[...] = jnp.zeros_like(dg_ref)

        dg = jnp.zeros((1, HEAD_DIM), F32)
        for kh in range(n_kv):
            cs = slice(kh * HEAD_DIM, (kh + 1) * HEAD_DIM)
            x = k_ref[:, cs]
            d = d_ref[:, cs]
            r = lax.rsqrt(jnp.mean(x * x, axis=-1, keepdims=True) + NORM_EPS)
            xh = x * r
            dg = dg + jnp.sum(d * xh, axis=0, keepdims=True)
            dyg = d * g_ref[...]
            dk_ref[:, cs] = r * (dyg - xh * jnp.mean(dyg * xh, axis=-1, keepdims=True))
        dg_ref[...] += dg

    return _pcall(body, name="knorm_bwd", grid=(Lp // tr,),
                  in_specs=[pl.BlockSpec((tr, kv_w), lambda i: (i, attn_w // kv_w)), pl.BlockSpec((tr, kv_w), lambda i: (i, 0)),
                            pl.BlockSpec((1, HEAD_DIM), lambda i: (0, 0))],
                  out_specs=[pl.BlockSpec((tr, kv_w), lambda i: (i, 0)), pl.BlockSpec((1, HEAD_DIM), lambda i: (0, 0))],
                  out_shape=[_sds((Lp, kv_w), F32), _sds((1, HEAD_DIM), F32)],
                  compiler_params=_params(("arbitrary",)))(proj, dkn, gk)


def _ssm_bbar(lr, li, ls, br, bi):
    def fn(lr, li, ls, br, bi):
        fr, fi = _zoh_factor(lr, li, ls)
        return fr * br - fi * bi, fr * bi + fi * br

    return _ew("ssm_bbar", fn, [lr, li, ls, br, bi], [F32, F32])


def _lam_bar(lr, li, ls):
    dl = jnp.exp(ls)
    e = jnp.exp(lr * dl)
    return e * jnp.cos(li * dl), e * jnp.sin(li * dl), dl


def _zoh_factor(lr, li, ls):
    ar, ai, _ = _lam_bar(lr, li, ls)
    n2 = lr * lr + li * li
    ivr, ivi = lr / n2, -li / n2
    return (ar - 1.0) * ivr - ai * ivi, (ar - 1.0) * ivi + ai * ivr


SCAN_SHIFTS = (1, 2, 4)


def _ssm_tables(lr, li, ls):
    Wx = lr.shape[1]

    def body(lr_ref, li_ref, ls_ref, tf_ref, tr_ref):
        dl = jnp.exp(ls_ref[...])
        zr, zi = lr_ref[...] * dl, li_ref[...] * dl
        row = lax.broadcasted_iota(jnp.int32, (8, Wx), 0)

        def power(kf):
            e = jnp.exp(kf * zr)
            return e * jnp.cos(kf * zi), e * jnp.sin(kf * zi)

        for ref, rev in ((tf_ref, False), (tr_ref, True)):
            sgn = -1.0 if rev else 1.0
            for k, d in enumerate(SCAN_SHIFTS):
                ar, ai = power(jnp.full((8, Wx), float(d), F32))
                keep = (row < 8 - d) if rev else (row >= d)
                ref[k] = jnp.where(keep, ar, 0.0)
                ref[4 + k] = jnp.where(keep, sgn * ai, 0.0)
            pr, pi = power(((8 - row) if rev else (row + 1)).astype(F32))
            ref[3] = pr
            ref[7] = sgn * pi

    full = pl.BlockSpec((1, Wx), lambda: (0, 0))
    tab = pl.BlockSpec((8, 8, Wx), lambda: (0, 0, 0))
    return _pcall(body, name="ssm_tables", in_specs=[full] * 3, out_specs=[tab, tab],
                  out_shape=[_sds((8, 8, Wx), F32)] * 2,
                  compiler_params=pltpu.CompilerParams(vmem_limit_bytes=V7X_VMEM_LIMIT_BYTES))(lr, li, ls)


def _scan(name, br, bi, tab, reverse, states=None):
    L, Wx = br.shape
    TB = _pick(L, (384, 256, 128))
    CW = _pick(Wx, (512, 256, 128))
    nT, nG = L // TB, TB // 8

    def body(*refs):
        if reverse:
            br_ref, bi_ref, xr_ref, xi_ref, tab_ref, or_ref, oi_ref, s1_ref, s2_ref, cr_ref, ci_ref = refs
        else:
            br_ref, bi_ref, tab_ref, or_ref, oi_ref, cr_ref, ci_ref = refs

        @pl.when(pl.program_id(1) == 0)
        def _():
            cr_ref[...] = jnp.zeros_like(cr_ref)
            ci_ref[...] = jnp.zeros_like(ci_ref)
            if reverse:
                s1_ref[...] = jnp.zeros_like(s1_ref)
                s2_ref[...] = jnp.zeros_like(s2_ref)

        def step(q, carry):
            cr, ci = carry[0], carry[1]
            g = (nG - 1 - q) if reverse else q
            rows = pl.ds(pl.multiple_of(g * 8, 8), 8)
            b_r, b_i = br_ref[rows, :], bi_ref[rows, :]
            sr, si = b_r, b_i
            for k, d in enumerate(SCAN_SHIFTS):
                mr, mi = tab_ref[k], tab_ref[4 + k]
                sh = (8 - d) if reverse else d
                pr, pi = pltpu.roll(sr, sh, 0), pltpu.roll(si, sh, 0)
                sr, si = sr + mr * pr - mi * pi, si + mr * pi + mi * pr
            pwr, pwi = tab_ref[3], tab_ref[7]
            xr = sr + pwr * cr - pwi * ci
            xi = si + pwr * ci + pwi * cr
            or_ref[rows, :] = xr
            oi_ref[rows, :] = xi
            row = 0 if reverse else 7
            out = (jnp.broadcast_to(xr[row:row + 1, :], xr.shape), jnp.broadcast_to(xi[row:row + 1, :], xi.shape))
            if reverse:
                hr, hi = xr - b_r, xi - b_i
                st_r, st_i = xr_ref[rows, :], xi_ref[rows, :]
                out = out + (carry[2] + hr * st_r + hi * st_i, carry[3] + hi * st_r - hr * st_i)
            return out

        init = (cr_ref[...], ci_ref[...])
        if reverse:
            init = init + (jnp.zeros((8, CW), F32), jnp.zeros((8, CW), F32))
        fin = lax.fori_loop(0, nG, step, init)
        cr_ref[...] = fin[0]
        ci_ref[...] = fin[1]
        if reverse:
            s1_ref[...] += fin[2]
            s2_ref[...] += fin[3]

    tmap = (lambda j, t: (nT - 1 - t, j)) if reverse else (lambda j, t: (t, j))
    blk = pl.BlockSpec((TB, CW), tmap)
    tab_spec = pl.BlockSpec((8, 8, CW), lambda j, t: (0, 0, j))
    sum_spec = pl.BlockSpec((8, CW), lambda j, t: (0, j))
    ins = [br, bi] + (list(states) if reverse else []) + [tab]
    in_specs = [blk, blk] + ([blk, blk] if reverse else []) + [tab_spec]
    out_specs = [blk, blk] + ([sum_spec, sum_spec] if reverse else [])
    out_shape = [_sds((L, Wx), F32)] * 2 + ([_sds((8, Wx), F32)] * 2 if reverse else [])
    return _pcall(body, name=name, grid=(Wx // CW, nT), in_specs=in_specs, out_specs=out_specs, out_shape=out_shape,
                  scratch_shapes=[pltpu.VMEM((8, CW), F32), pltpu.VMEM((8, CW), F32)],
                  compiler_params=_params(("parallel", "arbitrary")))(*ins)


def _row_tile(L):
    return _pick(L, (1408, 704, 384, 128))


def _blockproj(name, src, off, w_r, w_i):
    L = src.shape[0]
    T = w_r.shape[0]
    tm = _row_tile(L)

    def body(s_ref, wr_ref, wi_ref, or_ref, oi_ref):
        s = s_ref[...].astype(BF16)
        or_ref[...] = jnp.dot(s, wr_ref[...], preferred_element_type=F32)
        oi_ref[...] = jnp.dot(s, wi_ref[...], preferred_element_type=F32)

    w_spec = pl.BlockSpec((None, LANES, LANES), lambda i, t: (t, 0, 0))
    o_spec = pl.BlockSpec((tm, LANES), lambda i, t: (i, t))
    return _pcall(body, name=name, grid=(L // tm, T),
                  in_specs=[pl.BlockSpec((tm, LANES), lambda i, t: (i, off + t // 4)), w_spec, w_spec],
                  out_specs=[o_spec, o_spec], out_shape=[_sds((L, T * LANES), F32)] * 2,
                  compiler_params=_params(("parallel", "arbitrary")))(src, w_r, w_i)


def _blockproj_grad(name, src, off, gr, gi):
    L = src.shape[0]
    T = gr.shape[1] // LANES
    tm = _row_tile(L)
    tn_dims = (((0,), (0,)), ((), ()))

    def body(s_ref, gr_ref, gi_ref, or_ref, oi_ref):
        @pl.when(pl.program_id(1) == 0)
        def _():
            or_ref[...] = jnp.zeros_like(or_ref)
            oi_ref[...] = jnp.zeros_like(oi_ref)

        s = s_ref[...].astype(BF16)
        or_ref[...] += lax.dot_general(s, gr_ref[...].astype(BF16), tn_dims, preferred_element_type=F32)
        oi_ref[...] += lax.dot_general(s, gi_ref[...].astype(BF16), tn_dims, preferred_element_type=F32)

    g_spec = pl.BlockSpec((tm, LANES), lambda t, i: (i, t))
    o_spec = pl.BlockSpec((None, LANES, LANES), lambda t, i: (t, 0, 0))
    return _pcall(body, name=name, grid=(T, L // tm),
                  in_specs=[pl.BlockSpec((tm, LANES), lambda t, i: (i, off + t // 4)), g_spec, g_spec],
                  out_specs=[o_spec, o_spec], out_shape=[_sds((T, LANES, LANES), F32)] * 2,
                  compiler_params=_params(("parallel", "arbitrary")))(src, gr, gi)


def _gelu(y):
    k = math.sqrt(2.0 / math.pi)
    return 0.5 * y * (1.0 + jnp.tanh(k * (y + 0.044715 * (y * y * y))))


def _gelu_grad(y):
    k = math.sqrt(2.0 / math.pi)
    t = jnp.tanh(k * (y + 0.044715 * (y * y * y)))
    return 0.5 * (1.0 + t) + 0.5 * y * (1.0 - t * t) * (k * (1.0 + 3 * 0.044715 * (y * y)))


def _ssm_out(xr, xi, w_r, w_i, proj, u_off, dvec):
    L = xr.shape[0]
    J = w_r.shape[0]
    SW = w_r.shape[1]
    tm = _row_tile(L)

    def body(xr_ref, xi_ref, wr_ref, wi_ref, u_ref, d_ref, y_ref, gl_ref):
        acc = jnp.dot(xr_ref[...].astype(BF16), wr_ref[...], preferred_element_type=F32)
        acc += jnp.dot(xi_ref[...].astype(BF16), wi_ref[...], preferred_element_type=F32)
        y = acc + d_ref[...] * u_ref[...]
        y_ref[...] = y
        gl_ref[...] = _gelu(y)

    x_spec = pl.BlockSpec((tm, SW), lambda j, i: (i, j))
    w_spec = pl.BlockSpec((None, SW, LANES), lambda j, i: (j, 0, 0))
    o_spec = pl.BlockSpec((tm, LANES), lambda j, i: (i, j))
    return _pcall(body, name="ssm_out", grid=(J, L // tm),
                  in_specs=[x_spec, x_spec, w_spec, w_spec, pl.BlockSpec((tm, LANES), lambda j, i: (i, u_off + j)),
                            pl.BlockSpec((1, LANES), lambda j, i: (0, j))],
                  out_specs=[o_spec, o_spec], out_shape=[_sds((L, J * LANES), F32)] * 2,
                  compiler_params=_params(("parallel", "parallel")))(xr, xi, w_r, w_i, proj, dvec)


def _ssm_du(gr, gi, w_r, w_i, dy, proj, u_off, dvec):
    L = gr.shape[0]
    J = w_r.shape[0]
    SW = w_r.shape[1]
    tm = _row_tile(L)

    def body(gr_ref, gi_ref, wr_ref, wi_ref, dy_ref, u_ref, d_ref, du_ref, dd_ref):
        i = pl.program_id(1)

        @pl.when(i == 0)
        def _():
            dd_ref[...] = jnp.zeros_like(dd_ref)

        acc = jnp.dot(gr_ref[...].astype(BF16), wr_ref[...], preferred_element_type=F32)
        acc += jnp.dot(gi_ref[...].astype(BF16), wi_ref[...], preferred_element_type=F32)
        dy = dy_ref[...]
        row = lax.broadcasted_iota(jnp.int32, (tm, LANES), 0) + i * tm
        du_ref[...] = jnp.where(row >= PAD, acc + d_ref[...] * dy, 0.0)
        dd_ref[...] += jnp.sum(dy * u_ref[...], axis=0, keepdims=True)

    x_spec = pl.BlockSpec((tm, SW), lambda j, i: (i, j))
    w_spec = pl.BlockSpec((None, SW, LANES), lambda j, i: (j, 0, 0))
    o_spec = pl.BlockSpec((tm, LANES), lambda j, i: (i, j))
    vec = pl.BlockSpec((1, LANES), lambda j, i: (0, j))
    return _pcall(body, name="ssm_du", grid=(J, L // tm),
                  in_specs=[x_spec, x_spec, w_spec, w_spec, o_spec, pl.BlockSpec((tm, LANES), lambda j, i: (i, u_off + j)), vec],
                  out_specs=[o_spec, vec], out_shape=[_sds((L, J * LANES), F32), _sds((1, J * LANES), F32)],
                  compiler_params=_params(("parallel", "arbitrary")))(gr, gi, w_r, w_i, dy, proj, dvec)


def _ssm_dc(xr, xi, dy, SW):
    L = xr.shape[0]
    J = dy.shape[1] // LANES
    tm = _row_tile(L)
    tn_dims = (((0,), (0,)), ((), ()))

    def body(xr_ref, xi_ref, dy_ref, or_ref, oi_ref):
        @pl.when(pl.program_id(1) == 0)
        def _():
            or_ref[...] = jnp.zeros_like(or_ref)
            oi_ref[...] = jnp.zeros_like(oi_ref)

        d = dy_ref[...].astype(BF16)
        or_ref[...] += lax.dot_general(xr_ref[...].astype(BF16), d, tn_dims, preferred_element_type=F32)
        oi_ref[...] += lax.dot_general(xi_ref[...].astype(BF16), d, tn_dims, preferred_element_type=F32)

    x_spec = pl.BlockSpec((tm, SW), lambda j, i: (i, j))
    o_spec = pl.BlockSpec((None, SW, LANES), lambda j, i: (j, 0, 0))
    return _pcall(body, name="ssm_dc", grid=(J, L // tm),
                  in_specs=[x_spec, x_spec, pl.BlockSpec((tm, LANES), lambda j, i: (i, j))],
                  out_specs=[o_spec, o_spec], out_shape=[_sds((J, SW, LANES), F32)] * 2,
                  compiler_params=_params(("parallel", "arbitrary")))(xr, xi, dy)


def _glu_dz(ds, gl, z):
    L, W = ds.shape
    tr = _pick(L, (384, 256, 128))

    def body(ds_ref, gl_ref, z_ref, dz_ref, db_ref):
        @pl.when(pl.program_id(0) == 0)
        def _():
            db_ref[...] = jnp.zeros_like(db_ref)

        sg = jax.nn.sigmoid(z_ref[...])
        dz = ds_ref[...] * gl_ref[...] * (sg * (1.0 - sg))
        dz_ref[...] = dz.astype(BF16)
        db_ref[...] += jnp.sum(dz, axis=0, keepdims=True)

    spec = pl.BlockSpec((tr, W), lambda i: (i, 0))
    vec = pl.BlockSpec((1, W), lambda i: (0, 0))
    return _pcall(body, name="glu_dz", grid=(L // tr,), in_specs=[spec] * 3, out_specs=[spec, vec],
                  out_shape=[_sds((L, W), BF16), _sds((1, W), F32)], compiler_params=_params(("arbitrary",)))(ds, gl, z)


def _ssm_param_bwd_flat(lr, li, ls, br, bi, dbbr, dbbi):
    def seg_sum(x):
        for s in (8, 4, 2, 1):
            x = x + pltpu.roll(x, LANES - s, 1)
        return x

    def fn(lr, li, ls, br, bi, dbbr, dbbi):
        fr, fi = _zoh_factor(lr, li, ls)
        return (fr * dbbr + fi * dbbi, fr * dbbi - fi * dbbr,
                seg_sum(br * dbbr + bi * dbbi), seg_sum(br * dbbi - bi * dbbr))

    return _ew("ssm_param_bwd_flat", fn, [lr, li, ls, br, bi, dbbr, dbbi], [F32] * 4)


def _ssm_param_bwd(lr, li, ls, dfr, dfi, s1, s2):
    G, P = lr.shape

    def body(lr_ref, li_ref, ls_ref, dfr_ref, dfi_ref, s1_ref, s2_ref, dlr_ref, dli_ref, dls_ref):
        lr, li = lr_ref[...], li_ref[...]
        ar, ai, dl = _lam_bar(lr, li, ls_ref[...])
        sr, si = s1_ref[0], s2_ref[0]
        for k in range(1, 8):
            sr = sr + s1_ref[k]
            si = si + s2_ref[k]
        a2 = ar * ar + ai * ai
        gar, gai = (sr * ar - si * ai) / a2, (sr * ai + si * ar) / a2
        n2 = lr * lr + li * li
        ivr, ivi = lr / n2, -li / n2
        fr = (ar - 1.0) * ivr - ai * ivi
        fi = (ar - 1.0) * ivi + ai * ivr
        dfr, dfi = dfr_ref[...], dfi_ref[...]
        gar = gar + ivr * dfr + ivi * dfi
        gai = gai + ivr * dfi - ivi * dfr
        wr, wi = -(fr * ivr - fi * ivi), -(fr * ivi + fi * ivr)
        glr, gli = wr * dfr + wi * dfi, wr * dfi - wi * dfr
        gzr, gzi = ar * gar + ai * gai, ar * gai - ai * gar
        dlr_ref[...] = glr + dl * gzr
        dli_ref[...] = gli + dl * gzi
        dls_ref[...] = dl * jnp.sum(lr * gzr + li * gzi, axis=-1, keepdims=True)

    m = pl.BlockSpec((G, P), lambda: (0, 0))
    v = pl.BlockSpec((G, 1), lambda: (0, 0))
    s = pl.BlockSpec((8, G, P), lambda: (0, 0, 0))
    return _pcall(body, name="ssm_param_bwd", in_specs=[m, m, v, m, m, s, s], out_specs=[m, m, v],
                  out_shape=[_sds((G, P), F32), _sds((G, P), F32), _sds((G, 1), F32)])(lr, li, ls, dfr, dfi, s1, s2)


def _tile_mask(G):
    T = G // 2
    e = np.zeros((T, 8, 1, 2, 1), np.float32)
    for t in range(T):
        for c in range(2):
            e[t, (2 * t + c) % 8, 0, c, 0] = 1.0
    return e


def _tile_w(arr):
    G = arr.shape[0]
    a = arr.reshape(G // 2, 1, 2, STATE, GROUP_CH).transpose(0, 1, 4, 2, 3)
    return (a * _tile_mask(G)).reshape(G // 2, LANES, LANES).astype(BF16)


def _tile_w_grad(dw):
    G = dw.shape[0] * 2
    d = dw.reshape(G // 2, 8, GROUP_CH, 2, STATE) * _tile_mask(G)
    return d.sum(axis=1).transpose(0, 2, 3, 1).reshape(G, STATE, GROUP_CH)


def _slab_w(arr):
    G = arr.shape[0]
    a = arr.reshape(G // 8, 8, STATE, 1, GROUP_CH)
    eye = np.eye(8, dtype=np.float32).reshape(1, 8, 1, 8, 1)
    return (a * eye).reshape(G // 8, 8 * STATE, LANES).astype(BF16)


def _slab_w_grad(dw):
    J = dw.shape[0]
    eye = np.eye(8, dtype=np.float32).reshape(1, 8, 1, 8, 1)
    return (dw.reshape(J, 8, STATE, 8, GROUP_CH) * eye).sum(axis=3).reshape(J * 8, STATE, GROUP_CH)


def _exchange(name, ins, out_sds, remote, local):
    n_in, n_out, n_r, n_l = len(ins), len(out_sds), len(remote), len(local)

    def body(*refs):
        in_refs, out_refs = refs[:n_in], refs[n_in:n_in + n_out]
        send_sems, recv_sems, local_sems = refs[n_in + n_out:]
        x, y, c = lax.axis_index("x"), lax.axis_index("y"), lax.axis_index("c")

        def place(px, py, pc):
            return dict(x=px, y=py, c=pc, chip=2 * px + py)

        def flip(mask):
            mx, my, mc = mask
            return ((1 - x) if mx else x, (1 - y) if my else y, (1 - c) if mc else c)

        me = place(x, y, c)
        sends = []
        for k, (ii, src, oi, dst, mask) in enumerate(remote):
            cp = pltpu.make_async_remote_copy(src_ref=src(in_refs[ii], me), dst_ref=dst(out_refs[oi], me),
                                              send_sem=send_sems.at[k], recv_sem=recv_sems.at[k],
                                              device_id=flip(mask), device_id_type=MESH)
            cp.start()
            sends.append(cp)
        locals_ = []
        for k, (ii, src, oi, dst) in enumerate(local):
            cp = pltpu.make_async_copy(src(in_refs[ii], me), dst(out_refs[oi], me), local_sems.at[k])
            cp.start()
            locals_.append(cp)
        for k, (ii, src, oi, dst, mask) in enumerate(remote):
            sends[k].wait_send()
            peer = flip(mask)
            pltpu.make_async_remote_copy(src_ref=src(in_refs[ii], me), dst_ref=dst(out_refs[oi], place(*peer)),
                                         send_sem=send_sems.at[k], recv_sem=recv_sems.at[k],
                                         device_id=peer, device_id_type=MESH).wait_recv()
        for cp in locals_:
            cp.wait()

    any_spec = pl.BlockSpec(memory_space=pl.ANY)
    return _pcall(body, name=name, in_specs=[any_spec] * n_in, out_specs=[any_spec] * n_out, out_shape=list(out_sds),
                  scratch_shapes=[pltpu.SemaphoreType.DMA((n_r,)), pltpu.SemaphoreType.DMA((n_r,)),
                                  pltpu.SemaphoreType.DMA((max(n_l, 1),))])(*ins)


CHIP_MASKS = ((0, 1, 0), (1, 0, 0), (1, 1, 0))
SIBLING = (0, 0, 1)


def _whole(ref, p):
    return ref


def _all_gather(name, shards, col_sharded):
    def dst_view(col):
        def view(ref, p):
            r, cdim = ref.shape[0] // (1 if col else N_CHIPS), ref.shape[1] // (N_CHIPS if col else 1)
            if col:
                return ref.at[:, pl.ds(pl.multiple_of(p["chip"] * cdim, LANES), cdim)]
            return ref.at[pl.ds(pl.multiple_of(p["chip"] * r, 8), r), :]
        return view

    out_sds = [_sds((s.shape[0], s.shape[1] * N_CHIPS) if col else (s.shape[0] * N_CHIPS, s.shape[1]), s.dtype)
               for s, col in zip(shards, col_sharded)]
    remote = [(a, _whole, a, dst_view(col), m) for a, col in enumerate(col_sharded) for m in CHIP_MASKS]
    local = [(a, _whole, a, dst_view(col)) for a, col in enumerate(col_sharded)]
    return _exchange(name, shards, out_sds, remote, local)


def _piece_view(col, j, other):
    def view(ref, p):
        R, C = ref.shape
        cc = (1 - p["c"]) if other else p["c"]
        if col:
            hr, sc = R // 2, C // N_CHIPS
            return ref.at[pl.ds(pl.multiple_of(cc * hr, 16), hr), pl.ds(j * sc, sc)]
        hr = R // (2 * N_CHIPS)
        return ref.at[pl.ds(pl.multiple_of((2 * j + cc) * hr, 8), hr), :]
    return view


def _piece_shape(shape, col):
    R, C = shape
    return (R // 2, C // N_CHIPS) if col else (R // (2 * N_CHIPS), C)


def _reduce_scatter(tag, grads, col_sharded, wire_dtype):
    n = len(grads)
    shapes = [_piece_shape(g.shape, col) for g, col in zip(grads, col_sharded)]

    slot = lambda j: (lambda ref, p: ref.at[j])
    remote = [(a, _piece_view(col_sharded[a], j, True), a, slot(j), SIBLING) for a in range(n) for j in range(N_CHIPS)]
    local = [(a, _piece_view(col_sharded[a], j, False), n + a, slot(j)) for a in range(n) for j in range(N_CHIPS)]
    got = _exchange("rs_sibling_" + tag, grads, [_sds((N_CHIPS,) + s, g.dtype) for s, g in zip(shapes, grads)] * 2, remote, local)
    theirs, mine = got[:n], got[n:]
    chip_sum = [_ew("rs_add2_" + tag, lambda a, b: (a.astype(F32) + b.astype(F32),),
                    [m.reshape(-1, m.shape[-1]), t.reshape(-1, t.shape[-1])], [wire_dtype])[0].reshape(m.shape)
                for m, t in zip(mine, theirs)]

    def send_view(mask):
        return lambda ref, p: ref.at[2 * ((1 - p["x"]) if mask[0] else p["x"]) + ((1 - p["y"]) if mask[1] else p["y"])]
    remote = [(a, send_view(m), a, slot(k), m) for a in range(n) for k, m in enumerate(CHIP_MASKS)]
    local = [(a, lambda ref, p: ref.at[p["chip"]], n + a, _whole) for a in range(n)]
    got = _exchange("rs_chips_" + tag, chip_sum,
                    [_sds((len(CHIP_MASKS),) + s, wire_dtype) for s in shapes] + [_sds(s, wire_dtype) for s in shapes], remote, local)
    landed, own = got[:n], got[n:]
    half = [_ew("rs_add4_" + tag, lambda o, a, b, c: (((o.astype(F32) + a.astype(F32)) + b.astype(F32)) + c.astype(F32),),
                [o, l[0], l[1], l[2]], [F32])[0] for o, l in zip(own, landed)]

    def half_rows(ref, p):
        hr = ref.shape[0] // 2
        return ref.at[pl.ds(pl.multiple_of(p["c"] * hr, 8), hr), :]
    remote = [(a, _whole, a, half_rows, SIBLING) for a in range(n)]
    local = [(a, _whole, a, half_rows) for a in range(n)]
    return _exchange("rs_halves_" + tag, half, [_sds((2 * s[0], s[1]), F32) for s in shapes], remote, local)


def _ssm_prepare(p):
    lr, li, ls = p['ssm_lambda_re'], p['ssm_lambda_im'], p['ssm_log_step']
    G = lr.shape[0]
    flat = lambda a: a.reshape(-1, LANES)
    bc = lambda a: flat(jnp.broadcast_to(a, (G, STATE, GROUP_CH)))
    lr3, li3, ls3 = bc(lr[:, :, None]), bc(li[:, :, None]), bc(ls[:, None, None])
    bbr, bbi = _ssm_bbar(lr3, li3, ls3, flat(p['ssm_b_re']), flat(p['ssm_b_im']))
    bbr, bbi = bbr.reshape(G, STATE, GROUP_CH), bbi.reshape(G, STATE, GROUP_CH)
    row = lambda a: a.reshape(1, G * STATE)
    tf, tr = _ssm_tables(row(lr), row(li), row(jnp.broadcast_to(ls[:, None], (G, STATE))))
    cr = p['ssm_c_re'].transpose(0, 2, 1)
    ci = -p['ssm_c_im'].transpose(0, 2, 1)
    return dict(flat3=(lr3, li3, ls3), tf=tf, tr=tr,
                wb=(_tile_w(bbr), _tile_w(bbi)), wbT=(_slab_w(bbr), _slab_w(bbi)),
                wc=(_slab_w(cr), _slab_w(ci)), wcT=(_tile_w(cr), _tile_w(ci)))


def _layer_fwd(x, p, w, dims):
    attn_w, kv_w, u_off = dims['attn_w'], dims['kv_w'], dims['u_off']
    s = _ssm_prepare(p)
    h = _rms_fwd("norm_mix", [x], [p['norm_mix_g']], BF16)
    proj, = _mm("mm_in", h, w['w_in'], 'nn', [F32])
    attn = _attn_fwd(proj, p['q_norm_g'], p['k_norm_g'], p['attn_sinks'], attn_w, kv_w)
    bur, bui = _blockproj("ssm_bu", proj, u_off, *s['wb'])
    xr, xi = _scan("ssm_scan_fwd", bur, bui, s['tf'], False)
    y, gl = _ssm_out(xr, xi, *s['wc'], proj, u_off, p['ssm_d'])
    ssm, z = _mm("mm_glu", gl, w['w_glu'], 'nn', [F32, F32], extras=[('row', p['b_glu']), ('tile', gl)],
                 epi=lambda acc, b, g: ((lambda zz: (g * jax.nn.sigmoid(zz), zz))(acc + b)))
    mix = _rms_fwd("norm_heads", [attn, ssm], [p['attn_out_g'], p['ssm_out_g']], BF16)
    x_mid, = _mm("mm_out", mix, w['w_out'], 'nn', [F32], extras=[('tile', x)], epi=lambda acc, r: (acc + r,))
    h2 = _rms_fwd("norm_mlp", [x_mid], [p['norm_mlp_g']], BF16)
    a, r = _mm("mm_up", h2, w['w_up'], 'nn', [F32, BF16],
               epi=lambda acc: (acc, jnp.square(jnp.maximum(acc, 0.0))))
    x_out, = _mm("mm_down", r, w['w_down'], 'nn', [F32], extras=[('tile', x_mid)], epi=lambda acc, rr: (acc + rr,))
    saved = dict(x=x, h=h, proj=proj, attn=attn, xr=xr, xi=xi, y=y, gl=gl, z=z, ssm=ssm, mix=mix, x_mid=x_mid, h2=h2, a=a, r=r, s=s)
    return x_out, saved


def _layer_bwd(dx, sv, p, w, dims):
    attn_w, kv_w, u_off = dims['attn_w'], dims['kv_w'], dims['u_off']
    s = sv['s']
    gb, gs = {}, {}
    da, = _mm("mm_down_dx", dx, w['w_down'], 'nt', [BF16], extras=[('tile', sv['a'])],
              epi=lambda acc, a: (acc * (2.0 * jnp.maximum(a, 0.0)),))
    gb['w_down'], = _mm("mm_down_dw", sv['r'], dx, 'tn', [BF16])
    dh2, = _mm("mm_up_dx", da, w['w_up'], 'nt', [F32])
    gb['w_up'], = _mm("mm_up_dw", sv['h2'], da, 'tn', [BF16])
    (dx_mid,), (gs['norm_mlp_g'],) = _rms_bwd("norm_mlp_bwd", [sv['x_mid']], [p['norm_mlp_g']], dh2, resid=dx)
    dmix, = _mm("mm_out_dx", dx_mid, w['w_out'], 'nt', [F32])
    gb['w_out'], = _mm("mm_out_dw", sv['mix'], dx_mid, 'tn', [BF16])
    (dattn, dssm), (gs['attn_out_g'], gs['ssm_out_g']) = _rms_bwd(
        "norm_heads_bwd", [sv['attn'], sv['ssm']], [p['attn_out_g'], p['ssm_out_g']], dmix)
    dz, gs['b_glu'] = _glu_dz(dssm, sv['gl'], sv['z'])
    dy, = _mm("mm_glu_dx", dz, w['w_glu'], 'nt', [F32], extras=[('tile', dssm), ('tile', sv['z']), ('tile', sv['y'])],
              epi=lambda acc, ds, z, y: ((acc + ds * jax.nn.sigmoid(z)) * _gelu_grad(y),))
    gb['w_glu'], = _mm("mm_glu_dw", sv['gl'], dz, 'tn', [BF16])
    dxr, dxi = _blockproj("ssm_dstate", dy, 0, *s['wcT'])
    gxr, gxi, s1, s2 = _scan("ssm_scan_bwd", dxr, dxi, s['tr'], True, states=(sv['xr'], sv['xi']))
    du, gs['ssm_d'] = _ssm_du(gxr, gxi, *s['wbT'], dy, sv['proj'], u_off, p['ssm_d'])
    dwb_r, dwb_i = _blockproj_grad("ssm_dbbar", sv['proj'], u_off, gxr, gxi)
    dwc_r, dwc_i = _ssm_dc(sv['xr'], sv['xi'], dy, s['wc'][0].shape[1])
    gs['ssm_c_re'] = _slab_w_grad(dwc_r).transpose(0, 2, 1)
    gs['ssm_c_im'] = -_slab_w_grad(dwc_i).transpose(0, 2, 1)
    G = p['ssm_lambda_re'].shape[0]
    flat = lambda a_: a_.reshape(-1, LANES)
    dbr, dbi, qr, qi = _ssm_param_bwd_flat(*s['flat3'], flat(p['ssm_b_re']), flat(p['ssm_b_im']),
                                           flat(_tile_w_grad(dwb_r)), flat(_tile_w_grad(dwb_i)))
    gs['ssm_b_re'], gs['ssm_b_im'] = dbr.reshape(G, STATE, GROUP_CH), dbi.reshape(G, STATE, GROUP_CH)
    pick = lambda q: q[:, ::GROUP_CH].reshape(G, STATE)
    dlr, dli, dls = _ssm_param_bwd(p['ssm_lambda_re'], p['ssm_lambda_im'], p['ssm_log_step'][:, None], pick(qr), pick(qi),
                                   s1.reshape(8, G, STATE), s2.reshape(8, G, STATE))
    gs['ssm_lambda_re'], gs['ssm_lambda_im'], gs['ssm_log_step'] = dlr, dli, dls[:, 0]
    dq, dkn, dv, gs['q_norm_g'], gs['attn_sinks'] = _attn_bwd(sv['proj'], sv['attn'], dattn, p['q_norm_g'], p['k_norm_g'],
                                                               p['attn_sinks'], attn_w, kv_w)
    dk, gs['k_norm_g'] = _knorm_bwd(sv['proj'], dkn, p['k_norm_g'], attn_w, kv_w)
    dproj = jnp.concatenate([dq.astype(BF16), dk.astype(BF16), dv.astype(BF16), du.astype(BF16)], axis=1)
    dh, = _mm("mm_in_dx", dproj, w['w_in'], 'nt', [F32])
    gb['w_in'], = _mm("mm_in_dw", sv['h'], dproj, 'tn', [BF16])
    (dx_in,), (gs['norm_mix_g'],) = _rms_bwd("norm_mix_bwd", [sv['x']], [p['norm_mix_g']], dh, resid=dx_mid)
    return dx_in, gb, gs


PACK_COLS = 1024


def _pack(arrs, rows):
    flat = jnp.concatenate([a.reshape(-1).astype(F32) for a in arrs])
    return jnp.pad(flat, (0, rows * PACK_COLS - flat.shape[0])).reshape(rows, PACK_COLS)


def _unpack(packed, shapes):
    flat = packed.reshape(-1)
    out, off = [], 0
    for s in shapes:
        n = int(np.prod(s))
        out.append(flat[off:off + n].reshape(s))
        off += n
    return out


def _pack_rows(shapes, multiple):
    n = sum(int(np.prod(s)) for s in shapes)
    rows = -(-n // PACK_COLS)
    return -(-rows // multiple) * multiple


def kernel(x, meta_tokens, norm_mix_g, w_in, q_norm_g, k_norm_g, attn_sinks, ssm_lambda_re, ssm_lambda_im, ssm_log_step, ssm_b_re, ssm_b_im, ssm_c_re, ssm_c_im, ssm_d, w_glu, b_glu, attn_out_g, ssm_out_g, w_out, norm_mlp_g, w_up, w_down, loss_target, m_meta_tokens, m_norm_mix_g, m_w_in, m_q_norm_g, m_k_norm_g, m_attn_sinks, m_ssm_lambda_re, m_ssm_lambda_im, m_ssm_log_step, m_ssm_b_re, m_ssm_b_im, m_ssm_c_re, m_ssm_c_im, m_ssm_d, m_w_glu, m_b_glu, m_attn_out_g, m_ssm_out_g, m_w_out, m_norm_mlp_g, m_w_up, m_w_down, v_meta_tokens, v_norm_mix_g, v_w_in, v_q_norm_g, v_k_norm_g, v_attn_sinks, v_ssm_lambda_re, v_ssm_lambda_im, v_ssm_log_step, v_ssm_b_re, v_ssm_b_im, v_ssm_c_re, v_ssm_c_im, v_ssm_d, v_w_glu, v_b_glu, v_attn_out_g, v_ssm_out_g, v_w_out, v_norm_mlp_g, v_w_up, v_w_down):
    args = (meta_tokens, norm_mix_g, w_in, q_norm_g, k_norm_g, attn_sinks, ssm_lambda_re, ssm_lambda_im, ssm_log_step, ssm_b_re, ssm_b_im, ssm_c_re, ssm_c_im, ssm_d, w_glu, b_glu, attn_out_g, ssm_out_g, w_out, norm_mlp_g, w_up, w_down)
    ms = (m_meta_tokens, m_norm_mix_g, m_w_in, m_q_norm_g, m_k_norm_g, m_attn_sinks, m_ssm_lambda_re, m_ssm_lambda_im, m_ssm_log_step, m_ssm_b_re, m_ssm_b_im, m_ssm_c_re, m_ssm_c_im, m_ssm_d, m_w_glu, m_b_glu, m_attn_out_g, m_ssm_out_g, m_w_out, m_norm_mlp_g, m_w_up, m_w_down)
    vs = (v_meta_tokens, v_norm_mix_g, v_w_in, v_q_norm_g, v_k_norm_g, v_attn_sinks, v_ssm_lambda_re, v_ssm_lambda_im, v_ssm_log_step, v_ssm_b_re, v_ssm_b_im, v_ssm_c_re, v_ssm_c_im, v_ssm_d, v_w_glu, v_b_glu, v_attn_out_g, v_ssm_out_g, v_w_out, v_norm_mlp_g, v_w_up, v_w_down)
    W = dict(zip(WEIGHTS, args))
    M = dict(zip(WEIGHTS, ms))
    V = dict(zip(WEIGHTS, vs))
    depth = norm_mix_g.shape[0]
    seq, D = x.shape[1], x.shape[2]
    attn_w = D // 2
    kv_w = attn_w // KV_GROUP
    dims = dict(attn_w=attn_w, kv_w=kv_w, u_off=(attn_w + 2 * kv_w) // LANES)
    small_names = [n for n in WEIGHTS if n not in BIG and n != 'meta_tokens']
    chip = 2 * lax.axis_index("x") + lax.axis_index("y")

    meta_full, = _all_gather("ag_meta", [meta_tokens], [True])
    full = []
    for l in range(depth):
        got = _all_gather("ag_weights", [W[n][l].astype(BF16) for n in BIG], [COL_SHARDED[n] for n in BIG])
        full.append(dict(zip(BIG, got)))

    h_res = jnp.concatenate([jnp.zeros((PAD, D), F32), meta_full, x[0]], axis=0)
    layer_p = []
    for l in range(depth):
        p = {n: W[n][l] for n in small_names}
        for n in ('norm_mix_g', 'q_norm_g', 'k_norm_g', 'attn_sinks', 'ssm_d', 'b_glu', 'attn_out_g', 'ssm_out_g', 'norm_mlp_g'):
            p[n] = p[n][None, :]
        layer_p.append(p)
    saved = []
    for l in range(depth):
        h_res, sv = _layer_fwd(h_res, layer_p[l], full[l], dims)
        saved.append(sv)
    loss_local, dx = _loss(h_res, loss_target[0])
    loss = lax.psum(loss_local, ("x", "y", "c"))

    small_grads = [None] * depth
    shard_grads = [None] * depth
    for l in reversed(range(depth)):
        dx, gb, gs = _layer_bwd(dx, saved[l], layer_p[l], full[l], dims)
        saved[l] = None
        small_grads[l] = gs
        shard_grads[l] = dict(zip(BIG, _reduce_scatter("w", [gb[n] for n in BIG], [COL_SHARDED[n] for n in BIG], BF16)))
    grad_x = dx[BLOCK:].reshape(x.shape)

    g_small = {n: jnp.stack([small_grads[l][n].reshape(W[n].shape[1:]) for l in range(depth)]) for n in small_names}
    g_shapes = [(N_META, D)] + [W[n].shape for n in small_names]
    rows = _pack_rows(g_shapes, 8 * 2 * N_CHIPS)
    packed = _pack([dx[PAD:BLOCK]] + [g_small[n] for n in small_names], rows)
    red, = _reduce_scatter("small", [packed], [False], F32)
    red_full, = _all_gather("ag_small", [red], [False])
    g_list = _unpack(red_full, g_shapes)
    g_meta = lax.dynamic_slice_in_dim(g_list[0], chip * meta_tokens.shape[1], meta_tokens.shape[1], axis=1)
    G = dict(zip(small_names, g_list[1:]))
    G['meta_tokens'] = g_meta

    out = {}
    for n in BIG:
        out[n] = _adamw_stacked("adamw_" + n, [shard_grads[l][n] for l in range(depth)], W[n], M[n], V[n])
    names = ['meta_tokens'] + small_names
    shapes = [W[n].shape for n in names]
    prow = _pack_rows(shapes, 8)
    d_p, m_p, v_p = _ew("adamw_small", _adam_math, [_pack([W[n] for n in names], prow), _pack([G[n] for n in names], prow),
                                                    _pack([M[n] for n in names], prow), _pack([V[n] for n in names], prow)], [F32] * 3)
    for n, d_, m_, v_ in zip(names, _unpack(d_p, shapes), _unpack(m_p, shapes), _unpack(v_p, shapes)):
        out[n] = (G[n], d_, m_, v_)
    return (loss, grad_x, *[out[n][0] for n in WEIGHTS], *[out[n][1] for n in WEIGHTS],
            *[out[n][2] for n in WEIGHTS], *[out[n][3] for n in WEIGHTS])
```

```python
import functools
import math

import numpy as np
import jax
import jax.numpy as jnp
from jax import lax
from jax.experimental import pallas as pl
from jax.experimental.pallas import tpu as pltpu

F32 = jnp.float32
BF16 = jnp.bfloat16
MESH = pl.DeviceIdType.MESH

N_META = 16
HEAD_DIM = 64
KV_GROUP = 4
GROUP_CH = 16
STATE = 64
BLOCK = 128
PAD = BLOCK - N_META
NORM_EPS = 1e-6
NEG_INF = -1e30
LANES = 128
V7X_VMEM_LIMIT_BYTES = 56 * 1024 * 1024

ADAM_LR, ADAM_B1, ADAM_B2, ADAM_EPS, ADAM_WD, ADAM_STEP = 0.001, 0.9, 0.999, 1e-08, 0.01, 10

WEIGHTS = ['meta_tokens', 'norm_mix_g', 'w_in', 'q_norm_g', 'k_norm_g', 'attn_sinks', 'ssm_lambda_re',
           'ssm_lambda_im', 'ssm_log_step', 'ssm_b_re', 'ssm_b_im', 'ssm_c_re', 'ssm_c_im', 'ssm_d', 'w_glu',
           'b_glu', 'attn_out_g', 'ssm_out_g', 'w_out', 'norm_mlp_g', 'w_up', 'w_down']
BIG = ['w_in', 'w_glu', 'w_out', 'w_up', 'w_down']
COL_SHARDED = {'w_in': True, 'w_glu': False, 'w_out': False, 'w_up': True, 'w_down': False}
N_CHIPS = 4


def _pick(n, cands):
    for c in cands:
        if c <= n and n % c == 0:
            return c
    return n


def _params(sem):
    return pltpu.CompilerParams(dimension_semantics=sem, vmem_limit_bytes=V7X_VMEM_LIMIT_BYTES)


def _pcall(body, **kw):
    return pl.pallas_call(body, **kw)


def _sds(shape, dtype):
    return jax.ShapeDtypeStruct(shape, dtype)


_DIMS = {'nn': ((1,), (0,)), 'nt': ((1,), (1,)), 'tn': ((0,), (0,))}


def _mm(name, a, b, mode, out_dtypes, extras=(), epi=None):
    if mode == 'nn':
        (M, K), (_, N) = a.shape, b.shape
    elif mode == 'nt':
        (M, K), (N, _) = a.shape, b.shape
    else:
        (K, M), (_, N) = a.shape, b.shape
    if mode == 'tn':
        tm, tn, tk = _pick(M, (1024, 512, 256, 128)), _pick(N, (512, 640, 256, 128)), _pick(K, (1408, 704, 384, 128))
    else:
        tk_cands = (2048, 1024, 512, 256, 128) if a.dtype == BF16 else (1024, 512, 256, 128)
        tm, tn, tk = _pick(M, (1408, 704, 384, 128)), _pick(N, (512, 640, 256, 128)), _pick(K, tk_cands)
    nk = K // tk
    a_spec = pl.BlockSpec((tk, tm), lambda i, j, k: (k, i)) if mode == 'tn' else pl.BlockSpec((tm, tk), lambda i, j, k: (i, k))
    b_spec = pl.BlockSpec((tn, tk), lambda i, j, k: (j, k)) if mode == 'nt' else pl.BlockSpec((tk, tn), lambda i, j, k: (k, j))
    ex_specs = [pl.BlockSpec((tm, tn), lambda i, j, k: (i, j)) if kind == 'tile' else pl.BlockSpec((1, tn), lambda i, j, k: (0, j))
                for kind, _ in extras]
    ne, no = len(extras), len(out_dtypes)
    dims = (_DIMS[mode], ((), ()))

    def body(a_ref, b_ref, *rest):
        ex, outs, acc = rest[:ne], rest[ne:ne + no], rest[ne + no]
        k = pl.program_id(2)

        @pl.when(k == 0)
        def _():
            acc[...] = jnp.zeros_like(acc)

        acc[...] += lax.dot_general(a_ref[...].astype(BF16), b_ref[...].astype(BF16), dims, preferred_element_type=F32)

        @pl.when(k == nk - 1)
        def _():
            r = acc[...]
            res = epi(r, *[e[...] for e in ex]) if epi is not None else (r,)
            for o, v in zip(outs, res):
                o[...] = v.astype(o.dtype)

    outs = _pcall(
        body, name=name, grid=(M // tm, N // tn, nk),
        in_specs=[a_spec, b_spec] + ex_specs,
        out_specs=[pl.BlockSpec((tm, tn), lambda i, j, k: (i, j)) for _ in out_dtypes],
        out_shape=[_sds((M, N), d) for d in out_dtypes],
        scratch_shapes=[pltpu.VMEM((tm, tn), F32)],
        compiler_params=_params(("parallel", "parallel", "arbitrary")),
    )(a, b, *[e for _, e in extras])
    return outs


def _ew(name, fn, ins, out_dtypes):
    R, C = ins[0].shape
    tr = _pick(R, tuple(t for t in (1024, 512, 256, 128, 64, 32, 16, 8) if t * C <= 512 * 1024) or (8,))
    n_in = len(ins)

    def body(*refs):
        res = fn(*[r[...] for r in refs[:n_in]])
        for o, v in zip(refs[n_in:], res):
            o[...] = v.astype(o.dtype)

    spec = pl.BlockSpec((tr, C), lambda i: (i, 0))
    return _pcall(body, name=name, grid=(R // tr,), in_specs=[spec] * n_in, out_specs=[spec] * len(out_dtypes),
                  out_shape=[_sds((R, C), d) for d in out_dtypes], compiler_params=_params(("parallel",)))(*ins)


def _adam_math(w, g, m, v):
    m = ADAM_B1 * m + (1.0 - ADAM_B1) * g
    v = ADAM_B2 * v + (1.0 - ADAM_B2) * (g * g)
    m_hat = m / (1.0 - ADAM_B1 ** ADAM_STEP)
    v_hat = v / (1.0 - ADAM_B2 ** ADAM_STEP)
    delta = -ADAM_LR * (m_hat / (jnp.sqrt(v_hat) + ADAM_EPS) + ADAM_WD * w)
    return delta, m, v


def _rms_fwd(name, xs, gs, out_dtype):
    L = xs[0].shape[0]
    ws = [x.shape[1] for x in xs]
    n = len(xs)
    tr = _pick(L, (384, 256, 128))

    def body(*refs):
        o = refs[2 * n]
        off = 0
        for i in range(n):
            x = refs[i][...]
            r = lax.rsqrt(jnp.mean(x * x, axis=-1, keepdims=True) + NORM_EPS)
            o[:, off:off + ws[i]] = ((x * r) * refs[n + i][...]).astype(o.dtype)
            off += ws[i]

    return _pcall(body, name=name, grid=(L // tr,),
                  in_specs=[pl.BlockSpec((tr, w), lambda i: (i, 0)) for w in ws] + [pl.BlockSpec((1, w), lambda i: (0, 0)) for w in ws],
                  out_specs=pl.BlockSpec((tr, sum(ws)), lambda i: (i, 0)), out_shape=_sds((L, sum(ws)), out_dtype),
                  compiler_params=_params(("parallel",)))(*xs, *gs)


def _rms_bwd(name, xs, gs, dy, resid=None):
    L = xs[0].shape[0]
    ws = [x.shape[1] for x in xs]
    n = len(xs)
    tr = _pick(L, (384, 256, 128))
    has_res = resid is not None

    def body(*refs):
        x_refs, g_refs, dy_ref = refs[:n], refs[n:2 * n], refs[2 * n]
        p = 2 * n + 1
        res_ref = refs[p] if has_res else None
        p += 1 if has_res else 0
        dx_refs, dg_refs = refs[p:p + n], refs[p + n:p + 2 * n]
        first = pl.program_id(0) == 0
        off = 0
        for i in range(n):
            x = x_refs[i][...]
            d = dy_ref[:, off:off + ws[i]]
            r = lax.rsqrt(jnp.mean(x * x, axis=-1, keepdims=True) + NORM_EPS)
            xh = x * r
            dg = jnp.sum(d * xh, axis=0, keepdims=True)

            @pl.when(first)
            def _(i=i):
                dg_refs[i][...] = jnp.zeros_like(dg_refs[i])

            dg_refs[i][...] += dg
            dyg = d * g_refs[i][...]
            dx = r * (dyg - xh * jnp.mean(dyg * xh, axis=-1, keepdims=True))
            if has_res:
                dx = dx + res_ref[...]
            dx_refs[i][...] = dx
            off += ws[i]

    in_specs = ([pl.BlockSpec((tr, w), lambda i: (i, 0)) for w in ws] + [pl.BlockSpec((1, w), lambda i: (0, 0)) for w in ws]
                + [pl.BlockSpec((tr, sum(ws)), lambda i: (i, 0))])
    ins = list(xs) + list(gs) + [dy]
    if has_res:
        in_specs.append(pl.BlockSpec((tr, ws[0]), lambda i: (i, 0)))
        ins.append(resid)
    outs = _pcall(body, name=name, grid=(L // tr,), in_specs=in_specs,
                  out_specs=[pl.BlockSpec((tr, w), lambda i: (i, 0)) for w in ws] + [pl.BlockSpec((1, w), lambda i: (0, 0)) for w in ws],
                  out_shape=[_sds((L, w), F32) for w in ws] + [_sds((1, w), F32) for w in ws],
                  compiler_params=_params(("arbitrary",)))(*ins)
    return outs[:n], outs[n:]


def _loss(xl, target):
    Lp, D = xl.shape

    def body(x_ref, t_ref, dy_ref, loss_ref):
        n = pl.program_id(0)

        @pl.when(n == 0)
        def _():
            loss_ref[...] = jnp.zeros_like(loss_ref)
            dy_ref[...] = jnp.zeros_like(dy_ref)

        @pl.when(n > 0)
        def _():
            err = x_ref[...] - t_ref[...]
            dy_ref[...] = err * (1.0 / D)
            loss_ref[...] += jnp.sum(err * err) * (0.5 / D)

    dy, loss = _pcall(body, name="loss_head", grid=(Lp // BLOCK,),
                      in_specs=[pl.BlockSpec((BLOCK, D), lambda n: (n, 0)), pl.BlockSpec((BLOCK, D), lambda n: (jnp.maximum(n - 1, 0), 0))],
                      out_specs=[pl.BlockSpec((BLOCK, D), lambda n: (n, 0)), pl.BlockSpec((8, LANES), lambda n: (0, 0))],
                      out_shape=[_sds((Lp, D), F32), _sds((8, LANES), F32)],
                      compiler_params=_params(("arbitrary",)))(xl, target)
    return loss[0, 0], dy


def _attn_mask_dist(n):
    i = lax.broadcasted_iota(jnp.int32, (BLOCK, 3 * BLOCK), 0)
    j = lax.broadcasted_iota(jnp.int32, (BLOCK, 3 * BLOCK), 1)
    in_band = j < 2 * BLOCK
    band = in_band & (j > i) & (j <= i + BLOCK) & (j >= 2 * BLOCK - BLOCK * n)
    jm = j - 2 * BLOCK
    meta = (~in_band) & (jm >= PAD) & (jm <= BLOCK * n + i)
    dist = jnp.where(in_band, BLOCK + i - j, BLOCK * n + i - jm).astype(F32)
    return band | meta, dist


def _head_norm(x, g):
    r = lax.rsqrt(jnp.mean(x * x, axis=-1, keepdims=True) + NORM_EPS)
    return (x * r) * g, r


def _attn_specs(attn_w, kv_w):
    kb = attn_w // kv_w
    q_spec = pl.BlockSpec((BLOCK, attn_w), lambda n: (n, 0))

    def kv(col):
        return [pl.BlockSpec((BLOCK, kv_w), lambda n: (jnp.maximum(n - 1, 0), col)),
                pl.BlockSpec((BLOCK, kv_w), lambda n: (n, col)),
                pl.BlockSpec((BLOCK, kv_w), lambda n: (0, col))]

    return q_spec, kv(kb), kv(kb + 1)


def _slopes(n_heads):
    return [2.0 ** (-8.0 * (h + 1) / n_heads) for h in range(n_heads)]


def _scores(qn, kn, slope, sink, mask, dist):
    s = lax.dot_general(qn.astype(BF16), kn.astype(BF16), (((1,), (1,)), ((), ())), preferred_element_type=F32)
    s = s * (1.0 / math.sqrt(HEAD_DIM)) - slope * dist
    s = jnp.where(mask, s, NEG_INF)
    m = jnp.maximum(jnp.max(s, axis=-1, keepdims=True), sink)
    p = jnp.exp(s - m)
    ps = jnp.exp(sink - m)
    inv = 1.0 / (jnp.sum(p, axis=-1, keepdims=True) + ps)
    return p * inv, ps * inv


def _attn_fwd(proj, gq, gk, sinks, attn_w, kv_w):
    Lp = proj.shape[0]
    n_heads, n_kv = attn_w // HEAD_DIM, kv_w // HEAD_DIM
    slopes = _slopes(n_heads)
    q_spec, k_specs, v_specs = _attn_specs(attn_w, kv_w)

    def body(q_ref, kp, kc, km, vp, vc, vm, gq_ref, gk_ref, sk_ref, o_ref):
        mask, dist = _attn_mask_dist(pl.program_id(0))
        for kh in range(n_kv):
            cs = slice(kh * HEAD_DIM, (kh + 1) * HEAD_DIM)
            kn, _ = _head_norm(jnp.concatenate([kp[:, cs], kc[:, cs], km[:, cs]], axis=0), gk_ref[...])
            vcat = jnp.concatenate([vp[:, cs], vc[:, cs], vm[:, cs]], axis=0).astype(BF16)
            for g in range(KV_GROUP):
                h = kh * KV_GROUP + g
                hs = slice(h * HEAD_DIM, (h + 1) * HEAD_DIM)
                qn, _ = _head_norm(q_ref[:, hs], gq_ref[...])
                p, _ = _scores(qn, kn, slopes[h], sk_ref[0:1, h:h + 1], mask, dist)
                o_ref[:, hs] = jnp.dot(p.astype(BF16), vcat, preferred_element_type=F32)

    small = lambda w: pl.BlockSpec((1, w), lambda n: (0, 0))
    return _pcall(body, name="attn_fwd", grid=(Lp // BLOCK,),
                  in_specs=[q_spec] + k_specs + v_specs + [small(HEAD_DIM), small(HEAD_DIM), small(n_heads)],
                  out_specs=pl.BlockSpec((BLOCK, attn_w), lambda n: (n, 0)), out_shape=_sds((Lp, attn_w), F32),
                  compiler_params=_params(("parallel",)))(proj, proj, proj, proj, proj, proj, proj, gq, gk, sinks)


def _attn_bwd(proj, attn, dattn, gq, gk, sinks, attn_w, kv_w):
    Lp = proj.shape[0]
    n_heads, n_kv = attn_w // HEAD_DIM, kv_w // HEAD_DIM
    slopes = _slopes(n_heads)
    q_spec, k_specs, v_specs = _attn_specs(attn_w, kv_w)
    scale = 1.0 / math.sqrt(HEAD_DIM)
    tn_dims = (((0,), (0,)), ((), ()))

    def body(q_ref, kp, kc, km, vp, vc, vm, o_ref, do_ref, gq_ref, gk_ref, sk_ref, dq_ref, dk_ref, dv_ref, dgq_ref, dsk_ref):
        n = pl.program_id(0)

        @pl.when(n == 0)
        def _():
            dk_ref[...] = jnp.zeros_like(dk_ref)
            dv_ref[...] = jnp.zeros_like(dv_ref)
            dgq_ref[...] = jnp.zeros_like(dgq_ref)
            dsk_ref[...] = jnp.zeros_like(dsk_ref)

        mask, dist = _attn_mask_dist(n)
        lane = lax.broadcasted_iota(jnp.int32, (1, n_heads), 1)
        rows_prev = pl.ds(pl.multiple_of(jnp.maximum(n - 1, 0) * BLOCK, BLOCK), BLOCK)
        rows_cur = pl.ds(pl.multiple_of(n * BLOCK, BLOCK), BLOCK)
        rows_meta = pl.ds(0, BLOCK)
        dgq = jnp.zeros((1, HEAD_DIM), F32)
        dsk = jnp.zeros((1, n_heads), F32)
        for kh in range(n_kv):
            cs = slice(kh * HEAD_DIM, (kh + 1) * HEAD_DIM)
            kn, _ = _head_norm(jnp.concatenate([kp[:, cs], kc[:, cs], km[:, cs]], axis=0), gk_ref[...])
            kn16 = kn.astype(BF16)
            vcat = jnp.concatenate([vp[:, cs], vc[:, cs], vm[:, cs]], axis=0).astype(BF16)
            dkn = jnp.zeros((3 * BLOCK, HEAD_DIM), F32)
            dvc = jnp.zeros((3 * BLOCK, HEAD_DIM), F32)
            for g in range(KV_GROUP):
                h = kh * KV_GROUP + g
                hs = slice(h * HEAD_DIM, (h + 1) * HEAD_DIM)
                q = q_ref[:, hs]
                qn, rq = _head_norm(q, gq_ref[...])
                p, ps = _scores(qn, kn, slopes[h], sk_ref[0:1, h:h + 1], mask, dist)
                do = do_ref[:, hs]
                dd = jnp.sum(do * o_ref[:, hs], axis=-1, keepdims=True)
                do16 = do.astype(BF16)
                dp = lax.dot_general(do16, vcat, (((1,), (1,)), ((), ())), preferred_element_type=F32)
                ds16 = (p * (dp - dd)).astype(BF16)
                dsk = dsk + jnp.where(lane == h, jnp.sum(-ps * dd), 0.0)
                dqn = jnp.dot(ds16, kn16, preferred_element_type=F32) * scale
                dkn = dkn + lax.dot_general(ds16, qn.astype(BF16), tn_dims, preferred_element_type=F32) * scale
                dvc = dvc + lax.dot_general(p.astype(BF16), do16, tn_dims, preferred_element_type=F32)
                xh = q * rq
                dgq = dgq + jnp.sum(dqn * xh, axis=0, keepdims=True)
                dyg = dqn * gq_ref[...]
                dq_ref[:, hs] = rq * (dyg - xh * jnp.mean(dyg * xh, axis=-1, keepdims=True))
            for part, rows in enumerate((rows_prev, rows_cur, rows_meta)):
                ps_ = slice(part * BLOCK, (part + 1) * BLOCK)
                dk_ref[rows, cs] += dkn[ps_]
                dv_ref[rows, cs] += dvc[ps_]
        dgq_ref[...] += dgq
        dsk_ref[...] += dsk

    small = lambda w: pl.BlockSpec((1, w), lambda n: (0, 0))
    blk = pl.BlockSpec((BLOCK, attn_w), lambda n: (n, 0))
    whole = pl.BlockSpec((Lp, kv_w), lambda n: (0, 0))
    return _pcall(body, name="attn_bwd", grid=(Lp // BLOCK,),
                  in_specs=[q_spec] + k_specs + v_specs + [blk, blk, small(HEAD_DIM), small(HEAD_DIM), small(n_heads)],
                  out_specs=[blk, whole, whole, small(HEAD_DIM), small(n_heads)],
                  out_shape=[_sds((Lp, attn_w), F32), _sds((Lp, kv_w), F32), _sds((Lp, kv_w), F32),
                             _sds((1, HEAD_DIM), F32), _sds((1, n_heads), F32)],
                  compiler_params=_params(("arbitrary",)))(proj, proj, proj, proj, proj, proj, proj, attn, dattn, gq, gk, sinks)


def _knorm_bwd(proj, dkn, gk, attn_w, kv_w):
    Lp = proj.shape[0]
    n_kv = kv_w // HEAD_DIM
    tr = _pick(Lp, (384, 256, 128))

    def body(k_ref, d_ref, g_ref, dk_ref, dg_ref):
        @pl.when(pl.program_id(0) == 0)
        def _():
            dg_ref[...] = jnp.zeros_like(dg_ref)

        dg = jnp.zeros((1, HEAD_DIM), F32)
        for kh in range(n_kv):
            cs = slice(kh * HEAD_DIM, (kh + 1) * HEAD_DIM)
            x = k_ref[:, cs]
            d = d_ref[:, cs]
            r = lax.rsqrt(jnp.mean(x * x, axis=-1, keepdims=True) + NORM_EPS)
            xh = x * r
            dg = dg + jnp.sum(d * xh, axis=0, keepdims=True)
            dyg = d * g_ref[...]
            dk_ref[:, cs] = r * (dyg - xh * jnp.mean(dyg * xh, axis=-1, keepdims=True))
        dg_ref[...] += dg

    return _pcall(body, name="knorm_bwd", grid=(Lp // tr,),
                  in_specs=[pl.BlockSpec((tr, kv_w), lambda i: (i, attn_w // kv_w)), pl.BlockSpec((tr, kv_w), lambda i: (i, 0)),
                            pl.BlockSpec((1, HEAD_DIM), lambda i: (0, 0))],
                  out_specs=[pl.BlockSpec((tr, kv_w), lambda i: (i, 0)), pl.BlockSpec((1, HEAD_DIM), lambda i: (0, 0))],
                  out_shape=[_sds((Lp, kv_w), F32), _sds((1, HEAD_DIM), F32)],
                  compiler_params=_params(("arbitrary",)))(proj, dkn, gk)


def _ssm_bbar(lr, li, ls, br, bi):
    def fn(lr, li, ls, br, bi):
        fr, fi = _zoh_factor(lr, li, ls)
        return fr * br - fi * bi, fr * bi + fi * br

    return _ew("ssm_bbar", fn, [lr, li, ls, br, bi], [F32, F32])


def _lam_bar(lr, li, ls):
    dl = jnp.exp(ls)
    e = jnp.exp(lr * dl)
    return e * jnp.cos(li * dl), e * jnp.sin(li * dl), dl


def _zoh_factor(lr, li, ls):
    ar, ai, _ = _lam_bar(lr, li, ls)
    n2 = lr * lr + li * li
    ivr, ivi = lr / n2, -li / n2
    return (ar - 1.0) * ivr - ai * ivi, (ar - 1.0) * ivi + ai * ivr


SCAN_SHIFTS = (1, 2, 4)


def _ssm_tables(lr, li, ls):
    Wx = lr.shape[1]

    def body(lr_ref, li_ref, ls_ref, tf_ref, tr_ref):
        dl = jnp.exp(ls_ref[...])
        zr, zi = lr_ref[...] * dl, li_ref[...] * dl
        row = lax.broadcasted_iota(jnp.int32, (8, Wx), 0)

        def power(kf):
            e = jnp.exp(kf * zr)
            return e * jnp.cos(kf * zi), e * jnp.sin(kf * zi)

        for ref, rev in ((tf_ref, False), (tr_ref, True)):
            sgn = -1.0 if rev else 1.0
            for k, d in enumerate(SCAN_SHIFTS):
                ar, ai = power(jnp.full((8, Wx), float(d), F32))
                keep = (row < 8 - d) if rev else (row >= d)
                ref[k] = jnp.where(keep, ar, 0.0)
                ref[4 + k] = jnp.where(keep, sgn * ai, 0.0)
            pr, pi = power(((8 - row) if rev else (row + 1)).astype(F32))
            ref[3] = pr
            ref[7] = sgn * pi

    full = pl.BlockSpec((1, Wx), lambda: (0, 0))
    tab = pl.BlockSpec((8, 8, Wx), lambda: (0, 0, 0))
    return _pcall(body, name="ssm_tables", in_specs=[full] * 3, out_specs=[tab, tab],
                  out_shape=[_sds((8, 8, Wx), F32)] * 2,
                  compiler_params=pltpu.CompilerParams(vmem_limit_bytes=V7X_VMEM_LIMIT_BYTES))(lr, li, ls)


def _scan(name, br, bi, tab, reverse, states=None):
    L, Wx = br.shape
    TB = _pick(L, (384, 256, 128))
    CW = _pick(Wx, (512, 256, 128))
    nT, nG = L // TB, TB // 8

    def body(*refs):
        if reverse:
            br_ref, bi_ref, xr_ref, xi_ref, tab_ref, or_ref, oi_ref, s1_ref, s2_ref, cr_ref, ci_ref = refs
        else:
            br_ref, bi_ref, tab_ref, or_ref, oi_ref, cr_ref, ci_ref = refs

        @pl.when(pl.program_id(1) == 0)
        def _():
            cr_ref[...] = jnp.zeros_like(cr_ref)
            ci_ref[...] = jnp.zeros_like(ci_ref)
            if reverse:
                s1_ref[...] = jnp.zeros_like(s1_ref)
                s2_ref[...] = jnp.zeros_like(s2_ref)

        def step(q, carry):
            cr, ci = carry[0], carry[1]
            g = (nG - 1 - q) if reverse else q
            rows = pl.ds(pl.multiple_of(g * 8, 8), 8)
            b_r, b_i = br_ref[rows, :], bi_ref[rows, :]
            sr, si = b_r, b_i
            for k, d in enumerate(SCAN_SHIFTS):
                mr, mi = tab_ref[k], tab_ref[4 + k]
                sh = (8 - d) if reverse else d
                pr, pi = pltpu.roll(sr, sh, 0), pltpu.roll(si, sh, 0)
                sr, si = sr + mr * pr - mi * pi, si + mr * pi + mi * pr
            pwr, pwi = tab_ref[3], tab_ref[7]
            xr = sr + pwr * cr - pwi * ci
            xi = si + pwr * ci + pwi * cr
            or_ref[rows, :] = xr
            oi_ref[rows, :] = xi
            row = 0 if reverse else 7
            out = (jnp.broadcast_to(xr[row:row + 1, :], xr.shape), jnp.broadcast_to(xi[row:row + 1, :], xi.shape))
            if reverse:
                hr, hi = xr - b_r, xi - b_i
                st_r, st_i = xr_ref[rows, :], xi_ref[rows, :]
                out = out + (carry[2] + hr * st_r + hi * st_i, carry[3] + hi * st_r - hr * st_i)
            return out

        init = (cr_ref[...], ci_ref[...])
        if reverse:
            init = init + (jnp.zeros((8, CW), F32), jnp.zeros((8, CW), F32))
        fin = lax.fori_loop(0, nG, step, init)
        cr_ref[...] = fin[0]
        ci_ref[...] = fin[1]
        if reverse:
            s1_ref[...] += fin[2]
            s2_ref[...] += fin[3]

    tmap = (lambda j, t: (nT - 1 - t, j)) if reverse else (lambda j, t: (t, j))
    blk = pl.BlockSpec((TB, CW), tmap)
    tab_spec = pl.BlockSpec((8, 8, CW), lambda j, t: (0, 0, j))
    sum_spec = pl.BlockSpec((8, CW), lambda j, t: (0, j))
    ins = [br, bi] + (list(states) if reverse else []) + [tab]
    in_specs = [blk, blk] + ([blk, blk] if reverse else []) + [tab_spec]
    out_specs = [blk, blk] + ([sum_spec, sum_spec] if reverse else [])
    out_shape = [_sds((L, Wx), F32)] * 2 + ([_sds((8, Wx), F32)] * 2 if reverse else [])
    return _pcall(body, name=name, grid=(Wx // CW, nT), in_specs=in_specs, out_specs=out_specs, out_shape=out_shape,
                  scratch_shapes=[pltpu.VMEM((8, CW), F32), pltpu.VMEM((8, CW), F32)],
                  compiler_params=_params(("parallel", "arbitrary")))(*ins)


def _row_tile(L):
    return _pick(L, (1408, 704, 384, 128))


def _blockproj(name, src, off, w_r, w_i):
    L = src.shape[0]
    T = w_r.shape[0]
    tm = _row_tile(L)

    def body(s_ref, wr_ref, wi_ref, or_ref, oi_ref):
        s = s_ref[...].astype(BF16)
        or_ref[...] = jnp.dot(s, wr_ref[...], preferred_element_type=F32)
        oi_ref[...] = jnp.dot(s, wi_ref[...], preferred_element_type=F32)

    w_spec = pl.BlockSpec((None, LANES, LANES), lambda i, t: (t, 0, 0))
    o_spec = pl.BlockSpec((tm, LANES), lambda i, t: (i, t))
    return _pcall(body, name=name, grid=(L // tm, T),
                  in_specs=[pl.BlockSpec((tm, LANES), lambda i, t: (i, off + t // 4)), w_spec, w_spec],
                  out_specs=[o_spec, o_spec], out_shape=[_sds((L, T * LANES), F32)] * 2,
                  compiler_params=_params(("parallel", "arbitrary")))(src, w_r, w_i)


def _blockproj_grad(name, src, off, gr, gi):
    L = src.shape[0]
    T = gr.shape[1] // LANES
    tm = _row_tile(L)
    tn_dims = (((0,), (0,)), ((), ()))

    def body(s_ref, gr_ref, gi_ref, or_ref, oi_ref):
        @pl.when(pl.program_id(1) == 0)
        def _():
            or_ref[...] = jnp.zeros_like(or_ref)
            oi_ref[...] = jnp.zeros_like(oi_ref)

        s = s_ref[...].astype(BF16)
        or_ref[...] += lax.dot_general(s, gr_ref[...].astype(BF16), tn_dims, preferred_element_type=F32)
        oi_ref[...] += lax.dot_general(s, gi_ref[...].astype(BF16), tn_dims, preferred_element_type=F32)

    g_spec = pl.BlockSpec((tm, LANES), lambda t, i: (i, t))
    o_spec = pl.BlockSpec((None, LANES, LANES), lambda t, i: (t, 0, 0))
    return _pcall(body, name=name, grid=(T, L // tm),
                  in_specs=[pl.BlockSpec((tm, LANES), lambda t, i: (i, off + t // 4)), g_spec, g_spec],
                  out_specs=[o_spec, o_spec], out_shape=[_sds((T, LANES, LANES), F32)] * 2,
                  compiler_params=_params(("parallel", "arbitrary")))(src, gr, gi)


def _gelu(y):
    k = math.sqrt(2.0 / math.pi)
    return 0.5 * y * (1.0 + jnp.tanh(k * (y + 0.044715 * (y * y * y))))


def _gelu_grad(y):
    k = math.sqrt(2.0 / math.pi)
    t = jnp.tanh(k * (y + 0.044715 * (y * y * y)))
    return 0.5 * (1.0 + t) + 0.5 * y * (1.0 - t * t) * (k * (1.0 + 3 * 0.044715 * (y * y)))


def _ssm_out(xr, xi, w_r, w_i, proj, u_off, dvec):
    L = xr.shape[0]
    J = w_r.shape[0]
    SW = w_r.shape[1]
    tm = _row_tile(L)

    def body(xr_ref, xi_ref, wr_ref, wi_ref, u_ref, d_ref, y_ref, gl_ref):
        acc = jnp.dot(xr_ref[...].astype(BF16), wr_ref[...], preferred_element_type=F32)
        acc += jnp.dot(xi_ref[...].astype(BF16), wi_ref[...], preferred_element_type=F32)
        y = acc + d_ref[...] * u_ref[...]
        y_ref[...] = y
        gl_ref[...] = _gelu(y)

    x_spec = pl.BlockSpec((tm, SW), lambda j, i: (i, j))
    w_spec = pl.BlockSpec((None, SW, LANES), lambda j, i: (j, 0, 0))
    o_spec = pl.BlockSpec((tm, LANES), lambda j, i: (i, j))
    return _pcall(body, name="ssm_out", grid=(J, L // tm),
                  in_specs=[x_spec, x_spec, w_spec, w_spec, pl.BlockSpec((tm, LANES), lambda j, i: (i, u_off + j)),
                            pl.BlockSpec((1, LANES), lambda j, i: (0, j))],
                  out_specs=[o_spec, o_spec], out_shape=[_sds((L, J * LANES), F32)] * 2,
                  compiler_params=_params(("parallel", "parallel")))(xr, xi, w_r, w_i, proj, dvec)


def _ssm_du(gr, gi, w_r, w_i, dy, proj, u_off, dvec):
    L = gr.shape[0]
    J = w_r.shape[0]
    SW = w_r.shape[1]
    tm = _row_tile(L)

    def body(gr_ref, gi_ref, wr_ref, wi_ref, dy_ref, u_ref, d_ref, du_ref, dd_ref):
        i = pl.program_id(1)

        @pl.when(i == 0)
        def _():
            dd_ref[...] = jnp.zeros_like(dd_ref)

        acc = jnp.dot(gr_ref[...].astype(BF16), wr_ref[...], preferred_element_type=F32)
        acc += jnp.dot(gi_ref[...].astype(BF16), wi_ref[...], preferred_element_type=F32)
        dy = dy_ref[...]
        row = lax.broadcasted_iota(jnp.int32, (tm, LANES), 0) + i * tm
        du_ref[...] = jnp.where(row >= PAD, acc + d_ref[...] * dy, 0.0)
        dd_ref[...] += jnp.sum(dy * u_ref[...], axis=0, keepdims=True)

    x_spec = pl.BlockSpec((tm, SW), lambda j, i: (i, j))
    w_spec = pl.BlockSpec((None, SW, LANES), lambda j, i: (j, 0, 0))
    o_spec = pl.BlockSpec((tm, LANES), lambda j, i: (i, j))
    vec = pl.BlockSpec((1, LANES), lambda j, i: (0, j))
    return _pcall(body, name="ssm_du", grid=(J, L // tm),
                  in_specs=[x_spec, x_spec, w_spec, w_spec, o_spec, pl.BlockSpec((tm, LANES), lambda j, i: (i, u_off + j)), vec],
                  out_specs=[o_spec, vec], out_shape=[_sds((L, J * LANES), F32), _sds((1, J * LANES), F32)],
                  compiler_params=_params(("parallel", "arbitrary")))(gr, gi, w_r, w_i, dy, proj, dvec)


def _ssm_dc(xr, xi, dy, SW):
    L = xr.shape[0]
    J = dy.shape[1] // LANES
    tm = _row_tile(L)
    tn_dims = (((0,), (0,)), ((), ()))

    def body(xr_ref, xi_ref, dy_ref, or_ref, oi_ref):
        @pl.when(pl.program_id(1) == 0)
        def _():
            or_ref[...] = jnp.zeros_like(or_ref)
            oi_ref[...] = jnp.zeros_like(oi_ref)

        d = dy_ref[...].astype(BF16)
        or_ref[...] += lax.dot_general(xr_ref[...].astype(BF16), d, tn_dims, preferred_element_type=F32)
        oi_ref[...] += lax.dot_general(xi_ref[...].astype(BF16), d, tn_dims, preferred_element_type=F32)

    x_spec = pl.BlockSpec((tm, SW), lambda j, i: (i, j))
    o_spec = pl.BlockSpec((None, SW, LANES), lambda j, i: (j, 0, 0))
    return _pcall(body, name="ssm_dc", grid=(J, L // tm),
                  in_specs=[x_spec, x_spec, pl.BlockSpec((tm, LANES), lambda j, i: (i, j))],
                  out_specs=[o_spec, o_spec], out_shape=[_sds((J, SW, LANES), F32)] * 2,
                  compiler_params=_params(("parallel", "arbitrary")))(xr, xi, dy)


def _glu_dz(ds, gl, z):
    L, W = ds.shape
    tr = _pick(L, (384, 256, 128))

    def body(ds_ref, gl_ref, z_ref, dz_ref, db_ref):
        @pl.when(pl.program_id(0) == 0)
        def _():
            db_ref[...] = jnp.zeros_like(db_ref)

        sg = jax.nn.sigmoid(z_ref[...])
        dz = ds_ref[...] * gl_ref[...] * (sg * (1.0 - sg))
        dz_ref[...] = dz.astype(BF16)
        db_ref[...] += jnp.sum(dz, axis=0, keepdims=True)

    spec = pl.BlockSpec((tr, W), lambda i: (i, 0))
    vec = pl.BlockSpec((1, W), lambda i: (0, 0))
    return _pcall(body, name="glu_dz", grid=(L // tr,), in_specs=[spec] * 3, out_specs=[spec, vec],
                  out_shape=[_sds((L, W), BF16), _sds((1, W), F32)], compiler_params=_params(("arbitrary",)))(ds, gl, z)


def _ssm_param_bwd_flat(lr, li, ls, br, bi, dbbr, dbbi):
    def seg_sum(x):
        for s in (8, 4, 2, 1):
            x = x + pltpu.roll(x, LANES - s, 1)
        return x

    def fn(lr, li, ls, br, bi, dbbr, dbbi):
        fr, fi = _zoh_factor(lr, li, ls)
        return (fr * dbbr + fi * dbbi, fr * dbbi - fi * dbbr,
                seg_sum(br * dbbr + bi * dbbi), seg_sum(br * dbbi - bi * dbbr))

    return _ew("ssm_param_bwd_flat", fn, [lr, li, ls, br, bi, dbbr, dbbi], [F32] * 4)


def _ssm_param_bwd(lr, li, ls, dfr, dfi, s1, s2):
    G, P = lr.shape

    def body(lr_ref, li_ref, ls_ref, dfr_ref, dfi_ref, s1_ref, s2_ref, dlr_ref, dli_ref, dls_ref):
        lr, li = lr_ref[...], li_ref[...]
        ar, ai, dl = _lam_bar(lr, li, ls_ref[...])
        sr, si = s1_ref[0], s2_ref[0]
        for k in range(1, 8):
            sr = sr + s1_ref[k]
            si = si + s2_ref[k]
        a2 = ar * ar + ai * ai
        gar, gai = (sr * ar - si * ai) / a2, (sr * ai + si * ar) / a2
        n2 = lr * lr + li * li
        ivr, ivi = lr / n2, -li / n2
        fr = (ar - 1.0) * ivr - ai * ivi
        fi = (ar - 1.0) * ivi + ai * ivr
        dfr, dfi = dfr_ref[...], dfi_ref[...]
        gar = gar + ivr * dfr + ivi * dfi
        gai = gai + ivr * dfi - ivi * dfr
        wr, wi = -(fr * ivr - fi * ivi), -(fr * ivi + fi * ivr)
        glr, gli = wr * dfr + wi * dfi, wr * dfi - wi * dfr
        gzr, gzi = ar * gar + ai * gai, ar * gai - ai * gar
        dlr_ref[...] = glr + dl * gzr
        dli_ref[...] = gli + dl * gzi
        dls_ref[...] = dl * jnp.sum(lr * gzr + li * gzi, axis=-1, keepdims=True)

    m = pl.BlockSpec((G, P), lambda: (0, 0))
    v = pl.BlockSpec((G, 1), lambda: (0, 0))
    s = pl.BlockSpec((8, G, P), lambda: (0, 0, 0))
    return _pcall(body, name="ssm_param_bwd", in_specs=[m, m, v, m, m, s, s], out_specs=[m, m, v],
                  out_shape=[_sds((G, P), F32), _sds((G, P), F32), _sds((G, 1), F32)])(lr, li, ls, dfr, dfi, s1, s2)


def _tile_mask(G):
    T = G // 2
    e = np.zeros((T, 8, 1, 2, 1), np.float32)
    for t in range(T):
        for c in range(2):
            e[t, (2 * t + c) % 8, 0, c, 0] = 1.0
    return e


def _tile_w(arr):
    G = arr.shape[0]
    a = arr.reshape(G // 2, 1, 2, STATE, GROUP_CH).transpose(0, 1, 4, 2, 3)
    return (a * _tile_mask(G)).reshape(G // 2, LANES, LANES).astype(BF16)


def _tile_w_grad(dw):
    G = dw.shape[0] * 2
    d = dw.reshape(G // 2, 8, GROUP_CH, 2, STATE) * _tile_mask(G)
    return d.sum(axis=1).transpose(0, 2, 3, 1).reshape(G, STATE, GROUP_CH)


def _slab_w(arr):
    G = arr.shape[0]
    a = arr.reshape(G // 8, 8, STATE, 1, GROUP_CH)
    eye = np.eye(8, dtype=np.float32).reshape(1, 8, 1, 8, 1)
    return (a * eye).reshape(G // 8, 8 * STATE, LANES).astype(BF16)


def _slab_w_grad(dw):
    J = dw.shape[0]
    eye = np.eye(8, dtype=np.float32).reshape(1, 8, 1, 8, 1)
    return (dw.reshape(J, 8, STATE, 8, GROUP_CH) * eye).sum(axis=3).reshape(J * 8, STATE, GROUP_CH)


def _exchange(name, ins, out_sds, remote, local, aliases=None):
    n_in, n_out, n_r, n_l = len(ins), len(out_sds), len(remote), len(local)

    def body(*refs):
        in_refs, out_refs = refs[:n_in], refs[n_in:n_in + n_out]
        send_sems, recv_sems, local_sems = refs[n_in + n_out:]
        x, y, c = lax.axis_index("x"), lax.axis_index("y"), lax.axis_index("c")

        def place(px, py, pc):
            return dict(x=px, y=py, c=pc, chip=2 * px + py)

        def flip(mask):
            mx, my, mc = mask
            return ((1 - x) if mx else x, (1 - y) if my else y, (1 - c) if mc else c)

        me = place(x, y, c)
        sends = []
        for k, (ii, src, oi, dst, mask) in enumerate(remote):
            cp = pltpu.make_async_remote_copy(src_ref=src(in_refs[ii], me), dst_ref=dst(out_refs[oi], me),
                                              send_sem=send_sems.at[k], recv_sem=recv_sems.at[k],
                                              device_id=flip(mask), device_id_type=MESH)
            cp.start()
            sends.append(cp)
        locals_ = []
        for k, (ii, src, oi, dst) in enumerate(local):
            cp = pltpu.make_async_copy(src(in_refs[ii], me), dst(out_refs[oi], me), local_sems.at[k])
            cp.start()
            locals_.append(cp)
        for k, (ii, src, oi, dst, mask) in enumerate(remote):
            sends[k].wait_send()
            peer = flip(mask)
            pltpu.make_async_remote_copy(src_ref=src(in_refs[ii], me), dst_ref=dst(out_refs[oi], place(*peer)),
                                         send_sem=send_sems.at[k], recv_sem=recv_sems.at[k],
                                         device_id=peer, device_id_type=MESH).wait_recv()
        for cp in locals_:
            cp.wait()

    any_spec = pl.BlockSpec(memory_space=pl.ANY)
    return _pcall(body, name=name, in_specs=[any_spec] * n_in, out_specs=[any_spec] * n_out, out_shape=list(out_sds),
                  input_output_aliases=aliases or {},
                  scratch_shapes=[pltpu.SemaphoreType.DMA((n_r,)), pltpu.SemaphoreType.DMA((n_r,)),
                                  pltpu.SemaphoreType.DMA((max(n_l, 1),))])(*ins)


CHIP_MASKS = ((0, 1, 0), (1, 0, 0), (1, 1, 0))
SIBLING = (0, 0, 1)


def _whole(ref, p):
    return ref


def _all_gather(name, shards, col_sharded):
    def dst_view(col):
        def view(ref, p):
            r, cdim = ref.shape[0] // (1 if col else N_CHIPS), ref.shape[1] // (N_CHIPS if col else 1)
            if col:
                return ref.at[:, pl.ds(pl.multiple_of(p["chip"] * cdim, LANES), cdim)]
            return ref.at[pl.ds(pl.multiple_of(p["chip"] * r, 8), r), :]
        return view

    out_sds = [_sds((s.shape[0], s.shape[1] * N_CHIPS) if col else (s.shape[0] * N_CHIPS, s.shape[1]), s.dtype)
               for s, col in zip(shards, col_sharded)]
    remote = [(a, _whole, a, dst_view(col), m) for a, col in enumerate(col_sharded) for m in CHIP_MASKS]
    local = [(a, _whole, a, dst_view(col)) for a, col in enumerate(col_sharded)]
    return _exchange(name, shards, out_sds, remote, local)


class _Place:
    def __getitem__(self, k):
        return lax.axis_index("c") if k == 0 else 2 * lax.axis_index("x") + lax.axis_index("y")


def _placed_call(body, name, grid, in_specs, out_specs, out_shape, sem, ins):
    def wrap(spec):
        return pl.BlockSpec(spec.block_shape, lambda *idx: spec.index_map(*idx, _Place()))

    outs = [wrap(s) for s in out_specs] if isinstance(out_specs, (list, tuple)) else wrap(out_specs)
    return _pcall(body, name=name, grid=grid, in_specs=[wrap(s) for s in in_specs], out_specs=outs, out_shape=out_shape,
                  compiler_params=_params(sem))(*ins)


def _rows_within(n, width, limit=512 * 1024):
    return _pick(n, tuple(t for t in (1024, 512, 256, 128, 64, 32, 16) if t * width <= limit) or (16,))


def _region_view(col):
    def view(ref, p):
        if col:
            cdim = ref.shape[1] // N_CHIPS
            return ref.at[:, pl.ds(pl.multiple_of(p["chip"] * cdim, LANES), cdim)]
        r = ref.shape[0] // N_CHIPS
        return ref.at[pl.ds(pl.multiple_of(p["chip"] * r, 16), r), :]
    return view


def _ag_place(name, w, layer, col):
    _, r, cdim = w.shape
    tr = _rows_within(r, cdim)
    nb = r // tr

    def body(w_ref, o_ref):
        o_ref[...] = w_ref[...].astype(BF16)

    if col:
        out_shape, out_spec = (r, N_CHIPS * cdim), pl.BlockSpec((tr, cdim), lambda i, pr: (i, pr[1]))
    else:
        out_shape, out_spec = (N_CHIPS * r, cdim), pl.BlockSpec((tr, cdim), lambda i, pr: (pr[1] * nb + i, 0))
    return _placed_call(body, name, (nb,), [pl.BlockSpec((None, tr, cdim), lambda i, pr: (layer, i, 0))], out_spec,
                        _sds(out_shape, BF16), ("parallel",), [w])


def _ag_inplace(name, bufs, col_sharded):
    remote = [(a, _region_view(col), a, _region_view(col), m) for a, col in enumerate(col_sharded) for m in CHIP_MASKS]
    return _exchange(name, bufs, [_sds(b.shape, b.dtype) for b in bufs], remote, [], aliases={a: a for a in range(len(bufs))})


def _rs_add2(name, g4, a4, out_dtype):
    J, _, h, C = g4.shape
    tr = _rows_within(h, C)

    def body(g_ref, a_ref, o_ref):
        o_ref[...] = (g_ref[...].astype(F32) + a_ref[...].astype(F32)).astype(o_ref.dtype)

    return _placed_call(body, name, (J, h // tr),
                        [pl.BlockSpec((None, None, tr, C), lambda j, i, pr: (j, pr[0], i, 0)),
                         pl.BlockSpec((None, None, tr, C), lambda j, i, pr: (j, 0, i, 0))],
                        pl.BlockSpec((None, tr, C), lambda j, i, pr: (j, i, 0)), _sds((J, h, C), out_dtype),
                        ("parallel", "parallel"), [g4, a4])


def _rs_add4(name, p3, landed, col):
    _, h, w = landed.shape
    tr = _rows_within(h, w)

    def body(p_ref, a_ref, b_ref, c_ref, o_ref):
        o_ref[...] = ((p_ref[...].astype(F32) + a_ref[...].astype(F32)) + b_ref[...].astype(F32)) + c_ref[...].astype(F32)

    own = (pl.BlockSpec((None, tr, w), lambda i, pr: (0, i, pr[1])) if col else pl.BlockSpec((None, tr, w), lambda i, pr: (pr[1], i, 0)))
    slot = lambda k: pl.BlockSpec((None, tr, w), lambda i, pr: (k, i, 0))
    return _placed_call(body, name, (h // tr,), [own, slot(0), slot(1), slot(2)], pl.BlockSpec((tr, w), lambda i, pr: (i, 0)),
                        _sds((h, w), F32), ("parallel",), [p3, landed, landed, landed])


def _rs_big(grads, col_sharded):
    n = len(grads)
    g4 = [g.reshape((1, 2, g.shape[0] // 2, g.shape[1]) if col else (N_CHIPS, 2, g.shape[0] // (2 * N_CHIPS), g.shape[1]))
          for g, col in zip(grads, col_sharded)]
    other_half = lambda ref, p: ref.at[:, pl.ds(1 - p["c"], 1)]
    theirs = _exchange("rs_sibling_w", g4, [_sds((g.shape[0], 1) + g.shape[2:], g.dtype) for g in g4],
                       [(a, other_half, a, _whole, SIBLING) for a in range(n)], [])
    chip_sum = [_rs_add2("rs_add2_w", g4[a], theirs[a], BF16) for a in range(n)]

    def send_view(col, mask):
        def view(ref, p):
            t = 2 * ((1 - p["x"]) if mask[0] else p["x"]) + ((1 - p["y"]) if mask[1] else p["y"])
            if col:
                sc = ref.shape[2] // N_CHIPS
                return ref.at[0, :, pl.ds(pl.multiple_of(t * sc, LANES), sc)]
            return ref.at[t]
        return view
    slot = lambda k: (lambda ref, p: ref.at[k])
    piece = [(s.shape[1], s.shape[2] // N_CHIPS if col else s.shape[2]) for s, col in zip(chip_sum, col_sharded)]
    landed = _exchange("rs_chips_w", chip_sum, [_sds((len(CHIP_MASKS),) + s, BF16) for s in piece],
                       [(a, send_view(col_sharded[a], m), a, slot(k), m) for a in range(n) for k, m in enumerate(CHIP_MASKS)], [])
    mine = [_rs_add4("rs_add4_w", chip_sum[a], landed[a], col_sharded[a]) for a in range(n)]

    other = _exchange("rs_halves_w", mine, [_sds(m.shape, F32) for m in mine], [(a, _whole, a, _whole, SIBLING) for a in range(n)], [])
    return mine, other


def _adamw_big(name, mine, other, w, m, v):
    depth, R, C = w.shape
    h = R // 2
    tr = _pick(h, tuple(t for t in (512, 256, 128, 64, 32, 16, 8) if t * C <= 256 * 1024) or (8,))
    nb = h // tr

    def g_spec(kk, hh):
        def imap(l, s, i, pr):
            before = (l < kk) | ((l == kk) & (s < hh))
            return (jnp.where((l == kk) & (s == hh), i, jnp.where(before, 0, nb - 1)), 0)
        return pl.BlockSpec((tr, C), imap)

    st_spec = pl.BlockSpec((None, tr, C), lambda l, s, i, pr: (l, jnp.where(s == 0, pr[0], 1 - pr[0]) * nb + i, 0))

    def body(*refs):
        g_refs = refs[:2 * depth]
        w_ref, m_ref, v_ref, go_ref, d_ref, mo_ref, vo_ref = refs[2 * depth:]
        l, s = pl.program_id(0), pl.program_id(1)
        for kk in range(depth):
            for hh in range(2):
                @pl.when((l == kk) & (s == hh))
                def _(kk=kk, hh=hh):
                    g = g_refs[2 * kk + hh][...]
                    d, mn, vn = _adam_math(w_ref[...], g, m_ref[...], v_ref[...])
                    go_ref[...] = g
                    d_ref[...] = d
                    mo_ref[...] = mn
                    vo_ref[...] = vn

    gs, g_specs = [], []
    for kk in range(depth):
        gs += [mine[kk], other[kk]]
        g_specs += [g_spec(kk, 0), g_spec(kk, 1)]
    return _placed_call(body, name, (depth, 2, nb), g_specs + [st_spec] * 3, [st_spec] * 4, [_sds(w.shape, F32)] * 4,
                        ("arbitrary", "arbitrary", "arbitrary"), gs + [w, m, v])


def _piece_view(col, j, other):
    def view(ref, p):
        R, C = ref.shape
        cc = (1 - p["c"]) if other else p["c"]
        if col:
            hr, sc = R // 2, C // N_CHIPS
            return ref.at[pl.ds(pl.multiple_of(cc * hr, 16), hr), pl.ds(j * sc, sc)]
        hr = R // (2 * N_CHIPS)
        return ref.at[pl.ds(pl.multiple_of((2 * j + cc) * hr, 8), hr), :]
    return view


def _piece_shape(shape, col):
    R, C = shape
    return (R // 2, C // N_CHIPS) if col else (R // (2 * N_CHIPS), C)


def _reduce_scatter(tag, grads, col_sharded, wire_dtype):
    n = len(grads)
    shapes = [_piece_shape(g.shape, col) for g, col in zip(grads, col_sharded)]

    slot = lambda j: (lambda ref, p: ref.at[j])
    remote = [(a, _piece_view(col_sharded[a], j, True), a, slot(j), SIBLING) for a in range(n) for j in range(N_CHIPS)]
    local = [(a, _piece_view(col_sharded[a], j, False), n + a, slot(j)) for a in range(n) for j in range(N_CHIPS)]
    got = _exchange("rs_sibling_" + tag, grads, [_sds((N_CHIPS,) + s, g.dtype) for s, g in zip(shapes, grads)] * 2, remote, local)
    theirs, mine = got[:n], got[n:]
    chip_sum = [_ew("rs_add2_" + tag, lambda a, b: (a.astype(F32) + b.astype(F32),),
                    [m.reshape(-1, m.shape[-1]), t.reshape(-1, t.shape[-1])], [wire_dtype])[0].reshape(m.shape)
                for m, t in zip(mine, theirs)]

    def send_view(mask):
        return lambda ref, p: ref.at[2 * ((1 - p["x"]) if mask[0] else p["x"]) + ((1 - p["y"]) if mask[1] else p["y"])]
    remote = [(a, send_view(m), a, slot(k), m) for a in range(n) for k, m in enumerate(CHIP_MASKS)]
    local = [(a, lambda ref, p: ref.at[p["chip"]], n + a, _whole) for a in range(n)]
    got = _exchange("rs_chips_" + tag, chip_sum,
                    [_sds((len(CHIP_MASKS),) + s, wire_dtype) for s in shapes] + [_sds(s, wire_dtype) for s in shapes], remote, local)
    landed, own = got[:n], got[n:]
    half = [_ew("rs_add4_" + tag, lambda o, a, b, c: (((o.astype(F32) + a.astype(F32)) + b.astype(F32)) + c.astype(F32),),
                [o, l[0], l[1], l[2]], [F32])[0] for o, l in zip(own, landed)]

    def half_rows(ref, p):
        hr = ref.shape[0] // 2
        return ref.at[pl.ds(pl.multiple_of(p["c"] * hr, 8), hr), :]
    remote = [(a, _whole, a, half_rows, SIBLING) for a in range(n)]
    local = [(a, _whole, a, half_rows) for a in range(n)]
    return _exchange("rs_halves_" + tag, half, [_sds((2 * s[0], s[1]), F32) for s in shapes], remote, local)


def _ssm_prepare(p):
    lr, li, ls = p['ssm_lambda_re'], p['ssm_lambda_im'], p['ssm_log_step']
    G = lr.shape[0]
    flat = lambda a: a.reshape(-1, LANES)
    bc = lambda a: flat(jnp.broadcast_to(a, (G, STATE, GROUP_CH)))
    lr3, li3, ls3 = bc(lr[:, :, None]), bc(li[:, :, None]), bc(ls[:, None, None])
    bbr, bbi = _ssm_bbar(lr3, li3, ls3, flat(p['ssm_b_re']), flat(p['ssm_b_im']))
    bbr, bbi = bbr.reshape(G, STATE, GROUP_CH), bbi.reshape(G, STATE, GROUP_CH)
    row = lambda a: a.reshape(1, G * STATE)
    tf, tr = _ssm_tables(row(lr), row(li), row(jnp.broadcast_to(ls[:, None], (G, STATE))))
    cr = p['ssm_c_re'].transpose(0, 2, 1)
    ci = -p['ssm_c_im'].transpose(0, 2, 1)
    return dict(flat3=(lr3, li3, ls3), tf=tf, tr=tr,
                wb=(_tile_w(bbr), _tile_w(bbi)), wbT=(_slab_w(bbr), _slab_w(bbi)),
                wc=(_slab_w(cr), _slab_w(ci)), wcT=(_tile_w(cr), _tile_w(ci)))


def _layer_fwd(x, p, w, dims):
    attn_w, kv_w, u_off = dims['attn_w'], dims['kv_w'], dims['u_off']
    s = _ssm_prepare(p)
    h = _rms_fwd("norm_mix", [x], [p['norm_mix_g']], BF16)
    proj, = _mm("mm_in", h, w['w_in'], 'nn', [F32])
    attn = _attn_fwd(proj, p['q_norm_g'], p['k_norm_g'], p['attn_sinks'], attn_w, kv_w)
    bur, bui = _blockproj("ssm_bu", proj, u_off, *s['wb'])
    xr, xi = _scan("ssm_scan_fwd", bur, bui, s['tf'], False)
    y, gl = _ssm_out(xr, xi, *s['wc'], proj, u_off, p['ssm_d'])
    ssm, z = _mm("mm_glu", gl, w['w_glu'], 'nn', [F32, F32], extras=[('row', p['b_glu']), ('tile', gl)],
                 epi=lambda acc, b, g: ((lambda zz: (g * jax.nn.sigmoid(zz), zz))(acc + b)))
    mix = _rms_fwd("norm_heads", [attn, ssm], [p['attn_out_g'], p['ssm_out_g']], BF16)
    x_mid, = _mm("mm_out", mix, w['w_out'], 'nn', [F32], extras=[('tile', x)], epi=lambda acc, r: (acc + r,))
    h2 = _rms_fwd("norm_mlp", [x_mid], [p['norm_mlp_g']], BF16)
    a, r = _mm("mm_up", h2, w['w_up'], 'nn', [F32, BF16],
               epi=lambda acc: (acc, jnp.square(jnp.maximum(acc, 0.0))))
    x_out, = _mm("mm_down", r, w['w_down'], 'nn', [F32], extras=[('tile', x_mid)], epi=lambda acc, rr: (acc + rr,))
    saved = dict(x=x, h=h, proj=proj, attn=attn, xr=xr, xi=xi, y=y, gl=gl, z=z, ssm=ssm, mix=mix, x_mid=x_mid, h2=h2, a=a, r=r, s=s)
    return x_out, saved


def _layer_bwd(dx, sv, p, w, dims):
    attn_w, kv_w, u_off = dims['attn_w'], dims['kv_w'], dims['u_off']
    s = sv['s']
    gb, gs = {}, {}
    da, = _mm("mm_down_dx", dx, w['w_down'], 'nt', [BF16], extras=[('tile', sv['a'])],
              epi=lambda acc, a: (acc * (2.0 * jnp.maximum(a, 0.0)),))
    gb['w_down'], = _mm("mm_down_dw", sv['r'], dx, 'tn', [BF16])
    dh2, = _mm("mm_up_dx", da, w['w_up'], 'nt', [F32])
    gb['w_up'], = _mm("mm_up_dw", sv['h2'], da, 'tn', [BF16])
    (dx_mid,), (gs['norm_mlp_g'],) = _rms_bwd("norm_mlp_bwd", [sv['x_mid']], [p['norm_mlp_g']], dh2, resid=dx)
    dmix, = _mm("mm_out_dx", dx_mid, w['w_out'], 'nt', [F32])
    gb['w_out'], = _mm("mm_out_dw", sv['mix'], dx_mid, 'tn', [BF16])
    (dattn, dssm), (gs['attn_out_g'], gs['ssm_out_g']) = _rms_bwd(
        "norm_heads_bwd", [sv['attn'], sv['ssm']], [p['attn_out_g'], p['ssm_out_g']], dmix)
    dz, gs['b_glu'] = _glu_dz(dssm, sv['gl'], sv['z'])
    dy, = _mm("mm_glu_dx", dz, w['w_glu'], 'nt', [F32], extras=[('tile', dssm), ('tile', sv['z']), ('tile', sv['y'])],
              epi=lambda acc, ds, z, y: ((acc + ds * jax.nn.sigmoid(z)) * _gelu_grad(y),))
    gb['w_glu'], = _mm("mm_glu_dw", sv['gl'], dz, 'tn', [BF16])
    dxr, dxi = _blockproj("ssm_dstate", dy, 0, *s['wcT'])
    gxr, gxi, s1, s2 = _scan("ssm_scan_bwd", dxr, dxi, s['tr'], True, states=(sv['xr'], sv['xi']))
    du, gs['ssm_d'] = _ssm_du(gxr, gxi, *s['wbT'], dy, sv['proj'], u_off, p['ssm_d'])
    dwb_r, dwb_i = _blockproj_grad("ssm_dbbar", sv['proj'], u_off, gxr, gxi)
    dwc_r, dwc_i = _ssm_dc(sv['xr'], sv['xi'], dy, s['wc'][0].shape[1])
    gs['ssm_c_re'] = _slab_w_grad(dwc_r).transpose(0, 2, 1)
    gs['ssm_c_im'] = -_slab_w_grad(dwc_i).transpose(0, 2, 1)
    G = p['ssm_lambda_re'].shape[0]
    flat = lambda a_: a_.reshape(-1, LANES)
    dbr, dbi, qr, qi = _ssm_param_bwd_flat(*s['flat3'], flat(p['ssm_b_re']), flat(p['ssm_b_im']),
                                           flat(_tile_w_grad(dwb_r)), flat(_tile_w_grad(dwb_i)))
    gs['ssm_b_re'], gs['ssm_b_im'] = dbr.reshape(G, STATE, GROUP_CH), dbi.reshape(G, STATE, GROUP_CH)
    pick = lambda q: q[:, ::GROUP_CH].reshape(G, STATE)
    dlr, dli, dls = _ssm_param_bwd(p['ssm_lambda_re'], p['ssm_lambda_im'], p['ssm_log_step'][:, None], pick(qr), pick(qi),
                                   s1.reshape(8, G, STATE), s2.reshape(8, G, STATE))
    gs['ssm_lambda_re'], gs['ssm_lambda_im'], gs['ssm_log_step'] = dlr, dli, dls[:, 0]
    dq, dkn, dv, gs['q_norm_g'], gs['attn_sinks'] = _attn_bwd(sv['proj'], sv['attn'], dattn, p['q_norm_g'], p['k_norm_g'],
                                                               p['attn_sinks'], attn_w, kv_w)
    dk, gs['k_norm_g'] = _knorm_bwd(sv['proj'], dkn, p['k_norm_g'], attn_w, kv_w)
    dproj = jnp.concatenate([dq.astype(BF16), dk.astype(BF16), dv.astype(BF16), du.astype(BF16)], axis=1)
    dh, = _mm("mm_in_dx", dproj, w['w_in'], 'nt', [F32])
    gb['w_in'], = _mm("mm_in_dw", sv['h'], dproj, 'tn', [BF16])
    (dx_in,), (gs['norm_mix_g'],) = _rms_bwd("norm_mix_bwd", [sv['x']], [p['norm_mix_g']], dh, resid=dx_mid)
    return dx_in, gb, gs


PACK_COLS = 1024


def _pack(arrs, rows):
    flat = jnp.concatenate([a.reshape(-1).astype(F32) for a in arrs])
    return jnp.pad(flat, (0, rows * PACK_COLS - flat.shape[0])).reshape(rows, PACK_COLS)


def _unpack(packed, shapes):
    flat = packed.reshape(-1)
    out, off = [], 0
    for s in shapes:
        n = int(np.prod(s))
        out.append(flat[off:off + n].reshape(s))
        off += n
    return out


def _pack_rows(shapes, multiple):
    n = sum(int(np.prod(s)) for s in shapes)
    rows = -(-n // PACK_COLS)
    return -(-rows // multiple) * multiple


def kernel(x, meta_tokens, norm_mix_g, w_in, q_norm_g, k_norm_g, attn_sinks, ssm_lambda_re, ssm_lambda_im, ssm_log_step, ssm_b_re, ssm_b_im, ssm_c_re, ssm_c_im, ssm_d, w_glu, b_glu, attn_out_g, ssm_out_g, w_out, norm_mlp_g, w_up, w_down, loss_target, m_meta_tokens, m_norm_mix_g, m_w_in, m_q_norm_g, m_k_norm_g, m_attn_sinks, m_ssm_lambda_re, m_ssm_lambda_im, m_ssm_log_step, m_ssm_b_re, m_ssm_b_im, m_ssm_c_re, m_ssm_c_im, m_ssm_d, m_w_glu, m_b_glu, m_attn_out_g, m_ssm_out_g, m_w_out, m_norm_mlp_g, m_w_up, m_w_down, v_meta_tokens, v_norm_mix_g, v_w_in, v_q_norm_g, v_k_norm_g, v_attn_sinks, v_ssm_lambda_re, v_ssm_lambda_im, v_ssm_log_step, v_ssm_b_re, v_ssm_b_im, v_ssm_c_re, v_ssm_c_im, v_ssm_d, v_w_glu, v_b_glu, v_attn_out_g, v_ssm_out_g, v_w_out, v_norm_mlp_g, v_w_up, v_w_down):
    args = (meta_tokens, norm_mix_g, w_in, q_norm_g, k_norm_g, attn_sinks, ssm_lambda_re, ssm_lambda_im, ssm_log_step, ssm_b_re, ssm_b_im, ssm_c_re, ssm_c_im, ssm_d, w_glu, b_glu, attn_out_g, ssm_out_g, w_out, norm_mlp_g, w_up, w_down)
    ms = (m_meta_tokens, m_norm_mix_g, m_w_in, m_q_norm_g, m_k_norm_g, m_attn_sinks, m_ssm_lambda_re, m_ssm_lambda_im, m_ssm_log_step, m_ssm_b_re, m_ssm_b_im, m_ssm_c_re, m_ssm_c_im, m_ssm_d, m_w_glu, m_b_glu, m_attn_out_g, m_ssm_out_g, m_w_out, m_norm_mlp_g, m_w_up, m_w_down)
    vs = (v_meta_tokens, v_norm_mix_g, v_w_in, v_q_norm_g, v_k_norm_g, v_attn_sinks, v_ssm_lambda_re, v_ssm_lambda_im, v_ssm_log_step, v_ssm_b_re, v_ssm_b_im, v_ssm_c_re, v_ssm_c_im, v_ssm_d, v_w_glu, v_b_glu, v_attn_out_g, v_ssm_out_g, v_w_out, v_norm_mlp_g, v_w_up, v_w_down)
    W = dict(zip(WEIGHTS, args))
    M = dict(zip(WEIGHTS, ms))
    V = dict(zip(WEIGHTS, vs))
    depth = norm_mix_g.shape[0]
    seq, D = x.shape[1], x.shape[2]
    attn_w = D // 2
    kv_w = attn_w // KV_GROUP
    dims = dict(attn_w=attn_w, kv_w=kv_w, u_off=(attn_w + 2 * kv_w) // LANES)
    small_names = [n for n in WEIGHTS if n not in BIG and n != 'meta_tokens']
    chip = 2 * lax.axis_index("x") + lax.axis_index("y")
    cols = [COL_SHARDED[n] for n in BIG]

    meta_full, = _all_gather("ag_meta", [meta_tokens], [True])
    full = []
    for l in range(depth):
        placed = [_ag_place("ag_place_" + n, W[n], l, COL_SHARDED[n]) for n in BIG]
        full.append(dict(zip(BIG, _ag_inplace("ag_weights", placed, cols))))

    h_res = jnp.concatenate([jnp.zeros((PAD, D), F32), meta_full, x[0]], axis=0)
    layer_p = []
    for l in range(depth):
        p = {n: W[n][l] for n in small_names}
        for n in ('norm_mix_g', 'q_norm_g', 'k_norm_g', 'attn_sinks', 'ssm_d', 'b_glu', 'attn_out_g', 'ssm_out_g', 'norm_mlp_g'):
            p[n] = p[n][None, :]
        layer_p.append(p)
    saved = []
    for l in range(depth):
        h_res, sv = _layer_fwd(h_res, layer_p[l], full[l], dims)
        saved.append(sv)
    loss_local, dx = _loss(h_res, loss_target[0])
    loss = lax.psum(loss_local, ("x", "y", "c"))

    small_grads = [None] * depth
    shard_grads = [None] * depth
    for l in reversed(range(depth)):
        dx, gb, gs = _layer_bwd(dx, saved[l], layer_p[l], full[l], dims)
        saved[l] = None
        small_grads[l] = gs
        shard_grads[l] = _rs_big([gb[n] for n in BIG], cols)
    grad_x = dx[BLOCK:].reshape(x.shape)

    g_small = {n: jnp.stack([small_grads[l][n].reshape(W[n].shape[1:]) for l in range(depth)]) for n in small_names}
    g_shapes = [(N_META, D)] + [W[n].shape for n in small_names]
    rows = _pack_rows(g_shapes, 8 * 2 * N_CHIPS)
    packed = _pack([dx[PAD:BLOCK]] + [g_small[n] for n in small_names], rows)
    red, = _reduce_scatter("small", [packed], [False], F32)
    red_full, = _all_gather("ag_small", [red], [False])
    g_list = _unpack(red_full, g_shapes)
    g_meta = lax.dynamic_slice_in_dim(g_list[0], chip * meta_tokens.shape[1], meta_tokens.shape[1], axis=1)
    G = dict(zip(small_names, g_list[1:]))
    G['meta_tokens'] = g_meta

    out = {}
    for a, n in enumerate(BIG):
        out[n] = _adamw_big("adamw_" + n, [shard_grads[l][0][a] for l in range(depth)],
                            [shard_grads[l][1][a] for l in range(depth)], W[n], M[n], V[n])
    names = ['meta_tokens'] + small_names
    shapes = [W[n].shape for n in names]
    prow = _pack_rows(shapes, 8)
    d_p, m_p, v_p = _ew("adamw_small", _adam_math, [_pack([W[n] for n in names], prow), _pack([G[n] for n in names], prow),
                                                    _pack([M[n] for n in names], prow), _pack([V[n] for n in names], prow)], [F32] * 3)
    for n, d_, m_, v_ in zip(names, _unpack(d_p, shapes), _unpack(m_p, shapes), _unpack(v_p, shapes)):
        out[n] = (G[n], d_, m_, v_)
    return (loss, grad_x, *[out[n][0] for n in WEIGHTS], *[out[n][1] for n in WEIGHTS],
            *[out[n][2] for n in WEIGHTS], *[out[n][3] for n in WEIGHTS])
```

```python
import functools
import math

import numpy as np
import jax
import jax.numpy as jnp
from jax import lax
from jax.experimental import pallas as pl
from jax.experimental.pallas import tpu as pltpu

F32 = jnp.float32
BF16 = jnp.bfloat16
MESH = pl.DeviceIdType.MESH

N_META = 16
HEAD_DIM = 64
KV_GROUP = 4
GROUP_CH = 16
STATE = 64
BLOCK = 128
PAD = BLOCK - N_META
NORM_EPS = 1e-6
NEG_INF = -1e30
LANES = 128
V7X_VMEM_LIMIT_BYTES = 56 * 1024 * 1024

ADAM_LR, ADAM_B1, ADAM_B2, ADAM_EPS, ADAM_WD, ADAM_STEP = 0.001, 0.9, 0.999, 1e-08, 0.01, 10

WEIGHTS = ['meta_tokens', 'norm_mix_g', 'w_in', 'q_norm_g', 'k_norm_g', 'attn_sinks', 'ssm_lambda_re',
           'ssm_lambda_im', 'ssm_log_step', 'ssm_b_re', 'ssm_b_im', 'ssm_c_re', 'ssm_c_im', 'ssm_d', 'w_glu',
           'b_glu', 'attn_out_g', 'ssm_out_g', 'w_out', 'norm_mlp_g', 'w_up', 'w_down']
BIG = ['w_in', 'w_glu', 'w_out', 'w_up', 'w_down']
COL_SHARDED = {'w_in': True, 'w_glu': False, 'w_out': False, 'w_up': True, 'w_down': False}
N_CHIPS = 4


def _pick(n, cands):
    for c in cands:
        if c <= n and n % c == 0:
            return c
    return n


def _params(sem):
    return pltpu.CompilerParams(dimension_semantics=sem, vmem_limit_bytes=V7X_VMEM_LIMIT_BYTES)


def _pcall(body, **kw):
    return pl.pallas_call(body, **kw)


def _sds(shape, dtype):
    return jax.ShapeDtypeStruct(shape, dtype)


_DIMS = {'nn': ((1,), (0,)), 'nt': ((1,), (1,)), 'tn': ((0,), (0,))}


def _mm(name, a, b, mode, out_dtypes, extras=(), epi=None):
    if mode == 'nn':
        (M, K), (_, N) = a.shape, b.shape
    elif mode == 'nt':
        (M, K), (N, _) = a.shape, b.shape
    else:
        (K, M), (_, N) = a.shape, b.shape
    if mode == 'tn':
        tm, tn, tk = _pick(M, (1024, 512, 256, 128)), _pick(N, (512, 640, 256, 128)), _pick(K, (1408, 704, 384, 128))
    else:
        tk_cands = (2048, 1024, 512, 256, 128) if a.dtype == BF16 else (1024, 512, 256, 128)
        tm, tn, tk = _pick(M, (1408, 704, 384, 128)), _pick(N, (512, 640, 256, 128)), _pick(K, tk_cands)
    nk = K // tk
    a_spec = pl.BlockSpec((tk, tm), lambda i, j, k: (k, i)) if mode == 'tn' else pl.BlockSpec((tm, tk), lambda i, j, k: (i, k))
    b_spec = pl.BlockSpec((tn, tk), lambda i, j, k: (j, k)) if mode == 'nt' else pl.BlockSpec((tk, tn), lambda i, j, k: (k, j))
    ex_specs = [pl.BlockSpec((tm, tn), lambda i, j, k: (i, j)) if kind == 'tile' else pl.BlockSpec((1, tn), lambda i, j, k: (0, j))
                for kind, _ in extras]
    ne, no = len(extras), len(out_dtypes)
    dims = (_DIMS[mode], ((), ()))

    def body(a_ref, b_ref, *rest):
        ex, outs, acc = rest[:ne], rest[ne:ne + no], rest[ne + no]
        k = pl.program_id(2)

        @pl.when(k == 0)
        def _():
            acc[...] = jnp.zeros_like(acc)

        acc[...] += lax.dot_general(a_ref[...].astype(BF16), b_ref[...].astype(BF16), dims, preferred_element_type=F32)

        @pl.when(k == nk - 1)
        def _():
            r = acc[...]
            res = epi(r, *[e[...] for e in ex]) if epi is not None else (r,)
            for o, v in zip(outs, res):
                o[...] = v.astype(o.dtype)

    outs = _pcall(
        body, name=name, grid=(M // tm, N // tn, nk),
        in_specs=[a_spec, b_spec] + ex_specs,
        out_specs=[pl.BlockSpec((tm, tn), lambda i, j, k: (i, j)) for _ in out_dtypes],
        out_shape=[_sds((M, N), d) for d in out_dtypes],
        scratch_shapes=[pltpu.VMEM((tm, tn), F32)],
        compiler_params=_params(("parallel", "parallel", "arbitrary")),
    )(a, b, *[e for _, e in extras])
    return outs


def _ew(name, fn, ins, out_dtypes):
    R, C = ins[0].shape
    tr = _pick(R, tuple(t for t in (1024, 512, 256, 128, 64, 32, 16, 8) if t * C <= 512 * 1024) or (8,))
    n_in = len(ins)

    def body(*refs):
        res = fn(*[r[...] for r in refs[:n_in]])
        for o, v in zip(refs[n_in:], res):
            o[...] = v.astype(o.dtype)

    spec = pl.BlockSpec((tr, C), lambda i: (i, 0))
    return _pcall(body, name=name, grid=(R // tr,), in_specs=[spec] * n_in, out_specs=[spec] * len(out_dtypes),
                  out_shape=[_sds((R, C), d) for d in out_dtypes], compiler_params=_params(("parallel",)))(*ins)


def _adam_math(w, g, m, v):
    m = ADAM_B1 * m + (1.0 - ADAM_B1) * g
    v = ADAM_B2 * v + (1.0 - ADAM_B2) * (g * g)
    m_hat = m / (1.0 - ADAM_B1 ** ADAM_STEP)
    v_hat = v / (1.0 - ADAM_B2 ** ADAM_STEP)
    delta = -ADAM_LR * (m_hat / (jnp.sqrt(v_hat) + ADAM_EPS) + ADAM_WD * w)
    return delta, m, v


def _rms_fwd(name, xs, gs, out_dtype):
    L = xs[0].shape[0]
    ws = [x.shape[1] for x in xs]
    n = len(xs)
    tr = _pick(L, (384, 256, 128))

    def body(*refs):
        o = refs[2 * n]
        off = 0
        for i in range(n):
            x = refs[i][...]
            r = lax.rsqrt(jnp.mean(x * x, axis=-1, keepdims=True) + NORM_EPS)
            o[:, off:off + ws[i]] = ((x * r) * refs[n + i][...]).astype(o.dtype)
            off += ws[i]

    return _pcall(body, name=name, grid=(L // tr,),
                  in_specs=[pl.BlockSpec((tr, w), lambda i: (i, 0)) for w in ws] + [pl.BlockSpec((1, w), lambda i: (0, 0)) for w in ws],
                  out_specs=pl.BlockSpec((tr, sum(ws)), lambda i: (i, 0)), out_shape=_sds((L, sum(ws)), out_dtype),
                  compiler_params=_params(("parallel",)))(*xs, *gs)


def _rms_bwd(name, xs, gs, dy, resid=None):
    L = xs[0].shape[0]
    ws = [x.shape[1] for x in xs]
    n = len(xs)
    tr = _pick(L, (384, 256, 128))
    has_res = resid is not None

    def body(*refs):
        x_refs, g_refs, dy_ref = refs[:n], refs[n:2 * n], refs[2 * n]
        p = 2 * n + 1
        res_ref = refs[p] if has_res else None
        p += 1 if has_res else 0
        dx_refs, dg_refs = refs[p:p + n], refs[p + n:p + 2 * n]
        first = pl.program_id(0) == 0
        off = 0
        for i in range(n):
            x = x_refs[i][...]
            d = dy_ref[:, off:off + ws[i]]
            r = lax.rsqrt(jnp.mean(x * x, axis=-1, keepdims=True) + NORM_EPS)
            xh = x * r
            dg = jnp.sum(d * xh, axis=0, keepdims=True)

            @pl.when(first)
            def _(i=i):
                dg_refs[i][...] = jnp.zeros_like(dg_refs[i])

            dg_refs[i][...] += dg
            dyg = d * g_refs[i][...]
            dx = r * (dyg - xh * jnp.mean(dyg * xh, axis=-1, keepdims=True))
            if has_res:
                dx = dx + res_ref[...]
            dx_refs[i][...] = dx
            off += ws[i]

    in_specs = ([pl.BlockSpec((tr, w), lambda i: (i, 0)) for w in ws] + [pl.BlockSpec((1, w), lambda i: (0, 0)) for w in ws]
                + [pl.BlockSpec((tr, sum(ws)), lambda i: (i, 0))])
    ins = list(xs) + list(gs) + [dy]
    if has_res:
        in_specs.append(pl.BlockSpec((tr, ws[0]), lambda i: (i, 0)))
        ins.append(resid)
    outs = _pcall(body, name=name, grid=(L // tr,), in_specs=in_specs,
                  out_specs=[pl.BlockSpec((tr, w), lambda i: (i, 0)) for w in ws] + [pl.BlockSpec((1, w), lambda i: (0, 0)) for w in ws],
                  out_shape=[_sds((L, w), F32) for w in ws] + [_sds((1, w), F32) for w in ws],
                  compiler_params=_params(("arbitrary",)))(*ins)
    return outs[:n], outs[n:]


def _loss(xl, target):
    Lp, D = xl.shape

    def body(x_ref, t_ref, dy_ref, loss_ref):
        n = pl.program_id(0)

        @pl.when(n == 0)
        def _():
            loss_ref[...] = jnp.zeros_like(loss_ref)
            dy_ref[...] = jnp.zeros_like(dy_ref)

        @pl.when(n > 0)
        def _():
            err = x_ref[...] - t_ref[...]
            dy_ref[...] = err * (1.0 / D)
            loss_ref[...] += jnp.sum(err * err) * (0.5 / D)

    dy, loss = _pcall(body, name="loss_head", grid=(Lp // BLOCK,),
                      in_specs=[pl.BlockSpec((BLOCK, D), lambda n: (n, 0)), pl.BlockSpec((BLOCK, D), lambda n: (jnp.maximum(n - 1, 0), 0))],
                      out_specs=[pl.BlockSpec((BLOCK, D), lambda n: (n, 0)), pl.BlockSpec((8, LANES), lambda n: (0, 0))],
                      out_shape=[_sds((Lp, D), F32), _sds((8, LANES), F32)],
                      compiler_params=_params(("arbitrary",)))(xl, target)
    return loss[0, 0], dy


def _attn_mask_dist(n):
    i = lax.broadcasted_iota(jnp.int32, (BLOCK, 3 * BLOCK), 0)
    j = lax.broadcasted_iota(jnp.int32, (BLOCK, 3 * BLOCK), 1)
    in_band = j < 2 * BLOCK
    band = in_band & (j > i) & (j <= i + BLOCK) & (j >= 2 * BLOCK - BLOCK * n)
    jm = j - 2 * BLOCK
    meta = (~in_band) & (jm >= PAD) & (jm <= BLOCK * n + i)
    dist = jnp.where(in_band, BLOCK + i - j, BLOCK * n + i - jm).astype(F32)
    return band | meta, dist


def _head_norm(x, g):
    r = lax.rsqrt(jnp.mean(x * x, axis=-1, keepdims=True) + NORM_EPS)
    return (x * r) * g, r


def _attn_specs(attn_w, kv_w):
    kb = attn_w // kv_w
    q_spec = pl.BlockSpec((BLOCK, attn_w), lambda n: (n, 0))

    def kv(col):
        return [pl.BlockSpec((BLOCK, kv_w), lambda n: (jnp.maximum(n - 1, 0), col)),
                pl.BlockSpec((BLOCK, kv_w), lambda n: (n, col)),
                pl.BlockSpec((BLOCK, kv_w), lambda n: (0, col))]

    return q_spec, kv(kb), kv(kb + 1)


def _slopes(n_heads):
    return [2.0 ** (-8.0 * (h + 1) / n_heads) for h in range(n_heads)]


def _scores(qn, kn, slope, sink, mask, dist):
    s = lax.dot_general(qn.astype(BF16), kn.astype(BF16), (((1,), (1,)), ((), ())), preferred_element_type=F32)
    s = s * (1.0 / math.sqrt(HEAD_DIM)) - slope * dist
    s = jnp.where(mask, s, NEG_INF)
    m = jnp.maximum(jnp.max(s, axis=-1, keepdims=True), sink)
    p = jnp.exp(s - m)
    ps = jnp.exp(sink - m)
    inv = 1.0 / (jnp.sum(p, axis=-1, keepdims=True) + ps)
    return p * inv, ps * inv


def _attn_fwd(proj, gq, gk, sinks, attn_w, kv_w):
    Lp = proj.shape[0]
    n_heads, n_kv = attn_w // HEAD_DIM, kv_w // HEAD_DIM
    slopes = _slopes(n_heads)
    q_spec, k_specs, v_specs = _attn_specs(attn_w, kv_w)

    def body(q_ref, kp, kc, km, vp, vc, vm, gq_ref, gk_ref, sk_ref, o_ref):
        mask, dist = _attn_mask_dist(pl.program_id(0))
        for kh in range(n_kv):
            cs = slice(kh * HEAD_DIM, (kh + 1) * HEAD_DIM)
            kn, _ = _head_norm(jnp.concatenate([kp[:, cs], kc[:, cs], km[:, cs]], axis=0), gk_ref[...])
            vcat = jnp.concatenate([vp[:, cs], vc[:, cs], vm[:, cs]], axis=0).astype(BF16)
            for g in range(KV_GROUP):
                h = kh * KV_GROUP + g
                hs = slice(h * HEAD_DIM, (h + 1) * HEAD_DIM)
                qn, _ = _head_norm(q_ref[:, hs], gq_ref[...])
                p, _ = _scores(qn, kn, slopes[h], sk_ref[0:1, h:h + 1], mask, dist)
                o_ref[:, hs] = jnp.dot(p.astype(BF16), vcat, preferred_element_type=F32)

    small = lambda w: pl.BlockSpec((1, w), lambda n: (0, 0))
    return _pcall(body, name="attn_fwd", grid=(Lp // BLOCK,),
                  in_specs=[q_spec] + k_specs + v_specs + [small(HEAD_DIM), small(HEAD_DIM), small(n_heads)],
                  out_specs=pl.BlockSpec((BLOCK, attn_w), lambda n: (n, 0)), out_shape=_sds((Lp, attn_w), F32),
                  compiler_params=_params(("parallel",)))(proj, proj, proj, proj, proj, proj, proj, gq, gk, sinks)


def _attn_bwd(proj, attn, dattn, gq, gk, sinks, attn_w, kv_w):
    Lp = proj.shape[0]
    n_heads, n_kv = attn_w // HEAD_DIM, kv_w // HEAD_DIM
    slopes = _slopes(n_heads)
    q_spec, k_specs, v_specs = _attn_specs(attn_w, kv_w)
    scale = 1.0 / math.sqrt(HEAD_DIM)
    tn_dims = (((0,), (0,)), ((), ()))

    def body(q_ref, kp, kc, km, vp, vc, vm, o_ref, do_ref, gq_ref, gk_ref, sk_ref, dq_ref, dk_ref, dv_ref, dgq_ref, dsk_ref):
        n = pl.program_id(0)

        @pl.when(n == 0)
        def _():
            dk_ref[...] = jnp.zeros_like(dk_ref)
            dv_ref[...] = jnp.zeros_like(dv_ref)
            dgq_ref[...] = jnp.zeros_like(dgq_ref)
            dsk_ref[...] = jnp.zeros_like(dsk_ref)

        mask, dist = _attn_mask_dist(n)
        lane = lax.broadcasted_iota(jnp.int32, (1, n_heads), 1)
        rows_prev = pl.ds(pl.multiple_of(jnp.maximum(n - 1, 0) * BLOCK, BLOCK), BLOCK)
        rows_cur = pl.ds(pl.multiple_of(n * BLOCK, BLOCK), BLOCK)
        rows_meta = pl.ds(0, BLOCK)
        dgq = jnp.zeros((1, HEAD_DIM), F32)
        dsk = jnp.zeros((1, n_heads), F32)
        for kh in range(n_kv):
            cs = slice(kh * HEAD_DIM, (kh + 1) * HEAD_DIM)
            kn, _ = _head_norm(jnp.concatenate([kp[:, cs], kc[:, cs], km[:, cs]], axis=0), gk_ref[...])
            kn16 = kn.astype(BF16)
            vcat = jnp.concatenate([vp[:, cs], vc[:, cs], vm[:, cs]], axis=0).astype(BF16)
            dkn = jnp.zeros((3 * BLOCK, HEAD_DIM), F32)
            dvc = jnp.zeros((3 * BLOCK, HEAD_DIM), F32)
            for g in range(KV_GROUP):
                h = kh * KV_GROUP + g
                hs = slice(h * HEAD_DIM, (h + 1) * HEAD_DIM)
                q = q_ref[:, hs]
                qn, rq = _head_norm(q, gq_ref[...])
                p, ps = _scores(qn, kn, slopes[h], sk_ref[0:1, h:h + 1], mask, dist)
                do = do_ref[:, hs]
                dd = jnp.sum(do * o_ref[:, hs], axis=-1, keepdims=True)
                do16 = do.astype(BF16)
                dp = lax.dot_general(do16, vcat, (((1,), (1,)), ((), ())), preferred_element_type=F32)
                ds16 = (p * (dp - dd)).astype(BF16)
                dsk = dsk + jnp.where(lane == h, jnp.sum(-ps * dd), 0.0)
                dqn = jnp.dot(ds16, kn16, preferred_element_type=F32) * scale
                dkn = dkn + lax.dot_general(ds16, qn.astype(BF16), tn_dims, preferred_element_type=F32) * scale
                dvc = dvc + lax.dot_general(p.astype(BF16), do16, tn_dims, preferred_element_type=F32)
                xh = q * rq
                dgq = dgq + jnp.sum(dqn * xh, axis=0, keepdims=True)
                dyg = dqn * gq_ref[...]
                dq_ref[:, hs] = rq * (dyg - xh * jnp.mean(dyg * xh, axis=-1, keepdims=True))
            for part, rows in enumerate((rows_prev, rows_cur, rows_meta)):
                ps_ = slice(part * BLOCK, (part + 1) * BLOCK)
                dk_ref[rows, cs] += dkn[ps_]
                dv_ref[rows, cs] += dvc[ps_]
        dgq_ref[...] += dgq
        dsk_ref[...] += dsk

    small = lambda w: pl.BlockSpec((1, w), lambda n: (0, 0))
    blk = pl.BlockSpec((BLOCK, attn_w), lambda n: (n, 0))
    whole = pl.BlockSpec((Lp, kv_w), lambda n: (0, 0))
    return _pcall(body, name="attn_bwd", grid=(Lp // BLOCK,),
                  in_specs=[q_spec] + k_specs + v_specs + [blk, blk, small(HEAD_DIM), small(HEAD_DIM), small(n_heads)],
                  out_specs=[blk, whole, whole, small(HEAD_DIM), small(n_heads)],
                  out_shape=[_sds((Lp, attn_w), F32), _sds((Lp, kv_w), F32), _sds((Lp, kv_w), F32),
                             _sds((1, HEAD_DIM), F32), _sds((1, n_heads), F32)],
                  compiler_params=_params(("arbitrary",)))(proj, proj, proj, proj, proj, proj, proj, attn, dattn, gq, gk, sinks)


def _knorm_bwd(proj, dkn, gk, attn_w, kv_w):
    Lp = proj.shape[0]
    n_kv = kv_w // HEAD_DIM
    tr = _pick(Lp, (384, 256, 128))

    def body(k_ref, d_ref, g_ref, dk_ref, dg_ref):
        @pl.when(pl.program_id(0) == 0)
        def _():
            dg_ref[...] = jnp.zeros_like(dg_ref)

        dg = jnp.zeros((1, HEAD_DIM), F32)
        for kh in range(n_kv):
            cs = slice(kh * HEAD_DIM, (kh + 1) * HEAD_DIM)
            x = k_ref[:, cs]
            d = d_ref[:, cs]
            r = lax.rsqrt(jnp.mean(x * x, axis=-1, keepdims=True) + NORM_EPS)
            xh = x * r
            dg = dg + jnp.sum(d * xh, axis=0, keepdims=True)
            dyg = d * g_ref[...]
            dk_ref[:, cs] = r * (dyg - xh * jnp.mean(dyg * xh, axis=-1, keepdims=True))
        dg_ref[...] += dg

    return _pcall(body, name="knorm_bwd", grid=(Lp // tr,),
                  in_specs=[pl.BlockSpec((tr, kv_w), lambda i: (i, attn_w // kv_w)), pl.BlockSpec((tr, kv_w), lambda i: (i, 0)),
                            pl.BlockSpec((1, HEAD_DIM), lambda i: (0, 0))],
                  out_specs=[pl.BlockSpec((tr, kv_w), lambda i: (i, 0)), pl.BlockSpec((1, HEAD_DIM), lambda i: (0, 0))],
                  out_shape=[_sds((Lp, kv_w), F32), _sds((1, HEAD_DIM), F32)],
                  compiler_params=_params(("arbitrary",)))(proj, dkn, gk)


def _ssm_bbar(lr, li, ls, br, bi):
    def fn(lr, li, ls, br, bi):
        fr, fi = _zoh_factor(lr, li, ls)
        return fr * br - fi * bi, fr * bi + fi * br

    return _ew("ssm_bbar", fn, [lr, li, ls, br, bi], [F32, F32])


def _lam_bar(lr, li, ls):
    dl = jnp.exp(ls)
    e = jnp.exp(lr * dl)
    return e * jnp.cos(li * dl), e * jnp.sin(li * dl), dl


def _zoh_factor(lr, li, ls):
    ar, ai, _ = _lam_bar(lr, li, ls)
    n2 = lr * lr + li * li
    ivr, ivi = lr / n2, -li / n2
    return (ar - 1.0) * ivr - ai * ivi, (ar - 1.0) * ivi + ai * ivr


SCAN_SHIFTS = (1, 2, 4)


def _ssm_tables(lr, li, ls):
    Wx = lr.shape[1]

    def body(lr_ref, li_ref, ls_ref, tf_ref, tr_ref):
        dl = jnp.exp(ls_ref[...])
        zr, zi = lr_ref[...] * dl, li_ref[...] * dl
        row = lax.broadcasted_iota(jnp.int32, (8, Wx), 0)

        def power(kf):
            e = jnp.exp(kf * zr)
            return e * jnp.cos(kf * zi), e * jnp.sin(kf * zi)

        for ref, rev in ((tf_ref, False), (tr_ref, True)):
            sgn = -1.0 if rev else 1.0
            for k, d in enumerate(SCAN_SHIFTS):
                ar, ai = power(jnp.full((8, Wx), float(d), F32))
                keep = (row < 8 - d) if rev else (row >= d)
                ref[k] = jnp.where(keep, ar, 0.0)
                ref[4 + k] = jnp.where(keep, sgn * ai, 0.0)
            pr, pi = power(((8 - row) if rev else (row + 1)).astype(F32))
            ref[3] = pr
            ref[7] = sgn * pi

    full = pl.BlockSpec((1, Wx), lambda: (0, 0))
    tab = pl.BlockSpec((8, 8, Wx), lambda: (0, 0, 0))
    return _pcall(body, name="ssm_tables", in_specs=[full] * 3, out_specs=[tab, tab],
                  out_shape=[_sds((8, 8, Wx), F32)] * 2,
                  compiler_params=pltpu.CompilerParams(vmem_limit_bytes=V7X_VMEM_LIMIT_BYTES))(lr, li, ls)


def _scan(name, br, bi, tab, reverse, states=None):
    L, Wx = br.shape
    TB = _pick(L, (384, 256, 128))
    CW = _pick(Wx, (512, 256, 128))
    nT, nG = L // TB, TB // 8

    def body(*refs):
        if reverse:
            br_ref, bi_ref, xr_ref, xi_ref, tab_ref, or_ref, oi_ref, s1_ref, s2_ref, cr_ref, ci_ref = refs
        else:
            br_ref, bi_ref, tab_ref, or_ref, oi_ref, cr_ref, ci_ref = refs

        @pl.when(pl.program_id(1) == 0)
        def _():
            cr_ref[...] = jnp.zeros_like(cr_ref)
            ci_ref[...] = jnp.zeros_like(ci_ref)
            if reverse:
                s1_ref[...] = jnp.zeros_like(s1_ref)
                s2_ref[...] = jnp.zeros_like(s2_ref)

        def step(q, carry):
            cr, ci = carry[0], carry[1]
            g = (nG - 1 - q) if reverse else q
            rows = pl.ds(pl.multiple_of(g * 8, 8), 8)
            b_r, b_i = br_ref[rows, :], bi_ref[rows, :]
            sr, si = b_r, b_i
            for k, d in enumerate(SCAN_SHIFTS):
                mr, mi = tab_ref[k], tab_ref[4 + k]
                sh = (8 - d) if reverse else d
                pr, pi = pltpu.roll(sr, sh, 0), pltpu.roll(si, sh, 0)
                sr, si = sr + mr * pr - mi * pi, si + mr * pi + mi * pr
            pwr, pwi = tab_ref[3], tab_ref[7]
            xr = sr + pwr * cr - pwi * ci
            xi = si + pwr * ci + pwi * cr
            or_ref[rows, :] = xr
            oi_ref[rows, :] = xi
            row = 0 if reverse else 7
            out = (jnp.broadcast_to(xr[row:row + 1, :], xr.shape), jnp.broadcast_to(xi[row:row + 1, :], xi.shape))
            if reverse:
                hr, hi = xr - b_r, xi - b_i
                st_r, st_i = xr_ref[rows, :], xi_ref[rows, :]
                out = out + (carry[2] + hr * st_r + hi * st_i, carry[3] + hi * st_r - hr * st_i)
            return out

        init = (cr_ref[...], ci_ref[...])
        if reverse:
            init = init + (jnp.zeros((8, CW), F32), jnp.zeros((8, CW), F32))
        fin = lax.fori_loop(0, nG, step, init)
        cr_ref[...] = fin[0]
        ci_ref[...] = fin[1]
        if reverse:
            s1_ref[...] += fin[2]
            s2_ref[...] += fin[3]

    tmap = (lambda j, t: (nT - 1 - t, j)) if reverse else (lambda j, t: (t, j))
    blk = pl.BlockSpec((TB, CW), tmap)
    tab_spec = pl.BlockSpec((8, 8, CW), lambda j, t: (0, 0, j))
    sum_spec = pl.BlockSpec((8, CW), lambda j, t: (0, j))
    ins = [br, bi] + (list(states) if reverse else []) + [tab]
    in_specs = [blk, blk] + ([blk, blk] if reverse else []) + [tab_spec]
    out_specs = [blk, blk] + ([sum_spec, sum_spec] if reverse else [])
    out_shape = [_sds((L, Wx), F32)] * 2 + ([_sds((8, Wx), F32)] * 2 if reverse else [])
    return _pcall(body, name=name, grid=(Wx // CW, nT), in_specs=in_specs, out_specs=out_specs, out_shape=out_shape,
                  scratch_shapes=[pltpu.VMEM((8, CW), F32), pltpu.VMEM((8, CW), F32)],
                  compiler_params=_params(("parallel", "arbitrary")))(*ins)


def _row_tile(L):
    return _pick(L, (1408, 704, 384, 128))


def _blockproj(name, src, off, w_r, w_i):
    L = src.shape[0]
    T = w_r.shape[0]
    tm = _row_tile(L)

    def body(s_ref, wr_ref, wi_ref, or_ref, oi_ref):
        s = s_ref[...].astype(BF16)
        or_ref[...] = jnp.dot(s, wr_ref[...], preferred_element_type=F32)
        oi_ref[...] = jnp.dot(s, wi_ref[...], preferred_element_type=F32)

    w_spec = pl.BlockSpec((None, LANES, LANES), lambda i, t: (t, 0, 0))
    o_spec = pl.BlockSpec((tm, LANES), lambda i, t: (i, t))
    return _pcall(body, name=name, grid=(L // tm, T),
                  in_specs=[pl.BlockSpec((tm, LANES), lambda i, t: (i, off + t // 4)), w_spec, w_spec],
                  out_specs=[o_spec, o_spec], out_shape=[_sds((L, T * LANES), F32)] * 2,
                  compiler_params=_params(("parallel", "arbitrary")))(src, w_r, w_i)


def _blockproj_grad(name, src, off, gr, gi):
    L = src.shape[0]
    T = gr.shape[1] // LANES
    tm = _row_tile(L)
    tn_dims = (((0,), (0,)), ((), ()))

    def body(s_ref, gr_ref, gi_ref, or_ref, oi_ref):
        @pl.when(pl.program_id(1) == 0)
        def _():
            or_ref[...] = jnp.zeros_like(or_ref)
            oi_ref[...] = jnp.zeros_like(oi_ref)

        s = s_ref[...].astype(BF16)
        or_ref[...] += lax.dot_general(s, gr_ref[...].astype(BF16), tn_dims, preferred_element_type=F32)
        oi_ref[...] += lax.dot_general(s, gi_ref[...].astype(BF16), tn_dims, preferred_element_type=F32)

    g_spec = pl.BlockSpec((tm, LANES), lambda t, i: (i, t))
    o_spec = pl.BlockSpec((None, LANES, LANES), lambda t, i: (t, 0, 0))
    return _pcall(body, name=name, grid=(T, L // tm),
                  in_specs=[pl.BlockSpec((tm, LANES), lambda t, i: (i, off + t // 4)), g_spec, g_spec],
                  out_specs=[o_spec, o_spec], out_shape=[_sds((T, LANES, LANES), F32)] * 2,
                  compiler_params=_params(("parallel", "arbitrary")))(src, gr, gi)


def _gelu(y):
    k = math.sqrt(2.0 / math.pi)
    return 0.5 * y * (1.0 + jnp.tanh(k * (y + 0.044715 * (y * y * y))))


def _gelu_grad(y):
    k = math.sqrt(2.0 / math.pi)
    t = jnp.tanh(k * (y + 0.044715 * (y * y * y)))
    return 0.5 * (1.0 + t) + 0.5 * y * (1.0 - t * t) * (k * (1.0 + 3 * 0.044715 * (y * y)))


def _ssm_out(xr, xi, w_r, w_i, proj, u_off, dvec):
    L = xr.shape[0]
    J = w_r.shape[0]
    SW = w_r.shape[1]
    tm = _row_tile(L)

    def body(xr_ref, xi_ref, wr_ref, wi_ref, u_ref, d_ref, y_ref, gl_ref):
        acc = jnp.dot(xr_ref[...].astype(BF16), wr_ref[...], preferred_element_type=F32)
        acc += jnp.dot(xi_ref[...].astype(BF16), wi_ref[...], preferred_element_type=F32)
        y = acc + d_ref[...] * u_ref[...]
        y_ref[...] = y
        gl_ref[...] = _gelu(y)

    x_spec = pl.BlockSpec((tm, SW), lambda j, i: (i, j))
    w_spec = pl.BlockSpec((None, SW, LANES), lambda j, i: (j, 0, 0))
    o_spec = pl.BlockSpec((tm, LANES), lambda j, i: (i, j))
    return _pcall(body, name="ssm_out", grid=(J, L // tm),
                  in_specs=[x_spec, x_spec, w_spec, w_spec, pl.BlockSpec((tm, LANES), lambda j, i: (i, u_off + j)),
                            pl.BlockSpec((1, LANES), lambda j, i: (0, j))],
                  out_specs=[o_spec, o_spec], out_shape=[_sds((L, J * LANES), F32)] * 2,
                  compiler_params=_params(("parallel", "parallel")))(xr, xi, w_r, w_i, proj, dvec)


def _ssm_du(gr, gi, w_r, w_i, dy, proj, u_off, dvec):
    L = gr.shape[0]
    J = w_r.shape[0]
    SW = w_r.shape[1]
    tm = _row_tile(L)

    def body(gr_ref, gi_ref, wr_ref, wi_ref, dy_ref, u_ref, d_ref, du_ref, dd_ref):
        i = pl.program_id(1)

        @pl.when(i == 0)
        def _():
            dd_ref[...] = jnp.zeros_like(dd_ref)

        acc = jnp.dot(gr_ref[...].astype(BF16), wr_ref[...], preferred_element_type=F32)
        acc += jnp.dot(gi_ref[...].astype(BF16), wi_ref[...], preferred_element_type=F32)
        dy = dy_ref[...]
        row = lax.broadcasted_iota(jnp.int32, (tm, LANES), 0) + i * tm
        du_ref[...] = jnp.where(row >= PAD, acc + d_ref[...] * dy, 0.0)
        dd_ref[...] += jnp.sum(dy * u_ref[...], axis=0, keepdims=True)

    x_spec = pl.BlockSpec((tm, SW), lambda j, i: (i, j))
    w_spec = pl.BlockSpec((None, SW, LANES), lambda j, i: (j, 0, 0))
    o_spec = pl.BlockSpec((tm, LANES), lambda j, i: (i, j))
    vec = pl.BlockSpec((1, LANES), lambda j, i: (0, j))
    return _pcall(body, name="ssm_du", grid=(J, L // tm),
                  in_specs=[x_spec, x_spec, w_spec, w_spec, o_spec, pl.BlockSpec((tm, LANES), lambda j, i: (i, u_off + j)), vec],
                  out_specs=[o_spec, vec], out_shape=[_sds((L, J * LANES), F32), _sds((1, J * LANES), F32)],
                  compiler_params=_params(("parallel", "arbitrary")))(gr, gi, w_r, w_i, dy, proj, dvec)


def _ssm_dc(xr, xi, dy, SW):
    L = xr.shape[0]
    J = dy.shape[1] // LANES
    tm = _row_tile(L)
    tn_dims = (((0,), (0,)), ((), ()))

    def body(xr_ref, xi_ref, dy_ref, or_ref, oi_ref):
        @pl.when(pl.program_id(1) == 0)
        def _():
            or_ref[...] = jnp.zeros_like(or_ref)
            oi_ref[...] = jnp.zeros_like(oi_ref)

        d = dy_ref[...].astype(BF16)
        or_ref[...] += lax.dot_general(xr_ref[...].astype(BF16), d, tn_dims, preferred_element_type=F32)
        oi_ref[...] += lax.dot_general(xi_ref[...].astype(BF16), d, tn_dims, preferred_element_type=F32)

    x_spec = pl.BlockSpec((tm, SW), lambda j, i: (i, j))
    o_spec = pl.BlockSpec((None, SW, LANES), lambda j, i: (j, 0, 0))
    return _pcall(body, name="ssm_dc", grid=(J, L // tm),
                  in_specs=[x_spec, x_spec, pl.BlockSpec((tm, LANES), lambda j, i: (i, j))],
                  out_specs=[o_spec, o_spec], out_shape=[_sds((J, SW, LANES), F32)] * 2,
                  compiler_params=_params(("parallel", "arbitrary")))(xr, xi, dy)


def _glu_dz(ds, gl, z):
    L, W = ds.shape
    tr = _pick(L, (384, 256, 128))

    def body(ds_ref, gl_ref, z_ref, dz_ref, db_ref):
        @pl.when(pl.program_id(0) == 0)
        def _():
            db_ref[...] = jnp.zeros_like(db_ref)

        sg = jax.nn.sigmoid(z_ref[...])
        dz = ds_ref[...] * gl_ref[...] * (sg * (1.0 - sg))
        dz_ref[...] = dz.astype(BF16)
        db_ref[...] += jnp.sum(dz, axis=0, keepdims=True)

    spec = pl.BlockSpec((tr, W), lambda i: (i, 0))
    vec = pl.BlockSpec((1, W), lambda i: (0, 0))
    return _pcall(body, name="glu_dz", grid=(L // tr,), in_specs=[spec] * 3, out_specs=[spec, vec],
                  out_shape=[_sds((L, W), BF16), _sds((1, W), F32)], compiler_params=_params(("arbitrary",)))(ds, gl, z)


def _ssm_param_bwd_flat(lr, li, ls, br, bi, dbbr, dbbi):
    def seg_sum(x):
        for s in (8, 4, 2, 1):
            x = x + pltpu.roll(x, LANES - s, 1)
        return x

    def fn(lr, li, ls, br, bi, dbbr, dbbi):
        fr, fi = _zoh_factor(lr, li, ls)
        return (fr * dbbr + fi * dbbi, fr * dbbi - fi * dbbr,
                seg_sum(br * dbbr + bi * dbbi), seg_sum(br * dbbi - bi * dbbr))

    return _ew("ssm_param_bwd_flat", fn, [lr, li, ls, br, bi, dbbr, dbbi], [F32] * 4)


def _ssm_param_bwd(lr, li, ls, dfr, dfi, s1, s2):
    G, P = lr.shape

    def body(lr_ref, li_ref, ls_ref, dfr_ref, dfi_ref, s1_ref, s2_ref, dlr_ref, dli_ref, dls_ref):
        lr, li = lr_ref[...], li_ref[...]
        ar, ai, dl = _lam_bar(lr, li, ls_ref[...])
        sr, si = s1_ref[0], s2_ref[0]
        for k in range(1, 8):
            sr = sr + s1_ref[k]
            si = si + s2_ref[k]
        a2 = ar * ar + ai * ai
        gar, gai = (sr * ar - si * ai) / a2, (sr * ai + si * ar) / a2
        n2 = lr * lr + li * li
        ivr, ivi = lr / n2, -li / n2
        fr = (ar - 1.0) * ivr - ai * ivi
        fi = (ar - 1.0) * ivi + ai * ivr
        dfr, dfi = dfr_ref[...], dfi_ref[...]
        gar = gar + ivr * dfr + ivi * dfi
        gai = gai + ivr * dfi - ivi * dfr
        wr, wi = -(fr * ivr - fi * ivi), -(fr * ivi + fi * ivr)
        glr, gli = wr * dfr + wi * dfi, wr * dfi - wi * dfr
        gzr, gzi = ar * gar + ai * gai, ar * gai - ai * gar
        dlr_ref[...] = glr + dl * gzr
        dli_ref[...] = gli + dl * gzi
        dls_ref[...] = dl * jnp.sum(lr * gzr + li * gzi, axis=-1, keepdims=True)

    m = pl.BlockSpec((G, P), lambda: (0, 0))
    v = pl.BlockSpec((G, 1), lambda: (0, 0))
    s = pl.BlockSpec((8, G, P), lambda: (0, 0, 0))
    return _pcall(body, name="ssm_param_bwd", in_specs=[m, m, v, m, m, s, s], out_specs=[m, m, v],
                  out_shape=[_sds((G, P), F32), _sds((G, P), F32), _sds((G, 1), F32)])(lr, li, ls, dfr, dfi, s1, s2)


def _tile_mask(G):
    T = G // 2
    e = np.zeros((T, 8, 1, 2, 1), np.float32)
    for t in range(T):
        for c in range(2):
            e[t, (2 * t + c) % 8, 0, c, 0] = 1.0
    return e


def _tile_w(arr):
    G = arr.shape[0]
    a = arr.reshape(G // 2, 1, 2, STATE, GROUP_CH).transpose(0, 1, 4, 2, 3)
    return (a * _tile_mask(G)).reshape(G // 2, LANES, LANES).astype(BF16)


def _tile_w_grad(dw):
    G = dw.shape[0] * 2
    d = dw.reshape(G // 2, 8, GROUP_CH, 2, STATE) * _tile_mask(G)
    return d.sum(axis=1).transpose(0, 2, 3, 1).reshape(G, STATE, GROUP_CH)


def _slab_w(arr):
    G = arr.shape[0]
    a = arr.reshape(G // 8, 8, STATE, 1, GROUP_CH)
    eye = np.eye(8, dtype=np.float32).reshape(1, 8, 1, 8, 1)
    return (a * eye).reshape(G // 8, 8 * STATE, LANES).astype(BF16)


def _slab_w_grad(dw):
    J = dw.shape[0]
    eye = np.eye(8, dtype=np.float32).reshape(1, 8, 1, 8, 1)
    return (dw.reshape(J, 8, STATE, 8, GROUP_CH) * eye).sum(axis=3).reshape(J * 8, STATE, GROUP_CH)


def _exchange(name, ins, out_sds, remote, local, aliases=None):
    n_in, n_out, n_r, n_l = len(ins), len(out_sds), len(remote), len(local)

    def body(*refs):
        in_refs, out_refs = refs[:n_in], refs[n_in:n_in + n_out]
        send_sems, recv_sems, local_sems = refs[n_in + n_out:]
        x, y, c = lax.axis_index("x"), lax.axis_index("y"), lax.axis_index("c")

        def place(px, py, pc):
            return dict(x=px, y=py, c=pc, chip=2 * px + py)

        def flip(mask):
            mx, my, mc = mask
            return ((1 - x) if mx else x, (1 - y) if my else y, (1 - c) if mc else c)

        me = place(x, y, c)
        sends = []
        for k, (ii, src, oi, dst, mask) in enumerate(remote):
            cp = pltpu.make_async_remote_copy(src_ref=src(in_refs[ii], me), dst_ref=dst(out_refs[oi], me),
                                              send_sem=send_sems.at[k], recv_sem=recv_sems.at[k],
                                              device_id=flip(mask), device_id_type=MESH)
            cp.start()
            sends.append(cp)
        locals_ = []
        for k, (ii, src, oi, dst) in enumerate(local):
            cp = pltpu.make_async_copy(src(in_refs[ii], me), dst(out_refs[oi], me), local_sems.at[k])
            cp.start()
            locals_.append(cp)
        for k, (ii, src, oi, dst, mask) in enumerate(remote):
            sends[k].wait_send()
            peer = flip(mask)
            pltpu.make_async_remote_copy(src_ref=src(in_refs[ii], me), dst_ref=dst(out_refs[oi], place(*peer)),
                                         send_sem=send_sems.at[k], recv_sem=recv_sems.at[k],
                                         device_id=peer, device_id_type=MESH).wait_recv()
        for cp in locals_:
            cp.wait()

    any_spec = pl.BlockSpec(memory_space=pl.ANY)
    return _pcall(body, name=name, in_specs=[any_spec] * n_in, out_specs=[any_spec] * n_out, out_shape=list(out_sds),
                  input_output_aliases=aliases or {},
                  scratch_shapes=[pltpu.SemaphoreType.DMA((n_r,)), pltpu.SemaphoreType.DMA((n_r,)),
                                  pltpu.SemaphoreType.DMA((max(n_l, 1),))])(*ins)


def _mesh_place():
    x, y, c = lax.axis_index("x"), lax.axis_index("y"), lax.axis_index("c")

    def place(px, py, pc):
        return dict(x=px, y=py, c=pc, chip=2 * px + py)

    def flip(mask):
        mx, my, mc = mask
        return ((1 - x) if mx else x, (1 - y) if my else y, (1 - c) if mc else c)

    return place(x, y, c), place, flip


_HBM = pl.BlockSpec(memory_space=pltpu.HBM)
_SEM = pl.BlockSpec(memory_space=pltpu.SEMAPHORE)
_EFFECT = pltpu.SideEffectType.DATAFLOW_SIDE_EFFECTING


def _split_start(name, bufs, remote):
    n, n_r = len(bufs), len(remote)

    def body(*refs):
        in_refs, send_sems, recv_sems, token = refs[:n], refs[n], refs[n + 1], refs[-1]
        me, _, flip = _mesh_place()
        for k, (si, src, di, dst, mask) in enumerate(remote):
            pltpu.make_async_remote_copy(src_ref=src(in_refs[si], me), dst_ref=dst(in_refs[di], me),
                                         send_sem=send_sems.at[k], recv_sem=recv_sems.at[k],
                                         device_id=flip(mask), device_id_type=MESH).start()
        token[...] = jnp.zeros_like(token)

    outs = _pcall(body, name=name,
                  out_shape=(pltpu.SemaphoreType.DMA((n_r,)), pltpu.SemaphoreType.DMA((n_r,)),
                             *[pltpu.HBM(b.shape, b.dtype) for b in bufs], _sds((8, LANES), F32)),
                  in_specs=[_HBM] * n, out_specs=(_SEM, _SEM, *[_HBM] * n, pl.BlockSpec(memory_space=pltpu.VMEM)),
                  input_output_aliases={i: 2 + i for i in range(n)},
                  compiler_params=pltpu.CompilerParams(has_side_effects=_EFFECT),
                  )(*[pltpu.with_memory_space_constraint(b, pltpu.HBM) for b in bufs])
    return outs[0], outs[1], list(outs[2:2 + n]), outs[-1]


def _split_wait(name, bufs, send_sems, recv_sems, after, remote):
    n = len(bufs)

    def body(*refs):
        in_refs, ssem, rsem = refs[:n], refs[n], refs[n + 1]
        me, place, flip = _mesh_place()
        for k, (si, src, di, dst, mask) in enumerate(remote):
            peer = flip(mask)
            cp = pltpu.make_async_remote_copy(src_ref=src(in_refs[si], me), dst_ref=dst(in_refs[di], place(*peer)),
                                              send_sem=ssem.at[k], recv_sem=rsem.at[k], device_id=peer, device_id_type=MESH)
            cp.wait_send()
            cp.wait_recv()

    return list(_pcall(body, name=name, out_shape=tuple(pltpu.HBM(b.shape, b.dtype) for b in bufs),
                       in_specs=[_HBM] * n + [_SEM, _SEM, pl.BlockSpec(memory_space=pl.ANY)], out_specs=tuple([_HBM] * n),
                       input_output_aliases={i: i for i in range(n)},
                       compiler_params=pltpu.CompilerParams(has_side_effects=_EFFECT))(*bufs, send_sems, recv_sems, after))


CHIP_MASKS = ((0, 1, 0), (1, 0, 0), (1, 1, 0))
SIBLING = (0, 0, 1)


def _whole(ref, p):
    return ref


def _all_gather(name, shards, col_sharded):
    def dst_view(col):
        def view(ref, p):
            r, cdim = ref.shape[0] // (1 if col else N_CHIPS), ref.shape[1] // (N_CHIPS if col else 1)
            if col:
                return ref.at[:, pl.ds(pl.multiple_of(p["chip"] * cdim, LANES), cdim)]
            return ref.at[pl.ds(pl.multiple_of(p["chip"] * r, 8), r), :]
        return view

    out_sds = [_sds((s.shape[0], s.shape[1] * N_CHIPS) if col else (s.shape[0] * N_CHIPS, s.shape[1]), s.dtype)
               for s, col in zip(shards, col_sharded)]
    remote = [(a, _whole, a, dst_view(col), m) for a, col in enumerate(col_sharded) for m in CHIP_MASKS]
    local = [(a, _whole, a, dst_view(col)) for a, col in enumerate(col_sharded)]
    return _exchange(name, shards, out_sds, remote, local)


class _Place:
    def __getitem__(self, k):
        return lax.axis_index("c") if k == 0 else 2 * lax.axis_index("x") + lax.axis_index("y")


def _placed_call(body, name, grid, in_specs, out_specs, out_shape, sem, ins):
    def wrap(spec):
        return pl.BlockSpec(spec.block_shape, lambda *idx: spec.index_map(*idx, _Place()))

    outs = [wrap(s) for s in out_specs] if isinstance(out_specs, (list, tuple)) else wrap(out_specs)
    return _pcall(body, name=name, grid=grid, in_specs=[wrap(s) for s in in_specs], out_specs=outs, out_shape=out_shape,
                  compiler_params=_params(sem))(*ins)


def _rows_within(n, width, limit=512 * 1024):
    return _pick(n, tuple(t for t in (1024, 512, 256, 128, 64, 32, 16) if t * width <= limit) or (16,))


def _region_view(col):
    def view(ref, p):
        if col:
            cdim = ref.shape[1] // N_CHIPS
            return ref.at[:, pl.ds(pl.multiple_of(p["chip"] * cdim, LANES), cdim)]
        r = ref.shape[0] // N_CHIPS
        return ref.at[pl.ds(pl.multiple_of(p["chip"] * r, 16), r), :]
    return view


def _ag_place(name, w, layer, col):
    _, r, cdim = w.shape
    tr = _rows_within(r, cdim)
    nb = r // tr

    def body(w_ref, o_ref):
        o_ref[...] = w_ref[...].astype(BF16)

    if col:
        out_shape, out_spec = (r, N_CHIPS * cdim), pl.BlockSpec((tr, cdim), lambda i, pr: (i, pr[1]))
    else:
        out_shape, out_spec = (N_CHIPS * r, cdim), pl.BlockSpec((tr, cdim), lambda i, pr: (pr[1] * nb + i, 0))
    return _placed_call(body, name, (nb,), [pl.BlockSpec((None, tr, cdim), lambda i, pr: (layer, i, 0))], out_spec,
                        _sds(out_shape, BF16), ("parallel",), [w])


def _ag_copies(col_sharded):
    return [(a, _region_view(col), a, _region_view(col), m) for a, col in enumerate(col_sharded) for m in CHIP_MASKS]


def _rs_add2(name, g4, a4, out_dtype):
    J, _, h, C = g4.shape
    tr = _rows_within(h, C)

    def body(g_ref, a_ref, o_ref):
        o_ref[...] = (g_ref[...].astype(F32) + a_ref[...].astype(F32)).astype(o_ref.dtype)

    return _placed_call(body, name, (J, h // tr),
                        [pl.BlockSpec((None, None, tr, C), lambda j, i, pr: (j, pr[0], i, 0)),
                         pl.BlockSpec((None, None, tr, C), lambda j, i, pr: (j, 0, i, 0))],
                        pl.BlockSpec((None, tr, C), lambda j, i, pr: (j, i, 0)), _sds((J, h, C), out_dtype),
                        ("parallel", "parallel"), [g4, a4])


def _rs_add4(name, p3, landed, col):
    _, h, w = landed.shape
    tr = _rows_within(h, w)

    def body(p_ref, a_ref, b_ref, c_ref, o_ref):
        o_ref[...] = ((p_ref[...].astype(F32) + a_ref[...].astype(F32)) + b_ref[...].astype(F32)) + c_ref[...].astype(F32)

    own = (pl.BlockSpec((None, tr, w), lambda i, pr: (0, i, pr[1])) if col else pl.BlockSpec((None, tr, w), lambda i, pr: (pr[1], i, 0)))
    slot = lambda k: pl.BlockSpec((None, tr, w), lambda i, pr: (k, i, 0))
    return _placed_call(body, name, (h // tr,), [own, slot(0), slot(1), slot(2)], pl.BlockSpec((tr, w), lambda i, pr: (i, 0)),
                        _sds((h, w), F32), ("parallel",), [p3, landed, landed, landed])


def _rs_begin(tag, grads, col_sharded):
    n = len(grads)
    g4 = [g.reshape((1, 2, g.shape[0] // 2, g.shape[1]) if col else (N_CHIPS, 2, g.shape[0] // (2 * N_CHIPS), g.shape[1]))
          for g, col in zip(grads, col_sharded)]
    other_half = lambda ref, p: ref.at[:, pl.ds(1 - p["c"], 1)]
    theirs = _exchange("rs_sibling_w", g4, [_sds((g.shape[0], 1) + g.shape[2:], g.dtype) for g in g4],
                       [(a, other_half, a, _whole, SIBLING) for a in range(n)], [])
    chip_sum = [_rs_add2("rs_add2_w", g4[a], theirs[a], BF16) for a in range(n)]

    def send_view(col, mask):
        def view(ref, p):
            t = 2 * ((1 - p["x"]) if mask[0] else p["x"]) + ((1 - p["y"]) if mask[1] else p["y"])
            if col:
                sc = ref.shape[2] // N_CHIPS
                return ref.at[0, :, pl.ds(pl.multiple_of(t * sc, LANES), sc)]
            return ref.at[t]
        return view
    slot = lambda k: (lambda ref, p: ref.at[k])
    piece = [(s.shape[1], s.shape[2] // N_CHIPS if col else s.shape[2]) for s, col in zip(chip_sum, col_sharded)]
    landing = [lax.empty((len(CHIP_MASKS),) + s, BF16) for s in piece]
    copies = [(a, send_view(col_sharded[a], m), n + a, slot(k), m) for a in range(n) for k, m in enumerate(CHIP_MASKS)]
    send_sems, recv_sems, bufs, token = _split_start("rs_chips_start_" + tag, chip_sum + landing, copies)
    return dict(send_sems=send_sems, recv_sems=recv_sems, bufs=bufs, copies=copies, token=token)


def _rs_finish(tag, st, after, col_sharded):
    n = len(col_sharded)
    bufs = _split_wait("rs_chips_wait_" + tag, st['bufs'], st['send_sems'], st['recv_sems'], after, st['copies'])
    chip_sum, landed = bufs[:n], bufs[n:]
    mine = [_rs_add4("rs_add4_w", chip_sum[a], landed[a], col_sharded[a]) for a in range(n)]
    other = _exchange("rs_halves_w", mine, [_sds(m.shape, F32) for m in mine], [(a, _whole, a, _whole, SIBLING) for a in range(n)], [])
    return mine, other


def _adamw_big(name, mine, other, w, m, v):
    depth, R, C = w.shape
    h = R // 2
    tr = _pick(h, tuple(t for t in (512, 256, 128, 64, 32, 16, 8) if t * C <= 256 * 1024) or (8,))
    nb = h // tr

    def g_spec(kk, hh):
        def imap(l, s, i, pr):
            before = (l < kk) | ((l == kk) & (s < hh))
            return (jnp.where((l == kk) & (s == hh), i, jnp.where(before, 0, nb - 1)), 0)
        return pl.BlockSpec((tr, C), imap)

    st_spec = pl.BlockSpec((None, tr, C), lambda l, s, i, pr: (l, jnp.where(s == 0, pr[0], 1 - pr[0]) * nb + i, 0))

    def body(*refs):
        g_refs = refs[:2 * depth]
        w_ref, m_ref, v_ref, go_ref, d_ref, mo_ref, vo_ref = refs[2 * depth:]
        l, s = pl.program_id(0), pl.program_id(1)
        for kk in range(depth):
            for hh in range(2):
                @pl.when((l == kk) & (s == hh))
                def _(kk=kk, hh=hh):
                    g = g_refs[2 * kk + hh][...]
                    d, mn, vn = _adam_math(w_ref[...], g, m_ref[...], v_ref[...])
                    go_ref[...] = g
                    d_ref[...] = d
                    mo_ref[...] = mn
                    vo_ref[...] = vn

    gs, g_specs = [], []
    for kk in range(depth):
        gs += [mine[kk], other[kk]]
        g_specs += [g_spec(kk, 0), g_spec(kk, 1)]
    return _placed_call(body, name, (depth, 2, nb), g_specs + [st_spec] * 3, [st_spec] * 4, [_sds(w.shape, F32)] * 4,
                        ("arbitrary", "arbitrary", "arbitrary"), gs + [w, m, v])


def _piece_view(col, j, other):
    def view(ref, p):
        R, C = ref.shape
        cc = (1 - p["c"]) if other else p["c"]
        if col:
            hr, sc = R // 2, C // N_CHIPS
            return ref.at[pl.ds(pl.multiple_of(cc * hr, 16), hr), pl.ds(j * sc, sc)]
        hr = R // (2 * N_CHIPS)
        return ref.at[pl.ds(pl.multiple_of((2 * j + cc) * hr, 8), hr), :]
    return view


def _piece_shape(shape, col):
    R, C = shape
    return (R // 2, C // N_CHIPS) if col else (R // (2 * N_CHIPS), C)


def _reduce_scatter(tag, grads, col_sharded, wire_dtype):
    n = len(grads)
    shapes = [_piece_shape(g.shape, col) for g, col in zip(grads, col_sharded)]

    slot = lambda j: (lambda ref, p: ref.at[j])
    remote = [(a, _piece_view(col_sharded[a], j, True), a, slot(j), SIBLING) for a in range(n) for j in range(N_CHIPS)]
    local = [(a, _piece_view(col_sharded[a], j, False), n + a, slot(j)) for a in range(n) for j in range(N_CHIPS)]
    got = _exchange("rs_sibling_" + tag, grads, [_sds((N_CHIPS,) + s, g.dtype) for s, g in zip(shapes, grads)] * 2, remote, local)
    theirs, mine = got[:n], got[n:]
    chip_sum = [_ew("rs_add2_" + tag, lambda a, b: (a.astype(F32) + b.astype(F32),),
                    [m.reshape(-1, m.shape[-1]), t.reshape(-1, t.shape[-1])], [wire_dtype])[0].reshape(m.shape)
                for m, t in zip(mine, theirs)]

    def send_view(mask):
        return lambda ref, p: ref.at[2 * ((1 - p["x"]) if mask[0] else p["x"]) + ((1 - p["y"]) if mask[1] else p["y"])]
    remote = [(a, send_view(m), a, slot(k), m) for a in range(n) for k, m in enumerate(CHIP_MASKS)]
    local = [(a, lambda ref, p: ref.at[p["chip"]], n + a, _whole) for a in range(n)]
    got = _exchange("rs_chips_" + tag, chip_sum,
                    [_sds((len(CHIP_MASKS),) + s, wire_dtype) for s in shapes] + [_sds(s, wire_dtype) for s in shapes], remote, local)
    landed, own = got[:n], got[n:]
    half = [_ew("rs_add4_" + tag, lambda o, a, b, c: (((o.astype(F32) + a.astype(F32)) + b.astype(F32)) + c.astype(F32),),
                [o, l[0], l[1], l[2]], [F32])[0] for o, l in zip(own, landed)]

    def half_rows(ref, p):
        hr = ref.shape[0] // 2
        return ref.at[pl.ds(pl.multiple_of(p["c"] * hr, 8), hr), :]
    remote = [(a, _whole, a, half_rows, SIBLING) for a in range(n)]
    local = [(a, _whole, a, half_rows) for a in range(n)]
    return _exchange("rs_halves_" + tag, half, [_sds((2 * s[0], s[1]), F32) for s in shapes], remote, local)


def _ssm_prepare(p):
    lr, li, ls = p['ssm_lambda_re'], p['ssm_lambda_im'], p['ssm_log_step']
    G = lr.shape[0]
    flat = lambda a: a.reshape(-1, LANES)
    bc = lambda a: flat(jnp.broadcast_to(a, (G, STATE, GROUP_CH)))
    lr3, li3, ls3 = bc(lr[:, :, None]), bc(li[:, :, None]), bc(ls[:, None, None])
    bbr, bbi = _ssm_bbar(lr3, li3, ls3, flat(p['ssm_b_re']), flat(p['ssm_b_im']))
    bbr, bbi = bbr.reshape(G, STATE, GROUP_CH), bbi.reshape(G, STATE, GROUP_CH)
    row = lambda a: a.reshape(1, G * STATE)
    tf, tr = _ssm_tables(row(lr), row(li), row(jnp.broadcast_to(ls[:, None], (G, STATE))))
    cr = p['ssm_c_re'].transpose(0, 2, 1)
    ci = -p['ssm_c_im'].transpose(0, 2, 1)
    return dict(flat3=(lr3, li3, ls3), tf=tf, tr=tr,
                wb=(_tile_w(bbr), _tile_w(bbi)), wbT=(_slab_w(bbr), _slab_w(bbi)),
                wc=(_slab_w(cr), _slab_w(ci)), wcT=(_tile_w(cr), _tile_w(ci)))


def _layer_fwd(x, p, w, dims):
    attn_w, kv_w, u_off = dims['attn_w'], dims['kv_w'], dims['u_off']
    s = _ssm_prepare(p)
    h = _rms_fwd("norm_mix", [x], [p['norm_mix_g']], BF16)
    proj, = _mm("mm_in", h, w['w_in'], 'nn', [F32])
    attn = _attn_fwd(proj, p['q_norm_g'], p['k_norm_g'], p['attn_sinks'], attn_w, kv_w)
    bur, bui = _blockproj("ssm_bu", proj, u_off, *s['wb'])
    xr, xi = _scan("ssm_scan_fwd", bur, bui, s['tf'], False)
    y, gl = _ssm_out(xr, xi, *s['wc'], proj, u_off, p['ssm_d'])
    ssm, z = _mm("mm_glu", gl, w['w_glu'], 'nn', [F32, F32], extras=[('row', p['b_glu']), ('tile', gl)],
                 epi=lambda acc, b, g: ((lambda zz: (g * jax.nn.sigmoid(zz), zz))(acc + b)))
    mix = _rms_fwd("norm_heads", [attn, ssm], [p['attn_out_g'], p['ssm_out_g']], BF16)
    x_mid, = _mm("mm_out", mix, w['w_out'], 'nn', [F32], extras=[('tile', x)], epi=lambda acc, r: (acc + r,))
    h2 = _rms_fwd("norm_mlp", [x_mid], [p['norm_mlp_g']], BF16)
    a, r = _mm("mm_up", h2, w['w_up'], 'nn', [F32, BF16],
               epi=lambda acc: (acc, jnp.square(jnp.maximum(acc, 0.0))))
    x_out, = _mm("mm_down", r, w['w_down'], 'nn', [F32], extras=[('tile', x_mid)], epi=lambda acc, rr: (acc + rr,))
    saved = dict(x=x, h=h, proj=proj, attn=attn, xr=xr, xi=xi, y=y, gl=gl, z=z, ssm=ssm, mix=mix, x_mid=x_mid, h2=h2, a=a, r=r, s=s)
    return x_out, saved


def _layer_bwd(dx, sv, p, w, dims):
    attn_w, kv_w, u_off = dims['attn_w'], dims['kv_w'], dims['u_off']
    s = sv['s']
    gb, gs = {}, {}
    da, = _mm("mm_down_dx", dx, w['w_down'], 'nt', [BF16], extras=[('tile', sv['a'])],
              epi=lambda acc, a: (acc * (2.0 * jnp.maximum(a, 0.0)),))
    gb['w_down'], = _mm("mm_down_dw", sv['r'], dx, 'tn', [BF16])
    dh2, = _mm("mm_up_dx", da, w['w_up'], 'nt', [F32])
    gb['w_up'], = _mm("mm_up_dw", sv['h2'], da, 'tn', [BF16])
    (dx_mid,), (gs['norm_mlp_g'],) = _rms_bwd("norm_mlp_bwd", [sv['x_mid']], [p['norm_mlp_g']], dh2, resid=dx)
    dmix, = _mm("mm_out_dx", dx_mid, w['w_out'], 'nt', [F32])
    gb['w_out'], = _mm("mm_out_dw", sv['mix'], dx_mid, 'tn', [BF16])
    (dattn, dssm), (gs['attn_out_g'], gs['ssm_out_g']) = _rms_bwd(
        "norm_heads_bwd", [sv['attn'], sv['ssm']], [p['attn_out_g'], p['ssm_out_g']], dmix)
    dz, gs['b_glu'] = _glu_dz(dssm, sv['gl'], sv['z'])
    dy, = _mm("mm_glu_dx", dz, w['w_glu'], 'nt', [F32], extras=[('tile', dssm), ('tile', sv['z']), ('tile', sv['y'])],
              epi=lambda acc, ds, z, y: ((acc + ds * jax.nn.sigmoid(z)) * _gelu_grad(y),))
    gb['w_glu'], = _mm("mm_glu_dw", sv['gl'], dz, 'tn', [BF16])
    dxr, dxi = _blockproj("ssm_dstate", dy, 0, *s['wcT'])
    gxr, gxi, s1, s2 = _scan("ssm_scan_bwd", dxr, dxi, s['tr'], True, states=(sv['xr'], sv['xi']))
    du, gs['ssm_d'] = _ssm_du(gxr, gxi, *s['wbT'], dy, sv['proj'], u_off, p['ssm_d'])
    dwb_r, dwb_i = _blockproj_grad("ssm_dbbar", sv['proj'], u_off, gxr, gxi)
    dwc_r, dwc_i = _ssm_dc(sv['xr'], sv['xi'], dy, s['wc'][0].shape[1])
    gs['ssm_c_re'] = _slab_w_grad(dwc_r).transpose(0, 2, 1)
    gs['ssm_c_im'] = -_slab_w_grad(dwc_i).transpose(0, 2, 1)
    G = p['ssm_lambda_re'].shape[0]
    flat = lambda a_: a_.reshape(-1, LANES)
    dbr, dbi, qr, qi = _ssm_param_bwd_flat(*s['flat3'], flat(p['ssm_b_re']), flat(p['ssm_b_im']),
                                           flat(_tile_w_grad(dwb_r)), flat(_tile_w_grad(dwb_i)))
    gs['ssm_b_re'], gs['ssm_b_im'] = dbr.reshape(G, STATE, GROUP_CH), dbi.reshape(G, STATE, GROUP_CH)
    pick = lambda q: q[:, ::GROUP_CH].reshape(G, STATE)
    dlr, dli, dls = _ssm_param_bwd(p['ssm_lambda_re'], p['ssm_lambda_im'], p['ssm_log_step'][:, None], pick(qr), pick(qi),
                                   s1.reshape(8, G, STATE), s2.reshape(8, G, STATE))
    gs['ssm_lambda_re'], gs['ssm_lambda_im'], gs['ssm_log_step'] = dlr, dli, dls[:, 0]
    dq, dkn, dv, gs['q_norm_g'], gs['attn_sinks'] = _attn_bwd(sv['proj'], sv['attn'], dattn, p['q_norm_g'], p['k_norm_g'],
                                                               p['attn_sinks'], attn_w, kv_w)
    dk, gs['k_norm_g'] = _knorm_bwd(sv['proj'], dkn, p['k_norm_g'], attn_w, kv_w)
    dproj = jnp.concatenate([dq.astype(BF16), dk.astype(BF16), dv.astype(BF16), du.astype(BF16)], axis=1)
    dh, = _mm("mm_in_dx", dproj, w['w_in'], 'nt', [F32])
    gb['w_in'], = _mm("mm_in_dw", sv['h'], dproj, 'tn', [BF16])
    (dx_in,), (gs['norm_mix_g'],) = _rms_bwd("norm_mix_bwd", [sv['x']], [p['norm_mix_g']], dh, resid=dx_mid)
    return dx_in, gb, gs


PACK_COLS = 1024


def _pack(arrs, rows):
    flat = jnp.concatenate([a.reshape(-1).astype(F32) for a in arrs])
    return jnp.pad(flat, (0, rows * PACK_COLS - flat.shape[0])).reshape(rows, PACK_COLS)


def _unpack(packed, shapes):
    flat = packed.reshape(-1)
    out, off = [], 0
    for s in shapes:
        n = int(np.prod(s))
        out.append(flat[off:off + n].reshape(s))
        off += n
    return out


def _pack_rows(shapes, multiple):
    n = sum(int(np.prod(s)) for s in shapes)
    rows = -(-n // PACK_COLS)
    return -(-rows // multiple) * multiple


def kernel(x, meta_tokens, norm_mix_g, w_in, q_norm_g, k_norm_g, attn_sinks, ssm_lambda_re, ssm_lambda_im, ssm_log_step, ssm_b_re, ssm_b_im, ssm_c_re, ssm_c_im, ssm_d, w_glu, b_glu, attn_out_g, ssm_out_g, w_out, norm_mlp_g, w_up, w_down, loss_target, m_meta_tokens, m_norm_mix_g, m_w_in, m_q_norm_g, m_k_norm_g, m_attn_sinks, m_ssm_lambda_re, m_ssm_lambda_im, m_ssm_log_step, m_ssm_b_re, m_ssm_b_im, m_ssm_c_re, m_ssm_c_im, m_ssm_d, m_w_glu, m_b_glu, m_attn_out_g, m_ssm_out_g, m_w_out, m_norm_mlp_g, m_w_up, m_w_down, v_meta_tokens, v_norm_mix_g, v_w_in, v_q_norm_g, v_k_norm_g, v_attn_sinks, v_ssm_lambda_re, v_ssm_lambda_im, v_ssm_log_step, v_ssm_b_re, v_ssm_b_im, v_ssm_c_re, v_ssm_c_im, v_ssm_d, v_w_glu, v_b_glu, v_attn_out_g, v_ssm_out_g, v_w_out, v_norm_mlp_g, v_w_up, v_w_down):
    args = (meta_tokens, norm_mix_g, w_in, q_norm_g, k_norm_g, attn_sinks, ssm_lambda_re, ssm_lambda_im, ssm_log_step, ssm_b_re, ssm_b_im, ssm_c_re, ssm_c_im, ssm_d, w_glu, b_glu, attn_out_g, ssm_out_g, w_out, norm_mlp_g, w_up, w_down)
    ms = (m_meta_tokens, m_norm_mix_g, m_w_in, m_q_norm_g, m_k_norm_g, m_attn_sinks, m_ssm_lambda_re, m_ssm_lambda_im, m_ssm_log_step, m_ssm_b_re, m_ssm_b_im, m_ssm_c_re, m_ssm_c_im, m_ssm_d, m_w_glu, m_b_glu, m_attn_out_g, m_ssm_out_g, m_w_out, m_norm_mlp_g, m_w_up, m_w_down)
    vs = (v_meta_tokens, v_norm_mix_g, v_w_in, v_q_norm_g, v_k_norm_g, v_attn_sinks, v_ssm_lambda_re, v_ssm_lambda_im, v_ssm_log_step, v_ssm_b_re, v_ssm_b_im, v_ssm_c_re, v_ssm_c_im, v_ssm_d, v_w_glu, v_b_glu, v_attn_out_g, v_ssm_out_g, v_w_out, v_norm_mlp_g, v_w_up, v_w_down)
    W = dict(zip(WEIGHTS, args))
    M = dict(zip(WEIGHTS, ms))
    V = dict(zip(WEIGHTS, vs))
    depth = norm_mix_g.shape[0]
    seq, D = x.shape[1], x.shape[2]
    attn_w = D // 2
    kv_w = attn_w // KV_GROUP
    dims = dict(attn_w=attn_w, kv_w=kv_w, u_off=(attn_w + 2 * kv_w) // LANES)
    small_names = [n for n in WEIGHTS if n not in BIG and n != 'meta_tokens']
    chip = 2 * lax.axis_index("x") + lax.axis_index("y")
    cols = [COL_SHARDED[n] for n in BIG]

    meta_full, = _all_gather("ag_meta", [meta_tokens], [True])
    ag_copies = _ag_copies(cols)
    gathers, started = [], jnp.zeros((), F32)
    for l in range(depth):
        placed = [_ag_place("ag_place_" + n, W[n], l, COL_SHARDED[n]) for n in BIG]
        send_sems, recv_sems, bufs, token = _split_start("ag_start_%d" % l, placed, ag_copies)
        gathers.append((send_sems, recv_sems, bufs))
        started = started + token[0, 0]

    h_res = jnp.concatenate([jnp.zeros((PAD, D), F32), meta_full, x[0]], axis=0) + started
    layer_p = []
    for l in range(depth):
        p = {n: W[n][l] for n in small_names}
        for n in ('norm_mix_g', 'q_norm_g', 'k_norm_g', 'attn_sinks', 'ssm_d', 'b_glu', 'attn_out_g', 'ssm_out_g', 'norm_mlp_g'):
            p[n] = p[n][None, :]
        layer_p.append(p)
    saved, full = [], []
    for l in range(depth):
        send_sems, recv_sems, bufs = gathers[l]
        full.append(dict(zip(BIG, _split_wait("ag_wait_%d" % l, bufs, send_sems, recv_sems, h_res, ag_copies))))
        h_res, sv = _layer_fwd(h_res, layer_p[l], full[l], dims)
        saved.append(sv)
    loss_local, dx = _loss(h_res, loss_target[0])
    loss = lax.psum(loss_local, ("x", "y", "c"))

    small_grads = [None] * depth
    shard_grads = [None] * depth
    pending = None
    for l in reversed(range(depth)):
        dx, gb, gs = _layer_bwd(dx, saved[l], layer_p[l], full[l], dims)
        saved[l] = None
        small_grads[l] = gs
        if pending is not None:
            shard_grads[l + 1] = _rs_finish("%d" % (l + 1), pending, dx, cols)
        pending = _rs_begin("%d" % l, [gb[n] for n in BIG], cols)
        if l > 0:
            dx = dx + pending['token'][0, 0]
    grad_x = dx[BLOCK:].reshape(x.shape)

    g_small = {n: jnp.stack([small_grads[l][n].reshape(W[n].shape[1:]) for l in range(depth)]) for n in small_names}
    g_shapes = [(N_META, D)] + [W[n].shape for n in small_names]
    rows = _pack_rows(g_shapes, 8 * 2 * N_CHIPS)
    packed = _pack([dx[PAD:BLOCK]] + [g_small[n] for n in small_names], rows)
    red, = _reduce_scatter("small", [packed], [False], F32)
    red_full, = _all_gather("ag_small", [red], [False])
    shard_grads[0] = _rs_finish("0", pending, red_full, cols)
    g_list = _unpack(red_full, g_shapes)
    g_meta = lax.dynamic_slice_in_dim(g_list[0], chip * meta_tokens.shape[1], meta_tokens.shape[1], axis=1)
    G = dict(zip(small_names, g_list[1:]))
    G['meta_tokens'] = g_meta

    out = {}
    for a, n in enumerate(BIG):
        out[n] = _adamw_big("adamw_" + n, [shard_grads[l][0][a] for l in range(depth)],
                            [shard_grads[l][1][a] for l in range(depth)], W[n], M[n], V[n])
    names = ['meta_tokens'] + small_names
    shapes = [W[n].shape for n in names]
    prow = _pack_rows(shapes, 8)
    d_p, m_p, v_p = _ew("adamw_small", _adam_math, [_pack([W[n] for n in names], prow), _pack([G[n] for n in names], prow),
                                                    _pack([M[n] for n in names], prow), _pack([V[n] for n in names], prow)], [F32] * 3)
    for n, d_, m_, v_ in zip(names, _unpack(d_p, shapes), _unpack(m_p, shapes), _unpack(v_p, shapes)):
        out[n] = (G[n], d_, m_, v_)
    return (loss, grad_x, *[out[n][0] for n in WEIGHTS], *[out[n][1] for n in WEIGHTS],
            *[out[n][2] for n in WEIGHTS], *[out[n][3] for n in WEIGHTS])
```

```python
import functools
import math

import numpy as np
import jax
import jax.numpy as jnp
from jax import lax
from jax.experimental import pallas as pl
from jax.experimental.pallas import tpu as pltpu

F32 = jnp.float32
BF16 = jnp.bfloat16
MESH = pl.DeviceIdType.MESH

N_META = 16
HEAD_DIM = 64
KV_GROUP = 4
GROUP_CH = 16
STATE = 64
BLOCK = 128
PAD = BLOCK - N_META
NORM_EPS = 1e-6
NEG_INF = -1e30
LANES = 128
V7X_VMEM_LIMIT_BYTES = 56 * 1024 * 1024

ADAM_LR, ADAM_B1, ADAM_B2, ADAM_EPS, ADAM_WD, ADAM_STEP = 0.001, 0.9, 0.999, 1e-08, 0.01, 10

WEIGHTS = ['meta_tokens', 'norm_mix_g', 'w_in', 'q_norm_g', 'k_norm_g', 'attn_sinks', 'ssm_lambda_re',
           'ssm_lambda_im', 'ssm_log_step', 'ssm_b_re', 'ssm_b_im', 'ssm_c_re', 'ssm_c_im', 'ssm_d', 'w_glu',
           'b_glu', 'attn_out_g', 'ssm_out_g', 'w_out', 'norm_mlp_g', 'w_up', 'w_down']
BIG = ['w_in', 'w_glu', 'w_out', 'w_up', 'w_down']
COL_SHARDED = {'w_in': True, 'w_glu': False, 'w_out': False, 'w_up': True, 'w_down': False}
N_CHIPS = 4


def _pick(n, cands):
    for c in cands:
        if c <= n and n % c == 0:
            return c
    return n


def _params(sem):
    return pltpu.CompilerParams(dimension_semantics=sem, vmem_limit_bytes=V7X_VMEM_LIMIT_BYTES)


def _pcall(body, **kw):
    return pl.pallas_call(body, **kw)


def _sds(shape, dtype):
    return jax.ShapeDtypeStruct(shape, dtype)


_DIMS = {'nn': ((1,), (0,)), 'nt': ((1,), (1,)), 'tn': ((0,), (0,))}


def _mm(name, a, b, mode, out_dtypes, extras=(), epi=None):
    if mode == 'nn':
        (M, K), (_, N) = a.shape, b.shape
    elif mode == 'nt':
        (M, K), (N, _) = a.shape, b.shape
    else:
        (K, M), (_, N) = a.shape, b.shape
    if mode == 'tn':
        tm, tn, tk = _pick(M, (1024, 512, 256, 128)), _pick(N, (512, 640, 256, 128)), _pick(K, (1408, 704, 384, 128))
    else:
        tk_cands = (2048, 1024, 512, 256, 128) if a.dtype == BF16 else (1024, 512, 256, 128)
        tm, tn, tk = _pick(M, (1408, 704, 384, 128)), _pick(N, (512, 640, 256, 128)), _pick(K, tk_cands)
    nk = K // tk
    a_spec = pl.BlockSpec((tk, tm), lambda i, j, k: (k, i)) if mode == 'tn' else pl.BlockSpec((tm, tk), lambda i, j, k: (i, k))
    b_spec = pl.BlockSpec((tn, tk), lambda i, j, k: (j, k)) if mode == 'nt' else pl.BlockSpec((tk, tn), lambda i, j, k: (k, j))
    ex_specs = [pl.BlockSpec((tm, tn), lambda i, j, k: (i, j)) if kind == 'tile' else pl.BlockSpec((1, tn), lambda i, j, k: (0, j))
                for kind, _ in extras]
    ne, no = len(extras), len(out_dtypes)
    dims = (_DIMS[mode], ((), ()))

    def body(a_ref, b_ref, *rest):
        ex, outs, acc = rest[:ne], rest[ne:ne + no], rest[ne + no]
        k = pl.program_id(2)

        @pl.when(k == 0)
        def _():
            acc[...] = jnp.zeros_like(acc)

        acc[...] += lax.dot_general(a_ref[...].astype(BF16), b_ref[...].astype(BF16), dims, preferred_element_type=F32)

        @pl.when(k == nk - 1)
        def _():
            r = acc[...]
            res = epi(r, *[e[...] for e in ex]) if epi is not None else (r,)
            for o, v in zip(outs, res):
                o[...] = v.astype(o.dtype)

    outs = _pcall(
        body, name=name, grid=(M // tm, N // tn, nk),
        in_specs=[a_spec, b_spec] + ex_specs,
        out_specs=[pl.BlockSpec((tm, tn), lambda i, j, k: (i, j)) for _ in out_dtypes],
        out_shape=[_sds((M, N), d) for d in out_dtypes],
        scratch_shapes=[pltpu.VMEM((tm, tn), F32)],
        compiler_params=_params(("parallel", "parallel", "arbitrary")),
    )(a, b, *[e for _, e in extras])
    return outs


def _ew(name, fn, ins, out_dtypes):
    R, C = ins[0].shape
    tr = _pick(R, tuple(t for t in (1024, 512, 256, 128, 64, 32, 16, 8) if t * C <= 512 * 1024) or (8,))
    n_in = len(ins)

    def body(*refs):
        res = fn(*[r[...] for r in refs[:n_in]])
        for o, v in zip(refs[n_in:], res):
            o[...] = v.astype(o.dtype)

    spec = pl.BlockSpec((tr, C), lambda i: (i, 0))
    return _pcall(body, name=name, grid=(R // tr,), in_specs=[spec] * n_in, out_specs=[spec] * len(out_dtypes),
                  out_shape=[_sds((R, C), d) for d in out_dtypes], compiler_params=_params(("parallel",)))(*ins)


def _adam_math(w, g, m, v):
    m = ADAM_B1 * m + (1.0 - ADAM_B1) * g
    v = ADAM_B2 * v + (1.0 - ADAM_B2) * (g * g)
    m_hat = m / (1.0 - ADAM_B1 ** ADAM_STEP)
    v_hat = v / (1.0 - ADAM_B2 ** ADAM_STEP)
    delta = -ADAM_LR * (m_hat / (jnp.sqrt(v_hat) + ADAM_EPS) + ADAM_WD * w)
    return delta, m, v


def _rms_fwd(name, xs, gs, out_dtype):
    L = xs[0].shape[0]
    ws = [x.shape[1] for x in xs]
    n = len(xs)
    tr = _pick(L, (384, 256, 128))

    def body(*refs):
        o = refs[2 * n]
        off = 0
        for i in range(n):
            x = refs[i][...]
            r = lax.rsqrt(jnp.mean(x * x, axis=-1, keepdims=True) + NORM_EPS)
            o[:, off:off + ws[i]] = ((x * r) * refs[n + i][...]).astype(o.dtype)
            off += ws[i]

    return _pcall(body, name=name, grid=(L // tr,),
                  in_specs=[pl.BlockSpec((tr, w), lambda i: (i, 0)) for w in ws] + [pl.BlockSpec((1, w), lambda i: (0, 0)) for w in ws],
                  out_specs=pl.BlockSpec((tr, sum(ws)), lambda i: (i, 0)), out_shape=_sds((L, sum(ws)), out_dtype),
                  compiler_params=_params(("parallel",)))(*xs, *gs)


def _rms_bwd(name, xs, gs, dy, resid=None):
    L = xs[0].shape[0]
    ws = [x.shape[1] for x in xs]
    n = len(xs)
    tr = _pick(L, (384, 256, 128))
    has_res = resid is not None

    def body(*refs):
        x_refs, g_refs, dy_ref = refs[:n], refs[n:2 * n], refs[2 * n]
        p = 2 * n + 1
        res_ref = refs[p] if has_res else None
        p += 1 if has_res else 0
        dx_refs, dg_refs = refs[p:p + n], refs[p + n:p + 2 * n]
        first = pl.program_id(0) == 0
        off = 0
        for i in range(n):
            x = x_refs[i][...]
            d = dy_ref[:, off:off + ws[i]]
            r = lax.rsqrt(jnp.mean(x * x, axis=-1, keepdims=True) + NORM_EPS)
            xh = x * r
            dg = jnp.sum(d * xh, axis=0, keepdims=True)

            @pl.when(first)
            def _(i=i):
                dg_refs[i][...] = jnp.zeros_like(dg_refs[i])

            dg_refs[i][...] += dg
            dyg = d * g_refs[i][...]
            dx = r * (dyg - xh * jnp.mean(dyg * xh, axis=-1, keepdims=True))
            if has_res:
                dx = dx + res_ref[...]
            dx_refs[i][...] = dx
            off += ws[i]

    in_specs = ([pl.BlockSpec((tr, w), lambda i: (i, 0)) for w in ws] + [pl.BlockSpec((1, w), lambda i: (0, 0)) for w in ws]
                + [pl.BlockSpec((tr, sum(ws)), lambda i: (i, 0))])
    ins = list(xs) + list(gs) + [dy]
    if has_res:
        in_specs.append(pl.BlockSpec((tr, ws[0]), lambda i: (i, 0)))
        ins.append(resid)
    outs = _pcall(body, name=name, grid=(L // tr,), in_specs=in_specs,
                  out_specs=[pl.BlockSpec((tr, w), lambda i: (i, 0)) for w in ws] + [pl.BlockSpec((1, w), lambda i: (0, 0)) for w in ws],
                  out_shape=[_sds((L, w), F32) for w in ws] + [_sds((1, w), F32) for w in ws],
                  compiler_params=_params(("arbitrary",)))(*ins)
    return outs[:n], outs[n:]


def _loss(xl, target):
    Lp, D = xl.shape

    def body(x_ref, t_ref, dy_ref, loss_ref):
        n = pl.program_id(0)

        @pl.when(n == 0)
        def _():
            loss_ref[...] = jnp.zeros_like(loss_ref)
            dy_ref[...] = jnp.zeros_like(dy_ref)

        @pl.when(n > 0)
        def _():
            err = x_ref[...] - t_ref[...]
            dy_ref[...] = err * (1.0 / D)
            loss_ref[...] += jnp.sum(err * err) * (0.5 / D)

    dy, loss = _pcall(body, name="loss_head", grid=(Lp // BLOCK,),
                      in_specs=[pl.BlockSpec((BLOCK, D), lambda n: (n, 0)), pl.BlockSpec((BLOCK, D), lambda n: (jnp.maximum(n - 1, 0), 0))],
                      out_specs=[pl.BlockSpec((BLOCK, D), lambda n: (n, 0)), pl.BlockSpec((8, LANES), lambda n: (0, 0))],
                      out_shape=[_sds((Lp, D), F32), _sds((8, LANES), F32)],
                      compiler_params=_params(("arbitrary",)))(xl, target)
    return loss[0, 0], dy


def _attn_mask_dist(n):
    i = lax.broadcasted_iota(jnp.int32, (BLOCK, 3 * BLOCK), 0)
    j = lax.broadcasted_iota(jnp.int32, (BLOCK, 3 * BLOCK), 1)
    in_band = j < 2 * BLOCK
    band = in_band & (j > i) & (j <= i + BLOCK) & (j >= 2 * BLOCK - BLOCK * n)
    jm = j - 2 * BLOCK
    meta = (~in_band) & (jm >= PAD) & (jm <= BLOCK * n + i)
    dist = jnp.where(in_band, BLOCK + i - j, BLOCK * n + i - jm).astype(F32)
    return band | meta, dist


def _head_norm(x, g):
    r = lax.rsqrt(jnp.mean(x * x, axis=-1, keepdims=True) + NORM_EPS)
    return (x * r) * g, r


def _attn_specs(attn_w, kv_w):
    kb = attn_w // kv_w
    q_spec = pl.BlockSpec((BLOCK, attn_w), lambda n: (n, 0))

    def kv(col):
        return [pl.BlockSpec((BLOCK, kv_w), lambda n: (jnp.maximum(n - 1, 0), col)),
                pl.BlockSpec((BLOCK, kv_w), lambda n: (n, col)),
                pl.BlockSpec((BLOCK, kv_w), lambda n: (0, col))]

    return q_spec, kv(kb), kv(kb + 1)


def _slopes(n_heads):
    return [2.0 ** (-8.0 * (h + 1) / n_heads) for h in range(n_heads)]


def _scores(qn, kn, slope, sink, mask, dist):
    s = lax.dot_general(qn.astype(BF16), kn.astype(BF16), (((1,), (1,)), ((), ())), preferred_element_type=F32)
    s = s * (1.0 / math.sqrt(HEAD_DIM)) - slope * dist
    s = jnp.where(mask, s, NEG_INF)
    m = jnp.maximum(jnp.max(s, axis=-1, keepdims=True), sink)
    p = jnp.exp(s - m)
    ps = jnp.exp(sink - m)
    inv = 1.0 / (jnp.sum(p, axis=-1, keepdims=True) + ps)
    return p * inv, ps * inv


def _attn_fwd(proj, gq, gk, sinks, attn_w, kv_w):
    Lp = proj.shape[0]
    n_heads, n_kv = attn_w // HEAD_DIM, kv_w // HEAD_DIM
    slopes = _slopes(n_heads)
    q_spec, k_specs, v_specs = _attn_specs(attn_w, kv_w)

    def body(q_ref, kp, kc, km, vp, vc, vm, gq_ref, gk_ref, sk_ref, o_ref):
        mask, dist = _attn_mask_dist(pl.program_id(0))
        for kh in range(n_kv):
            cs = slice(kh * HEAD_DIM, (kh + 1) * HEAD_DIM)
            kn, _ = _head_norm(jnp.concatenate([kp[:, cs], kc[:, cs], km[:, cs]], axis=0), gk_ref[...])
            vcat = jnp.concatenate([vp[:, cs], vc[:, cs], vm[:, cs]], axis=0).astype(BF16)
            for g in range(KV_GROUP):
                h = kh * KV_GROUP + g
                hs = slice(h * HEAD_DIM, (h + 1) * HEAD_DIM)
                qn, _ = _head_norm(q_ref[:, hs], gq_ref[...])
                p, _ = _scores(qn, kn, slopes[h], sk_ref[0:1, h:h + 1], mask, dist)
                o_ref[:, hs] = jnp.dot(p.astype(BF16), vcat, preferred_element_type=F32)

    small = lambda w: pl.BlockSpec((1, w), lambda n: (0, 0))
    return _pcall(body, name="attn_fwd", grid=(Lp // BLOCK,),
                  in_specs=[q_spec] + k_specs + v_specs + [small(HEAD_DIM), small(HEAD_DIM), small(n_heads)],
                  out_specs=pl.BlockSpec((BLOCK, attn_w), lambda n: (n, 0)), out_shape=_sds((Lp, attn_w), F32),
                  compiler_params=_params(("parallel",)))(proj, proj, proj, proj, proj, proj, proj, gq, gk, sinks)


def _attn_bwd(proj, attn, dattn, gq, gk, sinks, attn_w, kv_w):
    Lp = proj.shape[0]
    n_heads, n_kv = attn_w // HEAD_DIM, kv_w // HEAD_DIM
    slopes = _slopes(n_heads)
    q_spec, k_specs, v_specs = _attn_specs(attn_w, kv_w)
    scale = 1.0 / math.sqrt(HEAD_DIM)
    tn_dims = (((0,), (0,)), ((), ()))

    def body(q_ref, kp, kc, km, vp, vc, vm, o_ref, do_ref, gq_ref, gk_ref, sk_ref, dq_ref, dk_ref, dv_ref, dgq_ref, dsk_ref):
        n = pl.program_id(0)

        @pl.when(n == 0)
        def _():
            dk_ref[...] = jnp.zeros_like(dk_ref)
            dv_ref[...] = jnp.zeros_like(dv_ref)
            dgq_ref[...] = jnp.zeros_like(dgq_ref)
            dsk_ref[...] = jnp.zeros_like(dsk_ref)

        mask, dist = _attn_mask_dist(n)
        lane = lax.broadcasted_iota(jnp.int32, (1, n_heads), 1)
        rows_prev = pl.ds(pl.multiple_of(jnp.maximum(n - 1, 0) * BLOCK, BLOCK), BLOCK)
        rows_cur = pl.ds(pl.multiple_of(n * BLOCK, BLOCK), BLOCK)
        rows_meta = pl.ds(0, BLOCK)
        dgq = jnp.zeros((1, HEAD_DIM), F32)
        dsk = jnp.zeros((1, n_heads), F32)
        for kh in range(n_kv):
            cs = slice(kh * HEAD_DIM, (kh + 1) * HEAD_DIM)
            kn, _ = _head_norm(jnp.concatenate([kp[:, cs], kc[:, cs], km[:, cs]], axis=0), gk_ref[...])
            kn16 = kn.astype(BF16)
            vcat = jnp.concatenate([vp[:, cs], vc[:, cs], vm[:, cs]], axis=0).astype(BF16)
            dkn = jnp.zeros((3 * BLOCK, HEAD_DIM), F32)
            dvc = jnp.zeros((3 * BLOCK, HEAD_DIM), F32)
            for g in range(KV_GROUP):
                h = kh * KV_GROUP + g
                hs = slice(h * HEAD_DIM, (h + 1) * HEAD_DIM)
                q = q_ref[:, hs]
                qn, rq = _head_norm(q, gq_ref[...])
                p, ps = _scores(qn, kn, slopes[h], sk_ref[0:1, h:h + 1], mask, dist)
                do = do_ref[:, hs]
                dd = jnp.sum(do * o_ref[:, hs], axis=-1, keepdims=True)
                do16 = do.astype(BF16)
                dp = lax.dot_general(do16, vcat, (((1,), (1,)), ((), ())), preferred_element_type=F32)
                ds16 = (p * (dp - dd)).astype(BF16)
                dsk = dsk + jnp.where(lane == h, jnp.sum(-ps * dd), 0.0)
                dqn = jnp.dot(ds16, kn16, preferred_element_type=F32) * scale
                dkn = dkn + lax.dot_general(ds16, qn.astype(BF16), tn_dims, preferred_element_type=F32) * scale
                dvc = dvc + lax.dot_general(p.astype(BF16), do16, tn_dims, preferred_element_type=F32)
                xh = q * rq
                dgq = dgq + jnp.sum(dqn * xh, axis=0, keepdims=True)
                dyg = dqn * gq_ref[...]
                dq_ref[:, hs] = rq * (dyg - xh * jnp.mean(dyg * xh, axis=-1, keepdims=True))
            for part, rows in enumerate((rows_prev, rows_cur, rows_meta)):
                ps_ = slice(part * BLOCK, (part + 1) * BLOCK)
                dk_ref[rows, cs] += dkn[ps_]
                dv_ref[rows, cs] += dvc[ps_]
        dgq_ref[...] += dgq
        dsk_ref[...] += dsk

    small = lambda w: pl.BlockSpec((1, w), lambda n: (0, 0))
    blk = pl.BlockSpec((BLOCK, attn_w), lambda n: (n, 0))
    whole = pl.BlockSpec((Lp, kv_w), lambda n: (0, 0))
    return _pcall(body, name="attn_bwd", grid=(Lp // BLOCK,),
                  in_specs=[q_spec] + k_specs + v_specs + [blk, blk, small(HEAD_DIM), small(HEAD_DIM), small(n_heads)],
                  out_specs=[blk, whole, whole, small(HEAD_DIM), small(n_heads)],
                  out_shape=[_sds((Lp, attn_w), F32), _sds((Lp, kv_w), F32), _sds((Lp, kv_w), F32),
                             _sds((1, HEAD_DIM), F32), _sds((1, n_heads), F32)],
                  compiler_params=_params(("arbitrary",)))(proj, proj, proj, proj, proj, proj, proj, attn, dattn, gq, gk, sinks)


def _knorm_bwd(proj, dkn, gk, attn_w, kv_w):
    Lp = proj.shape[0]
    n_kv = kv_w // HEAD_DIM
    tr = _pick(Lp, (384, 256, 128))

    def body(k_ref, d_ref, g_ref, dk_ref, dg_ref):
        @pl.when(pl.program_id(0) == 0)
        def _():
            dg_ref[...] = jnp.zeros_like(dg_ref)

        dg = jnp.zeros((1, HEAD_DIM), F32)
        for kh in range(n_kv):
            cs = slice(kh * HEAD_DIM, (kh + 1) * HEAD_DIM)
            x = k_ref[:, cs]
            d = d_ref[:, cs]
            r = lax.rsqrt(jnp.mean(x * x, axis=-1, keepdims=True) + NORM_EPS)
            xh = x * r
            dg = dg + jnp.sum(d * xh, axis=0, keepdims=True)
            dyg = d * g_ref[...]
            dk_ref[:, cs] = r * (dyg - xh * jnp.mean(dyg * xh, axis=-1, keepdims=True))
        dg_ref[...] += dg

    return _pcall(body, name="knorm_bwd", grid=(Lp // tr,),
                  in_specs=[pl.BlockSpec((tr, kv_w), lambda i: (i, attn_w // kv_w)), pl.BlockSpec((tr, kv_w), lambda i: (i, 0)),
                            pl.BlockSpec((1, HEAD_DIM), lambda i: (0, 0))],
                  out_specs=[pl.BlockSpec((tr, kv_w), lambda i: (i, 0)), pl.BlockSpec((1, HEAD_DIM), lambda i: (0, 0))],
                  out_shape=[_sds((Lp, kv_w), F32), _sds((1, HEAD_DIM), F32)],
                  compiler_params=_params(("arbitrary",)))(proj, dkn, gk)


def _ssm_bbar(lr, li, ls, br, bi):
    def fn(lr, li, ls, br, bi):
        fr, fi = _zoh_factor(lr, li, ls)
        return fr * br - fi * bi, fr * bi + fi * br

    return _ew("ssm_bbar", fn, [lr, li, ls, br, bi], [F32, F32])


def _lam_bar(lr, li, ls):
    dl = jnp.exp(ls)
    e = jnp.exp(lr * dl)
    return e * jnp.cos(li * dl), e * jnp.sin(li * dl), dl


def _zoh_factor(lr, li, ls):
    ar, ai, _ = _lam_bar(lr, li, ls)
    n2 = lr * lr + li * li
    ivr, ivi = lr / n2, -li / n2
    return (ar - 1.0) * ivr - ai * ivi, (ar - 1.0) * ivi + ai * ivr


SCAN_SHIFTS = (1, 2, 4)


def _ssm_tables(lr, li, ls):
    Wx = lr.shape[1]

    def body(lr_ref, li_ref, ls_ref, tf_ref, tr_ref):
        dl = jnp.exp(ls_ref[...])
        zr, zi = lr_ref[...] * dl, li_ref[...] * dl
        row = lax.broadcasted_iota(jnp.int32, (8, Wx), 0)

        def power(kf):
            e = jnp.exp(kf * zr)
            return e * jnp.cos(kf * zi), e * jnp.sin(kf * zi)

        for ref, rev in ((tf_ref, False), (tr_ref, True)):
            sgn = -1.0 if rev else 1.0
            for k, d in enumerate(SCAN_SHIFTS):
                ar, ai = power(jnp.full((8, Wx), float(d), F32))
                keep = (row < 8 - d) if rev else (row >= d)
                ref[k] = jnp.where(keep, ar, 0.0)
                ref[4 + k] = jnp.where(keep, sgn * ai, 0.0)
            pr, pi = power(((8 - row) if rev else (row + 1)).astype(F32))
            ref[3] = pr
            ref[7] = sgn * pi

    full = pl.BlockSpec((1, Wx), lambda: (0, 0))
    tab = pl.BlockSpec((8, 8, Wx), lambda: (0, 0, 0))
    return _pcall(body, name="ssm_tables", in_specs=[full] * 3, out_specs=[tab, tab],
                  out_shape=[_sds((8, 8, Wx), F32)] * 2,
                  compiler_params=pltpu.CompilerParams(vmem_limit_bytes=V7X_VMEM_LIMIT_BYTES))(lr, li, ls)


def _scan(name, br, bi, tab, reverse, states=None):
    L, Wx = br.shape
    TB = _pick(L, (384, 256, 128))
    CW = _pick(Wx, (512, 256, 128))
    nT, nG = L // TB, TB // 8

    def body(*refs):
        if reverse:
            br_ref, bi_ref, xr_ref, xi_ref, tab_ref, or_ref, oi_ref, s1_ref, s2_ref, cr_ref, ci_ref = refs
        else:
            br_ref, bi_ref, tab_ref, or_ref, oi_ref, cr_ref, ci_ref = refs

        @pl.when(pl.program_id(1) == 0)
        def _():
            cr_ref[...] = jnp.zeros_like(cr_ref)
            ci_ref[...] = jnp.zeros_like(ci_ref)
            if reverse:
                s1_ref[...] = jnp.zeros_like(s1_ref)
                s2_ref[...] = jnp.zeros_like(s2_ref)

        def step(q, carry):
            cr, ci = carry[0], carry[1]
            g = (nG - 1 - q) if reverse else q
            rows = pl.ds(pl.multiple_of(g * 8, 8), 8)
            b_r, b_i = br_ref[rows, :], bi_ref[rows, :]
            sr, si = b_r, b_i
            for k, d in enumerate(SCAN_SHIFTS):
                mr, mi = tab_ref[k], tab_ref[4 + k]
                sh = (8 - d) if reverse else d
                pr, pi = pltpu.roll(sr, sh, 0), pltpu.roll(si, sh, 0)
                sr, si = sr + mr * pr - mi * pi, si + mr * pi + mi * pr
            pwr, pwi = tab_ref[3], tab_ref[7]
            xr = sr + pwr * cr - pwi * ci
            xi = si + pwr * ci + pwi * cr
            or_ref[rows, :] = xr
            oi_ref[rows, :] = xi
            row = 0 if reverse else 7
            out = (jnp.broadcast_to(xr[row:row + 1, :], xr.shape), jnp.broadcast_to(xi[row:row + 1, :], xi.shape))
            if reverse:
                hr, hi = xr - b_r, xi - b_i
                st_r, st_i = xr_ref[rows, :], xi_ref[rows, :]
                out = out + (carry[2] + hr * st_r + hi * st_i, carry[3] + hi * st_r - hr * st_i)
            return out

        init = (cr_ref[...], ci_ref[...])
        if reverse:
            init = init + (jnp.zeros((8, CW), F32), jnp.zeros((8, CW), F32))
        fin = lax.fori_loop(0, nG, step, init)
        cr_ref[...] = fin[0]
        ci_ref[...] = fin[1]
        if reverse:
            s1_ref[...] += fin[2]
            s2_ref[...] += fin[3]

    tmap = (lambda j, t: (nT - 1 - t, j)) if reverse else (lambda j, t: (t, j))
    blk = pl.BlockSpec((TB, CW), tmap)
    tab_spec = pl.BlockSpec((8, 8, CW), lambda j, t: (0, 0, j))
    sum_spec = pl.BlockSpec((8, CW), lambda j, t: (0, j))
    ins = [br, bi] + (list(states) if reverse else []) + [tab]
    in_specs = [blk, blk] + ([blk, blk] if reverse else []) + [tab_spec]
    out_specs = [blk, blk] + ([sum_spec, sum_spec] if reverse else [])
    out_shape = [_sds((L, Wx), F32)] * 2 + ([_sds((8, Wx), F32)] * 2 if reverse else [])
    return _pcall(body, name=name, grid=(Wx // CW, nT), in_specs=in_specs, out_specs=out_specs, out_shape=out_shape,
                  scratch_shapes=[pltpu.VMEM((8, CW), F32), pltpu.VMEM((8, CW), F32)],
                  compiler_params=_params(("parallel", "arbitrary")))(*ins)


def _row_tile(L):
    return _pick(L, (1408, 704, 384, 128))


def _blockproj(name, src, off, w_r, w_i):
    L = src.shape[0]
    T = w_r.shape[0]
    tm = _row_tile(L)

    def body(s_ref, wr_ref, wi_ref, or_ref, oi_ref):
        s = s_ref[...].astype(BF16)
        or_ref[...] = jnp.dot(s, wr_ref[...], preferred_element_type=F32)
        oi_ref[...] = jnp.dot(s, wi_ref[...], preferred_element_type=F32)

    w_spec = pl.BlockSpec((None, LANES, LANES), lambda i, t: (t, 0, 0))
    o_spec = pl.BlockSpec((tm, LANES), lambda i, t: (i, t))
    return _pcall(body, name=name, grid=(L // tm, T),
                  in_specs=[pl.BlockSpec((tm, LANES), lambda i, t: (i, off + t // 4)), w_spec, w_spec],
                  out_specs=[o_spec, o_spec], out_shape=[_sds((L, T * LANES), F32)] * 2,
                  compiler_params=_params(("parallel", "arbitrary")))(src, w_r, w_i)


def _blockproj_grad(name, src, off, gr, gi):
    L = src.shape[0]
    T = gr.shape[1] // LANES
    tm = _row_tile(L)
    tn_dims = (((0,), (0,)), ((), ()))

    def body(s_ref, gr_ref, gi_ref, or_ref, oi_ref):
        @pl.when(pl.program_id(1) == 0)
        def _():
            or_ref[...] = jnp.zeros_like(or_ref)
            oi_ref[...] = jnp.zeros_like(oi_ref)

        s = s_ref[...].astype(BF16)
        or_ref[...] += lax.dot_general(s, gr_ref[...].astype(BF16), tn_dims, preferred_element_type=F32)
        oi_ref[...] += lax.dot_general(s, gi_ref[...].astype(BF16), tn_dims, preferred_element_type=F32)

    g_spec = pl.BlockSpec((tm, LANES), lambda t, i: (i, t))
    o_spec = pl.BlockSpec((None, LANES, LANES), lambda t, i: (t, 0, 0))
    return _pcall(body, name=name, grid=(T, L // tm),
                  in_specs=[pl.BlockSpec((tm, LANES), lambda t, i: (i, off + t // 4)), g_spec, g_spec],
                  out_specs=[o_spec, o_spec], out_shape=[_sds((T, LANES, LANES), F32)] * 2,
                  compiler_params=_params(("parallel", "arbitrary")))(src, gr, gi)


def _gelu(y):
    k = math.sqrt(2.0 / math.pi)
    return 0.5 * y * (1.0 + jnp.tanh(k * (y + 0.044715 * (y * y * y))))


def _gelu_grad(y):
    k = math.sqrt(2.0 / math.pi)
    t = jnp.tanh(k * (y + 0.044715 * (y * y * y)))
    return 0.5 * (1.0 + t) + 0.5 * y * (1.0 - t * t) * (k * (1.0 + 3 * 0.044715 * (y * y)))


def _ssm_out(xr, xi, w_r, w_i, proj, u_off, dvec):
    L = xr.shape[0]
    J = w_r.shape[0]
    SW = w_r.shape[1]
    tm = _row_tile(L)

    def body(xr_ref, xi_ref, wr_ref, wi_ref, u_ref, d_ref, y_ref, gl_ref):
        acc = jnp.dot(xr_ref[...].astype(BF16), wr_ref[...], preferred_element_type=F32)
        acc += jnp.dot(xi_ref[...].astype(BF16), wi_ref[...], preferred_element_type=F32)
        y = acc + d_ref[...] * u_ref[...]
        y_ref[...] = y
        gl_ref[...] = _gelu(y)

    x_spec = pl.BlockSpec((tm, SW), lambda j, i: (i, j))
    w_spec = pl.BlockSpec((None, SW, LANES), lambda j, i: (j, 0, 0))
    o_spec = pl.BlockSpec((tm, LANES), lambda j, i: (i, j))
    return _pcall(body, name="ssm_out", grid=(J, L // tm),
                  in_specs=[x_spec, x_spec, w_spec, w_spec, pl.BlockSpec((tm, LANES), lambda j, i: (i, u_off + j)),
                            pl.BlockSpec((1, LANES), lambda j, i: (0, j))],
                  out_specs=[o_spec, o_spec], out_shape=[_sds((L, J * LANES), F32)] * 2,
                  compiler_params=_params(("parallel", "parallel")))(xr, xi, w_r, w_i, proj, dvec)


def _ssm_du(gr, gi, w_r, w_i, dy, proj, u_off, dvec):
    L = gr.shape[0]
    J = w_r.shape[0]
    SW = w_r.shape[1]
    tm = _row_tile(L)

    def body(gr_ref, gi_ref, wr_ref, wi_ref, dy_ref, u_ref, d_ref, du_ref, dd_ref):
        i = pl.program_id(1)

        @pl.when(i == 0)
        def _():
            dd_ref[...] = jnp.zeros_like(dd_ref)

        acc = jnp.dot(gr_ref[...].astype(BF16), wr_ref[...], preferred_element_type=F32)
        acc += jnp.dot(gi_ref[...].astype(BF16), wi_ref[...], preferred_element_type=F32)
        dy = dy_ref[...]
        row = lax.broadcasted_iota(jnp.int32, (tm, LANES), 0) + i * tm
        du_ref[...] = jnp.where(row >= PAD, acc + d_ref[...] * dy, 0.0)
        dd_ref[...] += jnp.sum(dy * u_ref[...], axis=0, keepdims=True)

    x_spec = pl.BlockSpec((tm, SW), lambda j, i: (i, j))
    w_spec = pl.BlockSpec((None, SW, LANES), lambda j, i: (j, 0, 0))
    o_spec = pl.BlockSpec((tm, LANES), lambda j, i: (i, j))
    vec = pl.BlockSpec((1, LANES), lambda j, i: (0, j))
    return _pcall(body, name="ssm_du", grid=(J, L // tm),
                  in_specs=[x_spec, x_spec, w_spec, w_spec, o_spec, pl.BlockSpec((tm, LANES), lambda j, i: (i, u_off + j)), vec],
                  out_specs=[o_spec, vec], out_shape=[_sds((L, J * LANES), F32), _sds((1, J * LANES), F32)],
                  compiler_params=_params(("parallel", "arbitrary")))(gr, gi, w_r, w_i, dy, proj, dvec)


def _ssm_dc(xr, xi, dy, SW):
    L = xr.shape[0]
    J = dy.shape[1] // LANES
    tm = _row_tile(L)
    tn_dims = (((0,), (0,)), ((), ()))

    def body(xr_ref, xi_ref, dy_ref, or_ref, oi_ref):
        @pl.when(pl.program_id(1) == 0)
        def _():
            or_ref[...] = jnp.zeros_like(or_ref)
            oi_ref[...] = jnp.zeros_like(oi_ref)

        d = dy_ref[...].astype(BF16)
        or_ref[...] += lax.dot_general(xr_ref[...].astype(BF16), d, tn_dims, preferred_element_type=F32)
        oi_ref[...] += lax.dot_general(xi_ref[...].astype(BF16), d, tn_dims, preferred_element_type=F32)

    x_spec = pl.BlockSpec((tm, SW), lambda j, i: (i, j))
    o_spec = pl.BlockSpec((None, SW, LANES), lambda j, i: (j, 0, 0))
    return _pcall(body, name="ssm_dc", grid=(J, L // tm),
                  in_specs=[x_spec, x_spec, pl.BlockSpec((tm, LANES), lambda j, i: (i, j))],
                  out_specs=[o_spec, o_spec], out_shape=[_sds((J, SW, LANES), F32)] * 2,
                  compiler_params=_params(("parallel", "arbitrary")))(xr, xi, dy)


def _glu_dz(ds, gl, z):
    L, W = ds.shape
    tr = _pick(L, (384, 256, 128))

    def body(ds_ref, gl_ref, z_ref, dz_ref, db_ref):
        @pl.when(pl.program_id(0) == 0)
        def _():
            db_ref[...] = jnp.zeros_like(db_ref)

        sg = jax.nn.sigmoid(z_ref[...])
        dz = ds_ref[...] * gl_ref[...] * (sg * (1.0 - sg))
        dz_ref[...] = dz.astype(BF16)
        db_ref[...] += jnp.sum(dz, axis=0, keepdims=True)

    spec = pl.BlockSpec((tr, W), lambda i: (i, 0))
    vec = pl.BlockSpec((1, W), lambda i: (0, 0))
    return _pcall(body, name="glu_dz", grid=(L // tr,), in_specs=[spec] * 3, out_specs=[spec, vec],
                  out_shape=[_sds((L, W), BF16), _sds((1, W), F32)], compiler_params=_params(("arbitrary",)))(ds, gl, z)


def _ssm_param_bwd_flat(lr, li, ls, br, bi, dbbr, dbbi):
    def seg_sum(x):
        for s in (8, 4, 2, 1):
            x = x + pltpu.roll(x, LANES - s, 1)
        return x

    def fn(lr, li, ls, br, bi, dbbr, dbbi):
        fr, fi = _zoh_factor(lr, li, ls)
        return (fr * dbbr + fi * dbbi, fr * dbbi - fi * dbbr,
                seg_sum(br * dbbr + bi * dbbi), seg_sum(br * dbbi - bi * dbbr))

    return _ew("ssm_param_bwd_flat", fn, [lr, li, ls, br, bi, dbbr, dbbi], [F32] * 4)


def _ssm_param_bwd(lr, li, ls, dfr, dfi, s1, s2):
    G, P = lr.shape

    def body(lr_ref, li_ref, ls_ref, dfr_ref, dfi_ref, s1_ref, s2_ref, dlr_ref, dli_ref, dls_ref):
        lr, li = lr_ref[...], li_ref[...]
        ar, ai, dl = _lam_bar(lr, li, ls_ref[...])
        sr, si = s1_ref[0], s2_ref[0]
        for k in range(1, 8):
            sr = sr + s1_ref[k]
            si = si + s2_ref[k]
        a2 = ar * ar + ai * ai
        gar, gai = (sr * ar - si * ai) / a2, (sr * ai + si * ar) / a2
        n2 = lr * lr + li * li
        ivr, ivi = lr / n2, -li / n2
        fr = (ar - 1.0) * ivr - ai * ivi
        fi = (ar - 1.0) * ivi + ai * ivr
        dfr, dfi = dfr_ref[...], dfi_ref[...]
        gar = gar + ivr * dfr + ivi * dfi
        gai = gai + ivr * dfi - ivi * dfr
        wr, wi = -(fr * ivr - fi * ivi), -(fr * ivi + fi * ivr)
        glr, gli = wr * dfr + wi * dfi, wr * dfi - wi * dfr
        gzr, gzi = ar * gar + ai * gai, ar * gai - ai * gar
        dlr_ref[...] = glr + dl * gzr
        dli_ref[...] = gli + dl * gzi
        dls_ref[...] = dl * jnp.sum(lr * gzr + li * gzi, axis=-1, keepdims=True)

    m = pl.BlockSpec((G, P), lambda: (0, 0))
    v = pl.BlockSpec((G, 1), lambda: (0, 0))
    s = pl.BlockSpec((8, G, P), lambda: (0, 0, 0))
    return _pcall(body, name="ssm_param_bwd", in_specs=[m, m, v, m, m, s, s], out_specs=[m, m, v],
                  out_shape=[_sds((G, P), F32), _sds((G, P), F32), _sds((G, 1), F32)])(lr, li, ls, dfr, dfi, s1, s2)


def _tile_mask(G):
    T = G // 2
    e = np.zeros((T, 8, 1, 2, 1), np.float32)
    for t in range(T):
        for c in range(2):
            e[t, (2 * t + c) % 8, 0, c, 0] = 1.0
    return e


def _tile_w(arr):
    G = arr.shape[0]
    a = arr.reshape(G // 2, 1, 2, STATE, GROUP_CH).transpose(0, 1, 4, 2, 3)
    return (a * _tile_mask(G)).reshape(G // 2, LANES, LANES).astype(BF16)


def _tile_w_grad(dw):
    G = dw.shape[0] * 2
    d = dw.reshape(G // 2, 8, GROUP_CH, 2, STATE) * _tile_mask(G)
    return d.sum(axis=1).transpose(0, 2, 3, 1).reshape(G, STATE, GROUP_CH)


def _slab_w(arr):
    G = arr.shape[0]
    a = arr.reshape(G // 8, 8, STATE, 1, GROUP_CH)
    eye = np.eye(8, dtype=np.float32).reshape(1, 8, 1, 8, 1)
    return (a * eye).reshape(G // 8, 8 * STATE, LANES).astype(BF16)


def _slab_w_grad(dw):
    J = dw.shape[0]
    eye = np.eye(8, dtype=np.float32).reshape(1, 8, 1, 8, 1)
    return (dw.reshape(J, 8, STATE, 8, GROUP_CH) * eye).sum(axis=3).reshape(J * 8, STATE, GROUP_CH)


def _exchange(name, ins, out_sds, remote, local, aliases=None):
    n_in, n_out, n_r, n_l = len(ins), len(out_sds), len(remote), len(local)

    def body(*refs):
        in_refs, out_refs = refs[:n_in], refs[n_in:n_in + n_out]
        send_sems, recv_sems, local_sems = refs[n_in + n_out:]
        x, y, c = lax.axis_index("x"), lax.axis_index("y"), lax.axis_index("c")

        def place(px, py, pc):
            return dict(x=px, y=py, c=pc, chip=2 * px + py)

        def flip(mask):
            mx, my, mc = mask
            return ((1 - x) if mx else x, (1 - y) if my else y, (1 - c) if mc else c)

        me = place(x, y, c)
        sends = []
        for k, (ii, src, oi, dst, mask) in enumerate(remote):
            cp = pltpu.make_async_remote_copy(src_ref=src(in_refs[ii], me), dst_ref=dst(out_refs[oi], me),
                                              send_sem=send_sems.at[k], recv_sem=recv_sems.at[k],
                                              device_id=flip(mask), device_id_type=MESH)
            cp.start()
            sends.append(cp)
        locals_ = []
        for k, (ii, src, oi, dst) in enumerate(local):
            cp = pltpu.make_async_copy(src(in_refs[ii], me), dst(out_refs[oi], me), local_sems.at[k])
            cp.start()
            locals_.append(cp)
        for k, (ii, src, oi, dst, mask) in enumerate(remote):
            sends[k].wait_send()
            peer = flip(mask)
            pltpu.make_async_remote_copy(src_ref=src(in_refs[ii], me), dst_ref=dst(out_refs[oi], place(*peer)),
                                         send_sem=send_sems.at[k], recv_sem=recv_sems.at[k],
                                         device_id=peer, device_id_type=MESH).wait_recv()
        for cp in locals_:
            cp.wait()

    any_spec = pl.BlockSpec(memory_space=pl.ANY)
    return _pcall(body, name=name, in_specs=[any_spec] * n_in, out_specs=[any_spec] * n_out, out_shape=list(out_sds),
                  input_output_aliases=aliases or {},
                  scratch_shapes=[pltpu.SemaphoreType.DMA((n_r,)), pltpu.SemaphoreType.DMA((n_r,)),
                                  pltpu.SemaphoreType.DMA((max(n_l, 1),))])(*ins)


def _mesh_place():
    x, y, c = lax.axis_index("x"), lax.axis_index("y"), lax.axis_index("c")

    def place(px, py, pc):
        return dict(x=px, y=py, c=pc, chip=2 * px + py)

    def flip(mask):
        mx, my, mc = mask
        return ((1 - x) if mx else x, (1 - y) if my else y, (1 - c) if mc else c)

    return place(x, y, c), place, flip


_HBM = pl.BlockSpec(memory_space=pltpu.HBM)
_SEM = pl.BlockSpec(memory_space=pltpu.SEMAPHORE)
_EFFECT = pltpu.SideEffectType.DATAFLOW_SIDE_EFFECTING


def _split_start(name, bufs, groups):
    n, ng = len(bufs), len(groups)

    def body(*refs):
        in_refs, sems, token = refs[:n], refs[n:n + 2 * ng], refs[-1]
        me, _, flip = _mesh_place()
        for g, copies in enumerate(groups):
            for k, (si, src, di, dst, mask) in enumerate(copies):
                pltpu.make_async_remote_copy(src_ref=src(in_refs[si], me), dst_ref=dst(in_refs[di], me),
                                             send_sem=sems[2 * g].at[k], recv_sem=sems[2 * g + 1].at[k],
                                             device_id=flip(mask), device_id_type=MESH).start()
        token[...] = jnp.zeros_like(token)

    outs = _pcall(body, name=name,
                  out_shape=(*[pltpu.SemaphoreType.DMA((len(g),)) for g in groups for _ in range(2)],
                             *[pltpu.HBM(b.shape, b.dtype) for b in bufs], _sds((8, LANES), F32)),
                  in_specs=[_HBM] * n, out_specs=(*[_SEM] * (2 * ng), *[_HBM] * n, pl.BlockSpec(memory_space=pltpu.VMEM)),
                  input_output_aliases={i: 2 * ng + i for i in range(n)},
                  compiler_params=pltpu.CompilerParams(has_side_effects=_EFFECT),
                  )(*[pltpu.with_memory_space_constraint(b, pltpu.HBM) for b in bufs])
    return [(outs[2 * g], outs[2 * g + 1]) for g in range(ng)], list(outs[2 * ng:2 * ng + n]), outs[-1]


def _split_wait(name, bufs, sems, after, remote):
    n = len(bufs)
    send_sems, recv_sems = sems

    def body(*refs):
        in_refs, ssem, rsem = refs[:n], refs[n], refs[n + 1]
        me, place, flip = _mesh_place()
        for k, (si, src, di, dst, mask) in enumerate(remote):
            peer = flip(mask)
            cp = pltpu.make_async_remote_copy(src_ref=src(in_refs[si], me), dst_ref=dst(in_refs[di], place(*peer)),
                                              send_sem=ssem.at[k], recv_sem=rsem.at[k], device_id=peer, device_id_type=MESH)
            cp.wait_send()
            cp.wait_recv()

    return list(_pcall(body, name=name, out_shape=tuple(pltpu.HBM(b.shape, b.dtype) for b in bufs),
                       in_specs=[_HBM] * n + [_SEM, _SEM, pl.BlockSpec(memory_space=pl.ANY)], out_specs=tuple([_HBM] * n),
                       input_output_aliases={i: i for i in range(n)},
                       compiler_params=pltpu.CompilerParams(has_side_effects=_EFFECT))(*bufs, send_sems, recv_sems, after))


CHIP_MASKS = ((0, 1, 0), (1, 0, 0), (1, 1, 0))
SIBLING = (0, 0, 1)


def _whole(ref, p):
    return ref


def _all_gather(name, shards, col_sharded):
    def dst_view(col):
        def view(ref, p):
            r, cdim = ref.shape[0] // (1 if col else N_CHIPS), ref.shape[1] // (N_CHIPS if col else 1)
            if col:
                return ref.at[:, pl.ds(pl.multiple_of(p["chip"] * cdim, LANES), cdim)]
            return ref.at[pl.ds(pl.multiple_of(p["chip"] * r, 8), r), :]
        return view

    out_sds = [_sds((s.shape[0], s.shape[1] * N_CHIPS) if col else (s.shape[0] * N_CHIPS, s.shape[1]), s.dtype)
               for s, col in zip(shards, col_sharded)]
    remote = [(a, _whole, a, dst_view(col), m) for a, col in enumerate(col_sharded) for m in CHIP_MASKS]
    local = [(a, _whole, a, dst_view(col)) for a, col in enumerate(col_sharded)]
    return _exchange(name, shards, out_sds, remote, local)


class _Place:
    def __getitem__(self, k):
        return lax.axis_index("c") if k == 0 else 2 * lax.axis_index("x") + lax.axis_index("y")


def _placed_call(body, name, grid, in_specs, out_specs, out_shape, sem, ins):
    def wrap(spec):
        return pl.BlockSpec(spec.block_shape, lambda *idx: spec.index_map(*idx, _Place()))

    outs = [wrap(s) for s in out_specs] if isinstance(out_specs, (list, tuple)) else wrap(out_specs)
    return _pcall(body, name=name, grid=grid, in_specs=[wrap(s) for s in in_specs], out_specs=outs, out_shape=out_shape,
                  compiler_params=_params(sem))(*ins)


def _rows_within(n, width, limit=512 * 1024):
    return _pick(n, tuple(t for t in (1024, 512, 256, 128, 64, 32, 16) if t * width <= limit) or (16,))


def _region_view(col):
    def view(ref, p):
        if col:
            cdim = ref.shape[1] // N_CHIPS
            return ref.at[:, pl.ds(pl.multiple_of(p["chip"] * cdim, LANES), cdim)]
        r = ref.shape[0] // N_CHIPS
        return ref.at[pl.ds(pl.multiple_of(p["chip"] * r, 16), r), :]
    return view


def _ag_place(name, w, layer, col, dtype):
    _, r, cdim = w.shape
    tr = _rows_within(r, cdim)
    nb = r // tr

    def body(w_ref, o_ref):
        o_ref[...] = w_ref[...].astype(dtype)

    if col:
        out_shape, out_spec = (r, N_CHIPS * cdim), pl.BlockSpec((tr, cdim), lambda i, pr: (i, pr[1]))
    else:
        out_shape, out_spec = (N_CHIPS * r, cdim), pl.BlockSpec((tr, cdim), lambda i, pr: (pr[1] * nb + i, 0))
    return _placed_call(body, name, (nb,), [pl.BlockSpec((None, tr, cdim), lambda i, pr: (layer, i, 0))], out_spec,
                        _sds(out_shape, dtype), ("parallel",), [w])


def _ag_copies(a, col):
    return [(a, _region_view(col), a, _region_view(col), m) for m in CHIP_MASKS]


def _rs_add2(name, g4, a4, out_dtype):
    J, _, h, C = g4.shape
    tr = _rows_within(h, C)

    def body(g_ref, a_ref, o_ref):
        o_ref[...] = (g_ref[...].astype(F32) + a_ref[...].astype(F32)).astype(o_ref.dtype)

    return _placed_call(body, name, (J, h // tr),
                        [pl.BlockSpec((None, None, tr, C), lambda j, i, pr: (j, pr[0], i, 0)),
                         pl.BlockSpec((None, None, tr, C), lambda j, i, pr: (j, 0, i, 0))],
                        pl.BlockSpec((None, tr, C), lambda j, i, pr: (j, i, 0)), _sds((J, h, C), out_dtype),
                        ("parallel", "parallel"), [g4, a4])


def _rs_add4(name, p3, landed, col):
    _, h, w = landed.shape
    tr = _rows_within(h, w)

    def body(p_ref, a_ref, b_ref, c_ref, o_ref):
        o_ref[...] = ((p_ref[...].astype(F32) + a_ref[...].astype(F32)) + b_ref[...].astype(F32)) + c_ref[...].astype(F32)

    own = (pl.BlockSpec((None, tr, w), lambda i, pr: (0, i, pr[1])) if col else pl.BlockSpec((None, tr, w), lambda i, pr: (pr[1], i, 0)))
    slot = lambda k: pl.BlockSpec((None, tr, w), lambda i, pr: (k, i, 0))
    return _placed_call(body, name, (h // tr,), [own, slot(0), slot(1), slot(2)], pl.BlockSpec((tr, w), lambda i, pr: (i, 0)),
                        _sds((h, w), F32), ("parallel",), [p3, landed, landed, landed])


def _rs_begin(tag, grads, col_sharded):
    n = len(grads)
    g4 = [g.reshape((1, 2, g.shape[0] // 2, g.shape[1]) if col else (N_CHIPS, 2, g.shape[0] // (2 * N_CHIPS), g.shape[1]))
          for g, col in zip(grads, col_sharded)]
    other_half = lambda ref, p: ref.at[:, pl.ds(1 - p["c"], 1)]
    theirs = _exchange("rs_sibling_w", g4, [_sds((g.shape[0], 1) + g.shape[2:], g.dtype) for g in g4],
                       [(a, other_half, a, _whole, SIBLING) for a in range(n)], [])
    chip_sum = [_rs_add2("rs_add2_w", g4[a], theirs[a], BF16) for a in range(n)]

    def send_view(col, mask):
        def view(ref, p):
            t = 2 * ((1 - p["x"]) if mask[0] else p["x"]) + ((1 - p["y"]) if mask[1] else p["y"])
            if col:
                sc = ref.shape[2] // N_CHIPS
                return ref.at[0, :, pl.ds(pl.multiple_of(t * sc, LANES), sc)]
            return ref.at[t]
        return view
    slot = lambda k: (lambda ref, p: ref.at[k])
    piece = [(s.shape[1], s.shape[2] // N_CHIPS if col else s.shape[2]) for s, col in zip(chip_sum, col_sharded)]
    landing = [lax.empty((len(CHIP_MASKS),) + s, BF16) for s in piece]
    copies = [(a, send_view(col_sharded[a], m), n + a, slot(k), m) for a in range(n) for k, m in enumerate(CHIP_MASKS)]
    (sems,), bufs, token = _split_start("rs_chips_start_" + tag, chip_sum + landing, [copies])
    return dict(sems=sems, bufs=bufs, copies=copies, token=token, col_sharded=col_sharded)


def _rs_finish(tag, st, after):
    col_sharded = st['col_sharded']
    n = len(col_sharded)
    bufs = _split_wait("rs_chips_wait_" + tag, st['bufs'], st['sems'], after, st['copies'])
    chip_sum, landed = bufs[:n], bufs[n:]
    mine = [_rs_add4("rs_add4_w", chip_sum[a], landed[a], col_sharded[a]) for a in range(n)]
    other = _exchange("rs_halves_w", mine, [_sds(m.shape, F32) for m in mine], [(a, _whole, a, _whole, SIBLING) for a in range(n)], [])
    return mine, other


def _adamw_big(name, mine, other, w, m, v):
    depth, R, C = w.shape
    h = R // 2
    tr = _pick(h, tuple(t for t in (512, 256, 128, 64, 32, 16, 8) if t * C <= 256 * 1024) or (8,))
    nb = h // tr

    def g_spec(kk, hh):
        def imap(l, s, i, pr):
            before = (l < kk) | ((l == kk) & (s < hh))
            return (jnp.where((l == kk) & (s == hh), i, jnp.where(before, 0, nb - 1)), 0)
        return pl.BlockSpec((tr, C), imap)

    st_spec = pl.BlockSpec((None, tr, C), lambda l, s, i, pr: (l, jnp.where(s == 0, pr[0], 1 - pr[0]) * nb + i, 0))

    def body(*refs):
        g_refs = refs[:2 * depth]
        w_ref, m_ref, v_ref, go_ref, d_ref, mo_ref, vo_ref = refs[2 * depth:]
        l, s = pl.program_id(0), pl.program_id(1)
        for kk in range(depth):
            for hh in range(2):
                @pl.when((l == kk) & (s == hh))
                def _(kk=kk, hh=hh):
                    g = g_refs[2 * kk + hh][...]
                    d, mn, vn = _adam_math(w_ref[...], g, m_ref[...], v_ref[...])
                    go_ref[...] = g
                    d_ref[...] = d
                    mo_ref[...] = mn
                    vo_ref[...] = vn

    gs, g_specs = [], []
    for kk in range(depth):
        gs += [mine[kk], other[kk]]
        g_specs += [g_spec(kk, 0), g_spec(kk, 1)]
    return _placed_call(body, name, (depth, 2, nb), g_specs + [st_spec] * 3, [st_spec] * 4, [_sds(w.shape, F32)] * 4,
                        ("arbitrary", "arbitrary", "arbitrary"), gs + [w, m, v])


def _piece_view(col, j, other):
    def view(ref, p):
        R, C = ref.shape
        cc = (1 - p["c"]) if other else p["c"]
        if col:
            hr, sc = R // 2, C // N_CHIPS
            return ref.at[pl.ds(pl.multiple_of(cc * hr, 16), hr), pl.ds(j * sc, sc)]
        hr = R // (2 * N_CHIPS)
        return ref.at[pl.ds(pl.multiple_of((2 * j + cc) * hr, 8), hr), :]
    return view


def _piece_shape(shape, col):
    R, C = shape
    return (R // 2, C // N_CHIPS) if col else (R // (2 * N_CHIPS), C)


def _reduce_scatter(tag, grads, col_sharded, wire_dtype):
    n = len(grads)
    shapes = [_piece_shape(g.shape, col) for g, col in zip(grads, col_sharded)]

    slot = lambda j: (lambda ref, p: ref.at[j])
    remote = [(a, _piece_view(col_sharded[a], j, True), a, slot(j), SIBLING) for a in range(n) for j in range(N_CHIPS)]
    local = [(a, _piece_view(col_sharded[a], j, False), n + a, slot(j)) for a in range(n) for j in range(N_CHIPS)]
    got = _exchange("rs_sibling_" + tag, grads, [_sds((N_CHIPS,) + s, g.dtype) for s, g in zip(shapes, grads)] * 2, remote, local)
    theirs, mine = got[:n], got[n:]
    chip_sum = [_ew("rs_add2_" + tag, lambda a, b: (a.astype(F32) + b.astype(F32),),
                    [m.reshape(-1, m.shape[-1]), t.reshape(-1, t.shape[-1])], [wire_dtype])[0].reshape(m.shape)
                for m, t in zip(mine, theirs)]

    def send_view(mask):
        return lambda ref, p: ref.at[2 * ((1 - p["x"]) if mask[0] else p["x"]) + ((1 - p["y"]) if mask[1] else p["y"])]
    remote = [(a, send_view(m), a, slot(k), m) for a in range(n) for k, m in enumerate(CHIP_MASKS)]
    local = [(a, lambda ref, p: ref.at[p["chip"]], n + a, _whole) for a in range(n)]
    got = _exchange("rs_chips_" + tag, chip_sum,
                    [_sds((len(CHIP_MASKS),) + s, wire_dtype) for s in shapes] + [_sds(s, wire_dtype) for s in shapes], remote, local)
    landed, own = got[:n], got[n:]
    half = [_ew("rs_add4_" + tag, lambda o, a, b, c: (((o.astype(F32) + a.astype(F32)) + b.astype(F32)) + c.astype(F32),),
                [o, l[0], l[1], l[2]], [F32])[0] for o, l in zip(own, landed)]

    def half_rows(ref, p):
        hr = ref.shape[0] // 2
        return ref.at[pl.ds(pl.multiple_of(p["c"] * hr, 8), hr), :]
    remote = [(a, _whole, a, half_rows, SIBLING) for a in range(n)]
    local = [(a, _whole, a, half_rows) for a in range(n)]
    return _exchange("rs_halves_" + tag, half, [_sds((2 * s[0], s[1]), F32) for s in shapes], remote, local)


def _ssm_prepare(p):
    lr, li, ls = p['ssm_lambda_re'], p['ssm_lambda_im'], p['ssm_log_step']
    G = lr.shape[0]
    flat = lambda a: a.reshape(-1, LANES)
    bc = lambda a: flat(jnp.broadcast_to(a, (G, STATE, GROUP_CH)))
    lr3, li3, ls3 = bc(lr[:, :, None]), bc(li[:, :, None]), bc(ls[:, None, None])
    bbr, bbi = _ssm_bbar(lr3, li3, ls3, flat(p['ssm_b_re']), flat(p['ssm_b_im']))
    bbr, bbi = bbr.reshape(G, STATE, GROUP_CH), bbi.reshape(G, STATE, GROUP_CH)
    row = lambda a: a.reshape(1, G * STATE)
    tf, tr = _ssm_tables(row(lr), row(li), row(jnp.broadcast_to(ls[:, None], (G, STATE))))
    cr = p['ssm_c_re'].transpose(0, 2, 1)
    ci = -p['ssm_c_im'].transpose(0, 2, 1)
    return dict(flat3=(lr3, li3, ls3), tf=tf, tr=tr,
                wb=(_tile_w(bbr), _tile_w(bbi)), wbT=(_slab_w(bbr), _slab_w(bbi)),
                wc=(_slab_w(cr), _slab_w(ci)), wcT=(_tile_w(cr), _tile_w(ci)))


def _layer_fwd(x, p, weight, dims):
    attn_w, kv_w, u_off = dims['attn_w'], dims['kv_w'], dims['u_off']
    s = _ssm_prepare(p)
    h = _rms_fwd("norm_mix", [x], [p['norm_mix_g']], BF16)
    w = {'w_in': weight('w_in', h)}
    proj, = _mm("mm_in", h, w['w_in'], 'nn', [F32])
    attn = _attn_fwd(proj, p['q_norm_g'], p['k_norm_g'], p['attn_sinks'], attn_w, kv_w)
    bur, bui = _blockproj("ssm_bu", proj, u_off, *s['wb'])
    xr, xi = _scan("ssm_scan_fwd", bur, bui, s['tf'], False)
    y, gl = _ssm_out(xr, xi, *s['wc'], proj, u_off, p['ssm_d'])
    w['w_glu'] = weight('w_glu', gl)
    ssm, z = _mm("mm_glu", gl, w['w_glu'], 'nn', [F32, F32], extras=[('row', p['b_glu']), ('tile', gl)],
                 epi=lambda acc, b, g: ((lambda zz: (g * jax.nn.sigmoid(zz), zz))(acc + b)))
    mix = _rms_fwd("norm_heads", [attn, ssm], [p['attn_out_g'], p['ssm_out_g']], BF16)
    w['w_out'] = weight('w_out', mix)
    x_mid, = _mm("mm_out", mix, w['w_out'], 'nn', [F32], extras=[('tile', x)], epi=lambda acc, r: (acc + r,))
    h2 = _rms_fwd("norm_mlp", [x_mid], [p['norm_mlp_g']], BF16)
    w['w_up'] = weight('w_up', h2)
    a, r = _mm("mm_up", h2, w['w_up'], 'nn', [F32, BF16],
               epi=lambda acc: (acc, jnp.square(jnp.maximum(acc, 0.0))))
    w['w_down'] = weight('w_down', r)
    x_out, = _mm("mm_down", r, w['w_down'], 'nn', [F32], extras=[('tile', x_mid)], epi=lambda acc, rr: (acc + rr,))
    saved = dict(x=x, h=h, proj=proj, attn=attn, xr=xr, xi=xi, y=y, gl=gl, z=z, ssm=ssm, mix=mix, x_mid=x_mid, h2=h2, a=a, r=r, s=s, w=w)
    return x_out, saved


def _layer_bwd(dx, sv, p, dims, reduce_grads):
    attn_w, kv_w, u_off = dims['attn_w'], dims['kv_w'], dims['u_off']
    s, w = sv['s'], sv['w']
    gb, gs = {}, {}
    da, = _mm("mm_down_dx", dx, w['w_down'], 'nt', [BF16], extras=[('tile', sv['a'])],
              epi=lambda acc, a: (acc * (2.0 * jnp.maximum(a, 0.0)),))
    gb['w_down'], = _mm("mm_down_dw", sv['r'], dx, 'tn', [BF16])
    dh2, = _mm("mm_up_dx", da, w['w_up'], 'nt', [F32])
    gb['w_up'], = _mm("mm_up_dw", sv['h2'], da, 'tn', [BF16])
    token = reduce_grads(('w_up', 'w_down'), [gb['w_up'], gb['w_down']])
    (dx_mid,), (gs['norm_mlp_g'],) = _rms_bwd("norm_mlp_bwd", [sv['x_mid']], [p['norm_mlp_g'] + token], dh2, resid=dx)
    dmix, = _mm("mm_out_dx", dx_mid, w['w_out'], 'nt', [F32])
    gb['w_out'], = _mm("mm_out_dw", sv['mix'], dx_mid, 'tn', [BF16])
    (dattn, dssm), (gs['attn_out_g'], gs['ssm_out_g']) = _rms_bwd(
        "norm_heads_bwd", [sv['attn'], sv['ssm']], [p['attn_out_g'], p['ssm_out_g']], dmix)
    dz, gs['b_glu'] = _glu_dz(dssm, sv['gl'], sv['z'])
    dy, = _mm("mm_glu_dx", dz, w['w_glu'], 'nt', [F32], extras=[('tile', dssm), ('tile', sv['z']), ('tile', sv['y'])],
              epi=lambda acc, ds, z, y: ((acc + ds * jax.nn.sigmoid(z)) * _gelu_grad(y),))
    gb['w_glu'], = _mm("mm_glu_dw", sv['gl'], dz, 'tn', [BF16])
    dxr, dxi = _blockproj("ssm_dstate", dy, 0, *s['wcT'])
    gxr, gxi, s1, s2 = _scan("ssm_scan_bwd", dxr, dxi, s['tr'], True, states=(sv['xr'], sv['xi']))
    du, gs['ssm_d'] = _ssm_du(gxr, gxi, *s['wbT'], dy, sv['proj'], u_off, p['ssm_d'])
    dwb_r, dwb_i = _blockproj_grad("ssm_dbbar", sv['proj'], u_off, gxr, gxi)
    dwc_r, dwc_i = _ssm_dc(sv['xr'], sv['xi'], dy, s['wc'][0].shape[1])
    gs['ssm_c_re'] = _slab_w_grad(dwc_r).transpose(0, 2, 1)
    gs['ssm_c_im'] = -_slab_w_grad(dwc_i).transpose(0, 2, 1)
    G = p['ssm_lambda_re'].shape[0]
    flat = lambda a_: a_.reshape(-1, LANES)
    dbr, dbi, qr, qi = _ssm_param_bwd_flat(*s['flat3'], flat(p['ssm_b_re']), flat(p['ssm_b_im']),
                                           flat(_tile_w_grad(dwb_r)), flat(_tile_w_grad(dwb_i)))
    gs['ssm_b_re'], gs['ssm_b_im'] = dbr.reshape(G, STATE, GROUP_CH), dbi.reshape(G, STATE, GROUP_CH)
    pick = lambda q: q[:, ::GROUP_CH].reshape(G, STATE)
    dlr, dli, dls = _ssm_param_bwd(p['ssm_lambda_re'], p['ssm_lambda_im'], p['ssm_log_step'][:, None], pick(qr), pick(qi),
                                   s1.reshape(8, G, STATE), s2.reshape(8, G, STATE))
    gs['ssm_lambda_re'], gs['ssm_lambda_im'], gs['ssm_log_step'] = dlr, dli, dls[:, 0]
    dq, dkn, dv, gs['q_norm_g'], gs['attn_sinks'] = _attn_bwd(sv['proj'], sv['attn'], dattn, p['q_norm_g'], p['k_norm_g'],
                                                               p['attn_sinks'], attn_w, kv_w)
    dk, gs['k_norm_g'] = _knorm_bwd(sv['proj'], dkn, p['k_norm_g'], attn_w, kv_w)
    dproj = jnp.concatenate([dq.astype(BF16), dk.astype(BF16), dv.astype(BF16), du.astype(BF16)], axis=1)
    dh, = _mm("mm_in_dx", dproj, w['w_in'], 'nt', [F32])
    gb['w_in'], = _mm("mm_in_dw", sv['h'], dproj, 'tn', [BF16])
    token = reduce_grads(('w_in', 'w_glu', 'w_out'), [gb['w_in'], gb['w_glu'], gb['w_out']])
    (dx_in,), (gs['norm_mix_g'],) = _rms_bwd("norm_mix_bwd", [sv['x']], [p['norm_mix_g'] + token], dh, resid=dx_mid)
    return dx_in, gs


PACK_COLS = 1024


def _pack(arrs, rows):
    flat = jnp.concatenate([a.reshape(-1).astype(F32) for a in arrs])
    return jnp.pad(flat, (0, rows * PACK_COLS - flat.shape[0])).reshape(rows, PACK_COLS)


def _unpack(packed, shapes):
    flat = packed.reshape(-1)
    out, off = [], 0
    for s in shapes:
        n = int(np.prod(s))
        out.append(flat[off:off + n].reshape(s))
        off += n
    return out


def _pack_rows(shapes, multiple):
    n = sum(int(np.prod(s)) for s in shapes)
    rows = -(-n // PACK_COLS)
    return -(-rows // multiple) * multiple


def kernel(x, meta_tokens, norm_mix_g, w_in, q_norm_g, k_norm_g, attn_sinks, ssm_lambda_re, ssm_lambda_im, ssm_log_step, ssm_b_re, ssm_b_im, ssm_c_re, ssm_c_im, ssm_d, w_glu, b_glu, attn_out_g, ssm_out_g, w_out, norm_mlp_g, w_up, w_down, loss_target, m_meta_tokens, m_norm_mix_g, m_w_in, m_q_norm_g, m_k_norm_g, m_attn_sinks, m_ssm_lambda_re, m_ssm_lambda_im, m_ssm_log_step, m_ssm_b_re, m_ssm_b_im, m_ssm_c_re, m_ssm_c_im, m_ssm_d, m_w_glu, m_b_glu, m_attn_out_g, m_ssm_out_g, m_w_out, m_norm_mlp_g, m_w_up, m_w_down, v_meta_tokens, v_norm_mix_g, v_w_in, v_q_norm_g, v_k_norm_g, v_attn_sinks, v_ssm_lambda_re, v_ssm_lambda_im, v_ssm_log_step, v_ssm_b_re, v_ssm_b_im, v_ssm_c_re, v_ssm_c_im, v_ssm_d, v_w_glu, v_b_glu, v_attn_out_g, v_ssm_out_g, v_w_out, v_norm_mlp_g, v_w_up, v_w_down):
    args = (meta_tokens, norm_mix_g, w_in, q_norm_g, k_norm_g, attn_sinks, ssm_lambda_re, ssm_lambda_im, ssm_log_step, ssm_b_re, ssm_b_im, ssm_c_re, ssm_c_im, ssm_d, w_glu, b_glu, attn_out_g, ssm_out_g, w_out, norm_mlp_g, w_up, w_down)
    ms = (m_meta_tokens, m_norm_mix_g, m_w_in, m_q_norm_g, m_k_norm_g, m_attn_sinks, m_ssm_lambda_re, m_ssm_lambda_im, m_ssm_log_step, m_ssm_b_re, m_ssm_b_im, m_ssm_c_re, m_ssm_c_im, m_ssm_d, m_w_glu, m_b_glu, m_attn_out_g, m_ssm_out_g, m_w_out, m_norm_mlp_g, m_w_up, m_w_down)
    vs = (v_meta_tokens, v_norm_mix_g, v_w_in, v_q_norm_g, v_k_norm_g, v_attn_sinks, v_ssm_lambda_re, v_ssm_lambda_im, v_ssm_log_step, v_ssm_b_re, v_ssm_b_im, v_ssm_c_re, v_ssm_c_im, v_ssm_d, v_w_glu, v_b_glu, v_attn_out_g, v_ssm_out_g, v_w_out, v_norm_mlp_g, v_w_up, v_w_down)
    W = dict(zip(WEIGHTS, args))
    M = dict(zip(WEIGHTS, ms))
    V = dict(zip(WEIGHTS, vs))
    depth = norm_mix_g.shape[0]
    seq, D = x.shape[1], x.shape[2]
    attn_w = D // 2
    kv_w = attn_w // KV_GROUP
    dims = dict(attn_w=attn_w, kv_w=kv_w, u_off=(attn_w + 2 * kv_w) // LANES)
    small_names = [n for n in WEIGHTS if n not in BIG and n != 'meta_tokens']
    chip = 2 * lax.axis_index("x") + lax.axis_index("y")

    gathers, started = [], jnp.zeros((), F32)
    for l in range(depth):
        placed = [_ag_place("ag_place_" + n, W[n], l, COL_SHARDED[n], BF16) for n in BIG]
        groups = [_ag_copies(a, COL_SHARDED[n]) for a, n in enumerate(BIG)]
        if l == 0:
            placed = [_ag_place("ag_place_meta", meta_tokens[None], 0, True, F32)] + placed
            groups = [_ag_copies(0, True)] + [_ag_copies(a + 1, COL_SHARDED[n]) for a, n in enumerate(BIG)]
        sems, bufs, token = _split_start("ag_start_%d" % l, placed, groups)
        gathers.append(dict(zip((['meta_tokens'] if l == 0 else []) + BIG, zip(sems, bufs))))
        started = started + token[0, 0]

    def gathered(l, n, after):
        sems, buf = gathers[l][n]
        return _split_wait("ag_wait_%d_%s" % (l, n), [buf], sems, after, _ag_copies(0, n == 'meta_tokens' or COL_SHARDED[n]))[0]

    h_res = jnp.concatenate([jnp.zeros((PAD, D), F32), gathered(0, 'meta_tokens', started.reshape(1, 1)), x[0]], axis=0)
    layer_p = []
    for l in range(depth):
        p = {n: W[n][l] for n in small_names}
        for n in ('norm_mix_g', 'q_norm_g', 'k_norm_g', 'attn_sinks', 'ssm_d', 'b_glu', 'attn_out_g', 'ssm_out_g', 'norm_mlp_g'):
            p[n] = p[n][None, :]
        layer_p.append(p)
    saved = []
    for l in range(depth):
        h_res, sv = _layer_fwd(h_res, layer_p[l], functools.partial(gathered, l), dims)
        saved.append(sv)
    loss_local, dx = _loss(h_res, loss_target[0])
    loss = lax.psum(loss_local, ("x", "y", "c"))

    small_grads = [None] * depth
    shard_grads = {}
    pending = []

    def finish(after):
        while pending:
            l_, names, st = pending.pop(0)
            mine, other = _rs_finish("%d_%s" % (l_, names[0]), st, after)
            for a, n in enumerate(names):
                shard_grads[(l_, n)] = (mine[a], other[a])

    def reduce_grads(l, names, grads):
        st = _rs_begin("%d_%s" % (l, names[0]), list(grads), [COL_SHARDED[n] for n in names])
        pending.append((l, names, st))
        return st['token'][0, 0]

    for l in reversed(range(depth)):
        dx, gs = _layer_bwd(dx, saved[l], layer_p[l], dims, functools.partial(reduce_grads, l))
        saved[l] = None
        small_grads[l] = gs
        newest = pending.pop()
        finish(dx)
        pending.append(newest)
    grad_x = dx[BLOCK:].reshape(x.shape)

    g_small = {n: jnp.stack([small_grads[l][n].reshape(W[n].shape[1:]) for l in range(depth)]) for n in small_names}
    g_shapes = [(N_META, D)] + [W[n].shape for n in small_names]
    rows = _pack_rows(g_shapes, 8 * 2 * N_CHIPS)
    packed = _pack([dx[PAD:BLOCK]] + [g_small[n] for n in small_names], rows)
    red, = _reduce_scatter("small", [packed], [False], F32)
    red_full, = _all_gather("ag_small", [red], [False])
    finish(red_full)
    g_list = _unpack(red_full, g_shapes)
    g_meta = lax.dynamic_slice_in_dim(g_list[0], chip * meta_tokens.shape[1], meta_tokens.shape[1], axis=1)
    G = dict(zip(small_names, g_list[1:]))
    G['meta_tokens'] = g_meta

    out = {}
    for n in BIG:
        out[n] = _adamw_big("adamw_" + n, [shard_grads[(l, n)][0] for l in range(depth)],
                            [shard_grads[(l, n)][1] for l in range(depth)], W[n], M[n], V[n])
    names = ['meta_tokens'] + small_names
    shapes = [W[n].shape for n in names]
    prow = _pack_rows(shapes, 8)
    d_p, m_p, v_p = _ew("adamw_small", _adam_math, [_pack([W[n] for n in names], prow), _pack([G[n] for n in names], prow),
                                                    _pack([M[n] for n in names], prow), _pack([V[n] for n in names], prow)], [F32] * 3)
    for n, d_, m_, v_ in zip(names, _unpack(d_p, shapes), _unpack(m_p, shapes), _unpack(v_p, shapes)):
        out[n] = (G[n], d_, m_, v_)
    return (loss, grad_x, *[out[n][0] for n in WEIGHTS], *[out[n][1] for n in WEIGHTS],
            *[out[n][2] for n in WEIGHTS], *[out[n][3] for n in WEIGHTS])
```

```python
import functools
import math

import numpy as np
import jax
import jax.numpy as jnp
from jax import lax
from jax.experimental import pallas as pl
from jax.experimental.pallas import tpu as pltpu

F32 = jnp.float32
BF16 = jnp.bfloat16
MESH = pl.DeviceIdType.MESH

N_META = 16
HEAD_DIM = 64
KV_GROUP = 4
GROUP_CH = 16
STATE = 64
BLOCK = 128
PAD = BLOCK - N_META
NORM_EPS = 1e-6
NEG_INF = -1e30
LANES = 128
V7X_VMEM_LIMIT_BYTES = 56 * 1024 * 1024
MM_VMEM_BUDGET_BYTES = 44 * 1024 * 1024

ADAM_LR, ADAM_B1, ADAM_B2, ADAM_EPS, ADAM_WD, ADAM_STEP = 0.001, 0.9, 0.999, 1e-08, 0.01, 10

WEIGHTS = ['meta_tokens', 'norm_mix_g', 'w_in', 'q_norm_g', 'k_norm_g', 'attn_sinks', 'ssm_lambda_re',
           'ssm_lambda_im', 'ssm_log_step', 'ssm_b_re', 'ssm_b_im', 'ssm_c_re', 'ssm_c_im', 'ssm_d', 'w_glu',
           'b_glu', 'attn_out_g', 'ssm_out_g', 'w_out', 'norm_mlp_g', 'w_up', 'w_down']
BIG = ['w_in', 'w_glu', 'w_out', 'w_up', 'w_down']
COL_SHARDED = {'w_in': True, 'w_glu': False, 'w_out': False, 'w_up': True, 'w_down': False}
N_CHIPS = 4


def _pick(n, cands):
    for c in cands:
        if c <= n and n % c == 0:
            return c
    return n


def _params(sem):
    return pltpu.CompilerParams(dimension_semantics=sem, vmem_limit_bytes=V7X_VMEM_LIMIT_BYTES)


def _pcall(body, **kw):
    return pl.pallas_call(body, **kw)


def _sds(shape, dtype):
    return jax.ShapeDtypeStruct(shape, dtype)


_DIMS = {'nn': ((1,), (0,)), 'nt': ((1,), (1,)), 'tn': ((0,), (0,))}


def _mm(name, a, b, mode, out_dtypes, extras=(), epi=None):
    if mode == 'nn':
        (M, K), (_, N) = a.shape, b.shape
    elif mode == 'nt':
        (M, K), (N, _) = a.shape, b.shape
    else:
        (K, M), (_, N) = a.shape, b.shape
    tile_bytes = 4 * len([k for k, _ in extras if k == 'tile']) + sum(jnp.dtype(d).itemsize for d in out_dtypes)

    def fits(tm, tn, tk):
        need = 2 * tm * tk * a.dtype.itemsize + 2 * tk * tn * b.dtype.itemsize + 4 * tm * tn + 2 * tm * tn * tile_bytes
        return need <= MM_VMEM_BUDGET_BYTES

    if mode == 'tn':
        tm, tk_cands = _pick(M, (1024, 512, 256, 128)), (1408, 704, 384, 128)
    else:
        tm, tk_cands = _pick(M, (1408, 704, 384, 128)), (2048, 1024, 512, 256, 128)
    tk_cands = [t for t in tk_cands if t <= K and K % t == 0] or [K]
    tn_cands = [t for t in (2048, 1280, 1024, 640, 512, 256, 128) if t <= N and N % t == 0] or [N]
    if mode != 'tn' and tk_cands[0] == K and a.dtype == BF16:
        tk_cands = tk_cands[:1]
    tn, tk = next(((tn_, tk_) for tn_ in tn_cands for tk_ in tk_cands if fits(tm, tn_, tk_)), (tn_cands[-1], tk_cands[-1]))
    nk = K // tk
    a_spec = pl.BlockSpec((tk, tm), lambda i, j, k: (k, i)) if mode == 'tn' else pl.BlockSpec((tm, tk), lambda i, j, k: (i, k))
    b_spec = pl.BlockSpec((tn, tk), lambda i, j, k: (j, k)) if mode == 'nt' else pl.BlockSpec((tk, tn), lambda i, j, k: (k, j))
    ex_specs = [pl.BlockSpec((tm, tn), lambda i, j, k: (i, j)) if kind == 'tile' else pl.BlockSpec((1, tn), lambda i, j, k: (0, j))
                for kind, _ in extras]
    ne, no = len(extras), len(out_dtypes)
    dims = (_DIMS[mode], ((), ()))

    def body(a_ref, b_ref, *rest):
        ex, outs, acc = rest[:ne], rest[ne:ne + no], rest[ne + no]
        k = pl.program_id(2)

        @pl.when(k == 0)
        def _():
            acc[...] = jnp.zeros_like(acc)

        acc[...] += lax.dot_general(a_ref[...].astype(BF16), b_ref[...].astype(BF16), dims, preferred_element_type=F32)

        @pl.when(k == nk - 1)
        def _():
            r = acc[...]
            res = epi(r, *[e[...] for e in ex]) if epi is not None else (r,)
            for o, v in zip(outs, res):
                o[...] = v.astype(o.dtype)

    outs = _pcall(
        body, name=name, grid=(M // tm, N // tn, nk),
        in_specs=[a_spec, b_spec] + ex_specs,
        out_specs=[pl.BlockSpec((tm, tn), lambda i, j, k: (i, j)) for _ in out_dtypes],
        out_shape=[_sds((M, N), d) for d in out_dtypes],
        scratch_shapes=[pltpu.VMEM((tm, tn), F32)],
        compiler_params=_params(("parallel", "parallel", "arbitrary")),
    )(a, b, *[e for _, e in extras])
    return outs


def _ew(name, fn, ins, out_dtypes):
    R, C = ins[0].shape
    tr = _pick(R, tuple(t for t in (1024, 512, 256, 128, 64, 32, 16, 8) if t * C <= 512 * 1024) or (8,))
    n_in = len(ins)

    def body(*refs):
        res = fn(*[r[...] for r in refs[:n_in]])
        for o, v in zip(refs[n_in:], res):
            o[...] = v.astype(o.dtype)

    spec = pl.BlockSpec((tr, C), lambda i: (i, 0))
    return _pcall(body, name=name, grid=(R // tr,), in_specs=[spec] * n_in, out_specs=[spec] * len(out_dtypes),
                  out_shape=[_sds((R, C), d) for d in out_dtypes], compiler_params=_params(("parallel",)))(*ins)


def _adam_math(w, g, m, v):
    m = ADAM_B1 * m + (1.0 - ADAM_B1) * g
    v = ADAM_B2 * v + (1.0 - ADAM_B2) * (g * g)
    m_hat = m / (1.0 - ADAM_B1 ** ADAM_STEP)
    v_hat = v / (1.0 - ADAM_B2 ** ADAM_STEP)
    delta = -ADAM_LR * (m_hat / (jnp.sqrt(v_hat) + ADAM_EPS) + ADAM_WD * w)
    return delta, m, v


def _rms_fwd(name, xs, gs, out_dtype):
    L = xs[0].shape[0]
    ws = [x.shape[1] for x in xs]
    n = len(xs)
    tr = _pick(L, (384, 256, 128))

    def body(*refs):
        o = refs[2 * n]
        off = 0
        for i in range(n):
            x = refs[i][...]
            r = lax.rsqrt(jnp.mean(x * x, axis=-1, keepdims=True) + NORM_EPS)
            o[:, off:off + ws[i]] = ((x * r) * refs[n + i][...]).astype(o.dtype)
            off += ws[i]

    return _pcall(body, name=name, grid=(L // tr,),
                  in_specs=[pl.BlockSpec((tr, w), lambda i: (i, 0)) for w in ws] + [pl.BlockSpec((1, w), lambda i: (0, 0)) for w in ws],
                  out_specs=pl.BlockSpec((tr, sum(ws)), lambda i: (i, 0)), out_shape=_sds((L, sum(ws)), out_dtype),
                  compiler_params=_params(("parallel",)))(*xs, *gs)


def _rms_bwd(name, xs, gs, dy, resid=None):
    L = xs[0].shape[0]
    ws = [x.shape[1] for x in xs]
    n = len(xs)
    tr = _pick(L, (384, 256, 128))
    has_res = resid is not None

    def body(*refs):
        x_refs, g_refs, dy_ref = refs[:n], refs[n:2 * n], refs[2 * n]
        p = 2 * n + 1
        res_ref = refs[p] if has_res else None
        p += 1 if has_res else 0
        dx_refs, dg_refs = refs[p:p + n], refs[p + n:p + 2 * n]
        dx16_ref = refs[p + 2 * n] if has_res else None
        first = pl.program_id(0) == 0
        off = 0
        for i in range(n):
            x = x_refs[i][...]
            d = dy_ref[:, off:off + ws[i]]
            r = lax.rsqrt(jnp.mean(x * x, axis=-1, keepdims=True) + NORM_EPS)
            xh = x * r
            dg = jnp.sum(d * xh, axis=0, keepdims=True)

            @pl.when(first)
            def _(i=i):
                dg_refs[i][...] = jnp.zeros_like(dg_refs[i])

            dg_refs[i][...] += dg
            dyg = d * g_refs[i][...]
            dx = r * (dyg - xh * jnp.mean(dyg * xh, axis=-1, keepdims=True))
            if has_res:
                dx = dx + res_ref[...]
                dx16_ref[...] = dx.astype(BF16)
            dx_refs[i][...] = dx
            off += ws[i]

    in_specs = ([pl.BlockSpec((tr, w), lambda i: (i, 0)) for w in ws] + [pl.BlockSpec((1, w), lambda i: (0, 0)) for w in ws]
                + [pl.BlockSpec((tr, sum(ws)), lambda i: (i, 0))])
    ins = list(xs) + list(gs) + [dy]
    if has_res:
        in_specs.append(pl.BlockSpec((tr, ws[0]), lambda i: (i, 0)))
        ins.append(resid)
    out_specs = [pl.BlockSpec((tr, w), lambda i: (i, 0)) for w in ws] + [pl.BlockSpec((1, w), lambda i: (0, 0)) for w in ws]
    out_shape = [_sds((L, w), F32) for w in ws] + [_sds((1, w), F32) for w in ws]
    if has_res:
        out_specs.append(pl.BlockSpec((tr, ws[0]), lambda i: (i, 0)))
        out_shape.append(_sds((L, ws[0]), BF16))
    outs = _pcall(body, name=name, grid=(L // tr,), in_specs=in_specs, out_specs=out_specs, out_shape=out_shape,
                  compiler_params=_params(("arbitrary",)))(*ins)
    return (outs[:n], outs[n:2 * n], outs[2 * n]) if has_res else (outs[:n], outs[n:])


def _loss(xl, target):
    Lp, D = xl.shape

    def body(x_ref, t_ref, dy_ref, dy16_ref, loss_ref):
        n = pl.program_id(0)

        @pl.when(n == 0)
        def _():
            loss_ref[...] = jnp.zeros_like(loss_ref)
            dy_ref[...] = jnp.zeros_like(dy_ref)
            dy16_ref[...] = jnp.zeros_like(dy16_ref)

        @pl.when(n > 0)
        def _():
            err = x_ref[...] - t_ref[...]
            dy = err * (1.0 / D)
            dy_ref[...] = dy
            dy16_ref[...] = dy.astype(BF16)
            loss_ref[...] += jnp.sum(err * err) * (0.5 / D)

    blk = pl.BlockSpec((BLOCK, D), lambda n: (n, 0))
    dy, dy16, loss = _pcall(body, name="loss_head", grid=(Lp // BLOCK,),
                            in_specs=[blk, pl.BlockSpec((BLOCK, D), lambda n: (jnp.maximum(n - 1, 0), 0))],
                            out_specs=[blk, blk, pl.BlockSpec((8, LANES), lambda n: (0, 0))],
                            out_shape=[_sds((Lp, D), F32), _sds((Lp, D), BF16), _sds((8, LANES), F32)],
                            compiler_params=_params(("arbitrary",)))(xl, target)
    return loss[0, 0], dy, dy16


GROUP_ROWS = KV_GROUP * BLOCK


def _attn_mask_dist(n):
    r = lax.broadcasted_iota(jnp.int32, (GROUP_ROWS, 3 * BLOCK), 0)
    i = r & (BLOCK - 1)
    j = lax.broadcasted_iota(jnp.int32, (GROUP_ROWS, 3 * BLOCK), 1)
    in_band = j < 2 * BLOCK
    band = in_band & (j > i) & (j <= i + BLOCK) & (j >= 2 * BLOCK - BLOCK * n)
    jm = j - 2 * BLOCK
    meta = (~in_band) & (jm >= PAD) & (jm <= BLOCK * n + i)
    dist = jnp.where(in_band, BLOCK + i - j, BLOCK * n + i - jm).astype(F32)
    return band | meta, dist


def _head_norm(x, g):
    r = lax.rsqrt(jnp.mean(x * x, axis=-1, keepdims=True) + NORM_EPS)
    return (x * r) * g, r


def _attn_specs(attn_w, kv_w):
    kb = attn_w // kv_w
    q_spec = pl.BlockSpec((BLOCK, attn_w), lambda n: (n, 0))

    def kv(col):
        return [pl.BlockSpec((BLOCK, kv_w), lambda n: (jnp.maximum(n - 1, 0), col)),
                pl.BlockSpec((BLOCK, kv_w), lambda n: (n, col)),
                pl.BlockSpec((BLOCK, kv_w), lambda n: (0, col))]

    return q_spec, kv(kb), kv(kb + 1)


def _slopes(n_heads):
    return [2.0 ** (-8.0 * (h + 1) / n_heads) for h in range(n_heads)]


def _head_slice(h):
    return slice(h * HEAD_DIM, (h + 1) * HEAD_DIM)


def _stack_heads(ref, kh):
    return jnp.concatenate([ref[:, _head_slice(kh * KV_GROUP + g)] for g in range(KV_GROUP)], axis=0)


def _group_column(vals):
    return jnp.concatenate([jnp.broadcast_to(v, (BLOCK, 1)) for v in vals], axis=0)


def _group_inputs(kh, slopes, q_ref, kp, kc, km, vp, vc, vm, gq_ref, gk_ref, sk_ref):
    cs = _head_slice(kh)
    kn, _ = _head_norm(jnp.concatenate([kp[:, cs], kc[:, cs], km[:, cs]], axis=0), gk_ref[...])
    vcat = jnp.concatenate([vp[:, cs], vc[:, cs], vm[:, cs]], axis=0).astype(BF16)
    q = _stack_heads(q_ref, kh)
    qn, rq = _head_norm(q, gq_ref[...])
    heads = range(kh * KV_GROUP, (kh + 1) * KV_GROUP)
    slope = _group_column([jnp.full((1, 1), slopes[h], F32) for h in heads])
    sink = _group_column([sk_ref[0:1, h:h + 1] for h in heads])
    return q, qn, rq, kn, vcat, slope, sink


def _scores(qn, kn, slope, sink, mask, dist):
    s = lax.dot_general(qn.astype(BF16), kn.astype(BF16), (((1,), (1,)), ((), ())), preferred_element_type=F32)
    s = s * (1.0 / math.sqrt(HEAD_DIM)) - slope * dist
    s = jnp.where(mask, s, NEG_INF)
    m = jnp.maximum(jnp.max(s, axis=-1, keepdims=True), sink)
    p = jnp.exp(s - m)
    ps = jnp.exp(sink - m)
    inv = 1.0 / (jnp.sum(p, axis=-1, keepdims=True) + ps)
    return p * inv, ps * inv


def _attn_fwd(proj, gq, gk, sinks, attn_w, kv_w):
    Lp = proj.shape[0]
    n_heads, n_kv = attn_w // HEAD_DIM, kv_w // HEAD_DIM
    slopes = _slopes(n_heads)
    q_spec, k_specs, v_specs = _attn_specs(attn_w, kv_w)

    def body(q_ref, kp, kc, km, vp, vc, vm, gq_ref, gk_ref, sk_ref, o_ref):
        mask, dist = _attn_mask_dist(pl.program_id(0))
        for kh in range(n_kv):
            _, qn, _, kn, vcat, slope, sink = _group_inputs(kh, slopes, q_ref, kp, kc, km, vp, vc, vm, gq_ref, gk_ref, sk_ref)
            p, _ = _scores(qn, kn, slope, sink, mask, dist)
            o = jnp.dot(p.astype(BF16), vcat, preferred_element_type=F32)
            for g in range(KV_GROUP):
                o_ref[:, _head_slice(kh * KV_GROUP + g)] = o[g * BLOCK:(g + 1) * BLOCK]

    small = lambda w: pl.BlockSpec((1, w), lambda n: (0, 0))
    return _pcall(body, name="attn_fwd", grid=(Lp // BLOCK,),
                  in_specs=[q_spec] + k_specs + v_specs + [small(HEAD_DIM), small(HEAD_DIM), small(n_heads)],
                  out_specs=pl.BlockSpec((BLOCK, attn_w), lambda n: (n, 0)), out_shape=_sds((Lp, attn_w), F32),
                  compiler_params=_params(("parallel",)))(proj, proj, proj, proj, proj, proj, proj, gq, gk, sinks)


def _attn_bwd(proj, attn, dattn, gq, gk, sinks, attn_w, kv_w):
    Lp = proj.shape[0]
    n_heads, n_kv = attn_w // HEAD_DIM, kv_w // HEAD_DIM
    slopes = _slopes(n_heads)
    q_spec, k_specs, v_specs = _attn_specs(attn_w, kv_w)
    scale = 1.0 / math.sqrt(HEAD_DIM)
    tn_dims = (((0,), (0,)), ((), ()))

    def body(q_ref, kp, kc, km, vp, vc, vm, o_ref, do_ref, gq_ref, gk_ref, sk_ref, dq_ref, dk_ref, dv_ref, dgq_ref, dsk_ref):
        n = pl.program_id(0)

        @pl.when(n == 0)
        def _():
            dk_ref[...] = jnp.zeros_like(dk_ref)
            dv_ref[...] = jnp.zeros_like(dv_ref)
            dgq_ref[...] = jnp.zeros_like(dgq_ref)
            dsk_ref[...] = jnp.zeros_like(dsk_ref)

        mask, dist = _attn_mask_dist(n)
        lane = lax.broadcasted_iota(jnp.int32, (1, n_heads), 1)
        rows_prev = pl.ds(pl.multiple_of(jnp.maximum(n - 1, 0) * BLOCK, BLOCK), BLOCK)
        rows_cur = pl.ds(pl.multiple_of(n * BLOCK, BLOCK), BLOCK)
        rows_meta = pl.ds(0, BLOCK)
        dgq = jnp.zeros((1, HEAD_DIM), F32)
        dsk = jnp.zeros((1, n_heads), F32)
        for kh in range(n_kv):
            cs = _head_slice(kh)
            q, qn, rq, kn, vcat, slope, sink = _group_inputs(kh, slopes, q_ref, kp, kc, km, vp, vc, vm, gq_ref, gk_ref, sk_ref)
            p, ps = _scores(qn, kn, slope, sink, mask, dist)
            do = _stack_heads(do_ref, kh)
            dd = jnp.sum(do * _stack_heads(o_ref, kh), axis=-1, keepdims=True)
            do16 = do.astype(BF16)
            dp = lax.dot_general(do16, vcat, (((1,), (1,)), ((), ())), preferred_element_type=F32)
            ds16 = (p * (dp - dd)).astype(BF16)
            dsink = -ps * dd
            for g in range(KV_GROUP):
                dsk = dsk + jnp.where(lane == kh * KV_GROUP + g, jnp.sum(dsink[g * BLOCK:(g + 1) * BLOCK]), 0.0)
            dqn = jnp.dot(ds16, kn.astype(BF16), preferred_element_type=F32) * scale
            dkn = lax.dot_general(ds16, qn.astype(BF16), tn_dims, preferred_element_type=F32) * scale
            dvc = lax.dot_general(p.astype(BF16), do16, tn_dims, preferred_element_type=F32)
            xh = q * rq
            dgq = dgq + jnp.sum(dqn * xh, axis=0, keepdims=True)
            dyg = dqn * gq_ref[...]
            dq = rq * (dyg - xh * jnp.mean(dyg * xh, axis=-1, keepdims=True))
            for g in range(KV_GROUP):
                dq_ref[:, _head_slice(kh * KV_GROUP + g)] = dq[g * BLOCK:(g + 1) * BLOCK]
            for part, rows in enumerate((rows_prev, rows_cur, rows_meta)):
                ps_ = slice(part * BLOCK, (part + 1) * BLOCK)
                dk_ref[rows, cs] += dkn[ps_]
                dv_ref[rows, cs] += dvc[ps_]
        dgq_ref[...] += dgq
        dsk_ref[...] += dsk

    small = lambda w: pl.BlockSpec((1, w), lambda n: (0, 0))
    blk = pl.BlockSpec((BLOCK, attn_w), lambda n: (n, 0))
    whole = pl.BlockSpec((Lp, kv_w), lambda n: (0, 0))
    return _pcall(body, name="attn_bwd", grid=(Lp // BLOCK,),
                  in_specs=[q_spec] + k_specs + v_specs + [blk, blk, small(HEAD_DIM), small(HEAD_DIM), small(n_heads)],
                  out_specs=[blk, whole, whole, small(HEAD_DIM), small(n_heads)],
                  out_shape=[_sds((Lp, attn_w), F32), _sds((Lp, kv_w), F32), _sds((Lp, kv_w), F32),
                             _sds((1, HEAD_DIM), F32), _sds((1, n_heads), F32)],
                  compiler_params=_params(("arbitrary",)))(proj, proj, proj, proj, proj, proj, proj, attn, dattn, gq, gk, sinks)


def _knorm_bwd(proj, dkn, gk, attn_w, kv_w):
    Lp = proj.shape[0]
    n_kv = kv_w // HEAD_DIM
    tr = _pick(Lp, (384, 256, 128))

    def body(k_ref, d_ref, g_ref, dk_ref, dg_ref):
        @pl.when(pl.program_id(0) == 0)
        def _():
            dg_ref[...] = jnp.zeros_like(dg_ref)

        dg = jnp.zeros((1, HEAD_DIM), F32)
        for kh in range(n_kv):
            cs = slice(kh * HEAD_DIM, (kh + 1) * HEAD_DIM)
            x = k_ref[:, cs]
            d = d_ref[:, cs]
            r = lax.rsqrt(jnp.mean(x * x, axis=-1, keepdims=True) + NORM_EPS)
            xh = x * r
            dg = dg + jnp.sum(d * xh, axis=0, keepdims=True)
            dyg = d * g_ref[...]
            dk_ref[:, cs] = r * (dyg - xh * jnp.mean(dyg * xh, axis=-1, keepdims=True))
        dg_ref[...] += dg

    return _pcall(body, name="knorm_bwd", grid=(Lp // tr,),
                  in_specs=[pl.BlockSpec((tr, kv_w), lambda i: (i, attn_w // kv_w)), pl.BlockSpec((tr, kv_w), lambda i: (i, 0)),
                            pl.BlockSpec((1, HEAD_DIM), lambda i: (0, 0))],
                  out_specs=[pl.BlockSpec((tr, kv_w), lambda i: (i, 0)), pl.BlockSpec((1, HEAD_DIM), lambda i: (0, 0))],
                  out_shape=[_sds((Lp, kv_w), F32), _sds((1, HEAD_DIM), F32)],
                  compiler_params=_params(("arbitrary",)))(proj, dkn, gk)


def _ssm_bbar(lr, li, ls, br, bi):
    def fn(lr, li, ls, br, bi):
        fr, fi = _zoh_factor(lr, li, ls)
        return fr * br - fi * bi, fr * bi + fi * br

    return _ew("ssm_bbar", fn, [lr, li, ls, br, bi], [F32, F32])


def _lam_bar(lr, li, ls):
    dl = jnp.exp(ls)
    e = jnp.exp(lr * dl)
    return e * jnp.cos(li * dl), e * jnp.sin(li * dl), dl


def _zoh_factor(lr, li, ls):
    ar, ai, _ = _lam_bar(lr, li, ls)
    n2 = lr * lr + li * li
    ivr, ivi = lr / n2, -li / n2
    return (ar - 1.0) * ivr - ai * ivi, (ar - 1.0) * ivi + ai * ivr


SCAN_SHIFTS = (1, 2, 4)


def _ssm_tables(lr, li, ls):
    Wx = lr.shape[1]

    def body(lr_ref, li_ref, ls_ref, tf_ref, tr_ref):
        dl = jnp.exp(ls_ref[...])
        zr, zi = lr_ref[...] * dl, li_ref[...] * dl
        row = lax.broadcasted_iota(jnp.int32, (8, Wx), 0)

        def power(kf):
            e = jnp.exp(kf * zr)
            return e * jnp.cos(kf * zi), e * jnp.sin(kf * zi)

        for ref, rev in ((tf_ref, False), (tr_ref, True)):
            sgn = -1.0 if rev else 1.0
            for k, d in enumerate(SCAN_SHIFTS):
                ar, ai = power(jnp.full((8, Wx), float(d), F32))
                keep = (row < 8 - d) if rev else (row >= d)
                ref[k] = jnp.where(keep, ar, 0.0)
                ref[4 + k] = jnp.where(keep, sgn * ai, 0.0)
            pr, pi = power(((8 - row) if rev else (row + 1)).astype(F32))
            ref[3] = pr
            ref[7] = sgn * pi

    full = pl.BlockSpec((1, Wx), lambda: (0, 0))
    tab = pl.BlockSpec((8, 8, Wx), lambda: (0, 0, 0))
    return _pcall(body, name="ssm_tables", in_specs=[full] * 3, out_specs=[tab, tab],
                  out_shape=[_sds((8, 8, Wx), F32)] * 2,
                  compiler_params=pltpu.CompilerParams(vmem_limit_bytes=V7X_VMEM_LIMIT_BYTES))(lr, li, ls)


def _scan(name, br, bi, tab, reverse, states=None):
    L, Wx = br.shape
    TB = _pick(L, (384, 256, 128))
    CW = _pick(Wx, (512, 256, 128))
    nT, nG = L // TB, TB // 8

    def body(*refs):
        if reverse:
            br_ref, bi_ref, xr_ref, xi_ref, tab_ref, or_ref, oi_ref, s1_ref, s2_ref, cr_ref, ci_ref = refs
        else:
            br_ref, bi_ref, tab_ref, or_ref, oi_ref, cr_ref, ci_ref = refs

        @pl.when(pl.program_id(1) == 0)
        def _():
            cr_ref[...] = jnp.zeros_like(cr_ref)
            ci_ref[...] = jnp.zeros_like(ci_ref)
            if reverse:
                s1_ref[...] = jnp.zeros_like(s1_ref)
                s2_ref[...] = jnp.zeros_like(s2_ref)

        def step(q, carry):
            cr, ci = carry[0], carry[1]
            g = (nG - 1 - q) if reverse else q
            rows = pl.ds(pl.multiple_of(g * 8, 8), 8)
            b_r, b_i = br_ref[rows, :], bi_ref[rows, :]
            sr, si = b_r, b_i
            for k, d in enumerate(SCAN_SHIFTS):
                mr, mi = tab_ref[k], tab_ref[4 + k]
                sh = (8 - d) if reverse else d
                pr, pi = pltpu.roll(sr, sh, 0), pltpu.roll(si, sh, 0)
                sr, si = sr + mr * pr - mi * pi, si + mr * pi + mi * pr
            pwr, pwi = tab_ref[3], tab_ref[7]
            xr = sr + pwr * cr - pwi * ci
            xi = si + pwr * ci + pwi * cr
            or_ref[rows, :] = xr
            oi_ref[rows, :] = xi
            row = 0 if reverse else 7
            out = (jnp.broadcast_to(xr[row:row + 1, :], xr.shape), jnp.broadcast_to(xi[row:row + 1, :], xi.shape))
            if reverse:
                hr, hi = xr - b_r, xi - b_i
                st_r, st_i = xr_ref[rows, :], xi_ref[rows, :]
                out = out + (carry[2] + hr * st_r + hi * st_i, carry[3] + hi * st_r - hr * st_i)
            return out

        init = (cr_ref[...], ci_ref[...])
        if reverse:
            init = init + (jnp.zeros((8, CW), F32), jnp.zeros((8, CW), F32))
        fin = lax.fori_loop(0, nG, step, init)
        cr_ref[...] = fin[0]
        ci_ref[...] = fin[1]
        if reverse:
            s1_ref[...] += fin[2]
            s2_ref[...] += fin[3]

    tmap = (lambda j, t: (nT - 1 - t, j)) if reverse else (lambda j, t: (t, j))
    blk = pl.BlockSpec((TB, CW), tmap)
    tab_spec = pl.BlockSpec((8, 8, CW), lambda j, t: (0, 0, j))
    sum_spec = pl.BlockSpec((8, CW), lambda j, t: (0, j))
    ins = [br, bi] + (list(states) if reverse else []) + [tab]
    in_specs = [blk, blk] + ([blk, blk] if reverse else []) + [tab_spec]
    out_specs = [blk, blk] + ([sum_spec, sum_spec] if reverse else [])
    out_shape = [_sds((L, Wx), F32)] * 2 + ([_sds((8, Wx), F32)] * 2 if reverse else [])
    return _pcall(body, name=name, grid=(Wx // CW, nT), in_specs=in_specs, out_specs=out_specs, out_shape=out_shape,
                  scratch_shapes=[pltpu.VMEM((8, CW), F32), pltpu.VMEM((8, CW), F32)],
                  compiler_params=_params(("parallel", "arbitrary")))(*ins)


def _row_tile(L):
    return _pick(L, (1408, 704, 384, 128))


def _blockproj(name, src, off, w_r, w_i):
    L = src.shape[0]
    T = w_r.shape[0]
    tm = _row_tile(L)

    def body(s_ref, wr_ref, wi_ref, or_ref, oi_ref):
        s = s_ref[...].astype(BF16)
        or_ref[...] = jnp.dot(s, wr_ref[...], preferred_element_type=F32)
        oi_ref[...] = jnp.dot(s, wi_ref[...], preferred_element_type=F32)

    w_spec = pl.BlockSpec((None, LANES, LANES), lambda i, t: (t, 0, 0))
    o_spec = pl.BlockSpec((tm, LANES), lambda i, t: (i, t))
    return _pcall(body, name=name, grid=(L // tm, T),
                  in_specs=[pl.BlockSpec((tm, LANES), lambda i, t: (i, off + t // 4)), w_spec, w_spec],
                  out_specs=[o_spec, o_spec], out_shape=[_sds((L, T * LANES), F32)] * 2,
                  compiler_params=_params(("parallel", "arbitrary")))(src, w_r, w_i)


def _blockproj_grad(name, src, off, gr, gi):
    L = src.shape[0]
    T = gr.shape[1] // LANES
    tm = _row_tile(L)
    tn_dims = (((0,), (0,)), ((), ()))

    def body(s_ref, gr_ref, gi_ref, or_ref, oi_ref):
        @pl.when(pl.program_id(1) == 0)
        def _():
            or_ref[...] = jnp.zeros_like(or_ref)
            oi_ref[...] = jnp.zeros_like(oi_ref)

        s = s_ref[...].astype(BF16)
        or_ref[...] += lax.dot_general(s, gr_ref[...].astype(BF16), tn_dims, preferred_element_type=F32)
        oi_ref[...] += lax.dot_general(s, gi_ref[...].astype(BF16), tn_dims, preferred_element_type=F32)

    g_spec = pl.BlockSpec((tm, LANES), lambda t, i: (i, t))
    o_spec = pl.BlockSpec((None, LANES, LANES), lambda t, i: (t, 0, 0))
    return _pcall(body, name=name, grid=(T, L // tm),
                  in_specs=[pl.BlockSpec((tm, LANES), lambda t, i: (i, off + t // 4)), g_spec, g_spec],
                  out_specs=[o_spec, o_spec], out_shape=[_sds((T, LANES, LANES), F32)] * 2,
                  compiler_params=_params(("parallel", "arbitrary")))(src, gr, gi)


def _gelu(y):
    k = math.sqrt(2.0 / math.pi)
    return 0.5 * y * (1.0 + jnp.tanh(k * (y + 0.044715 * (y * y * y))))


def _gelu_grad(y):
    k = math.sqrt(2.0 / math.pi)
    t = jnp.tanh(k * (y + 0.044715 * (y * y * y)))
    return 0.5 * (1.0 + t) + 0.5 * y * (1.0 - t * t) * (k * (1.0 + 3 * 0.044715 * (y * y)))


def _ssm_out(xr, xi, w_r, w_i, proj, u_off, dvec):
    L = xr.shape[0]
    J = w_r.shape[0]
    SW = w_r.shape[1]
    tm = _row_tile(L)

    def body(xr_ref, xi_ref, wr_ref, wi_ref, u_ref, d_ref, y_ref, gl_ref):
        acc = jnp.dot(xr_ref[...].astype(BF16), wr_ref[...], preferred_element_type=F32)
        acc += jnp.dot(xi_ref[...].astype(BF16), wi_ref[...], preferred_element_type=F32)
        y = acc + d_ref[...] * u_ref[...]
        y_ref[...] = y
        gl_ref[...] = _gelu(y)

    x_spec = pl.BlockSpec((tm, SW), lambda j, i: (i, j))
    w_spec = pl.BlockSpec((None, SW, LANES), lambda j, i: (j, 0, 0))
    o_spec = pl.BlockSpec((tm, LANES), lambda j, i: (i, j))
    return _pcall(body, name="ssm_out", grid=(J, L // tm),
                  in_specs=[x_spec, x_spec, w_spec, w_spec, pl.BlockSpec((tm, LANES), lambda j, i: (i, u_off + j)),
                            pl.BlockSpec((1, LANES), lambda j, i: (0, j))],
                  out_specs=[o_spec, o_spec], out_shape=[_sds((L, J * LANES), F32)] * 2,
                  compiler_params=_params(("parallel", "parallel")))(xr, xi, w_r, w_i, proj, dvec)


def _ssm_du(gr, gi, w_r, w_i, dy, proj, u_off, dvec):
    L = gr.shape[0]
    J = w_r.shape[0]
    SW = w_r.shape[1]
    tm = _row_tile(L)

    def body(gr_ref, gi_ref, wr_ref, wi_ref, dy_ref, u_ref, d_ref, du_ref, dd_ref):
        i = pl.program_id(1)

        @pl.when(i == 0)
        def _():
            dd_ref[...] = jnp.zeros_like(dd_ref)

        acc = jnp.dot(gr_ref[...].astype(BF16), wr_ref[...], preferred_element_type=F32)
        acc += jnp.dot(gi_ref[...].astype(BF16), wi_ref[...], preferred_element_type=F32)
        dy = dy_ref[...]
        row = lax.broadcasted_iota(jnp.int32, (tm, LANES), 0) + i * tm
        du_ref[...] = jnp.where(row >= PAD, acc + d_ref[...] * dy, 0.0)
        dd_ref[...] += jnp.sum(dy * u_ref[...], axis=0, keepdims=True)

    x_spec = pl.BlockSpec((tm, SW), lambda j, i: (i, j))
    w_spec = pl.BlockSpec((None, SW, LANES), lambda j, i: (j, 0, 0))
    o_spec = pl.BlockSpec((tm, LANES), lambda j, i: (i, j))
    vec = pl.BlockSpec((1, LANES), lambda j, i: (0, j))
    return _pcall(body, name="ssm_du", grid=(J, L // tm),
                  in_specs=[x_spec, x_spec, w_spec, w_spec, o_spec, pl.BlockSpec((tm, LANES), lambda j, i: (i, u_off + j)), vec],
                  out_specs=[o_spec, vec], out_shape=[_sds((L, J * LANES), F32), _sds((1, J * LANES), F32)],
                  compiler_params=_params(("parallel", "arbitrary")))(gr, gi, w_r, w_i, dy, proj, dvec)


def _ssm_dc(xr, xi, dy, SW):
    L = xr.shape[0]
    J = dy.shape[1] // LANES
    tm = _row_tile(L)
    tn_dims = (((0,), (0,)), ((), ()))

    def body(xr_ref, xi_ref, dy_ref, or_ref, oi_ref):
        @pl.when(pl.program_id(1) == 0)
        def _():
            or_ref[...] = jnp.zeros_like(or_ref)
            oi_ref[...] = jnp.zeros_like(oi_ref)

        d = dy_ref[...].astype(BF16)
        or_ref[...] += lax.dot_general(xr_ref[...].astype(BF16), d, tn_dims, preferred_element_type=F32)
        oi_ref[...] += lax.dot_general(xi_ref[...].astype(BF16), d, tn_dims, preferred_element_type=F32)

    x_spec = pl.BlockSpec((tm, SW), lambda j, i: (i, j))
    o_spec = pl.BlockSpec((None, SW, LANES), lambda j, i: (j, 0, 0))
    return _pcall(body, name="ssm_dc", grid=(J, L // tm),
                  in_specs=[x_spec, x_spec, pl.BlockSpec((tm, LANES), lambda j, i: (i, j))],
                  out_specs=[o_spec, o_spec], out_shape=[_sds((J, SW, LANES), F32)] * 2,
                  compiler_params=_params(("parallel", "arbitrary")))(xr, xi, dy)


def _glu_dz(ds, gl, z):
    L, W = ds.shape
    tr = _pick(L, (384, 256, 128))

    def body(ds_ref, gl_ref, z_ref, dz_ref, db_ref):
        @pl.when(pl.program_id(0) == 0)
        def _():
            db_ref[...] = jnp.zeros_like(db_ref)

        sg = jax.nn.sigmoid(z_ref[...])
        dz = ds_ref[...] * gl_ref[...] * (sg * (1.0 - sg))
        dz_ref[...] = dz.astype(BF16)
        db_ref[...] += jnp.sum(dz, axis=0, keepdims=True)

    spec = pl.BlockSpec((tr, W), lambda i: (i, 0))
    vec = pl.BlockSpec((1, W), lambda i: (0, 0))
    return _pcall(body, name="glu_dz", grid=(L // tr,), in_specs=[spec] * 3, out_specs=[spec, vec],
                  out_shape=[_sds((L, W), BF16), _sds((1, W), F32)], compiler_params=_params(("arbitrary",)))(ds, gl, z)


def _ssm_param_bwd_flat(lr, li, ls, br, bi, dbbr, dbbi):
    def seg_sum(x):
        for s in (8, 4, 2, 1):
            x = x + pltpu.roll(x, LANES - s, 1)
        return x

    def fn(lr, li, ls, br, bi, dbbr, dbbi):
        fr, fi = _zoh_factor(lr, li, ls)
        return (fr * dbbr + fi * dbbi, fr * dbbi - fi * dbbr,
                seg_sum(br * dbbr + bi * dbbi), seg_sum(br * dbbi - bi * dbbr))

    return _ew("ssm_param_bwd_flat", fn, [lr, li, ls, br, bi, dbbr, dbbi], [F32] * 4)


def _ssm_param_bwd(lr, li, ls, dfr, dfi, s1, s2):
    G, P = lr.shape

    def body(lr_ref, li_ref, ls_ref, dfr_ref, dfi_ref, s1_ref, s2_ref, dlr_ref, dli_ref, dls_ref):
        lr, li = lr_ref[...], li_ref[...]
        ar, ai, dl = _lam_bar(lr, li, ls_ref[...])
        sr, si = s1_ref[0], s2_ref[0]
        for k in range(1, 8):
            sr = sr + s1_ref[k]
            si = si + s2_ref[k]
        a2 = ar * ar + ai * ai
        gar, gai = (sr * ar - si * ai) / a2, (sr * ai + si * ar) / a2
        n2 = lr * lr + li * li
        ivr, ivi = lr / n2, -li / n2
        fr = (ar - 1.0) * ivr - ai * ivi
        fi = (ar - 1.0) * ivi + ai * ivr
        dfr, dfi = dfr_ref[...], dfi_ref[...]
        gar = gar + ivr * dfr + ivi * dfi
        gai = gai + ivr * dfi - ivi * dfr
        wr, wi = -(fr * ivr - fi * ivi), -(fr * ivi + fi * ivr)
        glr, gli = wr * dfr + wi * dfi, wr * dfi - wi * dfr
        gzr, gzi = ar * gar + ai * gai, ar * gai - ai * gar
        dlr_ref[...] = glr + dl * gzr
        dli_ref[...] = gli + dl * gzi
        dls_ref[...] = dl * jnp.sum(lr * gzr + li * gzi, axis=-1, keepdims=True)

    m = pl.BlockSpec((G, P), lambda: (0, 0))
    v = pl.BlockSpec((G, 1), lambda: (0, 0))
    s = pl.BlockSpec((8, G, P), lambda: (0, 0, 0))
    return _pcall(body, name="ssm_param_bwd", in_specs=[m, m, v, m, m, s, s], out_specs=[m, m, v],
                  out_shape=[_sds((G, P), F32), _sds((G, P), F32), _sds((G, 1), F32)])(lr, li, ls, dfr, dfi, s1, s2)


def _tile_mask(G):
    T = G // 2
    e = np.zeros((T, 8, 1, 2, 1), np.float32)
    for t in range(T):
        for c in range(2):
            e[t, (2 * t + c) % 8, 0, c, 0] = 1.0
    return e


def _tile_w(arr):
    G = arr.shape[0]
    a = arr.reshape(G // 2, 1, 2, STATE, GROUP_CH).transpose(0, 1, 4, 2, 3)
    return (a * _tile_mask(G)).reshape(G // 2, LANES, LANES).astype(BF16)


def _tile_w_grad(dw):
    G = dw.shape[0] * 2
    d = dw.reshape(G // 2, 8, GROUP_CH, 2, STATE) * _tile_mask(G)
    return d.sum(axis=1).transpose(0, 2, 3, 1).reshape(G, STATE, GROUP_CH)


def _slab_w(arr):
    G = arr.shape[0]
    a = arr.reshape(G // 8, 8, STATE, 1, GROUP_CH)
    eye = np.eye(8, dtype=np.float32).reshape(1, 8, 1, 8, 1)
    return (a * eye).reshape(G // 8, 8 * STATE, LANES).astype(BF16)


def _slab_w_grad(dw):
    J = dw.shape[0]
    eye = np.eye(8, dtype=np.float32).reshape(1, 8, 1, 8, 1)
    return (dw.reshape(J, 8, STATE, 8, GROUP_CH) * eye).sum(axis=3).reshape(J * 8, STATE, GROUP_CH)


def _exchange(name, ins, out_sds, remote, local, aliases=None):
    n_in, n_out, n_r, n_l = len(ins), len(out_sds), len(remote), len(local)

    def body(*refs):
        in_refs, out_refs = refs[:n_in], refs[n_in:n_in + n_out]
        send_sems, recv_sems, local_sems = refs[n_in + n_out:]
        x, y, c = lax.axis_index("x"), lax.axis_index("y"), lax.axis_index("c")

        def place(px, py, pc):
            return dict(x=px, y=py, c=pc, chip=2 * px + py)

        def flip(mask):
            mx, my, mc = mask
            return ((1 - x) if mx else x, (1 - y) if my else y, (1 - c) if mc else c)

        me = place(x, y, c)
        sends = []
        for k, (ii, src, oi, dst, mask) in enumerate(remote):
            cp = pltpu.make_async_remote_copy(src_ref=src(in_refs[ii], me), dst_ref=dst(out_refs[oi], me),
                                              send_sem=send_sems.at[k], recv_sem=recv_sems.at[k],
                                              device_id=flip(mask), device_id_type=MESH)
            cp.start()
            sends.append(cp)
        locals_ = []
        for k, (ii, src, oi, dst) in enumerate(local):
            cp = pltpu.make_async_copy(src(in_refs[ii], me), dst(out_refs[oi], me), local_sems.at[k])
            cp.start()
            locals_.append(cp)
        for k, (ii, src, oi, dst, mask) in enumerate(remote):
            sends[k].wait_send()
            peer = flip(mask)
            pltpu.make_async_remote_copy(src_ref=src(in_refs[ii], me), dst_ref=dst(out_refs[oi], place(*peer)),
                                         send_sem=send_sems.at[k], recv_sem=recv_sems.at[k],
                                         device_id=peer, device_id_type=MESH).wait_recv()
        for cp in locals_:
            cp.wait()

    any_spec = pl.BlockSpec(memory_space=pl.ANY)
    return _pcall(body, name=name, in_specs=[any_spec] * n_in, out_specs=[any_spec] * n_out, out_shape=list(out_sds),
                  input_output_aliases=aliases or {},
                  scratch_shapes=[pltpu.SemaphoreType.DMA((n_r,)), pltpu.SemaphoreType.DMA((n_r,)),
                                  pltpu.SemaphoreType.DMA((max(n_l, 1),))])(*ins)


def _mesh_place():
    x, y, c = lax.axis_index("x"), lax.axis_index("y"), lax.axis_index("c")

    def place(px, py, pc):
        return dict(x=px, y=py, c=pc, chip=2 * px + py)

    def flip(mask):
        mx, my, mc = mask
        return ((1 - x) if mx else x, (1 - y) if my else y, (1 - c) if mc else c)

    return place(x, y, c), place, flip


_HBM = pl.BlockSpec(memory_space=pltpu.HBM)
_SEM = pl.BlockSpec(memory_space=pltpu.SEMAPHORE)
_EFFECT = pltpu.SideEffectType.DATAFLOW_SIDE_EFFECTING


def _split_start(name, bufs, groups):
    n, ng = len(bufs), len(groups)

    def body(*refs):
        in_refs, sems, token = refs[:n], refs[n:n + 2 * ng], refs[-1]
        me, _, flip = _mesh_place()
        for g, copies in enumerate(groups):
            for k, (si, src, di, dst, mask) in enumerate(copies):
                pltpu.make_async_remote_copy(src_ref=src(in_refs[si], me), dst_ref=dst(in_refs[di], me),
                                             send_sem=sems[2 * g].at[k], recv_sem=sems[2 * g + 1].at[k],
                                             device_id=flip(mask), device_id_type=MESH).start()
        token[...] = jnp.zeros_like(token)

    outs = _pcall(body, name=name,
                  out_shape=(*[pltpu.SemaphoreType.DMA((len(g),)) for g in groups for _ in range(2)],
                             *[pltpu.HBM(b.shape, b.dtype) for b in bufs], _sds((8, LANES), F32)),
                  in_specs=[_HBM] * n, out_specs=(*[_SEM] * (2 * ng), *[_HBM] * n, pl.BlockSpec(memory_space=pltpu.VMEM)),
                  input_output_aliases={i: 2 * ng + i for i in range(n)},
                  compiler_params=pltpu.CompilerParams(has_side_effects=_EFFECT),
                  )(*[pltpu.with_memory_space_constraint(b, pltpu.HBM) for b in bufs])
    return [(outs[2 * g], outs[2 * g + 1]) for g in range(ng)], list(outs[2 * ng:2 * ng + n]), outs[-1]


def _split_wait(name, bufs, sems, after, remote):
    n = len(bufs)
    send_sems, recv_sems = sems

    def body(*refs):
        in_refs, ssem, rsem = refs[:n], refs[n], refs[n + 1]
        me, place, flip = _mesh_place()
        for k, (si, src, di, dst, mask) in enumerate(remote):
            peer = flip(mask)
            cp = pltpu.make_async_remote_copy(src_ref=src(in_refs[si], me), dst_ref=dst(in_refs[di], place(*peer)),
                                              send_sem=ssem.at[k], recv_sem=rsem.at[k], device_id=peer, device_id_type=MESH)
            cp.wait_send()
            cp.wait_recv()

    return list(_pcall(body, name=name, out_shape=tuple(pltpu.HBM(b.shape, b.dtype) for b in bufs),
                       in_specs=[_HBM] * n + [_SEM, _SEM, pl.BlockSpec(memory_space=pl.ANY)], out_specs=tuple([_HBM] * n),
                       input_output_aliases={i: i for i in range(n)},
                       compiler_params=pltpu.CompilerParams(has_side_effects=_EFFECT))(*bufs, send_sems, recv_sems, after))


CHIP_MASKS = ((0, 1, 0), (1, 0, 0), (1, 1, 0))
SIBLING = (0, 0, 1)


def _whole(ref, p):
    return ref


def _all_gather(name, shards, col_sharded):
    def dst_view(col):
        def view(ref, p):
            r, cdim = ref.shape[0] // (1 if col else N_CHIPS), ref.shape[1] // (N_CHIPS if col else 1)
            if col:
                return ref.at[:, pl.ds(pl.multiple_of(p["chip"] * cdim, LANES), cdim)]
            return ref.at[pl.ds(pl.multiple_of(p["chip"] * r, 8), r), :]
        return view

    out_sds = [_sds((s.shape[0], s.shape[1] * N_CHIPS) if col else (s.shape[0] * N_CHIPS, s.shape[1]), s.dtype)
               for s, col in zip(shards, col_sharded)]
    remote = [(a, _whole, a, dst_view(col), m) for a, col in enumerate(col_sharded) for m in CHIP_MASKS]
    local = [(a, _whole, a, dst_view(col)) for a, col in enumerate(col_sharded)]
    return _exchange(name, shards, out_sds, remote, local)


class _Place:
    def __getitem__(self, k):
        return lax.axis_index("c") if k == 0 else 2 * lax.axis_index("x") + lax.axis_index("y")


def _placed_call(body, name, grid, in_specs, out_specs, out_shape, sem, ins):
    def wrap(spec):
        return pl.BlockSpec(spec.block_shape, lambda *idx: spec.index_map(*idx, _Place()))

    outs = [wrap(s) for s in out_specs] if isinstance(out_specs, (list, tuple)) else wrap(out_specs)
    return _pcall(body, name=name, grid=grid, in_specs=[wrap(s) for s in in_specs], out_specs=outs, out_shape=out_shape,
                  compiler_params=_params(sem))(*ins)


def _rows_within(n, width, limit=512 * 1024):
    return _pick(n, tuple(t for t in (1024, 512, 256, 128, 64, 32, 16) if t * width <= limit) or (16,))


def _region_view(col):
    def view(ref, p):
        if col:
            cdim = ref.shape[1] // N_CHIPS
            return ref.at[:, pl.ds(pl.multiple_of(p["chip"] * cdim, LANES), cdim)]
        r = ref.shape[0] // N_CHIPS
        return ref.at[pl.ds(pl.multiple_of(p["chip"] * r, 16), r), :]
    return view


def _ag_place(name, w, layer, col, dtype):
    _, r, cdim = w.shape
    tr = _rows_within(r, cdim)
    nb = r // tr

    def body(w_ref, o_ref):
        o_ref[...] = w_ref[...].astype(dtype)

    if col:
        out_shape, out_spec = (r, N_CHIPS * cdim), pl.BlockSpec((tr, cdim), lambda i, pr: (i, pr[1]))
    else:
        out_shape, out_spec = (N_CHIPS * r, cdim), pl.BlockSpec((tr, cdim), lambda i, pr: (pr[1] * nb + i, 0))
    return _placed_call(body, name, (nb,), [pl.BlockSpec((None, tr, cdim), lambda i, pr: (layer, i, 0))], out_spec,
                        _sds(out_shape, dtype), ("parallel",), [w])


def _ag_copies(a, col):
    return [(a, _region_view(col), a, _region_view(col), m) for m in CHIP_MASKS]


def _rs_add2(name, g4, a4, out_dtype):
    J, _, h, C = g4.shape
    tr = _rows_within(h, C)

    def body(g_ref, a_ref, o_ref):
        o_ref[...] = (g_ref[...].astype(F32) + a_ref[...].astype(F32)).astype(o_ref.dtype)

    return _placed_call(body, name, (J, h // tr),
                        [pl.BlockSpec((None, None, tr, C), lambda j, i, pr: (j, pr[0], i, 0)),
                         pl.BlockSpec((None, None, tr, C), lambda j, i, pr: (j, 0, i, 0))],
                        pl.BlockSpec((None, tr, C), lambda j, i, pr: (j, i, 0)), _sds((J, h, C), out_dtype),
                        ("parallel", "parallel"), [g4, a4])


def _rs_add4(name, p3, landed, col):
    _, h, w = landed.shape
    tr = _rows_within(h, w)

    def body(p_ref, a_ref, b_ref, c_ref, o_ref):
        o_ref[...] = ((p_ref[...].astype(F32) + a_ref[...].astype(F32)) + b_ref[...].astype(F32)) + c_ref[...].astype(F32)

    own = (pl.BlockSpec((None, tr, w), lambda i, pr: (0, i, pr[1])) if col else pl.BlockSpec((None, tr, w), lambda i, pr: (pr[1], i, 0)))
    slot = lambda k: pl.BlockSpec((None, tr, w), lambda i, pr: (k, i, 0))
    return _placed_call(body, name, (h // tr,), [own, slot(0), slot(1), slot(2)], pl.BlockSpec((tr, w), lambda i, pr: (i, 0)),
                        _sds((h, w), F32), ("parallel",), [p3, landed, landed, landed])


def _rs_begin(tag, grads, col_sharded):
    n = len(grads)
    g4 = [g.reshape((1, 2, g.shape[0] // 2, g.shape[1]) if col else (N_CHIPS, 2, g.shape[0] // (2 * N_CHIPS), g.shape[1]))
          for g, col in zip(grads, col_sharded)]
    other_half = lambda ref, p: ref.at[:, pl.ds(1 - p["c"], 1)]
    theirs = _exchange("rs_sibling_w", g4, [_sds((g.shape[0], 1) + g.shape[2:], g.dtype) for g in g4],
                       [(a, other_half, a, _whole, SIBLING) for a in range(n)], [])
    chip_sum = [_rs_add2("rs_add2_w", g4[a], theirs[a], BF16) for a in range(n)]

    def send_view(col, mask):
        def view(ref, p):
            t = 2 * ((1 - p["x"]) if mask[0] else p["x"]) + ((1 - p["y"]) if mask[1] else p["y"])
            if col:
                sc = ref.shape[2] // N_CHIPS
                return ref.at[0, :, pl.ds(pl.multiple_of(t * sc, LANES), sc)]
            return ref.at[t]
        return view
    slot = lambda k: (lambda ref, p: ref.at[k])
    piece = [(s.shape[1], s.shape[2] // N_CHIPS if col else s.shape[2]) for s, col in zip(chip_sum, col_sharded)]
    landing = [lax.empty((len(CHIP_MASKS),) + s, BF16) for s in piece]
    copies = [(a, send_view(col_sharded[a], m), n + a, slot(k), m) for a in range(n) for k, m in enumerate(CHIP_MASKS)]
    (sems,), bufs, token = _split_start("rs_chips_start_" + tag, chip_sum + landing, [copies])
    return dict(sems=sems, bufs=bufs, copies=copies, token=token, col_sharded=col_sharded)


def _rs_finish(tag, st, after):
    col_sharded = st['col_sharded']
    n = len(col_sharded)
    bufs = _split_wait("rs_chips_wait_" + tag, st['bufs'], st['sems'], after, st['copies'])
    chip_sum, landed = bufs[:n], bufs[n:]
    mine = [_rs_add4("rs_add4_w", chip_sum[a], landed[a], col_sharded[a]) for a in range(n)]
    other = _exchange("rs_halves_w", mine, [_sds(m.shape, F32) for m in mine], [(a, _whole, a, _whole, SIBLING) for a in range(n)], [])
    return mine, other


def _adamw_big(name, mine, other, w, m, v):
    depth, R, C = w.shape
    h = R // 2
    tr = _pick(h, tuple(t for t in (512, 256, 128, 64, 32, 16, 8) if t * C <= 256 * 1024) or (8,))
    nb = h // tr

    def g_spec(kk, hh):
        def imap(l, s, i, pr):
            before = (l < kk) | ((l == kk) & (s < hh))
            return (jnp.where((l == kk) & (s == hh), i, jnp.where(before, 0, nb - 1)), 0)
        return pl.BlockSpec((tr, C), imap)

    st_spec = pl.BlockSpec((None, tr, C), lambda l, s, i, pr: (l, jnp.where(s == 0, pr[0], 1 - pr[0]) * nb + i, 0))

    def body(*refs):
        g_refs = refs[:2 * depth]
        w_ref, m_ref, v_ref, go_ref, d_ref, mo_ref, vo_ref = refs[2 * depth:]
        l, s = pl.program_id(0), pl.program_id(1)
        for kk in range(depth):
            for hh in range(2):
                @pl.when((l == kk) & (s == hh))
                def _(kk=kk, hh=hh):
                    g = g_refs[2 * kk + hh][...]
                    d, mn, vn = _adam_math(w_ref[...], g, m_ref[...], v_ref[...])
                    go_ref[...] = g
                    d_ref[...] = d
                    mo_ref[...] = mn
                    vo_ref[...] = vn

    gs, g_specs = [], []
    for kk in range(depth):
        gs += [mine[kk], other[kk]]
        g_specs += [g_spec(kk, 0), g_spec(kk, 1)]
    return _placed_call(body, name, (depth, 2, nb), g_specs + [st_spec] * 3, [st_spec] * 4, [_sds(w.shape, F32)] * 4,
                        ("arbitrary", "arbitrary", "arbitrary"), gs + [w, m, v])


def _piece_view(col, j, other):
    def view(ref, p):
        R, C = ref.shape
        cc = (1 - p["c"]) if other else p["c"]
        if col:
            hr, sc = R // 2, C // N_CHIPS
            return ref.at[pl.ds(pl.multiple_of(cc * hr, 16), hr), pl.ds(j * sc, sc)]
        hr = R // (2 * N_CHIPS)
        return ref.at[pl.ds(pl.multiple_of((2 * j + cc) * hr, 8), hr), :]
    return view


def _piece_shape(shape, col):
    R, C = shape
    return (R // 2, C // N_CHIPS) if col else (R // (2 * N_CHIPS), C)


def _reduce_scatter(tag, grads, col_sharded, wire_dtype):
    n = len(grads)
    shapes = [_piece_shape(g.shape, col) for g, col in zip(grads, col_sharded)]

    slot = lambda j: (lambda ref, p: ref.at[j])
    remote = [(a, _piece_view(col_sharded[a], j, True), a, slot(j), SIBLING) for a in range(n) for j in range(N_CHIPS)]
    local = [(a, _piece_view(col_sharded[a], j, False), n + a, slot(j)) for a in range(n) for j in range(N_CHIPS)]
    got = _exchange("rs_sibling_" + tag, grads, [_sds((N_CHIPS,) + s, g.dtype) for s, g in zip(shapes, grads)] * 2, remote, local)
    theirs, mine = got[:n], got[n:]
    chip_sum = [_ew("rs_add2_" + tag, lambda a, b: (a.astype(F32) + b.astype(F32),),
                    [m.reshape(-1, m.shape[-1]), t.reshape(-1, t.shape[-1])], [wire_dtype])[0].reshape(m.shape)
                for m, t in zip(mine, theirs)]

    def send_view(mask):
        return lambda ref, p: ref.at[2 * ((1 - p["x"]) if mask[0] else p["x"]) + ((1 - p["y"]) if mask[1] else p["y"])]
    remote = [(a, send_view(m), a, slot(k), m) for a in range(n) for k, m in enumerate(CHIP_MASKS)]
    local = [(a, lambda ref, p: ref.at[p["chip"]], n + a, _whole) for a in range(n)]
    got = _exchange("rs_chips_" + tag, chip_sum,
                    [_sds((len(CHIP_MASKS),) + s, wire_dtype) for s in shapes] + [_sds(s, wire_dtype) for s in shapes], remote, local)
    landed, own = got[:n], got[n:]
    half = [_ew("rs_add4_" + tag, lambda o, a, b, c: (((o.astype(F32) + a.astype(F32)) + b.astype(F32)) + c.astype(F32),),
                [o, l[0], l[1], l[2]], [F32])[0] for o, l in zip(own, landed)]

    def half_rows(ref, p):
        hr = ref.shape[0] // 2
        return ref.at[pl.ds(pl.multiple_of(p["c"] * hr, 8), hr), :]
    remote = [(a, _whole, a, half_rows, SIBLING) for a in range(n)]
    local = [(a, _whole, a, half_rows) for a in range(n)]
    return _exchange("rs_halves_" + tag, half, [_sds((2 * s[0], s[1]), F32) for s in shapes], remote, local)


def _ssm_prepare(p):
    lr, li, ls = p['ssm_lambda_re'], p['ssm_lambda_im'], p['ssm_log_step']
    G = lr.shape[0]
    flat = lambda a: a.reshape(-1, LANES)
    bc = lambda a: flat(jnp.broadcast_to(a, (G, STATE, GROUP_CH)))
    lr3, li3, ls3 = bc(lr[:, :, None]), bc(li[:, :, None]), bc(ls[:, None, None])
    bbr, bbi = _ssm_bbar(lr3, li3, ls3, flat(p['ssm_b_re']), flat(p['ssm_b_im']))
    bbr, bbi = bbr.reshape(G, STATE, GROUP_CH), bbi.reshape(G, STATE, GROUP_CH)
    row = lambda a: a.reshape(1, G * STATE)
    tf, tr = _ssm_tables(row(lr), row(li), row(jnp.broadcast_to(ls[:, None], (G, STATE))))
    cr = p['ssm_c_re'].transpose(0, 2, 1)
    ci = -p['ssm_c_im'].transpose(0, 2, 1)
    return dict(flat3=(lr3, li3, ls3), tf=tf, tr=tr,
                wb=(_tile_w(bbr), _tile_w(bbi)), wbT=(_slab_w(bbr), _slab_w(bbi)),
                wc=(_slab_w(cr), _slab_w(ci)), wcT=(_tile_w(cr), _tile_w(ci)))


def _layer_fwd(x, p, weight, dims):
    attn_w, kv_w, u_off = dims['attn_w'], dims['kv_w'], dims['u_off']
    s = _ssm_prepare(p)
    h = _rms_fwd("norm_mix", [x], [p['norm_mix_g']], BF16)
    w = {'w_in': weight('w_in', h)}
    proj, = _mm("mm_in", h, w['w_in'], 'nn', [F32])
    attn = _attn_fwd(proj, p['q_norm_g'], p['k_norm_g'], p['attn_sinks'], attn_w, kv_w)
    bur, bui = _blockproj("ssm_bu", proj, u_off, *s['wb'])
    xr, xi = _scan("ssm_scan_fwd", bur, bui, s['tf'], False)
    y, gl = _ssm_out(xr, xi, *s['wc'], proj, u_off, p['ssm_d'])
    w['w_glu'] = weight('w_glu', gl)
    ssm, z = _mm("mm_glu", gl, w['w_glu'], 'nn', [F32, F32], extras=[('row', p['b_glu']), ('tile', gl)],
                 epi=lambda acc, b, g: ((lambda zz: (g * jax.nn.sigmoid(zz), zz))(acc + b)))
    mix = _rms_fwd("norm_heads", [attn, ssm], [p['attn_out_g'], p['ssm_out_g']], BF16)
    w['w_out'] = weight('w_out', mix)
    x_mid, = _mm("mm_out", mix, w['w_out'], 'nn', [F32], extras=[('tile', x)], epi=lambda acc, r: (acc + r,))
    h2 = _rms_fwd("norm_mlp", [x_mid], [p['norm_mlp_g']], BF16)
    w['w_up'] = weight('w_up', h2)
    a, r = _mm("mm_up", h2, w['w_up'], 'nn', [F32, BF16],
               epi=lambda acc: (acc, jnp.square(jnp.maximum(acc, 0.0))))
    w['w_down'] = weight('w_down', r)
    x_out, = _mm("mm_down", r, w['w_down'], 'nn', [F32], extras=[('tile', x_mid)], epi=lambda acc, rr: (acc + rr,))
    saved = dict(x=x, h=h, proj=proj, attn=attn, xr=xr, xi=xi, y=y, gl=gl, z=z, ssm=ssm, mix=mix, x_mid=x_mid, h2=h2, a=a, r=r, s=s, w=w)
    return x_out, saved


def _layer_bwd(dx, dx16, sv, p, dims, reduce_grads):
    attn_w, kv_w, u_off = dims['attn_w'], dims['kv_w'], dims['u_off']
    s, w = sv['s'], sv['w']
    gb, gs = {}, {}
    da, = _mm("mm_down_dx", dx16, w['w_down'], 'nt', [BF16], extras=[('tile', sv['a'])],
              epi=lambda acc, a: (acc * (2.0 * jnp.maximum(a, 0.0)),))
    gb['w_down'], = _mm("mm_down_dw", sv['r'], dx16, 'tn', [BF16])
    dh2, = _mm("mm_up_dx", da, w['w_up'], 'nt', [F32])
    gb['w_up'], = _mm("mm_up_dw", sv['h2'], da, 'tn', [BF16])
    token = reduce_grads(('w_up', 'w_down'), [gb['w_up'], gb['w_down']])
    (dx_mid,), (gs['norm_mlp_g'],), dx_mid16 = _rms_bwd("norm_mlp_bwd", [sv['x_mid']], [p['norm_mlp_g'] + token], dh2, resid=dx)
    dmix, = _mm("mm_out_dx", dx_mid16, w['w_out'], 'nt', [F32])
    gb['w_out'], = _mm("mm_out_dw", sv['mix'], dx_mid16, 'tn', [BF16])
    (dattn, dssm), (gs['attn_out_g'], gs['ssm_out_g']) = _rms_bwd(
        "norm_heads_bwd", [sv['attn'], sv['ssm']], [p['attn_out_g'], p['ssm_out_g']], dmix)
    dz, gs['b_glu'] = _glu_dz(dssm, sv['gl'], sv['z'])
    dy, = _mm("mm_glu_dx", dz, w['w_glu'], 'nt', [F32], extras=[('tile', dssm), ('tile', sv['z']), ('tile', sv['y'])],
              epi=lambda acc, ds, z, y: ((acc + ds * jax.nn.sigmoid(z)) * _gelu_grad(y),))
    gb['w_glu'], = _mm("mm_glu_dw", sv['gl'], dz, 'tn', [BF16])
    dxr, dxi = _blockproj("ssm_dstate", dy, 0, *s['wcT'])
    gxr, gxi, s1, s2 = _scan("ssm_scan_bwd", dxr, dxi, s['tr'], True, states=(sv['xr'], sv['xi']))
    du, gs['ssm_d'] = _ssm_du(gxr, gxi, *s['wbT'], dy, sv['proj'], u_off, p['ssm_d'])
    dwb_r, dwb_i = _blockproj_grad("ssm_dbbar", sv['proj'], u_off, gxr, gxi)
    dwc_r, dwc_i = _ssm_dc(sv['xr'], sv['xi'], dy, s['wc'][0].shape[1])
    gs['ssm_c_re'] = _slab_w_grad(dwc_r).transpose(0, 2, 1)
    gs['ssm_c_im'] = -_slab_w_grad(dwc_i).transpose(0, 2, 1)
    G = p['ssm_lambda_re'].shape[0]
    flat = lambda a_: a_.reshape(-1, LANES)
    dbr, dbi, qr, qi = _ssm_param_bwd_flat(*s['flat3'], flat(p['ssm_b_re']), flat(p['ssm_b_im']),
                                           flat(_tile_w_grad(dwb_r)), flat(_tile_w_grad(dwb_i)))
    gs['ssm_b_re'], gs['ssm_b_im'] = dbr.reshape(G, STATE, GROUP_CH), dbi.reshape(G, STATE, GROUP_CH)
    pick = lambda q: q[:, ::GROUP_CH].reshape(G, STATE)
    dlr, dli, dls = _ssm_param_bwd(p['ssm_lambda_re'], p['ssm_lambda_im'], p['ssm_log_step'][:, None], pick(qr), pick(qi),
                                   s1.reshape(8, G, STATE), s2.reshape(8, G, STATE))
    gs['ssm_lambda_re'], gs['ssm_lambda_im'], gs['ssm_log_step'] = dlr, dli, dls[:, 0]
    dq, dkn, dv, gs['q_norm_g'], gs['attn_sinks'] = _attn_bwd(sv['proj'], sv['attn'], dattn, p['q_norm_g'], p['k_norm_g'],
                                                               p['attn_sinks'], attn_w, kv_w)
    dk, gs['k_norm_g'] = _knorm_bwd(sv['proj'], dkn, p['k_norm_g'], attn_w, kv_w)
    dproj = jnp.concatenate([dq.astype(BF16), dk.astype(BF16), dv.astype(BF16), du.astype(BF16)], axis=1)
    dh, = _mm("mm_in_dx", dproj, w['w_in'], 'nt', [F32])
    gb['w_in'], = _mm("mm_in_dw", sv['h'], dproj, 'tn', [BF16])
    token = reduce_grads(('w_in', 'w_glu', 'w_out'), [gb['w_in'], gb['w_glu'], gb['w_out']])
    (dx_in,), (gs['norm_mix_g'],), dx_in16 = _rms_bwd("norm_mix_bwd", [sv['x']], [p['norm_mix_g'] + token], dh, resid=dx_mid)
    return dx_in, dx_in16, gs


PACK_COLS = 1024


def _pack(arrs, rows):
    flat = jnp.concatenate([a.reshape(-1).astype(F32) for a in arrs])
    return jnp.pad(flat, (0, rows * PACK_COLS - flat.shape[0])).reshape(rows, PACK_COLS)


def _unpack(packed, shapes):
    flat = packed.reshape(-1)
    out, off = [], 0
    for s in shapes:
        n = int(np.prod(s))
        out.append(flat[off:off + n].reshape(s))
        off += n
    return out


def _pack_rows(shapes, multiple):
    n = sum(int(np.prod(s)) for s in shapes)
    rows = -(-n // PACK_COLS)
    return -(-rows // multiple) * multiple


def kernel(x, meta_tokens, norm_mix_g, w_in, q_norm_g, k_norm_g, attn_sinks, ssm_lambda_re, ssm_lambda_im, ssm_log_step, ssm_b_re, ssm_b_im, ssm_c_re, ssm_c_im, ssm_d, w_glu, b_glu, attn_out_g, ssm_out_g, w_out, norm_mlp_g, w_up, w_down, loss_target, m_meta_tokens, m_norm_mix_g, m_w_in, m_q_norm_g, m_k_norm_g, m_attn_sinks, m_ssm_lambda_re, m_ssm_lambda_im, m_ssm_log_step, m_ssm_b_re, m_ssm_b_im, m_ssm_c_re, m_ssm_c_im, m_ssm_d, m_w_glu, m_b_glu, m_attn_out_g, m_ssm_out_g, m_w_out, m_norm_mlp_g, m_w_up, m_w_down, v_meta_tokens, v_norm_mix_g, v_w_in, v_q_norm_g, v_k_norm_g, v_attn_sinks, v_ssm_lambda_re, v_ssm_lambda_im, v_ssm_log_step, v_ssm_b_re, v_ssm_b_im, v_ssm_c_re, v_ssm_c_im, v_ssm_d, v_w_glu, v_b_glu, v_attn_out_g, v_ssm_out_g, v_w_out, v_norm_mlp_g, v_w_up, v_w_down):
    args = (meta_tokens, norm_mix_g, w_in, q_norm_g, k_norm_g, attn_sinks, ssm_lambda_re, ssm_lambda_im, ssm_log_step, ssm_b_re, ssm_b_im, ssm_c_re, ssm_c_im, ssm_d, w_glu, b_glu, attn_out_g, ssm_out_g, w_out, norm_mlp_g, w_up, w_down)
    ms = (m_meta_tokens, m_norm_mix_g, m_w_in, m_q_norm_g, m_k_norm_g, m_attn_sinks, m_ssm_lambda_re, m_ssm_lambda_im, m_ssm_log_step, m_ssm_b_re, m_ssm_b_im, m_ssm_c_re, m_ssm_c_im, m_ssm_d, m_w_glu, m_b_glu, m_attn_out_g, m_ssm_out_g, m_w_out, m_norm_mlp_g, m_w_up, m_w_down)
    vs = (v_meta_tokens, v_norm_mix_g, v_w_in, v_q_norm_g, v_k_norm_g, v_attn_sinks, v_ssm_lambda_re, v_ssm_lambda_im, v_ssm_log_step, v_ssm_b_re, v_ssm_b_im, v_ssm_c_re, v_ssm_c_im, v_ssm_d, v_w_glu, v_b_glu, v_attn_out_g, v_ssm_out_g, v_w_out, v_norm_mlp_g, v_w_up, v_w_down)
    W = dict(zip(WEIGHTS, args))
    M = dict(zip(WEIGHTS, ms))
    V = dict(zip(WEIGHTS, vs))
    depth = norm_mix_g.shape[0]
    seq, D = x.shape[1], x.shape[2]
    attn_w = D // 2
    kv_w = attn_w // KV_GROUP
    dims = dict(attn_w=attn_w, kv_w=kv_w, u_off=(attn_w + 2 * kv_w) // LANES)
    small_names = [n for n in WEIGHTS if n not in BIG and n != 'meta_tokens']
    chip = 2 * lax.axis_index("x") + lax.axis_index("y")

    gathers, started = [], jnp.zeros((), F32)
    for l in range(depth):
        placed = [_ag_place("ag_place_" + n, W[n], l, COL_SHARDED[n], BF16) for n in BIG]
        groups = [_ag_copies(a, COL_SHARDED[n]) for a, n in enumerate(BIG)]
        if l == 0:
            placed = [_ag_place("ag_place_meta", meta_tokens[None], 0, True, F32)] + placed
            groups = [_ag_copies(0, True)] + [_ag_copies(a + 1, COL_SHARDED[n]) for a, n in enumerate(BIG)]
        sems, bufs, token = _split_start("ag_start_%d" % l, placed, groups)
        gathers.append(dict(zip((['meta_tokens'] if l == 0 else []) + BIG, zip(sems, bufs))))
        started = started + token[0, 0]

    def gathered(l, n, after):
        sems, buf = gathers[l][n]
        return _split_wait("ag_wait_%d_%s" % (l, n), [buf], sems, after, _ag_copies(0, n == 'meta_tokens' or COL_SHARDED[n]))[0]

    h_res = jnp.concatenate([jnp.zeros((PAD, D), F32), gathered(0, 'meta_tokens', started.reshape(1, 1)), x[0]], axis=0)
    layer_p = []
    for l in range(depth):
        p = {n: W[n][l] for n in small_names}
        for n in ('norm_mix_g', 'q_norm_g', 'k_norm_g', 'attn_sinks', 'ssm_d', 'b_glu', 'attn_out_g', 'ssm_out_g', 'norm_mlp_g'):
            p[n] = p[n][None, :]
        layer_p.append(p)
    saved = []
    for l in range(depth):
        h_res, sv = _layer_fwd(h_res, layer_p[l], functools.partial(gathered, l), dims)
        saved.append(sv)
    loss_local, dx, dx16 = _loss(h_res, loss_target[0])
    loss = lax.psum(loss_local, ("x", "y", "c"))

    small_grads = [None] * depth
    shard_grads = {}
    pending = []

    def finish(after):
        while pending:
            l_, names, st = pending.pop(0)
            mine, other = _rs_finish("%d_%s" % (l_, names[0]), st, after)
            for a, n in enumerate(names):
                shard_grads[(l_, n)] = (mine[a], other[a])

    def reduce_grads(l, names, grads):
        st = _rs_begin("%d_%s" % (l, names[0]), list(grads), [COL_SHARDED[n] for n in names])
        pending.append((l, names, st))
        return st['token'][0, 0]

    for l in reversed(range(depth)):
        dx, dx16, gs = _layer_bwd(dx, dx16, saved[l], layer_p[l], dims, functools.partial(reduce_grads, l))
        saved[l] = None
        small_grads[l] = gs
        newest = pending.pop()
        finish(dx)
        pending.append(newest)
    grad_x = dx[BLOCK:].reshape(x.shape)

    g_small = {n: jnp.stack([small_grads[l][n].reshape(W[n].shape[1:]) for l in range(depth)]) for n in small_names}
    g_shapes = [(N_META, D)] + [W[n].shape for n in small_names]
    rows = _pack_rows(g_shapes, 8 * 2 * N_CHIPS)
    packed = _pack([dx[PAD:BLOCK]] + [g_small[n] for n in small_names], rows)
    red, = _reduce_scatter("small", [packed], [False], F32)
    red_full, = _all_gather("ag_small", [red], [False])
    finish(red_full)
    g_list = _unpack(red_full, g_shapes)
    g_meta = lax.dynamic_slice_in_dim(g_list[0], chip * meta_tokens.shape[1], meta_tokens.shape[1], axis=1)
    G = dict(zip(small_names, g_list[1:]))
    G['meta_tokens'] = g_meta

    out = {}
    for n in BIG:
        out[n] = _adamw_big("adamw_" + n, [shard_grads[(l, n)][0] for l in range(depth)],
                            [shard_grads[(l, n)][1] for l in range(depth)], W[n], M[n], V[n])
    names = ['meta_tokens'] + small_names
    shapes = [W[n].shape for n in names]
    prow = _pack_rows(shapes, 8)
    d_p, m_p, v_p = _ew("adamw_small", _adam_math, [_pack([W[n] for n in names], prow), _pack([G[n] for n in names], prow),
                                                    _pack([M[n] for n in names], prow), _pack([V[n] for n in names], prow)], [F32] * 3)
    for n, d_, m_, v_ in zip(names, _unpack(d_p, shapes), _unpack(m_p, shapes), _unpack(v_p, shapes)):
        out[n] = (G[n], d_, m_, v_)
    return (loss, grad_x, *[out[n][0] for n in WEIGHTS], *[out[n][1] for n in WEIGHTS],
            *[out[n][2] for n in WEIGHTS], *[out[n][3] for n in WEIGHTS])
```

```python
import functools
import math

import numpy as np
import jax
import jax.numpy as jnp
from jax import lax
from jax.experimental import pallas as pl
from jax.experimental.pallas import tpu as pltpu

F32 = jnp.float32
BF16 = jnp.bfloat16
MESH = pl.DeviceIdType.MESH

N_META = 16
HEAD_DIM = 64
KV_GROUP = 4
GROUP_CH = 16
STATE = 64
BLOCK = 128
PAD = BLOCK - N_META
NORM_EPS = 1e-6
NEG_INF = -1e30
LANES = 128
V7X_VMEM_LIMIT_BYTES = 56 * 1024 * 1024
MM_VMEM_BUDGET_BYTES = 44 * 1024 * 1024

ADAM_LR, ADAM_B1, ADAM_B2, ADAM_EPS, ADAM_WD, ADAM_STEP = 0.001, 0.9, 0.999, 1e-08, 0.01, 10

WEIGHTS = ['meta_tokens', 'norm_mix_g', 'w_in', 'q_norm_g', 'k_norm_g', 'attn_sinks', 'ssm_lambda_re',
           'ssm_lambda_im', 'ssm_log_step', 'ssm_b_re', 'ssm_b_im', 'ssm_c_re', 'ssm_c_im', 'ssm_d', 'w_glu',
           'b_glu', 'attn_out_g', 'ssm_out_g', 'w_out', 'norm_mlp_g', 'w_up', 'w_down']
BIG = ['w_in', 'w_glu', 'w_out', 'w_up', 'w_down']
COL_SHARDED = {'w_in': True, 'w_glu': False, 'w_out': False, 'w_up': True, 'w_down': False}
N_CHIPS = 4


def _pick(n, cands):
    for c in cands:
        if c <= n and n % c == 0:
            return c
    return n


def _params(sem):
    return pltpu.CompilerParams(dimension_semantics=sem, vmem_limit_bytes=V7X_VMEM_LIMIT_BYTES)


def _pcall(body, **kw):
    return pl.pallas_call(body, **kw)


def _sds(shape, dtype):
    return jax.ShapeDtypeStruct(shape, dtype)


_DIMS = {'nn': ((1,), (0,)), 'nt': ((1,), (1,)), 'tn': ((0,), (0,))}


def _mm(name, a, b, mode, out_dtypes, extras=(), epi=None):
    if mode == 'nn':
        (M, K), (_, N) = a.shape, b.shape
    elif mode == 'nt':
        (M, K), (N, _) = a.shape, b.shape
    else:
        (K, M), (_, N) = a.shape, b.shape
    tile_bytes = 4 * len([k for k, _ in extras if k == 'tile']) + sum(jnp.dtype(d).itemsize for d in out_dtypes)

    def fits(tm, tn, tk):
        need = 2 * tm * tk * a.dtype.itemsize + 2 * tk * tn * b.dtype.itemsize + 4 * tm * tn + 2 * tm * tn * tile_bytes
        return need <= MM_VMEM_BUDGET_BYTES

    if mode == 'tn':
        tm, tk_cands = _pick(M, (1024, 512, 256, 128)), (1408, 704, 384, 128)
    else:
        tm, tk_cands = _pick(M, (1408, 704, 384, 128)), (2048, 1024, 512, 256, 128)
    tk_cands = [t for t in tk_cands if t <= K and K % t == 0] or [K]
    tn_cands = [t for t in (2048, 1280, 1024, 640, 512, 256, 128) if t <= N and N % t == 0] or [N]
    if mode != 'tn' and tk_cands[0] == K and a.dtype == BF16:
        tk_cands = tk_cands[:1]
    tn, tk = next(((tn_, tk_) for tn_ in tn_cands for tk_ in tk_cands if fits(tm, tn_, tk_)), (tn_cands[-1], tk_cands[-1]))
    nk = K // tk
    a_spec = pl.BlockSpec((tk, tm), lambda i, j, k: (k, i)) if mode == 'tn' else pl.BlockSpec((tm, tk), lambda i, j, k: (i, k))
    b_spec = pl.BlockSpec((tn, tk), lambda i, j, k: (j, k)) if mode == 'nt' else pl.BlockSpec((tk, tn), lambda i, j, k: (k, j))
    ex_specs = [pl.BlockSpec((tm, tn), lambda i, j, k: (i, j)) if kind == 'tile' else pl.BlockSpec((1, tn), lambda i, j, k: (0, j))
                for kind, _ in extras]
    ne, no = len(extras), len(out_dtypes)
    dims = (_DIMS[mode], ((), ()))

    def body(a_ref, b_ref, *rest):
        ex, outs, acc = rest[:ne], rest[ne:ne + no], rest[ne + no]
        k = pl.program_id(2)

        @pl.when(k == 0)
        def _():
            acc[...] = jnp.zeros_like(acc)

        acc[...] += lax.dot_general(a_ref[...].astype(BF16), b_ref[...].astype(BF16), dims, preferred_element_type=F32)

        @pl.when(k == nk - 1)
        def _():
            r = acc[...]
            res = epi(r, *[e[...] for e in ex]) if epi is not None else (r,)
            for o, v in zip(outs, res):
                o[...] = v.astype(o.dtype)

    outs = _pcall(
        body, name=name, grid=(M // tm, N // tn, nk),
        in_specs=[a_spec, b_spec] + ex_specs,
        out_specs=[pl.BlockSpec((tm, tn), lambda i, j, k: (i, j)) for _ in out_dtypes],
        out_shape=[_sds((M, N), d) for d in out_dtypes],
        scratch_shapes=[pltpu.VMEM((tm, tn), F32)],
        compiler_params=_params(("parallel", "parallel", "arbitrary")),
    )(a, b, *[e for _, e in extras])
    return outs


def _ew(name, fn, ins, out_dtypes):
    R, C = ins[0].shape
    tr = _pick(R, tuple(t for t in (1024, 512, 256, 128, 64, 32, 16, 8) if t * C <= 512 * 1024) or (8,))
    n_in = len(ins)

    def body(*refs):
        res = fn(*[r[...] for r in refs[:n_in]])
        for o, v in zip(refs[n_in:], res):
            o[...] = v.astype(o.dtype)

    spec = pl.BlockSpec((tr, C), lambda i: (i, 0))
    return _pcall(body, name=name, grid=(R // tr,), in_specs=[spec] * n_in, out_specs=[spec] * len(out_dtypes),
                  out_shape=[_sds((R, C), d) for d in out_dtypes], compiler_params=_params(("parallel",)))(*ins)


def _adam_math(w, g, m, v):
    m = ADAM_B1 * m + (1.0 - ADAM_B1) * g
    v = ADAM_B2 * v + (1.0 - ADAM_B2) * (g * g)
    m_hat = m / (1.0 - ADAM_B1 ** ADAM_STEP)
    v_hat = v / (1.0 - ADAM_B2 ** ADAM_STEP)
    delta = -ADAM_LR * (m_hat / (jnp.sqrt(v_hat) + ADAM_EPS) + ADAM_WD * w)
    return delta, m, v


def _rms_fwd(name, xs, gs, out_dtype):
    L = xs[0].shape[0]
    ws = [x.shape[1] for x in xs]
    n = len(xs)
    tr = _pick(L, (384, 256, 128))

    def body(*refs):
        o = refs[2 * n]
        off = 0
        for i in range(n):
            x = refs[i][...]
            r = lax.rsqrt(jnp.mean(x * x, axis=-1, keepdims=True) + NORM_EPS)
            o[:, off:off + ws[i]] = ((x * r) * refs[n + i][...]).astype(o.dtype)
            off += ws[i]

    return _pcall(body, name=name, grid=(L // tr,),
                  in_specs=[pl.BlockSpec((tr, w), lambda i: (i, 0)) for w in ws] + [pl.BlockSpec((1, w), lambda i: (0, 0)) for w in ws],
                  out_specs=pl.BlockSpec((tr, sum(ws)), lambda i: (i, 0)), out_shape=_sds((L, sum(ws)), out_dtype),
                  compiler_params=_params(("parallel",)))(*xs, *gs)


def _rms_bwd(name, xs, gs, dy, resid=None):
    L = xs[0].shape[0]
    ws = [x.shape[1] for x in xs]
    n = len(xs)
    tr = _pick(L, (384, 256, 128))
    has_res = resid is not None

    def body(*refs):
        x_refs, g_refs, dy_ref = refs[:n], refs[n:2 * n], refs[2 * n]
        p = 2 * n + 1
        res_ref = refs[p] if has_res else None
        p += 1 if has_res else 0
        dx_refs, dg_refs = refs[p:p + n], refs[p + n:p + 2 * n]
        dx16_ref = refs[p + 2 * n] if has_res else None
        first = pl.program_id(0) == 0
        off = 0
        for i in range(n):
            x = x_refs[i][...]
            d = dy_ref[:, off:off + ws[i]]
            r = lax.rsqrt(jnp.mean(x * x, axis=-1, keepdims=True) + NORM_EPS)
            xh = x * r
            dg = jnp.sum(d * xh, axis=0, keepdims=True)

            @pl.when(first)
            def _(i=i):
                dg_refs[i][...] = jnp.zeros_like(dg_refs[i])

            dg_refs[i][...] += dg
            dyg = d * g_refs[i][...]
            dx = r * (dyg - xh * jnp.mean(dyg * xh, axis=-1, keepdims=True))
            if has_res:
                dx = dx + res_ref[...]
                dx16_ref[...] = dx.astype(BF16)
            dx_refs[i][...] = dx
            off += ws[i]

    in_specs = ([pl.BlockSpec((tr, w), lambda i: (i, 0)) for w in ws] + [pl.BlockSpec((1, w), lambda i: (0, 0)) for w in ws]
                + [pl.BlockSpec((tr, sum(ws)), lambda i: (i, 0))])
    ins = list(xs) + list(gs) + [dy]
    if has_res:
        in_specs.append(pl.BlockSpec((tr, ws[0]), lambda i: (i, 0)))
        ins.append(resid)
    out_specs = [pl.BlockSpec((tr, w), lambda i: (i, 0)) for w in ws] + [pl.BlockSpec((1, w), lambda i: (0, 0)) for w in ws]
    out_shape = [_sds((L, w), F32) for w in ws] + [_sds((1, w), F32) for w in ws]
    if has_res:
        out_specs.append(pl.BlockSpec((tr, ws[0]), lambda i: (i, 0)))
        out_shape.append(_sds((L, ws[0]), BF16))
    outs = _pcall(body, name=name, grid=(L // tr,), in_specs=in_specs, out_specs=out_specs, out_shape=out_shape,
                  compiler_params=_params(("arbitrary",)))(*ins)
    return (outs[:n], outs[n:2 * n], outs[2 * n]) if has_res else (outs[:n], outs[n:])


def _loss(xl, target):
    Lp, D = xl.shape

    def body(x_ref, t_ref, dy_ref, dy16_ref, loss_ref):
        n = pl.program_id(0)

        @pl.when(n == 0)
        def _():
            loss_ref[...] = jnp.zeros_like(loss_ref)
            dy_ref[...] = jnp.zeros_like(dy_ref)
            dy16_ref[...] = jnp.zeros_like(dy16_ref)

        @pl.when(n > 0)
        def _():
            err = x_ref[...] - t_ref[...]
            dy = err * (1.0 / D)
            dy_ref[...] = dy
            dy16_ref[...] = dy.astype(BF16)
            loss_ref[...] += jnp.sum(err * err) * (0.5 / D)

    blk = pl.BlockSpec((BLOCK, D), lambda n: (n, 0))
    dy, dy16, loss = _pcall(body, name="loss_head", grid=(Lp // BLOCK,),
                            in_specs=[blk, pl.BlockSpec((BLOCK, D), lambda n: (jnp.maximum(n - 1, 0), 0))],
                            out_specs=[blk, blk, pl.BlockSpec((8, LANES), lambda n: (0, 0))],
                            out_shape=[_sds((Lp, D), F32), _sds((Lp, D), BF16), _sds((8, LANES), F32)],
                            compiler_params=_params(("arbitrary",)))(xl, target)
    return loss[0, 0], dy, dy16


GROUP_ROWS = KV_GROUP * BLOCK


def _attn_mask_dist(n):
    r = lax.broadcasted_iota(jnp.int32, (GROUP_ROWS, 3 * BLOCK), 0)
    i = r & (BLOCK - 1)
    j = lax.broadcasted_iota(jnp.int32, (GROUP_ROWS, 3 * BLOCK), 1)
    in_band = j < 2 * BLOCK
    band = in_band & (j > i) & (j <= i + BLOCK) & (j >= 2 * BLOCK - BLOCK * n)
    jm = j - 2 * BLOCK
    meta = (~in_band) & (jm >= PAD) & (jm <= BLOCK * n + i)
    dist = jnp.where(in_band, BLOCK + i - j, BLOCK * n + i - jm).astype(F32)
    return band | meta, dist


def _head_norm(x, g):
    r = lax.rsqrt(jnp.mean(x * x, axis=-1, keepdims=True) + NORM_EPS)
    return (x * r) * g, r


def _attn_specs(attn_w, kv_w):
    kb = attn_w // kv_w
    q_spec = pl.BlockSpec((BLOCK, attn_w), lambda n: (n, 0))

    def kv(col):
        return [pl.BlockSpec((BLOCK, kv_w), lambda n: (jnp.maximum(n - 1, 0), col)),
                pl.BlockSpec((BLOCK, kv_w), lambda n: (n, col)),
                pl.BlockSpec((BLOCK, kv_w), lambda n: (0, col))]

    return q_spec, kv(kb), kv(kb + 1)


def _slopes(n_heads):
    return [2.0 ** (-8.0 * (h + 1) / n_heads) for h in range(n_heads)]


def _head_slice(h):
    return slice(h * HEAD_DIM, (h + 1) * HEAD_DIM)


def _stack_heads(ref, kh):
    return jnp.concatenate([ref[:, _head_slice(kh * KV_GROUP + g)] for g in range(KV_GROUP)], axis=0)


def _group_column(vals):
    return jnp.concatenate([jnp.broadcast_to(v, (BLOCK, 1)) for v in vals], axis=0)


def _group_inputs(kh, slopes, q_ref, kp, kc, km, vp, vc, vm, gq_ref, gk_ref, sk_ref):
    cs = _head_slice(kh)
    kn, _ = _head_norm(jnp.concatenate([kp[:, cs], kc[:, cs], km[:, cs]], axis=0), gk_ref[...])
    vcat = jnp.concatenate([vp[:, cs], vc[:, cs], vm[:, cs]], axis=0).astype(BF16)
    q = _stack_heads(q_ref, kh)
    qn, rq = _head_norm(q, gq_ref[...])
    heads = range(kh * KV_GROUP, (kh + 1) * KV_GROUP)
    slope = _group_column([jnp.full((1, 1), slopes[h], F32) for h in heads])
    sink = _group_column([sk_ref[0:1, h:h + 1] for h in heads])
    return q, qn, rq, kn, vcat, slope, sink


def _scores(qn, kn, slope, sink, mask, dist):
    s = lax.dot_general(qn.astype(BF16), kn.astype(BF16), (((1,), (1,)), ((), ())), preferred_element_type=F32)
    s = s * (1.0 / math.sqrt(HEAD_DIM)) - slope * dist
    s = jnp.where(mask, s, NEG_INF)
    m = jnp.maximum(jnp.max(s, axis=-1, keepdims=True), sink)
    p = jnp.exp(s - m)
    ps = jnp.exp(sink - m)
    inv = 1.0 / (jnp.sum(p, axis=-1, keepdims=True) + ps)
    return p * inv, ps * inv


def _attn_fwd(proj, gq, gk, sinks, attn_w, kv_w):
    Lp = proj.shape[0]
    n_heads, n_kv = attn_w // HEAD_DIM, kv_w // HEAD_DIM
    slopes = _slopes(n_heads)
    q_spec, k_specs, v_specs = _attn_specs(attn_w, kv_w)

    def body(q_ref, kp, kc, km, vp, vc, vm, gq_ref, gk_ref, sk_ref, o_ref):
        mask, dist = _attn_mask_dist(pl.program_id(0))
        for kh in range(n_kv):
            _, qn, _, kn, vcat, slope, sink = _group_inputs(kh, slopes, q_ref, kp, kc, km, vp, vc, vm, gq_ref, gk_ref, sk_ref)
            p, _ = _scores(qn, kn, slope, sink, mask, dist)
            o = jnp.dot(p.astype(BF16), vcat, preferred_element_type=F32)
            for g in range(KV_GROUP):
                o_ref[:, _head_slice(kh * KV_GROUP + g)] = o[g * BLOCK:(g + 1) * BLOCK]

    small = lambda w: pl.BlockSpec((1, w), lambda n: (0, 0))
    return _pcall(body, name="attn_fwd", grid=(Lp // BLOCK,),
                  in_specs=[q_spec] + k_specs + v_specs + [small(HEAD_DIM), small(HEAD_DIM), small(n_heads)],
                  out_specs=pl.BlockSpec((BLOCK, attn_w), lambda n: (n, 0)), out_shape=_sds((Lp, attn_w), F32),
                  compiler_params=_params(("parallel",)))(proj, proj, proj, proj, proj, proj, proj, gq, gk, sinks)


def _attn_bwd(proj, attn, dattn, gq, gk, sinks, attn_w, kv_w):
    Lp = proj.shape[0]
    n_heads, n_kv = attn_w // HEAD_DIM, kv_w // HEAD_DIM
    slopes = _slopes(n_heads)
    q_spec, k_specs, v_specs = _attn_specs(attn_w, kv_w)
    scale = 1.0 / math.sqrt(HEAD_DIM)
    tn_dims = (((0,), (0,)), ((), ()))

    def body(q_ref, kp, kc, km, vp, vc, vm, o_ref, do_ref, gq_ref, gk_ref, sk_ref, dq_ref, dk_ref, dv_ref, dgq_ref, dsk_ref):
        n = pl.program_id(0)

        @pl.when(n == 0)
        def _():
            dk_ref[...] = jnp.zeros_like(dk_ref)
            dv_ref[...] = jnp.zeros_like(dv_ref)
            dgq_ref[...] = jnp.zeros_like(dgq_ref)
            dsk_ref[...] = jnp.zeros_like(dsk_ref)

        mask, dist = _attn_mask_dist(n)
        lane = lax.broadcasted_iota(jnp.int32, (1, n_heads), 1)
        rows_prev = pl.ds(pl.multiple_of(jnp.maximum(n - 1, 0) * BLOCK, BLOCK), BLOCK)
        rows_cur = pl.ds(pl.multiple_of(n * BLOCK, BLOCK), BLOCK)
        rows_meta = pl.ds(0, BLOCK)
        dgq = jnp.zeros((1, HEAD_DIM), F32)
        dsk = jnp.zeros((1, n_heads), F32)
        for kh in range(n_kv):
            cs = _head_slice(kh)
            q, qn, rq, kn, vcat, slope, sink = _group_inputs(kh, slopes, q_ref, kp, kc, km, vp, vc, vm, gq_ref, gk_ref, sk_ref)
            p, ps = _scores(qn, kn, slope, sink, mask, dist)
            do = _stack_heads(do_ref, kh)
            dd = jnp.sum(do * _stack_heads(o_ref, kh), axis=-1, keepdims=True)
            do16 = do.astype(BF16)
            dp = lax.dot_general(do16, vcat, (((1,), (1,)), ((), ())), preferred_element_type=F32)
            ds16 = (p * (dp - dd)).astype(BF16)
            dsink = -ps * dd
            for g in range(KV_GROUP):
                dsk = dsk + jnp.where(lane == kh * KV_GROUP + g, jnp.sum(dsink[g * BLOCK:(g + 1) * BLOCK]), 0.0)
            dqn = jnp.dot(ds16, kn.astype(BF16), preferred_element_type=F32) * scale
            dkn = lax.dot_general(ds16, qn.astype(BF16), tn_dims, preferred_element_type=F32) * scale
            dvc = lax.dot_general(p.astype(BF16), do16, tn_dims, preferred_element_type=F32)
            xh = q * rq
            dgq = dgq + jnp.sum(dqn * xh, axis=0, keepdims=True)
            dyg = dqn * gq_ref[...]
            dq = rq * (dyg - xh * jnp.mean(dyg * xh, axis=-1, keepdims=True))
            for g in range(KV_GROUP):
                dq_ref[:, _head_slice(kh * KV_GROUP + g)] = dq[g * BLOCK:(g + 1) * BLOCK]
            for part, rows in enumerate((rows_prev, rows_cur, rows_meta)):
                ps_ = slice(part * BLOCK, (part + 1) * BLOCK)
                dk_ref[rows, cs] += dkn[ps_]
                dv_ref[rows, cs] += dvc[ps_]
        dgq_ref[...] += dgq
        dsk_ref[...] += dsk

    small = lambda w: pl.BlockSpec((1, w), lambda n: (0, 0))
    blk = pl.BlockSpec((BLOCK, attn_w), lambda n: (n, 0))
    whole = pl.BlockSpec((Lp, kv_w), lambda n: (0, 0))
    return _pcall(body, name="attn_bwd", grid=(Lp // BLOCK,),
                  in_specs=[q_spec] + k_specs + v_specs + [blk, blk, small(HEAD_DIM), small(HEAD_DIM), small(n_heads)],
                  out_specs=[blk, whole, whole, small(HEAD_DIM), small(n_heads)],
                  out_shape=[_sds((Lp, attn_w), F32), _sds((Lp, kv_w), F32), _sds((Lp, kv_w), F32),
                             _sds((1, HEAD_DIM), F32), _sds((1, n_heads), F32)],
                  compiler_params=_params(("arbitrary",)))(proj, proj, proj, proj, proj, proj, proj, attn, dattn, gq, gk, sinks)


def _knorm_bwd(proj, dkn, gk, attn_w, kv_w):
    Lp = proj.shape[0]
    n_kv = kv_w // HEAD_DIM
    tr = _pick(Lp, (384, 256, 128))

    def body(k_ref, d_ref, g_ref, dk_ref, dg_ref):
        @pl.when(pl.program_id(0) == 0)
        def _():
            dg_ref[...] = jnp.zeros_like(dg_ref)

        dg = jnp.zeros((1, HEAD_DIM), F32)
        for kh in range(n_kv):
            cs = slice(kh * HEAD_DIM, (kh + 1) * HEAD_DIM)
            x = k_ref[:, cs]
            d = d_ref[:, cs]
            r = lax.rsqrt(jnp.mean(x * x, axis=-1, keepdims=True) + NORM_EPS)
            xh = x * r
            dg = dg + jnp.sum(d * xh, axis=0, keepdims=True)
            dyg = d * g_ref[...]
            dk_ref[:, cs] = r * (dyg - xh * jnp.mean(dyg * xh, axis=-1, keepdims=True))
        dg_ref[...] += dg

    return _pcall(body, name="knorm_bwd", grid=(Lp // tr,),
                  in_specs=[pl.BlockSpec((tr, kv_w), lambda i: (i, attn_w // kv_w)), pl.BlockSpec((tr, kv_w), lambda i: (i, 0)),
                            pl.BlockSpec((1, HEAD_DIM), lambda i: (0, 0))],
                  out_specs=[pl.BlockSpec((tr, kv_w), lambda i: (i, 0)), pl.BlockSpec((1, HEAD_DIM), lambda i: (0, 0))],
                  out_shape=[_sds((Lp, kv_w), F32), _sds((1, HEAD_DIM), F32)],
                  compiler_params=_params(("arbitrary",)))(proj, dkn, gk)


def _ssm_bbar(lr, li, ls, br, bi):
    def fn(lr, li, ls, br, bi):
        fr, fi = _zoh_factor(lr, li, ls)
        return fr * br - fi * bi, fr * bi + fi * br

    return _ew("ssm_bbar", fn, [lr, li, ls, br, bi], [F32, F32])


def _lam_bar(lr, li, ls):
    dl = jnp.exp(ls)
    e = jnp.exp(lr * dl)
    return e * jnp.cos(li * dl), e * jnp.sin(li * dl), dl


def _zoh_factor(lr, li, ls):
    ar, ai, _ = _lam_bar(lr, li, ls)
    n2 = lr * lr + li * li
    ivr, ivi = lr / n2, -li / n2
    return (ar - 1.0) * ivr - ai * ivi, (ar - 1.0) * ivi + ai * ivr


SCAN_SHIFTS = (1, 2, 4)


def _ssm_tables(lr, li, ls):
    Wx = lr.shape[1]

    def body(lr_ref, li_ref, ls_ref, tf_ref, tr_ref):
        dl = jnp.exp(ls_ref[...])
        zr, zi = lr_ref[...] * dl, li_ref[...] * dl
        row = lax.broadcasted_iota(jnp.int32, (8, Wx), 0)

        def power(kf):
            e = jnp.exp(kf * zr)
            return e * jnp.cos(kf * zi), e * jnp.sin(kf * zi)

        for ref, rev in ((tf_ref, False), (tr_ref, True)):
            sgn = -1.0 if rev else 1.0
            for k, d in enumerate(SCAN_SHIFTS):
                ar, ai = power(jnp.full((8, Wx), float(d), F32))
                keep = (row < 8 - d) if rev else (row >= d)
                ref[k] = jnp.where(keep, ar, 0.0)
                ref[4 + k] = jnp.where(keep, sgn * ai, 0.0)
            pr, pi = power(((8 - row) if rev else (row + 1)).astype(F32))
            ref[3] = pr
            ref[7] = sgn * pi

    full = pl.BlockSpec((1, Wx), lambda: (0, 0))
    tab = pl.BlockSpec((8, 8, Wx), lambda: (0, 0, 0))
    return _pcall(body, name="ssm_tables", in_specs=[full] * 3, out_specs=[tab, tab],
                  out_shape=[_sds((8, 8, Wx), F32)] * 2,
                  compiler_params=pltpu.CompilerParams(vmem_limit_bytes=V7X_VMEM_LIMIT_BYTES))(lr, li, ls)


def _scan(name, br, bi, tab, reverse, states=None):
    L, Wx = br.shape
    TB = _pick(L, (384, 256, 128))
    CW = _pick(Wx, (1024, 512, 256, 128))
    nT, nG = L // TB, TB // 8

    def body(*refs):
        if reverse:
            br_ref, bi_ref, xr_ref, xi_ref, tab_ref, or_ref, oi_ref, s1_ref, s2_ref, cr_ref, ci_ref = refs
        else:
            br_ref, bi_ref, tab_ref, or_ref, oi_ref, cr_ref, ci_ref = refs

        @pl.when(pl.program_id(1) == 0)
        def _():
            cr_ref[...] = jnp.zeros_like(cr_ref)
            ci_ref[...] = jnp.zeros_like(ci_ref)
            if reverse:
                s1_ref[...] = jnp.zeros_like(s1_ref)
                s2_ref[...] = jnp.zeros_like(s2_ref)

        def step(q, carry):
            cr, ci = carry[0], carry[1]
            g = (nG - 1 - q) if reverse else q
            rows = pl.ds(pl.multiple_of(g * 8, 8), 8)
            b_r, b_i = br_ref[rows, :], bi_ref[rows, :]
            sr, si = b_r, b_i
            for k, d in enumerate(SCAN_SHIFTS):
                mr, mi = tab_ref[k], tab_ref[4 + k]
                sh = (8 - d) if reverse else d
                pr, pi = pltpu.roll(sr, sh, 0), pltpu.roll(si, sh, 0)
                sr, si = sr + mr * pr - mi * pi, si + mr * pi + mi * pr
            pwr, pwi = tab_ref[3], tab_ref[7]
            xr = sr + pwr * cr - pwi * ci
            xi = si + pwr * ci + pwi * cr
            or_ref[rows, :] = xr
            oi_ref[rows, :] = xi
            row = 0 if reverse else 7
            out = (jnp.broadcast_to(xr[row:row + 1, :], xr.shape), jnp.broadcast_to(xi[row:row + 1, :], xi.shape))
            if reverse:
                hr, hi = xr - b_r, xi - b_i
                st_r, st_i = xr_ref[rows, :], xi_ref[rows, :]
                out = out + (carry[2] + hr * st_r + hi * st_i, carry[3] + hi * st_r - hr * st_i)
            return out

        init = (cr_ref[...], ci_ref[...])
        if reverse:
            init = init + (jnp.zeros((8, CW), F32), jnp.zeros((8, CW), F32))
        fin = lax.fori_loop(0, nG, step, init, unroll=2)
        cr_ref[...] = fin[0]
        ci_ref[...] = fin[1]
        if reverse:
            s1_ref[...] += fin[2]
            s2_ref[...] += fin[3]

    tmap = (lambda j, t: (nT - 1 - t, j)) if reverse else (lambda j, t: (t, j))
    blk = pl.BlockSpec((TB, CW), tmap)
    tab_spec = pl.BlockSpec((8, 8, CW), lambda j, t: (0, 0, j))
    sum_spec = pl.BlockSpec((8, CW), lambda j, t: (0, j))
    ins = [br, bi] + (list(states) if reverse else []) + [tab]
    in_specs = [blk, blk] + ([blk, blk] if reverse else []) + [tab_spec]
    out_specs = [blk, blk] + ([sum_spec, sum_spec] if reverse else [])
    out_shape = [_sds((L, Wx), F32)] * 2 + ([_sds((8, Wx), F32)] * 2 if reverse else [])
    return _pcall(body, name=name, grid=(Wx // CW, nT), in_specs=in_specs, out_specs=out_specs, out_shape=out_shape,
                  scratch_shapes=[pltpu.VMEM((8, CW), F32), pltpu.VMEM((8, CW), F32)],
                  compiler_params=_params(("parallel", "arbitrary")))(*ins)


def _row_tile(L):
    return _pick(L, (1408, 704, 384, 128))


TILES_PER_BLOCK = 4


def _blockproj(name, src, off, w_r, w_i):
    L = src.shape[0]
    T = w_r.shape[0]
    tm = _row_tile(L)
    wide = TILES_PER_BLOCK * LANES

    def body(s_ref, wr_ref, wi_ref, or_ref, oi_ref):
        s = s_ref[...].astype(BF16)
        for k in range(TILES_PER_BLOCK):
            cols = slice(k * LANES, (k + 1) * LANES)
            or_ref[:, cols] = jnp.dot(s, wr_ref[k], preferred_element_type=F32)
            oi_ref[:, cols] = jnp.dot(s, wi_ref[k], preferred_element_type=F32)

    w_spec = pl.BlockSpec((TILES_PER_BLOCK, LANES, LANES), lambda i, q: (q, 0, 0))
    o_spec = pl.BlockSpec((tm, wide), lambda i, q: (i, q))
    return _pcall(body, name=name, grid=(L // tm, T // TILES_PER_BLOCK),
                  in_specs=[pl.BlockSpec((tm, LANES), lambda i, q: (i, off + q)), w_spec, w_spec],
                  out_specs=[o_spec, o_spec], out_shape=[_sds((L, T * LANES), F32)] * 2,
                  compiler_params=_params(("parallel", "arbitrary")))(src, w_r, w_i)


def _blockproj_grad(name, src, off, gr, gi):
    L = src.shape[0]
    T = gr.shape[1] // LANES
    tm = _row_tile(L)
    wide = TILES_PER_BLOCK * LANES
    tn_dims = (((0,), (0,)), ((), ()))

    def body(s_ref, gr_ref, gi_ref, or_ref, oi_ref):
        @pl.when(pl.program_id(1) == 0)
        def _():
            or_ref[...] = jnp.zeros_like(or_ref)
            oi_ref[...] = jnp.zeros_like(oi_ref)

        s = s_ref[...].astype(BF16)
        for k in range(TILES_PER_BLOCK):
            cols = slice(k * LANES, (k + 1) * LANES)
            or_ref[k] += lax.dot_general(s, gr_ref[:, cols].astype(BF16), tn_dims, preferred_element_type=F32)
            oi_ref[k] += lax.dot_general(s, gi_ref[:, cols].astype(BF16), tn_dims, preferred_element_type=F32)

    g_spec = pl.BlockSpec((tm, wide), lambda q, i: (i, q))
    o_spec = pl.BlockSpec((TILES_PER_BLOCK, LANES, LANES), lambda q, i: (q, 0, 0))
    return _pcall(body, name=name, grid=(T // TILES_PER_BLOCK, L // tm),
                  in_specs=[pl.BlockSpec((tm, LANES), lambda q, i: (i, off + q)), g_spec, g_spec],
                  out_specs=[o_spec, o_spec], out_shape=[_sds((T, LANES, LANES), F32)] * 2,
                  compiler_params=_params(("parallel", "arbitrary")))(src, gr, gi)


def _gelu(y):
    k = math.sqrt(2.0 / math.pi)
    return 0.5 * y * (1.0 + jnp.tanh(k * (y + 0.044715 * (y * y * y))))


def _gelu_grad(y):
    k = math.sqrt(2.0 / math.pi)
    t = jnp.tanh(k * (y + 0.044715 * (y * y * y)))
    return 0.5 * (1.0 + t) + 0.5 * y * (1.0 - t * t) * (k * (1.0 + 3 * 0.044715 * (y * y)))


def _ssm_out(xr, xi, w_r, w_i, proj, u_off, dvec):
    L = xr.shape[0]
    J = w_r.shape[0]
    SW = w_r.shape[1]
    tm = _row_tile(L)

    def body(xr_ref, xi_ref, wr_ref, wi_ref, u_ref, d_ref, y_ref, gl_ref):
        acc = jnp.dot(xr_ref[...].astype(BF16), wr_ref[...], preferred_element_type=F32)
        acc += jnp.dot(xi_ref[...].astype(BF16), wi_ref[...], preferred_element_type=F32)
        y = acc + d_ref[...] * u_ref[...]
        y_ref[...] = y
        gl_ref[...] = _gelu(y)

    x_spec = pl.BlockSpec((tm, SW), lambda j, i: (i, j))
    w_spec = pl.BlockSpec((None, SW, LANES), lambda j, i: (j, 0, 0))
    o_spec = pl.BlockSpec((tm, LANES), lambda j, i: (i, j))
    return _pcall(body, name="ssm_out", grid=(J, L // tm),
                  in_specs=[x_spec, x_spec, w_spec, w_spec, pl.BlockSpec((tm, LANES), lambda j, i: (i, u_off + j)),
                            pl.BlockSpec((1, LANES), lambda j, i: (0, j))],
                  out_specs=[o_spec, o_spec], out_shape=[_sds((L, J * LANES), F32)] * 2,
                  compiler_params=_params(("parallel", "parallel")))(xr, xi, w_r, w_i, proj, dvec)


def _ssm_du(gr, gi, w_r, w_i, dy, proj, u_off, dvec):
    L = gr.shape[0]
    J = w_r.shape[0]
    SW = w_r.shape[1]
    tm = _row_tile(L)

    def body(gr_ref, gi_ref, wr_ref, wi_ref, dy_ref, u_ref, d_ref, du_ref, dd_ref):
        i = pl.program_id(1)

        @pl.when(i == 0)
        def _():
            dd_ref[...] = jnp.zeros_like(dd_ref)

        acc = jnp.dot(gr_ref[...].astype(BF16), wr_ref[...], preferred_element_type=F32)
        acc += jnp.dot(gi_ref[...].astype(BF16), wi_ref[...], preferred_element_type=F32)
        dy = dy_ref[...]
        row = lax.broadcasted_iota(jnp.int32, (tm, LANES), 0) + i * tm
        du_ref[...] = jnp.where(row >= PAD, acc + d_ref[...] * dy, 0.0)
        dd_ref[...] += jnp.sum(dy * u_ref[...], axis=0, keepdims=True)

    x_spec = pl.BlockSpec((tm, SW), lambda j, i: (i, j))
    w_spec = pl.BlockSpec((None, SW, LANES), lambda j, i: (j, 0, 0))
    o_spec = pl.BlockSpec((tm, LANES), lambda j, i: (i, j))
    vec = pl.BlockSpec((1, LANES), lambda j, i: (0, j))
    return _pcall(body, name="ssm_du", grid=(J, L // tm),
                  in_specs=[x_spec, x_spec, w_spec, w_spec, o_spec, pl.BlockSpec((tm, LANES), lambda j, i: (i, u_off + j)), vec],
                  out_specs=[o_spec, vec], out_shape=[_sds((L, J * LANES), F32), _sds((1, J * LANES), F32)],
                  compiler_params=_params(("parallel", "arbitrary")))(gr, gi, w_r, w_i, dy, proj, dvec)


def _ssm_dc(xr, xi, dy, SW):
    L = xr.shape[0]
    J = dy.shape[1] // LANES
    tm = _row_tile(L)
    tn_dims = (((0,), (0,)), ((), ()))

    def body(xr_ref, xi_ref, dy_ref, or_ref, oi_ref):
        @pl.when(pl.program_id(1) == 0)
        def _():
            or_ref[...] = jnp.zeros_like(or_ref)
            oi_ref[...] = jnp.zeros_like(oi_ref)

        d = dy_ref[...].astype(BF16)
        or_ref[...] += lax.dot_general(xr_ref[...].astype(BF16), d, tn_dims, preferred_element_type=F32)
        oi_ref[...] += lax.dot_general(xi_ref[...].astype(BF16), d, tn_dims, preferred_element_type=F32)

    x_spec = pl.BlockSpec((tm, SW), lambda j, i: (i, j))
    o_spec = pl.BlockSpec((None, SW, LANES), lambda j, i: (j, 0, 0))
    return _pcall(body, name="ssm_dc", grid=(J, L // tm),
                  in_specs=[x_spec, x_spec, pl.BlockSpec((tm, LANES), lambda j, i: (i, j))],
                  out_specs=[o_spec, o_spec], out_shape=[_sds((J, SW, LANES), F32)] * 2,
                  compiler_params=_params(("parallel", "arbitrary")))(xr, xi, dy)


def _glu_dz(ds, gl, z):
    L, W = ds.shape
    tr = _pick(L, (384, 256, 128))

    def body(ds_ref, gl_ref, z_ref, dz_ref, db_ref):
        @pl.when(pl.program_id(0) == 0)
        def _():
            db_ref[...] = jnp.zeros_like(db_ref)

        sg = jax.nn.sigmoid(z_ref[...])
        dz = ds_ref[...] * gl_ref[...] * (sg * (1.0 - sg))
        dz_ref[...] = dz.astype(BF16)
        db_ref[...] += jnp.sum(dz, axis=0, keepdims=True)

    spec = pl.BlockSpec((tr, W), lambda i: (i, 0))
    vec = pl.BlockSpec((1, W), lambda i: (0, 0))
    return _pcall(body, name="glu_dz", grid=(L // tr,), in_specs=[spec] * 3, out_specs=[spec, vec],
                  out_shape=[_sds((L, W), BF16), _sds((1, W), F32)], compiler_params=_params(("arbitrary",)))(ds, gl, z)


def _ssm_param_bwd_flat(lr, li, ls, br, bi, dbbr, dbbi):
    def seg_sum(x):
        for s in (8, 4, 2, 1):
            x = x + pltpu.roll(x, LANES - s, 1)
        return x

    def fn(lr, li, ls, br, bi, dbbr, dbbi):
        fr, fi = _zoh_factor(lr, li, ls)
        return (fr * dbbr + fi * dbbi, fr * dbbi - fi * dbbr,
                seg_sum(br * dbbr + bi * dbbi), seg_sum(br * dbbi - bi * dbbr))

    return _ew("ssm_param_bwd_flat", fn, [lr, li, ls, br, bi, dbbr, dbbi], [F32] * 4)


def _ssm_param_bwd(lr, li, ls, dfr, dfi, s1, s2):
    G, P = lr.shape

    def body(lr_ref, li_ref, ls_ref, dfr_ref, dfi_ref, s1_ref, s2_ref, dlr_ref, dli_ref, dls_ref):
        lr, li = lr_ref[...], li_ref[...]
        ar, ai, dl = _lam_bar(lr, li, ls_ref[...])
        sr, si = s1_ref[0], s2_ref[0]
        for k in range(1, 8):
            sr = sr + s1_ref[k]
            si = si + s2_ref[k]
        a2 = ar * ar + ai * ai
        gar, gai = (sr * ar - si * ai) / a2, (sr * ai + si * ar) / a2
        n2 = lr * lr + li * li
        ivr, ivi = lr / n2, -li / n2
        fr = (ar - 1.0) * ivr - ai * ivi
        fi = (ar - 1.0) * ivi + ai * ivr
        dfr, dfi = dfr_ref[...], dfi_ref[...]
        gar = gar + ivr * dfr + ivi * dfi
        gai = gai + ivr * dfi - ivi * dfr
        wr, wi = -(fr * ivr - fi * ivi), -(fr * ivi + fi * ivr)
        glr, gli = wr * dfr + wi * dfi, wr * dfi - wi * dfr
        gzr, gzi = ar * gar + ai * gai, ar * gai - ai * gar
        dlr_ref[...] = glr + dl * gzr
        dli_ref[...] = gli + dl * gzi
        dls_ref[...] = dl * jnp.sum(lr * gzr + li * gzi, axis=-1, keepdims=True)

    m = pl.BlockSpec((G, P), lambda: (0, 0))
    v = pl.BlockSpec((G, 1), lambda: (0, 0))
    s = pl.BlockSpec((8, G, P), lambda: (0, 0, 0))
    return _pcall(body, name="ssm_param_bwd", in_specs=[m, m, v, m, m, s, s], out_specs=[m, m, v],
                  out_shape=[_sds((G, P), F32), _sds((G, P), F32), _sds((G, 1), F32)])(lr, li, ls, dfr, dfi, s1, s2)


def _tile_mask(G):
    T = G // 2
    e = np.zeros((T, 8, 1, 2, 1), np.float32)
    for t in range(T):
        for c in range(2):
            e[t, (2 * t + c) % 8, 0, c, 0] = 1.0
    return e


def _tile_w(arr):
    G = arr.shape[0]
    a = arr.reshape(G // 2, 1, 2, STATE, GROUP_CH).transpose(0, 1, 4, 2, 3)
    return (a * _tile_mask(G)).reshape(G // 2, LANES, LANES).astype(BF16)


def _tile_w_grad(dw):
    G = dw.shape[0] * 2
    d = dw.reshape(G // 2, 8, GROUP_CH, 2, STATE) * _tile_mask(G)
    return d.sum(axis=1).transpose(0, 2, 3, 1).reshape(G, STATE, GROUP_CH)


def _slab_w(arr):
    G = arr.shape[0]
    a = arr.reshape(G // 8, 8, STATE, 1, GROUP_CH)
    eye = np.eye(8, dtype=np.float32).reshape(1, 8, 1, 8, 1)
    return (a * eye).reshape(G // 8, 8 * STATE, LANES).astype(BF16)


def _slab_w_grad(dw):
    J = dw.shape[0]
    eye = np.eye(8, dtype=np.float32).reshape(1, 8, 1, 8, 1)
    return (dw.reshape(J, 8, STATE, 8, GROUP_CH) * eye).sum(axis=3).reshape(J * 8, STATE, GROUP_CH)


def _exchange(name, ins, out_sds, remote, local, aliases=None):
    n_in, n_out, n_r, n_l = len(ins), len(out_sds), len(remote), len(local)

    def body(*refs):
        in_refs, out_refs = refs[:n_in], refs[n_in:n_in + n_out]
        send_sems, recv_sems, local_sems = refs[n_in + n_out:]
        x, y, c = lax.axis_index("x"), lax.axis_index("y"), lax.axis_index("c")

        def place(px, py, pc):
            return dict(x=px, y=py, c=pc, chip=2 * px + py)

        def flip(mask):
            mx, my, mc = mask
            return ((1 - x) if mx else x, (1 - y) if my else y, (1 - c) if mc else c)

        me = place(x, y, c)
        sends = []
        for k, (ii, src, oi, dst, mask) in enumerate(remote):
            cp = pltpu.make_async_remote_copy(src_ref=src(in_refs[ii], me), dst_ref=dst(out_refs[oi], me),
                                              send_sem=send_sems.at[k], recv_sem=recv_sems.at[k],
                                              device_id=flip(mask), device_id_type=MESH)
            cp.start()
            sends.append(cp)
        locals_ = []
        for k, (ii, src, oi, dst) in enumerate(local):
            cp = pltpu.make_async_copy(src(in_refs[ii], me), dst(out_refs[oi], me), local_sems.at[k])
            cp.start()
            locals_.append(cp)
        for k, (ii, src, oi, dst, mask) in enumerate(remote):
            sends[k].wait_send()
            peer = flip(mask)
            pltpu.make_async_remote_copy(src_ref=src(in_refs[ii], me), dst_ref=dst(out_refs[oi], place(*peer)),
                                         send_sem=send_sems.at[k], recv_sem=recv_sems.at[k],
                                         device_id=peer, device_id_type=MESH).wait_recv()
        for cp in locals_:
            cp.wait()

    any_spec = pl.BlockSpec(memory_space=pl.ANY)
    return _pcall(body, name=name, in_specs=[any_spec] * n_in, out_specs=[any_spec] * n_out, out_shape=list(out_sds),
                  input_output_aliases=aliases or {},
                  scratch_shapes=[pltpu.SemaphoreType.DMA((n_r,)), pltpu.SemaphoreType.DMA((n_r,)),
                                  pltpu.SemaphoreType.DMA((max(n_l, 1),))])(*ins)


def _mesh_place():
    x, y, c = lax.axis_index("x"), lax.axis_index("y"), lax.axis_index("c")

    def place(px, py, pc):
        return dict(x=px, y=py, c=pc, chip=2 * px + py)

    def flip(mask):
        mx, my, mc = mask
        return ((1 - x) if mx else x, (1 - y) if my else y, (1 - c) if mc else c)

    return place(x, y, c), place, flip


_HBM = pl.BlockSpec(memory_space=pltpu.HBM)
_SEM = pl.BlockSpec(memory_space=pltpu.SEMAPHORE)
_EFFECT = pltpu.SideEffectType.DATAFLOW_SIDE_EFFECTING


def _split_start(name, bufs, groups):
    n, ng = len(bufs), len(groups)

    def body(*refs):
        in_refs, sems, token = refs[:n], refs[n:n + 2 * ng], refs[-1]
        me, _, flip = _mesh_place()
        for g, copies in enumerate(groups):
            for k, (si, src, di, dst, mask) in enumerate(copies):
                pltpu.make_async_remote_copy(src_ref=src(in_refs[si], me), dst_ref=dst(in_refs[di], me),
                                             send_sem=sems[2 * g].at[k], recv_sem=sems[2 * g + 1].at[k],
                                             device_id=flip(mask), device_id_type=MESH).start()
        token[...] = jnp.zeros_like(token)

    outs = _pcall(body, name=name,
                  out_shape=(*[pltpu.SemaphoreType.DMA((len(g),)) for g in groups for _ in range(2)],
                             *[pltpu.HBM(b.shape, b.dtype) for b in bufs], _sds((8, LANES), F32)),
                  in_specs=[_HBM] * n, out_specs=(*[_SEM] * (2 * ng), *[_HBM] * n, pl.BlockSpec(memory_space=pltpu.VMEM)),
                  input_output_aliases={i: 2 * ng + i for i in range(n)},
                  compiler_params=pltpu.CompilerParams(has_side_effects=_EFFECT),
                  )(*[pltpu.with_memory_space_constraint(b, pltpu.HBM) for b in bufs])
    return [(outs[2 * g], outs[2 * g + 1]) for g in range(ng)], list(outs[2 * ng:2 * ng + n]), outs[-1]


def _split_wait(name, bufs, sems, after, remote):
    n = len(bufs)
    send_sems, recv_sems = sems

    def body(*refs):
        in_refs, ssem, rsem = refs[:n], refs[n], refs[n + 1]
        me, place, flip = _mesh_place()
        for k, (si, src, di, dst, mask) in enumerate(remote):
            peer = flip(mask)
            cp = pltpu.make_async_remote_copy(src_ref=src(in_refs[si], me), dst_ref=dst(in_refs[di], place(*peer)),
                                              send_sem=ssem.at[k], recv_sem=rsem.at[k], device_id=peer, device_id_type=MESH)
            cp.wait_send()
            cp.wait_recv()

    return list(_pcall(body, name=name, out_shape=tuple(pltpu.HBM(b.shape, b.dtype) for b in bufs),
                       in_specs=[_HBM] * n + [_SEM, _SEM, pl.BlockSpec(memory_space=pl.ANY)], out_specs=tuple([_HBM] * n),
                       input_output_aliases={i: i for i in range(n)},
                       compiler_params=pltpu.CompilerParams(has_side_effects=_EFFECT))(*bufs, send_sems, recv_sems, after))


CHIP_MASKS = ((0, 1, 0), (1, 0, 0), (1, 1, 0))
SIBLING = (0, 0, 1)


def _whole(ref, p):
    return ref


def _all_gather(name, shards, col_sharded):
    def dst_view(col):
        def view(ref, p):
            r, cdim = ref.shape[0] // (1 if col else N_CHIPS), ref.shape[1] // (N_CHIPS if col else 1)
            if col:
                return ref.at[:, pl.ds(pl.multiple_of(p["chip"] * cdim, LANES), cdim)]
            return ref.at[pl.ds(pl.multiple_of(p["chip"] * r, 8), r), :]
        return view

    out_sds = [_sds((s.shape[0], s.shape[1] * N_CHIPS) if col else (s.shape[0] * N_CHIPS, s.shape[1]), s.dtype)
               for s, col in zip(shards, col_sharded)]
    remote = [(a, _whole, a, dst_view(col), m) for a, col in enumerate(col_sharded) for m in CHIP_MASKS]
    local = [(a, _whole, a, dst_view(col)) for a, col in enumerate(col_sharded)]
    return _exchange(name, shards, out_sds, remote, local)


class _Place:
    def __getitem__(self, k):
        return lax.axis_index("c") if k == 0 else 2 * lax.axis_index("x") + lax.axis_index("y")


def _placed_call(body, name, grid, in_specs, out_specs, out_shape, sem, ins):
    def wrap(spec):
        return pl.BlockSpec(spec.block_shape, lambda *idx: spec.index_map(*idx, _Place()))

    outs = [wrap(s) for s in out_specs] if isinstance(out_specs, (list, tuple)) else wrap(out_specs)
    return _pcall(body, name=name, grid=grid, in_specs=[wrap(s) for s in in_specs], out_specs=outs, out_shape=out_shape,
                  compiler_params=_params(sem))(*ins)


def _rows_within(n, width, limit=512 * 1024):
    return _pick(n, tuple(t for t in (1024, 512, 256, 128, 64, 32, 16) if t * width <= limit) or (16,))


def _region_view(col):
    def view(ref, p):
        if col:
            cdim = ref.shape[1] // N_CHIPS
            return ref.at[:, pl.ds(pl.multiple_of(p["chip"] * cdim, LANES), cdim)]
        r = ref.shape[0] // N_CHIPS
        return ref.at[pl.ds(pl.multiple_of(p["chip"] * r, 16), r), :]
    return view


def _ag_place(name, w, layer, col, dtype):
    _, r, cdim = w.shape
    tr = _rows_within(r, cdim)
    nb = r // tr

    def body(w_ref, o_ref):
        o_ref[...] = w_ref[...].astype(dtype)

    if col:
        out_shape, out_spec = (r, N_CHIPS * cdim), pl.BlockSpec((tr, cdim), lambda i, pr: (i, pr[1]))
    else:
        out_shape, out_spec = (N_CHIPS * r, cdim), pl.BlockSpec((tr, cdim), lambda i, pr: (pr[1] * nb + i, 0))
    return _placed_call(body, name, (nb,), [pl.BlockSpec((None, tr, cdim), lambda i, pr: (layer, i, 0))], out_spec,
                        _sds(out_shape, dtype), ("parallel",), [w])


def _ag_copies(a, col):
    return [(a, _region_view(col), a, _region_view(col), m) for m in CHIP_MASKS]


def _rs_add2(name, g4, a4, out_dtype):
    J, _, h, C = g4.shape
    tr = _rows_within(h, C)

    def body(g_ref, a_ref, o_ref):
        o_ref[...] = (g_ref[...].astype(F32) + a_ref[...].astype(F32)).astype(o_ref.dtype)

    return _placed_call(body, name, (J, h // tr),
                        [pl.BlockSpec((None, None, tr, C), lambda j, i, pr: (j, pr[0], i, 0)),
                         pl.BlockSpec((None, None, tr, C), lambda j, i, pr: (j, 0, i, 0))],
                        pl.BlockSpec((None, tr, C), lambda j, i, pr: (j, i, 0)), _sds((J, h, C), out_dtype),
                        ("parallel", "parallel"), [g4, a4])


def _rs_add4(name, p3, landed, col):
    _, h, w = landed.shape
    tr = _rows_within(h, w)

    def body(p_ref, a_ref, b_ref, c_ref, o_ref):
        o_ref[...] = ((p_ref[...].astype(F32) + a_ref[...].astype(F32)) + b_ref[...].astype(F32)) + c_ref[...].astype(F32)

    own = (pl.BlockSpec((None, tr, w), lambda i, pr: (0, i, pr[1])) if col else pl.BlockSpec((None, tr, w), lambda i, pr: (pr[1], i, 0)))
    slot = lambda k: pl.BlockSpec((None, tr, w), lambda i, pr: (k, i, 0))
    return _placed_call(body, name, (h // tr,), [own, slot(0), slot(1), slot(2)], pl.BlockSpec((tr, w), lambda i, pr: (i, 0)),
                        _sds((h, w), F32), ("parallel",), [p3, landed, landed, landed])


def _rs_begin(tag, grads, col_sharded):
    n = len(grads)
    g4 = [g.reshape((1, 2, g.shape[0] // 2, g.shape[1]) if col else (N_CHIPS, 2, g.shape[0] // (2 * N_CHIPS), g.shape[1]))
          for g, col in zip(grads, col_sharded)]
    other_half = lambda ref, p: ref.at[:, pl.ds(1 - p["c"], 1)]
    theirs = _exchange("rs_sibling_w", g4, [_sds((g.shape[0], 1) + g.shape[2:], g.dtype) for g in g4],
                       [(a, other_half, a, _whole, SIBLING) for a in range(n)], [])
    chip_sum = [_rs_add2("rs_add2_w", g4[a], theirs[a], BF16) for a in range(n)]

    def send_view(col, mask):
        def view(ref, p):
            t = 2 * ((1 - p["x"]) if mask[0] else p["x"]) + ((1 - p["y"]) if mask[1] else p["y"])
            if col:
                sc = ref.shape[2] // N_CHIPS
                return ref.at[0, :, pl.ds(pl.multiple_of(t * sc, LANES), sc)]
            return ref.at[t]
        return view
    slot = lambda k: (lambda ref, p: ref.at[k])
    piece = [(s.shape[1], s.shape[2] // N_CHIPS if col else s.shape[2]) for s, col in zip(chip_sum, col_sharded)]
    landing = [lax.empty((len(CHIP_MASKS),) + s, BF16) for s in piece]
    copies = [(a, send_view(col_sharded[a], m), n + a, slot(k), m) for a in range(n) for k, m in enumerate(CHIP_MASKS)]
    (sems,), bufs, token = _split_start("rs_chips_start_" + tag, chip_sum + landing, [copies])
    return dict(sems=sems, bufs=bufs, copies=copies, token=token, col_sharded=col_sharded)


def _rs_finish(tag, st, after):
    col_sharded = st['col_sharded']
    n = len(col_sharded)
    bufs = _split_wait("rs_chips_wait_" + tag, st['bufs'], st['sems'], after, st['copies'])
    chip_sum, landed = bufs[:n], bufs[n:]
    mine = [_rs_add4("rs_add4_w", chip_sum[a], landed[a], col_sharded[a]) for a in range(n)]
    other = _exchange("rs_halves_w", mine, [_sds(m.shape, F32) for m in mine], [(a, _whole, a, _whole, SIBLING) for a in range(n)], [])
    return mine, other


def _adamw_big(name, mine, other, w, m, v):
    depth, R, C = w.shape
    h = R // 2
    tr = _pick(h, tuple(t for t in (512, 256, 128, 64, 32, 16, 8) if t * C <= 256 * 1024) or (8,))
    nb = h // tr

    def g_spec(kk, hh):
        def imap(l, s, i, pr):
            before = (l < kk) | ((l == kk) & (s < hh))
            return (jnp.where((l == kk) & (s == hh), i, jnp.where(before, 0, nb - 1)), 0)
        return pl.BlockSpec((tr, C), imap)

    st_spec = pl.BlockSpec((None, tr, C), lambda l, s, i, pr: (l, jnp.where(s == 0, pr[0], 1 - pr[0]) * nb + i, 0))

    def body(*refs):
        g_refs = refs[:2 * depth]
        w_ref, m_ref, v_ref, go_ref, d_ref, mo_ref, vo_ref = refs[2 * depth:]
        l, s = pl.program_id(0), pl.program_id(1)
        for kk in range(depth):
            for hh in range(2):
                @pl.when((l == kk) & (s == hh))
                def _(kk=kk, hh=hh):
                    g = g_refs[2 * kk + hh][...]
                    d, mn, vn = _adam_math(w_ref[...], g, m_ref[...], v_ref[...])
                    go_ref[...] = g
                    d_ref[...] = d
                    mo_ref[...] = mn
                    vo_ref[...] = vn

    gs, g_specs = [], []
    for kk in range(depth):
        gs += [mine[kk], other[kk]]
        g_specs += [g_spec(kk, 0), g_spec(kk, 1)]
    return _placed_call(body, name, (depth, 2, nb), g_specs + [st_spec] * 3, [st_spec] * 4, [_sds(w.shape, F32)] * 4,
                        ("arbitrary", "arbitrary", "arbitrary"), gs + [w, m, v])


def _piece_view(col, j, other):
    def view(ref, p):
        R, C = ref.shape
        cc = (1 - p["c"]) if other else p["c"]
        if col:
            hr, sc = R // 2, C // N_CHIPS
            return ref.at[pl.ds(pl.multiple_of(cc * hr, 16), hr), pl.ds(j * sc, sc)]
        hr = R // (2 * N_CHIPS)
        return ref.at[pl.ds(pl.multiple_of((2 * j + cc) * hr, 8), hr), :]
    return view


def _piece_shape(shape, col):
    R, C = shape
    return (R // 2, C // N_CHIPS) if col else (R // (2 * N_CHIPS), C)


def _reduce_scatter(tag, grads, col_sharded, wire_dtype):
    n = len(grads)
    shapes = [_piece_shape(g.shape, col) for g, col in zip(grads, col_sharded)]

    slot = lambda j: (lambda ref, p: ref.at[j])
    remote = [(a, _piece_view(col_sharded[a], j, True), a, slot(j), SIBLING) for a in range(n) for j in range(N_CHIPS)]
    local = [(a, _piece_view(col_sharded[a], j, False), n + a, slot(j)) for a in range(n) for j in range(N_CHIPS)]
    got = _exchange("rs_sibling_" + tag, grads, [_sds((N_CHIPS,) + s, g.dtype) for s, g in zip(shapes, grads)] * 2, remote, local)
    theirs, mine = got[:n], got[n:]
    chip_sum = [_ew("rs_add2_" + tag, lambda a, b: (a.astype(F32) + b.astype(F32),),
                    [m.reshape(-1, m.shape[-1]), t.reshape(-1, t.shape[-1])], [wire_dtype])[0].reshape(m.shape)
                for m, t in zip(mine, theirs)]

    def send_view(mask):
        return lambda ref, p: ref.at[2 * ((1 - p["x"]) if mask[0] else p["x"]) + ((1 - p["y"]) if mask[1] else p["y"])]
    remote = [(a, send_view(m), a, slot(k), m) for a in range(n) for k, m in enumerate(CHIP_MASKS)]
    local = [(a, lambda ref, p: ref.at[p["chip"]], n + a, _whole) for a in range(n)]
    got = _exchange("rs_chips_" + tag, chip_sum,
                    [_sds((len(CHIP_MASKS),) + s, wire_dtype) for s in shapes] + [_sds(s, wire_dtype) for s in shapes], remote, local)
    landed, own = got[:n], got[n:]
    half = [_ew("rs_add4_" + tag, lambda o, a, b, c: (((o.astype(F32) + a.astype(F32)) + b.astype(F32)) + c.astype(F32),),
                [o, l[0], l[1], l[2]], [F32])[0] for o, l in zip(own, landed)]

    def half_rows(ref, p):
        hr = ref.shape[0] // 2
        return ref.at[pl.ds(pl.multiple_of(p["c"] * hr, 8), hr), :]
    remote = [(a, _whole, a, half_rows, SIBLING) for a in range(n)]
    local = [(a, _whole, a, half_rows) for a in range(n)]
    return _exchange("rs_halves_" + tag, half, [_sds((2 * s[0], s[1]), F32) for s in shapes], remote, local)


def _ssm_prepare(W):
    lr, li, ls = W['ssm_lambda_re'], W['ssm_lambda_im'], W['ssm_log_step']
    depth, G = ls.shape
    GG = depth * G
    flat = lambda a: a.reshape(-1, LANES)
    bc = lambda a: flat(jnp.broadcast_to(a, (depth, G, STATE, GROUP_CH)))
    flat3 = (bc(lr[..., None]), bc(li[..., None]), bc(ls[:, :, None, None]))
    bbr, bbi = _ssm_bbar(*flat3, flat(W['ssm_b_re']), flat(W['ssm_b_im']))
    bbr, bbi = bbr.reshape(GG, STATE, GROUP_CH), bbi.reshape(GG, STATE, GROUP_CH)
    row = lambda a: a.reshape(1, GG * STATE)
    tf, tr = _ssm_tables(row(lr), row(li), row(jnp.broadcast_to(ls[..., None], (depth, G, STATE))))
    cr = W['ssm_c_re'].reshape(GG, GROUP_CH, STATE).transpose(0, 2, 1)
    ci = -W['ssm_c_im'].reshape(GG, GROUP_CH, STATE).transpose(0, 2, 1)
    stacked = dict(wb=(_tile_w(bbr), _tile_w(bbi)), wbT=(_slab_w(bbr), _slab_w(bbi)),
                   wc=(_slab_w(cr), _slab_w(ci)), wcT=(_tile_w(cr), _tile_w(ci)))
    Wx = G * STATE
    layers = []
    for l in range(depth):
        s = {k: tuple(a.reshape((depth, -1) + a.shape[1:])[l] for a in v) for k, v in stacked.items()}
        s['tf'], s['tr'] = tf[:, :, l * Wx:(l + 1) * Wx], tr[:, :, l * Wx:(l + 1) * Wx]
        layers.append(s)
    return flat3, layers


def _ssm_param_grads(W, flat3, raw):
    depth, G = W['ssm_log_step'].shape
    GG = depth * G
    cat = lambda k: jnp.concatenate([r[k] for r in raw], axis=0)
    flat = lambda a: a.reshape(-1, LANES)
    out = {}
    out['ssm_c_re'] = _slab_w_grad(cat(2)).transpose(0, 2, 1).reshape(W['ssm_c_re'].shape)
    out['ssm_c_im'] = -_slab_w_grad(cat(3)).transpose(0, 2, 1).reshape(W['ssm_c_im'].shape)
    dbr, dbi, qr, qi = _ssm_param_bwd_flat(*flat3, flat(W['ssm_b_re']), flat(W['ssm_b_im']),
                                           flat(_tile_w_grad(cat(0))), flat(_tile_w_grad(cat(1))))
    out['ssm_b_re'], out['ssm_b_im'] = dbr.reshape(W['ssm_b_re'].shape), dbi.reshape(W['ssm_b_im'].shape)
    pick = lambda q: q[:, ::GROUP_CH].reshape(GG, STATE)
    sums = lambda k: jnp.concatenate([r[k].reshape(8, G, STATE) for r in raw], axis=1)
    dlr, dli, dls = _ssm_param_bwd(W['ssm_lambda_re'].reshape(GG, STATE), W['ssm_lambda_im'].reshape(GG, STATE),
                                   W['ssm_log_step'].reshape(GG, 1), pick(qr), pick(qi), sums(4), sums(5))
    out['ssm_lambda_re'], out['ssm_lambda_im'] = dlr.reshape(depth, G, STATE), dli.reshape(depth, G, STATE)
    out['ssm_log_step'] = dls.reshape(depth, G)
    return out


def _layer_fwd(x, p, weight, dims):
    attn_w, kv_w, u_off = dims['attn_w'], dims['kv_w'], dims['u_off']
    s = p['s5']
    h = _rms_fwd("norm_mix", [x], [p['norm_mix_g']], BF16)
    w = {'w_in': weight('w_in', h)}
    proj, = _mm("mm_in", h, w['w_in'], 'nn', [F32])
    attn = _attn_fwd(proj, p['q_norm_g'], p['k_norm_g'], p['attn_sinks'], attn_w, kv_w)
    bur, bui = _blockproj("ssm_bu", proj, u_off, *s['wb'])
    xr, xi = _scan("ssm_scan_fwd", bur, bui, s['tf'], False)
    y, gl = _ssm_out(xr, xi, *s['wc'], proj, u_off, p['ssm_d'])
    w['w_glu'] = weight('w_glu', gl)
    ssm, z = _mm("mm_glu", gl, w['w_glu'], 'nn', [F32, F32], extras=[('row', p['b_glu']), ('tile', gl)],
                 epi=lambda acc, b, g: ((lambda zz: (g * jax.nn.sigmoid(zz), zz))(acc + b)))
    mix = _rms_fwd("norm_heads", [attn, ssm], [p['attn_out_g'], p['ssm_out_g']], BF16)
    w['w_out'] = weight('w_out', mix)
    x_mid, = _mm("mm_out", mix, w['w_out'], 'nn', [F32], extras=[('tile', x)], epi=lambda acc, r: (acc + r,))
    h2 = _rms_fwd("norm_mlp", [x_mid], [p['norm_mlp_g']], BF16)
    w['w_up'] = weight('w_up', h2)
    a, r = _mm("mm_up", h2, w['w_up'], 'nn', [F32, BF16],
               epi=lambda acc: (acc, jnp.square(jnp.maximum(acc, 0.0))))
    w['w_down'] = weight('w_down', r)
    x_out, = _mm("mm_down", r, w['w_down'], 'nn', [F32], extras=[('tile', x_mid)], epi=lambda acc, rr: (acc + rr,))
    saved = dict(x=x, h=h, proj=proj, attn=attn, xr=xr, xi=xi, y=y, gl=gl, z=z, ssm=ssm, mix=mix, x_mid=x_mid, h2=h2, a=a, r=r, w=w)
    return x_out, saved


def _layer_bwd(dx, dx16, sv, p, dims, reduce_grads):
    attn_w, kv_w, u_off = dims['attn_w'], dims['kv_w'], dims['u_off']
    s, w = p['s5'], sv['w']
    gb, gs = {}, {}
    da, = _mm("mm_down_dx", dx16, w['w_down'], 'nt', [BF16], extras=[('tile', sv['a'])],
              epi=lambda acc, a: (acc * (2.0 * jnp.maximum(a, 0.0)),))
    gb['w_down'], = _mm("mm_down_dw", sv['r'], dx16, 'tn', [BF16])
    dh2, = _mm("mm_up_dx", da, w['w_up'], 'nt', [F32])
    gb['w_up'], = _mm("mm_up_dw", sv['h2'], da, 'tn', [BF16])
    token = reduce_grads(('w_up', 'w_down'), [gb['w_up'], gb['w_down']])
    (dx_mid,), (gs['norm_mlp_g'],), dx_mid16 = _rms_bwd("norm_mlp_bwd", [sv['x_mid']], [p['norm_mlp_g'] + token], dh2, resid=dx)
    dmix, = _mm("mm_out_dx", dx_mid16, w['w_out'], 'nt', [F32])
    gb['w_out'], = _mm("mm_out_dw", sv['mix'], dx_mid16, 'tn', [BF16])
    (dattn, dssm), (gs['attn_out_g'], gs['ssm_out_g']) = _rms_bwd(
        "norm_heads_bwd", [sv['attn'], sv['ssm']], [p['attn_out_g'], p['ssm_out_g']], dmix)
    dz, gs['b_glu'] = _glu_dz(dssm, sv['gl'], sv['z'])
    dy, = _mm("mm_glu_dx", dz, w['w_glu'], 'nt', [F32], extras=[('tile', dssm), ('tile', sv['z']), ('tile', sv['y'])],
              epi=lambda acc, ds, z, y: ((acc + ds * jax.nn.sigmoid(z)) * _gelu_grad(y),))
    gb['w_glu'], = _mm("mm_glu_dw", sv['gl'], dz, 'tn', [BF16])
    dxr, dxi = _blockproj("ssm_dstate", dy, 0, *s['wcT'])
    gxr, gxi, s1, s2 = _scan("ssm_scan_bwd", dxr, dxi, s['tr'], True, states=(sv['xr'], sv['xi']))
    du, gs['ssm_d'] = _ssm_du(gxr, gxi, *s['wbT'], dy, sv['proj'], u_off, p['ssm_d'])
    dwb_r, dwb_i = _blockproj_grad("ssm_dbbar", sv['proj'], u_off, gxr, gxi)
    dwc_r, dwc_i = _ssm_dc(sv['xr'], sv['xi'], dy, s['wc'][0].shape[1])
    gs['s5_raw'] = (dwb_r, dwb_i, dwc_r, dwc_i, s1, s2)
    dq, dkn, dv, gs['q_norm_g'], gs['attn_sinks'] = _attn_bwd(sv['proj'], sv['attn'], dattn, p['q_norm_g'], p['k_norm_g'],
                                                               p['attn_sinks'], attn_w, kv_w)
    dk, gs['k_norm_g'] = _knorm_bwd(sv['proj'], dkn, p['k_norm_g'], attn_w, kv_w)
    dproj = jnp.concatenate([dq.astype(BF16), dk.astype(BF16), dv.astype(BF16), du.astype(BF16)], axis=1)
    dh, = _mm("mm_in_dx", dproj, w['w_in'], 'nt', [F32])
    gb['w_in'], = _mm("mm_in_dw", sv['h'], dproj, 'tn', [BF16])
    token = reduce_grads(('w_in', 'w_glu', 'w_out'), [gb['w_in'], gb['w_glu'], gb['w_out']])
    (dx_in,), (gs['norm_mix_g'],), dx_in16 = _rms_bwd("norm_mix_bwd", [sv['x']], [p['norm_mix_g'] + token], dh, resid=dx_mid)
    return dx_in, dx_in16, gs


PACK_COLS = 1024


def _pack(arrs, rows):
    flat = jnp.concatenate([a.reshape(-1).astype(F32) for a in arrs])
    return jnp.pad(flat, (0, rows * PACK_COLS - flat.shape[0])).reshape(rows, PACK_COLS)


def _unpack(packed, shapes):
    flat = packed.reshape(-1)
    out, off = [], 0
    for s in shapes:
        n = int(np.prod(s))
        out.append(flat[off:off + n].reshape(s))
        off += n
    return out


def _pack_rows(shapes, multiple):
    n = sum(int(np.prod(s)) for s in shapes)
    rows = -(-n // PACK_COLS)
    return -(-rows // multiple) * multiple


def kernel(x, meta_tokens, norm_mix_g, w_in, q_norm_g, k_norm_g, attn_sinks, ssm_lambda_re, ssm_lambda_im, ssm_log_step, ssm_b_re, ssm_b_im, ssm_c_re, ssm_c_im, ssm_d, w_glu, b_glu, attn_out_g, ssm_out_g, w_out, norm_mlp_g, w_up, w_down, loss_target, m_meta_tokens, m_norm_mix_g, m_w_in, m_q_norm_g, m_k_norm_g, m_attn_sinks, m_ssm_lambda_re, m_ssm_lambda_im, m_ssm_log_step, m_ssm_b_re, m_ssm_b_im, m_ssm_c_re, m_ssm_c_im, m_ssm_d, m_w_glu, m_b_glu, m_attn_out_g, m_ssm_out_g, m_w_out, m_norm_mlp_g, m_w_up, m_w_down, v_meta_tokens, v_norm_mix_g, v_w_in, v_q_norm_g, v_k_norm_g, v_attn_sinks, v_ssm_lambda_re, v_ssm_lambda_im, v_ssm_log_step, v_ssm_b_re, v_ssm_b_im, v_ssm_c_re, v_ssm_c_im, v_ssm_d, v_w_glu, v_b_glu, v_attn_out_g, v_ssm_out_g, v_w_out, v_norm_mlp_g, v_w_up, v_w_down):
    args = (meta_tokens, norm_mix_g, w_in, q_norm_g, k_norm_g, attn_sinks, ssm_lambda_re, ssm_lambda_im, ssm_log_step, ssm_b_re, ssm_b_im, ssm_c_re, ssm_c_im, ssm_d, w_glu, b_glu, attn_out_g, ssm_out_g, w_out, norm_mlp_g, w_up, w_down)
    ms = (m_meta_tokens, m_norm_mix_g, m_w_in, m_q_norm_g, m_k_norm_g, m_attn_sinks, m_ssm_lambda_re, m_ssm_lambda_im, m_ssm_log_step, m_ssm_b_re, m_ssm_b_im, m_ssm_c_re, m_ssm_c_im, m_ssm_d, m_w_glu, m_b_glu, m_attn_out_g, m_ssm_out_g, m_w_out, m_norm_mlp_g, m_w_up, m_w_down)
    vs = (v_meta_tokens, v_norm_mix_g, v_w_in, v_q_norm_g, v_k_norm_g, v_attn_sinks, v_ssm_lambda_re, v_ssm_lambda_im, v_ssm_log_step, v_ssm_b_re, v_ssm_b_im, v_ssm_c_re, v_ssm_c_im, v_ssm_d, v_w_glu, v_b_glu, v_attn_out_g, v_ssm_out_g, v_w_out, v_norm_mlp_g, v_w_up, v_w_down)
    W = dict(zip(WEIGHTS, args))
    M = dict(zip(WEIGHTS, ms))
    V = dict(zip(WEIGHTS, vs))
    depth = norm_mix_g.shape[0]
    seq, D = x.shape[1], x.shape[2]
    attn_w = D // 2
    kv_w = attn_w // KV_GROUP
    dims = dict(attn_w=attn_w, kv_w=kv_w, u_off=(attn_w + 2 * kv_w) // LANES)
    small_names = [n for n in WEIGHTS if n not in BIG and n != 'meta_tokens']
    chip = 2 * lax.axis_index("x") + lax.axis_index("y")

    gathers, started = [], jnp.zeros((), F32)
    for l in range(depth):
        placed = [_ag_place("ag_place_" + n, W[n], l, COL_SHARDED[n], BF16) for n in BIG]
        groups = [_ag_copies(a, COL_SHARDED[n]) for a, n in enumerate(BIG)]
        if l == 0:
            placed = [_ag_place("ag_place_meta", meta_tokens[None], 0, True, F32)] + placed
            groups = [_ag_copies(0, True)] + [_ag_copies(a + 1, COL_SHARDED[n]) for a, n in enumerate(BIG)]
        sems, bufs, token = _split_start("ag_start_%d" % l, placed, groups)
        gathers.append(dict(zip((['meta_tokens'] if l == 0 else []) + BIG, zip(sems, bufs))))
        started = started + token[0, 0]

    def gathered(l, n, after):
        sems, buf = gathers[l][n]
        return _split_wait("ag_wait_%d_%s" % (l, n), [buf], sems, after, _ag_copies(0, n == 'meta_tokens' or COL_SHARDED[n]))[0]

    h_res = jnp.concatenate([jnp.zeros((PAD, D), F32), gathered(0, 'meta_tokens', started.reshape(1, 1)), x[0]], axis=0)
    s5_flat3, s5_layers = _ssm_prepare(W)
    layer_p = []
    for l in range(depth):
        p = {n: W[n][l][None, :] for n in ('norm_mix_g', 'q_norm_g', 'k_norm_g', 'attn_sinks', 'ssm_d', 'b_glu', 'attn_out_g',
                                             'ssm_out_g', 'norm_mlp_g')}
        p['s5'] = s5_layers[l]
        layer_p.append(p)
    saved = []
    for l in range(depth):
        h_res, sv = _layer_fwd(h_res, layer_p[l], functools.partial(gathered, l), dims)
        saved.append(sv)
    loss_local, dx, dx16 = _loss(h_res, loss_target[0])
    loss = lax.psum(loss_local, ("x", "y", "c"))

    small_grads = [None] * depth
    shard_grads = {}
    pending = []

    def finish(after):
        while pending:
            l_, names, st = pending.pop(0)
            mine, other = _rs_finish("%d_%s" % (l_, names[0]), st, after)
            for a, n in enumerate(names):
                shard_grads[(l_, n)] = (mine[a], other[a])

    def reduce_grads(l, names, grads):
        st = _rs_begin("%d_%s" % (l, names[0]), list(grads), [COL_SHARDED[n] for n in names])
        pending.append((l, names, st))
        return st['token'][0, 0]

    for l in reversed(range(depth)):
        dx, dx16, gs = _layer_bwd(dx, dx16, saved[l], layer_p[l], dims, functools.partial(reduce_grads, l))
        saved[l] = None
        small_grads[l] = gs
        newest = pending.pop()
        finish(dx)
        pending.append(newest)
    grad_x = dx[BLOCK:].reshape(x.shape)

    g_small = _ssm_param_grads(W, s5_flat3, [small_grads[l]['s5_raw'] for l in range(depth)])
    for n in small_names:
        if n not in g_small:
            g_small[n] = jnp.stack([small_grads[l][n].reshape(W[n].shape[1:]) for l in range(depth)])
    g_shapes = [(N_META, D)] + [W[n].shape for n in small_names]
    rows = _pack_rows(g_shapes, 8 * 2 * N_CHIPS)
    packed = _pack([dx[PAD:BLOCK]] + [g_small[n] for n in small_names], rows)
    red, = _reduce_scatter("small", [packed], [False], F32)
    red_full, = _all_gather("ag_small", [red], [False])
    finish(red_full)
    g_list = _unpack(red_full, g_shapes)
    g_meta = lax.dynamic_slice_in_dim(g_list[0], chip * meta_tokens.shape[1], meta_tokens.shape[1], axis=1)
    G = dict(zip(small_names, g_list[1:]))
    G['meta_tokens'] = g_meta

    out = {}
    for n in BIG:
        out[n] = _adamw_big("adamw_" + n, [shard_grads[(l, n)][0] for l in range(depth)],
                            [shard_grads[(l, n)][1] for l in range(depth)], W[n], M[n], V[n])
    names = ['meta_tokens'] + small_names
    shapes = [W[n].shape for n in names]
    prow = _pack_rows(shapes, 8)
    d_p, m_p, v_p = _ew("adamw_small", _adam_math, [_pack([W[n] for n in names], prow), _pack([G[n] for n in names], prow),
                                                    _pack([M[n] for n in names], prow), _pack([V[n] for n in names], prow)], [F32] * 3)
    for n, d_, m_, v_ in zip(names, _unpack(d_p, shapes), _unpack(m_p, shapes), _unpack(v_p, shapes)):
        out[n] = (G[n], d_, m_, v_)
    return (loss, grad_x, *[out[n][0] for n in WEIGHTS], *[out[n][1] for n in WEIGHTS],
            *[out[n][2] for n in WEIGHTS], *[out[n][3] for n in WEIGHTS])
```

```python
import functools
import math

import numpy as np
import jax
import jax.numpy as jnp
from jax import lax
from jax.experimental import pallas as pl
from jax.experimental.pallas import tpu as pltpu

F32 = jnp.float32
BF16 = jnp.bfloat16
MESH = pl.DeviceIdType.MESH

N_META = 16
HEAD_DIM = 64
KV_GROUP = 4
GROUP_CH = 16
STATE = 64
BLOCK = 128
PAD = BLOCK - N_META
NORM_EPS = 1e-6
NEG_INF = -1e30
LANES = 128
V7X_VMEM_LIMIT_BYTES = 56 * 1024 * 1024
MM_VMEM_BUDGET_BYTES = 44 * 1024 * 1024

ADAM_LR, ADAM_B1, ADAM_B2, ADAM_EPS, ADAM_WD, ADAM_STEP = 0.001, 0.9, 0.999, 1e-08, 0.01, 10

WEIGHTS = ['meta_tokens', 'norm_mix_g', 'w_in', 'q_norm_g', 'k_norm_g', 'attn_sinks', 'ssm_lambda_re',
           'ssm_lambda_im', 'ssm_log_step', 'ssm_b_re', 'ssm_b_im', 'ssm_c_re', 'ssm_c_im', 'ssm_d', 'w_glu',
           'b_glu', 'attn_out_g', 'ssm_out_g', 'w_out', 'norm_mlp_g', 'w_up', 'w_down']
BIG = ['w_in', 'w_glu', 'w_out', 'w_up', 'w_down']
COL_SHARDED = {'w_in': True, 'w_glu': False, 'w_out': False, 'w_up': True, 'w_down': False}
N_CHIPS = 4


def _pick(n, cands):
    for c in cands:
        if c <= n and n % c == 0:
            return c
    return n


def _params(sem):
    return pltpu.CompilerParams(dimension_semantics=sem, vmem_limit_bytes=V7X_VMEM_LIMIT_BYTES)


def _pcall(body, **kw):
    return pl.pallas_call(body, **kw)


def _sds(shape, dtype):
    return jax.ShapeDtypeStruct(shape, dtype)


_DIMS = {'nn': ((1,), (0,)), 'nt': ((1,), (1,)), 'tn': ((0,), (0,))}


def _mm(name, a, b, mode, out_dtypes, extras=(), epi=None):
    if mode == 'nn':
        (M, K), (_, N) = a.shape, b.shape
    elif mode == 'nt':
        (M, K), (N, _) = a.shape, b.shape
    else:
        (K, M), (_, N) = a.shape, b.shape
    tile_bytes = 4 * len([k for k, _ in extras if k == 'tile']) + sum(jnp.dtype(d).itemsize for d in out_dtypes)

    def fits(tm, tn, tk):
        need = 2 * tm * tk * a.dtype.itemsize + 2 * tk * tn * b.dtype.itemsize + 4 * tm * tn + 2 * tm * tn * tile_bytes
        return need <= MM_VMEM_BUDGET_BYTES

    if mode == 'tn':
        tm, tk_cands = _pick(M, (1024, 512, 256, 128)), (1408, 704, 384, 128)
    else:
        tm, tk_cands = _pick(M, (1408, 704, 384, 128)), (2048, 1024, 512, 256, 128)
    tk_cands = [t for t in tk_cands if t <= K and K % t == 0] or [K]
    tn_cands = [t for t in (2048, 1280, 1024, 640, 512, 256, 128) if t <= N and N % t == 0] or [N]
    if mode != 'tn' and tk_cands[0] == K and a.dtype == BF16:
        tk_cands = tk_cands[:1]
    tn, tk = next(((tn_, tk_) for tn_ in tn_cands for tk_ in tk_cands if fits(tm, tn_, tk_)), (tn_cands[-1], tk_cands[-1]))
    nk = K // tk
    a_spec = pl.BlockSpec((tk, tm), lambda i, j, k: (k, i)) if mode == 'tn' else pl.BlockSpec((tm, tk), lambda i, j, k: (i, k))
    b_spec = pl.BlockSpec((tn, tk), lambda i, j, k: (j, k)) if mode == 'nt' else pl.BlockSpec((tk, tn), lambda i, j, k: (k, j))
    ex_specs = [pl.BlockSpec((tm, tn), lambda i, j, k: (i, j)) if kind == 'tile' else pl.BlockSpec((1, tn), lambda i, j, k: (0, j))
                for kind, _ in extras]
    ne, no = len(extras), len(out_dtypes)
    dims = (_DIMS[mode], ((), ()))

    def body(a_ref, b_ref, *rest):
        ex, outs, acc = rest[:ne], rest[ne:ne + no], rest[ne + no]
        k = pl.program_id(2)

        @pl.when(k == 0)
        def _():
            acc[...] = jnp.zeros_like(acc)

        acc[...] += lax.dot_general(a_ref[...].astype(BF16), b_ref[...].astype(BF16), dims, preferred_element_type=F32)

        @pl.when(k == nk - 1)
        def _():
            r = acc[...]
            res = epi(r, *[e[...] for e in ex]) if epi is not None else (r,)
            for o, v in zip(outs, res):
                o[...] = v.astype(o.dtype)

    outs = _pcall(
        body, name=name, grid=(M // tm, N // tn, nk),
        in_specs=[a_spec, b_spec] + ex_specs,
        out_specs=[pl.BlockSpec((tm, tn), lambda i, j, k: (i, j)) for _ in out_dtypes],
        out_shape=[_sds((M, N), d) for d in out_dtypes],
        scratch_shapes=[pltpu.VMEM((tm, tn), F32)],
        compiler_params=_params(("parallel", "parallel", "arbitrary")),
    )(a, b, *[e for _, e in extras])
    return outs


def _ew(name, fn, ins, out_dtypes):
    R, C = ins[0].shape
    tr = _pick(R, tuple(t for t in (1024, 512, 256, 128, 64, 32, 16, 8) if t * C <= 512 * 1024) or (8,))
    n_in = len(ins)

    def body(*refs):
        res = fn(*[r[...] for r in refs[:n_in]])
        for o, v in zip(refs[n_in:], res):
            o[...] = v.astype(o.dtype)

    spec = pl.BlockSpec((tr, C), lambda i: (i, 0))
    return _pcall(body, name=name, grid=(R // tr,), in_specs=[spec] * n_in, out_specs=[spec] * len(out_dtypes),
                  out_shape=[_sds((R, C), d) for d in out_dtypes], compiler_params=_params(("parallel",)))(*ins)


def _adam_math(w, g, m, v):
    m = ADAM_B1 * m + (1.0 - ADAM_B1) * g
    v = ADAM_B2 * v + (1.0 - ADAM_B2) * (g * g)
    m_hat = m / (1.0 - ADAM_B1 ** ADAM_STEP)
    v_hat = v / (1.0 - ADAM_B2 ** ADAM_STEP)
    delta = -ADAM_LR * (m_hat / (jnp.sqrt(v_hat) + ADAM_EPS) + ADAM_WD * w)
    return delta, m, v


def _rms_fwd(name, xs, gs, out_dtype):
    L = xs[0].shape[0]
    ws = [x.shape[1] for x in xs]
    n = len(xs)
    tr = _pick(L, (384, 256, 128))

    def body(*refs):
        o = refs[2 * n]
        off = 0
        for i in range(n):
            x = refs[i][...]
            r = lax.rsqrt(jnp.mean(x * x, axis=-1, keepdims=True) + NORM_EPS)
            o[:, off:off + ws[i]] = ((x * r) * refs[n + i][...]).astype(o.dtype)
            off += ws[i]

    return _pcall(body, name=name, grid=(L // tr,),
                  in_specs=[pl.BlockSpec((tr, w), lambda i: (i, 0)) for w in ws] + [pl.BlockSpec((1, w), lambda i: (0, 0)) for w in ws],
                  out_specs=pl.BlockSpec((tr, sum(ws)), lambda i: (i, 0)), out_shape=_sds((L, sum(ws)), out_dtype),
                  compiler_params=_params(("parallel",)))(*xs, *gs)


def _rms_bwd(name, xs, gs, dy, resid=None):
    L = xs[0].shape[0]
    ws = [x.shape[1] for x in xs]
    n = len(xs)
    tr = _pick(L, (384, 256, 128))
    has_res = resid is not None

    def body(*refs):
        x_refs, g_refs, dy_ref = refs[:n], refs[n:2 * n], refs[2 * n]
        p = 2 * n + 1
        res_ref = refs[p] if has_res else None
        p += 1 if has_res else 0
        dx_refs, dg_refs = refs[p:p + n], refs[p + n:p + 2 * n]
        dx16_ref = refs[p + 2 * n] if has_res else None
        first = pl.program_id(0) == 0
        off = 0
        for i in range(n):
            x = x_refs[i][...]
            d = dy_ref[:, off:off + ws[i]]
            r = lax.rsqrt(jnp.mean(x * x, axis=-1, keepdims=True) + NORM_EPS)
            xh = x * r
            dg = jnp.sum(d * xh, axis=0, keepdims=True)

            @pl.when(first)
            def _(i=i):
                dg_refs[i][...] = jnp.zeros_like(dg_refs[i])

            dg_refs[i][...] += dg
            dyg = d * g_refs[i][...]
            dx = r * (dyg - xh * jnp.mean(dyg * xh, axis=-1, keepdims=True))
            if has_res:
                dx = dx + res_ref[...]
                dx16_ref[...] = dx.astype(BF16)
            dx_refs[i][...] = dx
            off += ws[i]

    in_specs = ([pl.BlockSpec((tr, w), lambda i: (i, 0)) for w in ws] + [pl.BlockSpec((1, w), lambda i: (0, 0)) for w in ws]
                + [pl.BlockSpec((tr, sum(ws)), lambda i: (i, 0))])
    ins = list(xs) + list(gs) + [dy]
    if has_res:
        in_specs.append(pl.BlockSpec((tr, ws[0]), lambda i: (i, 0)))
        ins.append(resid)
    out_specs = [pl.BlockSpec((tr, w), lambda i: (i, 0)) for w in ws] + [pl.BlockSpec((1, w), lambda i: (0, 0)) for w in ws]
    out_shape = [_sds((L, w), F32) for w in ws] + [_sds((1, w), F32) for w in ws]
    if has_res:
        out_specs.append(pl.BlockSpec((tr, ws[0]), lambda i: (i, 0)))
        out_shape.append(_sds((L, ws[0]), BF16))
    outs = _pcall(body, name=name, grid=(L // tr,), in_specs=in_specs, out_specs=out_specs, out_shape=out_shape,
                  compiler_params=_params(("arbitrary",)))(*ins)
    return (outs[:n], outs[n:2 * n], outs[2 * n]) if has_res else (outs[:n], outs[n:])


def _loss(xl, target):
    Lp, D = xl.shape

    def body(x_ref, t_ref, dy_ref, dy16_ref, loss_ref):
        n = pl.program_id(0)

        @pl.when(n == 0)
        def _():
            loss_ref[...] = jnp.zeros_like(loss_ref)
            dy_ref[...] = jnp.zeros_like(dy_ref)
            dy16_ref[...] = jnp.zeros_like(dy16_ref)

        @pl.when(n > 0)
        def _():
            err = x_ref[...] - t_ref[...]
            dy = err * (1.0 / D)
            dy_ref[...] = dy
            dy16_ref[...] = dy.astype(BF16)
            loss_ref[...] += jnp.sum(err * err) * (0.5 / D)

    blk = pl.BlockSpec((BLOCK, D), lambda n: (n, 0))
    dy, dy16, loss = _pcall(body, name="loss_head", grid=(Lp // BLOCK,),
                            in_specs=[blk, pl.BlockSpec((BLOCK, D), lambda n: (jnp.maximum(n - 1, 0), 0))],
                            out_specs=[blk, blk, pl.BlockSpec((8, LANES), lambda n: (0, 0))],
                            out_shape=[_sds((Lp, D), F32), _sds((Lp, D), BF16), _sds((8, LANES), F32)],
                            compiler_params=_params(("arbitrary",)))(xl, target)
    return loss[0, 0], dy, dy16


GROUP_ROWS = KV_GROUP * BLOCK


def _attn_mask_dist(n):
    r = lax.broadcasted_iota(jnp.int32, (GROUP_ROWS, 3 * BLOCK), 0)
    i = r & (BLOCK - 1)
    j = lax.broadcasted_iota(jnp.int32, (GROUP_ROWS, 3 * BLOCK), 1)
    in_band = j < 2 * BLOCK
    band = in_band & (j > i) & (j <= i + BLOCK) & (j >= 2 * BLOCK - BLOCK * n)
    jm = j - 2 * BLOCK
    meta = (~in_band) & (jm >= PAD) & (jm <= BLOCK * n + i)
    dist = jnp.where(in_band, BLOCK + i - j, BLOCK * n + i - jm).astype(F32)
    return band | meta, dist


def _head_norm(x, g):
    r = lax.rsqrt(jnp.mean(x * x, axis=-1, keepdims=True) + NORM_EPS)
    return (x * r) * g, r


def _attn_specs(attn_w, kv_w):
    kb = attn_w // kv_w
    q_spec = pl.BlockSpec((BLOCK, attn_w), lambda n: (n, 0))

    def kv(col):
        return [pl.BlockSpec((BLOCK, kv_w), lambda n: (jnp.maximum(n - 1, 0), col)),
                pl.BlockSpec((BLOCK, kv_w), lambda n: (n, col)),
                pl.BlockSpec((BLOCK, kv_w), lambda n: (0, col))]

    return q_spec, kv(kb), kv(kb + 1)


def _slopes(n_heads):
    return [2.0 ** (-8.0 * (h + 1) / n_heads) for h in range(n_heads)]


def _head_slice(h):
    return slice(h * HEAD_DIM, (h + 1) * HEAD_DIM)


def _stack_heads(ref, kh):
    return jnp.concatenate([ref[:, _head_slice(kh * KV_GROUP + g)] for g in range(KV_GROUP)], axis=0)


def _group_column(vals):
    return jnp.concatenate([jnp.broadcast_to(v, (BLOCK, 1)) for v in vals], axis=0)


def _group_inputs(kh, slopes, q_ref, kp, kc, km, vp, vc, vm, gq_ref, gk_ref, sk_ref):
    cs = _head_slice(kh)
    kn, _ = _head_norm(jnp.concatenate([kp[:, cs], kc[:, cs], km[:, cs]], axis=0), gk_ref[...])
    vcat = jnp.concatenate([vp[:, cs], vc[:, cs], vm[:, cs]], axis=0).astype(BF16)
    q = _stack_heads(q_ref, kh)
    qn, rq = _head_norm(q, gq_ref[...])
    heads = range(kh * KV_GROUP, (kh + 1) * KV_GROUP)
    slope = _group_column([jnp.full((1, 1), slopes[h], F32) for h in heads])
    sink = _group_column([sk_ref[0:1, h:h + 1] for h in heads])
    return q, qn, rq, kn, vcat, slope, sink


def _scores(qn, kn, slope, sink, mask, dist):
    s = lax.dot_general(qn.astype(BF16), kn.astype(BF16), (((1,), (1,)), ((), ())), preferred_element_type=F32)
    s = s * (1.0 / math.sqrt(HEAD_DIM)) - slope * dist
    s = jnp.where(mask, s, NEG_INF)
    m = jnp.maximum(jnp.max(s, axis=-1, keepdims=True), sink)
    p = jnp.exp(s - m)
    ps = jnp.exp(sink - m)
    inv = 1.0 / (jnp.sum(p, axis=-1, keepdims=True) + ps)
    return p * inv, ps * inv


def _attn_fwd(proj, gq, gk, sinks, attn_w, kv_w):
    Lp = proj.shape[0]
    n_heads, n_kv = attn_w // HEAD_DIM, kv_w // HEAD_DIM
    slopes = _slopes(n_heads)
    q_spec, k_specs, v_specs = _attn_specs(attn_w, kv_w)

    def body(q_ref, kp, kc, km, vp, vc, vm, gq_ref, gk_ref, sk_ref, o_ref):
        mask, dist = _attn_mask_dist(pl.program_id(0))
        for kh in range(n_kv):
            _, qn, _, kn, vcat, slope, sink = _group_inputs(kh, slopes, q_ref, kp, kc, km, vp, vc, vm, gq_ref, gk_ref, sk_ref)
            p, _ = _scores(qn, kn, slope, sink, mask, dist)
            o = jnp.dot(p.astype(BF16), vcat, preferred_element_type=F32)
            for g in range(KV_GROUP):
                o_ref[:, _head_slice(kh * KV_GROUP + g)] = o[g * BLOCK:(g + 1) * BLOCK]

    small = lambda w: pl.BlockSpec((1, w), lambda n: (0, 0))
    return _pcall(body, name="attn_fwd", grid=(Lp // BLOCK,),
                  in_specs=[q_spec] + k_specs + v_specs + [small(HEAD_DIM), small(HEAD_DIM), small(n_heads)],
                  out_specs=pl.BlockSpec((BLOCK, attn_w), lambda n: (n, 0)), out_shape=_sds((Lp, attn_w), F32),
                  compiler_params=_params(("parallel",)))(proj, proj, proj, proj, proj, proj, proj, gq, gk, sinks)


def _attn_bwd(proj, attn, dattn, gq, gk, sinks, attn_w, kv_w):
    Lp = proj.shape[0]
    n_heads, n_kv = attn_w // HEAD_DIM, kv_w // HEAD_DIM
    slopes = _slopes(n_heads)
    q_spec, k_specs, v_specs = _attn_specs(attn_w, kv_w)
    scale = 1.0 / math.sqrt(HEAD_DIM)
    tn_dims = (((0,), (0,)), ((), ()))

    def body(q_ref, kp, kc, km, vp, vc, vm, o_ref, do_ref, gq_ref, gk_ref, sk_ref, dq_ref, dk_ref, dv_ref, dgq_ref, dsk_ref):
        n = pl.program_id(0)

        @pl.when(n == 0)
        def _():
            dk_ref[...] = jnp.zeros_like(dk_ref)
            dv_ref[...] = jnp.zeros_like(dv_ref)
            dgq_ref[...] = jnp.zeros_like(dgq_ref)
            dsk_ref[...] = jnp.zeros_like(dsk_ref)

        mask, dist = _attn_mask_dist(n)
        lane = lax.broadcasted_iota(jnp.int32, (1, n_heads), 1)
        rows_prev = pl.ds(pl.multiple_of(jnp.maximum(n - 1, 0) * BLOCK, BLOCK), BLOCK)
        rows_cur = pl.ds(pl.multiple_of(n * BLOCK, BLOCK), BLOCK)
        rows_meta = pl.ds(0, BLOCK)
        dgq = jnp.zeros((1, HEAD_DIM), F32)
        dsk = jnp.zeros((1, n_heads), F32)
        for kh in range(n_kv):
            cs = _head_slice(kh)
            q, qn, rq, kn, vcat, slope, sink = _group_inputs(kh, slopes, q_ref, kp, kc, km, vp, vc, vm, gq_ref, gk_ref, sk_ref)
            p, ps = _scores(qn, kn, slope, sink, mask, dist)
            do = _stack_heads(do_ref, kh)
            dd = jnp.sum(do * _stack_heads(o_ref, kh), axis=-1, keepdims=True)
            do16 = do.astype(BF16)
            dp = lax.dot_general(do16, vcat, (((1,), (1,)), ((), ())), preferred_element_type=F32)
            ds16 = (p * (dp - dd)).astype(BF16)
            dsink = -ps * dd
            for g in range(KV_GROUP):
                dsk = dsk + jnp.where(lane == kh * KV_GROUP + g, jnp.sum(dsink[g * BLOCK:(g + 1) * BLOCK]), 0.0)
            dqn = jnp.dot(ds16, kn.astype(BF16), preferred_element_type=F32) * scale
            dkn = lax.dot_general(ds16, qn.astype(BF16), tn_dims, preferred_element_type=F32) * scale
            dvc = lax.dot_general(p.astype(BF16), do16, tn_dims, preferred_element_type=F32)
            xh = q * rq
            dgq = dgq + jnp.sum(dqn * xh, axis=0, keepdims=True)
            dyg = dqn * gq_ref[...]
            dq = rq * (dyg - xh * jnp.mean(dyg * xh, axis=-1, keepdims=True))
            for g in range(KV_GROUP):
                dq_ref[:, _head_slice(kh * KV_GROUP + g)] = dq[g * BLOCK:(g + 1) * BLOCK].astype(dq_ref.dtype)
            for part, rows in enumerate((rows_prev, rows_cur, rows_meta)):
                ps_ = slice(part * BLOCK, (part + 1) * BLOCK)
                dk_ref[rows, cs] += dkn[ps_]
                dv_ref[rows, cs] += dvc[ps_]
        dgq_ref[...] += dgq
        dsk_ref[...] += dsk

    small = lambda w: pl.BlockSpec((1, w), lambda n: (0, 0))
    blk = pl.BlockSpec((BLOCK, attn_w), lambda n: (n, 0))
    whole = pl.BlockSpec((Lp, kv_w), lambda n: (0, 0))
    return _pcall(body, name="attn_bwd", grid=(Lp // BLOCK,),
                  in_specs=[q_spec] + k_specs + v_specs + [blk, blk, small(HEAD_DIM), small(HEAD_DIM), small(n_heads)],
                  out_specs=[blk, whole, whole, small(HEAD_DIM), small(n_heads)],
                  out_shape=[_sds(proj.shape, BF16), _sds((Lp, kv_w), F32), _sds((Lp, kv_w), F32),
                             _sds((1, HEAD_DIM), F32), _sds((1, n_heads), F32)],
                  compiler_params=_params(("arbitrary",)))(proj, proj, proj, proj, proj, proj, proj, attn, dattn, gq, gk, sinks)


def _knorm_bwd(proj, dkn, dv, gk, dproj, attn_w, kv_w):
    Lp = proj.shape[0]
    n_kv = kv_w // HEAD_DIM
    tr = _pick(Lp, (384, 256, 128))

    def body(k_ref, d_ref, dv_ref, g_ref, buf_ref, out_ref, dg_ref):
        @pl.when(pl.program_id(0) == 0)
        def _():
            dg_ref[...] = jnp.zeros_like(dg_ref)

        dg = jnp.zeros((1, HEAD_DIM), F32)
        for kh in range(n_kv):
            cs = slice(kh * HEAD_DIM, (kh + 1) * HEAD_DIM)
            x = k_ref[:, cs]
            d = d_ref[:, cs]
            r = lax.rsqrt(jnp.mean(x * x, axis=-1, keepdims=True) + NORM_EPS)
            xh = x * r
            dg = dg + jnp.sum(d * xh, axis=0, keepdims=True)
            dyg = d * g_ref[...]
            out_ref[:, cs] = (r * (dyg - xh * jnp.mean(dyg * xh, axis=-1, keepdims=True))).astype(out_ref.dtype)
        out_ref[:, kv_w:] = dv_ref[...].astype(out_ref.dtype)
        dg_ref[...] += dg

    kv_blk = pl.BlockSpec((tr, kv_w), lambda i: (i, 0))
    return _pcall(body, name="knorm_bwd", grid=(Lp // tr,),
                  in_specs=[pl.BlockSpec((tr, kv_w), lambda i: (i, attn_w // kv_w)), kv_blk, kv_blk,
                            pl.BlockSpec((1, HEAD_DIM), lambda i: (0, 0)), pl.BlockSpec(memory_space=pl.ANY)],
                  out_specs=[pl.BlockSpec((tr, 2 * kv_w), lambda i: (i, attn_w // (2 * kv_w))), pl.BlockSpec((1, HEAD_DIM), lambda i: (0, 0))],
                  out_shape=[_sds(dproj.shape, dproj.dtype), _sds((1, HEAD_DIM), F32)],
                  input_output_aliases={4: 0},
                  compiler_params=_params(("arbitrary",)))(proj, dkn, dv, gk, dproj)


def _ssm_bbar(lr, li, ls, br, bi):
    def fn(lr, li, ls, br, bi):
        fr, fi = _zoh_factor(lr, li, ls)
        return fr * br - fi * bi, fr * bi + fi * br

    return _ew("ssm_bbar", fn, [lr, li, ls, br, bi], [F32, F32])


def _lam_bar(lr, li, ls):
    dl = jnp.exp(ls)
    e = jnp.exp(lr * dl)
    return e * jnp.cos(li * dl), e * jnp.sin(li * dl), dl


def _zoh_factor(lr, li, ls):
    ar, ai, _ = _lam_bar(lr, li, ls)
    n2 = lr * lr + li * li
    ivr, ivi = lr / n2, -li / n2
    return (ar - 1.0) * ivr - ai * ivi, (ar - 1.0) * ivi + ai * ivr


SCAN_SHIFTS = (1, 2, 4)


def _ssm_tables(lr, li, ls):
    Wx = lr.shape[1]

    def body(lr_ref, li_ref, ls_ref, tf_ref, tr_ref):
        dl = jnp.exp(ls_ref[...])
        zr, zi = lr_ref[...] * dl, li_ref[...] * dl
        row = lax.broadcasted_iota(jnp.int32, (8, Wx), 0)

        def power(kf):
            e = jnp.exp(kf * zr)
            return e * jnp.cos(kf * zi), e * jnp.sin(kf * zi)

        for ref, rev in ((tf_ref, False), (tr_ref, True)):
            sgn = -1.0 if rev else 1.0
            for k, d in enumerate(SCAN_SHIFTS):
                ar, ai = power(jnp.full((8, Wx), float(d), F32))
                keep = (row < 8 - d) if rev else (row >= d)
                ref[k] = jnp.where(keep, ar, 0.0)
                ref[4 + k] = jnp.where(keep, sgn * ai, 0.0)
            pr, pi = power(((8 - row) if rev else (row + 1)).astype(F32))
            ref[3] = pr
            ref[7] = sgn * pi

    full = pl.BlockSpec((1, Wx), lambda: (0, 0))
    tab = pl.BlockSpec((8, 8, Wx), lambda: (0, 0, 0))
    return _pcall(body, name="ssm_tables", in_specs=[full] * 3, out_specs=[tab, tab],
                  out_shape=[_sds((8, 8, Wx), F32)] * 2,
                  compiler_params=pltpu.CompilerParams(vmem_limit_bytes=V7X_VMEM_LIMIT_BYTES))(lr, li, ls)


def _scan(name, br, bi, tab, layer, reverse, states=None):
    L, Wx = br.shape
    TB = _pick(L, (384, 256, 128))
    CW = _pick(Wx, (1024, 512, 256, 128))
    nT, nG = L // TB, TB // 8

    def body(*refs):
        if reverse:
            br_ref, bi_ref, xr_ref, xi_ref, tab_ref, or_ref, oi_ref, s1_ref, s2_ref, cr_ref, ci_ref = refs
        else:
            br_ref, bi_ref, tab_ref, or_ref, oi_ref, cr_ref, ci_ref = refs

        @pl.when(pl.program_id(1) == 0)
        def _():
            cr_ref[...] = jnp.zeros_like(cr_ref)
            ci_ref[...] = jnp.zeros_like(ci_ref)
            if reverse:
                s1_ref[...] = jnp.zeros_like(s1_ref)
                s2_ref[...] = jnp.zeros_like(s2_ref)

        def step(q, carry):
            cr, ci = carry[0], carry[1]
            g = (nG - 1 - q) if reverse else q
            rows = pl.ds(pl.multiple_of(g * 8, 8), 8)
            b_r, b_i = br_ref[rows, :], bi_ref[rows, :]
            sr, si = b_r, b_i
            for k, d in enumerate(SCAN_SHIFTS):
                mr, mi = tab_ref[k], tab_ref[4 + k]
                sh = (8 - d) if reverse else d
                pr, pi = pltpu.roll(sr, sh, 0), pltpu.roll(si, sh, 0)
                sr, si = sr + mr * pr - mi * pi, si + mr * pi + mi * pr
            pwr, pwi = tab_ref[3], tab_ref[7]
            xr = sr + pwr * cr - pwi * ci
            xi = si + pwr * ci + pwi * cr
            or_ref[rows, :] = xr
            oi_ref[rows, :] = xi
            row = 0 if reverse else 7
            out = (jnp.broadcast_to(xr[row:row + 1, :], xr.shape), jnp.broadcast_to(xi[row:row + 1, :], xi.shape))
            if reverse:
                hr, hi = xr - b_r, xi - b_i
                st_r, st_i = xr_ref[rows, :], xi_ref[rows, :]
                out = out + (carry[2] + hr * st_r + hi * st_i, carry[3] + hi * st_r - hr * st_i)
            return out

        init = (cr_ref[...], ci_ref[...])
        if reverse:
            init = init + (jnp.zeros((8, CW), F32), jnp.zeros((8, CW), F32))
        fin = lax.fori_loop(0, nG, step, init, unroll=2)
        cr_ref[...] = fin[0]
        ci_ref[...] = fin[1]
        if reverse:
            s1_ref[...] += fin[2]
            s2_ref[...] += fin[3]

    tmap = (lambda j, t: (nT - 1 - t, j)) if reverse else (lambda j, t: (t, j))
    blk = pl.BlockSpec((TB, CW), tmap)
    tab_spec = pl.BlockSpec((8, 8, CW), lambda j, t: (0, 0, layer * (Wx // CW) + j))
    sum_spec = pl.BlockSpec((8, CW), lambda j, t: (0, j))
    ins = [br, bi] + (list(states) if reverse else []) + [tab]
    in_specs = [blk, blk] + ([blk, blk] if reverse else []) + [tab_spec]
    out_specs = [blk, blk] + ([sum_spec, sum_spec] if reverse else [])
    out_shape = [_sds((L, Wx), F32)] * 2 + ([_sds((8, Wx), F32)] * 2 if reverse else [])
    return _pcall(body, name=name, grid=(Wx // CW, nT), in_specs=in_specs, out_specs=out_specs, out_shape=out_shape,
                  scratch_shapes=[pltpu.VMEM((8, CW), F32), pltpu.VMEM((8, CW), F32)],
                  compiler_params=_params(("parallel", "arbitrary")))(*ins)


def _row_tile(L):
    return _pick(L, (1408, 704, 384, 128))


TILES_PER_BLOCK = 4


def _blockproj(name, src, off, w_r, w_i, layer, depth):
    L = src.shape[0]
    T = w_r.shape[0] // depth
    tm = _row_tile(L)
    wide = TILES_PER_BLOCK * LANES

    def body(s_ref, wr_ref, wi_ref, or_ref, oi_ref):
        s = s_ref[...].astype(BF16)
        for k in range(TILES_PER_BLOCK):
            cols = slice(k * LANES, (k + 1) * LANES)
            or_ref[:, cols] = jnp.dot(s, wr_ref[k], preferred_element_type=F32)
            oi_ref[:, cols] = jnp.dot(s, wi_ref[k], preferred_element_type=F32)

    w_spec = pl.BlockSpec((TILES_PER_BLOCK, LANES, LANES), lambda i, q: (layer * (T // TILES_PER_BLOCK) + q, 0, 0))
    o_spec = pl.BlockSpec((tm, wide), lambda i, q: (i, q))
    return _pcall(body, name=name, grid=(L // tm, T // TILES_PER_BLOCK),
                  in_specs=[pl.BlockSpec((tm, LANES), lambda i, q: (i, off + q)), w_spec, w_spec],
                  out_specs=[o_spec, o_spec], out_shape=[_sds((L, T * LANES), F32)] * 2,
                  compiler_params=_params(("parallel", "arbitrary")))(src, w_r, w_i)


def _blockproj_grad(name, src, off, gr, gi):
    L = src.shape[0]
    T = gr.shape[1] // LANES
    tm = _row_tile(L)
    wide = TILES_PER_BLOCK * LANES
    tn_dims = (((0,), (0,)), ((), ()))

    def body(s_ref, gr_ref, gi_ref, or_ref, oi_ref):
        @pl.when(pl.program_id(1) == 0)
        def _():
            or_ref[...] = jnp.zeros_like(or_ref)
            oi_ref[...] = jnp.zeros_like(oi_ref)

        s = s_ref[...].astype(BF16)
        for k in range(TILES_PER_BLOCK):
            cols = slice(k * LANES, (k + 1) * LANES)
            or_ref[k] += lax.dot_general(s, gr_ref[:, cols].astype(BF16), tn_dims, preferred_element_type=F32)
            oi_ref[k] += lax.dot_general(s, gi_ref[:, cols].astype(BF16), tn_dims, preferred_element_type=F32)

    g_spec = pl.BlockSpec((tm, wide), lambda q, i: (i, q))
    o_spec = pl.BlockSpec((TILES_PER_BLOCK, LANES, LANES), lambda q, i: (q, 0, 0))
    return _pcall(body, name=name, grid=(T // TILES_PER_BLOCK, L // tm),
                  in_specs=[pl.BlockSpec((tm, LANES), lambda q, i: (i, off + q)), g_spec, g_spec],
                  out_specs=[o_spec, o_spec], out_shape=[_sds((T, LANES, LANES), F32)] * 2,
                  compiler_params=_params(("parallel", "arbitrary")))(src, gr, gi)


def _gelu(y):
    k = math.sqrt(2.0 / math.pi)
    return 0.5 * y * (1.0 + jnp.tanh(k * (y + 0.044715 * (y * y * y))))


def _gelu_grad(y):
    k = math.sqrt(2.0 / math.pi)
    t = jnp.tanh(k * (y + 0.044715 * (y * y * y)))
    return 0.5 * (1.0 + t) + 0.5 * y * (1.0 - t * t) * (k * (1.0 + 3 * 0.044715 * (y * y)))


def _ssm_out(xr, xi, w_r, w_i, proj, u_off, dvec, layer, depth):
    L = xr.shape[0]
    J = w_r.shape[0] // depth
    SW = w_r.shape[1]
    tm = _row_tile(L)

    def body(xr_ref, xi_ref, wr_ref, wi_ref, u_ref, d_ref, y_ref, gl_ref):
        acc = jnp.dot(xr_ref[...].astype(BF16), wr_ref[...], preferred_element_type=F32)
        acc += jnp.dot(xi_ref[...].astype(BF16), wi_ref[...], preferred_element_type=F32)
        y = acc + d_ref[...] * u_ref[...]
        y_ref[...] = y
        gl_ref[...] = _gelu(y)

    x_spec = pl.BlockSpec((tm, SW), lambda j, i: (i, j))
    w_spec = pl.BlockSpec((None, SW, LANES), lambda j, i: (layer * J + j, 0, 0))
    o_spec = pl.BlockSpec((tm, LANES), lambda j, i: (i, j))
    return _pcall(body, name="ssm_out", grid=(J, L // tm),
                  in_specs=[x_spec, x_spec, w_spec, w_spec, pl.BlockSpec((tm, LANES), lambda j, i: (i, u_off + j)),
                            pl.BlockSpec((1, LANES), lambda j, i: (0, j))],
                  out_specs=[o_spec, o_spec], out_shape=[_sds((L, J * LANES), F32)] * 2,
                  compiler_params=_params(("parallel", "parallel")))(xr, xi, w_r, w_i, proj, dvec)


def _ssm_du(gr, gi, w_r, w_i, dy, proj, u_off, dvec, dproj, layer, depth):
    L = gr.shape[0]
    J = w_r.shape[0] // depth
    SW = w_r.shape[1]
    tm = _row_tile(L)

    def body(gr_ref, gi_ref, wr_ref, wi_ref, dy_ref, u_ref, d_ref, buf_ref, du_ref, dd_ref):
        i = pl.program_id(1)

        @pl.when(i == 0)
        def _():
            dd_ref[...] = jnp.zeros_like(dd_ref)

        acc = jnp.dot(gr_ref[...].astype(BF16), wr_ref[...], preferred_element_type=F32)
        acc += jnp.dot(gi_ref[...].astype(BF16), wi_ref[...], preferred_element_type=F32)
        dy = dy_ref[...]
        row = lax.broadcasted_iota(jnp.int32, (tm, LANES), 0) + i * tm
        du_ref[...] = jnp.where(row >= PAD, acc + d_ref[...] * dy, 0.0).astype(du_ref.dtype)
        dd_ref[...] += jnp.sum(dy * u_ref[...], axis=0, keepdims=True)

    x_spec = pl.BlockSpec((tm, SW), lambda j, i: (i, j))
    w_spec = pl.BlockSpec((None, SW, LANES), lambda j, i: (layer * J + j, 0, 0))
    o_spec = pl.BlockSpec((tm, LANES), lambda j, i: (i, j))
    u_spec = pl.BlockSpec((tm, LANES), lambda j, i: (i, u_off + j))
    vec = pl.BlockSpec((1, LANES), lambda j, i: (0, j))
    return _pcall(body, name="ssm_du", grid=(J, L // tm),
                  in_specs=[x_spec, x_spec, w_spec, w_spec, o_spec, u_spec, vec, pl.BlockSpec(memory_space=pl.ANY)],
                  out_specs=[u_spec, vec], out_shape=[_sds(dproj.shape, dproj.dtype), _sds((1, J * LANES), F32)],
                  input_output_aliases={7: 0},
                  compiler_params=_params(("parallel", "arbitrary")))(gr, gi, w_r, w_i, dy, proj, dvec, dproj)


def _ssm_dc(xr, xi, dy, SW):
    L = xr.shape[0]
    J = dy.shape[1] // LANES
    tm = _row_tile(L)
    tn_dims = (((0,), (0,)), ((), ()))

    def body(xr_ref, xi_ref, dy_ref, or_ref, oi_ref):
        @pl.when(pl.program_id(1) == 0)
        def _():
            or_ref[...] = jnp.zeros_like(or_ref)
            oi_ref[...] = jnp.zeros_like(oi_ref)

        d = dy_ref[...].astype(BF16)
        or_ref[...] += lax.dot_general(xr_ref[...].astype(BF16), d, tn_dims, preferred_element_type=F32)
        oi_ref[...] += lax.dot_general(xi_ref[...].astype(BF16), d, tn_dims, preferred_element_type=F32)

    x_spec = pl.BlockSpec((tm, SW), lambda j, i: (i, j))
    o_spec = pl.BlockSpec((None, SW, LANES), lambda j, i: (j, 0, 0))
    return _pcall(body, name="ssm_dc", grid=(J, L // tm),
                  in_specs=[x_spec, x_spec, pl.BlockSpec((tm, LANES), lambda j, i: (i, j))],
                  out_specs=[o_spec, o_spec], out_shape=[_sds((J, SW, LANES), F32)] * 2,
                  compiler_params=_params(("parallel", "arbitrary")))(xr, xi, dy)


def _glu_dz(ds, gl, z):
    L, W = ds.shape
    tr = _pick(L, (384, 256, 128))

    def body(ds_ref, gl_ref, z_ref, dz_ref, db_ref):
        @pl.when(pl.program_id(0) == 0)
        def _():
            db_ref[...] = jnp.zeros_like(db_ref)

        sg = jax.nn.sigmoid(z_ref[...])
        dz = ds_ref[...] * gl_ref[...] * (sg * (1.0 - sg))
        dz_ref[...] = dz.astype(BF16)
        db_ref[...] += jnp.sum(dz, axis=0, keepdims=True)

    spec = pl.BlockSpec((tr, W), lambda i: (i, 0))
    vec = pl.BlockSpec((1, W), lambda i: (0, 0))
    return _pcall(body, name="glu_dz", grid=(L // tr,), in_specs=[spec] * 3, out_specs=[spec, vec],
                  out_shape=[_sds((L, W), BF16), _sds((1, W), F32)], compiler_params=_params(("arbitrary",)))(ds, gl, z)


def _ssm_param_bwd_flat(lr, li, ls, br, bi, dbbr, dbbi):
    def seg_sum(x):
        for s in (8, 4, 2, 1):
            x = x + pltpu.roll(x, LANES - s, 1)
        return x

    def fn(lr, li, ls, br, bi, dbbr, dbbi):
        fr, fi = _zoh_factor(lr, li, ls)
        return (fr * dbbr + fi * dbbi, fr * dbbi - fi * dbbr,
                seg_sum(br * dbbr + bi * dbbi), seg_sum(br * dbbi - bi * dbbr))

    return _ew("ssm_param_bwd_flat", fn, [lr, li, ls, br, bi, dbbr, dbbi], [F32] * 4)


def _ssm_param_bwd(lr, li, ls, dfr, dfi, s1, s2):
    G, P = lr.shape

    def body(lr_ref, li_ref, ls_ref, dfr_ref, dfi_ref, s1_ref, s2_ref, dlr_ref, dli_ref, dls_ref):
        lr, li = lr_ref[...], li_ref[...]
        ar, ai, dl = _lam_bar(lr, li, ls_ref[...])
        sr, si = s1_ref[0], s2_ref[0]
        for k in range(1, 8):
            sr = sr + s1_ref[k]
            si = si + s2_ref[k]
        a2 = ar * ar + ai * ai
        gar, gai = (sr * ar - si * ai) / a2, (sr * ai + si * ar) / a2
        n2 = lr * lr + li * li
        ivr, ivi = lr / n2, -li / n2
        fr = (ar - 1.0) * ivr - ai * ivi
        fi = (ar - 1.0) * ivi + ai * ivr
        dfr, dfi = dfr_ref[...], dfi_ref[...]
        gar = gar + ivr * dfr + ivi * dfi
        gai = gai + ivr * dfi - ivi * dfr
        wr, wi = -(fr * ivr - fi * ivi), -(fr * ivi + fi * ivr)
        glr, gli = wr * dfr + wi * dfi, wr * dfi - wi * dfr
        gzr, gzi = ar * gar + ai * gai, ar * gai - ai * gar
        dlr_ref[...] = glr + dl * gzr
        dli_ref[...] = gli + dl * gzi
        dls_ref[...] = dl * jnp.sum(lr * gzr + li * gzi, axis=-1, keepdims=True)

    m = pl.BlockSpec((G, P), lambda: (0, 0))
    v = pl.BlockSpec((G, 1), lambda: (0, 0))
    s = pl.BlockSpec((8, G, P), lambda: (0, 0, 0))
    return _pcall(body, name="ssm_param_bwd", in_specs=[m, m, v, m, m, s, s], out_specs=[m, m, v],
                  out_shape=[_sds((G, P), F32), _sds((G, P), F32), _sds((G, 1), F32)])(lr, li, ls, dfr, dfi, s1, s2)


def _tile_mask(G):
    T = G // 2
    e = np.zeros((T, 8, 1, 2, 1), np.float32)
    for t in range(T):
        for c in range(2):
            e[t, (2 * t + c) % 8, 0, c, 0] = 1.0
    return e


def _tile_w(arr):
    G = arr.shape[0]
    a = arr.reshape(G // 2, 1, 2, STATE, GROUP_CH).transpose(0, 1, 4, 2, 3)
    return (a * _tile_mask(G)).reshape(G // 2, LANES, LANES).astype(BF16)


def _tile_w_grad(dw):
    G = dw.shape[0] * 2
    d = dw.reshape(G // 2, 8, GROUP_CH, 2, STATE) * _tile_mask(G)
    return d.sum(axis=1).transpose(0, 2, 3, 1).reshape(G, STATE, GROUP_CH)


def _slab_w(arr):
    G = arr.shape[0]
    a = arr.reshape(G // 8, 8, STATE, 1, GROUP_CH)
    eye = np.eye(8, dtype=np.float32).reshape(1, 8, 1, 8, 1)
    return (a * eye).reshape(G // 8, 8 * STATE, LANES).astype(BF16)


def _slab_w_grad(dw):
    J = dw.shape[0]
    eye = np.eye(8, dtype=np.float32).reshape(1, 8, 1, 8, 1)
    return (dw.reshape(J, 8, STATE, 8, GROUP_CH) * eye).sum(axis=3).reshape(J * 8, STATE, GROUP_CH)


def _exchange(name, ins, out_sds, remote, local, aliases=None):
    n_in, n_out, n_r, n_l = len(ins), len(out_sds), len(remote), len(local)

    def body(*refs):
        in_refs, out_refs = refs[:n_in], refs[n_in:n_in + n_out]
        send_sems, recv_sems, local_sems = refs[n_in + n_out:]
        x, y, c = lax.axis_index("x"), lax.axis_index("y"), lax.axis_index("c")

        def place(px, py, pc):
            return dict(x=px, y=py, c=pc, chip=2 * px + py)

        def flip(mask):
            mx, my, mc = mask
            return ((1 - x) if mx else x, (1 - y) if my else y, (1 - c) if mc else c)

        me = place(x, y, c)
        sends = []
        for k, (ii, src, oi, dst, mask) in enumerate(remote):
            cp = pltpu.make_async_remote_copy(src_ref=src(in_refs[ii], me), dst_ref=dst(out_refs[oi], me),
                                              send_sem=send_sems.at[k], recv_sem=recv_sems.at[k],
                                              device_id=flip(mask), device_id_type=MESH)
            cp.start()
            sends.append(cp)
        locals_ = []
        for k, (ii, src, oi, dst) in enumerate(local):
            cp = pltpu.make_async_copy(src(in_refs[ii], me), dst(out_refs[oi], me), local_sems.at[k])
            cp.start()
            locals_.append(cp)
        for k, (ii, src, oi, dst, mask) in enumerate(remote):
            sends[k].wait_send()
            peer = flip(mask)
            pltpu.make_async_remote_copy(src_ref=src(in_refs[ii], me), dst_ref=dst(out_refs[oi], place(*peer)),
                                         send_sem=send_sems.at[k], recv_sem=recv_sems.at[k],
                                         device_id=peer, device_id_type=MESH).wait_recv()
        for cp in locals_:
            cp.wait()

    any_spec = pl.BlockSpec(memory_space=pl.ANY)
    return _pcall(body, name=name, in_specs=[any_spec] * n_in, out_specs=[any_spec] * n_out, out_shape=list(out_sds),
                  input_output_aliases=aliases or {},
                  scratch_shapes=[pltpu.SemaphoreType.DMA((n_r,)), pltpu.SemaphoreType.DMA((n_r,)),
                                  pltpu.SemaphoreType.DMA((max(n_l, 1),))])(*ins)


def _mesh_place():
    x, y, c = lax.axis_index("x"), lax.axis_index("y"), lax.axis_index("c")

    def place(px, py, pc):
        return dict(x=px, y=py, c=pc, chip=2 * px + py)

    def flip(mask):
        mx, my, mc = mask
        return ((1 - x) if mx else x, (1 - y) if my else y, (1 - c) if mc else c)

    return place(x, y, c), place, flip


_HBM = pl.BlockSpec(memory_space=pltpu.HBM)
_SEM = pl.BlockSpec(memory_space=pltpu.SEMAPHORE)
_EFFECT = pltpu.SideEffectType.DATAFLOW_SIDE_EFFECTING


def _split_start(name, bufs, groups):
    n, ng = len(bufs), len(groups)

    def body(*refs):
        in_refs, sems, token = refs[:n], refs[n:n + 2 * ng], refs[-1]
        me, _, flip = _mesh_place()
        for g, copies in enumerate(groups):
            for k, (si, src, di, dst, mask) in enumerate(copies):
                pltpu.make_async_remote_copy(src_ref=src(in_refs[si], me), dst_ref=dst(in_refs[di], me),
                                             send_sem=sems[2 * g].at[k], recv_sem=sems[2 * g + 1].at[k],
                                             device_id=flip(mask), device_id_type=MESH).start()
        token[...] = jnp.zeros_like(token)

    outs = _pcall(body, name=name,
                  out_shape=(*[pltpu.SemaphoreType.DMA((len(g),)) for g in groups for _ in range(2)],
                             *[pltpu.HBM(b.shape, b.dtype) for b in bufs], _sds((8, LANES), F32)),
                  in_specs=[_HBM] * n, out_specs=(*[_SEM] * (2 * ng), *[_HBM] * n, pl.BlockSpec(memory_space=pltpu.VMEM)),
                  input_output_aliases={i: 2 * ng + i for i in range(n)},
                  compiler_params=pltpu.CompilerParams(has_side_effects=_EFFECT),
                  )(*[pltpu.with_memory_space_constraint(b, pltpu.HBM) for b in bufs])
    return [(outs[2 * g], outs[2 * g + 1]) for g in range(ng)], list(outs[2 * ng:2 * ng + n]), outs[-1]


def _split_wait(name, bufs, sems, after, remote):
    n = len(bufs)
    send_sems, recv_sems = sems

    def body(*refs):
        in_refs, ssem, rsem = refs[:n], refs[n], refs[n + 1]
        me, place, flip = _mesh_place()
        for k, (si, src, di, dst, mask) in enumerate(remote):
            peer = flip(mask)
            cp = pltpu.make_async_remote_copy(src_ref=src(in_refs[si], me), dst_ref=dst(in_refs[di], place(*peer)),
                                              send_sem=ssem.at[k], recv_sem=rsem.at[k], device_id=peer, device_id_type=MESH)
            cp.wait_send()
            cp.wait_recv()

    return list(_pcall(body, name=name, out_shape=tuple(pltpu.HBM(b.shape, b.dtype) for b in bufs),
                       in_specs=[_HBM] * n + [_SEM, _SEM, pl.BlockSpec(memory_space=pl.ANY)], out_specs=tuple([_HBM] * n),
                       input_output_aliases={i: i for i in range(n)},
                       compiler_params=pltpu.CompilerParams(has_side_effects=_EFFECT))(*bufs, send_sems, recv_sems, after))


CHIP_MASKS = ((0, 1, 0), (1, 0, 0), (1, 1, 0))
SIBLING = (0, 0, 1)


def _whole(ref, p):
    return ref


def _all_gather(name, shards, col_sharded):
    def dst_view(col):
        def view(ref, p):
            r, cdim = ref.shape[0] // (1 if col else N_CHIPS), ref.shape[1] // (N_CHIPS if col else 1)
            if col:
                return ref.at[:, pl.ds(pl.multiple_of(p["chip"] * cdim, LANES), cdim)]
            return ref.at[pl.ds(pl.multiple_of(p["chip"] * r, 8), r), :]
        return view

    out_sds = [_sds((s.shape[0], s.shape[1] * N_CHIPS) if col else (s.shape[0] * N_CHIPS, s.shape[1]), s.dtype)
               for s, col in zip(shards, col_sharded)]
    remote = [(a, _whole, a, dst_view(col), m) for a, col in enumerate(col_sharded) for m in CHIP_MASKS]
    local = [(a, _whole, a, dst_view(col)) for a, col in enumerate(col_sharded)]
    return _exchange(name, shards, out_sds, remote, local)


class _Place:
    def __getitem__(self, k):
        return lax.axis_index("c") if k == 0 else 2 * lax.axis_index("x") + lax.axis_index("y")


def _placed_call(body, name, grid, in_specs, out_specs, out_shape, sem, ins):
    def wrap(spec):
        return pl.BlockSpec(spec.block_shape, lambda *idx: spec.index_map(*idx, _Place()))

    outs = [wrap(s) for s in out_specs] if isinstance(out_specs, (list, tuple)) else wrap(out_specs)
    return _pcall(body, name=name, grid=grid, in_specs=[wrap(s) for s in in_specs], out_specs=outs, out_shape=out_shape,
                  compiler_params=_params(sem))(*ins)


def _rows_within(n, width, limit=512 * 1024):
    return _pick(n, tuple(t for t in (1024, 512, 256, 128, 64, 32, 16) if t * width <= limit) or (16,))


def _region_view(col):
    def view(ref, p):
        if col:
            cdim = ref.shape[1] // N_CHIPS
            return ref.at[:, pl.ds(pl.multiple_of(p["chip"] * cdim, LANES), cdim)]
        r = ref.shape[0] // N_CHIPS
        return ref.at[pl.ds(pl.multiple_of(p["chip"] * r, 16), r), :]
    return view


def _ag_place(name, w, layer, col, dtype):
    _, r, cdim = w.shape
    tr = _rows_within(r, cdim)
    nb = r // tr

    def body(w_ref, o_ref):
        o_ref[...] = w_ref[...].astype(dtype)

    if col:
        out_shape, out_spec = (r, N_CHIPS * cdim), pl.BlockSpec((tr, cdim), lambda i, pr: (i, pr[1]))
    else:
        out_shape, out_spec = (N_CHIPS * r, cdim), pl.BlockSpec((tr, cdim), lambda i, pr: (pr[1] * nb + i, 0))
    return _placed_call(body, name, (nb,), [pl.BlockSpec((None, tr, cdim), lambda i, pr: (layer, i, 0))], out_spec,
                        _sds(out_shape, dtype), ("parallel",), [w])


def _ag_copies(a, col):
    return [(a, _region_view(col), a, _region_view(col), m) for m in CHIP_MASKS]


def _rs_add2(name, g4, a4, out_dtype):
    J, _, h, C = g4.shape
    tr = _rows_within(h, C)

    def body(g_ref, a_ref, o_ref):
        o_ref[...] = (g_ref[...].astype(F32) + a_ref[...].astype(F32)).astype(o_ref.dtype)

    return _placed_call(body, name, (J, h // tr),
                        [pl.BlockSpec((None, None, tr, C), lambda j, i, pr: (j, pr[0], i, 0)),
                         pl.BlockSpec((None, None, tr, C), lambda j, i, pr: (j, 0, i, 0))],
                        pl.BlockSpec((None, tr, C), lambda j, i, pr: (j, i, 0)), _sds((J, h, C), out_dtype),
                        ("parallel", "parallel"), [g4, a4])


def _rs_add4(name, p3, landed, col):
    _, h, w = landed.shape
    tr = _rows_within(h, w)

    def body(p_ref, a_ref, b_ref, c_ref, o_ref):
        o_ref[...] = ((p_ref[...].astype(F32) + a_ref[...].astype(F32)) + b_ref[...].astype(F32)) + c_ref[...].astype(F32)

    own = (pl.BlockSpec((None, tr, w), lambda i, pr: (0, i, pr[1])) if col else pl.BlockSpec((None, tr, w), lambda i, pr: (pr[1], i, 0)))
    slot = lambda k: pl.BlockSpec((None, tr, w), lambda i, pr: (k, i, 0))
    return _placed_call(body, name, (h // tr,), [own, slot(0), slot(1), slot(2)], pl.BlockSpec((tr, w), lambda i, pr: (i, 0)),
                        _sds((h, w), F32), ("parallel",), [p3, landed, landed, landed])


def _rs_begin(tag, grads, col_sharded):
    n = len(grads)
    g4 = [g.reshape((1, 2, g.shape[0] // 2, g.shape[1]) if col else (N_CHIPS, 2, g.shape[0] // (2 * N_CHIPS), g.shape[1]))
          for g, col in zip(grads, col_sharded)]
    other_half = lambda ref, p: ref.at[:, pl.ds(1 - p["c"], 1)]
    theirs = _exchange("rs_sibling_w", g4, [_sds((g.shape[0], 1) + g.shape[2:], g.dtype) for g in g4],
                       [(a, other_half, a, _whole, SIBLING) for a in range(n)], [])
    chip_sum = [_rs_add2("rs_add2_w", g4[a], theirs[a], BF16) for a in range(n)]

    def send_view(col, mask):
        def view(ref, p):
            t = 2 * ((1 - p["x"]) if mask[0] else p["x"]) + ((1 - p["y"]) if mask[1] else p["y"])
            if col:
                sc = ref.shape[2] // N_CHIPS
                return ref.at[0, :, pl.ds(pl.multiple_of(t * sc, LANES), sc)]
            return ref.at[t]
        return view
    slot = lambda k: (lambda ref, p: ref.at[k])
    piece = [(s.shape[1], s.shape[2] // N_CHIPS if col else s.shape[2]) for s, col in zip(chip_sum, col_sharded)]
    landing = [lax.empty((len(CHIP_MASKS),) + s, BF16) for s in piece]
    copies = [(a, send_view(col_sharded[a], m), n + a, slot(k), m) for a in range(n) for k, m in enumerate(CHIP_MASKS)]
    (sems,), bufs, token = _split_start("rs_chips_start_" + tag, chip_sum + landing, [copies])
    return dict(sems=sems, bufs=bufs, copies=copies, token=token, col_sharded=col_sharded)


def _rs_finish(tag, st, after):
    col_sharded = st['col_sharded']
    n = len(col_sharded)
    bufs = _split_wait("rs_chips_wait_" + tag, st['bufs'], st['sems'], after, st['copies'])
    chip_sum, landed = bufs[:n], bufs[n:]
    mine = [_rs_add4("rs_add4_w", chip_sum[a], landed[a], col_sharded[a]) for a in range(n)]
    other = _exchange("rs_halves_w", mine, [_sds(m.shape, F32) for m in mine], [(a, _whole, a, _whole, SIBLING) for a in range(n)], [])
    return mine, other


def _adamw_big(name, mine, other, w, m, v):
    depth, R, C = w.shape
    h = R // 2
    tr = _pick(h, tuple(t for t in (512, 256, 128, 64, 32, 16, 8) if t * C <= 256 * 1024) or (8,))
    nb = h // tr

    def g_spec(kk, hh):
        def imap(l, s, i, pr):
            before = (l < kk) | ((l == kk) & (s < hh))
            return (jnp.where((l == kk) & (s == hh), i, jnp.where(before, 0, nb - 1)), 0)
        return pl.BlockSpec((tr, C), imap)

    st_spec = pl.BlockSpec((None, tr, C), lambda l, s, i, pr: (l, jnp.where(s == 0, pr[0], 1 - pr[0]) * nb + i, 0))

    def body(*refs):
        g_refs = refs[:2 * depth]
        w_ref, m_ref, v_ref, go_ref, d_ref, mo_ref, vo_ref = refs[2 * depth:]
        l, s = pl.program_id(0), pl.program_id(1)
        for kk in range(depth):
            for hh in range(2):
                @pl.when((l == kk) & (s == hh))
                def _(kk=kk, hh=hh):
                    g = g_refs[2 * kk + hh][...]
                    d, mn, vn = _adam_math(w_ref[...], g, m_ref[...], v_ref[...])
                    go_ref[...] = g
                    d_ref[...] = d
                    mo_ref[...] = mn
                    vo_ref[...] = vn

    gs, g_specs = [], []
    for kk in range(depth):
        gs += [mine[kk], other[kk]]
        g_specs += [g_spec(kk, 0), g_spec(kk, 1)]
    return _placed_call(body, name, (depth, 2, nb), g_specs + [st_spec] * 3, [st_spec] * 4, [_sds(w.shape, F32)] * 4,
                        ("arbitrary", "arbitrary", "arbitrary"), gs + [w, m, v])


def _piece_view(col, j, other):
    def view(ref, p):
        R, C = ref.shape
        cc = (1 - p["c"]) if other else p["c"]
        if col:
            hr, sc = R // 2, C // N_CHIPS
            return ref.at[pl.ds(pl.multiple_of(cc * hr, 16), hr), pl.ds(j * sc, sc)]
        hr = R // (2 * N_CHIPS)
        return ref.at[pl.ds(pl.multiple_of((2 * j + cc) * hr, 8), hr), :]
    return view


def _piece_shape(shape, col):
    R, C = shape
    return (R // 2, C // N_CHIPS) if col else (R // (2 * N_CHIPS), C)


def _reduce_scatter(tag, grads, col_sharded, wire_dtype):
    n = len(grads)
    shapes = [_piece_shape(g.shape, col) for g, col in zip(grads, col_sharded)]

    slot = lambda j: (lambda ref, p: ref.at[j])
    remote = [(a, _piece_view(col_sharded[a], j, True), a, slot(j), SIBLING) for a in range(n) for j in range(N_CHIPS)]
    local = [(a, _piece_view(col_sharded[a], j, False), n + a, slot(j)) for a in range(n) for j in range(N_CHIPS)]
    got = _exchange("rs_sibling_" + tag, grads, [_sds((N_CHIPS,) + s, g.dtype) for s, g in zip(shapes, grads)] * 2, remote, local)
    theirs, mine = got[:n], got[n:]
    chip_sum = [_ew("rs_add2_" + tag, lambda a, b: (a.astype(F32) + b.astype(F32),),
                    [m.reshape(-1, m.shape[-1]), t.reshape(-1, t.shape[-1])], [wire_dtype])[0].reshape(m.shape)
                for m, t in zip(mine, theirs)]

    def send_view(mask):
        return lambda ref, p: ref.at[2 * ((1 - p["x"]) if mask[0] else p["x"]) + ((1 - p["y"]) if mask[1] else p["y"])]
    remote = [(a, send_view(m), a, slot(k), m) for a in range(n) for k, m in enumerate(CHIP_MASKS)]
    local = [(a, lambda ref, p: ref.at[p["chip"]], n + a, _whole) for a in range(n)]
    got = _exchange("rs_chips_" + tag, chip_sum,
                    [_sds((len(CHIP_MASKS),) + s, wire_dtype) for s in shapes] + [_sds(s, wire_dtype) for s in shapes], remote, local)
    landed, own = got[:n], got[n:]
    half = [_ew("rs_add4_" + tag, lambda o, a, b, c: (((o.astype(F32) + a.astype(F32)) + b.astype(F32)) + c.astype(F32),),
                [o, l[0], l[1], l[2]], [F32])[0] for o, l in zip(own, landed)]

    def half_rows(ref, p):
        hr = ref.shape[0] // 2
        return ref.at[pl.ds(pl.multiple_of(p["c"] * hr, 8), hr), :]
    remote = [(a, _whole, a, half_rows, SIBLING) for a in range(n)]
    local = [(a, _whole, a, half_rows) for a in range(n)]
    return _exchange("rs_halves_" + tag, half, [_sds((2 * s[0], s[1]), F32) for s in shapes], remote, local)


def _ssm_prepare(W):
    lr, li, ls = W['ssm_lambda_re'], W['ssm_lambda_im'], W['ssm_log_step']
    depth, G = ls.shape
    GG = depth * G
    flat = lambda a: a.reshape(-1, LANES)
    bc = lambda a: flat(jnp.broadcast_to(a, (depth, G, STATE, GROUP_CH)))
    flat3 = (bc(lr[..., None]), bc(li[..., None]), bc(ls[:, :, None, None]))
    bbr, bbi = _ssm_bbar(*flat3, flat(W['ssm_b_re']), flat(W['ssm_b_im']))
    bbr, bbi = bbr.reshape(GG, STATE, GROUP_CH), bbi.reshape(GG, STATE, GROUP_CH)
    row = lambda a: a.reshape(1, GG * STATE)
    tf, tr = _ssm_tables(row(lr), row(li), row(jnp.broadcast_to(ls[..., None], (depth, G, STATE))))
    cr = W['ssm_c_re'].reshape(GG, GROUP_CH, STATE).transpose(0, 2, 1)
    ci = -W['ssm_c_im'].reshape(GG, GROUP_CH, STATE).transpose(0, 2, 1)
    stacked = dict(wb=(_tile_w(bbr), _tile_w(bbi)), wbT=(_slab_w(bbr), _slab_w(bbi)),
                   wc=(_slab_w(cr), _slab_w(ci)), wcT=(_tile_w(cr), _tile_w(ci)))
    return flat3, [dict(stacked, tf=tf, tr=tr, layer=l, depth=depth) for l in range(depth)]


def _ssm_param_grads(W, flat3, raw):
    depth, G = W['ssm_log_step'].shape
    GG = depth * G
    cat = lambda k: jnp.concatenate([r[k] for r in raw], axis=0)
    flat = lambda a: a.reshape(-1, LANES)
    out = {}
    out['ssm_c_re'] = _slab_w_grad(cat(2)).transpose(0, 2, 1).reshape(W['ssm_c_re'].shape)
    out['ssm_c_im'] = -_slab_w_grad(cat(3)).transpose(0, 2, 1).reshape(W['ssm_c_im'].shape)
    dbr, dbi, qr, qi = _ssm_param_bwd_flat(*flat3, flat(W['ssm_b_re']), flat(W['ssm_b_im']),
                                           flat(_tile_w_grad(cat(0))), flat(_tile_w_grad(cat(1))))
    out['ssm_b_re'], out['ssm_b_im'] = dbr.reshape(W['ssm_b_re'].shape), dbi.reshape(W['ssm_b_im'].shape)
    pick = lambda q: q[:, ::GROUP_CH].reshape(GG, STATE)
    sums = lambda k: jnp.concatenate([r[k].reshape(8, G, STATE) for r in raw], axis=1)
    dlr, dli, dls = _ssm_param_bwd(W['ssm_lambda_re'].reshape(GG, STATE), W['ssm_lambda_im'].reshape(GG, STATE),
                                   W['ssm_log_step'].reshape(GG, 1), pick(qr), pick(qi), sums(4), sums(5))
    out['ssm_lambda_re'], out['ssm_lambda_im'] = dlr.reshape(depth, G, STATE), dli.reshape(depth, G, STATE)
    out['ssm_log_step'] = dls.reshape(depth, G)
    return out


def _layer_fwd(x, p, weight, dims):
    attn_w, kv_w, u_off = dims['attn_w'], dims['kv_w'], dims['u_off']
    s = p['s5']
    h = _rms_fwd("norm_mix", [x], [p['norm_mix_g']], BF16)
    w = {'w_in': weight('w_in', h)}
    proj, = _mm("mm_in", h, w['w_in'], 'nn', [F32])
    attn = _attn_fwd(proj, p['q_norm_g'], p['k_norm_g'], p['attn_sinks'], attn_w, kv_w)
    bur, bui = _blockproj("ssm_bu", proj, u_off, *s['wb'], s['layer'], s['depth'])
    xr, xi = _scan("ssm_scan_fwd", bur, bui, s['tf'], s['layer'], False)
    y, gl = _ssm_out(xr, xi, *s['wc'], proj, u_off, p['ssm_d'], s['layer'], s['depth'])
    w['w_glu'] = weight('w_glu', gl)
    ssm, z = _mm("mm_glu", gl, w['w_glu'], 'nn', [F32, F32], extras=[('row', p['b_glu']), ('tile', gl)],
                 epi=lambda acc, b, g: ((lambda zz: (g * jax.nn.sigmoid(zz), zz))(acc + b)))
    mix = _rms_fwd("norm_heads", [attn, ssm], [p['attn_out_g'], p['ssm_out_g']], BF16)
    w['w_out'] = weight('w_out', mix)
    x_mid, = _mm("mm_out", mix, w['w_out'], 'nn', [F32], extras=[('tile', x)], epi=lambda acc, r: (acc + r,))
    h2 = _rms_fwd("norm_mlp", [x_mid], [p['norm_mlp_g']], BF16)
    w['w_up'] = weight('w_up', h2)
    a, r = _mm("mm_up", h2, w['w_up'], 'nn', [F32, BF16],
               epi=lambda acc: (acc, jnp.square(jnp.maximum(acc, 0.0))))
    w['w_down'] = weight('w_down', r)
    x_out, = _mm("mm_down", r, w['w_down'], 'nn', [F32], extras=[('tile', x_mid)], epi=lambda acc, rr: (acc + rr,))
    saved = dict(x=x, h=h, proj=proj, attn=attn, xr=xr, xi=xi, y=y, gl=gl, z=z, ssm=ssm, mix=mix, x_mid=x_mid, h2=h2, a=a, r=r, w=w)
    return x_out, saved


def _layer_bwd(dx, dx16, sv, p, dims, reduce_grads):
    attn_w, kv_w, u_off = dims['attn_w'], dims['kv_w'], dims['u_off']
    s, w = p['s5'], sv['w']
    gb, gs = {}, {}
    da, = _mm("mm_down_dx", dx16, w['w_down'], 'nt', [BF16], extras=[('tile', sv['a'])],
              epi=lambda acc, a: (acc * (2.0 * jnp.maximum(a, 0.0)),))
    gb['w_down'], = _mm("mm_down_dw", sv['r'], dx16, 'tn', [BF16])
    dh2, = _mm("mm_up_dx", da, w['w_up'], 'nt', [F32])
    gb['w_up'], = _mm("mm_up_dw", sv['h2'], da, 'tn', [BF16])
    token = reduce_grads(('w_up', 'w_down'), [gb['w_up'], gb['w_down']])
    (dx_mid,), (gs['norm_mlp_g'],), dx_mid16 = _rms_bwd("norm_mlp_bwd", [sv['x_mid']], [p['norm_mlp_g'] + token], dh2, resid=dx)
    dmix, = _mm("mm_out_dx", dx_mid16, w['w_out'], 'nt', [F32])
    gb['w_out'], = _mm("mm_out_dw", sv['mix'], dx_mid16, 'tn', [BF16])
    (dattn, dssm), (gs['attn_out_g'], gs['ssm_out_g']) = _rms_bwd(
        "norm_heads_bwd", [sv['attn'], sv['ssm']], [p['attn_out_g'], p['ssm_out_g']], dmix)
    dz, gs['b_glu'] = _glu_dz(dssm, sv['gl'], sv['z'])
    dy, = _mm("mm_glu_dx", dz, w['w_glu'], 'nt', [F32], extras=[('tile', dssm), ('tile', sv['z']), ('tile', sv['y'])],
              epi=lambda acc, ds, z, y: ((acc + ds * jax.nn.sigmoid(z)) * _gelu_grad(y),))
    gb['w_glu'], = _mm("mm_glu_dw", sv['gl'], dz, 'tn', [BF16])
    dxr, dxi = _blockproj("ssm_dstate", dy, 0, *s['wcT'], s['layer'], s['depth'])
    gxr, gxi, s1, s2 = _scan("ssm_scan_bwd", dxr, dxi, s['tr'], s['layer'], True, states=(sv['xr'], sv['xi']))
    dwb_r, dwb_i = _blockproj_grad("ssm_dbbar", sv['proj'], u_off, gxr, gxi)
    dwc_r, dwc_i = _ssm_dc(sv['xr'], sv['xi'], dy, s['wc'][0].shape[1])
    gs['s5_raw'] = (dwb_r, dwb_i, dwc_r, dwc_i, s1, s2)
    dproj, dkn, dv, gs['q_norm_g'], gs['attn_sinks'] = _attn_bwd(sv['proj'], sv['attn'], dattn, p['q_norm_g'], p['k_norm_g'],
                                                                  p['attn_sinks'], attn_w, kv_w)
    dproj, gs['k_norm_g'] = _knorm_bwd(sv['proj'], dkn, dv, p['k_norm_g'], dproj, attn_w, kv_w)
    dproj, gs['ssm_d'] = _ssm_du(gxr, gxi, *s['wbT'], dy, sv['proj'], u_off, p['ssm_d'], dproj, s['layer'], s['depth'])
    dh, = _mm("mm_in_dx", dproj, w['w_in'], 'nt', [F32])
    gb['w_in'], = _mm("mm_in_dw", sv['h'], dproj, 'tn', [BF16])
    token = reduce_grads(('w_in', 'w_glu', 'w_out'), [gb['w_in'], gb['w_glu'], gb['w_out']])
    (dx_in,), (gs['norm_mix_g'],), dx_in16 = _rms_bwd("norm_mix_bwd", [sv['x']], [p['norm_mix_g'] + token], dh, resid=dx_mid)
    return dx_in, dx_in16, gs


PACK_COLS = 1024


def _pack(arrs, rows):
    flat = jnp.concatenate([a.reshape(-1).astype(F32) for a in arrs])
    return jnp.pad(flat, (0, rows * PACK_COLS - flat.shape[0])).reshape(rows, PACK_COLS)


def _unpack(packed, shapes):
    flat = packed.reshape(-1)
    out, off = [], 0
    for s in shapes:
        n = int(np.prod(s))
        out.append(flat[off:off + n].reshape(s))
        off += n
    return out


def _pack_rows(shapes, multiple):
    n = sum(int(np.prod(s)) for s in shapes)
    rows = -(-n // PACK_COLS)
    return -(-rows // multiple) * multiple


def kernel(x, meta_tokens, norm_mix_g, w_in, q_norm_g, k_norm_g, attn_sinks, ssm_lambda_re, ssm_lambda_im, ssm_log_step, ssm_b_re, ssm_b_im, ssm_c_re, ssm_c_im, ssm_d, w_glu, b_glu, attn_out_g, ssm_out_g, w_out, norm_mlp_g, w_up, w_down, loss_target, m_meta_tokens, m_norm_mix_g, m_w_in, m_q_norm_g, m_k_norm_g, m_attn_sinks, m_ssm_lambda_re, m_ssm_lambda_im, m_ssm_log_step, m_ssm_b_re, m_ssm_b_im, m_ssm_c_re, m_ssm_c_im, m_ssm_d, m_w_glu, m_b_glu, m_attn_out_g, m_ssm_out_g, m_w_out, m_norm_mlp_g, m_w_up, m_w_down, v_meta_tokens, v_norm_mix_g, v_w_in, v_q_norm_g, v_k_norm_g, v_attn_sinks, v_ssm_lambda_re, v_ssm_lambda_im, v_ssm_log_step, v_ssm_b_re, v_ssm_b_im, v_ssm_c_re, v_ssm_c_im, v_ssm_d, v_w_glu, v_b_glu, v_attn_out_g, v_ssm_out_g, v_w_out, v_norm_mlp_g, v_w_up, v_w_down):
    args = (meta_tokens, norm_mix_g, w_in, q_norm_g, k_norm_g, attn_sinks, ssm_lambda_re, ssm_lambda_im, ssm_log_step, ssm_b_re, ssm_b_im, ssm_c_re, ssm_c_im, ssm_d, w_glu, b_glu, attn_out_g, ssm_out_g, w_out, norm_mlp_g, w_up, w_down)
    ms = (m_meta_tokens, m_norm_mix_g, m_w_in, m_q_norm_g, m_k_norm_g, m_attn_sinks, m_ssm_lambda_re, m_ssm_lambda_im, m_ssm_log_step, m_ssm_b_re, m_ssm_b_im, m_ssm_c_re, m_ssm_c_im, m_ssm_d, m_w_glu, m_b_glu, m_attn_out_g, m_ssm_out_g, m_w_out, m_norm_mlp_g, m_w_up, m_w_down)
    vs = (v_meta_tokens, v_norm_mix_g, v_w_in, v_q_norm_g, v_k_norm_g, v_attn_sinks, v_ssm_lambda_re, v_ssm_lambda_im, v_ssm_log_step, v_ssm_b_re, v_ssm_b_im, v_ssm_c_re, v_ssm_c_im, v_ssm_d, v_w_glu, v_b_glu, v_attn_out_g, v_ssm_out_g, v_w_out, v_norm_mlp_g, v_w_up, v_w_down)
    W = dict(zip(WEIGHTS, args))
    M = dict(zip(WEIGHTS, ms))
    V = dict(zip(WEIGHTS, vs))
    depth = norm_mix_g.shape[0]
    seq, D = x.shape[1], x.shape[2]
    attn_w = D // 2
    kv_w = attn_w // KV_GROUP
    dims = dict(attn_w=attn_w, kv_w=kv_w, u_off=(attn_w + 2 * kv_w) // LANES)
    small_names = [n for n in WEIGHTS if n not in BIG and n != 'meta_tokens']
    chip = 2 * lax.axis_index("x") + lax.axis_index("y")

    gathers, started = [], jnp.zeros((), F32)
    for l in range(depth):
        placed = [_ag_place("ag_place_" + n, W[n], l, COL_SHARDED[n], BF16) for n in BIG]
        groups = [_ag_copies(a, COL_SHARDED[n]) for a, n in enumerate(BIG)]
        if l == 0:
            placed = [_ag_place("ag_place_meta", meta_tokens[None], 0, True, F32)] + placed
            groups = [_ag_copies(0, True)] + [_ag_copies(a + 1, COL_SHARDED[n]) for a, n in enumerate(BIG)]
        sems, bufs, token = _split_start("ag_start_%d" % l, placed, groups)
        gathers.append(dict(zip((['meta_tokens'] if l == 0 else []) + BIG, zip(sems, bufs))))
        started = started + token[0, 0]

    def gathered(l, n, after):
        sems, buf = gathers[l][n]
        return _split_wait("ag_wait_%d_%s" % (l, n), [buf], sems, after, _ag_copies(0, n == 'meta_tokens' or COL_SHARDED[n]))[0]

    h_res = jnp.concatenate([jnp.zeros((PAD, D), F32), gathered(0, 'meta_tokens', started.reshape(1, 1)), x[0]], axis=0)
    s5_flat3, s5_layers = _ssm_prepare(W)
    layer_p = []
    for l in range(depth):
        p = {n: W[n][l][None, :] for n in ('norm_mix_g', 'q_norm_g', 'k_norm_g', 'attn_sinks', 'ssm_d', 'b_glu', 'attn_out_g',
                                             'ssm_out_g', 'norm_mlp_g')}
        p['s5'] = s5_layers[l]
        layer_p.append(p)
    saved = []
    for l in range(depth):
        h_res, sv = _layer_fwd(h_res, layer_p[l], functools.partial(gathered, l), dims)
        saved.append(sv)
    loss_local, dx, dx16 = _loss(h_res, loss_target[0])
    loss = lax.psum(loss_local, ("x", "y", "c"))

    small_grads = [None] * depth
    shard_grads = {}
    pending = []

    def finish(after):
        while pending:
            l_, names, st = pending.pop(0)
            mine, other = _rs_finish("%d_%s" % (l_, names[0]), st, after)
            for a, n in enumerate(names):
                shard_grads[(l_, n)] = (mine[a], other[a])

    def reduce_grads(l, names, grads):
        st = _rs_begin("%d_%s" % (l, names[0]), list(grads), [COL_SHARDED[n] for n in names])
        pending.append((l, names, st))
        return st['token'][0, 0]

    for l in reversed(range(depth)):
        dx, dx16, gs = _layer_bwd(dx, dx16, saved[l], layer_p[l], dims, functools.partial(reduce_grads, l))
        saved[l] = None
        small_grads[l] = gs
        newest = pending.pop()
        finish(dx)
        pending.append(newest)
    grad_x = dx[BLOCK:].reshape(x.shape)

    g_small = _ssm_param_grads(W, s5_flat3, [small_grads[l]['s5_raw'] for l in range(depth)])
    for n in small_names:
        if n not in g_small:
            g_small[n] = jnp.stack([small_grads[l][n].reshape(W[n].shape[1:]) for l in range(depth)])
    g_shapes = [(N_META, D)] + [W[n].shape for n in small_names]
    rows = _pack_rows(g_shapes, 8 * 2 * N_CHIPS)
    packed = _pack([dx[PAD:BLOCK]] + [g_small[n] for n in small_names], rows)
    red, = _reduce_scatter("small", [packed], [False], F32)
    red_full, = _all_gather("ag_small", [red], [False])
    finish(red_full)
    g_list = _unpack(red_full, g_shapes)
    g_meta = lax.dynamic_slice_in_dim(g_list[0], chip * meta_tokens.shape[1], meta_tokens.shape[1], axis=1)
    G = dict(zip(small_names, g_list[1:]))
    G['meta_tokens'] = g_meta

    out = {}
    for n in BIG:
        out[n] = _adamw_big("adamw_" + n, [shard_grads[(l, n)][0] for l in range(depth)],
                            [shard_grads[(l, n)][1] for l in range(depth)], W[n], M[n], V[n])
    for n in ['meta_tokens'] + small_names:
        rows2d = lambda a: a.reshape(-1, a.shape[-1])
        upd = _ew("adamw_" + n, _adam_math, [rows2d(W[n]), rows2d(G[n]), rows2d(M[n]), rows2d(V[n])], [F32] * 3)
        out[n] = (G[n], *[u.reshape(W[n].shape) for u in upd])
    return (loss, grad_x, *[out[n][0] for n in WEIGHTS], *[out[n][1] for n in WEIGHTS],
            *[out[n][2] for n in WEIGHTS], *[out[n][3] for n in WEIGHTS])
```

```python
import functools
import math

import numpy as np
import jax
import jax.numpy as jnp
from jax import lax
from jax.experimental import pallas as pl
from jax.experimental.pallas import tpu as pltpu

F32 = jnp.float32
BF16 = jnp.bfloat16
MESH = pl.DeviceIdType.MESH

N_META = 16
HEAD_DIM = 64
KV_GROUP = 4
GROUP_CH = 16
STATE = 64
BLOCK = 128
PAD = BLOCK - N_META
NORM_EPS = 1e-6
NEG_INF = -1e30
LANES = 128
V7X_VMEM_LIMIT_BYTES = 56 * 1024 * 1024
MM_VMEM_BUDGET_BYTES = 44 * 1024 * 1024

ADAM_LR, ADAM_B1, ADAM_B2, ADAM_EPS, ADAM_WD, ADAM_STEP = 0.001, 0.9, 0.999, 1e-08, 0.01, 10

WEIGHTS = ['meta_tokens', 'norm_mix_g', 'w_in', 'q_norm_g', 'k_norm_g', 'attn_sinks', 'ssm_lambda_re',
           'ssm_lambda_im', 'ssm_log_step', 'ssm_b_re', 'ssm_b_im', 'ssm_c_re', 'ssm_c_im', 'ssm_d', 'w_glu',
           'b_glu', 'attn_out_g', 'ssm_out_g', 'w_out', 'norm_mlp_g', 'w_up', 'w_down']
BIG = ['w_in', 'w_glu', 'w_out', 'w_up', 'w_down']
COL_SHARDED = {'w_in': True, 'w_glu': False, 'w_out': False, 'w_up': True, 'w_down': False}
N_CHIPS = 4


def _pick(n, cands):
    for c in cands:
        if c <= n and n % c == 0:
            return c
    return n


def _params(sem):
    return pltpu.CompilerParams(dimension_semantics=sem, vmem_limit_bytes=V7X_VMEM_LIMIT_BYTES)


def _pcall(body, **kw):
    return pl.pallas_call(body, **kw)


def _sds(shape, dtype):
    return jax.ShapeDtypeStruct(shape, dtype)


_DIMS = {'nn': ((1,), (0,)), 'nt': ((1,), (1,)), 'tn': ((0,), (0,))}


def _mm(name, a, b, mode, out_dtypes, extras=(), epi=None):
    if mode == 'nn':
        (M, K), (_, N) = a.shape, b.shape
    elif mode == 'nt':
        (M, K), (N, _) = a.shape, b.shape
    else:
        (K, M), (_, N) = a.shape, b.shape
    tile_bytes = 4 * len([k for k, _ in extras if k == 'tile']) + sum(jnp.dtype(d).itemsize for d in out_dtypes)

    def fits(tm, tn, tk):
        need = 2 * tm * tk * a.dtype.itemsize + 2 * tk * tn * b.dtype.itemsize + 4 * tm * tn + 2 * tm * tn * tile_bytes
        return need <= MM_VMEM_BUDGET_BYTES

    if mode == 'tn':
        tm, tk_cands = _pick(M, (1024, 512, 256, 128)), (1408, 704, 384, 128)
    else:
        tm, tk_cands = _pick(M, (1408, 704, 384, 128)), (2048, 1024, 512, 256, 128)
    tk_cands = [t for t in tk_cands if t <= K and K % t == 0] or [K]
    tn_cands = [t for t in (2048, 1280, 1024, 640, 512, 256, 128) if t <= N and N % t == 0] or [N]
    if mode != 'tn' and tk_cands[0] == K and a.dtype == BF16:
        tk_cands = tk_cands[:1]
    tn, tk = next(((tn_, tk_) for tn_ in tn_cands for tk_ in tk_cands if fits(tm, tn_, tk_)), (tn_cands[-1], tk_cands[-1]))
    nk = K // tk
    a_spec = pl.BlockSpec((tk, tm), lambda i, j, k: (k, i)) if mode == 'tn' else pl.BlockSpec((tm, tk), lambda i, j, k: (i, k))
    b_spec = pl.BlockSpec((tn, tk), lambda i, j, k: (j, k)) if mode == 'nt' else pl.BlockSpec((tk, tn), lambda i, j, k: (k, j))
    ex_specs = [pl.BlockSpec((tm, tn), lambda i, j, k: (i, j)) if kind == 'tile' else pl.BlockSpec((1, tn), lambda i, j, k: (0, j))
                for kind, _ in extras]
    ne, no = len(extras), len(out_dtypes)
    dims = (_DIMS[mode], ((), ()))

    def body(a_ref, b_ref, *rest):
        ex, outs, acc = rest[:ne], rest[ne:ne + no], rest[ne + no]
        k = pl.program_id(2)

        @pl.when(k == 0)
        def _():
            acc[...] = jnp.zeros_like(acc)

        acc[...] += lax.dot_general(a_ref[...].astype(BF16), b_ref[...].astype(BF16), dims, preferred_element_type=F32)

        @pl.when(k == nk - 1)
        def _():
            r = acc[...]
            res = epi(r, *[e[...] for e in ex]) if epi is not None else (r,)
            for o, v in zip(outs, res):
                o[...] = v.astype(o.dtype)

    outs = _pcall(
        body, name=name, grid=(M // tm, N // tn, nk),
        in_specs=[a_spec, b_spec] + ex_specs,
        out_specs=[pl.BlockSpec((tm, tn), lambda i, j, k: (i, j)) for _ in out_dtypes],
        out_shape=[_sds((M, N), d) for d in out_dtypes],
        scratch_shapes=[pltpu.VMEM((tm, tn), F32)],
        compiler_params=_params(("parallel", "parallel", "arbitrary")),
    )(a, b, *[e for _, e in extras])
    return outs


def _ew(name, fn, ins, out_dtypes):
    R, C = ins[0].shape
    tr = _pick(R, tuple(t for t in (1024, 512, 256, 128, 64, 32, 16, 8) if t * C <= 512 * 1024) or (8,))
    n_in = len(ins)

    def body(*refs):
        res = fn(*[r[...] for r in refs[:n_in]])
        for o, v in zip(refs[n_in:], res):
            o[...] = v.astype(o.dtype)

    spec = pl.BlockSpec((tr, C), lambda i: (i, 0))
    return _pcall(body, name=name, grid=(R // tr,), in_specs=[spec] * n_in, out_specs=[spec] * len(out_dtypes),
                  out_shape=[_sds((R, C), d) for d in out_dtypes], compiler_params=_params(("parallel",)))(*ins)


def _adam_math(w, g, m, v):
    m = ADAM_B1 * m + (1.0 - ADAM_B1) * g
    v = ADAM_B2 * v + (1.0 - ADAM_B2) * (g * g)
    m_hat = m / (1.0 - ADAM_B1 ** ADAM_STEP)
    v_hat = v / (1.0 - ADAM_B2 ** ADAM_STEP)
    delta = -ADAM_LR * (m_hat / (jnp.sqrt(v_hat) + ADAM_EPS) + ADAM_WD * w)
    return delta, m, v


def _rms_fwd(name, xs, gs, out_dtype):
    L = xs[0].shape[0]
    ws = [x.shape[1] for x in xs]
    n = len(xs)
    tr = _pick(L, (384, 256, 128))

    def body(*refs):
        o = refs[2 * n]
        off = 0
        for i in range(n):
            x = refs[i][...]
            r = lax.rsqrt(jnp.mean(x * x, axis=-1, keepdims=True) + NORM_EPS)
            o[:, off:off + ws[i]] = ((x * r) * refs[n + i][...]).astype(o.dtype)
            off += ws[i]

    return _pcall(body, name=name, grid=(L // tr,),
                  in_specs=[pl.BlockSpec((tr, w), lambda i: (i, 0)) for w in ws] + [pl.BlockSpec((1, w), lambda i: (0, 0)) for w in ws],
                  out_specs=pl.BlockSpec((tr, sum(ws)), lambda i: (i, 0)), out_shape=_sds((L, sum(ws)), out_dtype),
                  compiler_params=_params(("parallel",)))(*xs, *gs)


def _rms_bwd(name, xs, gs, dy, resid=None):
    L = xs[0].shape[0]
    ws = [x.shape[1] for x in xs]
    n = len(xs)
    tr = _pick(L, (384, 256, 128))
    has_res = resid is not None

    def body(*refs):
        x_refs, g_refs, dy_ref = refs[:n], refs[n:2 * n], refs[2 * n]
        p = 2 * n + 1
        res_ref = refs[p] if has_res else None
        p += 1 if has_res else 0
        dx_refs, dg_refs = refs[p:p + n], refs[p + n:p + 2 * n]
        dx16_ref = refs[p + 2 * n] if has_res else None
        first = pl.program_id(0) == 0
        off = 0
        for i in range(n):
            x = x_refs[i][...]
            d = dy_ref[:, off:off + ws[i]]
            r = lax.rsqrt(jnp.mean(x * x, axis=-1, keepdims=True) + NORM_EPS)
            xh = x * r
            dg = jnp.sum(d * xh, axis=0, keepdims=True)

            @pl.when(first)
            def _(i=i):
                dg_refs[i][...] = jnp.zeros_like(dg_refs[i])

            dg_refs[i][...] += dg
            dyg = d * g_refs[i][...]
            dx = r * (dyg - xh * jnp.mean(dyg * xh, axis=-1, keepdims=True))
            if has_res:
                dx = dx + res_ref[...]
                dx16_ref[...] = dx.astype(BF16)
            dx_refs[i][...] = dx
            off += ws[i]

    in_specs = ([pl.BlockSpec((tr, w), lambda i: (i, 0)) for w in ws] + [pl.BlockSpec((1, w), lambda i: (0, 0)) for w in ws]
                + [pl.BlockSpec((tr, sum(ws)), lambda i: (i, 0))])
    ins = list(xs) + list(gs) + [dy]
    if has_res:
        in_specs.append(pl.BlockSpec((tr, ws[0]), lambda i: (i, 0)))
        ins.append(resid)
    out_specs = [pl.BlockSpec((tr, w), lambda i: (i, 0)) for w in ws] + [pl.BlockSpec((1, w), lambda i: (0, 0)) for w in ws]
    out_shape = [_sds((L, w), F32) for w in ws] + [_sds((1, w), F32) for w in ws]
    if has_res:
        out_specs.append(pl.BlockSpec((tr, ws[0]), lambda i: (i, 0)))
        out_shape.append(_sds((L, ws[0]), BF16))
    outs = _pcall(body, name=name, grid=(L // tr,), in_specs=in_specs, out_specs=out_specs, out_shape=out_shape,
                  compiler_params=_params(("arbitrary",)))(*ins)
    return (outs[:n], outs[n:2 * n], outs[2 * n]) if has_res else (outs[:n], outs[n:])


def _loss(xl, target):
    Lp, D = xl.shape

    def body(x_ref, t_ref, dy_ref, dy16_ref, loss_ref):
        n = pl.program_id(0)

        @pl.when(n == 0)
        def _():
            loss_ref[...] = jnp.zeros_like(loss_ref)
            dy_ref[...] = jnp.zeros_like(dy_ref)
            dy16_ref[...] = jnp.zeros_like(dy16_ref)

        @pl.when(n > 0)
        def _():
            err = x_ref[...] - t_ref[...]
            dy = err * (1.0 / D)
            dy_ref[...] = dy
            dy16_ref[...] = dy.astype(BF16)
            loss_ref[...] += jnp.sum(err * err) * (0.5 / D)

    blk = pl.BlockSpec((BLOCK, D), lambda n: (n, 0))
    dy, dy16, loss = _pcall(body, name="loss_head", grid=(Lp // BLOCK,),
                            in_specs=[blk, pl.BlockSpec((BLOCK, D), lambda n: (jnp.maximum(n - 1, 0), 0))],
                            out_specs=[blk, blk, pl.BlockSpec((8, LANES), lambda n: (0, 0))],
                            out_shape=[_sds((Lp, D), F32), _sds((Lp, D), BF16), _sds((8, LANES), F32)],
                            compiler_params=_params(("arbitrary",)))(xl, target)
    return loss[0, 0], dy, dy16


GROUP_ROWS = KV_GROUP * BLOCK


def _attn_mask_dist(n):
    r = lax.broadcasted_iota(jnp.int32, (GROUP_ROWS, 3 * BLOCK), 0)
    i = r & (BLOCK - 1)
    j = lax.broadcasted_iota(jnp.int32, (GROUP_ROWS, 3 * BLOCK), 1)
    in_band = j < 2 * BLOCK
    band = in_band & (j > i) & (j <= i + BLOCK) & (j >= 2 * BLOCK - BLOCK * n)
    jm = j - 2 * BLOCK
    meta = (~in_band) & (jm >= PAD) & (jm <= BLOCK * n + i)
    dist = jnp.where(in_band, BLOCK + i - j, BLOCK * n + i - jm).astype(F32)
    return band | meta, dist


def _head_norm(x, g):
    r = lax.rsqrt(jnp.mean(x * x, axis=-1, keepdims=True) + NORM_EPS)
    return (x * r) * g, r


def _attn_specs(attn_w, kv_w):
    kb = attn_w // kv_w
    q_spec = pl.BlockSpec((BLOCK, attn_w), lambda n: (n, 0))

    def kv(col):
        return [pl.BlockSpec((BLOCK, kv_w), lambda n: (jnp.maximum(n - 1, 0), col)),
                pl.BlockSpec((BLOCK, kv_w), lambda n: (n, col)),
                pl.BlockSpec((BLOCK, kv_w), lambda n: (0, col))]

    return q_spec, kv(kb), kv(kb + 1)


def _slopes(n_heads):
    return [2.0 ** (-8.0 * (h + 1) / n_heads) for h in range(n_heads)]


def _head_slice(h):
    return slice(h * HEAD_DIM, (h + 1) * HEAD_DIM)


def _stack_heads(ref, kh):
    return jnp.concatenate([ref[:, _head_slice(kh * KV_GROUP + g)] for g in range(KV_GROUP)], axis=0)


def _group_column(vals):
    return jnp.concatenate([jnp.broadcast_to(v, (BLOCK, 1)) for v in vals], axis=0)


def _group_inputs(kh, slopes, q_ref, kp, kc, km, vp, vc, vm, gq_ref, gk_ref, sk_ref):
    cs = _head_slice(kh)
    kn, _ = _head_norm(jnp.concatenate([kp[:, cs], kc[:, cs], km[:, cs]], axis=0), gk_ref[...])
    vcat = jnp.concatenate([vp[:, cs], vc[:, cs], vm[:, cs]], axis=0).astype(BF16)
    q = _stack_heads(q_ref, kh)
    qn, rq = _head_norm(q, gq_ref[...])
    heads = range(kh * KV_GROUP, (kh + 1) * KV_GROUP)
    slope = _group_column([jnp.full((1, 1), slopes[h], F32) for h in heads])
    sink = _group_column([sk_ref[0:1, h:h + 1] for h in heads])
    return q, qn, rq, kn, vcat, slope, sink


def _scores(qn, kn, slope, sink, mask, dist):
    s = lax.dot_general(qn.astype(BF16), kn.astype(BF16), (((1,), (1,)), ((), ())), preferred_element_type=F32)
    s = s * (1.0 / math.sqrt(HEAD_DIM)) - slope * dist
    s = jnp.where(mask, s, NEG_INF)
    m = jnp.maximum(jnp.max(s, axis=-1, keepdims=True), sink)
    p = jnp.exp(s - m)
    ps = jnp.exp(sink - m)
    inv = 1.0 / (jnp.sum(p, axis=-1, keepdims=True) + ps)
    return p * inv, ps * inv


def _attn_fwd(proj, gq, gk, sinks, attn_w, kv_w):
    Lp = proj.shape[0]
    n_heads, n_kv = attn_w // HEAD_DIM, kv_w // HEAD_DIM
    slopes = _slopes(n_heads)
    q_spec, k_specs, v_specs = _attn_specs(attn_w, kv_w)

    def body(q_ref, kp, kc, km, vp, vc, vm, gq_ref, gk_ref, sk_ref, o_ref):
        mask, dist = _attn_mask_dist(pl.program_id(0))
        for kh in range(n_kv):
            _, qn, _, kn, vcat, slope, sink = _group_inputs(kh, slopes, q_ref, kp, kc, km, vp, vc, vm, gq_ref, gk_ref, sk_ref)
            p, _ = _scores(qn, kn, slope, sink, mask, dist)
            o = jnp.dot(p.astype(BF16), vcat, preferred_element_type=F32)
            for g in range(KV_GROUP):
                o_ref[:, _head_slice(kh * KV_GROUP + g)] = o[g * BLOCK:(g + 1) * BLOCK]

    small = lambda w: pl.BlockSpec((1, w), lambda n: (0, 0))
    return _pcall(body, name="attn_fwd", grid=(Lp // BLOCK,),
                  in_specs=[q_spec] + k_specs + v_specs + [small(HEAD_DIM), small(HEAD_DIM), small(n_heads)],
                  out_specs=pl.BlockSpec((BLOCK, attn_w), lambda n: (n, 0)), out_shape=_sds((Lp, attn_w), F32),
                  compiler_params=_params(("parallel",)))(proj, proj, proj, proj, proj, proj, proj, gq, gk, sinks)


def _attn_bwd(proj, attn, dattn, gq, gk, sinks, attn_w, kv_w):
    Lp = proj.shape[0]
    n_heads, n_kv = attn_w // HEAD_DIM, kv_w // HEAD_DIM
    slopes = _slopes(n_heads)
    q_spec, k_specs, v_specs = _attn_specs(attn_w, kv_w)
    scale = 1.0 / math.sqrt(HEAD_DIM)
    tn_dims = (((0,), (0,)), ((), ()))

    def body(q_ref, kp, kc, km, vp, vc, vm, o_ref, do_ref, gq_ref, gk_ref, sk_ref, dq_ref, dk_ref, dv_ref, dgq_ref, dsk_ref):
        n = pl.program_id(0)

        @pl.when(n == 0)
        def _():
            dk_ref[...] = jnp.zeros_like(dk_ref)
            dv_ref[...] = jnp.zeros_like(dv_ref)
            dgq_ref[...] = jnp.zeros_like(dgq_ref)
            dsk_ref[...] = jnp.zeros_like(dsk_ref)

        mask, dist = _attn_mask_dist(n)
        lane = lax.broadcasted_iota(jnp.int32, (1, n_heads), 1)
        rows_prev = pl.ds(pl.multiple_of(jnp.maximum(n - 1, 0) * BLOCK, BLOCK), BLOCK)
        rows_cur = pl.ds(pl.multiple_of(n * BLOCK, BLOCK), BLOCK)
        rows_meta = pl.ds(0, BLOCK)
        dgq = jnp.zeros((1, HEAD_DIM), F32)
        dsk = jnp.zeros((1, n_heads), F32)
        for kh in range(n_kv):
            cs = _head_slice(kh)
            q, qn, rq, kn, vcat, slope, sink = _group_inputs(kh, slopes, q_ref, kp, kc, km, vp, vc, vm, gq_ref, gk_ref, sk_ref)
            p, ps = _scores(qn, kn, slope, sink, mask, dist)
            do = _stack_heads(do_ref, kh)
            dd = jnp.sum(do * _stack_heads(o_ref, kh), axis=-1, keepdims=True)
            do16 = do.astype(BF16)
            dp = lax.dot_general(do16, vcat, (((1,), (1,)), ((), ())), preferred_element_type=F32)
            ds16 = (p * (dp - dd)).astype(BF16)
            dsink = -ps * dd
            for g in range(KV_GROUP):
                dsk = dsk + jnp.where(lane == kh * KV_GROUP + g, jnp.sum(dsink[g * BLOCK:(g + 1) * BLOCK]), 0.0)
            dqn = jnp.dot(ds16, kn.astype(BF16), preferred_element_type=F32) * scale
            dkn = lax.dot_general(ds16, qn.astype(BF16), tn_dims, preferred_element_type=F32) * scale
            dvc = lax.dot_general(p.astype(BF16), do16, tn_dims, preferred_element_type=F32)
            xh = q * rq
            dgq = dgq + jnp.sum(dqn * xh, axis=0, keepdims=True)
            dyg = dqn * gq_ref[...]
            dq = rq * (dyg - xh * jnp.mean(dyg * xh, axis=-1, keepdims=True))
            for g in range(KV_GROUP):
                dq_ref[:, _head_slice(kh * KV_GROUP + g)] = dq[g * BLOCK:(g + 1) * BLOCK].astype(dq_ref.dtype)
            for part, rows in enumerate((rows_prev, rows_cur, rows_meta)):
                ps_ = slice(part * BLOCK, (part + 1) * BLOCK)
                dk_ref[rows, cs] += dkn[ps_]
                dv_ref[rows, cs] += dvc[ps_]
        dgq_ref[...] += dgq
        dsk_ref[...] += dsk

    small = lambda w: pl.BlockSpec((1, w), lambda n: (0, 0))
    blk = pl.BlockSpec((BLOCK, attn_w), lambda n: (n, 0))
    whole = pl.BlockSpec((Lp, kv_w), lambda n: (0, 0))
    return _pcall(body, name="attn_bwd", grid=(Lp // BLOCK,),
                  in_specs=[q_spec] + k_specs + v_specs + [blk, blk, small(HEAD_DIM), small(HEAD_DIM), small(n_heads)],
                  out_specs=[blk, whole, whole, small(HEAD_DIM), small(n_heads)],
                  out_shape=[_sds(proj.shape, BF16), _sds((Lp, kv_w), F32), _sds((Lp, kv_w), F32),
                             _sds((1, HEAD_DIM), F32), _sds((1, n_heads), F32)],
                  compiler_params=_params(("arbitrary",)))(proj, proj, proj, proj, proj, proj, proj, attn, dattn, gq, gk, sinks)


def _knorm_bwd(proj, dkn, dv, gk, dproj, attn_w, kv_w):
    Lp = proj.shape[0]
    n_kv = kv_w // HEAD_DIM
    tr = _pick(Lp, (384, 256, 128))

    def body(k_ref, d_ref, dv_ref, g_ref, buf_ref, out_ref, dg_ref):
        @pl.when(pl.program_id(0) == 0)
        def _():
            dg_ref[...] = jnp.zeros_like(dg_ref)

        dg = jnp.zeros((1, HEAD_DIM), F32)
        for kh in range(n_kv):
            cs = slice(kh * HEAD_DIM, (kh + 1) * HEAD_DIM)
            x = k_ref[:, cs]
            d = d_ref[:, cs]
            r = lax.rsqrt(jnp.mean(x * x, axis=-1, keepdims=True) + NORM_EPS)
            xh = x * r
            dg = dg + jnp.sum(d * xh, axis=0, keepdims=True)
            dyg = d * g_ref[...]
            out_ref[:, cs] = (r * (dyg - xh * jnp.mean(dyg * xh, axis=-1, keepdims=True))).astype(out_ref.dtype)
        out_ref[:, kv_w:] = dv_ref[...].astype(out_ref.dtype)
        dg_ref[...] += dg

    kv_blk = pl.BlockSpec((tr, kv_w), lambda i: (i, 0))
    return _pcall(body, name="knorm_bwd", grid=(Lp // tr,),
                  in_specs=[pl.BlockSpec((tr, kv_w), lambda i: (i, attn_w // kv_w)), kv_blk, kv_blk,
                            pl.BlockSpec((1, HEAD_DIM), lambda i: (0, 0)), pl.BlockSpec(memory_space=pl.ANY)],
                  out_specs=[pl.BlockSpec((tr, 2 * kv_w), lambda i: (i, attn_w // (2 * kv_w))), pl.BlockSpec((1, HEAD_DIM), lambda i: (0, 0))],
                  out_shape=[_sds(dproj.shape, dproj.dtype), _sds((1, HEAD_DIM), F32)],
                  input_output_aliases={4: 0},
                  compiler_params=_params(("arbitrary",)))(proj, dkn, dv, gk, dproj)


def _ssm_bbar(lr, li, ls, br, bi):
    def fn(lr, li, ls, br, bi):
        fr, fi = _zoh_factor(lr, li, ls)
        return fr * br - fi * bi, fr * bi + fi * br

    return _ew("ssm_bbar", fn, [lr, li, ls, br, bi], [F32, F32])


def _lam_bar(lr, li, ls):
    dl = jnp.exp(ls)
    e = jnp.exp(lr * dl)
    return e * jnp.cos(li * dl), e * jnp.sin(li * dl), dl


def _zoh_factor(lr, li, ls):
    ar, ai, _ = _lam_bar(lr, li, ls)
    n2 = lr * lr + li * li
    ivr, ivi = lr / n2, -li / n2
    return (ar - 1.0) * ivr - ai * ivi, (ar - 1.0) * ivi + ai * ivr


SCAN_SHIFTS = (1, 2, 4)


def _ssm_tables(lr, li, ls):
    Wx = lr.shape[1]

    def body(lr_ref, li_ref, ls_ref, tf_ref, tr_ref):
        dl = jnp.exp(ls_ref[...])
        zr, zi = lr_ref[...] * dl, li_ref[...] * dl
        row = lax.broadcasted_iota(jnp.int32, (8, Wx), 0)

        def power(kf):
            e = jnp.exp(kf * zr)
            return e * jnp.cos(kf * zi), e * jnp.sin(kf * zi)

        for ref, rev in ((tf_ref, False), (tr_ref, True)):
            sgn = -1.0 if rev else 1.0
            for k, d in enumerate(SCAN_SHIFTS):
                ar, ai = power(jnp.full((8, Wx), float(d), F32))
                keep = (row < 8 - d) if rev else (row >= d)
                ref[k] = jnp.where(keep, ar, 0.0)
                ref[4 + k] = jnp.where(keep, sgn * ai, 0.0)
            pr, pi = power(((8 - row) if rev else (row + 1)).astype(F32))
            ref[3] = pr
            ref[7] = sgn * pi

    full = pl.BlockSpec((1, Wx), lambda: (0, 0))
    tab = pl.BlockSpec((8, 8, Wx), lambda: (0, 0, 0))
    return _pcall(body, name="ssm_tables", in_specs=[full] * 3, out_specs=[tab, tab],
                  out_shape=[_sds((8, 8, Wx), F32)] * 2,
                  compiler_params=pltpu.CompilerParams(vmem_limit_bytes=V7X_VMEM_LIMIT_BYTES))(lr, li, ls)


def _scan(name, br, bi, tab, layer, reverse, states=None):
    L, Wx = br.shape
    TB = _pick(L, (384, 256, 128))
    CW = _pick(Wx, (1024, 512, 256, 128))
    nT, nG = L // TB, TB // 8

    def body(*refs):
        if reverse:
            br_ref, bi_ref, xr_ref, xi_ref, tab_ref, or_ref, oi_ref, s1_ref, s2_ref, cr_ref, ci_ref = refs
        else:
            br_ref, bi_ref, tab_ref, or_ref, oi_ref, cr_ref, ci_ref = refs

        @pl.when(pl.program_id(1) == 0)
        def _():
            cr_ref[...] = jnp.zeros_like(cr_ref)
            ci_ref[...] = jnp.zeros_like(ci_ref)
            if reverse:
                s1_ref[...] = jnp.zeros_like(s1_ref)
                s2_ref[...] = jnp.zeros_like(s2_ref)

        def step(q, carry):
            cr, ci = carry[0], carry[1]
            g = (nG - 1 - q) if reverse else q
            rows = pl.ds(pl.multiple_of(g * 8, 8), 8)
            b_r, b_i = br_ref[rows, :], bi_ref[rows, :]
            sr, si = b_r, b_i
            for k, d in enumerate(SCAN_SHIFTS):
                mr, mi = tab_ref[k], tab_ref[4 + k]
                sh = (8 - d) if reverse else d
                pr, pi = pltpu.roll(sr, sh, 0), pltpu.roll(si, sh, 0)
                sr, si = sr + mr * pr - mi * pi, si + mr * pi + mi * pr
            pwr, pwi = tab_ref[3], tab_ref[7]
            xr = sr + pwr * cr - pwi * ci
            xi = si + pwr * ci + pwi * cr
            or_ref[rows, :] = xr
            oi_ref[rows, :] = xi
            row = 0 if reverse else 7
            out = (jnp.broadcast_to(xr[row:row + 1, :], xr.shape), jnp.broadcast_to(xi[row:row + 1, :], xi.shape))
            if reverse:
                hr, hi = xr - b_r, xi - b_i
                st_r, st_i = xr_ref[rows, :], xi_ref[rows, :]
                out = out + (carry[2] + hr * st_r + hi * st_i, carry[3] + hi * st_r - hr * st_i)
            return out

        init = (cr_ref[...], ci_ref[...])
        if reverse:
            init = init + (jnp.zeros((8, CW), F32), jnp.zeros((8, CW), F32))
        fin = lax.fori_loop(0, nG, step, init, unroll=2)
        cr_ref[...] = fin[0]
        ci_ref[...] = fin[1]
        if reverse:
            s1_ref[...] += fin[2]
            s2_ref[...] += fin[3]

    tmap = (lambda j, t: (nT - 1 - t, j)) if reverse else (lambda j, t: (t, j))
    blk = pl.BlockSpec((TB, CW), tmap)
    tab_spec = pl.BlockSpec((8, 8, CW), lambda j, t: (0, 0, layer * (Wx // CW) + j))
    sum_spec = pl.BlockSpec((8, CW), lambda j, t: (0, j))
    ins = [br, bi] + (list(states) if reverse else []) + [tab]
    in_specs = [blk, blk] + ([blk, blk] if reverse else []) + [tab_spec]
    out_specs = [blk, blk] + ([sum_spec, sum_spec] if reverse else [])
    out_shape = [_sds((L, Wx), F32)] * 2 + ([_sds((8, Wx), F32)] * 2 if reverse else [])
    return _pcall(body, name=name, grid=(Wx // CW, nT), in_specs=in_specs, out_specs=out_specs, out_shape=out_shape,
                  scratch_shapes=[pltpu.VMEM((8, CW), F32), pltpu.VMEM((8, CW), F32)],
                  compiler_params=_params(("parallel", "arbitrary")))(*ins)


def _row_tile(L):
    return _pick(L, (1408, 704, 384, 128))


TILES_PER_BLOCK = 4


def _blockproj(name, src, off, w_r, w_i, layer, depth):
    L = src.shape[0]
    T = w_r.shape[0] // depth
    tm = _row_tile(L)
    wide = TILES_PER_BLOCK * LANES

    def body(s_ref, wr_ref, wi_ref, or_ref, oi_ref):
        s = s_ref[...].astype(BF16)
        for k in range(TILES_PER_BLOCK):
            cols = slice(k * LANES, (k + 1) * LANES)
            or_ref[:, cols] = jnp.dot(s, wr_ref[k], preferred_element_type=F32)
            oi_ref[:, cols] = jnp.dot(s, wi_ref[k], preferred_element_type=F32)

    w_spec = pl.BlockSpec((TILES_PER_BLOCK, LANES, LANES), lambda i, q: (layer * (T // TILES_PER_BLOCK) + q, 0, 0))
    o_spec = pl.BlockSpec((tm, wide), lambda i, q: (i, q))
    return _pcall(body, name=name, grid=(L // tm, T // TILES_PER_BLOCK),
                  in_specs=[pl.BlockSpec((tm, LANES), lambda i, q: (i, off + q)), w_spec, w_spec],
                  out_specs=[o_spec, o_spec], out_shape=[_sds((L, T * LANES), F32)] * 2,
                  compiler_params=_params(("parallel", "arbitrary")))(src, w_r, w_i)


def _blockproj_grad(name, src, off, gr, gi):
    L = src.shape[0]
    T = gr.shape[1] // LANES
    tm = _row_tile(L)
    wide = TILES_PER_BLOCK * LANES
    tn_dims = (((0,), (0,)), ((), ()))

    def body(s_ref, gr_ref, gi_ref, or_ref, oi_ref):
        @pl.when(pl.program_id(1) == 0)
        def _():
            or_ref[...] = jnp.zeros_like(or_ref)
            oi_ref[...] = jnp.zeros_like(oi_ref)

        s = s_ref[...].astype(BF16)
        for k in range(TILES_PER_BLOCK):
            cols = slice(k * LANES, (k + 1) * LANES)
            or_ref[k] += lax.dot_general(s, gr_ref[:, cols].astype(BF16), tn_dims, preferred_element_type=F32)
            oi_ref[k] += lax.dot_general(s, gi_ref[:, cols].astype(BF16), tn_dims, preferred_element_type=F32)

    g_spec = pl.BlockSpec((tm, wide), lambda q, i: (i, q))
    o_spec = pl.BlockSpec((TILES_PER_BLOCK, LANES, LANES), lambda q, i: (q, 0, 0))
    return _pcall(body, name=name, grid=(T // TILES_PER_BLOCK, L // tm),
                  in_specs=[pl.BlockSpec((tm, LANES), lambda q, i: (i, off + q)), g_spec, g_spec],
                  out_specs=[o_spec, o_spec], out_shape=[_sds((T, LANES, LANES), F32)] * 2,
                  compiler_params=_params(("parallel", "arbitrary")))(src, gr, gi)


def _gelu(y):
    k = math.sqrt(2.0 / math.pi)
    return 0.5 * y * (1.0 + jnp.tanh(k * (y + 0.044715 * (y * y * y))))


def _gelu_grad(y):
    k = math.sqrt(2.0 / math.pi)
    t = jnp.tanh(k * (y + 0.044715 * (y * y * y)))
    return 0.5 * (1.0 + t) + 0.5 * y * (1.0 - t * t) * (k * (1.0 + 3 * 0.044715 * (y * y)))


def _ssm_out(xr, xi, w_r, w_i, proj, u_off, dvec, layer, depth):
    L = xr.shape[0]
    J = w_r.shape[0] // depth
    SW = w_r.shape[1]
    tm = _row_tile(L)

    def body(xr_ref, xi_ref, wr_ref, wi_ref, u_ref, d_ref, y_ref, gl_ref):
        acc = jnp.dot(xr_ref[...].astype(BF16), wr_ref[...], preferred_element_type=F32)
        acc += jnp.dot(xi_ref[...].astype(BF16), wi_ref[...], preferred_element_type=F32)
        y = acc + d_ref[...] * u_ref[...]
        y_ref[...] = y
        gl_ref[...] = _gelu(y)

    x_spec = pl.BlockSpec((tm, SW), lambda j, i: (i, j))
    w_spec = pl.BlockSpec((None, SW, LANES), lambda j, i: (layer * J + j, 0, 0))
    o_spec = pl.BlockSpec((tm, LANES), lambda j, i: (i, j))
    return _pcall(body, name="ssm_out", grid=(J, L // tm),
                  in_specs=[x_spec, x_spec, w_spec, w_spec, pl.BlockSpec((tm, LANES), lambda j, i: (i, u_off + j)),
                            pl.BlockSpec((1, LANES), lambda j, i: (0, j))],
                  out_specs=[o_spec, o_spec], out_shape=[_sds((L, J * LANES), F32)] * 2,
                  compiler_params=_params(("parallel", "parallel")))(xr, xi, w_r, w_i, proj, dvec)


def _ssm_du(gr, gi, w_r, w_i, dy, proj, u_off, dvec, dproj, layer, depth):
    L = gr.shape[0]
    J = w_r.shape[0] // depth
    SW = w_r.shape[1]
    tm = _row_tile(L)

    def body(gr_ref, gi_ref, wr_ref, wi_ref, dy_ref, u_ref, d_ref, buf_ref, du_ref, dd_ref):
        i = pl.program_id(1)

        @pl.when(i == 0)
        def _():
            dd_ref[...] = jnp.zeros_like(dd_ref)

        acc = jnp.dot(gr_ref[...].astype(BF16), wr_ref[...], preferred_element_type=F32)
        acc += jnp.dot(gi_ref[...].astype(BF16), wi_ref[...], preferred_element_type=F32)
        dy = dy_ref[...]
        row = lax.broadcasted_iota(jnp.int32, (tm, LANES), 0) + i * tm
        du_ref[...] = jnp.where(row >= PAD, acc + d_ref[...] * dy, 0.0).astype(du_ref.dtype)
        dd_ref[...] += jnp.sum(dy * u_ref[...], axis=0, keepdims=True)

    x_spec = pl.BlockSpec((tm, SW), lambda j, i: (i, j))
    w_spec = pl.BlockSpec((None, SW, LANES), lambda j, i: (layer * J + j, 0, 0))
    o_spec = pl.BlockSpec((tm, LANES), lambda j, i: (i, j))
    u_spec = pl.BlockSpec((tm, LANES), lambda j, i: (i, u_off + j))
    vec = pl.BlockSpec((1, LANES), lambda j, i: (0, j))
    return _pcall(body, name="ssm_du", grid=(J, L // tm),
                  in_specs=[x_spec, x_spec, w_spec, w_spec, o_spec, u_spec, vec, pl.BlockSpec(memory_space=pl.ANY)],
                  out_specs=[u_spec, vec], out_shape=[_sds(dproj.shape, dproj.dtype), _sds((1, J * LANES), F32)],
                  input_output_aliases={7: 0},
                  compiler_params=_params(("parallel", "arbitrary")))(gr, gi, w_r, w_i, dy, proj, dvec, dproj)


def _ssm_dc(xr, xi, dy, SW):
    L = xr.shape[0]
    J = dy.shape[1] // LANES
    tm = _row_tile(L)
    tn_dims = (((0,), (0,)), ((), ()))

    def body(xr_ref, xi_ref, dy_ref, or_ref, oi_ref):
        @pl.when(pl.program_id(1) == 0)
        def _():
            or_ref[...] = jnp.zeros_like(or_ref)
            oi_ref[...] = jnp.zeros_like(oi_ref)

        d = dy_ref[...].astype(BF16)
        or_ref[...] += lax.dot_general(xr_ref[...].astype(BF16), d, tn_dims, preferred_element_type=F32)
        oi_ref[...] += lax.dot_general(xi_ref[...].astype(BF16), d, tn_dims, preferred_element_type=F32)

    x_spec = pl.BlockSpec((tm, SW), lambda j, i: (i, j))
    o_spec = pl.BlockSpec((None, SW, LANES), lambda j, i: (j, 0, 0))
    return _pcall(body, name="ssm_dc", grid=(J, L // tm),
                  in_specs=[x_spec, x_spec, pl.BlockSpec((tm, LANES), lambda j, i: (i, j))],
                  out_specs=[o_spec, o_spec], out_shape=[_sds((J, SW, LANES), F32)] * 2,
                  compiler_params=_params(("parallel", "arbitrary")))(xr, xi, dy)


SCAN_COLS = 1024
SCAN_TILES = SCAN_COLS // LANES
SCAN_CH = SCAN_COLS // STATE * GROUP_CH
SCAN_SLAB = 8 * STATE


def _scan_rows(b_r, b_i, tab_ref, carry, reverse):
    sr, si = b_r, b_i
    for k, d in enumerate(SCAN_SHIFTS):
        mr, mi = tab_ref[k], tab_ref[4 + k]
        sh = (8 - d) if reverse else d
        pr, pi = pltpu.roll(sr, sh, 0), pltpu.roll(si, sh, 0)
        sr, si = sr + mr * pr - mi * pi, si + mr * pi + mi * pr
    pwr, pwi = tab_ref[3], tab_ref[7]
    xr = sr + pwr * carry[0] - pwi * carry[1]
    xi = si + pwr * carry[1] + pwi * carry[0]
    row = 0 if reverse else 7
    return xr, xi, (jnp.broadcast_to(xr[row:row + 1, :], xr.shape), jnp.broadcast_to(xi[row:row + 1, :], xi.shape))


def _ssm_fwd(proj, u_off, s, dvec):
    L = proj.shape[0]
    layer, depth = s['layer'], s['depth']
    T, J = s['wb'][0].shape[0] // depth, s['wc'][0].shape[0] // depth
    Wx, nC = T * LANES, T * LANES // SCAN_COLS
    TB = _pick(L, (384, 256, 128))
    nT, nG = L // TB, TB // 8

    def body(u_ref, wbr_ref, wbi_ref, wcr_ref, wci_ref, d_ref, tab_ref, xr_ref, xi_ref, y_ref, gl_ref, cr_ref, ci_ref):
        @pl.when(pl.program_id(1) == 0)
        def _():
            cr_ref[...] = jnp.zeros_like(cr_ref)
            ci_ref[...] = jnp.zeros_like(ci_ref)

        u = u_ref[...]
        u16 = u.astype(BF16)
        for k in range(SCAN_TILES):
            blk = u16[:, (k // TILES_PER_BLOCK) * LANES:(k // TILES_PER_BLOCK + 1) * LANES]
            cols = slice(k * LANES, (k + 1) * LANES)
            xr_ref[:, cols] = jnp.dot(blk, wbr_ref[k], preferred_element_type=F32)
            xi_ref[:, cols] = jnp.dot(blk, wbi_ref[k], preferred_element_type=F32)

        def step(q, carry):
            rows = pl.ds(pl.multiple_of(q * 8, 8), 8)
            xr, xi, carry = _scan_rows(xr_ref[rows, :], xi_ref[rows, :], tab_ref, carry, False)
            xr_ref[rows, :] = xr
            xi_ref[rows, :] = xi
            return carry

        cr, ci = lax.fori_loop(0, nG, step, (cr_ref[...], ci_ref[...]), unroll=2)
        cr_ref[...] = cr
        ci_ref[...] = ci
        for c in range(SCAN_CH // LANES):
            slab, ch = slice(c * SCAN_SLAB, (c + 1) * SCAN_SLAB), slice(c * LANES, (c + 1) * LANES)
            acc = jnp.dot(xr_ref[:, slab].astype(BF16), wcr_ref[c], preferred_element_type=F32)
            acc += jnp.dot(xi_ref[:, slab].astype(BF16), wci_ref[c], preferred_element_type=F32)
            y = acc + d_ref[:, ch] * u[:, ch]
            y_ref[:, ch] = y
            gl_ref[:, ch] = _gelu(y)

    x_spec = pl.BlockSpec((TB, SCAN_COLS), lambda j, t: (t, j))
    y_spec = pl.BlockSpec((TB, SCAN_CH), lambda j, t: (t, j))
    tile_w = pl.BlockSpec((SCAN_TILES, LANES, LANES), lambda j, t: (layer * nC + j, 0, 0))
    slab_w = pl.BlockSpec((SCAN_CH // LANES, SCAN_SLAB, LANES), lambda j, t: (layer * nC + j, 0, 0))
    return _pcall(body, name="ssm_fwd", grid=(nC, nT),
                  in_specs=[pl.BlockSpec((TB, SCAN_CH), lambda j, t: (t, u_off * LANES // SCAN_CH + j)), tile_w, tile_w, slab_w, slab_w,
                            pl.BlockSpec((1, SCAN_CH), lambda j, t: (0, j)),
                            pl.BlockSpec((8, 8, SCAN_COLS), lambda j, t: (0, 0, layer * nC + j))],
                  out_specs=[x_spec, x_spec, y_spec, y_spec],
                  out_shape=[_sds((L, Wx), F32)] * 2 + [_sds((L, J * LANES), F32)] * 2,
                  scratch_shapes=[pltpu.VMEM((8, SCAN_COLS), F32)] * 2,
                  compiler_params=_params(("parallel", "arbitrary")))(proj, *s['wb'], *s['wc'], dvec, s['tf'])


def _ssm_bwd(dy, proj, u_off, xr, xi, s, dvec, dproj):
    L = dy.shape[0]
    layer, depth = s['layer'], s['depth']
    T, J = s['wb'][0].shape[0] // depth, s['wc'][0].shape[0] // depth
    Wx, nC = T * LANES, T * LANES // SCAN_COLS
    TB = _pick(L, (384, 256, 128))
    nT, nG = L // TB, TB // 8
    n_ch = SCAN_CH // LANES
    tn_dims = (((0,), (0,)), ((), ()))

    def body(dy_ref, u_ref, xr_ref, xi_ref, wcr_ref, wci_ref, wbr_ref, wbi_ref, d_ref, tab_ref, buf_ref,
             du_ref, dd_ref, dwbr_ref, dwbi_ref, dwcr_ref, dwci_ref, s1_ref, s2_ref, gr_ref, gi_ref, cr_ref, ci_ref):
        t = pl.program_id(1)

        @pl.when(t == 0)
        def _():
            for ref in (cr_ref, ci_ref, dd_ref, dwbr_ref, dwbi_ref, dwcr_ref, dwci_ref, s1_ref, s2_ref):
                ref[...] = jnp.zeros_like(ref)

        dy = dy_ref[...]
        dy16 = dy.astype(BF16)
        u = u_ref[...]
        u16 = u.astype(BF16)
        for k in range(SCAN_TILES):
            blk = dy16[:, (k // TILES_PER_BLOCK) * LANES:(k // TILES_PER_BLOCK + 1) * LANES]
            cols = slice(k * LANES, (k + 1) * LANES)
            gr_ref[:, cols] = jnp.dot(blk, wcr_ref[k], preferred_element_type=F32)
            gi_ref[:, cols] = jnp.dot(blk, wci_ref[k], preferred_element_type=F32)

        def step(q, carry):
            rows = pl.ds(pl.multiple_of((nG - 1 - q) * 8, 8), 8)
            b_r, b_i = gr_ref[rows, :], gi_ref[rows, :]
            g_r, g_i, edge = _scan_rows(b_r, b_i, tab_ref, carry[:2], True)
            gr_ref[rows, :] = g_r
            gi_ref[rows, :] = g_i
            hr, hi = g_r - b_r, g_i - b_i
            st_r, st_i = xr_ref[rows, :], xi_ref[rows, :]
            return edge + (carry[2] + hr * st_r + hi * st_i, carry[3] + hi * st_r - hr * st_i)

        zero = jnp.zeros((8, SCAN_COLS), F32)
        fin = lax.fori_loop(0, nG, step, (cr_ref[...], ci_ref[...], zero, zero), unroll=2)
        cr_ref[...] = fin[0]
        ci_ref[...] = fin[1]
        s1_ref[...] += fin[2]
        s2_ref[...] += fin[3]

        row = lax.broadcasted_iota(jnp.int32, (TB, LANES), 0) + (nT - 1 - t) * TB
        for c in range(n_ch):
            slab, ch = slice(c * SCAN_SLAB, (c + 1) * SCAN_SLAB), slice(c * LANES, (c + 1) * LANES)
            g16r, g16i = gr_ref[:, slab].astype(BF16), gi_ref[:, slab].astype(BF16)
            acc = jnp.dot(g16r, wbr_ref[c], preferred_element_type=F32) + jnp.dot(g16i, wbi_ref[c], preferred_element_type=F32)
            du_ref[:, ch] = jnp.where(row >= PAD, acc + d_ref[:, ch] * dy[:, ch], 0.0).astype(du_ref.dtype)
            dwcr_ref[c] += lax.dot_general(xr_ref[:, slab].astype(BF16), dy16[:, ch], tn_dims, preferred_element_type=F32)
            dwci_ref[c] += lax.dot_general(xi_ref[:, slab].astype(BF16), dy16[:, ch], tn_dims, preferred_element_type=F32)
            for kk in range(TILES_PER_BLOCK):
                k = c * TILES_PER_BLOCK + kk
                cols = slice(kk * LANES, (kk + 1) * LANES)
                dwbr_ref[k] += lax.dot_general(u16[:, ch], g16r[:, cols], tn_dims, preferred_element_type=F32)
                dwbi_ref[k] += lax.dot_general(u16[:, ch], g16i[:, cols], tn_dims, preferred_element_type=F32)
        dd_ref[...] += jnp.sum(dy * u, axis=0, keepdims=True)

    rev = lambda j, t: (nT - 1 - t, j)
    x_spec = pl.BlockSpec((TB, SCAN_COLS), rev)
    u_spec = pl.BlockSpec((TB, SCAN_CH), lambda j, t: (nT - 1 - t, u_off * LANES // SCAN_CH + j))
    tile_w = pl.BlockSpec((SCAN_TILES, LANES, LANES), lambda j, t: (layer * nC + j, 0, 0))
    slab_w = pl.BlockSpec((n_ch, SCAN_SLAB, LANES), lambda j, t: (layer * nC + j, 0, 0))
    tile_g = pl.BlockSpec((SCAN_TILES, LANES, LANES), lambda j, t: (j, 0, 0))
    slab_g = pl.BlockSpec((n_ch, SCAN_SLAB, LANES), lambda j, t: (j, 0, 0))
    vec = pl.BlockSpec((1, SCAN_CH), lambda j, t: (0, j))
    sums = pl.BlockSpec((8, SCAN_COLS), lambda j, t: (0, j))
    return _pcall(body, name="ssm_bwd", grid=(nC, nT),
                  in_specs=[pl.BlockSpec((TB, SCAN_CH), rev), u_spec, x_spec, x_spec, tile_w, tile_w, slab_w, slab_w, vec,
                            pl.BlockSpec((8, 8, SCAN_COLS), lambda j, t: (0, 0, layer * nC + j)), pl.BlockSpec(memory_space=pl.ANY)],
                  out_specs=[u_spec, vec, tile_g, tile_g, slab_g, slab_g, sums, sums],
                  out_shape=[_sds(dproj.shape, dproj.dtype), _sds((1, J * LANES), F32), _sds((T, LANES, LANES), F32),
                             _sds((T, LANES, LANES), F32), _sds((J, SCAN_SLAB, LANES), F32), _sds((J, SCAN_SLAB, LANES), F32),
                             _sds((8, Wx), F32), _sds((8, Wx), F32)],
                  input_output_aliases={10: 0},
                  scratch_shapes=[pltpu.VMEM((TB, SCAN_COLS), F32)] * 2 + [pltpu.VMEM((8, SCAN_COLS), F32)] * 2,
                  compiler_params=_params(("parallel", "arbitrary")))(dy, proj, xr, xi, *s['wcT'], *s['wbT'], dvec, s['tr'], dproj)


def _glu_dz(ds, gl, z):
    L, W = ds.shape
    tr = _pick(L, (384, 256, 128))

    def body(ds_ref, gl_ref, z_ref, dz_ref, db_ref):
        @pl.when(pl.program_id(0) == 0)
        def _():
            db_ref[...] = jnp.zeros_like(db_ref)

        sg = jax.nn.sigmoid(z_ref[...])
        dz = ds_ref[...] * gl_ref[...] * (sg * (1.0 - sg))
        dz_ref[...] = dz.astype(BF16)
        db_ref[...] += jnp.sum(dz, axis=0, keepdims=True)

    spec = pl.BlockSpec((tr, W), lambda i: (i, 0))
    vec = pl.BlockSpec((1, W), lambda i: (0, 0))
    return _pcall(body, name="glu_dz", grid=(L // tr,), in_specs=[spec] * 3, out_specs=[spec, vec],
                  out_shape=[_sds((L, W), BF16), _sds((1, W), F32)], compiler_params=_params(("arbitrary",)))(ds, gl, z)


def _ssm_param_bwd_flat(lr, li, ls, br, bi, dbbr, dbbi):
    def seg_sum(x):
        for s in (8, 4, 2, 1):
            x = x + pltpu.roll(x, LANES - s, 1)
        return x

    def fn(lr, li, ls, br, bi, dbbr, dbbi):
        fr, fi = _zoh_factor(lr, li, ls)
        return (fr * dbbr + fi * dbbi, fr * dbbi - fi * dbbr,
                seg_sum(br * dbbr + bi * dbbi), seg_sum(br * dbbi - bi * dbbr))

    return _ew("ssm_param_bwd_flat", fn, [lr, li, ls, br, bi, dbbr, dbbi], [F32] * 4)


def _ssm_param_bwd(lr, li, ls, dfr, dfi, s1, s2):
    G, P = lr.shape

    def body(lr_ref, li_ref, ls_ref, dfr_ref, dfi_ref, s1_ref, s2_ref, dlr_ref, dli_ref, dls_ref):
        lr, li = lr_ref[...], li_ref[...]
        ar, ai, dl = _lam_bar(lr, li, ls_ref[...])
        sr, si = s1_ref[0], s2_ref[0]
        for k in range(1, 8):
            sr = sr + s1_ref[k]
            si = si + s2_ref[k]
        a2 = ar * ar + ai * ai
        gar, gai = (sr * ar - si * ai) / a2, (sr * ai + si * ar) / a2
        n2 = lr * lr + li * li
        ivr, ivi = lr / n2, -li / n2
        fr = (ar - 1.0) * ivr - ai * ivi
        fi = (ar - 1.0) * ivi + ai * ivr
        dfr, dfi = dfr_ref[...], dfi_ref[...]
        gar = gar + ivr * dfr + ivi * dfi
        gai = gai + ivr * dfi - ivi * dfr
        wr, wi = -(fr * ivr - fi * ivi), -(fr * ivi + fi * ivr)
        glr, gli = wr * dfr + wi * dfi, wr * dfi - wi * dfr
        gzr, gzi = ar * gar + ai * gai, ar * gai - ai * gar
        dlr_ref[...] = glr + dl * gzr
        dli_ref[...] = gli + dl * gzi
        dls_ref[...] = dl * jnp.sum(lr * gzr + li * gzi, axis=-1, keepdims=True)

    m = pl.BlockSpec((G, P), lambda: (0, 0))
    v = pl.BlockSpec((G, 1), lambda: (0, 0))
    s = pl.BlockSpec((8, G, P), lambda: (0, 0, 0))
    return _pcall(body, name="ssm_param_bwd", in_specs=[m, m, v, m, m, s, s], out_specs=[m, m, v],
                  out_shape=[_sds((G, P), F32), _sds((G, P), F32), _sds((G, 1), F32)])(lr, li, ls, dfr, dfi, s1, s2)


def _tile_mask(G):
    T = G // 2
    e = np.zeros((T, 8, 1, 2, 1), np.float32)
    for t in range(T):
        for c in range(2):
            e[t, (2 * t + c) % 8, 0, c, 0] = 1.0
    return e


def _tile_w(arr):
    G = arr.shape[0]
    a = arr.reshape(G // 2, 1, 2, STATE, GROUP_CH).transpose(0, 1, 4, 2, 3)
    return (a * _tile_mask(G)).reshape(G // 2, LANES, LANES).astype(BF16)


def _tile_w_grad(dw):
    G = dw.shape[0] * 2
    d = dw.reshape(G // 2, 8, GROUP_CH, 2, STATE) * _tile_mask(G)
    return d.sum(axis=1).transpose(0, 2, 3, 1).reshape(G, STATE, GROUP_CH)


def _slab_w(arr):
    G = arr.shape[0]
    a = arr.reshape(G // 8, 8, STATE, 1, GROUP_CH)
    eye = np.eye(8, dtype=np.float32).reshape(1, 8, 1, 8, 1)
    return (a * eye).reshape(G // 8, 8 * STATE, LANES).astype(BF16)


def _slab_w_grad(dw):
    J = dw.shape[0]
    eye = np.eye(8, dtype=np.float32).reshape(1, 8, 1, 8, 1)
    return (dw.reshape(J, 8, STATE, 8, GROUP_CH) * eye).sum(axis=3).reshape(J * 8, STATE, GROUP_CH)


def _exchange(name, ins, out_sds, remote, local, aliases=None):
    n_in, n_out, n_r, n_l = len(ins), len(out_sds), len(remote), len(local)

    def body(*refs):
        in_refs, out_refs = refs[:n_in], refs[n_in:n_in + n_out]
        send_sems, recv_sems, local_sems = refs[n_in + n_out:]
        x, y, c = lax.axis_index("x"), lax.axis_index("y"), lax.axis_index("c")

        def place(px, py, pc):
            return dict(x=px, y=py, c=pc, chip=2 * px + py)

        def flip(mask):
            mx, my, mc = mask
            return ((1 - x) if mx else x, (1 - y) if my else y, (1 - c) if mc else c)

        me = place(x, y, c)
        sends = []
        for k, (ii, src, oi, dst, mask) in enumerate(remote):
            cp = pltpu.make_async_remote_copy(src_ref=src(in_refs[ii], me), dst_ref=dst(out_refs[oi], me),
                                              send_sem=send_sems.at[k], recv_sem=recv_sems.at[k],
                                              device_id=flip(mask), device_id_type=MESH)
            cp.start()
            sends.append(cp)
        locals_ = []
        for k, (ii, src, oi, dst) in enumerate(local):
            cp = pltpu.make_async_copy(src(in_refs[ii], me), dst(out_refs[oi], me), local_sems.at[k])
            cp.start()
            locals_.append(cp)
        for k, (ii, src, oi, dst, mask) in enumerate(remote):
            sends[k].wait_send()
            peer = flip(mask)
            pltpu.make_async_remote_copy(src_ref=src(in_refs[ii], me), dst_ref=dst(out_refs[oi], place(*peer)),
                                         send_sem=send_sems.at[k], recv_sem=recv_sems.at[k],
                                         device_id=peer, device_id_type=MESH).wait_recv()
        for cp in locals_:
            cp.wait()

    any_spec = pl.BlockSpec(memory_space=pl.ANY)
    return _pcall(body, name=name, in_specs=[any_spec] * n_in, out_specs=[any_spec] * n_out, out_shape=list(out_sds),
                  input_output_aliases=aliases or {},
                  scratch_shapes=[pltpu.SemaphoreType.DMA((n_r,)), pltpu.SemaphoreType.DMA((n_r,)),
                                  pltpu.SemaphoreType.DMA((max(n_l, 1),))])(*ins)


def _mesh_place():
    x, y, c = lax.axis_index("x"), lax.axis_index("y"), lax.axis_index("c")

    def place(px, py, pc):
        return dict(x=px, y=py, c=pc, chip=2 * px + py)

    def flip(mask):
        mx, my, mc = mask
        return ((1 - x) if mx else x, (1 - y) if my else y, (1 - c) if mc else c)

    return place(x, y, c), place, flip


_HBM = pl.BlockSpec(memory_space=pltpu.HBM)
_SEM = pl.BlockSpec(memory_space=pltpu.SEMAPHORE)
_EFFECT = pltpu.SideEffectType.DATAFLOW_SIDE_EFFECTING


def _split_start(name, bufs, groups):
    n, ng = len(bufs), len(groups)

    def body(*refs):
        in_refs, sems, token = refs[:n], refs[n:n + 2 * ng], refs[-1]
        me, _, flip = _mesh_place()
        for g, copies in enumerate(groups):
            for k, (si, src, di, dst, mask) in enumerate(copies):
                pltpu.make_async_remote_copy(src_ref=src(in_refs[si], me), dst_ref=dst(in_refs[di], me),
                                             send_sem=sems[2 * g].at[k], recv_sem=sems[2 * g + 1].at[k],
                                             device_id=flip(mask), device_id_type=MESH).start()
        token[...] = jnp.zeros_like(token)

    outs = _pcall(body, name=name,
                  out_shape=(*[pltpu.SemaphoreType.DMA((len(g),)) for g in groups for _ in range(2)],
                             *[pltpu.HBM(b.shape, b.dtype) for b in bufs], _sds((8, LANES), F32)),
                  in_specs=[_HBM] * n, out_specs=(*[_SEM] * (2 * ng), *[_HBM] * n, pl.BlockSpec(memory_space=pltpu.VMEM)),
                  input_output_aliases={i: 2 * ng + i for i in range(n)},
                  compiler_params=pltpu.CompilerParams(has_side_effects=_EFFECT),
                  )(*[pltpu.with_memory_space_constraint(b, pltpu.HBM) for b in bufs])
    return [(outs[2 * g], outs[2 * g + 1]) for g in range(ng)], list(outs[2 * ng:2 * ng + n]), outs[-1]


def _split_wait(name, bufs, sems, after, remote):
    n = len(bufs)
    send_sems, recv_sems = sems

    def body(*refs):
        in_refs, ssem, rsem = refs[:n], refs[n], refs[n + 1]
        me, place, flip = _mesh_place()
        for k, (si, src, di, dst, mask) in enumerate(remote):
            peer = flip(mask)
            cp = pltpu.make_async_remote_copy(src_ref=src(in_refs[si], me), dst_ref=dst(in_refs[di], place(*peer)),
                                              send_sem=ssem.at[k], recv_sem=rsem.at[k], device_id=peer, device_id_type=MESH)
            cp.wait_send()
            cp.wait_recv()

    return list(_pcall(body, name=name, out_shape=tuple(pltpu.HBM(b.shape, b.dtype) for b in bufs),
                       in_specs=[_HBM] * n + [_SEM, _SEM, pl.BlockSpec(memory_space=pl.ANY)], out_specs=tuple([_HBM] * n),
                       input_output_aliases={i: i for i in range(n)},
                       compiler_params=pltpu.CompilerParams(has_side_effects=_EFFECT))(*bufs, send_sems, recv_sems, after))


CHIP_MASKS = ((0, 1, 0), (1, 0, 0), (1, 1, 0))
SIBLING = (0, 0, 1)


def _whole(ref, p):
    return ref


def _all_gather(name, shards, col_sharded):
    def dst_view(col):
        def view(ref, p):
            r, cdim = ref.shape[0] // (1 if col else N_CHIPS), ref.shape[1] // (N_CHIPS if col else 1)
            if col:
                return ref.at[:, pl.ds(pl.multiple_of(p["chip"] * cdim, LANES), cdim)]
            return ref.at[pl.ds(pl.multiple_of(p["chip"] * r, 8), r), :]
        return view

    out_sds = [_sds((s.shape[0], s.shape[1] * N_CHIPS) if col else (s.shape[0] * N_CHIPS, s.shape[1]), s.dtype)
               for s, col in zip(shards, col_sharded)]
    remote = [(a, _whole, a, dst_view(col), m) for a, col in enumerate(col_sharded) for m in CHIP_MASKS]
    local = [(a, _whole, a, dst_view(col)) for a, col in enumerate(col_sharded)]
    return _exchange(name, shards, out_sds, remote, local)


class _Place:
    def __getitem__(self, k):
        return lax.axis_index("c") if k == 0 else 2 * lax.axis_index("x") + lax.axis_index("y")


def _placed_call(body, name, grid, in_specs, out_specs, out_shape, sem, ins):
    def wrap(spec):
        return pl.BlockSpec(spec.block_shape, lambda *idx: spec.index_map(*idx, _Place()))

    outs = [wrap(s) for s in out_specs] if isinstance(out_specs, (list, tuple)) else wrap(out_specs)
    return _pcall(body, name=name, grid=grid, in_specs=[wrap(s) for s in in_specs], out_specs=outs, out_shape=out_shape,
                  compiler_params=_params(sem))(*ins)


def _rows_within(n, width, limit=512 * 1024):
    return _pick(n, tuple(t for t in (1024, 512, 256, 128, 64, 32, 16) if t * width <= limit) or (16,))


def _region_view(col):
    def view(ref, p):
        if col:
            cdim = ref.shape[1] // N_CHIPS
            return ref.at[:, pl.ds(pl.multiple_of(p["chip"] * cdim, LANES), cdim)]
        r = ref.shape[0] // N_CHIPS
        return ref.at[pl.ds(pl.multiple_of(p["chip"] * r, 16), r), :]
    return view


def _ag_place(name, w, layer, col, dtype):
    _, r, cdim = w.shape
    tr = _rows_within(r, cdim)
    nb = r // tr

    def body(w_ref, o_ref):
        o_ref[...] = w_ref[...].astype(dtype)

    if col:
        out_shape, out_spec = (r, N_CHIPS * cdim), pl.BlockSpec((tr, cdim), lambda i, pr: (i, pr[1]))
    else:
        out_shape, out_spec = (N_CHIPS * r, cdim), pl.BlockSpec((tr, cdim), lambda i, pr: (pr[1] * nb + i, 0))
    return _placed_call(body, name, (nb,), [pl.BlockSpec((None, tr, cdim), lambda i, pr: (layer, i, 0))], out_spec,
                        _sds(out_shape, dtype), ("parallel",), [w])


def _ag_copies(a, col):
    return [(a, _region_view(col), a, _region_view(col), m) for m in CHIP_MASKS]


def _rs_add2(name, g4, a4, out_dtype):
    J, _, h, C = g4.shape
    tr = _rows_within(h, C)

    def body(g_ref, a_ref, o_ref):
        o_ref[...] = (g_ref[...].astype(F32) + a_ref[...].astype(F32)).astype(o_ref.dtype)

    return _placed_call(body, name, (J, h // tr),
                        [pl.BlockSpec((None, None, tr, C), lambda j, i, pr: (j, pr[0], i, 0)),
                         pl.BlockSpec((None, None, tr, C), lambda j, i, pr: (j, 0, i, 0))],
                        pl.BlockSpec((None, tr, C), lambda j, i, pr: (j, i, 0)), _sds((J, h, C), out_dtype),
                        ("parallel", "parallel"), [g4, a4])


def _rs_add4(name, p3, landed, col):
    _, h, w = landed.shape
    tr = _rows_within(h, w)

    def body(p_ref, a_ref, b_ref, c_ref, o_ref):
        o_ref[...] = ((p_ref[...].astype(F32) + a_ref[...].astype(F32)) + b_ref[...].astype(F32)) + c_ref[...].astype(F32)

    own = (pl.BlockSpec((None, tr, w), lambda i, pr: (0, i, pr[1])) if col else pl.BlockSpec((None, tr, w), lambda i, pr: (pr[1], i, 0)))
    slot = lambda k: pl.BlockSpec((None, tr, w), lambda i, pr: (k, i, 0))
    return _placed_call(body, name, (h // tr,), [own, slot(0), slot(1), slot(2)], pl.BlockSpec((tr, w), lambda i, pr: (i, 0)),
                        _sds((h, w), F32), ("parallel",), [p3, landed, landed, landed])


def _rs_begin(tag, grads, col_sharded):
    n = len(grads)
    g4 = [g.reshape((1, 2, g.shape[0] // 2, g.shape[1]) if col else (N_CHIPS, 2, g.shape[0] // (2 * N_CHIPS), g.shape[1]))
          for g, col in zip(grads, col_sharded)]
    other_half = lambda ref, p: ref.at[:, pl.ds(1 - p["c"], 1)]
    theirs = _exchange("rs_sibling_w", g4, [_sds((g.shape[0], 1) + g.shape[2:], g.dtype) for g in g4],
                       [(a, other_half, a, _whole, SIBLING) for a in range(n)], [])
    chip_sum = [_rs_add2("rs_add2_w", g4[a], theirs[a], BF16) for a in range(n)]

    def send_view(col, mask):
        def view(ref, p):
            t = 2 * ((1 - p["x"]) if mask[0] else p["x"]) + ((1 - p["y"]) if mask[1] else p["y"])
            if col:
                sc = ref.shape[2] // N_CHIPS
                return ref.at[0, :, pl.ds(pl.multiple_of(t * sc, LANES), sc)]
            return ref.at[t]
        return view
    slot = lambda k: (lambda ref, p: ref.at[k])
    piece = [(s.shape[1], s.shape[2] // N_CHIPS if col else s.shape[2]) for s, col in zip(chip_sum, col_sharded)]
    landing = [lax.empty((len(CHIP_MASKS),) + s, BF16) for s in piece]
    copies = [(a, send_view(col_sharded[a], m), n + a, slot(k), m) for a in range(n) for k, m in enumerate(CHIP_MASKS)]
    (sems,), bufs, token = _split_start("rs_chips_start_" + tag, chip_sum + landing, [copies])
    return dict(sems=sems, bufs=bufs, copies=copies, token=token, col_sharded=col_sharded)


def _rs_finish(tag, st, after):
    col_sharded = st['col_sharded']
    n = len(col_sharded)
    bufs = _split_wait("rs_chips_wait_" + tag, st['bufs'], st['sems'], after, st['copies'])
    chip_sum, landed = bufs[:n], bufs[n:]
    mine = [_rs_add4("rs_add4_w", chip_sum[a], landed[a], col_sharded[a]) for a in range(n)]
    other = _exchange("rs_halves_w", mine, [_sds(m.shape, F32) for m in mine], [(a, _whole, a, _whole, SIBLING) for a in range(n)], [])
    return mine, other


def _adamw_big(name, mine, other, w, m, v):
    depth, R, C = w.shape
    h = R // 2
    tr = _pick(h, tuple(t for t in (512, 256, 128, 64, 32, 16, 8) if t * C <= 256 * 1024) or (8,))
    nb = h // tr

    def g_spec(kk, hh):
        def imap(l, s, i, pr):
            before = (l < kk) | ((l == kk) & (s < hh))
            return (jnp.where((l == kk) & (s == hh), i, jnp.where(before, 0, nb - 1)), 0)
        return pl.BlockSpec((tr, C), imap)

    st_spec = pl.BlockSpec((None, tr, C), lambda l, s, i, pr: (l, jnp.where(s == 0, pr[0], 1 - pr[0]) * nb + i, 0))

    def body(*refs):
        g_refs = refs[:2 * depth]
        w_ref, m_ref, v_ref, go_ref, d_ref, mo_ref, vo_ref = refs[2 * depth:]
        l, s = pl.program_id(0), pl.program_id(1)
        for kk in range(depth):
            for hh in range(2):
                @pl.when((l == kk) & (s == hh))
                def _(kk=kk, hh=hh):
                    g = g_refs[2 * kk + hh][...]
                    d, mn, vn = _adam_math(w_ref[...], g, m_ref[...], v_ref[...])
                    go_ref[...] = g
                    d_ref[...] = d
                    mo_ref[...] = mn
                    vo_ref[...] = vn

    gs, g_specs = [], []
    for kk in range(depth):
        gs += [mine[kk], other[kk]]
        g_specs += [g_spec(kk, 0), g_spec(kk, 1)]
    return _placed_call(body, name, (depth, 2, nb), g_specs + [st_spec] * 3, [st_spec] * 4, [_sds(w.shape, F32)] * 4,
                        ("arbitrary", "arbitrary", "arbitrary"), gs + [w, m, v])


def _piece_view(col, j, other):
    def view(ref, p):
        R, C = ref.shape
        cc = (1 - p["c"]) if other else p["c"]
        if col:
            hr, sc = R // 2, C // N_CHIPS
            return ref.at[pl.ds(pl.multiple_of(cc * hr, 16), hr), pl.ds(j * sc, sc)]
        hr = R // (2 * N_CHIPS)
        return ref.at[pl.ds(pl.multiple_of((2 * j + cc) * hr, 8), hr), :]
    return view


def _piece_shape(shape, col):
    R, C = shape
    return (R // 2, C // N_CHIPS) if col else (R // (2 * N_CHIPS), C)


def _reduce_scatter(tag, grads, col_sharded, wire_dtype):
    n = len(grads)
    shapes = [_piece_shape(g.shape, col) for g, col in zip(grads, col_sharded)]

    slot = lambda j: (lambda ref, p: ref.at[j])
    remote = [(a, _piece_view(col_sharded[a], j, True), a, slot(j), SIBLING) for a in range(n) for j in range(N_CHIPS)]
    local = [(a, _piece_view(col_sharded[a], j, False), n + a, slot(j)) for a in range(n) for j in range(N_CHIPS)]
    got = _exchange("rs_sibling_" + tag, grads, [_sds((N_CHIPS,) + s, g.dtype) for s, g in zip(shapes, grads)] * 2, remote, local)
    theirs, mine = got[:n], got[n:]
    chip_sum = [_ew("rs_add2_" + tag, lambda a, b: (a.astype(F32) + b.astype(F32),),
                    [m.reshape(-1, m.shape[-1]), t.reshape(-1, t.shape[-1])], [wire_dtype])[0].reshape(m.shape)
                for m, t in zip(mine, theirs)]

    def send_view(mask):
        return lambda ref, p: ref.at[2 * ((1 - p["x"]) if mask[0] else p["x"]) + ((1 - p["y"]) if mask[1] else p["y"])]
    remote = [(a, send_view(m), a, slot(k), m) for a in range(n) for k, m in enumerate(CHIP_MASKS)]
    local = [(a, lambda ref, p: ref.at[p["chip"]], n + a, _whole) for a in range(n)]
    got = _exchange("rs_chips_" + tag, chip_sum,
                    [_sds((len(CHIP_MASKS),) + s, wire_dtype) for s in shapes] + [_sds(s, wire_dtype) for s in shapes], remote, local)
    landed, own = got[:n], got[n:]
    half = [_ew("rs_add4_" + tag, lambda o, a, b, c: (((o.astype(F32) + a.astype(F32)) + b.astype(F32)) + c.astype(F32),),
                [o, l[0], l[1], l[2]], [F32])[0] for o, l in zip(own, landed)]

    def half_rows(ref, p):
        hr = ref.shape[0] // 2
        return ref.at[pl.ds(pl.multiple_of(p["c"] * hr, 8), hr), :]
    remote = [(a, _whole, a, half_rows, SIBLING) for a in range(n)]
    local = [(a, _whole, a, half_rows) for a in range(n)]
    return _exchange("rs_halves_" + tag, half, [_sds((2 * s[0], s[1]), F32) for s in shapes], remote, local)


def _ssm_prepare(W):
    lr, li, ls = W['ssm_lambda_re'], W['ssm_lambda_im'], W['ssm_log_step']
    depth, G = ls.shape
    GG = depth * G
    flat = lambda a: a.reshape(-1, LANES)
    bc = lambda a: flat(jnp.broadcast_to(a, (depth, G, STATE, GROUP_CH)))
    flat3 = (bc(lr[..., None]), bc(li[..., None]), bc(ls[:, :, None, None]))
    bbr, bbi = _ssm_bbar(*flat3, flat(W['ssm_b_re']), flat(W['ssm_b_im']))
    bbr, bbi = bbr.reshape(GG, STATE, GROUP_CH), bbi.reshape(GG, STATE, GROUP_CH)
    row = lambda a: a.reshape(1, GG * STATE)
    tf, tr = _ssm_tables(row(lr), row(li), row(jnp.broadcast_to(ls[..., None], (depth, G, STATE))))
    cr = W['ssm_c_re'].reshape(GG, GROUP_CH, STATE).transpose(0, 2, 1)
    ci = -W['ssm_c_im'].reshape(GG, GROUP_CH, STATE).transpose(0, 2, 1)
    stacked = dict(wb=(_tile_w(bbr), _tile_w(bbi)), wbT=(_slab_w(bbr), _slab_w(bbi)),
                   wc=(_slab_w(cr), _slab_w(ci)), wcT=(_tile_w(cr), _tile_w(ci)))
    return flat3, [dict(stacked, tf=tf, tr=tr, layer=l, depth=depth) for l in range(depth)]


def _ssm_param_grads(W, flat3, raw):
    depth, G = W['ssm_log_step'].shape
    GG = depth * G
    cat = lambda k: jnp.concatenate([r[k] for r in raw], axis=0)
    flat = lambda a: a.reshape(-1, LANES)
    out = {}
    out['ssm_c_re'] = _slab_w_grad(cat(2)).transpose(0, 2, 1).reshape(W['ssm_c_re'].shape)
    out['ssm_c_im'] = -_slab_w_grad(cat(3)).transpose(0, 2, 1).reshape(W['ssm_c_im'].shape)
    dbr, dbi, qr, qi = _ssm_param_bwd_flat(*flat3, flat(W['ssm_b_re']), flat(W['ssm_b_im']),
                                           flat(_tile_w_grad(cat(0))), flat(_tile_w_grad(cat(1))))
    out['ssm_b_re'], out['ssm_b_im'] = dbr.reshape(W['ssm_b_re'].shape), dbi.reshape(W['ssm_b_im'].shape)
    pick = lambda q: q[:, ::GROUP_CH].reshape(GG, STATE)
    sums = lambda k: jnp.concatenate([r[k].reshape(8, G, STATE) for r in raw], axis=1)
    dlr, dli, dls = _ssm_param_bwd(W['ssm_lambda_re'].reshape(GG, STATE), W['ssm_lambda_im'].reshape(GG, STATE),
                                   W['ssm_log_step'].reshape(GG, 1), pick(qr), pick(qi), sums(4), sums(5))
    out['ssm_lambda_re'], out['ssm_lambda_im'] = dlr.reshape(depth, G, STATE), dli.reshape(depth, G, STATE)
    out['ssm_log_step'] = dls.reshape(depth, G)
    return out


def _layer_fwd(x, p, weight, dims):
    attn_w, kv_w, u_off = dims['attn_w'], dims['kv_w'], dims['u_off']
    s = p['s5']
    h = _rms_fwd("norm_mix", [x], [p['norm_mix_g']], BF16)
    w = {'w_in': weight('w_in', h)}
    proj, = _mm("mm_in", h, w['w_in'], 'nn', [F32])
    attn = _attn_fwd(proj, p['q_norm_g'], p['k_norm_g'], p['attn_sinks'], attn_w, kv_w)
    xr, xi, y, gl = _ssm_fwd(proj, u_off, s, p['ssm_d'])
    w['w_glu'] = weight('w_glu', gl)
    ssm, z = _mm("mm_glu", gl, w['w_glu'], 'nn', [F32, F32], extras=[('row', p['b_glu']), ('tile', gl)],
                 epi=lambda acc, b, g: ((lambda zz: (g * jax.nn.sigmoid(zz), zz))(acc + b)))
    mix = _rms_fwd("norm_heads", [attn, ssm], [p['attn_out_g'], p['ssm_out_g']], BF16)
    w['w_out'] = weight('w_out', mix)
    x_mid, = _mm("mm_out", mix, w['w_out'], 'nn', [F32], extras=[('tile', x)], epi=lambda acc, r: (acc + r,))
    h2 = _rms_fwd("norm_mlp", [x_mid], [p['norm_mlp_g']], BF16)
    w['w_up'] = weight('w_up', h2)
    a, r = _mm("mm_up", h2, w['w_up'], 'nn', [F32, BF16],
               epi=lambda acc: (acc, jnp.square(jnp.maximum(acc, 0.0))))
    w['w_down'] = weight('w_down', r)
    x_out, = _mm("mm_down", r, w['w_down'], 'nn', [F32], extras=[('tile', x_mid)], epi=lambda acc, rr: (acc + rr,))
    saved = dict(x=x, h=h, proj=proj, attn=attn, xr=xr, xi=xi, y=y, gl=gl, z=z, ssm=ssm, mix=mix, x_mid=x_mid, h2=h2, a=a, r=r, w=w)
    return x_out, saved


def _layer_bwd(dx, dx16, sv, p, dims, reduce_grads):
    attn_w, kv_w, u_off = dims['attn_w'], dims['kv_w'], dims['u_off']
    s, w = p['s5'], sv['w']
    gb, gs = {}, {}
    da, = _mm("mm_down_dx", dx16, w['w_down'], 'nt', [BF16], extras=[('tile', sv['a'])],
              epi=lambda acc, a: (acc * (2.0 * jnp.maximum(a, 0.0)),))
    gb['w_down'], = _mm("mm_down_dw", sv['r'], dx16, 'tn', [BF16])
    dh2, = _mm("mm_up_dx", da, w['w_up'], 'nt', [F32])
    gb['w_up'], = _mm("mm_up_dw", sv['h2'], da, 'tn', [BF16])
    token = reduce_grads(('w_up', 'w_down'), [gb['w_up'], gb['w_down']])
    (dx_mid,), (gs['norm_mlp_g'],), dx_mid16 = _rms_bwd("norm_mlp_bwd", [sv['x_mid']], [p['norm_mlp_g'] + token], dh2, resid=dx)
    dmix, = _mm("mm_out_dx", dx_mid16, w['w_out'], 'nt', [F32])
    gb['w_out'], = _mm("mm_out_dw", sv['mix'], dx_mid16, 'tn', [BF16])
    (dattn, dssm), (gs['attn_out_g'], gs['ssm_out_g']) = _rms_bwd(
        "norm_heads_bwd", [sv['attn'], sv['ssm']], [p['attn_out_g'], p['ssm_out_g']], dmix)
    dz, gs['b_glu'] = _glu_dz(dssm, sv['gl'], sv['z'])
    dy, = _mm("mm_glu_dx", dz, w['w_glu'], 'nt', [F32], extras=[('tile', dssm), ('tile', sv['z']), ('tile', sv['y'])],
              epi=lambda acc, ds, z, y: ((acc + ds * jax.nn.sigmoid(z)) * _gelu_grad(y),))
    gb['w_glu'], = _mm("mm_glu_dw", sv['gl'], dz, 'tn', [BF16])
    dproj, dkn, dv, gs['q_norm_g'], gs['attn_sinks'] = _attn_bwd(sv['proj'], sv['attn'], dattn, p['q_norm_g'], p['k_norm_g'],
                                                                  p['attn_sinks'], attn_w, kv_w)
    dproj, gs['k_norm_g'] = _knorm_bwd(sv['proj'], dkn, dv, p['k_norm_g'], dproj, attn_w, kv_w)
    dproj, gs['ssm_d'], *gs['s5_raw'] = _ssm_bwd(dy, sv['proj'], u_off, sv['xr'], sv['xi'], s, p['ssm_d'], dproj)
    dh, = _mm("mm_in_dx", dproj, w['w_in'], 'nt', [F32])
    gb['w_in'], = _mm("mm_in_dw", sv['h'], dproj, 'tn', [BF16])
    token = reduce_grads(('w_in', 'w_glu', 'w_out'), [gb['w_in'], gb['w_glu'], gb['w_out']])
    (dx_in,), (gs['norm_mix_g'],), dx_in16 = _rms_bwd("norm_mix_bwd", [sv['x']], [p['norm_mix_g'] + token], dh, resid=dx_mid)
    return dx_in, dx_in16, gs


PACK_COLS = 1024


def _pack(arrs, rows):
    flat = jnp.concatenate([a.reshape(-1).astype(F32) for a in arrs])
    return jnp.pad(flat, (0, rows * PACK_COLS - flat.shape[0])).reshape(rows, PACK_COLS)


def _unpack(packed, shapes):
    flat = packed.reshape(-1)
    out, off = [], 0
    for s in shapes:
        n = int(np.prod(s))
        out.append(flat[off:off + n].reshape(s))
        off += n
    return out


def _pack_rows(shapes, multiple):
    n = sum(int(np.prod(s)) for s in shapes)
    rows = -(-n // PACK_COLS)
    return -(-rows // multiple) * multiple


def kernel(x, meta_tokens, norm_mix_g, w_in, q_norm_g, k_norm_g, attn_sinks, ssm_lambda_re, ssm_lambda_im, ssm_log_step, ssm_b_re, ssm_b_im, ssm_c_re, ssm_c_im, ssm_d, w_glu, b_glu, attn_out_g, ssm_out_g, w_out, norm_mlp_g, w_up, w_down, loss_target, m_meta_tokens, m_norm_mix_g, m_w_in, m_q_norm_g, m_k_norm_g, m_attn_sinks, m_ssm_lambda_re, m_ssm_lambda_im, m_ssm_log_step, m_ssm_b_re, m_ssm_b_im, m_ssm_c_re, m_ssm_c_im, m_ssm_d, m_w_glu, m_b_glu, m_attn_out_g, m_ssm_out_g, m_w_out, m_norm_mlp_g, m_w_up, m_w_down, v_meta_tokens, v_norm_mix_g, v_w_in, v_q_norm_g, v_k_norm_g, v_attn_sinks, v_ssm_lambda_re, v_ssm_lambda_im, v_ssm_log_step, v_ssm_b_re, v_ssm_b_im, v_ssm_c_re, v_ssm_c_im, v_ssm_d, v_w_glu, v_b_glu, v_attn_out_g, v_ssm_out_g, v_w_out, v_norm_mlp_g, v_w_up, v_w_down):
    args = (meta_tokens, norm_mix_g, w_in, q_norm_g, k_norm_g, attn_sinks, ssm_lambda_re, ssm_lambda_im, ssm_log_step, ssm_b_re, ssm_b_im, ssm_c_re, ssm_c_im, ssm_d, w_glu, b_glu, attn_out_g, ssm_out_g, w_out, norm_mlp_g, w_up, w_down)
    ms = (m_meta_tokens, m_norm_mix_g, m_w_in, m_q_norm_g, m_k_norm_g, m_attn_sinks, m_ssm_lambda_re, m_ssm_lambda_im, m_ssm_log_step, m_ssm_b_re, m_ssm_b_im, m_ssm_c_re, m_ssm_c_im, m_ssm_d, m_w_glu, m_b_glu, m_attn_out_g, m_ssm_out_g, m_w_out, m_norm_mlp_g, m_w_up, m_w_down)
    vs = (v_meta_tokens, v_norm_mix_g, v_w_in, v_q_norm_g, v_k_norm_g, v_attn_sinks, v_ssm_lambda_re, v_ssm_lambda_im, v_ssm_log_step, v_ssm_b_re, v_ssm_b_im, v_ssm_c_re, v_ssm_c_im, v_ssm_d, v_w_glu, v_b_glu, v_attn_out_g, v_ssm_out_g, v_w_out, v_norm_mlp_g, v_w_up, v_w_down)
    W = dict(zip(WEIGHTS, args))
    M = dict(zip(WEIGHTS, ms))
    V = dict(zip(WEIGHTS, vs))
    depth = norm_mix_g.shape[0]
    seq, D = x.shape[1], x.shape[2]
    attn_w = D // 2
    kv_w = attn_w // KV_GROUP
    dims = dict(attn_w=attn_w, kv_w=kv_w, u_off=(attn_w + 2 * kv_w) // LANES)
    small_names = [n for n in WEIGHTS if n not in BIG and n != 'meta_tokens']
    chip = 2 * lax.axis_index("x") + lax.axis_index("y")

    gathers, started = [], jnp.zeros((), F32)
    for l in range(depth):
        placed = [_ag_place("ag_place_" + n, W[n], l, COL_SHARDED[n], BF16) for n in BIG]
        groups = [_ag_copies(a, COL_SHARDED[n]) for a, n in enumerate(BIG)]
        if l == 0:
            placed = [_ag_place("ag_place_meta", meta_tokens[None], 0, True, F32)] + placed
            groups = [_ag_copies(0, True)] + [_ag_copies(a + 1, COL_SHARDED[n]) for a, n in enumerate(BIG)]
        sems, bufs, token = _split_start("ag_start_%d" % l, placed, groups)
        gathers.append(dict(zip((['meta_tokens'] if l == 0 else []) + BIG, zip(sems, bufs))))
        started = started + token[0, 0]

    def gathered(l, n, after):
        sems, buf = gathers[l][n]
        return _split_wait("ag_wait_%d_%s" % (l, n), [buf], sems, after, _ag_copies(0, n == 'meta_tokens' or COL_SHARDED[n]))[0]

    h_res = jnp.concatenate([jnp.zeros((PAD, D), F32), gathered(0, 'meta_tokens', started.reshape(1, 1)), x[0]], axis=0)
    s5_flat3, s5_layers = _ssm_prepare(W)
    layer_p = []
    for l in range(depth):
        p = {n: W[n][l][None, :] for n in ('norm_mix_g', 'q_norm_g', 'k_norm_g', 'attn_sinks', 'ssm_d', 'b_glu', 'attn_out_g',
                                             'ssm_out_g', 'norm_mlp_g')}
        p['s5'] = s5_layers[l]
        layer_p.append(p)
    saved = []
    for l in range(depth):
        h_res, sv = _layer_fwd(h_res, layer_p[l], functools.partial(gathered, l), dims)
        saved.append(sv)
    loss_local, dx, dx16 = _loss(h_res, loss_target[0])
    loss = lax.psum(loss_local, ("x", "y", "c"))

    small_grads = [None] * depth
    shard_grads = {}
    pending = []

    def finish(after):
        while pending:
            l_, names, st = pending.pop(0)
            mine, other = _rs_finish("%d_%s" % (l_, names[0]), st, after)
            for a, n in enumerate(names):
                shard_grads[(l_, n)] = (mine[a], other[a])

    def reduce_grads(l, names, grads):
        st = _rs_begin("%d_%s" % (l, names[0]), list(grads), [COL_SHARDED[n] for n in names])
        pending.append((l, names, st))
        return st['token'][0, 0]

    for l in reversed(range(depth)):
        dx, dx16, gs = _layer_bwd(dx, dx16, saved[l], layer_p[l], dims, functools.partial(reduce_grads, l))
        saved[l] = None
        small_grads[l] = gs
        newest = pending.pop()
        finish(dx)
        pending.append(newest)
    grad_x = dx[BLOCK:].reshape(x.shape)

    g_small = _ssm_param_grads(W, s5_flat3, [small_grads[l]['s5_raw'] for l in range(depth)])
    for n in small_names:
        if n not in g_small:
            g_small[n] = jnp.stack([small_grads[l][n].reshape(W[n].shape[1:]) for l in range(depth)])
    g_shapes = [(N_META, D)] + [W[n].shape for n in small_names]
    rows = _pack_rows(g_shapes, 8 * 2 * N_CHIPS)
    packed = _pack([dx[PAD:BLOCK]] + [g_small[n] for n in small_names], rows)
    red, = _reduce_scatter("small", [packed], [False], F32)
    red_full, = _all_gather("ag_small", [red], [False])
    finish(red_full)
    g_list = _unpack(red_full, g_shapes)
    g_meta = lax.dynamic_slice_in_dim(g_list[0], chip * meta_tokens.shape[1], meta_tokens.shape[1], axis=1)
    G = dict(zip(small_names, g_list[1:]))
    G['meta_tokens'] = g_meta

    out = {}
    for n in BIG:
        out[n] = _adamw_big("adamw_" + n, [shard_grads[(l, n)][0] for l in range(depth)],
                            [shard_grads[(l, n)][1] for l in range(depth)], W[n], M[n], V[n])
    for n in ['meta_tokens'] + small_names:
        rows2d = lambda a: a.reshape(-1, a.shape[-1])
        upd = _ew("adamw_" + n, _adam_math, [rows2d(W[n]), rows2d(G[n]), rows2d(M[n]), rows2d(V[n])], [F32] * 3)
        out[n] = (G[n], *[u.reshape(W[n].shape) for u in upd])
    return (loss, grad_x, *[out[n][0] for n in WEIGHTS], *[out[n][1] for n in WEIGHTS],
            *[out[n][2] for n in WEIGHTS], *[out[n][3] for n in WEIGHTS])
```

```python
import functools
import math

import numpy as np
import jax
import jax.numpy as jnp
from jax import lax
from jax.experimental import pallas as pl
from jax.experimental.pallas import tpu as pltpu

F32 = jnp.float32
BF16 = jnp.bfloat16
MESH = pl.DeviceIdType.MESH

N_META = 16
HEAD_DIM = 64
KV_GROUP = 4
GROUP_CH = 16
STATE = 64
BLOCK = 128
PAD = BLOCK - N_META
NORM_EPS = 1e-6
NEG_INF = -1e30
LANES = 128
V7X_VMEM_LIMIT_BYTES = 56 * 1024 * 1024
MM_VMEM_BUDGET_BYTES = 44 * 1024 * 1024

ADAM_LR, ADAM_B1, ADAM_B2, ADAM_EPS, ADAM_WD, ADAM_STEP = 0.001, 0.9, 0.999, 1e-08, 0.01, 10

WEIGHTS = ['meta_tokens', 'norm_mix_g', 'w_in', 'q_norm_g', 'k_norm_g', 'attn_sinks', 'ssm_lambda_re',
           'ssm_lambda_im', 'ssm_log_step', 'ssm_b_re', 'ssm_b_im', 'ssm_c_re', 'ssm_c_im', 'ssm_d', 'w_glu',
           'b_glu', 'attn_out_g', 'ssm_out_g', 'w_out', 'norm_mlp_g', 'w_up', 'w_down']
BIG = ['w_in', 'w_glu', 'w_out', 'w_up', 'w_down']
COL_SHARDED = {'w_in': True, 'w_glu': False, 'w_out': False, 'w_up': True, 'w_down': False}
N_CHIPS = 4


def _pick(n, cands):
    for c in cands:
        if c <= n and n % c == 0:
            return c
    return n


def _params(sem):
    return pltpu.CompilerParams(dimension_semantics=sem, vmem_limit_bytes=V7X_VMEM_LIMIT_BYTES)


def _pcall(body, **kw):
    return pl.pallas_call(body, **kw)


def _sds(shape, dtype):
    return jax.ShapeDtypeStruct(shape, dtype)


_DIMS = {'nn': ((1,), (0,)), 'nt': ((1,), (1,)), 'tn': ((0,), (0,))}


def _mm(name, a, b, mode, out_dtypes, extras=(), epi=None):
    if mode == 'nn':
        (M, K), (_, N) = a.shape, b.shape
    elif mode == 'nt':
        (M, K), (N, _) = a.shape, b.shape
    else:
        (K, M), (_, N) = a.shape, b.shape
    tile_bytes = 4 * len([k for k, _ in extras if k == 'tile']) + sum(jnp.dtype(d).itemsize for d in out_dtypes)

    def fits(tm, tn, tk):
        need = 2 * tm * tk * a.dtype.itemsize + 2 * tk * tn * b.dtype.itemsize + 4 * tm * tn + 2 * tm * tn * tile_bytes
        return need <= MM_VMEM_BUDGET_BYTES

    if mode == 'tn':
        tm, tk_cands = _pick(M, (1024, 512, 256, 128)), (1408, 704, 384, 128)
    else:
        tm, tk_cands = _pick(M, (1408, 704, 384, 128)), (2048, 1024, 512, 256, 128)
    tk_cands = [t for t in tk_cands if t <= K and K % t == 0] or [K]
    tn_cands = [t for t in (2048, 1280, 1024, 640, 512, 256, 128) if t <= N and N % t == 0] or [N]
    if mode != 'tn' and tk_cands[0] == K and a.dtype == BF16:
        tk_cands = tk_cands[:1]
    tn, tk = next(((tn_, tk_) for tn_ in tn_cands for tk_ in tk_cands if fits(tm, tn_, tk_)), (tn_cands[-1], tk_cands[-1]))
    nk = K // tk
    a_spec = pl.BlockSpec((tk, tm), lambda i, j, k: (k, i)) if mode == 'tn' else pl.BlockSpec((tm, tk), lambda i, j, k: (i, k))
    b_spec = pl.BlockSpec((tn, tk), lambda i, j, k: (j, k)) if mode == 'nt' else pl.BlockSpec((tk, tn), lambda i, j, k: (k, j))
    ex_specs = [pl.BlockSpec((tm, tn), lambda i, j, k: (i, j)) if kind == 'tile' else pl.BlockSpec((1, tn), lambda i, j, k: (0, j))
                for kind, _ in extras]
    ne, no = len(extras), len(out_dtypes)
    dims = (_DIMS[mode], ((), ()))

    def body(a_ref, b_ref, *rest):
        ex, outs, acc = rest[:ne], rest[ne:ne + no], rest[ne + no]
        k = pl.program_id(2)

        @pl.when(k == 0)
        def _():
            acc[...] = jnp.zeros_like(acc)

        acc[...] += lax.dot_general(a_ref[...].astype(BF16), b_ref[...].astype(BF16), dims, preferred_element_type=F32)

        @pl.when(k == nk - 1)
        def _():
            r = acc[...]
            res = epi(r, *[e[...] for e in ex]) if epi is not None else (r,)
            for o, v in zip(outs, res):
                o[...] = v.astype(o.dtype)

    outs = _pcall(
        body, name=name, grid=(M // tm, N // tn, nk),
        in_specs=[a_spec, b_spec] + ex_specs,
        out_specs=[pl.BlockSpec((tm, tn), lambda i, j, k: (i, j)) for _ in out_dtypes],
        out_shape=[_sds((M, N), d) for d in out_dtypes],
        scratch_shapes=[pltpu.VMEM((tm, tn), F32)],
        compiler_params=_params(("parallel", "parallel", "arbitrary")),
    )(a, b, *[e for _, e in extras])
    return outs


def _ew(name, fn, ins, out_dtypes):
    R, C = ins[0].shape
    tr = _pick(R, tuple(t for t in (1024, 512, 256, 128, 64, 32, 16, 8) if t * C <= 512 * 1024) or (8,))
    n_in = len(ins)

    def body(*refs):
        res = fn(*[r[...] for r in refs[:n_in]])
        for o, v in zip(refs[n_in:], res):
            o[...] = v.astype(o.dtype)

    spec = pl.BlockSpec((tr, C), lambda i: (i, 0))
    return _pcall(body, name=name, grid=(R // tr,), in_specs=[spec] * n_in, out_specs=[spec] * len(out_dtypes),
                  out_shape=[_sds((R, C), d) for d in out_dtypes], compiler_params=_params(("parallel",)))(*ins)


def _adam_math(w, g, m, v):
    m = ADAM_B1 * m + (1.0 - ADAM_B1) * g
    v = ADAM_B2 * v + (1.0 - ADAM_B2) * (g * g)
    m_hat = m / (1.0 - ADAM_B1 ** ADAM_STEP)
    v_hat = v / (1.0 - ADAM_B2 ** ADAM_STEP)
    delta = -ADAM_LR * (m_hat / (jnp.sqrt(v_hat) + ADAM_EPS) + ADAM_WD * w)
    return delta, m, v


def _rms_fwd(name, xs, gs, out_dtype):
    L = xs[0].shape[0]
    ws = [x.shape[1] for x in xs]
    n = len(xs)
    tr = _pick(L, (384, 256, 128))

    def body(*refs):
        o = refs[2 * n]
        off = 0
        for i in range(n):
            x = refs[i][...]
            r = lax.rsqrt(jnp.mean(x * x, axis=-1, keepdims=True) + NORM_EPS)
            o[:, off:off + ws[i]] = ((x * r) * refs[n + i][...]).astype(o.dtype)
            off += ws[i]

    return _pcall(body, name=name, grid=(L // tr,),
                  in_specs=[pl.BlockSpec((tr, w), lambda i: (i, 0)) for w in ws] + [pl.BlockSpec((1, w), lambda i: (0, 0)) for w in ws],
                  out_specs=pl.BlockSpec((tr, sum(ws)), lambda i: (i, 0)), out_shape=_sds((L, sum(ws)), out_dtype),
                  compiler_params=_params(("parallel",)))(*xs, *gs)


def _rms_bwd(name, xs, gs, dy, resid=None):
    L = xs[0].shape[0]
    ws = [x.shape[1] for x in xs]
    n = len(xs)
    tr = _pick(L, (384, 256, 128))
    has_res = resid is not None

    def body(*refs):
        x_refs, g_refs, dy_ref = refs[:n], refs[n:2 * n], refs[2 * n]
        p = 2 * n + 1
        res_ref = refs[p] if has_res else None
        p += 1 if has_res else 0
        dx_refs, dg_refs = refs[p:p + n], refs[p + n:p + 2 * n]
        dx16_ref = refs[p + 2 * n] if has_res else None
        first = pl.program_id(0) == 0
        off = 0
        for i in range(n):
            x = x_refs[i][...]
            d = dy_ref[:, off:off + ws[i]]
            r = lax.rsqrt(jnp.mean(x * x, axis=-1, keepdims=True) + NORM_EPS)
            xh = x * r
            dg = jnp.sum(d * xh, axis=0, keepdims=True)

            @pl.when(first)
            def _(i=i):
                dg_refs[i][...] = jnp.zeros_like(dg_refs[i])

            dg_refs[i][...] += dg
            dyg = d * g_refs[i][...]
            dx = r * (dyg - xh * jnp.mean(dyg * xh, axis=-1, keepdims=True))
            if has_res:
                dx = dx + res_ref[...]
                dx16_ref[...] = dx.astype(BF16)
            dx_refs[i][...] = dx
            off += ws[i]

    in_specs = ([pl.BlockSpec((tr, w), lambda i: (i, 0)) for w in ws] + [pl.BlockSpec((1, w), lambda i: (0, 0)) for w in ws]
                + [pl.BlockSpec((tr, sum(ws)), lambda i: (i, 0))])
    ins = list(xs) + list(gs) + [dy]
    if has_res:
        in_specs.append(pl.BlockSpec((tr, ws[0]), lambda i: (i, 0)))
        ins.append(resid)
    out_specs = [pl.BlockSpec((tr, w), lambda i: (i, 0)) for w in ws] + [pl.BlockSpec((1, w), lambda i: (0, 0)) for w in ws]
    out_shape = [_sds((L, w), F32) for w in ws] + [_sds((1, w), F32) for w in ws]
    if has_res:
        out_specs.append(pl.BlockSpec((tr, ws[0]), lambda i: (i, 0)))
        out_shape.append(_sds((L, ws[0]), BF16))
    outs = _pcall(body, name=name, grid=(L // tr,), in_specs=in_specs, out_specs=out_specs, out_shape=out_shape,
                  compiler_params=_params(("arbitrary",)))(*ins)
    return (outs[:n], outs[n:2 * n], outs[2 * n]) if has_res else (outs[:n], outs[n:])


def _loss(xl, target):
    Lp, D = xl.shape

    def body(x_ref, t_ref, dy_ref, dy16_ref, loss_ref):
        n = pl.program_id(0)

        @pl.when(n == 0)
        def _():
            loss_ref[...] = jnp.zeros_like(loss_ref)
            dy_ref[...] = jnp.zeros_like(dy_ref)
            dy16_ref[...] = jnp.zeros_like(dy16_ref)

        @pl.when(n > 0)
        def _():
            err = x_ref[...] - t_ref[...]
            dy = err * (1.0 / D)
            dy_ref[...] = dy
            dy16_ref[...] = dy.astype(BF16)
            loss_ref[...] += jnp.sum(err * err) * (0.5 / D)

    blk = pl.BlockSpec((BLOCK, D), lambda n: (n, 0))
    dy, dy16, loss = _pcall(body, name="loss_head", grid=(Lp // BLOCK,),
                            in_specs=[blk, pl.BlockSpec((BLOCK, D), lambda n: (jnp.maximum(n - 1, 0), 0))],
                            out_specs=[blk, blk, pl.BlockSpec((8, LANES), lambda n: (0, 0))],
                            out_shape=[_sds((Lp, D), F32), _sds((Lp, D), BF16), _sds((8, LANES), F32)],
                            compiler_params=_params(("arbitrary",)))(xl, target)
    return loss[0, 0], dy, dy16


GROUP_ROWS = KV_GROUP * BLOCK


def _attn_mask_dist(n):
    r = lax.broadcasted_iota(jnp.int32, (GROUP_ROWS, 3 * BLOCK), 0)
    i = r & (BLOCK - 1)
    j = lax.broadcasted_iota(jnp.int32, (GROUP_ROWS, 3 * BLOCK), 1)
    in_band = j < 2 * BLOCK
    band = in_band & (j > i) & (j <= i + BLOCK) & (j >= 2 * BLOCK - BLOCK * n)
    jm = j - 2 * BLOCK
    meta = (~in_band) & (jm >= PAD) & (jm <= BLOCK * n + i)
    dist = jnp.where(in_band, BLOCK + i - j, BLOCK * n + i - jm).astype(F32)
    return band | meta, dist


def _head_norm(x, g):
    r = lax.rsqrt(jnp.mean(x * x, axis=-1, keepdims=True) + NORM_EPS)
    return (x * r) * g, r


def _attn_specs(attn_w, kv_w):
    kb = attn_w // kv_w
    q_spec = pl.BlockSpec((BLOCK, attn_w), lambda n: (n, 0))

    def kv(col):
        return [pl.BlockSpec((BLOCK, kv_w), lambda n: (jnp.maximum(n - 1, 0), col)),
                pl.BlockSpec((BLOCK, kv_w), lambda n: (n, col)),
                pl.BlockSpec((BLOCK, kv_w), lambda n: (0, col))]

    return q_spec, kv(kb), kv(kb + 1)


def _slopes(n_heads):
    return [2.0 ** (-8.0 * (h + 1) / n_heads) for h in range(n_heads)]


def _head_slice(h):
    return slice(h * HEAD_DIM, (h + 1) * HEAD_DIM)


def _stack_heads(ref, kh):
    return jnp.concatenate([ref[:, _head_slice(kh * KV_GROUP + g)] for g in range(KV_GROUP)], axis=0)


def _group_column(vals):
    return jnp.concatenate([jnp.broadcast_to(v, (BLOCK, 1)) for v in vals], axis=0)


def _group_inputs(kh, slopes, q_ref, kp, kc, km, vp, vc, vm, gq_ref, gk_ref, sk_ref):
    cs = _head_slice(kh)
    kn, _ = _head_norm(jnp.concatenate([kp[:, cs], kc[:, cs], km[:, cs]], axis=0), gk_ref[...])
    vcat = jnp.concatenate([vp[:, cs], vc[:, cs], vm[:, cs]], axis=0).astype(BF16)
    q = _stack_heads(q_ref, kh)
    qn, rq = _head_norm(q, gq_ref[...])
    heads = range(kh * KV_GROUP, (kh + 1) * KV_GROUP)
    slope = _group_column([jnp.full((1, 1), slopes[h], F32) for h in heads])
    sink = _group_column([sk_ref[0:1, h:h + 1] for h in heads])
    return q, qn, rq, kn, vcat, slope, sink


def _scores(qn, kn, slope, sink, mask, dist):
    s = lax.dot_general(qn.astype(BF16), kn.astype(BF16), (((1,), (1,)), ((), ())), preferred_element_type=F32)
    s = s * (1.0 / math.sqrt(HEAD_DIM)) - slope * dist
    s = jnp.where(mask, s, NEG_INF)
    m = jnp.maximum(jnp.max(s, axis=-1, keepdims=True), sink)
    p = jnp.exp(s - m)
    ps = jnp.exp(sink - m)
    inv = 1.0 / (jnp.sum(p, axis=-1, keepdims=True) + ps)
    return p * inv, ps * inv


def _attn_fwd(proj, gq, gk, sinks, attn_w, kv_w):
    Lp = proj.shape[0]
    n_heads, n_kv = attn_w // HEAD_DIM, kv_w // HEAD_DIM
    slopes = _slopes(n_heads)
    q_spec, k_specs, v_specs = _attn_specs(attn_w, kv_w)

    def body(q_ref, kp, kc, km, vp, vc, vm, gq_ref, gk_ref, sk_ref, o_ref):
        mask, dist = _attn_mask_dist(pl.program_id(0))
        for kh in range(n_kv):
            _, qn, _, kn, vcat, slope, sink = _group_inputs(kh, slopes, q_ref, kp, kc, km, vp, vc, vm, gq_ref, gk_ref, sk_ref)
            p, _ = _scores(qn, kn, slope, sink, mask, dist)
            o = jnp.dot(p.astype(BF16), vcat, preferred_element_type=F32)
            for g in range(KV_GROUP):
                o_ref[:, _head_slice(kh * KV_GROUP + g)] = o[g * BLOCK:(g + 1) * BLOCK]

    small = lambda w: pl.BlockSpec((1, w), lambda n: (0, 0))
    return _pcall(body, name="attn_fwd", grid=(Lp // BLOCK,),
                  in_specs=[q_spec] + k_specs + v_specs + [small(HEAD_DIM), small(HEAD_DIM), small(n_heads)],
                  out_specs=pl.BlockSpec((BLOCK, attn_w), lambda n: (n, 0)), out_shape=_sds((Lp, attn_w), F32),
                  compiler_params=_params(("parallel",)))(proj, proj, proj, proj, proj, proj, proj, gq, gk, sinks)


def _attn_bwd(proj, attn, dattn, gq, gk, sinks, attn_w, kv_w):
    Lp = proj.shape[0]
    n_heads, n_kv = attn_w // HEAD_DIM, kv_w // HEAD_DIM
    slopes = _slopes(n_heads)
    q_spec, k_specs, v_specs = _attn_specs(attn_w, kv_w)
    scale = 1.0 / math.sqrt(HEAD_DIM)
    tn_dims = (((0,), (0,)), ((), ()))

    def body(q_ref, kp, kc, km, vp, vc, vm, o_ref, do_ref, gq_ref, gk_ref, sk_ref, dq_ref, dk_ref, dv_ref, dgq_ref, dsk_ref):
        n = pl.program_id(0)

        @pl.when(n == 0)
        def _():
            dk_ref[...] = jnp.zeros_like(dk_ref)
            dv_ref[...] = jnp.zeros_like(dv_ref)
            dgq_ref[...] = jnp.zeros_like(dgq_ref)
            dsk_ref[...] = jnp.zeros_like(dsk_ref)

        mask, dist = _attn_mask_dist(n)
        lane = lax.broadcasted_iota(jnp.int32, (1, n_heads), 1)
        rows_prev = pl.ds(pl.multiple_of(jnp.maximum(n - 1, 0) * BLOCK, BLOCK), BLOCK)
        rows_cur = pl.ds(pl.multiple_of(n * BLOCK, BLOCK), BLOCK)
        rows_meta = pl.ds(0, BLOCK)
        dgq = jnp.zeros((1, HEAD_DIM), F32)
        dsk = jnp.zeros((1, n_heads), F32)
        for kh in range(n_kv):
            cs = _head_slice(kh)
            q, qn, rq, kn, vcat, slope, sink = _group_inputs(kh, slopes, q_ref, kp, kc, km, vp, vc, vm, gq_ref, gk_ref, sk_ref)
            p, ps = _scores(qn, kn, slope, sink, mask, dist)
            do = _stack_heads(do_ref, kh)
            dd = jnp.sum(do * _stack_heads(o_ref, kh), axis=-1, keepdims=True)
            do16 = do.astype(BF16)
            dp = lax.dot_general(do16, vcat, (((1,), (1,)), ((), ())), preferred_element_type=F32)
            ds16 = (p * (dp - dd)).astype(BF16)
            dsink = -ps * dd
            for g in range(KV_GROUP):
                dsk = dsk + jnp.where(lane == kh * KV_GROUP + g, jnp.sum(dsink[g * BLOCK:(g + 1) * BLOCK]), 0.0)
            dqn = jnp.dot(ds16, kn.astype(BF16), preferred_element_type=F32) * scale
            dkn = lax.dot_general(ds16, qn.astype(BF16), tn_dims, preferred_element_type=F32) * scale
            dvc = lax.dot_general(p.astype(BF16), do16, tn_dims, preferred_element_type=F32)
            xh = q * rq
            dgq = dgq + jnp.sum(dqn * xh, axis=0, keepdims=True)
            dyg = dqn * gq_ref[...]
            dq = rq * (dyg - xh * jnp.mean(dyg * xh, axis=-1, keepdims=True))
            for g in range(KV_GROUP):
                dq_ref[:, _head_slice(kh * KV_GROUP + g)] = dq[g * BLOCK:(g + 1) * BLOCK].astype(dq_ref.dtype)
            for part, rows in enumerate((rows_prev, rows_cur, rows_meta)):
                ps_ = slice(part * BLOCK, (part + 1) * BLOCK)
                dk_ref[rows, cs] += dkn[ps_]
                dv_ref[rows, cs] += dvc[ps_]
        dgq_ref[...] += dgq
        dsk_ref[...] += dsk

    small = lambda w: pl.BlockSpec((1, w), lambda n: (0, 0))
    blk = pl.BlockSpec((BLOCK, attn_w), lambda n: (n, 0))
    whole = pl.BlockSpec((Lp, kv_w), lambda n: (0, 0))
    return _pcall(body, name="attn_bwd", grid=(Lp // BLOCK,),
                  in_specs=[q_spec] + k_specs + v_specs + [blk, blk, small(HEAD_DIM), small(HEAD_DIM), small(n_heads)],
                  out_specs=[blk, whole, whole, small(HEAD_DIM), small(n_heads)],
                  out_shape=[_sds(proj.shape, BF16), _sds((Lp, kv_w), F32), _sds((Lp, kv_w), F32),
                             _sds((1, HEAD_DIM), F32), _sds((1, n_heads), F32)],
                  compiler_params=_params(("arbitrary",)))(proj, proj, proj, proj, proj, proj, proj, attn, dattn, gq, gk, sinks)


def _knorm_bwd(proj, dkn, dv, gk, dproj, attn_w, kv_w):
    Lp = proj.shape[0]
    n_kv = kv_w // HEAD_DIM
    tr = _pick(Lp, (384, 256, 128))

    def body(k_ref, d_ref, dv_ref, g_ref, buf_ref, out_ref, dg_ref):
        @pl.when(pl.program_id(0) == 0)
        def _():
            dg_ref[...] = jnp.zeros_like(dg_ref)

        dg = jnp.zeros((1, HEAD_DIM), F32)
        for kh in range(n_kv):
            cs = slice(kh * HEAD_DIM, (kh + 1) * HEAD_DIM)
            x = k_ref[:, cs]
            d = d_ref[:, cs]
            r = lax.rsqrt(jnp.mean(x * x, axis=-1, keepdims=True) + NORM_EPS)
            xh = x * r
            dg = dg + jnp.sum(d * xh, axis=0, keepdims=True)
            dyg = d * g_ref[...]
            out_ref[:, cs] = (r * (dyg - xh * jnp.mean(dyg * xh, axis=-1, keepdims=True))).astype(out_ref.dtype)
        out_ref[:, kv_w:] = dv_ref[...].astype(out_ref.dtype)
        dg_ref[...] += dg

    kv_blk = pl.BlockSpec((tr, kv_w), lambda i: (i, 0))
    return _pcall(body, name="knorm_bwd", grid=(Lp // tr,),
                  in_specs=[pl.BlockSpec((tr, kv_w), lambda i: (i, attn_w // kv_w)), kv_blk, kv_blk,
                            pl.BlockSpec((1, HEAD_DIM), lambda i: (0, 0)), pl.BlockSpec(memory_space=pl.ANY)],
                  out_specs=[pl.BlockSpec((tr, 2 * kv_w), lambda i: (i, attn_w // (2 * kv_w))), pl.BlockSpec((1, HEAD_DIM), lambda i: (0, 0))],
                  out_shape=[_sds(dproj.shape, dproj.dtype), _sds((1, HEAD_DIM), F32)],
                  input_output_aliases={4: 0},
                  compiler_params=_params(("arbitrary",)))(proj, dkn, dv, gk, dproj)


def _ssm_bbar(lr, li, ls, br, bi):
    def fn(lr, li, ls, br, bi):
        fr, fi = _zoh_factor(lr, li, ls)
        return fr * br - fi * bi, fr * bi + fi * br

    return _ew("ssm_bbar", fn, [lr, li, ls, br, bi], [F32, F32])


def _lam_bar(lr, li, ls):
    dl = jnp.exp(ls)
    e = jnp.exp(lr * dl)
    return e * jnp.cos(li * dl), e * jnp.sin(li * dl), dl


def _zoh_factor(lr, li, ls):
    ar, ai, _ = _lam_bar(lr, li, ls)
    n2 = lr * lr + li * li
    ivr, ivi = lr / n2, -li / n2
    return (ar - 1.0) * ivr - ai * ivi, (ar - 1.0) * ivi + ai * ivr


SCAN_SHIFTS = (1, 2, 4)


def _ssm_tables(lr, li, ls):
    Wx = lr.shape[1]

    def body(lr_ref, li_ref, ls_ref, tf_ref, tr_ref):
        dl = jnp.exp(ls_ref[...])
        zr, zi = lr_ref[...] * dl, li_ref[...] * dl
        row = lax.broadcasted_iota(jnp.int32, (8, Wx), 0)

        def power(kf):
            e = jnp.exp(kf * zr)
            return e * jnp.cos(kf * zi), e * jnp.sin(kf * zi)

        for ref, rev in ((tf_ref, False), (tr_ref, True)):
            sgn = -1.0 if rev else 1.0
            for k, d in enumerate(SCAN_SHIFTS):
                ar, ai = power(jnp.full((8, Wx), float(d), F32))
                keep = (row < 8 - d) if rev else (row >= d)
                ref[k] = jnp.where(keep, ar, 0.0)
                ref[4 + k] = jnp.where(keep, sgn * ai, 0.0)
            pr, pi = power(((8 - row) if rev else (row + 1)).astype(F32))
            ref[3] = pr
            ref[7] = sgn * pi

    full = pl.BlockSpec((1, Wx), lambda: (0, 0))
    tab = pl.BlockSpec((8, 8, Wx), lambda: (0, 0, 0))
    return _pcall(body, name="ssm_tables", in_specs=[full] * 3, out_specs=[tab, tab],
                  out_shape=[_sds((8, 8, Wx), F32)] * 2,
                  compiler_params=pltpu.CompilerParams(vmem_limit_bytes=V7X_VMEM_LIMIT_BYTES))(lr, li, ls)


def _scan(name, br, bi, tab, layer, reverse, states=None):
    L, Wx = br.shape
    TB = _pick(L, (384, 256, 128))
    CW = _pick(Wx, (1024, 512, 256, 128))
    nT, nG = L // TB, TB // 8

    def body(*refs):
        if reverse:
            br_ref, bi_ref, xr_ref, xi_ref, tab_ref, or_ref, oi_ref, s1_ref, s2_ref, cr_ref, ci_ref = refs
        else:
            br_ref, bi_ref, tab_ref, or_ref, oi_ref, cr_ref, ci_ref = refs

        @pl.when(pl.program_id(1) == 0)
        def _():
            cr_ref[...] = jnp.zeros_like(cr_ref)
            ci_ref[...] = jnp.zeros_like(ci_ref)
            if reverse:
                s1_ref[...] = jnp.zeros_like(s1_ref)
                s2_ref[...] = jnp.zeros_like(s2_ref)

        def step(q, carry):
            cr, ci = carry[0], carry[1]
            g = (nG - 1 - q) if reverse else q
            rows = pl.ds(pl.multiple_of(g * 8, 8), 8)
            b_r, b_i = br_ref[rows, :], bi_ref[rows, :]
            sr, si = b_r, b_i
            for k, d in enumerate(SCAN_SHIFTS):
                mr, mi = tab_ref[k], tab_ref[4 + k]
                sh = (8 - d) if reverse else d
                pr, pi = pltpu.roll(sr, sh, 0), pltpu.roll(si, sh, 0)
                sr, si = sr + mr * pr - mi * pi, si + mr * pi + mi * pr
            pwr, pwi = tab_ref[3], tab_ref[7]
            xr = sr + pwr * cr - pwi * ci
            xi = si + pwr * ci + pwi * cr
            or_ref[rows, :] = xr
            oi_ref[rows, :] = xi
            row = 0 if reverse else 7
            out = (jnp.broadcast_to(xr[row:row + 1, :], xr.shape), jnp.broadcast_to(xi[row:row + 1, :], xi.shape))
            if reverse:
                hr, hi = xr - b_r, xi - b_i
                st_r, st_i = xr_ref[rows, :], xi_ref[rows, :]
                out = out + (carry[2] + hr * st_r + hi * st_i, carry[3] + hi * st_r - hr * st_i)
            return out

        init = (cr_ref[...], ci_ref[...])
        if reverse:
            init = init + (jnp.zeros((8, CW), F32), jnp.zeros((8, CW), F32))
        fin = lax.fori_loop(0, nG, step, init, unroll=2)
        cr_ref[...] = fin[0]
        ci_ref[...] = fin[1]
        if reverse:
            s1_ref[...] += fin[2]
            s2_ref[...] += fin[3]

    tmap = (lambda j, t: (nT - 1 - t, j)) if reverse else (lambda j, t: (t, j))
    blk = pl.BlockSpec((TB, CW), tmap)
    tab_spec = pl.BlockSpec((8, 8, CW), lambda j, t: (0, 0, layer * (Wx // CW) + j))
    sum_spec = pl.BlockSpec((8, CW), lambda j, t: (0, j))
    ins = [br, bi] + (list(states) if reverse else []) + [tab]
    in_specs = [blk, blk] + ([blk, blk] if reverse else []) + [tab_spec]
    out_specs = [blk, blk] + ([sum_spec, sum_spec] if reverse else [])
    out_shape = [_sds((L, Wx), F32)] * 2 + ([_sds((8, Wx), F32)] * 2 if reverse else [])
    return _pcall(body, name=name, grid=(Wx // CW, nT), in_specs=in_specs, out_specs=out_specs, out_shape=out_shape,
                  scratch_shapes=[pltpu.VMEM((8, CW), F32), pltpu.VMEM((8, CW), F32)],
                  compiler_params=_params(("parallel", "arbitrary")))(*ins)


def _row_tile(L):
    return _pick(L, (1408, 704, 384, 128))


TILES_PER_BLOCK = 4


def _blockproj(name, src, off, w_r, w_i, layer, depth):
    L = src.shape[0]
    T = w_r.shape[0] // depth
    tm = _row_tile(L)
    wide = TILES_PER_BLOCK * LANES

    def body(s_ref, wr_ref, wi_ref, or_ref, oi_ref):
        s = s_ref[...].astype(BF16)
        for k in range(TILES_PER_BLOCK):
            cols = slice(k * LANES, (k + 1) * LANES)
            or_ref[:, cols] = jnp.dot(s, wr_ref[k], preferred_element_type=F32)
            oi_ref[:, cols] = jnp.dot(s, wi_ref[k], preferred_element_type=F32)

    w_spec = pl.BlockSpec((TILES_PER_BLOCK, LANES, LANES), lambda i, q: (layer * (T // TILES_PER_BLOCK) + q, 0, 0))
    o_spec = pl.BlockSpec((tm, wide), lambda i, q: (i, q))
    return _pcall(body, name=name, grid=(L // tm, T // TILES_PER_BLOCK),
                  in_specs=[pl.BlockSpec((tm, LANES), lambda i, q: (i, off + q)), w_spec, w_spec],
                  out_specs=[o_spec, o_spec], out_shape=[_sds((L, T * LANES), F32)] * 2,
                  compiler_params=_params(("parallel", "arbitrary")))(src, w_r, w_i)


def _blockproj_grad(name, src, off, gr, gi):
    L = src.shape[0]
    T = gr.shape[1] // LANES
    tm = _row_tile(L)
    wide = TILES_PER_BLOCK * LANES
    tn_dims = (((0,), (0,)), ((), ()))

    def body(s_ref, gr_ref, gi_ref, or_ref, oi_ref):
        @pl.when(pl.program_id(1) == 0)
        def _():
            or_ref[...] = jnp.zeros_like(or_ref)
            oi_ref[...] = jnp.zeros_like(oi_ref)

        s = s_ref[...].astype(BF16)
        for k in range(TILES_PER_BLOCK):
            cols = slice(k * LANES, (k + 1) * LANES)
            or_ref[k] += lax.dot_general(s, gr_ref[:, cols].astype(BF16), tn_dims, preferred_element_type=F32)
            oi_ref[k] += lax.dot_general(s, gi_ref[:, cols].astype(BF16), tn_dims, preferred_element_type=F32)

    g_spec = pl.BlockSpec((tm, wide), lambda q, i: (i, q))
    o_spec = pl.BlockSpec((TILES_PER_BLOCK, LANES, LANES), lambda q, i: (q, 0, 0))
    return _pcall(body, name=name, grid=(T // TILES_PER_BLOCK, L // tm),
                  in_specs=[pl.BlockSpec((tm, LANES), lambda q, i: (i, off + q)), g_spec, g_spec],
                  out_specs=[o_spec, o_spec], out_shape=[_sds((T, LANES, LANES), F32)] * 2,
                  compiler_params=_params(("parallel", "arbitrary")))(src, gr, gi)


def _gelu(y):
    k = math.sqrt(2.0 / math.pi)
    return 0.5 * y * (1.0 + jnp.tanh(k * (y + 0.044715 * (y * y * y))))


def _gelu_grad(y):
    k = math.sqrt(2.0 / math.pi)
    t = jnp.tanh(k * (y + 0.044715 * (y * y * y)))
    return 0.5 * (1.0 + t) + 0.5 * y * (1.0 - t * t) * (k * (1.0 + 3 * 0.044715 * (y * y)))


def _ssm_out(xr, xi, w_r, w_i, proj, u_off, dvec, layer, depth):
    L = xr.shape[0]
    J = w_r.shape[0] // depth
    SW = w_r.shape[1]
    tm = _row_tile(L)

    def body(xr_ref, xi_ref, wr_ref, wi_ref, u_ref, d_ref, y_ref, gl_ref):
        acc = jnp.dot(xr_ref[...].astype(BF16), wr_ref[...], preferred_element_type=F32)
        acc += jnp.dot(xi_ref[...].astype(BF16), wi_ref[...], preferred_element_type=F32)
        y = acc + d_ref[...] * u_ref[...]
        y_ref[...] = y
        gl_ref[...] = _gelu(y)

    x_spec = pl.BlockSpec((tm, SW), lambda j, i: (i, j))
    w_spec = pl.BlockSpec((None, SW, LANES), lambda j, i: (layer * J + j, 0, 0))
    o_spec = pl.BlockSpec((tm, LANES), lambda j, i: (i, j))
    return _pcall(body, name="ssm_out", grid=(J, L // tm),
                  in_specs=[x_spec, x_spec, w_spec, w_spec, pl.BlockSpec((tm, LANES), lambda j, i: (i, u_off + j)),
                            pl.BlockSpec((1, LANES), lambda j, i: (0, j))],
                  out_specs=[o_spec, o_spec], out_shape=[_sds((L, J * LANES), F32)] * 2,
                  compiler_params=_params(("parallel", "parallel")))(xr, xi, w_r, w_i, proj, dvec)


def _ssm_du(gr, gi, w_r, w_i, dy, proj, u_off, dvec, dproj, layer, depth):
    L = gr.shape[0]
    J = w_r.shape[0] // depth
    SW = w_r.shape[1]
    tm = _row_tile(L)

    def body(gr_ref, gi_ref, wr_ref, wi_ref, dy_ref, u_ref, d_ref, buf_ref, du_ref, dd_ref):
        i = pl.program_id(1)

        @pl.when(i == 0)
        def _():
            dd_ref[...] = jnp.zeros_like(dd_ref)

        acc = jnp.dot(gr_ref[...].astype(BF16), wr_ref[...], preferred_element_type=F32)
        acc += jnp.dot(gi_ref[...].astype(BF16), wi_ref[...], preferred_element_type=F32)
        dy = dy_ref[...]
        row = lax.broadcasted_iota(jnp.int32, (tm, LANES), 0) + i * tm
        du_ref[...] = jnp.where(row >= PAD, acc + d_ref[...] * dy, 0.0).astype(du_ref.dtype)
        dd_ref[...] += jnp.sum(dy * u_ref[...], axis=0, keepdims=True)

    x_spec = pl.BlockSpec((tm, SW), lambda j, i: (i, j))
    w_spec = pl.BlockSpec((None, SW, LANES), lambda j, i: (layer * J + j, 0, 0))
    o_spec = pl.BlockSpec((tm, LANES), lambda j, i: (i, j))
    u_spec = pl.BlockSpec((tm, LANES), lambda j, i: (i, u_off + j))
    vec = pl.BlockSpec((1, LANES), lambda j, i: (0, j))
    return _pcall(body, name="ssm_du", grid=(J, L // tm),
                  in_specs=[x_spec, x_spec, w_spec, w_spec, o_spec, u_spec, vec, pl.BlockSpec(memory_space=pl.ANY)],
                  out_specs=[u_spec, vec], out_shape=[_sds(dproj.shape, dproj.dtype), _sds((1, J * LANES), F32)],
                  input_output_aliases={7: 0},
                  compiler_params=_params(("parallel", "arbitrary")))(gr, gi, w_r, w_i, dy, proj, dvec, dproj)


def _ssm_dc(xr, xi, dy, SW):
    L = xr.shape[0]
    J = dy.shape[1] // LANES
    tm = _row_tile(L)
    tn_dims = (((0,), (0,)), ((), ()))

    def body(xr_ref, xi_ref, dy_ref, or_ref, oi_ref):
        @pl.when(pl.program_id(1) == 0)
        def _():
            or_ref[...] = jnp.zeros_like(or_ref)
            oi_ref[...] = jnp.zeros_like(oi_ref)

        d = dy_ref[...].astype(BF16)
        or_ref[...] += lax.dot_general(xr_ref[...].astype(BF16), d, tn_dims, preferred_element_type=F32)
        oi_ref[...] += lax.dot_general(xi_ref[...].astype(BF16), d, tn_dims, preferred_element_type=F32)

    x_spec = pl.BlockSpec((tm, SW), lambda j, i: (i, j))
    o_spec = pl.BlockSpec((None, SW, LANES), lambda j, i: (j, 0, 0))
    return _pcall(body, name="ssm_dc", grid=(J, L // tm),
                  in_specs=[x_spec, x_spec, pl.BlockSpec((tm, LANES), lambda j, i: (i, j))],
                  out_specs=[o_spec, o_spec], out_shape=[_sds((J, SW, LANES), F32)] * 2,
                  compiler_params=_params(("parallel", "arbitrary")))(xr, xi, dy)


SCAN_COLS = 1024
SCAN_TILES = SCAN_COLS // LANES
SCAN_CH = SCAN_COLS // STATE * GROUP_CH
SCAN_SLAB = 8 * STATE


def _scan_rows(b_r, b_i, tab_ref, carry, reverse):
    sr, si = b_r, b_i
    for k, d in enumerate(SCAN_SHIFTS):
        mr, mi = tab_ref[k], tab_ref[4 + k]
        sh = (8 - d) if reverse else d
        pr, pi = pltpu.roll(sr, sh, 0), pltpu.roll(si, sh, 0)
        sr, si = sr + mr * pr - mi * pi, si + mr * pi + mi * pr
    pwr, pwi = tab_ref[3], tab_ref[7]
    xr = sr + pwr * carry[0] - pwi * carry[1]
    xi = si + pwr * carry[1] + pwi * carry[0]
    row = 0 if reverse else 7
    return xr, xi, (jnp.broadcast_to(xr[row:row + 1, :], xr.shape), jnp.broadcast_to(xi[row:row + 1, :], xi.shape))


def _ssm_fwd(proj, u_off, s, dvec):
    L = proj.shape[0]
    layer, depth = s['layer'], s['depth']
    T, J = s['wb'][0].shape[0] // depth, s['wc'][0].shape[0] // depth
    Wx, nC = T * LANES, T * LANES // SCAN_COLS
    TB = _pick(L, (384, 256, 128))
    nT, nG = L // TB, TB // 8

    def body(u_ref, wbr_ref, wbi_ref, wcr_ref, wci_ref, d_ref, tab_ref, xr_ref, xi_ref, y_ref, gl_ref, cr_ref, ci_ref):
        @pl.when(pl.program_id(1) == 0)
        def _():
            cr_ref[...] = jnp.zeros_like(cr_ref)
            ci_ref[...] = jnp.zeros_like(ci_ref)

        u = u_ref[...]
        u16 = u.astype(BF16)
        for k in range(SCAN_TILES):
            blk = u16[:, (k // TILES_PER_BLOCK) * LANES:(k // TILES_PER_BLOCK + 1) * LANES]
            cols = slice(k * LANES, (k + 1) * LANES)
            xr_ref[:, cols] = jnp.dot(blk, wbr_ref[k], preferred_element_type=F32)
            xi_ref[:, cols] = jnp.dot(blk, wbi_ref[k], preferred_element_type=F32)

        def step(q, carry):
            rows = pl.ds(pl.multiple_of(q * 8, 8), 8)
            xr, xi, carry = _scan_rows(xr_ref[rows, :], xi_ref[rows, :], tab_ref, carry, False)
            xr_ref[rows, :] = xr
            xi_ref[rows, :] = xi
            return carry

        cr, ci = lax.fori_loop(0, nG, step, (cr_ref[...], ci_ref[...]), unroll=2)
        cr_ref[...] = cr
        ci_ref[...] = ci
        for c in range(SCAN_CH // LANES):
            slab, ch = slice(c * SCAN_SLAB, (c + 1) * SCAN_SLAB), slice(c * LANES, (c + 1) * LANES)
            acc = jnp.dot(xr_ref[:, slab].astype(BF16), wcr_ref[c], preferred_element_type=F32)
            acc += jnp.dot(xi_ref[:, slab].astype(BF16), wci_ref[c], preferred_element_type=F32)
            y = acc + d_ref[:, ch] * u[:, ch]
            y_ref[:, ch] = y
            gl_ref[:, ch] = _gelu(y)

    x_spec = pl.BlockSpec((TB, SCAN_COLS), lambda j, t: (t, j))
    y_spec = pl.BlockSpec((TB, SCAN_CH), lambda j, t: (t, j))
    tile_w = pl.BlockSpec((SCAN_TILES, LANES, LANES), lambda j, t: (layer * nC + j, 0, 0))
    slab_w = pl.BlockSpec((SCAN_CH // LANES, SCAN_SLAB, LANES), lambda j, t: (layer * nC + j, 0, 0))
    return _pcall(body, name="ssm_fwd", grid=(nC, nT),
                  in_specs=[pl.BlockSpec((TB, SCAN_CH), lambda j, t: (t, u_off * LANES // SCAN_CH + j)), tile_w, tile_w, slab_w, slab_w,
                            pl.BlockSpec((1, SCAN_CH), lambda j, t: (0, j)),
                            pl.BlockSpec((8, 8, SCAN_COLS), lambda j, t: (0, 0, layer * nC + j))],
                  out_specs=[x_spec, x_spec, y_spec, y_spec],
                  out_shape=[_sds((L, Wx), F32)] * 2 + [_sds((L, J * LANES), F32)] * 2,
                  scratch_shapes=[pltpu.VMEM((8, SCAN_COLS), F32)] * 2,
                  compiler_params=_params(("parallel", "arbitrary")))(proj, *s['wb'], *s['wc'], dvec, s['tf'])


def _ssm_bwd(dy, proj, u_off, xr, xi, s, dvec, dproj):
    L = dy.shape[0]
    layer, depth = s['layer'], s['depth']
    T, J = s['wb'][0].shape[0] // depth, s['wc'][0].shape[0] // depth
    Wx, nC = T * LANES, T * LANES // SCAN_COLS
    TB = _pick(L, (384, 256, 128))
    nT, nG = L // TB, TB // 8
    n_ch = SCAN_CH // LANES
    tn_dims = (((0,), (0,)), ((), ()))

    def body(dy_ref, u_ref, xr_ref, xi_ref, wcr_ref, wci_ref, wbr_ref, wbi_ref, d_ref, tab_ref, buf_ref,
             du_ref, dd_ref, dwbr_ref, dwbi_ref, dwcr_ref, dwci_ref, s1_ref, s2_ref, gr_ref, gi_ref, cr_ref, ci_ref):
        t = pl.program_id(1)

        @pl.when(t == 0)
        def _():
            for ref in (cr_ref, ci_ref, dd_ref, dwbr_ref, dwbi_ref, dwcr_ref, dwci_ref, s1_ref, s2_ref):
                ref[...] = jnp.zeros_like(ref)

        dy = dy_ref[...]
        dy16 = dy.astype(BF16)
        u = u_ref[...]
        u16 = u.astype(BF16)
        for k in range(SCAN_TILES):
            blk = dy16[:, (k // TILES_PER_BLOCK) * LANES:(k // TILES_PER_BLOCK + 1) * LANES]
            cols = slice(k * LANES, (k + 1) * LANES)
            gr_ref[:, cols] = jnp.dot(blk, wcr_ref[k], preferred_element_type=F32)
            gi_ref[:, cols] = jnp.dot(blk, wci_ref[k], preferred_element_type=F32)

        def step(q, carry):
            rows = pl.ds(pl.multiple_of((nG - 1 - q) * 8, 8), 8)
            b_r, b_i = gr_ref[rows, :], gi_ref[rows, :]
            g_r, g_i, edge = _scan_rows(b_r, b_i, tab_ref, carry[:2], True)
            gr_ref[rows, :] = g_r
            gi_ref[rows, :] = g_i
            hr, hi = g_r - b_r, g_i - b_i
            st_r, st_i = xr_ref[rows, :], xi_ref[rows, :]
            return edge + (carry[2] + hr * st_r + hi * st_i, carry[3] + hi * st_r - hr * st_i)

        zero = jnp.zeros((8, SCAN_COLS), F32)
        fin = lax.fori_loop(0, nG, step, (cr_ref[...], ci_ref[...], zero, zero), unroll=2)
        cr_ref[...] = fin[0]
        ci_ref[...] = fin[1]
        s1_ref[...] += fin[2]
        s2_ref[...] += fin[3]

        row = lax.broadcasted_iota(jnp.int32, (TB, LANES), 0) + (nT - 1 - t) * TB
        for c in range(n_ch):
            slab, ch = slice(c * SCAN_SLAB, (c + 1) * SCAN_SLAB), slice(c * LANES, (c + 1) * LANES)
            g16r, g16i = gr_ref[:, slab].astype(BF16), gi_ref[:, slab].astype(BF16)
            acc = jnp.dot(g16r, wbr_ref[c], preferred_element_type=F32) + jnp.dot(g16i, wbi_ref[c], preferred_element_type=F32)
            du_ref[:, ch] = jnp.where(row >= PAD, acc + d_ref[:, ch] * dy[:, ch], 0.0).astype(du_ref.dtype)
            dwcr_ref[c] += lax.dot_general(xr_ref[:, slab].astype(BF16), dy16[:, ch], tn_dims, preferred_element_type=F32)
            dwci_ref[c] += lax.dot_general(xi_ref[:, slab].astype(BF16), dy16[:, ch], tn_dims, preferred_element_type=F32)
            for kk in range(TILES_PER_BLOCK):
                k = c * TILES_PER_BLOCK + kk
                cols = slice(kk * LANES, (kk + 1) * LANES)
                dwbr_ref[k] += lax.dot_general(u16[:, ch], g16r[:, cols], tn_dims, preferred_element_type=F32)
                dwbi_ref[k] += lax.dot_general(u16[:, ch], g16i[:, cols], tn_dims, preferred_element_type=F32)
        dd_ref[...] += jnp.sum(dy * u, axis=0, keepdims=True)

    rev = lambda j, t: (nT - 1 - t, j)
    x_spec = pl.BlockSpec((TB, SCAN_COLS), rev)
    u_spec = pl.BlockSpec((TB, SCAN_CH), lambda j, t: (nT - 1 - t, u_off * LANES // SCAN_CH + j))
    tile_w = pl.BlockSpec((SCAN_TILES, LANES, LANES), lambda j, t: (layer * nC + j, 0, 0))
    slab_w = pl.BlockSpec((n_ch, SCAN_SLAB, LANES), lambda j, t: (layer * nC + j, 0, 0))
    tile_g = pl.BlockSpec((SCAN_TILES, LANES, LANES), lambda j, t: (j, 0, 0))
    slab_g = pl.BlockSpec((n_ch, SCAN_SLAB, LANES), lambda j, t: (j, 0, 0))
    vec = pl.BlockSpec((1, SCAN_CH), lambda j, t: (0, j))
    sums = pl.BlockSpec((8, SCAN_COLS), lambda j, t: (0, j))
    return _pcall(body, name="ssm_bwd", grid=(nC, nT),
                  in_specs=[pl.BlockSpec((TB, SCAN_CH), rev), u_spec, x_spec, x_spec, tile_w, tile_w, slab_w, slab_w, vec,
                            pl.BlockSpec((8, 8, SCAN_COLS), lambda j, t: (0, 0, layer * nC + j)), pl.BlockSpec(memory_space=pl.ANY)],
                  out_specs=[u_spec, vec, tile_g, tile_g, slab_g, slab_g, sums, sums],
                  out_shape=[_sds(dproj.shape, dproj.dtype), _sds((1, J * LANES), F32), _sds((T, LANES, LANES), F32),
                             _sds((T, LANES, LANES), F32), _sds((J, SCAN_SLAB, LANES), F32), _sds((J, SCAN_SLAB, LANES), F32),
                             _sds((8, Wx), F32), _sds((8, Wx), F32)],
                  input_output_aliases={10: 0},
                  scratch_shapes=[pltpu.VMEM((TB, SCAN_COLS), F32)] * 2 + [pltpu.VMEM((8, SCAN_COLS), F32)] * 2,
                  compiler_params=_params(("parallel", "arbitrary")))(dy, proj, xr, xi, *s['wcT'], *s['wbT'], dvec, s['tr'], dproj)


def _glu_dz(ds, gl, z):
    L, W = ds.shape
    tr = _pick(L, (384, 256, 128))

    def body(ds_ref, gl_ref, z_ref, dz_ref, db_ref):
        @pl.when(pl.program_id(0) == 0)
        def _():
            db_ref[...] = jnp.zeros_like(db_ref)

        sg = jax.nn.sigmoid(z_ref[...])
        dz = ds_ref[...] * gl_ref[...] * (sg * (1.0 - sg))
        dz_ref[...] = dz.astype(BF16)
        db_ref[...] += jnp.sum(dz, axis=0, keepdims=True)

    spec = pl.BlockSpec((tr, W), lambda i: (i, 0))
    vec = pl.BlockSpec((1, W), lambda i: (0, 0))
    return _pcall(body, name="glu_dz", grid=(L // tr,), in_specs=[spec] * 3, out_specs=[spec, vec],
                  out_shape=[_sds((L, W), BF16), _sds((1, W), F32)], compiler_params=_params(("arbitrary",)))(ds, gl, z)


def _ssm_param_bwd_flat(lr, li, ls, br, bi, dbbr, dbbi):
    def seg_sum(x):
        for s in (8, 4, 2, 1):
            x = x + pltpu.roll(x, LANES - s, 1)
        return x

    def fn(lr, li, ls, br, bi, dbbr, dbbi):
        fr, fi = _zoh_factor(lr, li, ls)
        return (fr * dbbr + fi * dbbi, fr * dbbi - fi * dbbr,
                seg_sum(br * dbbr + bi * dbbi), seg_sum(br * dbbi - bi * dbbr))

    return _ew("ssm_param_bwd_flat", fn, [lr, li, ls, br, bi, dbbr, dbbi], [F32] * 4)


def _ssm_param_bwd(lr, li, ls, dfr, dfi, s1, s2):
    G, P = lr.shape

    def body(lr_ref, li_ref, ls_ref, dfr_ref, dfi_ref, s1_ref, s2_ref, dlr_ref, dli_ref, dls_ref):
        lr, li = lr_ref[...], li_ref[...]
        ar, ai, dl = _lam_bar(lr, li, ls_ref[...])
        sr, si = s1_ref[0], s2_ref[0]
        for k in range(1, 8):
            sr = sr + s1_ref[k]
            si = si + s2_ref[k]
        a2 = ar * ar + ai * ai
        gar, gai = (sr * ar - si * ai) / a2, (sr * ai + si * ar) / a2
        n2 = lr * lr + li * li
        ivr, ivi = lr / n2, -li / n2
        fr = (ar - 1.0) * ivr - ai * ivi
        fi = (ar - 1.0) * ivi + ai * ivr
        dfr, dfi = dfr_ref[...], dfi_ref[...]
        gar = gar + ivr * dfr + ivi * dfi
        gai = gai + ivr * dfi - ivi * dfr
        wr, wi = -(fr * ivr - fi * ivi), -(fr * ivi + fi * ivr)
        glr, gli = wr * dfr + wi * dfi, wr * dfi - wi * dfr
        gzr, gzi = ar * gar + ai * gai, ar * gai - ai * gar
        dlr_ref[...] = glr + dl * gzr
        dli_ref[...] = gli + dl * gzi
        dls_ref[...] = dl * jnp.sum(lr * gzr + li * gzi, axis=-1, keepdims=True)

    m = pl.BlockSpec((G, P), lambda: (0, 0))
    v = pl.BlockSpec((G, 1), lambda: (0, 0))
    s = pl.BlockSpec((8, G, P), lambda: (0, 0, 0))
    return _pcall(body, name="ssm_param_bwd", in_specs=[m, m, v, m, m, s, s], out_specs=[m, m, v],
                  out_shape=[_sds((G, P), F32), _sds((G, P), F32), _sds((G, 1), F32)])(lr, li, ls, dfr, dfi, s1, s2)


def _tile_mask(G):
    T = G // 2
    e = np.zeros((T, 8, 1, 2, 1), np.float32)
    for t in range(T):
        for c in range(2):
            e[t, (2 * t + c) % 8, 0, c, 0] = 1.0
    return e


def _tile_w(arr):
    G = arr.shape[0]
    a = arr.reshape(G // 2, 1, 2, STATE, GROUP_CH).transpose(0, 1, 4, 2, 3)
    return (a * _tile_mask(G)).reshape(G // 2, LANES, LANES).astype(BF16)


def _tile_w_grad(dw):
    G = dw.shape[0] * 2
    d = dw.reshape(G // 2, 8, GROUP_CH, 2, STATE) * _tile_mask(G)
    return d.sum(axis=1).transpose(0, 2, 3, 1).reshape(G, STATE, GROUP_CH)


def _slab_w(arr):
    G = arr.shape[0]
    a = arr.reshape(G // 8, 8, STATE, 1, GROUP_CH)
    eye = np.eye(8, dtype=np.float32).reshape(1, 8, 1, 8, 1)
    return (a * eye).reshape(G // 8, 8 * STATE, LANES).astype(BF16)


def _slab_w_grad(dw):
    J = dw.shape[0]
    eye = np.eye(8, dtype=np.float32).reshape(1, 8, 1, 8, 1)
    return (dw.reshape(J, 8, STATE, 8, GROUP_CH) * eye).sum(axis=3).reshape(J * 8, STATE, GROUP_CH)


def _exchange(name, ins, out_sds, remote, local, aliases=None):
    n_in, n_out, n_r, n_l = len(ins), len(out_sds), len(remote), len(local)

    def body(*refs):
        in_refs, out_refs = refs[:n_in], refs[n_in:n_in + n_out]
        send_sems, recv_sems, local_sems = refs[n_in + n_out:]
        x, y, c = lax.axis_index("x"), lax.axis_index("y"), lax.axis_index("c")

        def place(px, py, pc):
            return dict(x=px, y=py, c=pc, chip=2 * px + py)

        def flip(mask):
            mx, my, mc = mask
            return ((1 - x) if mx else x, (1 - y) if my else y, (1 - c) if mc else c)

        me = place(x, y, c)
        sends = []
        for k, (ii, src, oi, dst, mask) in enumerate(remote):
            cp = pltpu.make_async_remote_copy(src_ref=src(in_refs[ii], me), dst_ref=dst(out_refs[oi], me),
                                              send_sem=send_sems.at[k], recv_sem=recv_sems.at[k],
                                              device_id=flip(mask), device_id_type=MESH)
            cp.start()
            sends.append(cp)
        locals_ = []
        for k, (ii, src, oi, dst) in enumerate(local):
            cp = pltpu.make_async_copy(src(in_refs[ii], me), dst(out_refs[oi], me), local_sems.at[k])
            cp.start()
            locals_.append(cp)
        for k, (ii, src, oi, dst, mask) in enumerate(remote):
            sends[k].wait_send()
            peer = flip(mask)
            pltpu.make_async_remote_copy(src_ref=src(in_refs[ii], me), dst_ref=dst(out_refs[oi], place(*peer)),
                                         send_sem=send_sems.at[k], recv_sem=recv_sems.at[k],
                                         device_id=peer, device_id_type=MESH).wait_recv()
        for cp in locals_:
            cp.wait()

    any_spec = pl.BlockSpec(memory_space=pl.ANY)
    return _pcall(body, name=name, in_specs=[any_spec] * n_in, out_specs=[any_spec] * n_out, out_shape=list(out_sds),
                  input_output_aliases=aliases or {},
                  scratch_shapes=[pltpu.SemaphoreType.DMA((n_r,)), pltpu.SemaphoreType.DMA((n_r,)),
                                  pltpu.SemaphoreType.DMA((max(n_l, 1),))])(*ins)


def _mesh_place():
    x, y, c = lax.axis_index("x"), lax.axis_index("y"), lax.axis_index("c")

    def place(px, py, pc):
        return dict(x=px, y=py, c=pc, chip=2 * px + py)

    def flip(mask):
        mx, my, mc = mask
        return ((1 - x) if mx else x, (1 - y) if my else y, (1 - c) if mc else c)

    return place(x, y, c), place, flip


_HBM = pl.BlockSpec(memory_space=pltpu.HBM)
_SEM = pl.BlockSpec(memory_space=pltpu.SEMAPHORE)
_EFFECT = pltpu.SideEffectType.DATAFLOW_SIDE_EFFECTING


def _split_start(name, bufs, groups):
    n, ng = len(bufs), len(groups)

    def body(*refs):
        in_refs, sems, token = refs[:n], refs[n:n + 2 * ng], refs[-1]
        me, _, flip = _mesh_place()
        for g, copies in enumerate(groups):
            for k, (si, src, di, dst, mask) in enumerate(copies):
                pltpu.make_async_remote_copy(src_ref=src(in_refs[si], me), dst_ref=dst(in_refs[di], me),
                                             send_sem=sems[2 * g].at[k], recv_sem=sems[2 * g + 1].at[k],
                                             device_id=flip(mask), device_id_type=MESH).start()
        token[...] = jnp.zeros_like(token)

    outs = _pcall(body, name=name,
                  out_shape=(*[pltpu.SemaphoreType.DMA((len(g),)) for g in groups for _ in range(2)],
                             *[pltpu.HBM(b.shape, b.dtype) for b in bufs], _sds((8, LANES), F32)),
                  in_specs=[_HBM] * n, out_specs=(*[_SEM] * (2 * ng), *[_HBM] * n, pl.BlockSpec(memory_space=pltpu.VMEM)),
                  input_output_aliases={i: 2 * ng + i for i in range(n)},
                  compiler_params=pltpu.CompilerParams(has_side_effects=_EFFECT),
                  )(*[pltpu.with_memory_space_constraint(b, pltpu.HBM) for b in bufs])
    return [(outs[2 * g], outs[2 * g + 1]) for g in range(ng)], list(outs[2 * ng:2 * ng + n]), outs[-1]


def _split_wait(name, bufs, sems, after, remote):
    n = len(bufs)
    send_sems, recv_sems = sems

    def body(*refs):
        in_refs, ssem, rsem = refs[:n], refs[n], refs[n + 1]
        me, place, flip = _mesh_place()
        for k, (si, src, di, dst, mask) in enumerate(remote):
            peer = flip(mask)
            cp = pltpu.make_async_remote_copy(src_ref=src(in_refs[si], me), dst_ref=dst(in_refs[di], place(*peer)),
                                              send_sem=ssem.at[k], recv_sem=rsem.at[k], device_id=peer, device_id_type=MESH)
            cp.wait_send()
            cp.wait_recv()

    return list(_pcall(body, name=name, out_shape=tuple(pltpu.HBM(b.shape, b.dtype) for b in bufs),
                       in_specs=[_HBM] * n + [_SEM, _SEM, pl.BlockSpec(memory_space=pl.ANY)], out_specs=tuple([_HBM] * n),
                       input_output_aliases={i: i for i in range(n)},
                       compiler_params=pltpu.CompilerParams(has_side_effects=_EFFECT))(*bufs, send_sems, recv_sems, after))


CHIP_MASKS = ((0, 1, 0), (1, 0, 0), (1, 1, 0))
SIBLING = (0, 0, 1)


def _whole(ref, p):
    return ref


def _all_gather(name, shards, col_sharded):
    def dst_view(col):
        def view(ref, p):
            r, cdim = ref.shape[0] // (1 if col else N_CHIPS), ref.shape[1] // (N_CHIPS if col else 1)
            if col:
                return ref.at[:, pl.ds(pl.multiple_of(p["chip"] * cdim, LANES), cdim)]
            return ref.at[pl.ds(pl.multiple_of(p["chip"] * r, 8), r), :]
        return view

    out_sds = [_sds((s.shape[0], s.shape[1] * N_CHIPS) if col else (s.shape[0] * N_CHIPS, s.shape[1]), s.dtype)
               for s, col in zip(shards, col_sharded)]
    remote = [(a, _whole, a, dst_view(col), m) for a, col in enumerate(col_sharded) for m in CHIP_MASKS]
    local = [(a, _whole, a, dst_view(col)) for a, col in enumerate(col_sharded)]
    return _exchange(name, shards, out_sds, remote, local)


class _Place:
    def __getitem__(self, k):
        return lax.axis_index("c") if k == 0 else 2 * lax.axis_index("x") + lax.axis_index("y")


def _placed_call(body, name, grid, in_specs, out_specs, out_shape, sem, ins):
    def wrap(spec):
        return pl.BlockSpec(spec.block_shape, lambda *idx: spec.index_map(*idx, _Place()))

    outs = [wrap(s) for s in out_specs] if isinstance(out_specs, (list, tuple)) else wrap(out_specs)
    return _pcall(body, name=name, grid=grid, in_specs=[wrap(s) for s in in_specs], out_specs=outs, out_shape=out_shape,
                  compiler_params=_params(sem))(*ins)


def _rows_within(n, width, limit=512 * 1024):
    return _pick(n, tuple(t for t in (1024, 512, 256, 128, 64, 32, 16) if t * width <= limit) or (16,))


def _region_view(col):
    def view(ref, p):
        if col:
            cdim = ref.shape[1] // N_CHIPS
            return ref.at[:, pl.ds(pl.multiple_of(p["chip"] * cdim, LANES), cdim)]
        r = ref.shape[0] // N_CHIPS
        return ref.at[pl.ds(pl.multiple_of(p["chip"] * r, 16), r), :]
    return view


def _ag_place(name, w, layer, col, dtype):
    _, r, cdim = w.shape
    tr = _rows_within(r, cdim)
    nb = r // tr

    def body(w_ref, o_ref):
        o_ref[...] = w_ref[...].astype(dtype)

    if col:
        out_shape, out_spec = (r, N_CHIPS * cdim), pl.BlockSpec((tr, cdim), lambda i, pr: (i, pr[1]))
    else:
        out_shape, out_spec = (N_CHIPS * r, cdim), pl.BlockSpec((tr, cdim), lambda i, pr: (pr[1] * nb + i, 0))
    return _placed_call(body, name, (nb,), [pl.BlockSpec((None, tr, cdim), lambda i, pr: (layer, i, 0))], out_spec,
                        _sds(out_shape, dtype), ("parallel",), [w])


def _ag_copies(a, col):
    return [(a, _region_view(col), a, _region_view(col), m) for m in CHIP_MASKS]


def _rs_add2(name, g4, a4, out_dtype):
    J, _, h, C = g4.shape
    tr = _rows_within(h, C)

    def body(g_ref, a_ref, o_ref):
        o_ref[...] = (g_ref[...].astype(F32) + a_ref[...].astype(F32)).astype(o_ref.dtype)

    return _placed_call(body, name, (J, h // tr),
                        [pl.BlockSpec((None, None, tr, C), lambda j, i, pr: (j, pr[0], i, 0)),
                         pl.BlockSpec((None, None, tr, C), lambda j, i, pr: (j, 0, i, 0))],
                        pl.BlockSpec((None, tr, C), lambda j, i, pr: (j, i, 0)), _sds((J, h, C), out_dtype),
                        ("parallel", "parallel"), [g4, a4])


def _rs_add4(name, p3, landed, col):
    _, h, w = landed.shape
    tr = _rows_within(h, w)

    def body(p_ref, a_ref, b_ref, c_ref, o_ref):
        o_ref[...] = ((p_ref[...].astype(F32) + a_ref[...].astype(F32)) + b_ref[...].astype(F32)) + c_ref[...].astype(F32)

    own = (pl.BlockSpec((None, tr, w), lambda i, pr: (0, i, pr[1])) if col else pl.BlockSpec((None, tr, w), lambda i, pr: (pr[1], i, 0)))
    slot = lambda k: pl.BlockSpec((None, tr, w), lambda i, pr: (k, i, 0))
    return _placed_call(body, name, (h // tr,), [own, slot(0), slot(1), slot(2)], pl.BlockSpec((tr, w), lambda i, pr: (i, 0)),
                        _sds((h, w), F32), ("parallel",), [p3, landed, landed, landed])


def _rs_start(tag, grads, col_sharded):
    n = len(grads)
    g4 = [g.reshape((1, 2, g.shape[0] // 2, g.shape[1]) if col else (N_CHIPS, 2, g.shape[0] // (2 * N_CHIPS), g.shape[1]))
          for g, col in zip(grads, col_sharded)]
    other_half = lambda ref, p: ref.at[:, pl.ds(1 - p["c"], 1)]
    landing = [lax.empty((g.shape[0], 1) + g.shape[2:], g.dtype) for g in g4]
    copies = [(a, other_half, n + a, _whole, SIBLING) for a in range(n)]
    (sems,), bufs, token = _split_start("rs_sibling_start_" + tag, g4 + landing, [copies])
    return dict(tag=tag, stage=0, sems=sems, bufs=bufs, copies=copies, token=token, col_sharded=list(col_sharded))


def _rs_advance(st, after):
    col_sharded, tag = st['col_sharded'], st['tag']
    n = len(col_sharded)
    stage = st['stage']
    st['stage'] = stage + 1
    if stage == 0:
        bufs = _split_wait("rs_sibling_wait_" + tag, st['bufs'], st['sems'], after, st['copies'])
        st.update(_rs_chips_start(tag, [_rs_add2("rs_add2_w", bufs[a], bufs[n + a], BF16) for a in range(n)], col_sharded))
    elif stage == 1:
        bufs = _split_wait("rs_chips_wait_" + tag, st['bufs'], st['sems'], after, st['copies'])
        mine = [_rs_add4("rs_add4_w", bufs[a], bufs[n + a], col_sharded[a]) for a in range(n)]
        copies = [(a, _whole, n + a, _whole, SIBLING) for a in range(n)]
        (sems,), bufs, token = _split_start("rs_halves_start_" + tag, mine + [lax.empty(m.shape, F32) for m in mine], [copies])
        st.update(sems=sems, bufs=bufs, copies=copies, token=token)
    else:
        bufs = _split_wait("rs_halves_wait_" + tag, st['bufs'], st['sems'], after, st['copies'])
        st['result'] = (bufs[:n], bufs[n:])


def _rs_chips_start(tag, chip_sum, col_sharded):
    n = len(chip_sum)

    def send_view(col, mask):
        def view(ref, p):
            t = 2 * ((1 - p["x"]) if mask[0] else p["x"]) + ((1 - p["y"]) if mask[1] else p["y"])
            if col:
                sc = ref.shape[2] // N_CHIPS
                return ref.at[0, :, pl.ds(pl.multiple_of(t * sc, LANES), sc)]
            return ref.at[t]
        return view
    slot = lambda k: (lambda ref, p: ref.at[k])
    piece = [(s.shape[1], s.shape[2] // N_CHIPS if col else s.shape[2]) for s, col in zip(chip_sum, col_sharded)]
    landing = [lax.empty((len(CHIP_MASKS),) + s, BF16) for s in piece]
    copies = [(a, send_view(col_sharded[a], m), n + a, slot(k), m) for a in range(n) for k, m in enumerate(CHIP_MASKS)]
    (sems,), bufs, token = _split_start("rs_chips_start_" + tag, chip_sum + landing, [copies])
    return dict(sems=sems, bufs=bufs, copies=copies, token=token)


def _adamw_big(name, mine, other, w, m, v):
    depth, R, C = w.shape
    h = R // 2
    tr = _pick(h, tuple(t for t in (512, 256, 128, 64, 32, 16, 8) if t * C <= 256 * 1024) or (8,))
    nb = h // tr

    def g_spec(kk, hh):
        def imap(l, s, i, pr):
            before = (l < kk) | ((l == kk) & (s < hh))
            return (jnp.where((l == kk) & (s == hh), i, jnp.where(before, 0, nb - 1)), 0)
        return pl.BlockSpec((tr, C), imap)

    st_spec = pl.BlockSpec((None, tr, C), lambda l, s, i, pr: (l, jnp.where(s == 0, pr[0], 1 - pr[0]) * nb + i, 0))

    def body(*refs):
        g_refs = refs[:2 * depth]
        w_ref, m_ref, v_ref, go_ref, d_ref, mo_ref, vo_ref = refs[2 * depth:]
        l, s = pl.program_id(0), pl.program_id(1)
        for kk in range(depth):
            for hh in range(2):
                @pl.when((l == kk) & (s == hh))
                def _(kk=kk, hh=hh):
                    g = g_refs[2 * kk + hh][...]
                    d, mn, vn = _adam_math(w_ref[...], g, m_ref[...], v_ref[...])
                    go_ref[...] = g
                    d_ref[...] = d
                    mo_ref[...] = mn
                    vo_ref[...] = vn

    gs, g_specs = [], []
    for kk in range(depth):
        gs += [mine[kk], other[kk]]
        g_specs += [g_spec(kk, 0), g_spec(kk, 1)]
    return _placed_call(body, name, (depth, 2, nb), g_specs + [st_spec] * 3, [st_spec] * 4, [_sds(w.shape, F32)] * 4,
                        ("arbitrary", "arbitrary", "arbitrary"), gs + [w, m, v])


def _piece_view(col, j, other):
    def view(ref, p):
        R, C = ref.shape
        cc = (1 - p["c"]) if other else p["c"]
        if col:
            hr, sc = R // 2, C // N_CHIPS
            return ref.at[pl.ds(pl.multiple_of(cc * hr, 16), hr), pl.ds(j * sc, sc)]
        hr = R // (2 * N_CHIPS)
        return ref.at[pl.ds(pl.multiple_of((2 * j + cc) * hr, 8), hr), :]
    return view


def _piece_shape(shape, col):
    R, C = shape
    return (R // 2, C // N_CHIPS) if col else (R // (2 * N_CHIPS), C)


def _reduce_scatter(tag, grads, col_sharded, wire_dtype):
    n = len(grads)
    shapes = [_piece_shape(g.shape, col) for g, col in zip(grads, col_sharded)]

    slot = lambda j: (lambda ref, p: ref.at[j])
    remote = [(a, _piece_view(col_sharded[a], j, True), a, slot(j), SIBLING) for a in range(n) for j in range(N_CHIPS)]
    local = [(a, _piece_view(col_sharded[a], j, False), n + a, slot(j)) for a in range(n) for j in range(N_CHIPS)]
    got = _exchange("rs_sibling_" + tag, grads, [_sds((N_CHIPS,) + s, g.dtype) for s, g in zip(shapes, grads)] * 2, remote, local)
    theirs, mine = got[:n], got[n:]
    chip_sum = [_ew("rs_add2_" + tag, lambda a, b: (a.astype(F32) + b.astype(F32),),
                    [m.reshape(-1, m.shape[-1]), t.reshape(-1, t.shape[-1])], [wire_dtype])[0].reshape(m.shape)
                for m, t in zip(mine, theirs)]

    def send_view(mask):
        return lambda ref, p: ref.at[2 * ((1 - p["x"]) if mask[0] else p["x"]) + ((1 - p["y"]) if mask[1] else p["y"])]
    remote = [(a, send_view(m), a, slot(k), m) for a in range(n) for k, m in enumerate(CHIP_MASKS)]
    local = [(a, lambda ref, p: ref.at[p["chip"]], n + a, _whole) for a in range(n)]
    got = _exchange("rs_chips_" + tag, chip_sum,
                    [_sds((len(CHIP_MASKS),) + s, wire_dtype) for s in shapes] + [_sds(s, wire_dtype) for s in shapes], remote, local)
    landed, own = got[:n], got[n:]
    half = [_ew("rs_add4_" + tag, lambda o, a, b, c: (((o.astype(F32) + a.astype(F32)) + b.astype(F32)) + c.astype(F32),),
                [o, l[0], l[1], l[2]], [F32])[0] for o, l in zip(own, landed)]

    def half_rows(ref, p):
        hr = ref.shape[0] // 2
        return ref.at[pl.ds(pl.multiple_of(p["c"] * hr, 8), hr), :]
    remote = [(a, _whole, a, half_rows, SIBLING) for a in range(n)]
    local = [(a, _whole, a, half_rows) for a in range(n)]
    return _exchange("rs_halves_" + tag, half, [_sds((2 * s[0], s[1]), F32) for s in shapes], remote, local)


def _ssm_prepare(W):
    lr, li, ls = W['ssm_lambda_re'], W['ssm_lambda_im'], W['ssm_log_step']
    depth, G = ls.shape
    GG = depth * G
    flat = lambda a: a.reshape(-1, LANES)
    bc = lambda a: flat(jnp.broadcast_to(a, (depth, G, STATE, GROUP_CH)))
    flat3 = (bc(lr[..., None]), bc(li[..., None]), bc(ls[:, :, None, None]))
    bbr, bbi = _ssm_bbar(*flat3, flat(W['ssm_b_re']), flat(W['ssm_b_im']))
    bbr, bbi = bbr.reshape(GG, STATE, GROUP_CH), bbi.reshape(GG, STATE, GROUP_CH)
    row = lambda a: a.reshape(1, GG * STATE)
    tf, tr = _ssm_tables(row(lr), row(li), row(jnp.broadcast_to(ls[..., None], (depth, G, STATE))))
    cr = W['ssm_c_re'].reshape(GG, GROUP_CH, STATE).transpose(0, 2, 1)
    ci = -W['ssm_c_im'].reshape(GG, GROUP_CH, STATE).transpose(0, 2, 1)
    stacked = dict(wb=(_tile_w(bbr), _tile_w(bbi)), wbT=(_slab_w(bbr), _slab_w(bbi)),
                   wc=(_slab_w(cr), _slab_w(ci)), wcT=(_tile_w(cr), _tile_w(ci)))
    return flat3, [dict(stacked, tf=tf, tr=tr, layer=l, depth=depth) for l in range(depth)]


def _ssm_param_grads(W, flat3, raw):
    depth, G = W['ssm_log_step'].shape
    GG = depth * G
    cat = lambda k: jnp.concatenate([r[k] for r in raw], axis=0)
    flat = lambda a: a.reshape(-1, LANES)
    out = {}
    out['ssm_c_re'] = _slab_w_grad(cat(2)).transpose(0, 2, 1).reshape(W['ssm_c_re'].shape)
    out['ssm_c_im'] = -_slab_w_grad(cat(3)).transpose(0, 2, 1).reshape(W['ssm_c_im'].shape)
    dbr, dbi, qr, qi = _ssm_param_bwd_flat(*flat3, flat(W['ssm_b_re']), flat(W['ssm_b_im']),
                                           flat(_tile_w_grad(cat(0))), flat(_tile_w_grad(cat(1))))
    out['ssm_b_re'], out['ssm_b_im'] = dbr.reshape(W['ssm_b_re'].shape), dbi.reshape(W['ssm_b_im'].shape)
    pick = lambda q: q[:, ::GROUP_CH].reshape(GG, STATE)
    sums = lambda k: jnp.concatenate([r[k].reshape(8, G, STATE) for r in raw], axis=1)
    dlr, dli, dls = _ssm_param_bwd(W['ssm_lambda_re'].reshape(GG, STATE), W['ssm_lambda_im'].reshape(GG, STATE),
                                   W['ssm_log_step'].reshape(GG, 1), pick(qr), pick(qi), sums(4), sums(5))
    out['ssm_lambda_re'], out['ssm_lambda_im'] = dlr.reshape(depth, G, STATE), dli.reshape(depth, G, STATE)
    out['ssm_log_step'] = dls.reshape(depth, G)
    return out


def _layer_fwd(x, p, weight, dims):
    attn_w, kv_w, u_off = dims['attn_w'], dims['kv_w'], dims['u_off']
    s = p['s5']
    h = _rms_fwd("norm_mix", [x], [p['norm_mix_g']], BF16)
    w = {'w_in': weight('w_in', h)}
    proj, = _mm("mm_in", h, w['w_in'], 'nn', [F32])
    attn = _attn_fwd(proj, p['q_norm_g'], p['k_norm_g'], p['attn_sinks'], attn_w, kv_w)
    xr, xi, y, gl = _ssm_fwd(proj, u_off, s, p['ssm_d'])
    w['w_glu'] = weight('w_glu', gl)
    ssm, z = _mm("mm_glu", gl, w['w_glu'], 'nn', [F32, F32], extras=[('row', p['b_glu']), ('tile', gl)],
                 epi=lambda acc, b, g: ((lambda zz: (g * jax.nn.sigmoid(zz), zz))(acc + b)))
    mix = _rms_fwd("norm_heads", [attn, ssm], [p['attn_out_g'], p['ssm_out_g']], BF16)
    w['w_out'] = weight('w_out', mix)
    x_mid, = _mm("mm_out", mix, w['w_out'], 'nn', [F32], extras=[('tile', x)], epi=lambda acc, r: (acc + r,))
    h2 = _rms_fwd("norm_mlp", [x_mid], [p['norm_mlp_g']], BF16)
    w['w_up'] = weight('w_up', h2)
    a, r = _mm("mm_up", h2, w['w_up'], 'nn', [F32, BF16],
               epi=lambda acc: (acc, jnp.square(jnp.maximum(acc, 0.0))))
    w['w_down'] = weight('w_down', r)
    x_out, = _mm("mm_down", r, w['w_down'], 'nn', [F32], extras=[('tile', x_mid)], epi=lambda acc, rr: (acc + rr,))
    saved = dict(x=x, h=h, proj=proj, attn=attn, xr=xr, xi=xi, y=y, gl=gl, z=z, ssm=ssm, mix=mix, x_mid=x_mid, h2=h2, a=a, r=r, w=w)
    return x_out, saved


def _layer_bwd(dx, dx16, sv, p, dims, reduce_grads, tick, token_in):
    attn_w, kv_w, u_off = dims['attn_w'], dims['kv_w'], dims['u_off']
    s, w = p['s5'], sv['w']
    gb, gs = {}, {}
    da, = _mm("mm_down_dx", dx16, w['w_down'], 'nt', [BF16], extras=[('tile', sv['a'])],
              epi=lambda acc, a: (acc * (2.0 * jnp.maximum(a, 0.0)),))
    gb['w_down'], = _mm("mm_down_dw", sv['r'], dx16, 'tn', [BF16])
    dh2, = _mm("mm_up_dx", da, w['w_up'], 'nt', [F32])
    gb['w_up'], = _mm("mm_up_dw", sv['h2'], da, 'tn', [BF16])
    token = reduce_grads(('w_up', 'w_down'), [gb['w_up'], gb['w_down']]) + token_in
    (dx_mid,), (gs['norm_mlp_g'],), dx_mid16 = _rms_bwd("norm_mlp_bwd", [sv['x_mid']], [p['norm_mlp_g'] + token], dh2, resid=dx)
    dmix, = _mm("mm_out_dx", dx_mid16, w['w_out'], 'nt', [F32])
    gb['w_out'], = _mm("mm_out_dw", sv['mix'], dx_mid16, 'tn', [BF16])
    token = tick(dmix)
    (dattn, dssm), (gs['attn_out_g'], gs['ssm_out_g']) = _rms_bwd(
        "norm_heads_bwd", [sv['attn'], sv['ssm']], [p['attn_out_g'] + token, p['ssm_out_g']], dmix)
    dz, gs['b_glu'] = _glu_dz(dssm, sv['gl'], sv['z'])
    dy, = _mm("mm_glu_dx", dz, w['w_glu'], 'nt', [F32], extras=[('tile', dssm), ('tile', sv['z']), ('tile', sv['y'])],
              epi=lambda acc, ds, z, y: ((acc + ds * jax.nn.sigmoid(z)) * _gelu_grad(y),))
    gb['w_glu'], = _mm("mm_glu_dw", sv['gl'], dz, 'tn', [BF16])
    token = tick(dy)
    dproj, dkn, dv, gs['q_norm_g'], gs['attn_sinks'] = _attn_bwd(sv['proj'], sv['attn'], dattn, p['q_norm_g'] + token, p['k_norm_g'],
                                                                  p['attn_sinks'], attn_w, kv_w)
    dproj, gs['k_norm_g'] = _knorm_bwd(sv['proj'], dkn, dv, p['k_norm_g'], dproj, attn_w, kv_w)
    dproj, gs['ssm_d'], *gs['s5_raw'] = _ssm_bwd(dy, sv['proj'], u_off, sv['xr'], sv['xi'], s, p['ssm_d'], dproj)
    dh, = _mm("mm_in_dx", dproj, w['w_in'], 'nt', [F32])
    gb['w_in'], = _mm("mm_in_dw", sv['h'], dproj, 'tn', [BF16])
    token = reduce_grads(('w_in', 'w_glu', 'w_out'), [gb['w_in'], gb['w_glu'], gb['w_out']])
    (dx_in,), (gs['norm_mix_g'],), dx_in16 = _rms_bwd("norm_mix_bwd", [sv['x']], [p['norm_mix_g'] + token], dh, resid=dx_mid)
    return dx_in, dx_in16, gs


PACK_COLS = 1024


def _pack(arrs, rows):
    flat = jnp.concatenate([a.reshape(-1).astype(F32) for a in arrs])
    return jnp.pad(flat, (0, rows * PACK_COLS - flat.shape[0])).reshape(rows, PACK_COLS)


def _unpack(packed, shapes):
    flat = packed.reshape(-1)
    out, off = [], 0
    for s in shapes:
        n = int(np.prod(s))
        out.append(flat[off:off + n].reshape(s))
        off += n
    return out


def _pack_rows(shapes, multiple):
    n = sum(int(np.prod(s)) for s in shapes)
    rows = -(-n // PACK_COLS)
    return -(-rows // multiple) * multiple


def kernel(x, meta_tokens, norm_mix_g, w_in, q_norm_g, k_norm_g, attn_sinks, ssm_lambda_re, ssm_lambda_im, ssm_log_step, ssm_b_re, ssm_b_im, ssm_c_re, ssm_c_im, ssm_d, w_glu, b_glu, attn_out_g, ssm_out_g, w_out, norm_mlp_g, w_up, w_down, loss_target, m_meta_tokens, m_norm_mix_g, m_w_in, m_q_norm_g, m_k_norm_g, m_attn_sinks, m_ssm_lambda_re, m_ssm_lambda_im, m_ssm_log_step, m_ssm_b_re, m_ssm_b_im, m_ssm_c_re, m_ssm_c_im, m_ssm_d, m_w_glu, m_b_glu, m_attn_out_g, m_ssm_out_g, m_w_out, m_norm_mlp_g, m_w_up, m_w_down, v_meta_tokens, v_norm_mix_g, v_w_in, v_q_norm_g, v_k_norm_g, v_attn_sinks, v_ssm_lambda_re, v_ssm_lambda_im, v_ssm_log_step, v_ssm_b_re, v_ssm_b_im, v_ssm_c_re, v_ssm_c_im, v_ssm_d, v_w_glu, v_b_glu, v_attn_out_g, v_ssm_out_g, v_w_out, v_norm_mlp_g, v_w_up, v_w_down):
    args = (meta_tokens, norm_mix_g, w_in, q_norm_g, k_norm_g, attn_sinks, ssm_lambda_re, ssm_lambda_im, ssm_log_step, ssm_b_re, ssm_b_im, ssm_c_re, ssm_c_im, ssm_d, w_glu, b_glu, attn_out_g, ssm_out_g, w_out, norm_mlp_g, w_up, w_down)
    ms = (m_meta_tokens, m_norm_mix_g, m_w_in, m_q_norm_g, m_k_norm_g, m_attn_sinks, m_ssm_lambda_re, m_ssm_lambda_im, m_ssm_log_step, m_ssm_b_re, m_ssm_b_im, m_ssm_c_re, m_ssm_c_im, m_ssm_d, m_w_glu, m_b_glu, m_attn_out_g, m_ssm_out_g, m_w_out, m_norm_mlp_g, m_w_up, m_w_down)
    vs = (v_meta_tokens, v_norm_mix_g, v_w_in, v_q_norm_g, v_k_norm_g, v_attn_sinks, v_ssm_lambda_re, v_ssm_lambda_im, v_ssm_log_step, v_ssm_b_re, v_ssm_b_im, v_ssm_c_re, v_ssm_c_im, v_ssm_d, v_w_glu, v_b_glu, v_attn_out_g, v_ssm_out_g, v_w_out, v_norm_mlp_g, v_w_up, v_w_down)
    W = dict(zip(WEIGHTS, args))
    M = dict(zip(WEIGHTS, ms))
    V = dict(zip(WEIGHTS, vs))
    depth = norm_mix_g.shape[0]
    seq, D = x.shape[1], x.shape[2]
    attn_w = D // 2
    kv_w = attn_w // KV_GROUP
    dims = dict(attn_w=attn_w, kv_w=kv_w, u_off=(attn_w + 2 * kv_w) // LANES)
    small_names = [n for n in WEIGHTS if n not in BIG and n != 'meta_tokens']
    chip = 2 * lax.axis_index("x") + lax.axis_index("y")

    gathers, started = [], jnp.zeros((), F32)
    for l in range(depth):
        placed = [_ag_place("ag_place_" + n, W[n], l, COL_SHARDED[n], BF16) for n in BIG]
        groups = [_ag_copies(a, COL_SHARDED[n]) for a, n in enumerate(BIG)]
        if l == 0:
            placed = [_ag_place("ag_place_meta", meta_tokens[None], 0, True, F32)] + placed
            groups = [_ag_copies(0, True)] + [_ag_copies(a + 1, COL_SHARDED[n]) for a, n in enumerate(BIG)]
        sems, bufs, token = _split_start("ag_start_%d" % l, placed, groups)
        gathers.append(dict(zip((['meta_tokens'] if l == 0 else []) + BIG, zip(sems, bufs))))
        started = started + token[0, 0]

    def gathered(l, n, after):
        sems, buf = gathers[l][n]
        return _split_wait("ag_wait_%d_%s" % (l, n), [buf], sems, after, _ag_copies(0, n == 'meta_tokens' or COL_SHARDED[n]))[0]

    h_res = jnp.concatenate([jnp.zeros((PAD, D), F32), gathered(0, 'meta_tokens', started.reshape(1, 1)), x[0]], axis=0)
    s5_flat3, s5_layers = _ssm_prepare(W)
    layer_p = []
    for l in range(depth):
        p = {n: W[n][l][None, :] for n in ('norm_mix_g', 'q_norm_g', 'k_norm_g', 'attn_sinks', 'ssm_d', 'b_glu', 'attn_out_g',
                                             'ssm_out_g', 'norm_mlp_g')}
        p['s5'] = s5_layers[l]
        layer_p.append(p)
    saved = []
    for l in range(depth):
        h_res, sv = _layer_fwd(h_res, layer_p[l], functools.partial(gathered, l), dims)
        saved.append(sv)
    loss_local, dx, dx16 = _loss(h_res, loss_target[0])
    loss = lax.psum(loss_local, ("x", "y", "c"))

    small_grads = [None] * depth
    shard_grads = {}
    pending = []

    def reduce_grads(l, names, grads):
        st = _rs_start("%d_%s" % (l, names[0]), list(grads), [COL_SHARDED[n] for n in names])
        st.update(layer=l, names=names, fresh=True)
        pending.append(st)
        return st['token'][0, 0]

    def tick(after):
        token = jnp.zeros((), F32)
        for st in list(pending):
            if st['fresh']:
                st['fresh'] = False
                continue
            _rs_advance(st, after)
            if 'result' in st:
                pending.remove(st)
                for a, n in enumerate(st['names']):
                    shard_grads[(st['layer'], n)] = (st['result'][0][a], st['result'][1][a])
            else:
                token = token + st['token'][0, 0]
        return token

    token = jnp.zeros((), F32)
    for l in reversed(range(depth)):
        dx, dx16, gs = _layer_bwd(dx, dx16, saved[l], layer_p[l], dims, functools.partial(reduce_grads, l), tick, token)
        saved[l] = None
        small_grads[l] = gs
        token = tick(dx)
    grad_x = dx[BLOCK:].reshape(x.shape)

    g_small = _ssm_param_grads(W, s5_flat3, [small_grads[l]['s5_raw'] for l in range(depth)])
    for n in small_names:
        if n not in g_small:
            g_small[n] = jnp.stack([small_grads[l][n].reshape(W[n].shape[1:]) for l in range(depth)])
    g_shapes = [(N_META, D)] + [W[n].shape for n in small_names]
    rows = _pack_rows(g_shapes, 8 * 2 * N_CHIPS)
    packed = _pack([dx[PAD:BLOCK]] + [g_small[n] for n in small_names], rows)
    tick(packed)
    red, = _reduce_scatter("small", [packed], [False], F32)
    tick(red)
    red_full, = _all_gather("ag_small", [red], [False])
    while pending:
        tick(red_full)
    g_list = _unpack(red_full, g_shapes)
    g_meta = lax.dynamic_slice_in_dim(g_list[0], chip * meta_tokens.shape[1], meta_tokens.shape[1], axis=1)
    G = dict(zip(small_names, g_list[1:]))
    G['meta_tokens'] = g_meta

    out = {}
    for n in BIG:
        out[n] = _adamw_big("adamw_" + n, [shard_grads[(l, n)][0] for l in range(depth)],
                            [shard_grads[(l, n)][1] for l in range(depth)], W[n], M[n], V[n])
    for n in ['meta_tokens'] + small_names:
        rows2d = lambda a: a.reshape(-1, a.shape[-1])
        upd = _ew("adamw_" + n, _adam_math, [rows2d(W[n]), rows2d(G[n]), rows2d(M[n]), rows2d(V[n])], [F32] * 3)
        out[n] = (G[n], *[u.reshape(W[n].shape) for u in upd])
    return (loss, grad_x, *[out[n][0] for n in WEIGHTS], *[out[n][1] for n in WEIGHTS],
            *[out[n][2] for n in WEIGHTS], *[out[n][3] for n in WEIGHTS])
```

```python
import functools
import math

import numpy as np
import jax
import jax.numpy as jnp
from jax import lax
from jax.experimental import pallas as pl
from jax.experimental.pallas import tpu as pltpu

F32 = jnp.float32
BF16 = jnp.bfloat16
MESH = pl.DeviceIdType.MESH

N_META = 16
HEAD_DIM = 64
KV_GROUP = 4
GROUP_CH = 16
STATE = 64
BLOCK = 128
PAD = BLOCK - N_META
NORM_EPS = 1e-6
NEG_INF = -1e30
LANES = 128
V7X_VMEM_LIMIT_BYTES = 56 * 1024 * 1024
MM_VMEM_BUDGET_BYTES = 44 * 1024 * 1024

ADAM_LR, ADAM_B1, ADAM_B2, ADAM_EPS, ADAM_WD, ADAM_STEP = 0.001, 0.9, 0.999, 1e-08, 0.01, 10

WEIGHTS = ['meta_tokens', 'norm_mix_g', 'w_in', 'q_norm_g', 'k_norm_g', 'attn_sinks', 'ssm_lambda_re',
           'ssm_lambda_im', 'ssm_log_step', 'ssm_b_re', 'ssm_b_im', 'ssm_c_re', 'ssm_c_im', 'ssm_d', 'w_glu',
           'b_glu', 'attn_out_g', 'ssm_out_g', 'w_out', 'norm_mlp_g', 'w_up', 'w_down']
BIG = ['w_in', 'w_glu', 'w_out', 'w_up', 'w_down']
COL_SHARDED = {'w_in': True, 'w_glu': False, 'w_out': False, 'w_up': True, 'w_down': False}
N_CHIPS = 4


def _pick(n, cands):
    for c in cands:
        if c <= n and n % c == 0:
            return c
    return n


def _params(sem):
    return pltpu.CompilerParams(dimension_semantics=sem, vmem_limit_bytes=V7X_VMEM_LIMIT_BYTES)


def _pcall(body, **kw):
    return pl.pallas_call(body, **kw)


def _sds(shape, dtype):
    return jax.ShapeDtypeStruct(shape, dtype)


_DIMS = {'nn': ((1,), (0,)), 'nt': ((1,), (1,)), 'tn': ((0,), (0,))}


def _mm(name, a, b, mode, out_dtypes, extras=(), epi=None):
    if mode == 'nn':
        (M, K), (_, N) = a.shape, b.shape
    elif mode == 'nt':
        (M, K), (N, _) = a.shape, b.shape
    else:
        (K, M), (_, N) = a.shape, b.shape
    tile_bytes = 4 * len([k for k, _ in extras if k == 'tile']) + sum(jnp.dtype(d).itemsize for d in out_dtypes)

    def fits(tm, tn, tk):
        need = 2 * tm * tk * a.dtype.itemsize + 2 * tk * tn * b.dtype.itemsize + 4 * tm * tn + 2 * tm * tn * tile_bytes
        return need <= MM_VMEM_BUDGET_BYTES

    if mode == 'tn':
        tm, tk_cands = _pick(M, (1024, 512, 256, 128)), (1408, 704, 384, 128)
    else:
        tm, tk_cands = _pick(M, (1408, 704, 384, 128)), (2048, 1024, 512, 256, 128)
    tk_cands = [t for t in tk_cands if t <= K and K % t == 0] or [K]
    tn_cands = [t for t in (2048, 1280, 1024, 640, 512, 256, 128) if t <= N and N % t == 0] or [N]
    if mode != 'tn' and tk_cands[0] == K and a.dtype == BF16:
        tk_cands = tk_cands[:1]
    tn, tk = next(((tn_, tk_) for tn_ in tn_cands for tk_ in tk_cands if fits(tm, tn_, tk_)), (tn_cands[-1], tk_cands[-1]))
    nk = K // tk
    a_spec = pl.BlockSpec((tk, tm), lambda i, j, k: (k, i)) if mode == 'tn' else pl.BlockSpec((tm, tk), lambda i, j, k: (i, k))
    b_spec = pl.BlockSpec((tn, tk), lambda i, j, k: (j, k)) if mode == 'nt' else pl.BlockSpec((tk, tn), lambda i, j, k: (k, j))
    ex_specs = [pl.BlockSpec((tm, tn), lambda i, j, k: (i, j)) if kind == 'tile' else pl.BlockSpec((1, tn), lambda i, j, k: (0, j))
                for kind, _ in extras]
    ne, no = len(extras), len(out_dtypes)
    dims = (_DIMS[mode], ((), ()))

    def body(a_ref, b_ref, *rest):
        ex, outs, acc = rest[:ne], rest[ne:ne + no], rest[ne + no]
        k = pl.program_id(2)

        @pl.when(k == 0)
        def _():
            acc[...] = jnp.zeros_like(acc)

        acc[...] += lax.dot_general(a_ref[...].astype(BF16), b_ref[...].astype(BF16), dims, preferred_element_type=F32)

        @pl.when(k == nk - 1)
        def _():
            r = acc[...]
            res = epi(r, *[e[...] for e in ex]) if epi is not None else (r,)
            for o, v in zip(outs, res):
                o[...] = v.astype(o.dtype)

    outs = _pcall(
        body, name=name, grid=(M // tm, N // tn, nk),
        in_specs=[a_spec, b_spec] + ex_specs,
        out_specs=[pl.BlockSpec((tm, tn), lambda i, j, k: (i, j)) for _ in out_dtypes],
        out_shape=[_sds((M, N), d) for d in out_dtypes],
        scratch_shapes=[pltpu.VMEM((tm, tn), F32)],
        compiler_params=_params(("parallel", "parallel", "arbitrary")),
    )(a, b, *[e for _, e in extras])
    return outs


def _ew(name, fn, ins, out_dtypes):
    R, C = ins[0].shape
    tr = _pick(R, tuple(t for t in (1024, 512, 256, 128, 64, 32, 16, 8) if t * C <= 512 * 1024) or (8,))
    n_in = len(ins)

    def body(*refs):
        res = fn(*[r[...] for r in refs[:n_in]])
        for o, v in zip(refs[n_in:], res):
            o[...] = v.astype(o.dtype)

    spec = pl.BlockSpec((tr, C), lambda i: (i, 0))
    return _pcall(body, name=name, grid=(R // tr,), in_specs=[spec] * n_in, out_specs=[spec] * len(out_dtypes),
                  out_shape=[_sds((R, C), d) for d in out_dtypes], compiler_params=_params(("parallel",)))(*ins)


def _adam_math(w, g, m, v):
    m = ADAM_B1 * m + (1.0 - ADAM_B1) * g
    v = ADAM_B2 * v + (1.0 - ADAM_B2) * (g * g)
    m_hat = m / (1.0 - ADAM_B1 ** ADAM_STEP)
    v_hat = v / (1.0 - ADAM_B2 ** ADAM_STEP)
    delta = -ADAM_LR * (m_hat / (jnp.sqrt(v_hat) + ADAM_EPS) + ADAM_WD * w)
    return delta, m, v


def _rms_fwd(name, xs, gs, out_dtype):
    L = xs[0].shape[0]
    ws = [x.shape[1] for x in xs]
    n = len(xs)
    tr = _pick(L, (384, 256, 128))

    def body(*refs):
        o = refs[2 * n]
        off = 0
        for i in range(n):
            x = refs[i][...]
            r = lax.rsqrt(jnp.mean(x * x, axis=-1, keepdims=True) + NORM_EPS)
            o[:, off:off + ws[i]] = ((x * r) * refs[n + i][...]).astype(o.dtype)
            off += ws[i]

    return _pcall(body, name=name, grid=(L // tr,),
                  in_specs=[pl.BlockSpec((tr, w), lambda i: (i, 0)) for w in ws] + [pl.BlockSpec((1, w), lambda i: (0, 0)) for w in ws],
                  out_specs=pl.BlockSpec((tr, sum(ws)), lambda i: (i, 0)), out_shape=_sds((L, sum(ws)), out_dtype),
                  compiler_params=_params(("parallel",)))(*xs, *gs)


def _rms_bwd(name, xs, gs, dy, resid=None):
    L = xs[0].shape[0]
    ws = [x.shape[1] for x in xs]
    n = len(xs)
    tr = _pick(L, (384, 256, 128))
    has_res = resid is not None

    def body(*refs):
        x_refs, g_refs, dy_ref = refs[:n], refs[n:2 * n], refs[2 * n]
        p = 2 * n + 1
        res_ref = refs[p] if has_res else None
        p += 1 if has_res else 0
        dx_refs, dg_refs = refs[p:p + n], refs[p + n:p + 2 * n]
        dx16_ref = refs[p + 2 * n] if has_res else None
        first = pl.program_id(0) == 0
        off = 0
        for i in range(n):
            x = x_refs[i][...]
            d = dy_ref[:, off:off + ws[i]]
            r = lax.rsqrt(jnp.mean(x * x, axis=-1, keepdims=True) + NORM_EPS)
            xh = x * r
            dg = jnp.sum(d * xh, axis=0, keepdims=True)

            @pl.when(first)
            def _(i=i):
                dg_refs[i][...] = jnp.zeros_like(dg_refs[i])

            dg_refs[i][...] += dg
            dyg = d * g_refs[i][...]
            dx = r * (dyg - xh * jnp.mean(dyg * xh, axis=-1, keepdims=True))
            if has_res:
                dx = dx + res_ref[...]
                dx16_ref[...] = dx.astype(BF16)
            dx_refs[i][...] = dx
            off += ws[i]

    in_specs = ([pl.BlockSpec((tr, w), lambda i: (i, 0)) for w in ws] + [pl.BlockSpec((1, w), lambda i: (0, 0)) for w in ws]
                + [pl.BlockSpec((tr, sum(ws)), lambda i: (i, 0))])
    ins = list(xs) + list(gs) + [dy]
    if has_res:
        in_specs.append(pl.BlockSpec((tr, ws[0]), lambda i: (i, 0)))
        ins.append(resid)
    out_specs = [pl.BlockSpec((tr, w), lambda i: (i, 0)) for w in ws] + [pl.BlockSpec((1, w), lambda i: (0, 0)) for w in ws]
    out_shape = [_sds((L, w), F32) for w in ws] + [_sds((1, w), F32) for w in ws]
    if has_res:
        out_specs.append(pl.BlockSpec((tr, ws[0]), lambda i: (i, 0)))
        out_shape.append(_sds((L, ws[0]), BF16))
    outs = _pcall(body, name=name, grid=(L // tr,), in_specs=in_specs, out_specs=out_specs, out_shape=out_shape,
                  compiler_params=_params(("arbitrary",)))(*ins)
    return (outs[:n], outs[n:2 * n], outs[2 * n]) if has_res else (outs[:n], outs[n:])


def _loss(xl, target):
    Lp, D = xl.shape

    def body(x_ref, t_ref, dy_ref, dy16_ref, loss_ref):
        n = pl.program_id(0)

        @pl.when(n == 0)
        def _():
            loss_ref[...] = jnp.zeros_like(loss_ref)
            dy_ref[...] = jnp.zeros_like(dy_ref)
            dy16_ref[...] = jnp.zeros_like(dy16_ref)

        @pl.when(n > 0)
        def _():
            err = x_ref[...] - t_ref[...]
            dy = err * (1.0 / D)
            dy_ref[...] = dy
            dy16_ref[...] = dy.astype(BF16)
            loss_ref[...] += jnp.sum(err * err) * (0.5 / D)

    blk = pl.BlockSpec((BLOCK, D), lambda n: (n, 0))
    dy, dy16, loss = _pcall(body, name="loss_head", grid=(Lp // BLOCK,),
                            in_specs=[blk, pl.BlockSpec((BLOCK, D), lambda n: (jnp.maximum(n - 1, 0), 0))],
                            out_specs=[blk, blk, pl.BlockSpec((8, LANES), lambda n: (0, 0))],
                            out_shape=[_sds((Lp, D), F32), _sds((Lp, D), BF16), _sds((8, LANES), F32)],
                            compiler_params=_params(("arbitrary",)))(xl, target)
    return loss[0, 0], dy, dy16


GROUP_ROWS = KV_GROUP * BLOCK


def _attn_mask_dist(n):
    r = lax.broadcasted_iota(jnp.int32, (GROUP_ROWS, 3 * BLOCK), 0)
    i = r & (BLOCK - 1)
    j = lax.broadcasted_iota(jnp.int32, (GROUP_ROWS, 3 * BLOCK), 1)
    in_band = j < 2 * BLOCK
    band = in_band & (j > i) & (j <= i + BLOCK) & (j >= 2 * BLOCK - BLOCK * n)
    jm = j - 2 * BLOCK
    meta = (~in_band) & (jm >= PAD) & (jm <= BLOCK * n + i)
    dist = jnp.where(in_band, BLOCK + i - j, BLOCK * n + i - jm).astype(F32)
    return band | meta, dist


def _head_norm(x, g):
    r = lax.rsqrt(jnp.mean(x * x, axis=-1, keepdims=True) + NORM_EPS)
    return (x * r) * g, r


def _qk_norm(proj, gq, gk, attn_w, kv_w):
    Lp = proj.shape[0]
    n_heads, n_kv = attn_w // HEAD_DIM, kv_w // HEAD_DIM
    tr = _pick(Lp, (384, 256, 128))
    scale = 1.0 / math.sqrt(HEAD_DIM)

    def body(q_ref, k_ref, gq_ref, gk_ref, qn_ref, kn_ref, rq_ref):
        lane = lax.broadcasted_iota(jnp.int32, (tr, LANES), 1)
        rq = jnp.zeros((tr, LANES), F32)
        for h in range(n_heads):
            qn, r = _head_norm(q_ref[:, _head_slice(h)], gq_ref[...])
            qn_ref[:, _head_slice(h)] = (qn * scale).astype(BF16)
            rq = jnp.where(lane == h, r, rq)
        rq_ref[...] = rq
        for kh in range(n_kv):
            kn_ref[:, _head_slice(kh)] = _head_norm(k_ref[:, _head_slice(kh)], gk_ref[...])[0].astype(BF16)

    small = pl.BlockSpec((1, HEAD_DIM), lambda i: (0, 0))
    return _pcall(body, name="qk_norm", grid=(Lp // tr,),
                  in_specs=[pl.BlockSpec((tr, attn_w), lambda i: (i, 0)), pl.BlockSpec((tr, kv_w), lambda i: (i, attn_w // kv_w)), small, small],
                  out_specs=[pl.BlockSpec((tr, attn_w), lambda i: (i, 0)), pl.BlockSpec((tr, kv_w), lambda i: (i, 0)),
                             pl.BlockSpec((tr, LANES), lambda i: (i, 0))],
                  out_shape=[_sds((Lp, attn_w), BF16), _sds((Lp, kv_w), BF16), _sds((Lp, LANES), F32)],
                  compiler_params=_params(("parallel",)))(proj, proj, gq, gk)


def _attn_specs(attn_w, kv_w):
    q_spec = pl.BlockSpec((BLOCK, attn_w), lambda n: (n, 0))

    def kv(col):
        return [pl.BlockSpec((BLOCK, kv_w), lambda n: (jnp.maximum(n - 1, 0), col)),
                pl.BlockSpec((BLOCK, kv_w), lambda n: (n, col)),
                pl.BlockSpec((BLOCK, kv_w), lambda n: (0, col))]

    return q_spec, kv(0), kv(attn_w // kv_w + 1)


def _slopes(n_heads):
    return [2.0 ** (-8.0 * (h + 1) / n_heads) for h in range(n_heads)]


def _head_slice(h):
    return slice(h * HEAD_DIM, (h + 1) * HEAD_DIM)


def _stack_heads(ref, kh):
    return jnp.concatenate([ref[:, _head_slice(kh * KV_GROUP + g)] for g in range(KV_GROUP)], axis=0)


def _group_column(vals):
    return jnp.concatenate([jnp.broadcast_to(v, (BLOCK, 1)) for v in vals], axis=0)


def _group_inputs(kh, slopes, qn_ref, kp, kc, km, vp, vc, vm, sk_ref):
    cs = _head_slice(kh)
    kn = jnp.concatenate([kp[:, cs], kc[:, cs], km[:, cs]], axis=0)
    vcat = jnp.concatenate([vp[:, cs], vc[:, cs], vm[:, cs]], axis=0).astype(BF16)
    heads = range(kh * KV_GROUP, (kh + 1) * KV_GROUP)
    slope = _group_column([jnp.full((1, 1), slopes[h], F32) for h in heads])
    sink = _group_column([sk_ref[0:1, h:h + 1] for h in heads])
    return _stack_heads(qn_ref, kh), kn, vcat, slope, sink


def _scores(qn, kn, slope, sink, mask, dist):
    s = lax.dot_general(qn, kn, (((1,), (1,)), ((), ())), preferred_element_type=F32) - slope * dist
    s = jnp.where(mask, s, NEG_INF)
    m = jnp.maximum(jnp.max(s, axis=-1, keepdims=True), sink)
    p = jnp.exp(s - m)
    ps = jnp.exp(sink - m)
    inv = 1.0 / (jnp.sum(p, axis=-1, keepdims=True) + ps)
    return p * inv, ps * inv


def _attn_fwd(proj, qn, kn, sinks, attn_w, kv_w):
    Lp = proj.shape[0]
    n_heads, n_kv = attn_w // HEAD_DIM, kv_w // HEAD_DIM
    slopes = _slopes(n_heads)
    q_spec, k_specs, v_specs = _attn_specs(attn_w, kv_w)

    def body(qn_ref, kp, kc, km, vp, vc, vm, sk_ref, o_ref):
        mask, dist = _attn_mask_dist(pl.program_id(0))
        for kh in range(n_kv):
            q16, k16, vcat, slope, sink = _group_inputs(kh, slopes, qn_ref, kp, kc, km, vp, vc, vm, sk_ref)
            p, _ = _scores(q16, k16, slope, sink, mask, dist)
            o = jnp.dot(p.astype(BF16), vcat, preferred_element_type=F32)
            for g in range(KV_GROUP):
                o_ref[:, _head_slice(kh * KV_GROUP + g)] = o[g * BLOCK:(g + 1) * BLOCK]

    return _pcall(body, name="attn_fwd", grid=(Lp // BLOCK,),
                  in_specs=[q_spec] + k_specs + v_specs + [pl.BlockSpec((1, n_heads), lambda n: (0, 0))],
                  out_specs=pl.BlockSpec((BLOCK, attn_w), lambda n: (n, 0)), out_shape=_sds((Lp, attn_w), F32),
                  compiler_params=_params(("parallel",)))(qn, kn, kn, kn, proj, proj, proj, sinks)


def _attn_bwd(proj, qn, kn, rq, attn, dattn, gq, sinks, attn_w, kv_w):
    Lp = proj.shape[0]
    n_heads, n_kv = attn_w // HEAD_DIM, kv_w // HEAD_DIM
    slopes = _slopes(n_heads)
    q_spec, k_specs, v_specs = _attn_specs(attn_w, kv_w)
    scale = 1.0 / math.sqrt(HEAD_DIM)
    tn_dims = (((0,), (0,)), ((), ()))

    def body(q_ref, qn_ref, rq_ref, kp, kc, km, vp, vc, vm, o_ref, do_ref, gq_ref, sk_ref, dq_ref, dk_ref, dv_ref, dgq_ref, dsk_ref):
        n = pl.program_id(0)

        @pl.when(n == 0)
        def _():
            dk_ref[...] = jnp.zeros_like(dk_ref)
            dv_ref[...] = jnp.zeros_like(dv_ref)
            dgq_ref[...] = jnp.zeros_like(dgq_ref)
            dsk_ref[...] = jnp.zeros_like(dsk_ref)

        mask, dist = _attn_mask_dist(n)
        lane = lax.broadcasted_iota(jnp.int32, (1, n_heads), 1)
        rows_prev = pl.ds(pl.multiple_of(jnp.maximum(n - 1, 0) * BLOCK, BLOCK), BLOCK)
        rows_cur = pl.ds(pl.multiple_of(n * BLOCK, BLOCK), BLOCK)
        rows_meta = pl.ds(0, BLOCK)
        dgq = jnp.zeros((1, HEAD_DIM), F32)
        dsk = jnp.zeros((1, n_heads), F32)
        for kh in range(n_kv):
            cs = _head_slice(kh)
            q16, k16, vcat, slope, sink = _group_inputs(kh, slopes, qn_ref, kp, kc, km, vp, vc, vm, sk_ref)
            q = _stack_heads(q_ref, kh)
            rq = _group_column([rq_ref[:, h:h + 1] for h in range(kh * KV_GROUP, (kh + 1) * KV_GROUP)])
            p, ps = _scores(q16, k16, slope, sink, mask, dist)
            do = _stack_heads(do_ref, kh)
            dd = jnp.sum(do * _stack_heads(o_ref, kh), axis=-1, keepdims=True)
            do16 = do.astype(BF16)
            dp = lax.dot_general(do16, vcat, (((1,), (1,)), ((), ())), preferred_element_type=F32)
            ds16 = (p * (dp - dd)).astype(BF16)
            dsink = -ps * dd
            for g in range(KV_GROUP):
                dsk = dsk + jnp.where(lane == kh * KV_GROUP + g, jnp.sum(dsink[g * BLOCK:(g + 1) * BLOCK]), 0.0)
            dqn = jnp.dot(ds16, k16, preferred_element_type=F32) * scale
            dkn = lax.dot_general(ds16, q16, tn_dims, preferred_element_type=F32)
            dvc = lax.dot_general(p.astype(BF16), do16, tn_dims, preferred_element_type=F32)
            xh = q * rq
            dgq = dgq + jnp.sum(dqn * xh, axis=0, keepdims=True)
            dyg = dqn * gq_ref[...]
            dq = rq * (dyg - xh * jnp.mean(dyg * xh, axis=-1, keepdims=True))
            for g in range(KV_GROUP):
                dq_ref[:, _head_slice(kh * KV_GROUP + g)] = dq[g * BLOCK:(g + 1) * BLOCK].astype(dq_ref.dtype)
            for part, rows in enumerate((rows_prev, rows_cur, rows_meta)):
                ps_ = slice(part * BLOCK, (part + 1) * BLOCK)
                dk_ref[rows, cs] += dkn[ps_]
                dv_ref[rows, cs] += dvc[ps_]
        dgq_ref[...] += dgq
        dsk_ref[...] += dsk

    small = lambda w: pl.BlockSpec((1, w), lambda n: (0, 0))
    blk = pl.BlockSpec((BLOCK, attn_w), lambda n: (n, 0))
    whole = pl.BlockSpec((Lp, kv_w), lambda n: (0, 0))
    return _pcall(body, name="attn_bwd", grid=(Lp // BLOCK,),
                  in_specs=[q_spec, q_spec, pl.BlockSpec((BLOCK, LANES), lambda n: (n, 0))] + k_specs + v_specs
                  + [blk, blk, small(HEAD_DIM), small(n_heads)],
                  out_specs=[blk, whole, whole, small(HEAD_DIM), small(n_heads)],
                  out_shape=[_sds(proj.shape, BF16), _sds((Lp, kv_w), F32), _sds((Lp, kv_w), F32),
                             _sds((1, HEAD_DIM), F32), _sds((1, n_heads), F32)],
                  compiler_params=_params(("arbitrary",)))(proj, qn, rq, kn, kn, kn, proj, proj, proj, attn, dattn, gq, sinks)


def _knorm_bwd(proj, dkn, dv, gk, dproj, attn_w, kv_w):
    Lp = proj.shape[0]
    n_kv = kv_w // HEAD_DIM
    tr = _pick(Lp, (384, 256, 128))

    def body(k_ref, d_ref, dv_ref, g_ref, buf_ref, out_ref, dg_ref):
        @pl.when(pl.program_id(0) == 0)
        def _():
            dg_ref[...] = jnp.zeros_like(dg_ref)

        dg = jnp.zeros((1, HEAD_DIM), F32)
        for kh in range(n_kv):
            cs = slice(kh * HEAD_DIM, (kh + 1) * HEAD_DIM)
            x = k_ref[:, cs]
            d = d_ref[:, cs]
            r = lax.rsqrt(jnp.mean(x * x, axis=-1, keepdims=True) + NORM_EPS)
            xh = x * r
            dg = dg + jnp.sum(d * xh, axis=0, keepdims=True)
            dyg = d * g_ref[...]
            out_ref[:, cs] = (r * (dyg - xh * jnp.mean(dyg * xh, axis=-1, keepdims=True))).astype(out_ref.dtype)
        out_ref[:, kv_w:] = dv_ref[...].astype(out_ref.dtype)
        dg_ref[...] += dg

    kv_blk = pl.BlockSpec((tr, kv_w), lambda i: (i, 0))
    return _pcall(body, name="knorm_bwd", grid=(Lp // tr,),
                  in_specs=[pl.BlockSpec((tr, kv_w), lambda i: (i, attn_w // kv_w)), kv_blk, kv_blk,
                            pl.BlockSpec((1, HEAD_DIM), lambda i: (0, 0)), pl.BlockSpec(memory_space=pl.ANY)],
                  out_specs=[pl.BlockSpec((tr, 2 * kv_w), lambda i: (i, attn_w // (2 * kv_w))), pl.BlockSpec((1, HEAD_DIM), lambda i: (0, 0))],
                  out_shape=[_sds(dproj.shape, dproj.dtype), _sds((1, HEAD_DIM), F32)],
                  input_output_aliases={4: 0},
                  compiler_params=_params(("arbitrary",)))(proj, dkn, dv, gk, dproj)


def _ssm_bbar(lr, li, ls, br, bi):
    def fn(lr, li, ls, br, bi):
        fr, fi = _zoh_factor(lr, li, ls)
        return fr * br - fi * bi, fr * bi + fi * br

    return _ew("ssm_bbar", fn, [lr, li, ls, br, bi], [F32, F32])


def _lam_bar(lr, li, ls):
    dl = jnp.exp(ls)
    e = jnp.exp(lr * dl)
    return e * jnp.cos(li * dl), e * jnp.sin(li * dl), dl


def _zoh_factor(lr, li, ls):
    ar, ai, _ = _lam_bar(lr, li, ls)
    n2 = lr * lr + li * li
    ivr, ivi = lr / n2, -li / n2
    return (ar - 1.0) * ivr - ai * ivi, (ar - 1.0) * ivi + ai * ivr


SCAN_SHIFTS = (1, 2, 4)


def _ssm_tables(lr, li, ls):
    Wx = lr.shape[1]

    def body(lr_ref, li_ref, ls_ref, tf_ref, tr_ref):
        dl = jnp.exp(ls_ref[...])
        zr, zi = lr_ref[...] * dl, li_ref[...] * dl
        row = lax.broadcasted_iota(jnp.int32, (8, Wx), 0)

        def power(kf):
            e = jnp.exp(kf * zr)
            return e * jnp.cos(kf * zi), e * jnp.sin(kf * zi)

        for ref, rev in ((tf_ref, False), (tr_ref, True)):
            sgn = -1.0 if rev else 1.0
            for k, d in enumerate(SCAN_SHIFTS):
                ar, ai = power(jnp.full((8, Wx), float(d), F32))
                keep = (row < 8 - d) if rev else (row >= d)
                ref[k] = jnp.where(keep, ar, 0.0)
                ref[4 + k] = jnp.where(keep, sgn * ai, 0.0)
            pr, pi = power(((8 - row) if rev else (row + 1)).astype(F32))
            ref[3] = pr
            ref[7] = sgn * pi

    full = pl.BlockSpec((1, Wx), lambda: (0, 0))
    tab = pl.BlockSpec((8, 8, Wx), lambda: (0, 0, 0))
    return _pcall(body, name="ssm_tables", in_specs=[full] * 3, out_specs=[tab, tab],
                  out_shape=[_sds((8, 8, Wx), F32)] * 2,
                  compiler_params=pltpu.CompilerParams(vmem_limit_bytes=V7X_VMEM_LIMIT_BYTES))(lr, li, ls)


def _scan(name, br, bi, tab, layer, reverse, states=None):
    L, Wx = br.shape
    TB = _pick(L, (384, 256, 128))
    CW = _pick(Wx, (1024, 512, 256, 128))
    nT, nG = L // TB, TB // 8

    def body(*refs):
        if reverse:
            br_ref, bi_ref, xr_ref, xi_ref, tab_ref, or_ref, oi_ref, s1_ref, s2_ref, cr_ref, ci_ref = refs
        else:
            br_ref, bi_ref, tab_ref, or_ref, oi_ref, cr_ref, ci_ref = refs

        @pl.when(pl.program_id(1) == 0)
        def _():
            cr_ref[...] = jnp.zeros_like(cr_ref)
            ci_ref[...] = jnp.zeros_like(ci_ref)
            if reverse:
                s1_ref[...] = jnp.zeros_like(s1_ref)
                s2_ref[...] = jnp.zeros_like(s2_ref)

        def step(q, carry):
            cr, ci = carry[0], carry[1]
            g = (nG - 1 - q) if reverse else q
            rows = pl.ds(pl.multiple_of(g * 8, 8), 8)
            b_r, b_i = br_ref[rows, :], bi_ref[rows, :]
            sr, si = b_r, b_i
            for k, d in enumerate(SCAN_SHIFTS):
                mr, mi = tab_ref[k], tab_ref[4 + k]
                sh = (8 - d) if reverse else d
                pr, pi = pltpu.roll(sr, sh, 0), pltpu.roll(si, sh, 0)
                sr, si = sr + mr * pr - mi * pi, si + mr * pi + mi * pr
            pwr, pwi = tab_ref[3], tab_ref[7]
            xr = sr + pwr * cr - pwi * ci
            xi = si + pwr * ci + pwi * cr
            or_ref[rows, :] = xr
            oi_ref[rows, :] = xi
            row = 0 if reverse else 7
            out = (jnp.broadcast_to(xr[row:row + 1, :], xr.shape), jnp.broadcast_to(xi[row:row + 1, :], xi.shape))
            if reverse:
                hr, hi = xr - b_r, xi - b_i
                st_r, st_i = xr_ref[rows, :], xi_ref[rows, :]
                out = out + (carry[2] + hr * st_r + hi * st_i, carry[3] + hi * st_r - hr * st_i)
            return out

        init = (cr_ref[...], ci_ref[...])
        if reverse:
            init = init + (jnp.zeros((8, CW), F32), jnp.zeros((8, CW), F32))
        fin = lax.fori_loop(0, nG, step, init, unroll=2)
        cr_ref[...] = fin[0]
        ci_ref[...] = fin[1]
        if reverse:
            s1_ref[...] += fin[2]
            s2_ref[...] += fin[3]

    tmap = (lambda j, t: (nT - 1 - t, j)) if reverse else (lambda j, t: (t, j))
    blk = pl.BlockSpec((TB, CW), tmap)
    tab_spec = pl.BlockSpec((8, 8, CW), lambda j, t: (0, 0, layer * (Wx // CW) + j))
    sum_spec = pl.BlockSpec((8, CW), lambda j, t: (0, j))
    ins = [br, bi] + (list(states) if reverse else []) + [tab]
    in_specs = [blk, blk] + ([blk, blk] if reverse else []) + [tab_spec]
    out_specs = [blk, blk] + ([sum_spec, sum_spec] if reverse else [])
    out_shape = [_sds((L, Wx), F32)] * 2 + ([_sds((8, Wx), F32)] * 2 if reverse else [])
    return _pcall(body, name=name, grid=(Wx // CW, nT), in_specs=in_specs, out_specs=out_specs, out_shape=out_shape,
                  scratch_shapes=[pltpu.VMEM((8, CW), F32), pltpu.VMEM((8, CW), F32)],
                  compiler_params=_params(("parallel", "arbitrary")))(*ins)


def _row_tile(L):
    return _pick(L, (1408, 704, 384, 128))


TILES_PER_BLOCK = 4


def _blockproj(name, src, off, w_r, w_i, layer, depth):
    L = src.shape[0]
    T = w_r.shape[0] // depth
    tm = _row_tile(L)
    wide = TILES_PER_BLOCK * LANES

    def body(s_ref, wr_ref, wi_ref, or_ref, oi_ref):
        s = s_ref[...].astype(BF16)
        for k in range(TILES_PER_BLOCK):
            cols = slice(k * LANES, (k + 1) * LANES)
            or_ref[:, cols] = jnp.dot(s, wr_ref[k], preferred_element_type=F32)
            oi_ref[:, cols] = jnp.dot(s, wi_ref[k], preferred_element_type=F32)

    w_spec = pl.BlockSpec((TILES_PER_BLOCK, LANES, LANES), lambda i, q: (layer * (T // TILES_PER_BLOCK) + q, 0, 0))
    o_spec = pl.BlockSpec((tm, wide), lambda i, q: (i, q))
    return _pcall(body, name=name, grid=(L // tm, T // TILES_PER_BLOCK),
                  in_specs=[pl.BlockSpec((tm, LANES), lambda i, q: (i, off + q)), w_spec, w_spec],
                  out_specs=[o_spec, o_spec], out_shape=[_sds((L, T * LANES), F32)] * 2,
                  compiler_params=_params(("parallel", "arbitrary")))(src, w_r, w_i)


def _blockproj_grad(name, src, off, gr, gi):
    L = src.shape[0]
    T = gr.shape[1] // LANES
    tm = _row_tile(L)
    wide = TILES_PER_BLOCK * LANES
    tn_dims = (((0,), (0,)), ((), ()))

    def body(s_ref, gr_ref, gi_ref, or_ref, oi_ref):
        @pl.when(pl.program_id(1) == 0)
        def _():
            or_ref[...] = jnp.zeros_like(or_ref)
            oi_ref[...] = jnp.zeros_like(oi_ref)

        s = s_ref[...].astype(BF16)
        for k in range(TILES_PER_BLOCK):
            cols = slice(k * LANES, (k + 1) * LANES)
            or_ref[k] += lax.dot_general(s, gr_ref[:, cols].astype(BF16), tn_dims, preferred_element_type=F32)
            oi_ref[k] += lax.dot_general(s, gi_ref[:, cols].astype(BF16), tn_dims, preferred_element_type=F32)

    g_spec = pl.BlockSpec((tm, wide), lambda q, i: (i, q))
    o_spec = pl.BlockSpec((TILES_PER_BLOCK, LANES, LANES), lambda q, i: (q, 0, 0))
    return _pcall(body, name=name, grid=(T // TILES_PER_BLOCK, L // tm),
                  in_specs=[pl.BlockSpec((tm, LANES), lambda q, i: (i, off + q)), g_spec, g_spec],
                  out_specs=[o_spec, o_spec], out_shape=[_sds((T, LANES, LANES), F32)] * 2,
                  compiler_params=_params(("parallel", "arbitrary")))(src, gr, gi)


def _gelu(y):
    k = math.sqrt(2.0 / math.pi)
    return 0.5 * y * (1.0 + jnp.tanh(k * (y + 0.044715 * (y * y * y))))


def _gelu_grad(y):
    k = math.sqrt(2.0 / math.pi)
    t = jnp.tanh(k * (y + 0.044715 * (y * y * y)))
    return 0.5 * (1.0 + t) + 0.5 * y * (1.0 - t * t) * (k * (1.0 + 3 * 0.044715 * (y * y)))


def _ssm_out(xr, xi, w_r, w_i, proj, u_off, dvec, layer, depth):
    L = xr.shape[0]
    J = w_r.shape[0] // depth
    SW = w_r.shape[1]
    tm = _row_tile(L)

    def body(xr_ref, xi_ref, wr_ref, wi_ref, u_ref, d_ref, y_ref, gl_ref):
        acc = jnp.dot(xr_ref[...].astype(BF16), wr_ref[...], preferred_element_type=F32)
        acc += jnp.dot(xi_ref[...].astype(BF16), wi_ref[...], preferred_element_type=F32)
        y = acc + d_ref[...] * u_ref[...]
        y_ref[...] = y
        gl_ref[...] = _gelu(y)

    x_spec = pl.BlockSpec((tm, SW), lambda j, i: (i, j))
    w_spec = pl.BlockSpec((None, SW, LANES), lambda j, i: (layer * J + j, 0, 0))
    o_spec = pl.BlockSpec((tm, LANES), lambda j, i: (i, j))
    return _pcall(body, name="ssm_out", grid=(J, L // tm),
                  in_specs=[x_spec, x_spec, w_spec, w_spec, pl.BlockSpec((tm, LANES), lambda j, i: (i, u_off + j)),
                            pl.BlockSpec((1, LANES), lambda j, i: (0, j))],
                  out_specs=[o_spec, o_spec], out_shape=[_sds((L, J * LANES), F32)] * 2,
                  compiler_params=_params(("parallel", "parallel")))(xr, xi, w_r, w_i, proj, dvec)


def _ssm_du(gr, gi, w_r, w_i, dy, proj, u_off, dvec, dproj, layer, depth):
    L = gr.shape[0]
    J = w_r.shape[0] // depth
    SW = w_r.shape[1]
    tm = _row_tile(L)

    def body(gr_ref, gi_ref, wr_ref, wi_ref, dy_ref, u_ref, d_ref, buf_ref, du_ref, dd_ref):
        i = pl.program_id(1)

        @pl.when(i == 0)
        def _():
            dd_ref[...] = jnp.zeros_like(dd_ref)

        acc = jnp.dot(gr_ref[...].astype(BF16), wr_ref[...], preferred_element_type=F32)
        acc += jnp.dot(gi_ref[...].astype(BF16), wi_ref[...], preferred_element_type=F32)
        dy = dy_ref[...]
        row = lax.broadcasted_iota(jnp.int32, (tm, LANES), 0) + i * tm
        du_ref[...] = jnp.where(row >= PAD, acc + d_ref[...] * dy, 0.0).astype(du_ref.dtype)
        dd_ref[...] += jnp.sum(dy * u_ref[...], axis=0, keepdims=True)

    x_spec = pl.BlockSpec((tm, SW), lambda j, i: (i, j))
    w_spec = pl.BlockSpec((None, SW, LANES), lambda j, i: (layer * J + j, 0, 0))
    o_spec = pl.BlockSpec((tm, LANES), lambda j, i: (i, j))
    u_spec = pl.BlockSpec((tm, LANES), lambda j, i: (i, u_off + j))
    vec = pl.BlockSpec((1, LANES), lambda j, i: (0, j))
    return _pcall(body, name="ssm_du", grid=(J, L // tm),
                  in_specs=[x_spec, x_spec, w_spec, w_spec, o_spec, u_spec, vec, pl.BlockSpec(memory_space=pl.ANY)],
                  out_specs=[u_spec, vec], out_shape=[_sds(dproj.shape, dproj.dtype), _sds((1, J * LANES), F32)],
                  input_output_aliases={7: 0},
                  compiler_params=_params(("parallel", "arbitrary")))(gr, gi, w_r, w_i, dy, proj, dvec, dproj)


def _ssm_dc(xr, xi, dy, SW):
    L = xr.shape[0]
    J = dy.shape[1] // LANES
    tm = _row_tile(L)
    tn_dims = (((0,), (0,)), ((), ()))

    def body(xr_ref, xi_ref, dy_ref, or_ref, oi_ref):
        @pl.when(pl.program_id(1) == 0)
        def _():
            or_ref[...] = jnp.zeros_like(or_ref)
            oi_ref[...] = jnp.zeros_like(oi_ref)

        d = dy_ref[...].astype(BF16)
        or_ref[...] += lax.dot_general(xr_ref[...].astype(BF16), d, tn_dims, preferred_element_type=F32)
        oi_ref[...] += lax.dot_general(xi_ref[...].astype(BF16), d, tn_dims, preferred_element_type=F32)

    x_spec = pl.BlockSpec((tm, SW), lambda j, i: (i, j))
    o_spec = pl.BlockSpec((None, SW, LANES), lambda j, i: (j, 0, 0))
    return _pcall(body, name="ssm_dc", grid=(J, L // tm),
                  in_specs=[x_spec, x_spec, pl.BlockSpec((tm, LANES), lambda j, i: (i, j))],
                  out_specs=[o_spec, o_spec], out_shape=[_sds((J, SW, LANES), F32)] * 2,
                  compiler_params=_params(("parallel", "arbitrary")))(xr, xi, dy)


SCAN_COLS = 1024
SCAN_TILES = SCAN_COLS // LANES
SCAN_CH = SCAN_COLS // STATE * GROUP_CH
SCAN_SLAB = 8 * STATE


def _scan_rows(b_r, b_i, tab_ref, carry, reverse):
    sr, si = b_r, b_i
    for k, d in enumerate(SCAN_SHIFTS):
        mr, mi = tab_ref[k], tab_ref[4 + k]
        sh = (8 - d) if reverse else d
        pr, pi = pltpu.roll(sr, sh, 0), pltpu.roll(si, sh, 0)
        sr, si = sr + mr * pr - mi * pi, si + mr * pi + mi * pr
    pwr, pwi = tab_ref[3], tab_ref[7]
    xr = sr + pwr * carry[0] - pwi * carry[1]
    xi = si + pwr * carry[1] + pwi * carry[0]
    row = 0 if reverse else 7
    return xr, xi, (jnp.broadcast_to(xr[row:row + 1, :], xr.shape), jnp.broadcast_to(xi[row:row + 1, :], xi.shape))


def _ssm_fwd(proj, u_off, s, dvec):
    L = proj.shape[0]
    layer, depth = s['layer'], s['depth']
    T, J = s['wb'][0].shape[0] // depth, s['wc'][0].shape[0] // depth
    Wx, nC = T * LANES, T * LANES // SCAN_COLS
    TB = _pick(L, (384, 256, 128))
    nT, nG = L // TB, TB // 8

    def body(u_ref, wbr_ref, wbi_ref, wcr_ref, wci_ref, d_ref, tab_ref, xr_ref, xi_ref, y_ref, gl_ref, cr_ref, ci_ref):
        @pl.when(pl.program_id(1) == 0)
        def _():
            cr_ref[...] = jnp.zeros_like(cr_ref)
            ci_ref[...] = jnp.zeros_like(ci_ref)

        u = u_ref[...]
        u16 = u.astype(BF16)
        for k in range(SCAN_TILES):
            blk = u16[:, (k // TILES_PER_BLOCK) * LANES:(k // TILES_PER_BLOCK + 1) * LANES]
            cols = slice(k * LANES, (k + 1) * LANES)
            xr_ref[:, cols] = jnp.dot(blk, wbr_ref[k], preferred_element_type=F32)
            xi_ref[:, cols] = jnp.dot(blk, wbi_ref[k], preferred_element_type=F32)

        def step(q, carry):
            rows = pl.ds(pl.multiple_of(q * 8, 8), 8)
            xr, xi, carry = _scan_rows(xr_ref[rows, :], xi_ref[rows, :], tab_ref, carry, False)
            xr_ref[rows, :] = xr
            xi_ref[rows, :] = xi
            return carry

        cr, ci = lax.fori_loop(0, nG, step, (cr_ref[...], ci_ref[...]), unroll=2)
        cr_ref[...] = cr
        ci_ref[...] = ci
        for c in range(SCAN_CH // LANES):
            slab, ch = slice(c * SCAN_SLAB, (c + 1) * SCAN_SLAB), slice(c * LANES, (c + 1) * LANES)
            acc = jnp.dot(xr_ref[:, slab].astype(BF16), wcr_ref[c], preferred_element_type=F32)
            acc += jnp.dot(xi_ref[:, slab].astype(BF16), wci_ref[c], preferred_element_type=F32)
            y = acc + d_ref[:, ch] * u[:, ch]
            y_ref[:, ch] = y
            gl_ref[:, ch] = _gelu(y)

    x_spec = pl.BlockSpec((TB, SCAN_COLS), lambda j, t: (t, j))
    y_spec = pl.BlockSpec((TB, SCAN_CH), lambda j, t: (t, j))
    tile_w = pl.BlockSpec((SCAN_TILES, LANES, LANES), lambda j, t: (layer * nC + j, 0, 0))
    slab_w = pl.BlockSpec((SCAN_CH // LANES, SCAN_SLAB, LANES), lambda j, t: (layer * nC + j, 0, 0))
    return _pcall(body, name="ssm_fwd", grid=(nC, nT),
                  in_specs=[pl.BlockSpec((TB, SCAN_CH), lambda j, t: (t, u_off * LANES // SCAN_CH + j)), tile_w, tile_w, slab_w, slab_w,
                            pl.BlockSpec((1, SCAN_CH), lambda j, t: (0, j)),
                            pl.BlockSpec((8, 8, SCAN_COLS), lambda j, t: (0, 0, layer * nC + j))],
                  out_specs=[x_spec, x_spec, y_spec, y_spec],
                  out_shape=[_sds((L, Wx), F32)] * 2 + [_sds((L, J * LANES), F32)] * 2,
                  scratch_shapes=[pltpu.VMEM((8, SCAN_COLS), F32)] * 2,
                  compiler_params=_params(("parallel", "arbitrary")))(proj, *s['wb'], *s['wc'], dvec, s['tf'])


def _ssm_bwd(dy, proj, u_off, xr, xi, s, dvec, dproj):
    L = dy.shape[0]
    layer, depth = s['layer'], s['depth']
    T, J = s['wb'][0].shape[0] // depth, s['wc'][0].shape[0] // depth
    Wx, nC = T * LANES, T * LANES // SCAN_COLS
    TB = _pick(L, (384, 256, 128))
    nT, nG = L // TB, TB // 8
    n_ch = SCAN_CH // LANES
    tn_dims = (((0,), (0,)), ((), ()))

    def body(dy_ref, u_ref, xr_ref, xi_ref, wcr_ref, wci_ref, wbr_ref, wbi_ref, d_ref, tab_ref, buf_ref,
             du_ref, dd_ref, dwbr_ref, dwbi_ref, dwcr_ref, dwci_ref, s1_ref, s2_ref, gr_ref, gi_ref, cr_ref, ci_ref):
        t = pl.program_id(1)

        @pl.when(t == 0)
        def _():
            for ref in (cr_ref, ci_ref, dd_ref, dwbr_ref, dwbi_ref, dwcr_ref, dwci_ref, s1_ref, s2_ref):
                ref[...] = jnp.zeros_like(ref)

        dy = dy_ref[...]
        dy16 = dy.astype(BF16)
        u = u_ref[...]
        u16 = u.astype(BF16)
        for k in range(SCAN_TILES):
            blk = dy16[:, (k // TILES_PER_BLOCK) * LANES:(k // TILES_PER_BLOCK + 1) * LANES]
            cols = slice(k * LANES, (k + 1) * LANES)
            gr_ref[:, cols] = jnp.dot(blk, wcr_ref[k], preferred_element_type=F32)
            gi_ref[:, cols] = jnp.dot(blk, wci_ref[k], preferred_element_type=F32)

        def step(q, carry):
            rows = pl.ds(pl.multiple_of((nG - 1 - q) * 8, 8), 8)
            b_r, b_i = gr_ref[rows, :], gi_ref[rows, :]
            g_r, g_i, edge = _scan_rows(b_r, b_i, tab_ref, carry[:2], True)
            gr_ref[rows, :] = g_r
            gi_ref[rows, :] = g_i
            hr, hi = g_r - b_r, g_i - b_i
            st_r, st_i = xr_ref[rows, :], xi_ref[rows, :]
            return edge + (carry[2] + hr * st_r + hi * st_i, carry[3] + hi * st_r - hr * st_i)

        zero = jnp.zeros((8, SCAN_COLS), F32)
        fin = lax.fori_loop(0, nG, step, (cr_ref[...], ci_ref[...], zero, zero), unroll=2)
        cr_ref[...] = fin[0]
        ci_ref[...] = fin[1]
        s1_ref[...] += fin[2]
        s2_ref[...] += fin[3]

        row = lax.broadcasted_iota(jnp.int32, (TB, LANES), 0) + (nT - 1 - t) * TB
        for c in range(n_ch):
            slab, ch = slice(c * SCAN_SLAB, (c + 1) * SCAN_SLAB), slice(c * LANES, (c + 1) * LANES)
            g16r, g16i = gr_ref[:, slab].astype(BF16), gi_ref[:, slab].astype(BF16)
            acc = jnp.dot(g16r, wbr_ref[c], preferred_element_type=F32) + jnp.dot(g16i, wbi_ref[c], preferred_element_type=F32)
            du_ref[:, ch] = jnp.where(row >= PAD, acc + d_ref[:, ch] * dy[:, ch], 0.0).astype(du_ref.dtype)
            dwcr_ref[c] += lax.dot_general(xr_ref[:, slab].astype(BF16), dy16[:, ch], tn_dims, preferred_element_type=F32)
            dwci_ref[c] += lax.dot_general(xi_ref[:, slab].astype(BF16), dy16[:, ch], tn_dims, preferred_element_type=F32)
            for kk in range(TILES_PER_BLOCK):
                k = c * TILES_PER_BLOCK + kk
                cols = slice(kk * LANES, (kk + 1) * LANES)
                dwbr_ref[k] += lax.dot_general(u16[:, ch], g16r[:, cols], tn_dims, preferred_element_type=F32)
                dwbi_ref[k] += lax.dot_general(u16[:, ch], g16i[:, cols], tn_dims, preferred_element_type=F32)
        dd_ref[...] += jnp.sum(dy * u, axis=0, keepdims=True)

    rev = lambda j, t: (nT - 1 - t, j)
    x_spec = pl.BlockSpec((TB, SCAN_COLS), rev)
    u_spec = pl.BlockSpec((TB, SCAN_CH), lambda j, t: (nT - 1 - t, u_off * LANES // SCAN_CH + j))
    tile_w = pl.BlockSpec((SCAN_TILES, LANES, LANES), lambda j, t: (layer * nC + j, 0, 0))
    slab_w = pl.BlockSpec((n_ch, SCAN_SLAB, LANES), lambda j, t: (layer * nC + j, 0, 0))
    tile_g = pl.BlockSpec((SCAN_TILES, LANES, LANES), lambda j, t: (j, 0, 0))
    slab_g = pl.BlockSpec((n_ch, SCAN_SLAB, LANES), lambda j, t: (j, 0, 0))
    vec = pl.BlockSpec((1, SCAN_CH), lambda j, t: (0, j))
    sums = pl.BlockSpec((8, SCAN_COLS), lambda j, t: (0, j))
    return _pcall(body, name="ssm_bwd", grid=(nC, nT),
                  in_specs=[pl.BlockSpec((TB, SCAN_CH), rev), u_spec, x_spec, x_spec, tile_w, tile_w, slab_w, slab_w, vec,
                            pl.BlockSpec((8, 8, SCAN_COLS), lambda j, t: (0, 0, layer * nC + j)), pl.BlockSpec(memory_space=pl.ANY)],
                  out_specs=[u_spec, vec, tile_g, tile_g, slab_g, slab_g, sums, sums],
                  out_shape=[_sds(dproj.shape, dproj.dtype), _sds((1, J * LANES), F32), _sds((T, LANES, LANES), F32),
                             _sds((T, LANES, LANES), F32), _sds((J, SCAN_SLAB, LANES), F32), _sds((J, SCAN_SLAB, LANES), F32),
                             _sds((8, Wx), F32), _sds((8, Wx), F32)],
                  input_output_aliases={10: 0},
                  scratch_shapes=[pltpu.VMEM((TB, SCAN_COLS), F32)] * 2 + [pltpu.VMEM((8, SCAN_COLS), F32)] * 2,
                  compiler_params=_params(("parallel", "arbitrary")))(dy, proj, xr, xi, *s['wcT'], *s['wbT'], dvec, s['tr'], dproj)


def _glu_dz(ds, gl, z):
    L, W = ds.shape
    tr = _pick(L, (384, 256, 128))

    def body(ds_ref, gl_ref, z_ref, dz_ref, db_ref):
        @pl.when(pl.program_id(0) == 0)
        def _():
            db_ref[...] = jnp.zeros_like(db_ref)

        sg = jax.nn.sigmoid(z_ref[...])
        dz = ds_ref[...] * gl_ref[...] * (sg * (1.0 - sg))
        dz_ref[...] = dz.astype(BF16)
        db_ref[...] += jnp.sum(dz, axis=0, keepdims=True)

    spec = pl.BlockSpec((tr, W), lambda i: (i, 0))
    vec = pl.BlockSpec((1, W), lambda i: (0, 0))
    return _pcall(body, name="glu_dz", grid=(L // tr,), in_specs=[spec] * 3, out_specs=[spec, vec],
                  out_shape=[_sds((L, W), BF16), _sds((1, W), F32)], compiler_params=_params(("arbitrary",)))(ds, gl, z)


def _ssm_param_bwd_flat(lr, li, ls, br, bi, dbbr, dbbi):
    def seg_sum(x):
        for s in (8, 4, 2, 1):
            x = x + pltpu.roll(x, LANES - s, 1)
        return x

    def fn(lr, li, ls, br, bi, dbbr, dbbi):
        fr, fi = _zoh_factor(lr, li, ls)
        return (fr * dbbr + fi * dbbi, fr * dbbi - fi * dbbr,
                seg_sum(br * dbbr + bi * dbbi), seg_sum(br * dbbi - bi * dbbr))

    return _ew("ssm_param_bwd_flat", fn, [lr, li, ls, br, bi, dbbr, dbbi], [F32] * 4)


def _ssm_param_bwd(lr, li, ls, dfr, dfi, s1, s2):
    G, P = lr.shape

    def body(lr_ref, li_ref, ls_ref, dfr_ref, dfi_ref, s1_ref, s2_ref, dlr_ref, dli_ref, dls_ref):
        lr, li = lr_ref[...], li_ref[...]
        ar, ai, dl = _lam_bar(lr, li, ls_ref[...])
        sr, si = s1_ref[0], s2_ref[0]
        for k in range(1, 8):
            sr = sr + s1_ref[k]
            si = si + s2_ref[k]
        a2 = ar * ar + ai * ai
        gar, gai = (sr * ar - si * ai) / a2, (sr * ai + si * ar) / a2
        n2 = lr * lr + li * li
        ivr, ivi = lr / n2, -li / n2
        fr = (ar - 1.0) * ivr - ai * ivi
        fi = (ar - 1.0) * ivi + ai * ivr
        dfr, dfi = dfr_ref[...], dfi_ref[...]
        gar = gar + ivr * dfr + ivi * dfi
        gai = gai + ivr * dfi - ivi * dfr
        wr, wi = -(fr * ivr - fi * ivi), -(fr * ivi + fi * ivr)
        glr, gli = wr * dfr + wi * dfi, wr * dfi - wi * dfr
        gzr, gzi = ar * gar + ai * gai, ar * gai - ai * gar
        dlr_ref[...] = glr + dl * gzr
        dli_ref[...] = gli + dl * gzi
        dls_ref[...] = dl * jnp.sum(lr * gzr + li * gzi, axis=-1, keepdims=True)

    m = pl.BlockSpec((G, P), lambda: (0, 0))
    v = pl.BlockSpec((G, 1), lambda: (0, 0))
    s = pl.BlockSpec((8, G, P), lambda: (0, 0, 0))
    return _pcall(body, name="ssm_param_bwd", in_specs=[m, m, v, m, m, s, s], out_specs=[m, m, v],
                  out_shape=[_sds((G, P), F32), _sds((G, P), F32), _sds((G, 1), F32)])(lr, li, ls, dfr, dfi, s1, s2)


def _tile_mask(G):
    T = G // 2
    e = np.zeros((T, 8, 1, 2, 1), np.float32)
    for t in range(T):
        for c in range(2):
            e[t, (2 * t + c) % 8, 0, c, 0] = 1.0
    return e


def _tile_w(arr):
    T = arr.shape[0] // 2
    a = arr.reshape(T, 2, STATE, GROUP_CH).transpose(0, 3, 1, 2).reshape(T, 1, GROUP_CH, LANES).astype(BF16)
    shape = (T, 8, 1, LANES)
    t, e, lane = (lax.broadcasted_iota(jnp.int32, shape, d) for d in (0, 1, 3))
    return jnp.where(e == (2 * t + lane // STATE) % 8, a, 0).reshape(T, LANES, LANES)


def _tile_w_grad(dw):
    G = dw.shape[0] * 2
    d = dw.reshape(G // 2, 8, GROUP_CH, 2, STATE) * _tile_mask(G)
    return d.sum(axis=1).transpose(0, 2, 3, 1).reshape(G, STATE, GROUP_CH)


def _slab_w(arr):
    J = arr.shape[0] // 8
    a = jnp.tile(arr.reshape(J, 8, STATE, GROUP_CH).astype(BF16), (1, 1, 1, 8))
    g, lane = (lax.broadcasted_iota(jnp.int32, (1, 8, 1, LANES), d) for d in (1, 3))
    return jnp.where(g == lane // GROUP_CH, a, 0).reshape(J, 8 * STATE, LANES)


def _slab_w_grad(dw):
    J = dw.shape[0]
    eye = np.eye(8, dtype=np.float32).reshape(1, 8, 1, 8, 1)
    return (dw.reshape(J, 8, STATE, 8, GROUP_CH) * eye).sum(axis=3).reshape(J * 8, STATE, GROUP_CH)


def _exchange(name, ins, out_sds, remote, local, aliases=None):
    n_in, n_out, n_r, n_l = len(ins), len(out_sds), len(remote), len(local)

    def body(*refs):
        in_refs, out_refs = refs[:n_in], refs[n_in:n_in + n_out]
        send_sems, recv_sems, local_sems = refs[n_in + n_out:]
        x, y, c = lax.axis_index("x"), lax.axis_index("y"), lax.axis_index("c")

        def place(px, py, pc):
            return dict(x=px, y=py, c=pc, chip=2 * px + py)

        def flip(mask):
            mx, my, mc = mask
            return ((1 - x) if mx else x, (1 - y) if my else y, (1 - c) if mc else c)

        me = place(x, y, c)
        sends = []
        for k, (ii, src, oi, dst, mask) in enumerate(remote):
            cp = pltpu.make_async_remote_copy(src_ref=src(in_refs[ii], me), dst_ref=dst(out_refs[oi], me),
                                              send_sem=send_sems.at[k], recv_sem=recv_sems.at[k],
                                              device_id=flip(mask), device_id_type=MESH)
            cp.start()
            sends.append(cp)
        locals_ = []
        for k, (ii, src, oi, dst) in enumerate(local):
            cp = pltpu.make_async_copy(src(in_refs[ii], me), dst(out_refs[oi], me), local_sems.at[k])
            cp.start()
            locals_.append(cp)
        for k, (ii, src, oi, dst, mask) in enumerate(remote):
            sends[k].wait_send()
            peer = flip(mask)
            pltpu.make_async_remote_copy(src_ref=src(in_refs[ii], me), dst_ref=dst(out_refs[oi], place(*peer)),
                                         send_sem=send_sems.at[k], recv_sem=recv_sems.at[k],
                                         device_id=peer, device_id_type=MESH).wait_recv()
        for cp in locals_:
            cp.wait()

    any_spec = pl.BlockSpec(memory_space=pl.ANY)
    return _pcall(body, name=name, in_specs=[any_spec] * n_in, out_specs=[any_spec] * n_out, out_shape=list(out_sds),
                  input_output_aliases=aliases or {},
                  scratch_shapes=[pltpu.SemaphoreType.DMA((n_r,)), pltpu.SemaphoreType.DMA((n_r,)),
                                  pltpu.SemaphoreType.DMA((max(n_l, 1),))])(*ins)


def _mesh_place():
    x, y, c = lax.axis_index("x"), lax.axis_index("y"), lax.axis_index("c")

    def place(px, py, pc):
        return dict(x=px, y=py, c=pc, chip=2 * px + py)

    def flip(mask):
        mx, my, mc = mask
        return ((1 - x) if mx else x, (1 - y) if my else y, (1 - c) if mc else c)

    return place(x, y, c), place, flip


_HBM = pl.BlockSpec(memory_space=pltpu.HBM)
_SEM = pl.BlockSpec(memory_space=pltpu.SEMAPHORE)
_EFFECT = pltpu.SideEffectType.DATAFLOW_SIDE_EFFECTING


def _split_start(name, bufs, groups):
    n, ng = len(bufs), len(groups)

    def body(*refs):
        in_refs, sems, token = refs[:n], refs[n:n + 2 * ng], refs[-1]
        me, _, flip = _mesh_place()
        for g, copies in enumerate(groups):
            for k, (si, src, di, dst, mask) in enumerate(copies):
                pltpu.make_async_remote_copy(src_ref=src(in_refs[si], me), dst_ref=dst(in_refs[di], me),
                                             send_sem=sems[2 * g].at[k], recv_sem=sems[2 * g + 1].at[k],
                                             device_id=flip(mask), device_id_type=MESH).start()
        token[...] = jnp.zeros_like(token)

    outs = _pcall(body, name=name,
                  out_shape=(*[pltpu.SemaphoreType.DMA((len(g),)) for g in groups for _ in range(2)],
                             *[pltpu.HBM(b.shape, b.dtype) for b in bufs], _sds((8, LANES), F32)),
                  in_specs=[_HBM] * n, out_specs=(*[_SEM] * (2 * ng), *[_HBM] * n, pl.BlockSpec(memory_space=pltpu.VMEM)),
                  input_output_aliases={i: 2 * ng + i for i in range(n)},
                  compiler_params=pltpu.CompilerParams(has_side_effects=_EFFECT),
                  )(*[pltpu.with_memory_space_constraint(b, pltpu.HBM) for b in bufs])
    return [(outs[2 * g], outs[2 * g + 1]) for g in range(ng)], list(outs[2 * ng:2 * ng + n]), outs[-1]


def _split_wait(name, bufs, sems, after, remote):
    n = len(bufs)
    send_sems, recv_sems = sems

    def body(*refs):
        in_refs, ssem, rsem = refs[:n], refs[n], refs[n + 1]
        me, place, flip = _mesh_place()
        for k, (si, src, di, dst, mask) in enumerate(remote):
            peer = flip(mask)
            cp = pltpu.make_async_remote_copy(src_ref=src(in_refs[si], me), dst_ref=dst(in_refs[di], place(*peer)),
                                              send_sem=ssem.at[k], recv_sem=rsem.at[k], device_id=peer, device_id_type=MESH)
            cp.wait_send()
            cp.wait_recv()

    return list(_pcall(body, name=name, out_shape=tuple(pltpu.HBM(b.shape, b.dtype) for b in bufs),
                       in_specs=[_HBM] * n + [_SEM, _SEM, pl.BlockSpec(memory_space=pl.ANY)], out_specs=tuple([_HBM] * n),
                       input_output_aliases={i: i for i in range(n)},
                       compiler_params=pltpu.CompilerParams(has_side_effects=_EFFECT))(*bufs, send_sems, recv_sems, after))


CHIP_MASKS = ((0, 1, 0), (1, 0, 0), (1, 1, 0))
SIBLING = (0, 0, 1)


def _whole(ref, p):
    return ref


def _all_gather(name, shards, col_sharded):
    def dst_view(col):
        def view(ref, p):
            r, cdim = ref.shape[0] // (1 if col else N_CHIPS), ref.shape[1] // (N_CHIPS if col else 1)
            if col:
                return ref.at[:, pl.ds(pl.multiple_of(p["chip"] * cdim, LANES), cdim)]
            return ref.at[pl.ds(pl.multiple_of(p["chip"] * r, 8), r), :]
        return view

    out_sds = [_sds((s.shape[0], s.shape[1] * N_CHIPS) if col else (s.shape[0] * N_CHIPS, s.shape[1]), s.dtype)
               for s, col in zip(shards, col_sharded)]
    remote = [(a, _whole, a, dst_view(col), m) for a, col in enumerate(col_sharded) for m in CHIP_MASKS]
    local = [(a, _whole, a, dst_view(col)) for a, col in enumerate(col_sharded)]
    return _exchange(name, shards, out_sds, remote, local)


class _Place:
    def __getitem__(self, k):
        return lax.axis_index("c") if k == 0 else 2 * lax.axis_index("x") + lax.axis_index("y")


def _placed_call(body, name, grid, in_specs, out_specs, out_shape, sem, ins):
    def wrap(spec):
        return pl.BlockSpec(spec.block_shape, lambda *idx: spec.index_map(*idx, _Place()))

    outs = [wrap(s) for s in out_specs] if isinstance(out_specs, (list, tuple)) else wrap(out_specs)
    return _pcall(body, name=name, grid=grid, in_specs=[wrap(s) for s in in_specs], out_specs=outs, out_shape=out_shape,
                  compiler_params=_params(sem))(*ins)


def _rows_within(n, width, limit=512 * 1024):
    return _pick(n, tuple(t for t in (1024, 512, 256, 128, 64, 32, 16) if t * width <= limit) or (16,))


def _region_view(col):
    def view(ref, p):
        if col:
            cdim = ref.shape[1] // N_CHIPS
            return ref.at[:, pl.ds(pl.multiple_of(p["chip"] * cdim, LANES), cdim)]
        r = ref.shape[0] // N_CHIPS
        return ref.at[pl.ds(pl.multiple_of(p["chip"] * r, 16), r), :]
    return view


def _ag_place(name, w, layer, col, dtype):
    _, r, cdim = w.shape
    tr = _rows_within(r, cdim)
    nb = r // tr

    def body(w_ref, o_ref):
        o_ref[...] = w_ref[...].astype(dtype)

    if col:
        out_shape, out_spec = (r, N_CHIPS * cdim), pl.BlockSpec((tr, cdim), lambda i, pr: (i, pr[1]))
    else:
        out_shape, out_spec = (N_CHIPS * r, cdim), pl.BlockSpec((tr, cdim), lambda i, pr: (pr[1] * nb + i, 0))
    return _placed_call(body, name, (nb,), [pl.BlockSpec((None, tr, cdim), lambda i, pr: (layer, i, 0))], out_spec,
                        _sds(out_shape, dtype), ("parallel",), [w])


def _ag_copies(a, col):
    return [(a, _region_view(col), a, _region_view(col), m) for m in CHIP_MASKS]


def _rs_add2(name, g4, a4, out_dtype):
    J, _, h, C = g4.shape
    tr = _rows_within(h, C)

    def body(g_ref, a_ref, o_ref):
        o_ref[...] = (g_ref[...].astype(F32) + a_ref[...].astype(F32)).astype(o_ref.dtype)

    return _placed_call(body, name, (J, h // tr),
                        [pl.BlockSpec((None, None, tr, C), lambda j, i, pr: (j, pr[0], i, 0)),
                         pl.BlockSpec((None, None, tr, C), lambda j, i, pr: (j, 0, i, 0))],
                        pl.BlockSpec((None, tr, C), lambda j, i, pr: (j, i, 0)), _sds((J, h, C), out_dtype),
                        ("parallel", "parallel"), [g4, a4])


def _rs_add4(name, p3, landed, col):
    _, h, w = landed.shape
    tr = _rows_within(h, w)

    def body(p_ref, a_ref, b_ref, c_ref, o_ref):
        o_ref[...] = ((p_ref[...].astype(F32) + a_ref[...].astype(F32)) + b_ref[...].astype(F32)) + c_ref[...].astype(F32)

    own = (pl.BlockSpec((None, tr, w), lambda i, pr: (0, i, pr[1])) if col else pl.BlockSpec((None, tr, w), lambda i, pr: (pr[1], i, 0)))
    slot = lambda k: pl.BlockSpec((None, tr, w), lambda i, pr: (k, i, 0))
    return _placed_call(body, name, (h // tr,), [own, slot(0), slot(1), slot(2)], pl.BlockSpec((tr, w), lambda i, pr: (i, 0)),
                        _sds((h, w), F32), ("parallel",), [p3, landed, landed, landed])


def _rs_start(tag, grads, col_sharded):
    n = len(grads)
    g4 = [g.reshape((1, 2, g.shape[0] // 2, g.shape[1]) if col else (N_CHIPS, 2, g.shape[0] // (2 * N_CHIPS), g.shape[1]))
          for g, col in zip(grads, col_sharded)]
    other_half = lambda ref, p: ref.at[:, pl.ds(1 - p["c"], 1)]
    landing = [lax.empty((g.shape[0], 1) + g.shape[2:], g.dtype) for g in g4]
    copies = [(a, other_half, n + a, _whole, SIBLING) for a in range(n)]
    (sems,), bufs, token = _split_start("rs_sibling_start_" + tag, g4 + landing, [copies])
    return dict(tag=tag, stage=0, sems=sems, bufs=bufs, copies=copies, token=token, col_sharded=list(col_sharded))


def _rs_advance(st, after):
    col_sharded, tag = st['col_sharded'], st['tag']
    n = len(col_sharded)
    stage = st['stage']
    st['stage'] = stage + 1
    if stage == 0:
        bufs = _split_wait("rs_sibling_wait_" + tag, st['bufs'], st['sems'], after, st['copies'])
        st.update(_rs_chips_start(tag, [_rs_add2("rs_add2_w", bufs[a], bufs[n + a], BF16) for a in range(n)], col_sharded))
    elif stage == 1:
        bufs = _split_wait("rs_chips_wait_" + tag, st['bufs'], st['sems'], after, st['copies'])
        mine = [_rs_add4("rs_add4_w", bufs[a], bufs[n + a], col_sharded[a]) for a in range(n)]
        copies = [(a, _whole, n + a, _whole, SIBLING) for a in range(n)]
        (sems,), bufs, token = _split_start("rs_halves_start_" + tag, mine + [lax.empty(m.shape, F32) for m in mine], [copies])
        st.update(sems=sems, bufs=bufs, copies=copies, token=token)
    else:
        bufs = _split_wait("rs_halves_wait_" + tag, st['bufs'], st['sems'], after, st['copies'])
        st['result'] = (bufs[:n], bufs[n:])


def _rs_chips_start(tag, chip_sum, col_sharded):
    n = len(chip_sum)

    def send_view(col, mask):
        def view(ref, p):
            t = 2 * ((1 - p["x"]) if mask[0] else p["x"]) + ((1 - p["y"]) if mask[1] else p["y"])
            if col:
                sc = ref.shape[2] // N_CHIPS
                return ref.at[0, :, pl.ds(pl.multiple_of(t * sc, LANES), sc)]
            return ref.at[t]
        return view
    slot = lambda k: (lambda ref, p: ref.at[k])
    piece = [(s.shape[1], s.shape[2] // N_CHIPS if col else s.shape[2]) for s, col in zip(chip_sum, col_sharded)]
    landing = [lax.empty((len(CHIP_MASKS),) + s, BF16) for s in piece]
    copies = [(a, send_view(col_sharded[a], m), n + a, slot(k), m) for a in range(n) for k, m in enumerate(CHIP_MASKS)]
    (sems,), bufs, token = _split_start("rs_chips_start_" + tag, chip_sum + landing, [copies])
    return dict(sems=sems, bufs=bufs, copies=copies, token=token)


def _adamw_big(name, mine, other, w, m, v):
    depth, R, C = w.shape
    h = R // 2
    tr = _pick(h, tuple(t for t in (512, 256, 128, 64, 32, 16, 8) if t * C <= 256 * 1024) or (8,))
    nb = h // tr

    def g_spec(kk, hh):
        def imap(l, s, i, pr):
            before = (l < kk) | ((l == kk) & (s < hh))
            return (jnp.where((l == kk) & (s == hh), i, jnp.where(before, 0, nb - 1)), 0)
        return pl.BlockSpec((tr, C), imap)

    st_spec = pl.BlockSpec((None, tr, C), lambda l, s, i, pr: (l, jnp.where(s == 0, pr[0], 1 - pr[0]) * nb + i, 0))

    def body(*refs):
        g_refs = refs[:2 * depth]
        w_ref, m_ref, v_ref, go_ref, d_ref, mo_ref, vo_ref = refs[2 * depth:]
        l, s = pl.program_id(0), pl.program_id(1)
        for kk in range(depth):
            for hh in range(2):
                @pl.when((l == kk) & (s == hh))
                def _(kk=kk, hh=hh):
                    g = g_refs[2 * kk + hh][...]
                    d, mn, vn = _adam_math(w_ref[...], g, m_ref[...], v_ref[...])
                    go_ref[...] = g
                    d_ref[...] = d
                    mo_ref[...] = mn
                    vo_ref[...] = vn

    gs, g_specs = [], []
    for kk in range(depth):
        gs += [mine[kk], other[kk]]
        g_specs += [g_spec(kk, 0), g_spec(kk, 1)]
    return _placed_call(body, name, (depth, 2, nb), g_specs + [st_spec] * 3, [st_spec] * 4, [_sds(w.shape, F32)] * 4,
                        ("arbitrary", "arbitrary", "arbitrary"), gs + [w, m, v])


def _piece_view(col, j, other):
    def view(ref, p):
        R, C = ref.shape
        cc = (1 - p["c"]) if other else p["c"]
        if col:
            hr, sc = R // 2, C // N_CHIPS
            return ref.at[pl.ds(pl.multiple_of(cc * hr, 16), hr), pl.ds(j * sc, sc)]
        hr = R // (2 * N_CHIPS)
        return ref.at[pl.ds(pl.multiple_of((2 * j + cc) * hr, 8), hr), :]
    return view


def _piece_shape(shape, col):
    R, C = shape
    return (R // 2, C // N_CHIPS) if col else (R // (2 * N_CHIPS), C)


def _reduce_scatter(tag, grads, col_sharded, wire_dtype):
    n = len(grads)
    shapes = [_piece_shape(g.shape, col) for g, col in zip(grads, col_sharded)]

    slot = lambda j: (lambda ref, p: ref.at[j])
    remote = [(a, _piece_view(col_sharded[a], j, True), a, slot(j), SIBLING) for a in range(n) for j in range(N_CHIPS)]
    local = [(a, _piece_view(col_sharded[a], j, False), n + a, slot(j)) for a in range(n) for j in range(N_CHIPS)]
    got = _exchange("rs_sibling_" + tag, grads, [_sds((N_CHIPS,) + s, g.dtype) for s, g in zip(shapes, grads)] * 2, remote, local)
    theirs, mine = got[:n], got[n:]
    chip_sum = [_ew("rs_add2_" + tag, lambda a, b: (a.astype(F32) + b.astype(F32),),
                    [m.reshape(-1, m.shape[-1]), t.reshape(-1, t.shape[-1])], [wire_dtype])[0].reshape(m.shape)
                for m, t in zip(mine, theirs)]

    def send_view(mask):
        return lambda ref, p: ref.at[2 * ((1 - p["x"]) if mask[0] else p["x"]) + ((1 - p["y"]) if mask[1] else p["y"])]
    remote = [(a, send_view(m), a, slot(k), m) for a in range(n) for k, m in enumerate(CHIP_MASKS)]
    local = [(a, lambda ref, p: ref.at[p["chip"]], n + a, _whole) for a in range(n)]
    got = _exchange("rs_chips_" + tag, chip_sum,
                    [_sds((len(CHIP_MASKS),) + s, wire_dtype) for s in shapes] + [_sds(s, wire_dtype) for s in shapes], remote, local)
    landed, own = got[:n], got[n:]
    half = [_ew("rs_add4_" + tag, lambda o, a, b, c: (((o.astype(F32) + a.astype(F32)) + b.astype(F32)) + c.astype(F32),),
                [o, l[0], l[1], l[2]], [F32])[0] for o, l in zip(own, landed)]

    def half_rows(ref, p):
        hr = ref.shape[0] // 2
        return ref.at[pl.ds(pl.multiple_of(p["c"] * hr, 8), hr), :]
    remote = [(a, _whole, a, half_rows, SIBLING) for a in range(n)]
    local = [(a, _whole, a, half_rows) for a in range(n)]
    return _exchange("rs_halves_" + tag, half, [_sds((2 * s[0], s[1]), F32) for s in shapes], remote, local)


def _ssm_prepare(W):
    lr, li, ls = W['ssm_lambda_re'], W['ssm_lambda_im'], W['ssm_log_step']
    depth, G = ls.shape
    GG = depth * G
    flat = lambda a: a.reshape(-1, LANES)
    bc = lambda a: flat(jnp.broadcast_to(a, (depth, G, STATE, GROUP_CH)))
    flat3 = (bc(lr[..., None]), bc(li[..., None]), bc(ls[:, :, None, None]))
    bbr, bbi = _ssm_bbar(*flat3, flat(W['ssm_b_re']), flat(W['ssm_b_im']))
    bbr, bbi = bbr.reshape(GG, STATE, GROUP_CH), bbi.reshape(GG, STATE, GROUP_CH)
    row = lambda a: a.reshape(1, GG * STATE)
    tf, tr = _ssm_tables(row(lr), row(li), row(jnp.broadcast_to(ls[..., None], (depth, G, STATE))))
    cr = W['ssm_c_re'].reshape(GG, GROUP_CH, STATE).transpose(0, 2, 1)
    ci = -W['ssm_c_im'].reshape(GG, GROUP_CH, STATE).transpose(0, 2, 1)
    stacked = dict(wb=(_tile_w(bbr), _tile_w(bbi)), wbT=(_slab_w(bbr), _slab_w(bbi)),
                   wc=(_slab_w(cr), _slab_w(ci)), wcT=(_tile_w(cr), _tile_w(ci)))
    return flat3, [dict(stacked, tf=tf, tr=tr, layer=l, depth=depth) for l in range(depth)]


def _ssm_param_grads(W, flat3, raw):
    depth, G = W['ssm_log_step'].shape
    GG = depth * G
    cat = lambda k: jnp.concatenate([r[k] for r in raw], axis=0)
    flat = lambda a: a.reshape(-1, LANES)
    out = {}
    out['ssm_c_re'] = _slab_w_grad(cat(2)).transpose(0, 2, 1).reshape(W['ssm_c_re'].shape)
    out['ssm_c_im'] = -_slab_w_grad(cat(3)).transpose(0, 2, 1).reshape(W['ssm_c_im'].shape)
    dbr, dbi, qr, qi = _ssm_param_bwd_flat(*flat3, flat(W['ssm_b_re']), flat(W['ssm_b_im']),
                                           flat(_tile_w_grad(cat(0))), flat(_tile_w_grad(cat(1))))
    out['ssm_b_re'], out['ssm_b_im'] = dbr.reshape(W['ssm_b_re'].shape), dbi.reshape(W['ssm_b_im'].shape)
    pick = lambda q: q[:, ::GROUP_CH].reshape(GG, STATE)
    sums = lambda k: jnp.concatenate([r[k].reshape(8, G, STATE) for r in raw], axis=1)
    dlr, dli, dls = _ssm_param_bwd(W['ssm_lambda_re'].reshape(GG, STATE), W['ssm_lambda_im'].reshape(GG, STATE),
                                   W['ssm_log_step'].reshape(GG, 1), pick(qr), pick(qi), sums(4), sums(5))
    out['ssm_lambda_re'], out['ssm_lambda_im'] = dlr.reshape(depth, G, STATE), dli.reshape(depth, G, STATE)
    out['ssm_log_step'] = dls.reshape(depth, G)
    return out


def _layer_fwd(x, p, weight, dims):
    attn_w, kv_w, u_off = dims['attn_w'], dims['kv_w'], dims['u_off']
    s = p['s5']
    h = _rms_fwd("norm_mix", [x], [p['norm_mix_g']], BF16)
    w = {'w_in': weight('w_in', h)}
    proj, = _mm("mm_in", h, w['w_in'], 'nn', [F32])
    qn, kn, rq = _qk_norm(proj, p['q_norm_g'], p['k_norm_g'], attn_w, kv_w)
    attn = _attn_fwd(proj, qn, kn, p['attn_sinks'], attn_w, kv_w)
    xr, xi, y, gl = _ssm_fwd(proj, u_off, s, p['ssm_d'])
    w['w_glu'] = weight('w_glu', gl)
    ssm, z = _mm("mm_glu", gl, w['w_glu'], 'nn', [F32, F32], extras=[('row', p['b_glu']), ('tile', gl)],
                 epi=lambda acc, b, g: ((lambda zz: (g * jax.nn.sigmoid(zz), zz))(acc + b)))
    mix = _rms_fwd("norm_heads", [attn, ssm], [p['attn_out_g'], p['ssm_out_g']], BF16)
    w['w_out'] = weight('w_out', mix)
    x_mid, = _mm("mm_out", mix, w['w_out'], 'nn', [F32], extras=[('tile', x)], epi=lambda acc, r: (acc + r,))
    h2 = _rms_fwd("norm_mlp", [x_mid], [p['norm_mlp_g']], BF16)
    w['w_up'] = weight('w_up', h2)
    a, r = _mm("mm_up", h2, w['w_up'], 'nn', [F32, BF16],
               epi=lambda acc: (acc, jnp.square(jnp.maximum(acc, 0.0))))
    w['w_down'] = weight('w_down', r)
    x_out, = _mm("mm_down", r, w['w_down'], 'nn', [F32], extras=[('tile', x_mid)], epi=lambda acc, rr: (acc + rr,))
    saved = dict(x=x, h=h, proj=proj, qn=qn, kn=kn, rq=rq, attn=attn, xr=xr, xi=xi, y=y, gl=gl, z=z, ssm=ssm, mix=mix, x_mid=x_mid,
                 h2=h2, a=a, r=r, w=w)
    return x_out, saved


def _layer_bwd(dx, dx16, sv, p, dims, reduce_grads, tick, token_in):
    attn_w, kv_w, u_off = dims['attn_w'], dims['kv_w'], dims['u_off']
    s, w = p['s5'], sv['w']
    gb, gs = {}, {}
    da, = _mm("mm_down_dx", dx16, w['w_down'], 'nt', [BF16], extras=[('tile', sv['a'])],
              epi=lambda acc, a: (acc * (2.0 * jnp.maximum(a, 0.0)),))
    gb['w_down'], = _mm("mm_down_dw", sv['r'], dx16, 'tn', [BF16])
    dh2, = _mm("mm_up_dx", da, w['w_up'], 'nt', [F32])
    gb['w_up'], = _mm("mm_up_dw", sv['h2'], da, 'tn', [BF16])
    token = reduce_grads(('w_up', 'w_down'), [gb['w_up'], gb['w_down']]) + token_in
    (dx_mid,), (gs['norm_mlp_g'],), dx_mid16 = _rms_bwd("norm_mlp_bwd", [sv['x_mid']], [p['norm_mlp_g'] + token], dh2, resid=dx)
    dmix, = _mm("mm_out_dx", dx_mid16, w['w_out'], 'nt', [F32])
    gb['w_out'], = _mm("mm_out_dw", sv['mix'], dx_mid16, 'tn', [BF16])
    token = tick(dmix)
    (dattn, dssm), (gs['attn_out_g'], gs['ssm_out_g']) = _rms_bwd(
        "norm_heads_bwd", [sv['attn'], sv['ssm']], [p['attn_out_g'] + token, p['ssm_out_g']], dmix)
    dz, gs['b_glu'] = _glu_dz(dssm, sv['gl'], sv['z'])
    dy, = _mm("mm_glu_dx", dz, w['w_glu'], 'nt', [F32], extras=[('tile', dssm), ('tile', sv['z']), ('tile', sv['y'])],
              epi=lambda acc, ds, z, y: ((acc + ds * jax.nn.sigmoid(z)) * _gelu_grad(y),))
    gb['w_glu'], = _mm("mm_glu_dw", sv['gl'], dz, 'tn', [BF16])
    token = tick(dy)
    dproj, dkn, dv, gs['q_norm_g'], gs['attn_sinks'] = _attn_bwd(sv['proj'], sv['qn'], sv['kn'], sv['rq'], sv['attn'], dattn,
                                                                  p['q_norm_g'] + token, p['attn_sinks'], attn_w, kv_w)
    dproj, gs['k_norm_g'] = _knorm_bwd(sv['proj'], dkn, dv, p['k_norm_g'], dproj, attn_w, kv_w)
    dproj, gs['ssm_d'], *gs['s5_raw'] = _ssm_bwd(dy, sv['proj'], u_off, sv['xr'], sv['xi'], s, p['ssm_d'], dproj)
    dh, = _mm("mm_in_dx", dproj, w['w_in'], 'nt', [F32])
    gb['w_in'], = _mm("mm_in_dw", sv['h'], dproj, 'tn', [BF16])
    token = reduce_grads(('w_in', 'w_glu', 'w_out'), [gb['w_in'], gb['w_glu'], gb['w_out']])
    (dx_in,), (gs['norm_mix_g'],), dx_in16 = _rms_bwd("norm_mix_bwd", [sv['x']], [p['norm_mix_g'] + token], dh, resid=dx_mid)
    return dx_in, dx_in16, gs


PACK_COLS = 1024


def _pack(arrs, rows):
    flat = jnp.concatenate([a.reshape(-1).astype(F32) for a in arrs])
    return jnp.pad(flat, (0, rows * PACK_COLS - flat.shape[0])).reshape(rows, PACK_COLS)


def _unpack(packed, shapes):
    flat = packed.reshape(-1)
    out, off = [], 0
    for s in shapes:
        n = int(np.prod(s))
        out.append(flat[off:off + n].reshape(s))
        off += n
    return out


def _pack_rows(shapes, multiple):
    n = sum(int(np.prod(s)) for s in shapes)
    rows = -(-n // PACK_COLS)
    return -(-rows // multiple) * multiple


def kernel(x, meta_tokens, norm_mix_g, w_in, q_norm_g, k_norm_g, attn_sinks, ssm_lambda_re, ssm_lambda_im, ssm_log_step, ssm_b_re, ssm_b_im, ssm_c_re, ssm_c_im, ssm_d, w_glu, b_glu, attn_out_g, ssm_out_g, w_out, norm_mlp_g, w_up, w_down, loss_target, m_meta_tokens, m_norm_mix_g, m_w_in, m_q_norm_g, m_k_norm_g, m_attn_sinks, m_ssm_lambda_re, m_ssm_lambda_im, m_ssm_log_step, m_ssm_b_re, m_ssm_b_im, m_ssm_c_re, m_ssm_c_im, m_ssm_d, m_w_glu, m_b_glu, m_attn_out_g, m_ssm_out_g, m_w_out, m_norm_mlp_g, m_w_up, m_w_down, v_meta_tokens, v_norm_mix_g, v_w_in, v_q_norm_g, v_k_norm_g, v_attn_sinks, v_ssm_lambda_re, v_ssm_lambda_im, v_ssm_log_step, v_ssm_b_re, v_ssm_b_im, v_ssm_c_re, v_ssm_c_im, v_ssm_d, v_w_glu, v_b_glu, v_attn_out_g, v_ssm_out_g, v_w_out, v_norm_mlp_g, v_w_up, v_w_down):
    args = (meta_tokens, norm_mix_g, w_in, q_norm_g, k_norm_g, attn_sinks, ssm_lambda_re, ssm_lambda_im, ssm_log_step, ssm_b_re, ssm_b_im, ssm_c_re, ssm_c_im, ssm_d, w_glu, b_glu, attn_out_g, ssm_out_g, w_out, norm_mlp_g, w_up, w_down)
    ms = (m_meta_tokens, m_norm_mix_g, m_w_in, m_q_norm_g, m_k_norm_g, m_attn_sinks, m_ssm_lambda_re, m_ssm_lambda_im, m_ssm_log_step, m_ssm_b_re, m_ssm_b_im, m_ssm_c_re, m_ssm_c_im, m_ssm_d, m_w_glu, m_b_glu, m_attn_out_g, m_ssm_out_g, m_w_out, m_norm_mlp_g, m_w_up, m_w_down)
    vs = (v_meta_tokens, v_norm_mix_g, v_w_in, v_q_norm_g, v_k_norm_g, v_attn_sinks, v_ssm_lambda_re, v_ssm_lambda_im, v_ssm_log_step, v_ssm_b_re, v_ssm_b_im, v_ssm_c_re, v_ssm_c_im, v_ssm_d, v_w_glu, v_b_glu, v_attn_out_g, v_ssm_out_g, v_w_out, v_norm_mlp_g, v_w_up, v_w_down)
    W = dict(zip(WEIGHTS, args))
    M = dict(zip(WEIGHTS, ms))
    V = dict(zip(WEIGHTS, vs))
    depth = norm_mix_g.shape[0]
    seq, D = x.shape[1], x.shape[2]
    attn_w = D // 2
    kv_w = attn_w // KV_GROUP
    dims = dict(attn_w=attn_w, kv_w=kv_w, u_off=(attn_w + 2 * kv_w) // LANES)
    small_names = [n for n in WEIGHTS if n not in BIG and n != 'meta_tokens']
    chip = 2 * lax.axis_index("x") + lax.axis_index("y")

    gathers, started = [], jnp.zeros((), F32)
    for l in range(depth):
        placed = [_ag_place("ag_place_" + n, W[n], l, COL_SHARDED[n], BF16) for n in BIG]
        groups = [_ag_copies(a, COL_SHARDED[n]) for a, n in enumerate(BIG)]
        if l == 0:
            placed = [_ag_place("ag_place_meta", meta_tokens[None], 0, True, F32)] + placed
            groups = [_ag_copies(0, True)] + [_ag_copies(a + 1, COL_SHARDED[n]) for a, n in enumerate(BIG)]
        sems, bufs, token = _split_start("ag_start_%d" % l, placed, groups)
        gathers.append(dict(zip((['meta_tokens'] if l == 0 else []) + BIG, zip(sems, bufs))))
        started = started + token[0, 0]

    def gathered(l, n, after):
        sems, buf = gathers[l][n]
        return _split_wait("ag_wait_%d_%s" % (l, n), [buf], sems, after, _ag_copies(0, n == 'meta_tokens' or COL_SHARDED[n]))[0]

    h_res = jnp.concatenate([jnp.zeros((PAD, D), F32), gathered(0, 'meta_tokens', started.reshape(1, 1)), x[0]], axis=0)
    s5_flat3, s5_layers = _ssm_prepare(W)
    layer_p = []
    for l in range(depth):
        p = {n: W[n][l][None, :] for n in ('norm_mix_g', 'q_norm_g', 'k_norm_g', 'attn_sinks', 'ssm_d', 'b_glu', 'attn_out_g',
                                             'ssm_out_g', 'norm_mlp_g')}
        p['s5'] = s5_layers[l]
        layer_p.append(p)
    saved = []
    for l in range(depth):
        h_res, sv = _layer_fwd(h_res, layer_p[l], functools.partial(gathered, l), dims)
        saved.append(sv)
    loss_local, dx, dx16 = _loss(h_res, loss_target[0])
    loss = lax.psum(loss_local, ("x", "y", "c"))

    small_grads = [None] * depth
    shard_grads = {}
    pending = []

    def reduce_grads(l, names, grads):
        st = _rs_start("%d_%s" % (l, names[0]), list(grads), [COL_SHARDED[n] for n in names])
        st.update(layer=l, names=names, fresh=True)
        pending.append(st)
        return st['token'][0, 0]

    def tick(after):
        token = jnp.zeros((), F32)
        for st in list(pending):
            if st['fresh']:
                st['fresh'] = False
                continue
            _rs_advance(st, after)
            if 'result' in st:
                pending.remove(st)
                for a, n in enumerate(st['names']):
                    shard_grads[(st['layer'], n)] = (st['result'][0][a], st['result'][1][a])
            else:
                token = token + st['token'][0, 0]
        return token

    token = jnp.zeros((), F32)
    for l in reversed(range(depth)):
        dx, dx16, gs = _layer_bwd(dx, dx16, saved[l], layer_p[l], dims, functools.partial(reduce_grads, l), tick, token)
        saved[l] = None
        small_grads[l] = gs
        token = tick(dx)
    grad_x = dx[BLOCK:].reshape(x.shape)

    g_small = _ssm_param_grads(W, s5_flat3, [small_grads[l]['s5_raw'] for l in range(depth)])
    for n in small_names:
        if n not in g_small:
            g_small[n] = jnp.stack([small_grads[l][n].reshape(W[n].shape[1:]) for l in range(depth)])
    g_shapes = [(N_META, D)] + [W[n].shape for n in small_names]
    rows = _pack_rows(g_shapes, 8 * 2 * N_CHIPS)
    packed = _pack([dx[PAD:BLOCK]] + [g_small[n] for n in small_names], rows)
    tick(packed)
    red, = _reduce_scatter("small", [packed], [False], F32)
    tick(red)
    red_full, = _all_gather("ag_small", [red], [False])
    while pending:
        tick(red_full)
    g_list = _unpack(red_full, g_shapes)
    g_meta = lax.dynamic_slice_in_dim(g_list[0], chip * meta_tokens.shape[1], meta_tokens.shape[1], axis=1)
    G = dict(zip(small_names, g_list[1:]))
    G['meta_tokens'] = g_meta

    out = {}
    for n in BIG:
        out[n] = _adamw_big("adamw_" + n, [shard_grads[(l, n)][0] for l in range(depth)],
                            [shard_grads[(l, n)][1] for l in range(depth)], W[n], M[n], V[n])
    for n in ['meta_tokens'] + small_names:
        rows2d = lambda a: a.reshape(-1, a.shape[-1])
        upd = _ew("adamw_" + n, _adam_math, [rows2d(W[n]), rows2d(G[n]), rows2d(M[n]), rows2d(V[n])], [F32] * 3)
        out[n] = (G[n], *[u.reshape(W[n].shape) for u in upd])
    return (loss, grad_x, *[out[n][0] for n in WEIGHTS], *[out[n][1] for n in WEIGHTS],
            *[out[n][2] for n in WEIGHTS], *[out[n][3] for n in WEIGHTS])
```

```python
import functools
import math

import numpy as np
import jax
import jax.numpy as jnp
from jax import lax
from jax.experimental import pallas as pl
from jax.experimental.pallas import tpu as pltpu

F32 = jnp.float32
BF16 = jnp.bfloat16
MESH = pl.DeviceIdType.MESH

N_META = 16
HEAD_DIM = 64
KV_GROUP = 4
GROUP_CH = 16
STATE = 64
BLOCK = 128
PAD = BLOCK - N_META
NORM_EPS = 1e-6
NEG_INF = -1e30
LANES = 128
V7X_VMEM_LIMIT_BYTES = 56 * 1024 * 1024
MM_VMEM_BUDGET_BYTES = 44 * 1024 * 1024

ADAM_LR, ADAM_B1, ADAM_B2, ADAM_EPS, ADAM_WD, ADAM_STEP = 0.001, 0.9, 0.999, 1e-08, 0.01, 10

WEIGHTS = ['meta_tokens', 'norm_mix_g', 'w_in', 'q_norm_g', 'k_norm_g', 'attn_sinks', 'ssm_lambda_re',
           'ssm_lambda_im', 'ssm_log_step', 'ssm_b_re', 'ssm_b_im', 'ssm_c_re', 'ssm_c_im', 'ssm_d', 'w_glu',
           'b_glu', 'attn_out_g', 'ssm_out_g', 'w_out', 'norm_mlp_g', 'w_up', 'w_down']
BIG = ['w_in', 'w_glu', 'w_out', 'w_up', 'w_down']
COL_SHARDED = {'w_in': True, 'w_glu': False, 'w_out': False, 'w_up': True, 'w_down': False}
N_CHIPS = 4


def _pick(n, cands):
    for c in cands:
        if c <= n and n % c == 0:
            return c
    return n


def _params(sem):
    return pltpu.CompilerParams(dimension_semantics=sem, vmem_limit_bytes=V7X_VMEM_LIMIT_BYTES)


def _pcall(body, **kw):
    return pl.pallas_call(body, **kw)


def _sds(shape, dtype):
    return jax.ShapeDtypeStruct(shape, dtype)


_DIMS = {'nn': ((1,), (0,)), 'nt': ((1,), (1,)), 'tn': ((0,), (0,))}


def _mm(name, a, b, mode, out_dtypes, extras=(), epi=None):
    if mode == 'nn':
        (M, K), (_, N) = a.shape, b.shape
    elif mode == 'nt':
        (M, K), (N, _) = a.shape, b.shape
    else:
        (K, M), (_, N) = a.shape, b.shape
    tile_bytes = 4 * len([k for k, _ in extras if k == 'tile']) + sum(jnp.dtype(d).itemsize for d in out_dtypes)

    def fits(tm, tn, tk):
        need = 2 * tm * tk * a.dtype.itemsize + 2 * tk * tn * b.dtype.itemsize + 4 * tm * tn + 2 * tm * tn * tile_bytes
        return need <= MM_VMEM_BUDGET_BYTES

    if mode == 'tn':
        tm, tk_cands = _pick(M, (1024, 512, 256, 128)), (1408, 704, 384, 128)
    else:
        tm, tk_cands = _pick(M, (1408, 704, 384, 128)), (2048, 1024, 512, 256, 128)
    tk_cands = [t for t in tk_cands if t <= K and K % t == 0] or [K]
    tn_cands = [t for t in (2048, 1280, 1024, 640, 512, 256, 128) if t <= N and N % t == 0] or [N]
    if mode != 'tn' and tk_cands[0] == K and a.dtype == BF16:
        tk_cands = tk_cands[:1]
    tn, tk = next(((tn_, tk_) for tn_ in tn_cands for tk_ in tk_cands if fits(tm, tn_, tk_)), (tn_cands[-1], tk_cands[-1]))
    nk = K // tk
    a_spec = pl.BlockSpec((tk, tm), lambda i, j, k: (k, i)) if mode == 'tn' else pl.BlockSpec((tm, tk), lambda i, j, k: (i, k))
    b_spec = pl.BlockSpec((tn, tk), lambda i, j, k: (j, k)) if mode == 'nt' else pl.BlockSpec((tk, tn), lambda i, j, k: (k, j))
    ex_specs = [pl.BlockSpec((tm, tn), lambda i, j, k: (i, j)) if kind == 'tile' else pl.BlockSpec((1, tn), lambda i, j, k: (0, j))
                for kind, _ in extras]
    ne, no = len(extras), len(out_dtypes)
    dims = (_DIMS[mode], ((), ()))

    def body(a_ref, b_ref, *rest):
        ex, outs, acc = rest[:ne], rest[ne:ne + no], rest[ne + no]
        k = pl.program_id(2)

        @pl.when(k == 0)
        def _():
            acc[...] = jnp.zeros_like(acc)

        acc[...] += lax.dot_general(a_ref[...].astype(BF16), b_ref[...].astype(BF16), dims, preferred_element_type=F32)

        @pl.when(k == nk - 1)
        def _():
            r = acc[...]
            res = epi(r, *[e[...] for e in ex]) if epi is not None else (r,)
            for o, v in zip(outs, res):
                o[...] = v.astype(o.dtype)

    outs = _pcall(
        body, name=name, grid=(M // tm, N // tn, nk),
        in_specs=[a_spec, b_spec] + ex_specs,
        out_specs=[pl.BlockSpec((tm, tn), lambda i, j, k: (i, j)) for _ in out_dtypes],
        out_shape=[_sds((M, N), d) for d in out_dtypes],
        scratch_shapes=[pltpu.VMEM((tm, tn), F32)],
        compiler_params=_params(("parallel", "parallel", "arbitrary")),
    )(a, b, *[e for _, e in extras])
    return outs


def _ew(name, fn, ins, out_dtypes):
    R, C = ins[0].shape
    tr = _pick(R, tuple(t for t in (1024, 512, 256, 128, 64, 32, 16, 8) if t * C <= 512 * 1024) or (8,))
    n_in = len(ins)

    def body(*refs):
        res = fn(*[r[...] for r in refs[:n_in]])
        for o, v in zip(refs[n_in:], res):
            o[...] = v.astype(o.dtype)

    spec = pl.BlockSpec((tr, C), lambda i: (i, 0))
    return _pcall(body, name=name, grid=(R // tr,), in_specs=[spec] * n_in, out_specs=[spec] * len(out_dtypes),
                  out_shape=[_sds((R, C), d) for d in out_dtypes], compiler_params=_params(("parallel",)))(*ins)


def _adam_math(w, g, m, v):
    m = ADAM_B1 * m + (1.0 - ADAM_B1) * g
    v = ADAM_B2 * v + (1.0 - ADAM_B2) * (g * g)
    m_hat = m / (1.0 - ADAM_B1 ** ADAM_STEP)
    v_hat = v / (1.0 - ADAM_B2 ** ADAM_STEP)
    delta = -ADAM_LR * (m_hat / (jnp.sqrt(v_hat) + ADAM_EPS) + ADAM_WD * w)
    return delta, m, v


def _rms_fwd(name, xs, gs, out_dtype):
    L = xs[0].shape[0]
    ws = [x.shape[1] for x in xs]
    n = len(xs)
    tr = _pick(L, (384, 256, 128))

    def body(*refs):
        o = refs[2 * n]
        off = 0
        for i in range(n):
            x = refs[i][...]
            r = lax.rsqrt(jnp.mean(x * x, axis=-1, keepdims=True) + NORM_EPS)
            o[:, off:off + ws[i]] = ((x * r) * refs[n + i][...]).astype(o.dtype)
            off += ws[i]

    return _pcall(body, name=name, grid=(L // tr,),
                  in_specs=[pl.BlockSpec((tr, w), lambda i: (i, 0)) for w in ws] + [pl.BlockSpec((1, w), lambda i: (0, 0)) for w in ws],
                  out_specs=pl.BlockSpec((tr, sum(ws)), lambda i: (i, 0)), out_shape=_sds((L, sum(ws)), out_dtype),
                  compiler_params=_params(("parallel",)))(*xs, *gs)


def _rms_bwd(name, xs, gs, dy, resid=None):
    L = xs[0].shape[0]
    ws = [x.shape[1] for x in xs]
    n = len(xs)
    tr = _pick(L, (384, 256, 128))
    has_res = resid is not None

    def body(*refs):
        x_refs, g_refs, dy_ref = refs[:n], refs[n:2 * n], refs[2 * n]
        p = 2 * n + 1
        res_ref = refs[p] if has_res else None
        p += 1 if has_res else 0
        dx_refs, dg_refs = refs[p:p + n], refs[p + n:p + 2 * n]
        dx16_ref = refs[p + 2 * n] if has_res else None
        first = pl.program_id(0) == 0
        off = 0
        for i in range(n):
            x = x_refs[i][...]
            d = dy_ref[:, off:off + ws[i]]
            r = lax.rsqrt(jnp.mean(x * x, axis=-1, keepdims=True) + NORM_EPS)
            xh = x * r
            dg = jnp.sum(d * xh, axis=0, keepdims=True)

            @pl.when(first)
            def _(i=i):
                dg_refs[i][...] = jnp.zeros_like(dg_refs[i])

            dg_refs[i][...] += dg
            dyg = d * g_refs[i][...]
            dx = r * (dyg - xh * jnp.mean(dyg * xh, axis=-1, keepdims=True))
            if has_res:
                dx = dx + res_ref[...]
                dx16_ref[...] = dx.astype(BF16)
            dx_refs[i][...] = dx
            off += ws[i]

    in_specs = ([pl.BlockSpec((tr, w), lambda i: (i, 0)) for w in ws] + [pl.BlockSpec((1, w), lambda i: (0, 0)) for w in ws]
                + [pl.BlockSpec((tr, sum(ws)), lambda i: (i, 0))])
    ins = list(xs) + list(gs) + [dy]
    if has_res:
        in_specs.append(pl.BlockSpec((tr, ws[0]), lambda i: (i, 0)))
        ins.append(resid)
    out_specs = [pl.BlockSpec((tr, w), lambda i: (i, 0)) for w in ws] + [pl.BlockSpec((1, w), lambda i: (0, 0)) for w in ws]
    out_shape = [_sds((L, w), F32) for w in ws] + [_sds((1, w), F32) for w in ws]
    if has_res:
        out_specs.append(pl.BlockSpec((tr, ws[0]), lambda i: (i, 0)))
        out_shape.append(_sds((L, ws[0]), BF16))
    outs = _pcall(body, name=name, grid=(L // tr,), in_specs=in_specs, out_specs=out_specs, out_shape=out_shape,
                  compiler_params=_params(("arbitrary",)))(*ins)
    return (outs[:n], outs[n:2 * n], outs[2 * n]) if has_res else (outs[:n], outs[n:])


def _loss(xl, target):
    Lp, D = xl.shape

    def body(x_ref, t_ref, dy_ref, dy16_ref, loss_ref):
        n = pl.program_id(0)

        @pl.when(n == 0)
        def _():
            loss_ref[...] = jnp.zeros_like(loss_ref)
            dy_ref[...] = jnp.zeros_like(dy_ref)
            dy16_ref[...] = jnp.zeros_like(dy16_ref)

        @pl.when(n > 0)
        def _():
            err = x_ref[...] - t_ref[...]
            dy = err * (1.0 / D)
            dy_ref[...] = dy
            dy16_ref[...] = dy.astype(BF16)
            loss_ref[...] += jnp.sum(err * err) * (0.5 / D)

    blk = pl.BlockSpec((BLOCK, D), lambda n: (n, 0))
    dy, dy16, loss = _pcall(body, name="loss_head", grid=(Lp // BLOCK,),
                            in_specs=[blk, pl.BlockSpec((BLOCK, D), lambda n: (jnp.maximum(n - 1, 0), 0))],
                            out_specs=[blk, blk, pl.BlockSpec((8, LANES), lambda n: (0, 0))],
                            out_shape=[_sds((Lp, D), F32), _sds((Lp, D), BF16), _sds((8, LANES), F32)],
                            compiler_params=_params(("arbitrary",)))(xl, target)
    return loss[0, 0], dy, dy16


GROUP_ROWS = KV_GROUP * BLOCK


def _attn_mask_dist(n):
    r = lax.broadcasted_iota(jnp.int32, (GROUP_ROWS, 3 * BLOCK), 0)
    i = r & (BLOCK - 1)
    j = lax.broadcasted_iota(jnp.int32, (GROUP_ROWS, 3 * BLOCK), 1)
    in_band = j < 2 * BLOCK
    band = in_band & (j > i) & (j <= i + BLOCK) & (j >= 2 * BLOCK - BLOCK * n)
    jm = j - 2 * BLOCK
    meta = (~in_band) & (jm >= PAD) & (jm <= BLOCK * n + i)
    dist = jnp.where(in_band, BLOCK + i - j, BLOCK * n + i - jm).astype(F32)
    return band | meta, dist


def _head_norm(x, g):
    r = lax.rsqrt(jnp.mean(x * x, axis=-1, keepdims=True) + NORM_EPS)
    return (x * r) * g, r


def _attn_specs(attn_w, kv_w):
    kb = attn_w // kv_w
    q_spec = pl.BlockSpec((BLOCK, attn_w), lambda n: (n, 0))

    def kv(col):
        return [pl.BlockSpec((BLOCK, kv_w), lambda n: (jnp.maximum(n - 1, 0), col)),
                pl.BlockSpec((BLOCK, kv_w), lambda n: (n, col)),
                pl.BlockSpec((BLOCK, kv_w), lambda n: (0, col))]

    return q_spec, kv(kb), kv(kb + 1)


def _slopes(n_heads):
    return [2.0 ** (-8.0 * (h + 1) / n_heads) for h in range(n_heads)]


def _head_slice(h):
    return slice(h * HEAD_DIM, (h + 1) * HEAD_DIM)


def _stack_heads(ref, kh):
    return jnp.concatenate([ref[:, _head_slice(kh * KV_GROUP + g)] for g in range(KV_GROUP)], axis=0)


def _group_column(vals):
    return jnp.concatenate([jnp.broadcast_to(v, (BLOCK, 1)) for v in vals], axis=0)


def _group_inputs(kh, slopes, q_ref, kp, kc, km, vp, vc, vm, gq_ref, gk_ref, sk_ref):
    cs = _head_slice(kh)
    kn, _ = _head_norm(jnp.concatenate([kp[:, cs], kc[:, cs], km[:, cs]], axis=0), gk_ref[...])
    vcat = jnp.concatenate([vp[:, cs], vc[:, cs], vm[:, cs]], axis=0).astype(BF16)
    q = _stack_heads(q_ref, kh)
    qn, rq = _head_norm(q, gq_ref[...])
    heads = range(kh * KV_GROUP, (kh + 1) * KV_GROUP)
    slope = _group_column([jnp.full((1, 1), slopes[h], F32) for h in heads])
    sink = _group_column([sk_ref[0:1, h:h + 1] for h in heads])
    return q, qn, rq, kn, vcat, slope, sink


def _scores(qn, kn, slope, sink, mask, dist):
    s = lax.dot_general(qn.astype(BF16), kn.astype(BF16), (((1,), (1,)), ((), ())), preferred_element_type=F32)
    s = s * (1.0 / math.sqrt(HEAD_DIM)) - slope * dist
    s = jnp.where(mask, s, NEG_INF)
    m = jnp.maximum(jnp.max(s, axis=-1, keepdims=True), sink)
    p = jnp.exp(s - m)
    ps = jnp.exp(sink - m)
    inv = 1.0 / (jnp.sum(p, axis=-1, keepdims=True) + ps)
    return p * inv, ps * inv


def _attn_fwd(proj, gq, gk, sinks, attn_w, kv_w):
    Lp = proj.shape[0]
    n_heads, n_kv = attn_w // HEAD_DIM, kv_w // HEAD_DIM
    slopes = _slopes(n_heads)
    q_spec, k_specs, v_specs = _attn_specs(attn_w, kv_w)

    def body(q_ref, kp, kc, km, vp, vc, vm, gq_ref, gk_ref, sk_ref, o_ref):
        mask, dist = _attn_mask_dist(pl.program_id(0))
        for kh in range(n_kv):
            _, qn, _, kn, vcat, slope, sink = _group_inputs(kh, slopes, q_ref, kp, kc, km, vp, vc, vm, gq_ref, gk_ref, sk_ref)
            p, _ = _scores(qn, kn, slope, sink, mask, dist)
            o = jnp.dot(p.astype(BF16), vcat, preferred_element_type=F32)
            for g in range(KV_GROUP):
                o_ref[:, _head_slice(kh * KV_GROUP + g)] = o[g * BLOCK:(g + 1) * BLOCK]

    small = lambda w: pl.BlockSpec((1, w), lambda n: (0, 0))
    return _pcall(body, name="attn_fwd", grid=(Lp // BLOCK,),
                  in_specs=[q_spec] + k_specs + v_specs + [small(HEAD_DIM), small(HEAD_DIM), small(n_heads)],
                  out_specs=pl.BlockSpec((BLOCK, attn_w), lambda n: (n, 0)), out_shape=_sds((Lp, attn_w), F32),
                  compiler_params=_params(("parallel",)))(proj, proj, proj, proj, proj, proj, proj, gq, gk, sinks)


def _attn_bwd(proj, attn, dattn, gq, gk, sinks, attn_w, kv_w):
    Lp = proj.shape[0]
    n_heads, n_kv = attn_w // HEAD_DIM, kv_w // HEAD_DIM
    slopes = _slopes(n_heads)
    q_spec, k_specs, v_specs = _attn_specs(attn_w, kv_w)
    scale = 1.0 / math.sqrt(HEAD_DIM)
    tn_dims = (((0,), (0,)), ((), ()))

    def body(q_ref, kp, kc, km, vp, vc, vm, o_ref, do_ref, gq_ref, gk_ref, sk_ref, dq_ref, dk_ref, dv_ref, dgq_ref, dsk_ref):
        n = pl.program_id(0)

        @pl.when(n == 0)
        def _():
            dk_ref[...] = jnp.zeros_like(dk_ref)
            dv_ref[...] = jnp.zeros_like(dv_ref)
            dgq_ref[...] = jnp.zeros_like(dgq_ref)
            dsk_ref[...] = jnp.zeros_like(dsk_ref)

        mask, dist = _attn_mask_dist(n)
        lane = lax.broadcasted_iota(jnp.int32, (1, n_heads), 1)
        rows_prev = pl.ds(pl.multiple_of(jnp.maximum(n - 1, 0) * BLOCK, BLOCK), BLOCK)
        rows_cur = pl.ds(pl.multiple_of(n * BLOCK, BLOCK), BLOCK)
        rows_meta = pl.ds(0, BLOCK)
        dgq = jnp.zeros((1, HEAD_DIM), F32)
        dsk = jnp.zeros((1, n_heads), F32)
        for kh in range(n_kv):
            cs = _head_slice(kh)
            q, qn, rq, kn, vcat, slope, sink = _group_inputs(kh, slopes, q_ref, kp, kc, km, vp, vc, vm, gq_ref, gk_ref, sk_ref)
            p, ps = _scores(qn, kn, slope, sink, mask, dist)
            do = _stack_heads(do_ref, kh)
            dd = jnp.sum(do * _stack_heads(o_ref, kh), axis=-1, keepdims=True)
            do16 = do.astype(BF16)
            dp = lax.dot_general(do16, vcat, (((1,), (1,)), ((), ())), preferred_element_type=F32)
            ds16 = (p * (dp - dd)).astype(BF16)
            dsink = -ps * dd
            for g in range(KV_GROUP):
                dsk = dsk + jnp.where(lane == kh * KV_GROUP + g, jnp.sum(dsink[g * BLOCK:(g + 1) * BLOCK]), 0.0)
            dqn = jnp.dot(ds16, kn.astype(BF16), preferred_element_type=F32) * scale
            dkn = lax.dot_general(ds16, qn.astype(BF16), tn_dims, preferred_element_type=F32) * scale
            dvc = lax.dot_general(p.astype(BF16), do16, tn_dims, preferred_element_type=F32)
            xh = q * rq
            dgq = dgq + jnp.sum(dqn * xh, axis=0, keepdims=True)
            dyg = dqn * gq_ref[...]
            dq = rq * (dyg - xh * jnp.mean(dyg * xh, axis=-1, keepdims=True))
            for g in range(KV_GROUP):
                dq_ref[:, _head_slice(kh * KV_GROUP + g)] = dq[g * BLOCK:(g + 1) * BLOCK].astype(dq_ref.dtype)
            for part, rows in enumerate((rows_prev, rows_cur, rows_meta)):
                ps_ = slice(part * BLOCK, (part + 1) * BLOCK)
                dk_ref[rows, cs] += dkn[ps_]
                dv_ref[rows, cs] += dvc[ps_]
        dgq_ref[...] += dgq
        dsk_ref[...] += dsk

    small = lambda w: pl.BlockSpec((1, w), lambda n: (0, 0))
    blk = pl.BlockSpec((BLOCK, attn_w), lambda n: (n, 0))
    whole = pl.BlockSpec((Lp, kv_w), lambda n: (0, 0))
    return _pcall(body, name="attn_bwd", grid=(Lp // BLOCK,),
                  in_specs=[q_spec] + k_specs + v_specs + [blk, blk, small(HEAD_DIM), small(HEAD_DIM), small(n_heads)],
                  out_specs=[blk, whole, whole, small(HEAD_DIM), small(n_heads)],
                  out_shape=[_sds(proj.shape, BF16), _sds((Lp, kv_w), F32), _sds((Lp, kv_w), F32),
                             _sds((1, HEAD_DIM), F32), _sds((1, n_heads), F32)],
                  compiler_params=_params(("arbitrary",)))(proj, proj, proj, proj, proj, proj, proj, attn, dattn, gq, gk, sinks)


def _knorm_bwd(proj, dkn, dv, gk, dproj, attn_w, kv_w):
    Lp = proj.shape[0]
    n_kv = kv_w // HEAD_DIM
    tr = _pick(Lp, (384, 256, 128))

    def body(k_ref, d_ref, dv_ref, g_ref, buf_ref, out_ref, dg_ref):
        @pl.when(pl.program_id(0) == 0)
        def _():
            dg_ref[...] = jnp.zeros_like(dg_ref)

        dg = jnp.zeros((1, HEAD_DIM), F32)
        for kh in range(n_kv):
            cs = slice(kh * HEAD_DIM, (kh + 1) * HEAD_DIM)
            x = k_ref[:, cs]
            d = d_ref[:, cs]
            r = lax.rsqrt(jnp.mean(x * x, axis=-1, keepdims=True) + NORM_EPS)
            xh = x * r
            dg = dg + jnp.sum(d * xh, axis=0, keepdims=True)
            dyg = d * g_ref[...]
            out_ref[:, cs] = (r * (dyg - xh * jnp.mean(dyg * xh, axis=-1, keepdims=True))).astype(out_ref.dtype)
        out_ref[:, kv_w:] = dv_ref[...].astype(out_ref.dtype)
        dg_ref[...] += dg

    kv_blk = pl.BlockSpec((tr, kv_w), lambda i: (i, 0))
    return _pcall(body, name="knorm_bwd", grid=(Lp // tr,),
                  in_specs=[pl.BlockSpec((tr, kv_w), lambda i: (i, attn_w // kv_w)), kv_blk, kv_blk,
                            pl.BlockSpec((1, HEAD_DIM), lambda i: (0, 0)), pl.BlockSpec(memory_space=pl.ANY)],
                  out_specs=[pl.BlockSpec((tr, 2 * kv_w), lambda i: (i, attn_w // (2 * kv_w))), pl.BlockSpec((1, HEAD_DIM), lambda i: (0, 0))],
                  out_shape=[_sds(dproj.shape, dproj.dtype), _sds((1, HEAD_DIM), F32)],
                  input_output_aliases={4: 0},
                  compiler_params=_params(("arbitrary",)))(proj, dkn, dv, gk, dproj)


def _ssm_bbar(lr, li, ls, br, bi):
    def fn(lr, li, ls, br, bi):
        fr, fi = _zoh_factor(lr, li, ls)
        return fr * br - fi * bi, fr * bi + fi * br

    return _ew("ssm_bbar", fn, [lr, li, ls, br, bi], [F32, F32])


def _lam_bar(lr, li, ls):
    dl = jnp.exp(ls)
    e = jnp.exp(lr * dl)
    return e * jnp.cos(li * dl), e * jnp.sin(li * dl), dl


def _zoh_factor(lr, li, ls):
    ar, ai, _ = _lam_bar(lr, li, ls)
    n2 = lr * lr + li * li
    ivr, ivi = lr / n2, -li / n2
    return (ar - 1.0) * ivr - ai * ivi, (ar - 1.0) * ivi + ai * ivr


SCAN_SHIFTS = (1, 2, 4)


def _ssm_tables(lr, li, ls):
    Wx = lr.shape[1]

    def body(lr_ref, li_ref, ls_ref, tf_ref, tr_ref):
        dl = jnp.exp(ls_ref[...])
        zr, zi = lr_ref[...] * dl, li_ref[...] * dl
        row = lax.broadcasted_iota(jnp.int32, (8, Wx), 0)

        def power(kf):
            e = jnp.exp(kf * zr)
            return e * jnp.cos(kf * zi), e * jnp.sin(kf * zi)

        for ref, rev in ((tf_ref, False), (tr_ref, True)):
            sgn = -1.0 if rev else 1.0
            for k, d in enumerate(SCAN_SHIFTS):
                ar, ai = power(jnp.full((8, Wx), float(d), F32))
                keep = (row < 8 - d) if rev else (row >= d)
                ref[k] = jnp.where(keep, ar, 0.0)
                ref[4 + k] = jnp.where(keep, sgn * ai, 0.0)
            pr, pi = power(((8 - row) if rev else (row + 1)).astype(F32))
            ref[3] = pr
            ref[7] = sgn * pi

    full = pl.BlockSpec((1, Wx), lambda: (0, 0))
    tab = pl.BlockSpec((8, 8, Wx), lambda: (0, 0, 0))
    return _pcall(body, name="ssm_tables", in_specs=[full] * 3, out_specs=[tab, tab],
                  out_shape=[_sds((8, 8, Wx), F32)] * 2,
                  compiler_params=pltpu.CompilerParams(vmem_limit_bytes=V7X_VMEM_LIMIT_BYTES))(lr, li, ls)


def _scan(name, br, bi, tab, layer, reverse, states=None):
    L, Wx = br.shape
    TB = _pick(L, (384, 256, 128))
    CW = _pick(Wx, (1024, 512, 256, 128))
    nT, nG = L // TB, TB // 8

    def body(*refs):
        if reverse:
            br_ref, bi_ref, xr_ref, xi_ref, tab_ref, or_ref, oi_ref, s1_ref, s2_ref, cr_ref, ci_ref = refs
        else:
            br_ref, bi_ref, tab_ref, or_ref, oi_ref, cr_ref, ci_ref = refs

        @pl.when(pl.program_id(1) == 0)
        def _():
            cr_ref[...] = jnp.zeros_like(cr_ref)
            ci_ref[...] = jnp.zeros_like(ci_ref)
            if reverse:
                s1_ref[...] = jnp.zeros_like(s1_ref)
                s2_ref[...] = jnp.zeros_like(s2_ref)

        def step(q, carry):
            cr, ci = carry[0], carry[1]
            g = (nG - 1 - q) if reverse else q
            rows = pl.ds(pl.multiple_of(g * 8, 8), 8)
            b_r, b_i = br_ref[rows, :], bi_ref[rows, :]
            sr, si = b_r, b_i
            for k, d in enumerate(SCAN_SHIFTS):
                mr, mi = tab_ref[k], tab_ref[4 + k]
                sh = (8 - d) if reverse else d
                pr, pi = pltpu.roll(sr, sh, 0), pltpu.roll(si, sh, 0)
                sr, si = sr + mr * pr - mi * pi, si + mr * pi + mi * pr
            pwr, pwi = tab_ref[3], tab_ref[7]
            xr = sr + pwr * cr - pwi * ci
            xi = si + pwr * ci + pwi * cr
            or_ref[rows, :] = xr
            oi_ref[rows, :] = xi
            row = 0 if reverse else 7
            out = (jnp.broadcast_to(xr[row:row + 1, :], xr.shape), jnp.broadcast_to(xi[row:row + 1, :], xi.shape))
            if reverse:
                hr, hi = xr - b_r, xi - b_i
                st_r, st_i = xr_ref[rows, :], xi_ref[rows, :]
                out = out + (carry[2] + hr * st_r + hi * st_i, carry[3] + hi * st_r - hr * st_i)
            return out

        init = (cr_ref[...], ci_ref[...])
        if reverse:
            init = init + (jnp.zeros((8, CW), F32), jnp.zeros((8, CW), F32))
        fin = lax.fori_loop(0, nG, step, init, unroll=2)
        cr_ref[...] = fin[0]
        ci_ref[...] = fin[1]
        if reverse:
            s1_ref[...] += fin[2]
            s2_ref[...] += fin[3]

    tmap = (lambda j, t: (nT - 1 - t, j)) if reverse else (lambda j, t: (t, j))
    blk = pl.BlockSpec((TB, CW), tmap)
    tab_spec = pl.BlockSpec((8, 8, CW), lambda j, t: (0, 0, layer * (Wx // CW) + j))
    sum_spec = pl.BlockSpec((8, CW), lambda j, t: (0, j))
    ins = [br, bi] + (list(states) if reverse else []) + [tab]
    in_specs = [blk, blk] + ([blk, blk] if reverse else []) + [tab_spec]
    out_specs = [blk, blk] + ([sum_spec, sum_spec] if reverse else [])
    out_shape = [_sds((L, Wx), F32)] * 2 + ([_sds((8, Wx), F32)] * 2 if reverse else [])
    return _pcall(body, name=name, grid=(Wx // CW, nT), in_specs=in_specs, out_specs=out_specs, out_shape=out_shape,
                  scratch_shapes=[pltpu.VMEM((8, CW), F32), pltpu.VMEM((8, CW), F32)],
                  compiler_params=_params(("parallel", "arbitrary")))(*ins)


def _row_tile(L):
    return _pick(L, (1408, 704, 384, 128))


TILES_PER_BLOCK = 4


def _blockproj(name, src, off, w_r, w_i, layer, depth):
    L = src.shape[0]
    T = w_r.shape[0] // depth
    tm = _row_tile(L)
    wide = TILES_PER_BLOCK * LANES

    def body(s_ref, wr_ref, wi_ref, or_ref, oi_ref):
        s = s_ref[...].astype(BF16)
        for k in range(TILES_PER_BLOCK):
            cols = slice(k * LANES, (k + 1) * LANES)
            or_ref[:, cols] = jnp.dot(s, wr_ref[k], preferred_element_type=F32)
            oi_ref[:, cols] = jnp.dot(s, wi_ref[k], preferred_element_type=F32)

    w_spec = pl.BlockSpec((TILES_PER_BLOCK, LANES, LANES), lambda i, q: (layer * (T // TILES_PER_BLOCK) + q, 0, 0))
    o_spec = pl.BlockSpec((tm, wide), lambda i, q: (i, q))
    return _pcall(body, name=name, grid=(L // tm, T // TILES_PER_BLOCK),
                  in_specs=[pl.BlockSpec((tm, LANES), lambda i, q: (i, off + q)), w_spec, w_spec],
                  out_specs=[o_spec, o_spec], out_shape=[_sds((L, T * LANES), F32)] * 2,
                  compiler_params=_params(("parallel", "arbitrary")))(src, w_r, w_i)


def _blockproj_grad(name, src, off, gr, gi):
    L = src.shape[0]
    T = gr.shape[1] // LANES
    tm = _row_tile(L)
    wide = TILES_PER_BLOCK * LANES
    tn_dims = (((0,), (0,)), ((), ()))

    def body(s_ref, gr_ref, gi_ref, or_ref, oi_ref):
        @pl.when(pl.program_id(1) == 0)
        def _():
            or_ref[...] = jnp.zeros_like(or_ref)
            oi_ref[...] = jnp.zeros_like(oi_ref)

        s = s_ref[...].astype(BF16)
        for k in range(TILES_PER_BLOCK):
            cols = slice(k * LANES, (k + 1) * LANES)
            or_ref[k] += lax.dot_general(s, gr_ref[:, cols].astype(BF16), tn_dims, preferred_element_type=F32)
            oi_ref[k] += lax.dot_general(s, gi_ref[:, cols].astype(BF16), tn_dims, preferred_element_type=F32)

    g_spec = pl.BlockSpec((tm, wide), lambda q, i: (i, q))
    o_spec = pl.BlockSpec((TILES_PER_BLOCK, LANES, LANES), lambda q, i: (q, 0, 0))
    return _pcall(body, name=name, grid=(T // TILES_PER_BLOCK, L // tm),
                  in_specs=[pl.BlockSpec((tm, LANES), lambda q, i: (i, off + q)), g_spec, g_spec],
                  out_specs=[o_spec, o_spec], out_shape=[_sds((T, LANES, LANES), F32)] * 2,
                  compiler_params=_params(("parallel", "arbitrary")))(src, gr, gi)


def _gelu(y):
    k = math.sqrt(2.0 / math.pi)
    return 0.5 * y * (1.0 + jnp.tanh(k * (y + 0.044715 * (y * y * y))))


def _gelu_grad(y):
    k = math.sqrt(2.0 / math.pi)
    t = jnp.tanh(k * (y + 0.044715 * (y * y * y)))
    return 0.5 * (1.0 + t) + 0.5 * y * (1.0 - t * t) * (k * (1.0 + 3 * 0.044715 * (y * y)))


def _ssm_out(xr, xi, w_r, w_i, proj, u_off, dvec, layer, depth):
    L = xr.shape[0]
    J = w_r.shape[0] // depth
    SW = w_r.shape[1]
    tm = _row_tile(L)

    def body(xr_ref, xi_ref, wr_ref, wi_ref, u_ref, d_ref, y_ref, gl_ref):
        acc = jnp.dot(xr_ref[...].astype(BF16), wr_ref[...], preferred_element_type=F32)
        acc += jnp.dot(xi_ref[...].astype(BF16), wi_ref[...], preferred_element_type=F32)
        y = acc + d_ref[...] * u_ref[...]
        y_ref[...] = y
        gl_ref[...] = _gelu(y)

    x_spec = pl.BlockSpec((tm, SW), lambda j, i: (i, j))
    w_spec = pl.BlockSpec((None, SW, LANES), lambda j, i: (layer * J + j, 0, 0))
    o_spec = pl.BlockSpec((tm, LANES), lambda j, i: (i, j))
    return _pcall(body, name="ssm_out", grid=(J, L // tm),
                  in_specs=[x_spec, x_spec, w_spec, w_spec, pl.BlockSpec((tm, LANES), lambda j, i: (i, u_off + j)),
                            pl.BlockSpec((1, LANES), lambda j, i: (0, j))],
                  out_specs=[o_spec, o_spec], out_shape=[_sds((L, J * LANES), F32)] * 2,
                  compiler_params=_params(("parallel", "parallel")))(xr, xi, w_r, w_i, proj, dvec)


def _ssm_du(gr, gi, w_r, w_i, dy, proj, u_off, dvec, dproj, layer, depth):
    L = gr.shape[0]
    J = w_r.shape[0] // depth
    SW = w_r.shape[1]
    tm = _row_tile(L)

    def body(gr_ref, gi_ref, wr_ref, wi_ref, dy_ref, u_ref, d_ref, buf_ref, du_ref, dd_ref):
        i = pl.program_id(1)

        @pl.when(i == 0)
        def _():
            dd_ref[...] = jnp.zeros_like(dd_ref)

        acc = jnp.dot(gr_ref[...].astype(BF16), wr_ref[...], preferred_element_type=F32)
        acc += jnp.dot(gi_ref[...].astype(BF16), wi_ref[...], preferred_element_type=F32)
        dy = dy_ref[...]
        row = lax.broadcasted_iota(jnp.int32, (tm, LANES), 0) + i * tm
        du_ref[...] = jnp.where(row >= PAD, acc + d_ref[...] * dy, 0.0).astype(du_ref.dtype)
        dd_ref[...] += jnp.sum(dy * u_ref[...], axis=0, keepdims=True)

    x_spec = pl.BlockSpec((tm, SW), lambda j, i: (i, j))
    w_spec = pl.BlockSpec((None, SW, LANES), lambda j, i: (layer * J + j, 0, 0))
    o_spec = pl.BlockSpec((tm, LANES), lambda j, i: (i, j))
    u_spec = pl.BlockSpec((tm, LANES), lambda j, i: (i, u_off + j))
    vec = pl.BlockSpec((1, LANES), lambda j, i: (0, j))
    return _pcall(body, name="ssm_du", grid=(J, L // tm),
                  in_specs=[x_spec, x_spec, w_spec, w_spec, o_spec, u_spec, vec, pl.BlockSpec(memory_space=pl.ANY)],
                  out_specs=[u_spec, vec], out_shape=[_sds(dproj.shape, dproj.dtype), _sds((1, J * LANES), F32)],
                  input_output_aliases={7: 0},
                  compiler_params=_params(("parallel", "arbitrary")))(gr, gi, w_r, w_i, dy, proj, dvec, dproj)


def _ssm_dc(xr, xi, dy, SW):
    L = xr.shape[0]
    J = dy.shape[1] // LANES
    tm = _row_tile(L)
    tn_dims = (((0,), (0,)), ((), ()))

    def body(xr_ref, xi_ref, dy_ref, or_ref, oi_ref):
        @pl.when(pl.program_id(1) == 0)
        def _():
            or_ref[...] = jnp.zeros_like(or_ref)
            oi_ref[...] = jnp.zeros_like(oi_ref)

        d = dy_ref[...].astype(BF16)
        or_ref[...] += lax.dot_general(xr_ref[...].astype(BF16), d, tn_dims, preferred_element_type=F32)
        oi_ref[...] += lax.dot_general(xi_ref[...].astype(BF16), d, tn_dims, preferred_element_type=F32)

    x_spec = pl.BlockSpec((tm, SW), lambda j, i: (i, j))
    o_spec = pl.BlockSpec((None, SW, LANES), lambda j, i: (j, 0, 0))
    return _pcall(body, name="ssm_dc", grid=(J, L // tm),
                  in_specs=[x_spec, x_spec, pl.BlockSpec((tm, LANES), lambda j, i: (i, j))],
                  out_specs=[o_spec, o_spec], out_shape=[_sds((J, SW, LANES), F32)] * 2,
                  compiler_params=_params(("parallel", "arbitrary")))(xr, xi, dy)


SCAN_COLS = 1024
SCAN_TILES = SCAN_COLS // LANES
SCAN_CH = SCAN_COLS // STATE * GROUP_CH
SCAN_SLAB = 8 * STATE


def _scan_rows(b_r, b_i, tab_ref, carry, reverse):
    sr, si = b_r, b_i
    for k, d in enumerate(SCAN_SHIFTS):
        mr, mi = tab_ref[k], tab_ref[4 + k]
        sh = (8 - d) if reverse else d
        pr, pi = pltpu.roll(sr, sh, 0), pltpu.roll(si, sh, 0)
        sr, si = sr + mr * pr - mi * pi, si + mr * pi + mi * pr
    pwr, pwi = tab_ref[3], tab_ref[7]
    xr = sr + pwr * carry[0] - pwi * carry[1]
    xi = si + pwr * carry[1] + pwi * carry[0]
    row = 0 if reverse else 7
    return xr, xi, (jnp.broadcast_to(xr[row:row + 1, :], xr.shape), jnp.broadcast_to(xi[row:row + 1, :], xi.shape))


def _ssm_fwd(proj, u_off, s, dvec):
    L = proj.shape[0]
    layer, depth = s['layer'], s['depth']
    T, J = s['wb'][0].shape[0] // depth, s['wc'][0].shape[0] // depth
    Wx, nC = T * LANES, T * LANES // SCAN_COLS
    TB = _pick(L, (384, 256, 128))
    nT, nG = L // TB, TB // 8

    def body(u_ref, wbr_ref, wbi_ref, wcr_ref, wci_ref, d_ref, tab_ref, xr_ref, xi_ref, y_ref, gl_ref, cr_ref, ci_ref):
        @pl.when(pl.program_id(1) == 0)
        def _():
            cr_ref[...] = jnp.zeros_like(cr_ref)
            ci_ref[...] = jnp.zeros_like(ci_ref)

        u = u_ref[...]
        u16 = u.astype(BF16)
        for k in range(SCAN_TILES):
            blk = u16[:, (k // TILES_PER_BLOCK) * LANES:(k // TILES_PER_BLOCK + 1) * LANES]
            cols = slice(k * LANES, (k + 1) * LANES)
            xr_ref[:, cols] = jnp.dot(blk, wbr_ref[k], preferred_element_type=F32)
            xi_ref[:, cols] = jnp.dot(blk, wbi_ref[k], preferred_element_type=F32)

        def step(q, carry):
            rows = pl.ds(pl.multiple_of(q * 8, 8), 8)
            xr, xi, carry = _scan_rows(xr_ref[rows, :], xi_ref[rows, :], tab_ref, carry, False)
            xr_ref[rows, :] = xr
            xi_ref[rows, :] = xi
            return carry

        cr, ci = lax.fori_loop(0, nG, step, (cr_ref[...], ci_ref[...]), unroll=2)
        cr_ref[...] = cr
        ci_ref[...] = ci
        for c in range(SCAN_CH // LANES):
            slab, ch = slice(c * SCAN_SLAB, (c + 1) * SCAN_SLAB), slice(c * LANES, (c + 1) * LANES)
            acc = jnp.dot(xr_ref[:, slab].astype(BF16), wcr_ref[c], preferred_element_type=F32)
            acc += jnp.dot(xi_ref[:, slab].astype(BF16), wci_ref[c], preferred_element_type=F32)
            y = acc + d_ref[:, ch] * u[:, ch]
            y_ref[:, ch] = y
            gl_ref[:, ch] = _gelu(y)

    x_spec = pl.BlockSpec((TB, SCAN_COLS), lambda j, t: (t, j))
    y_spec = pl.BlockSpec((TB, SCAN_CH), lambda j, t: (t, j))
    tile_w = pl.BlockSpec((SCAN_TILES, LANES, LANES), lambda j, t: (layer * nC + j, 0, 0))
    slab_w = pl.BlockSpec((SCAN_CH // LANES, SCAN_SLAB, LANES), lambda j, t: (layer * nC + j, 0, 0))
    return _pcall(body, name="ssm_fwd", grid=(nC, nT),
                  in_specs=[pl.BlockSpec((TB, SCAN_CH), lambda j, t: (t, u_off * LANES // SCAN_CH + j)), tile_w, tile_w, slab_w, slab_w,
                            pl.BlockSpec((1, SCAN_CH), lambda j, t: (0, j)),
                            pl.BlockSpec((8, 8, SCAN_COLS), lambda j, t: (0, 0, layer * nC + j))],
                  out_specs=[x_spec, x_spec, y_spec, y_spec],
                  out_shape=[_sds((L, Wx), F32)] * 2 + [_sds((L, J * LANES), F32)] * 2,
                  scratch_shapes=[pltpu.VMEM((8, SCAN_COLS), F32)] * 2,
                  compiler_params=_params(("parallel", "arbitrary")))(proj, *s['wb'], *s['wc'], dvec, s['tf'])


def _ssm_bwd(dy, proj, u_off, xr, xi, s, dvec, dproj):
    L = dy.shape[0]
    layer, depth = s['layer'], s['depth']
    T, J = s['wb'][0].shape[0] // depth, s['wc'][0].shape[0] // depth
    Wx, nC = T * LANES, T * LANES // SCAN_COLS
    TB = _pick(L, (384, 256, 128))
    nT, nG = L // TB, TB // 8
    n_ch = SCAN_CH // LANES
    tn_dims = (((0,), (0,)), ((), ()))

    def body(dy_ref, u_ref, xr_ref, xi_ref, wcr_ref, wci_ref, wbr_ref, wbi_ref, d_ref, tab_ref, buf_ref,
             du_ref, dd_ref, dwbr_ref, dwbi_ref, dwcr_ref, dwci_ref, s1_ref, s2_ref, gr_ref, gi_ref, cr_ref, ci_ref):
        t = pl.program_id(1)

        @pl.when(t == 0)
        def _():
            for ref in (cr_ref, ci_ref, dd_ref, dwbr_ref, dwbi_ref, dwcr_ref, dwci_ref, s1_ref, s2_ref):
                ref[...] = jnp.zeros_like(ref)

        dy = dy_ref[...]
        dy16 = dy.astype(BF16)
        u = u_ref[...]
        u16 = u.astype(BF16)
        for k in range(SCAN_TILES):
            blk = dy16[:, (k // TILES_PER_BLOCK) * LANES:(k // TILES_PER_BLOCK + 1) * LANES]
            cols = slice(k * LANES, (k + 1) * LANES)
            gr_ref[:, cols] = jnp.dot(blk, wcr_ref[k], preferred_element_type=F32)
            gi_ref[:, cols] = jnp.dot(blk, wci_ref[k], preferred_element_type=F32)

        def step(q, carry):
            rows = pl.ds(pl.multiple_of((nG - 1 - q) * 8, 8), 8)
            b_r, b_i = gr_ref[rows, :], gi_ref[rows, :]
            g_r, g_i, edge = _scan_rows(b_r, b_i, tab_ref, carry[:2], True)
            gr_ref[rows, :] = g_r
            gi_ref[rows, :] = g_i
            hr, hi = g_r - b_r, g_i - b_i
            st_r, st_i = xr_ref[rows, :], xi_ref[rows, :]
            return edge + (carry[2] + hr * st_r + hi * st_i, carry[3] + hi * st_r - hr * st_i)

        zero = jnp.zeros((8, SCAN_COLS), F32)
        fin = lax.fori_loop(0, nG, step, (cr_ref[...], ci_ref[...], zero, zero), unroll=2)
        cr_ref[...] = fin[0]
        ci_ref[...] = fin[1]
        s1_ref[...] += fin[2]
        s2_ref[...] += fin[3]

        row = lax.broadcasted_iota(jnp.int32, (TB, LANES), 0) + (nT - 1 - t) * TB
        for c in range(n_ch):
            slab, ch = slice(c * SCAN_SLAB, (c + 1) * SCAN_SLAB), slice(c * LANES, (c + 1) * LANES)
            g16r, g16i = gr_ref[:, slab].astype(BF16), gi_ref[:, slab].astype(BF16)
            acc = jnp.dot(g16r, wbr_ref[c], preferred_element_type=F32) + jnp.dot(g16i, wbi_ref[c], preferred_element_type=F32)
            du_ref[:, ch] = jnp.where(row >= PAD, acc + d_ref[:, ch] * dy[:, ch], 0.0).astype(du_ref.dtype)
            dwcr_ref[c] += lax.dot_general(xr_ref[:, slab].astype(BF16), dy16[:, ch], tn_dims, preferred_element_type=F32)
            dwci_ref[c] += lax.dot_general(xi_ref[:, slab].astype(BF16), dy16[:, ch], tn_dims, preferred_element_type=F32)
            for kk in range(TILES_PER_BLOCK):
                k = c * TILES_PER_BLOCK + kk
                cols = slice(kk * LANES, (kk + 1) * LANES)
                dwbr_ref[k] += lax.dot_general(u16[:, ch], g16r[:, cols], tn_dims, preferred_element_type=F32)
                dwbi_ref[k] += lax.dot_general(u16[:, ch], g16i[:, cols], tn_dims, preferred_element_type=F32)
        dd_ref[...] += jnp.sum(dy * u, axis=0, keepdims=True)

    rev = lambda j, t: (nT - 1 - t, j)
    x_spec = pl.BlockSpec((TB, SCAN_COLS), rev)
    u_spec = pl.BlockSpec((TB, SCAN_CH), lambda j, t: (nT - 1 - t, u_off * LANES // SCAN_CH + j))
    tile_w = pl.BlockSpec((SCAN_TILES, LANES, LANES), lambda j, t: (layer * nC + j, 0, 0))
    slab_w = pl.BlockSpec((n_ch, SCAN_SLAB, LANES), lambda j, t: (layer * nC + j, 0, 0))
    tile_g = pl.BlockSpec((SCAN_TILES, LANES, LANES), lambda j, t: (j, 0, 0))
    slab_g = pl.BlockSpec((n_ch, SCAN_SLAB, LANES), lambda j, t: (j, 0, 0))
    vec = pl.BlockSpec((1, SCAN_CH), lambda j, t: (0, j))
    sums = pl.BlockSpec((8, SCAN_COLS), lambda j, t: (0, j))
    return _pcall(body, name="ssm_bwd", grid=(nC, nT),
                  in_specs=[pl.BlockSpec((TB, SCAN_CH), rev), u_spec, x_spec, x_spec, tile_w, tile_w, slab_w, slab_w, vec,
                            pl.BlockSpec((8, 8, SCAN_COLS), lambda j, t: (0, 0, layer * nC + j)), pl.BlockSpec(memory_space=pl.ANY)],
                  out_specs=[u_spec, vec, tile_g, tile_g, slab_g, slab_g, sums, sums],
                  out_shape=[_sds(dproj.shape, dproj.dtype), _sds((1, J * LANES), F32), _sds((T, LANES, LANES), F32),
                             _sds((T, LANES, LANES), F32), _sds((J, SCAN_SLAB, LANES), F32), _sds((J, SCAN_SLAB, LANES), F32),
                             _sds((8, Wx), F32), _sds((8, Wx), F32)],
                  input_output_aliases={10: 0},
                  scratch_shapes=[pltpu.VMEM((TB, SCAN_COLS), F32)] * 2 + [pltpu.VMEM((8, SCAN_COLS), F32)] * 2,
                  compiler_params=_params(("parallel", "arbitrary")))(dy, proj, xr, xi, *s['wcT'], *s['wbT'], dvec, s['tr'], dproj)


def _glu_dz(ds, gl, z):
    L, W = ds.shape
    tr = _pick(L, (384, 256, 128))

    def body(ds_ref, gl_ref, z_ref, dz_ref, db_ref):
        @pl.when(pl.program_id(0) == 0)
        def _():
            db_ref[...] = jnp.zeros_like(db_ref)

        sg = jax.nn.sigmoid(z_ref[...])
        dz = ds_ref[...] * gl_ref[...] * (sg * (1.0 - sg))
        dz_ref[...] = dz.astype(BF16)
        db_ref[...] += jnp.sum(dz, axis=0, keepdims=True)

    spec = pl.BlockSpec((tr, W), lambda i: (i, 0))
    vec = pl.BlockSpec((1, W), lambda i: (0, 0))
    return _pcall(body, name="glu_dz", grid=(L // tr,), in_specs=[spec] * 3, out_specs=[spec, vec],
                  out_shape=[_sds((L, W), BF16), _sds((1, W), F32)], compiler_params=_params(("arbitrary",)))(ds, gl, z)


def _ssm_param_bwd_flat(lr, li, ls, br, bi, dbbr, dbbi):
    def seg_sum(x):
        for s in (8, 4, 2, 1):
            x = x + pltpu.roll(x, LANES - s, 1)
        return x

    def fn(lr, li, ls, br, bi, dbbr, dbbi):
        fr, fi = _zoh_factor(lr, li, ls)
        return (fr * dbbr + fi * dbbi, fr * dbbi - fi * dbbr,
                seg_sum(br * dbbr + bi * dbbi), seg_sum(br * dbbi - bi * dbbr))

    return _ew("ssm_param_bwd_flat", fn, [lr, li, ls, br, bi, dbbr, dbbi], [F32] * 4)


def _ssm_param_bwd(lr, li, ls, dfr, dfi, s1, s2):
    G, P = lr.shape

    def body(lr_ref, li_ref, ls_ref, dfr_ref, dfi_ref, s1_ref, s2_ref, dlr_ref, dli_ref, dls_ref):
        lr, li = lr_ref[...], li_ref[...]
        ar, ai, dl = _lam_bar(lr, li, ls_ref[...])
        sr, si = s1_ref[0], s2_ref[0]
        for k in range(1, 8):
            sr = sr + s1_ref[k]
            si = si + s2_ref[k]
        a2 = ar * ar + ai * ai
        gar, gai = (sr * ar - si * ai) / a2, (sr * ai + si * ar) / a2
        n2 = lr * lr + li * li
        ivr, ivi = lr / n2, -li / n2
        fr = (ar - 1.0) * ivr - ai * ivi
        fi = (ar - 1.0) * ivi + ai * ivr
        dfr, dfi = dfr_ref[...], dfi_ref[...]
        gar = gar + ivr * dfr + ivi * dfi
        gai = gai + ivr * dfi - ivi * dfr
        wr, wi = -(fr * ivr - fi * ivi), -(fr * ivi + fi * ivr)
        glr, gli = wr * dfr + wi * dfi, wr * dfi - wi * dfr
        gzr, gzi = ar * gar + ai * gai, ar * gai - ai * gar
        dlr_ref[...] = glr + dl * gzr
        dli_ref[...] = gli + dl * gzi
        dls_ref[...] = dl * jnp.sum(lr * gzr + li * gzi, axis=-1, keepdims=True)

    m = pl.BlockSpec((G, P), lambda: (0, 0))
    v = pl.BlockSpec((G, 1), lambda: (0, 0))
    s = pl.BlockSpec((8, G, P), lambda: (0, 0, 0))
    return _pcall(body, name="ssm_param_bwd", in_specs=[m, m, v, m, m, s, s], out_specs=[m, m, v],
                  out_shape=[_sds((G, P), F32), _sds((G, P), F32), _sds((G, 1), F32)])(lr, li, ls, dfr, dfi, s1, s2)


def _tile_mask(G):
    T = G // 2
    e = np.zeros((T, 8, 1, 2, 1), np.float32)
    for t in range(T):
        for c in range(2):
            e[t, (2 * t + c) % 8, 0, c, 0] = 1.0
    return e


def _tile_w(arr):
    T = arr.shape[0] // 2
    a = arr.reshape(T, 2, STATE, GROUP_CH).transpose(0, 3, 1, 2).reshape(T, 1, GROUP_CH, LANES).astype(BF16)
    shape = (T, 8, 1, LANES)
    t, e, lane = (lax.broadcasted_iota(jnp.int32, shape, d) for d in (0, 1, 3))
    return jnp.where(e == (2 * t + lane // STATE) % 8, a, 0).reshape(T, LANES, LANES)


def _tile_w_grad(dw):
    G = dw.shape[0] * 2
    d = dw.reshape(G // 2, 8, GROUP_CH, 2, STATE) * _tile_mask(G)
    return d.sum(axis=1).transpose(0, 2, 3, 1).reshape(G, STATE, GROUP_CH)


def _slab_w(arr):
    J = arr.shape[0] // 8
    a = jnp.tile(arr.reshape(J, 8, STATE, GROUP_CH).astype(BF16), (1, 1, 1, 8))
    g, lane = (lax.broadcasted_iota(jnp.int32, (1, 8, 1, LANES), d) for d in (1, 3))
    return jnp.where(g == lane // GROUP_CH, a, 0).reshape(J, 8 * STATE, LANES)


def _slab_w_grad(dw):
    J = dw.shape[0]
    eye = np.eye(8, dtype=np.float32).reshape(1, 8, 1, 8, 1)
    return (dw.reshape(J, 8, STATE, 8, GROUP_CH) * eye).sum(axis=3).reshape(J * 8, STATE, GROUP_CH)


def _exchange(name, ins, out_sds, remote, local, aliases=None):
    n_in, n_out, n_r, n_l = len(ins), len(out_sds), len(remote), len(local)

    def body(*refs):
        in_refs, out_refs = refs[:n_in], refs[n_in:n_in + n_out]
        send_sems, recv_sems, local_sems = refs[n_in + n_out:]
        x, y, c = lax.axis_index("x"), lax.axis_index("y"), lax.axis_index("c")

        def place(px, py, pc):
            return dict(x=px, y=py, c=pc, chip=2 * px + py)

        def flip(mask):
            mx, my, mc = mask
            return ((1 - x) if mx else x, (1 - y) if my else y, (1 - c) if mc else c)

        me = place(x, y, c)
        sends = []
        for k, (ii, src, oi, dst, mask) in enumerate(remote):
            cp = pltpu.make_async_remote_copy(src_ref=src(in_refs[ii], me), dst_ref=dst(out_refs[oi], me),
                                              send_sem=send_sems.at[k], recv_sem=recv_sems.at[k],
                                              device_id=flip(mask), device_id_type=MESH)
            cp.start()
            sends.append(cp)
        locals_ = []
        for k, (ii, src, oi, dst) in enumerate(local):
            cp = pltpu.make_async_copy(src(in_refs[ii], me), dst(out_refs[oi], me), local_sems.at[k])
            cp.start()
            locals_.append(cp)
        for k, (ii, src, oi, dst, mask) in enumerate(remote):
            sends[k].wait_send()
            peer = flip(mask)
            pltpu.make_async_remote_copy(src_ref=src(in_refs[ii], me), dst_ref=dst(out_refs[oi], place(*peer)),
                                         send_sem=send_sems.at[k], recv_sem=recv_sems.at[k],
                                         device_id=peer, device_id_type=MESH).wait_recv()
        for cp in locals_:
            cp.wait()

    any_spec = pl.BlockSpec(memory_space=pl.ANY)
    return _pcall(body, name=name, in_specs=[any_spec] * n_in, out_specs=[any_spec] * n_out, out_shape=list(out_sds),
                  input_output_aliases=aliases or {},
                  scratch_shapes=[pltpu.SemaphoreType.DMA((n_r,)), pltpu.SemaphoreType.DMA((n_r,)),
                                  pltpu.SemaphoreType.DMA((max(n_l, 1),))])(*ins)


def _mesh_place():
    x, y, c = lax.axis_index("x"), lax.axis_index("y"), lax.axis_index("c")

    def place(px, py, pc):
        return dict(x=px, y=py, c=pc, chip=2 * px + py)

    def flip(mask):
        mx, my, mc = mask
        return ((1 - x) if mx else x, (1 - y) if my else y, (1 - c) if mc else c)

    return place(x, y, c), place, flip


_HBM = pl.BlockSpec(memory_space=pltpu.HBM)
_SEM = pl.BlockSpec(memory_space=pltpu.SEMAPHORE)
_EFFECT = pltpu.SideEffectType.DATAFLOW_SIDE_EFFECTING


def _split_start(name, bufs, groups):
    n, ng = len(bufs), len(groups)

    def body(*refs):
        in_refs, sems, token = refs[:n], refs[n:n + 2 * ng], refs[-1]
        me, _, flip = _mesh_place()
        for g, copies in enumerate(groups):
            for k, (si, src, di, dst, mask) in enumerate(copies):
                pltpu.make_async_remote_copy(src_ref=src(in_refs[si], me), dst_ref=dst(in_refs[di], me),
                                             send_sem=sems[2 * g].at[k], recv_sem=sems[2 * g + 1].at[k],
                                             device_id=flip(mask), device_id_type=MESH).start()
        token[...] = jnp.zeros_like(token)

    outs = _pcall(body, name=name,
                  out_shape=(*[pltpu.SemaphoreType.DMA((len(g),)) for g in groups for _ in range(2)],
                             *[pltpu.HBM(b.shape, b.dtype) for b in bufs], _sds((8, LANES), F32)),
                  in_specs=[_HBM] * n, out_specs=(*[_SEM] * (2 * ng), *[_HBM] * n, pl.BlockSpec(memory_space=pltpu.VMEM)),
                  input_output_aliases={i: 2 * ng + i for i in range(n)},
                  compiler_params=pltpu.CompilerParams(has_side_effects=_EFFECT),
                  )(*[pltpu.with_memory_space_constraint(b, pltpu.HBM) for b in bufs])
    return [(outs[2 * g], outs[2 * g + 1]) for g in range(ng)], list(outs[2 * ng:2 * ng + n]), outs[-1]


def _split_wait(name, bufs, sems, after, remote):
    n = len(bufs)
    send_sems, recv_sems = sems

    def body(*refs):
        in_refs, ssem, rsem = refs[:n], refs[n], refs[n + 1]
        me, place, flip = _mesh_place()
        for k, (si, src, di, dst, mask) in enumerate(remote):
            peer = flip(mask)
            cp = pltpu.make_async_remote_copy(src_ref=src(in_refs[si], me), dst_ref=dst(in_refs[di], place(*peer)),
                                              send_sem=ssem.at[k], recv_sem=rsem.at[k], device_id=peer, device_id_type=MESH)
            cp.wait_send()
            cp.wait_recv()

    return list(_pcall(body, name=name, out_shape=tuple(pltpu.HBM(b.shape, b.dtype) for b in bufs),
                       in_specs=[_HBM] * n + [_SEM, _SEM, pl.BlockSpec(memory_space=pl.ANY)], out_specs=tuple([_HBM] * n),
                       input_output_aliases={i: i for i in range(n)},
                       compiler_params=pltpu.CompilerParams(has_side_effects=_EFFECT))(*bufs, send_sems, recv_sems, after))


CHIP_MASKS = ((0, 1, 0), (1, 0, 0), (1, 1, 0))
SIBLING = (0, 0, 1)


def _whole(ref, p):
    return ref


def _all_gather(name, shards, col_sharded):
    def dst_view(col):
        def view(ref, p):
            r, cdim = ref.shape[0] // (1 if col else N_CHIPS), ref.shape[1] // (N_CHIPS if col else 1)
            if col:
                return ref.at[:, pl.ds(pl.multiple_of(p["chip"] * cdim, LANES), cdim)]
            return ref.at[pl.ds(pl.multiple_of(p["chip"] * r, 8), r), :]
        return view

    out_sds = [_sds((s.shape[0], s.shape[1] * N_CHIPS) if col else (s.shape[0] * N_CHIPS, s.shape[1]), s.dtype)
               for s, col in zip(shards, col_sharded)]
    remote = [(a, _whole, a, dst_view(col), m) for a, col in enumerate(col_sharded) for m in CHIP_MASKS]
    local = [(a, _whole, a, dst_view(col)) for a, col in enumerate(col_sharded)]
    return _exchange(name, shards, out_sds, remote, local)


class _Place:
    def __getitem__(self, k):
        return lax.axis_index("c") if k == 0 else 2 * lax.axis_index("x") + lax.axis_index("y")


def _placed_call(body, name, grid, in_specs, out_specs, out_shape, sem, ins):
    def wrap(spec):
        return pl.BlockSpec(spec.block_shape, lambda *idx: spec.index_map(*idx, _Place()))

    outs = [wrap(s) for s in out_specs] if isinstance(out_specs, (list, tuple)) else wrap(out_specs)
    return _pcall(body, name=name, grid=grid, in_specs=[wrap(s) for s in in_specs], out_specs=outs, out_shape=out_shape,
                  compiler_params=_params(sem))(*ins)


def _rows_within(n, width, limit=512 * 1024):
    return _pick(n, tuple(t for t in (1024, 512, 256, 128, 64, 32, 16) if t * width <= limit) or (16,))


def _region_view(col):
    def view(ref, p):
        if col:
            cdim = ref.shape[1] // N_CHIPS
            return ref.at[:, pl.ds(pl.multiple_of(p["chip"] * cdim, LANES), cdim)]
        r = ref.shape[0] // N_CHIPS
        return ref.at[pl.ds(pl.multiple_of(p["chip"] * r, 16), r), :]
    return view


def _ag_place(name, w, layer, col, dtype):
    _, r, cdim = w.shape
    tr = _rows_within(r, cdim)
    nb = r // tr

    def body(w_ref, o_ref):
        o_ref[...] = w_ref[...].astype(dtype)

    if col:
        out_shape, out_spec = (r, N_CHIPS * cdim), pl.BlockSpec((tr, cdim), lambda i, pr: (i, pr[1]))
    else:
        out_shape, out_spec = (N_CHIPS * r, cdim), pl.BlockSpec((tr, cdim), lambda i, pr: (pr[1] * nb + i, 0))
    return _placed_call(body, name, (nb,), [pl.BlockSpec((None, tr, cdim), lambda i, pr: (layer, i, 0))], out_spec,
                        _sds(out_shape, dtype), ("parallel",), [w])


def _ag_copies(a, col):
    return [(a, _region_view(col), a, _region_view(col), m) for m in CHIP_MASKS]


def _rs_add2(name, g4, a4, out_dtype):
    J, _, h, C = g4.shape
    tr = _rows_within(h, C)

    def body(g_ref, a_ref, o_ref):
        o_ref[...] = (g_ref[...].astype(F32) + a_ref[...].astype(F32)).astype(o_ref.dtype)

    return _placed_call(body, name, (J, h // tr),
                        [pl.BlockSpec((None, None, tr, C), lambda j, i, pr: (j, pr[0], i, 0)),
                         pl.BlockSpec((None, None, tr, C), lambda j, i, pr: (j, 0, i, 0))],
                        pl.BlockSpec((None, tr, C), lambda j, i, pr: (j, i, 0)), _sds((J, h, C), out_dtype),
                        ("parallel", "parallel"), [g4, a4])


def _rs_add4(name, p3, landed, col):
    _, h, w = landed.shape
    tr = _rows_within(h, w)

    def body(p_ref, a_ref, b_ref, c_ref, o_ref):
        o_ref[...] = ((p_ref[...].astype(F32) + a_ref[...].astype(F32)) + b_ref[...].astype(F32)) + c_ref[...].astype(F32)

    own = (pl.BlockSpec((None, tr, w), lambda i, pr: (0, i, pr[1])) if col else pl.BlockSpec((None, tr, w), lambda i, pr: (pr[1], i, 0)))
    slot = lambda k: pl.BlockSpec((None, tr, w), lambda i, pr: (k, i, 0))
    return _placed_call(body, name, (h // tr,), [own, slot(0), slot(1), slot(2)], pl.BlockSpec((tr, w), lambda i, pr: (i, 0)),
                        _sds((h, w), F32), ("parallel",), [p3, landed, landed, landed])


def _rs_start(tag, grads, col_sharded):
    n = len(grads)
    g4 = [g.reshape((1, 2, g.shape[0] // 2, g.shape[1]) if col else (N_CHIPS, 2, g.shape[0] // (2 * N_CHIPS), g.shape[1]))
          for g, col in zip(grads, col_sharded)]
    other_half = lambda ref, p: ref.at[:, pl.ds(1 - p["c"], 1)]
    landing = [lax.empty((g.shape[0], 1) + g.shape[2:], g.dtype) for g in g4]
    copies = [(a, other_half, n + a, _whole, SIBLING) for a in range(n)]
    (sems,), bufs, token = _split_start("rs_sibling_start_" + tag, g4 + landing, [copies])
    return dict(tag=tag, stage=0, sems=sems, bufs=bufs, copies=copies, token=token, col_sharded=list(col_sharded))


def _rs_advance(st, after):
    col_sharded, tag = st['col_sharded'], st['tag']
    n = len(col_sharded)
    stage = st['stage']
    st['stage'] = stage + 1
    if stage == 0:
        bufs = _split_wait("rs_sibling_wait_" + tag, st['bufs'], st['sems'], after, st['copies'])
        st.update(_rs_chips_start(tag, [_rs_add2("rs_add2_w", bufs[a], bufs[n + a], BF16) for a in range(n)], col_sharded))
    elif stage == 1:
        bufs = _split_wait("rs_chips_wait_" + tag, st['bufs'], st['sems'], after, st['copies'])
        mine = [_rs_add4("rs_add4_w", bufs[a], bufs[n + a], col_sharded[a]) for a in range(n)]
        copies = [(a, _whole, n + a, _whole, SIBLING) for a in range(n)]
        (sems,), bufs, token = _split_start("rs_halves_start_" + tag, mine + [lax.empty(m.shape, F32) for m in mine], [copies])
        st.update(sems=sems, bufs=bufs, copies=copies, token=token)
    else:
        bufs = _split_wait("rs_halves_wait_" + tag, st['bufs'], st['sems'], after, st['copies'])
        st['result'] = (bufs[:n], bufs[n:])


def _rs_chips_start(tag, chip_sum, col_sharded):
    n = len(chip_sum)

    def send_view(col, mask):
        def view(ref, p):
            t = 2 * ((1 - p["x"]) if mask[0] else p["x"]) + ((1 - p["y"]) if mask[1] else p["y"])
            if col:
                sc = ref.shape[2] // N_CHIPS
                return ref.at[0, :, pl.ds(pl.multiple_of(t * sc, LANES), sc)]
            return ref.at[t]
        return view
    slot = lambda k: (lambda ref, p: ref.at[k])
    piece = [(s.shape[1], s.shape[2] // N_CHIPS if col else s.shape[2]) for s, col in zip(chip_sum, col_sharded)]
    landing = [lax.empty((len(CHIP_MASKS),) + s, BF16) for s in piece]
    copies = [(a, send_view(col_sharded[a], m), n + a, slot(k), m) for a in range(n) for k, m in enumerate(CHIP_MASKS)]
    (sems,), bufs, token = _split_start("rs_chips_start_" + tag, chip_sum + landing, [copies])
    return dict(sems=sems, bufs=bufs, copies=copies, token=token)


def _adamw_big(name, mine, other, w, m, v):
    depth, R, C = w.shape
    h = R // 2
    tr = _pick(h, tuple(t for t in (512, 256, 128, 64, 32, 16, 8) if t * C <= 256 * 1024) or (8,))
    nb = h // tr

    def g_spec(kk, hh):
        def imap(l, s, i, pr):
            before = (l < kk) | ((l == kk) & (s < hh))
            return (jnp.where((l == kk) & (s == hh), i, jnp.where(before, 0, nb - 1)), 0)
        return pl.BlockSpec((tr, C), imap)

    st_spec = pl.BlockSpec((None, tr, C), lambda l, s, i, pr: (l, jnp.where(s == 0, pr[0], 1 - pr[0]) * nb + i, 0))

    def body(*refs):
        g_refs = refs[:2 * depth]
        w_ref, m_ref, v_ref, go_ref, d_ref, mo_ref, vo_ref = refs[2 * depth:]
        l, s = pl.program_id(0), pl.program_id(1)
        for kk in range(depth):
            for hh in range(2):
                @pl.when((l == kk) & (s == hh))
                def _(kk=kk, hh=hh):
                    g = g_refs[2 * kk + hh][...]
                    d, mn, vn = _adam_math(w_ref[...], g, m_ref[...], v_ref[...])
                    go_ref[...] = g
                    d_ref[...] = d
                    mo_ref[...] = mn
                    vo_ref[...] = vn

    gs, g_specs = [], []
    for kk in range(depth):
        gs += [mine[kk], other[kk]]
        g_specs += [g_spec(kk, 0), g_spec(kk, 1)]
    return _placed_call(body, name, (depth, 2, nb), g_specs + [st_spec] * 3, [st_spec] * 4, [_sds(w.shape, F32)] * 4,
                        ("arbitrary", "arbitrary", "arbitrary"), gs + [w, m, v])


def _piece_view(col, j, other):
    def view(ref, p):
        R, C = ref.shape
        cc = (1 - p["c"]) if other else p["c"]
        if col:
            hr, sc = R // 2, C // N_CHIPS
            return ref.at[pl.ds(pl.multiple_of(cc * hr, 16), hr), pl.ds(j * sc, sc)]
        hr = R // (2 * N_CHIPS)
        return ref.at[pl.ds(pl.multiple_of((2 * j + cc) * hr, 8), hr), :]
    return view


def _piece_shape(shape, col):
    R, C = shape
    return (R // 2, C // N_CHIPS) if col else (R // (2 * N_CHIPS), C)


def _reduce_scatter(tag, grads, col_sharded, wire_dtype):
    n = len(grads)
    shapes = [_piece_shape(g.shape, col) for g, col in zip(grads, col_sharded)]

    slot = lambda j: (lambda ref, p: ref.at[j])
    remote = [(a, _piece_view(col_sharded[a], j, True), a, slot(j), SIBLING) for a in range(n) for j in range(N_CHIPS)]
    local = [(a, _piece_view(col_sharded[a], j, False), n + a, slot(j)) for a in range(n) for j in range(N_CHIPS)]
    got = _exchange("rs_sibling_" + tag, grads, [_sds((N_CHIPS,) + s, g.dtype) for s, g in zip(shapes, grads)] * 2, remote, local)
    theirs, mine = got[:n], got[n:]
    chip_sum = [_ew("rs_add2_" + tag, lambda a, b: (a.astype(F32) + b.astype(F32),),
                    [m.reshape(-1, m.shape[-1]), t.reshape(-1, t.shape[-1])], [wire_dtype])[0].reshape(m.shape)
                for m, t in zip(mine, theirs)]

    def send_view(mask):
        return lambda ref, p: ref.at[2 * ((1 - p["x"]) if mask[0] else p["x"]) + ((1 - p["y"]) if mask[1] else p["y"])]
    remote = [(a, send_view(m), a, slot(k), m) for a in range(n) for k, m in enumerate(CHIP_MASKS)]
    local = [(a, lambda ref, p: ref.at[p["chip"]], n + a, _whole) for a in range(n)]
    got = _exchange("rs_chips_" + tag, chip_sum,
                    [_sds((len(CHIP_MASKS),) + s, wire_dtype) for s in shapes] + [_sds(s, wire_dtype) for s in shapes], remote, local)
    landed, own = got[:n], got[n:]
    half = [_ew("rs_add4_" + tag, lambda o, a, b, c: (((o.astype(F32) + a.astype(F32)) + b.astype(F32)) + c.astype(F32),),
                [o, l[0], l[1], l[2]], [F32])[0] for o, l in zip(own, landed)]

    def half_rows(ref, p):
        hr = ref.shape[0] // 2
        return ref.at[pl.ds(pl.multiple_of(p["c"] * hr, 8), hr), :]
    remote = [(a, _whole, a, half_rows, SIBLING) for a in range(n)]
    local = [(a, _whole, a, half_rows) for a in range(n)]
    return _exchange("rs_halves_" + tag, half, [_sds((2 * s[0], s[1]), F32) for s in shapes], remote, local)


def _ssm_prepare(W):
    lr, li, ls = W['ssm_lambda_re'], W['ssm_lambda_im'], W['ssm_log_step']
    depth, G = ls.shape
    GG = depth * G
    flat = lambda a: a.reshape(-1, LANES)
    bc = lambda a: flat(jnp.broadcast_to(a, (depth, G, STATE, GROUP_CH)))
    flat3 = (bc(lr[..., None]), bc(li[..., None]), bc(ls[:, :, None, None]))
    bbr, bbi = _ssm_bbar(*flat3, flat(W['ssm_b_re']), flat(W['ssm_b_im']))
    bbr, bbi = bbr.reshape(GG, STATE, GROUP_CH), bbi.reshape(GG, STATE, GROUP_CH)
    row = lambda a: a.reshape(1, GG * STATE)
    tf, tr = _ssm_tables(row(lr), row(li), row(jnp.broadcast_to(ls[..., None], (depth, G, STATE))))
    cr = W['ssm_c_re'].reshape(GG, GROUP_CH, STATE).transpose(0, 2, 1)
    ci = -W['ssm_c_im'].reshape(GG, GROUP_CH, STATE).transpose(0, 2, 1)
    stacked = dict(wb=(_tile_w(bbr), _tile_w(bbi)), wbT=(_slab_w(bbr), _slab_w(bbi)),
                   wc=(_slab_w(cr), _slab_w(ci)), wcT=(_tile_w(cr), _tile_w(ci)))
    return flat3, [dict(stacked, tf=tf, tr=tr, layer=l, depth=depth) for l in range(depth)]


def _ssm_param_grads(W, flat3, raw):
    depth, G = W['ssm_log_step'].shape
    GG = depth * G
    cat = lambda k: jnp.concatenate([r[k] for r in raw], axis=0)
    flat = lambda a: a.reshape(-1, LANES)
    out = {}
    out['ssm_c_re'] = _slab_w_grad(cat(2)).transpose(0, 2, 1).reshape(W['ssm_c_re'].shape)
    out['ssm_c_im'] = -_slab_w_grad(cat(3)).transpose(0, 2, 1).reshape(W['ssm_c_im'].shape)
    dbr, dbi, qr, qi = _ssm_param_bwd_flat(*flat3, flat(W['ssm_b_re']), flat(W['ssm_b_im']),
                                           flat(_tile_w_grad(cat(0))), flat(_tile_w_grad(cat(1))))
    out['ssm_b_re'], out['ssm_b_im'] = dbr.reshape(W['ssm_b_re'].shape), dbi.reshape(W['ssm_b_im'].shape)
    pick = lambda q: q[:, ::GROUP_CH].reshape(GG, STATE)
    sums = lambda k: jnp.concatenate([r[k].reshape(8, G, STATE) for r in raw], axis=1)
    dlr, dli, dls = _ssm_param_bwd(W['ssm_lambda_re'].reshape(GG, STATE), W['ssm_lambda_im'].reshape(GG, STATE),
                                   W['ssm_log_step'].reshape(GG, 1), pick(qr), pick(qi), sums(4), sums(5))
    out['ssm_lambda_re'], out['ssm_lambda_im'] = dlr.reshape(depth, G, STATE), dli.reshape(depth, G, STATE)
    out['ssm_log_step'] = dls.reshape(depth, G)
    return out


def _layer_fwd(x, p, weight, dims):
    attn_w, kv_w, u_off = dims['attn_w'], dims['kv_w'], dims['u_off']
    s = p['s5']
    h = _rms_fwd("norm_mix", [x], [p['norm_mix_g']], BF16)
    w = {'w_in': weight('w_in', h)}
    proj, = _mm("mm_in", h, w['w_in'], 'nn', [F32])
    attn = _attn_fwd(proj, p['q_norm_g'], p['k_norm_g'], p['attn_sinks'], attn_w, kv_w)
    xr, xi, y, gl = _ssm_fwd(proj, u_off, s, p['ssm_d'])
    w['w_glu'] = weight('w_glu', gl)
    ssm, z = _mm("mm_glu", gl, w['w_glu'], 'nn', [F32, F32], extras=[('row', p['b_glu']), ('tile', gl)],
                 epi=lambda acc, b, g: ((lambda zz: (g * jax.nn.sigmoid(zz), zz))(acc + b)))
    mix = _rms_fwd("norm_heads", [attn, ssm], [p['attn_out_g'], p['ssm_out_g']], BF16)
    w['w_out'] = weight('w_out', mix)
    x_mid, = _mm("mm_out", mix, w['w_out'], 'nn', [F32], extras=[('tile', x)], epi=lambda acc, r: (acc + r,))
    h2 = _rms_fwd("norm_mlp", [x_mid], [p['norm_mlp_g']], BF16)
    w['w_up'] = weight('w_up', h2)
    a, r = _mm("mm_up", h2, w['w_up'], 'nn', [F32, BF16],
               epi=lambda acc: (acc, jnp.square(jnp.maximum(acc, 0.0))))
    w['w_down'] = weight('w_down', r)
    x_out, = _mm("mm_down", r, w['w_down'], 'nn', [F32], extras=[('tile', x_mid)], epi=lambda acc, rr: (acc + rr,))
    saved = dict(x=x, h=h, proj=proj, attn=attn, xr=xr, xi=xi, y=y, gl=gl, z=z, ssm=ssm, mix=mix, x_mid=x_mid, h2=h2, a=a, r=r, w=w)
    return x_out, saved


def _layer_bwd(dx, dx16, sv, p, dims, reduce_grads, tick, token_in):
    attn_w, kv_w, u_off = dims['attn_w'], dims['kv_w'], dims['u_off']
    s, w = p['s5'], sv['w']
    gb, gs = {}, {}
    da, = _mm("mm_down_dx", dx16, w['w_down'], 'nt', [BF16], extras=[('tile', sv['a'])],
              epi=lambda acc, a: (acc * (2.0 * jnp.maximum(a, 0.0)),))
    gb['w_down'], = _mm("mm_down_dw", sv['r'], dx16, 'tn', [BF16])
    dh2, = _mm("mm_up_dx", da, w['w_up'], 'nt', [F32])
    gb['w_up'], = _mm("mm_up_dw", sv['h2'], da, 'tn', [BF16])
    token = reduce_grads(('w_up', 'w_down'), [gb['w_up'], gb['w_down']]) + token_in
    (dx_mid,), (gs['norm_mlp_g'],), dx_mid16 = _rms_bwd("norm_mlp_bwd", [sv['x_mid']], [p['norm_mlp_g'] + token], dh2, resid=dx)
    dmix, = _mm("mm_out_dx", dx_mid16, w['w_out'], 'nt', [F32])
    gb['w_out'], = _mm("mm_out_dw", sv['mix'], dx_mid16, 'tn', [BF16])
    token = tick(dmix)
    (dattn, dssm), (gs['attn_out_g'], gs['ssm_out_g']) = _rms_bwd(
        "norm_heads_bwd", [sv['attn'], sv['ssm']], [p['attn_out_g'] + token, p['ssm_out_g']], dmix)
    dz, gs['b_glu'] = _glu_dz(dssm, sv['gl'], sv['z'])
    dy, = _mm("mm_glu_dx", dz, w['w_glu'], 'nt', [F32], extras=[('tile', dssm), ('tile', sv['z']), ('tile', sv['y'])],
              epi=lambda acc, ds, z, y: ((acc + ds * jax.nn.sigmoid(z)) * _gelu_grad(y),))
    gb['w_glu'], = _mm("mm_glu_dw", sv['gl'], dz, 'tn', [BF16])
    token = tick(dy)
    dproj, dkn, dv, gs['q_norm_g'], gs['attn_sinks'] = _attn_bwd(sv['proj'], sv['attn'], dattn, p['q_norm_g'] + token, p['k_norm_g'],
                                                                  p['attn_sinks'], attn_w, kv_w)
    dproj, gs['k_norm_g'] = _knorm_bwd(sv['proj'], dkn, dv, p['k_norm_g'], dproj, attn_w, kv_w)
    dproj, gs['ssm_d'], *gs['s5_raw'] = _ssm_bwd(dy, sv['proj'], u_off, sv['xr'], sv['xi'], s, p['ssm_d'], dproj)
    dh, = _mm("mm_in_dx", dproj, w['w_in'], 'nt', [F32])
    gb['w_in'], = _mm("mm_in_dw", sv['h'], dproj, 'tn', [BF16])
    token = reduce_grads(('w_in', 'w_glu', 'w_out'), [gb['w_in'], gb['w_glu'], gb['w_out']])
    (dx_in,), (gs['norm_mix_g'],), dx_in16 = _rms_bwd("norm_mix_bwd", [sv['x']], [p['norm_mix_g'] + token], dh, resid=dx_mid)
    return dx_in, dx_in16, gs


PACK_COLS = 1024


def _pack(arrs, rows):
    flat = jnp.concatenate([a.reshape(-1).astype(F32) for a in arrs])
    return jnp.pad(flat, (0, rows * PACK_COLS - flat.shape[0])).reshape(rows, PACK_COLS)


def _unpack(packed, shapes):
    flat = packed.reshape(-1)
    out, off = [], 0
    for s in shapes:
        n = int(np.prod(s))
        out.append(flat[off:off + n].reshape(s))
        off += n
    return out


def _pack_rows(shapes, multiple):
    n = sum(int(np.prod(s)) for s in shapes)
    rows = -(-n // PACK_COLS)
    return -(-rows // multiple) * multiple


def kernel(x, meta_tokens, norm_mix_g, w_in, q_norm_g, k_norm_g, attn_sinks, ssm_lambda_re, ssm_lambda_im, ssm_log_step, ssm_b_re, ssm_b_im, ssm_c_re, ssm_c_im, ssm_d, w_glu, b_glu, attn_out_g, ssm_out_g, w_out, norm_mlp_g, w_up, w_down, loss_target, m_meta_tokens, m_norm_mix_g, m_w_in, m_q_norm_g, m_k_norm_g, m_attn_sinks, m_ssm_lambda_re, m_ssm_lambda_im, m_ssm_log_step, m_ssm_b_re, m_ssm_b_im, m_ssm_c_re, m_ssm_c_im, m_ssm_d, m_w_glu, m_b_glu, m_attn_out_g, m_ssm_out_g, m_w_out, m_norm_mlp_g, m_w_up, m_w_down, v_meta_tokens, v_norm_mix_g, v_w_in, v_q_norm_g, v_k_norm_g, v_attn_sinks, v_ssm_lambda_re, v_ssm_lambda_im, v_ssm_log_step, v_ssm_b_re, v_ssm_b_im, v_ssm_c_re, v_ssm_c_im, v_ssm_d, v_w_glu, v_b_glu, v_attn_out_g, v_ssm_out_g, v_w_out, v_norm_mlp_g, v_w_up, v_w_down):
    args = (meta_tokens, norm_mix_g, w_in, q_norm_g, k_norm_g, attn_sinks, ssm_lambda_re, ssm_lambda_im, ssm_log_step, ssm_b_re, ssm_b_im, ssm_c_re, ssm_c_im, ssm_d, w_glu, b_glu, attn_out_g, ssm_out_g, w_out, norm_mlp_g, w_up, w_down)
    ms = (m_meta_tokens, m_norm_mix_g, m_w_in, m_q_norm_g, m_k_norm_g, m_attn_sinks, m_ssm_lambda_re, m_ssm_lambda_im, m_ssm_log_step, m_ssm_b_re, m_ssm_b_im, m_ssm_c_re, m_ssm_c_im, m_ssm_d, m_w_glu, m_b_glu, m_attn_out_g, m_ssm_out_g, m_w_out, m_norm_mlp_g, m_w_up, m_w_down)
    vs = (v_meta_tokens, v_norm_mix_g, v_w_in, v_q_norm_g, v_k_norm_g, v_attn_sinks, v_ssm_lambda_re, v_ssm_lambda_im, v_ssm_log_step, v_ssm_b_re, v_ssm_b_im, v_ssm_c_re, v_ssm_c_im, v_ssm_d, v_w_glu, v_b_glu, v_attn_out_g, v_ssm_out_g, v_w_out, v_norm_mlp_g, v_w_up, v_w_down)
    W = dict(zip(WEIGHTS, args))
    M = dict(zip(WEIGHTS, ms))
    V = dict(zip(WEIGHTS, vs))
    depth = norm_mix_g.shape[0]
    seq, D = x.shape[1], x.shape[2]
    attn_w = D // 2
    kv_w = attn_w // KV_GROUP
    dims = dict(attn_w=attn_w, kv_w=kv_w, u_off=(attn_w + 2 * kv_w) // LANES)
    small_names = [n for n in WEIGHTS if n not in BIG and n != 'meta_tokens']
    chip = 2 * lax.axis_index("x") + lax.axis_index("y")

    gathers, started = [], jnp.zeros((), F32)
    for l in range(depth):
        placed = [_ag_place("ag_place_" + n, W[n], l, COL_SHARDED[n], BF16) for n in BIG]
        groups = [_ag_copies(a, COL_SHARDED[n]) for a, n in enumerate(BIG)]
        if l == 0:
            placed = [_ag_place("ag_place_meta", meta_tokens[None], 0, True, F32)] + placed
            groups = [_ag_copies(0, True)] + [_ag_copies(a + 1, COL_SHARDED[n]) for a, n in enumerate(BIG)]
        sems, bufs, token = _split_start("ag_start_%d" % l, placed, groups)
        gathers.append(dict(zip((['meta_tokens'] if l == 0 else []) + BIG, zip(sems, bufs))))
        started = started + token[0, 0]

    def gathered(l, n, after):
        sems, buf = gathers[l][n]
        return _split_wait("ag_wait_%d_%s" % (l, n), [buf], sems, after, _ag_copies(0, n == 'meta_tokens' or COL_SHARDED[n]))[0]

    h_res = jnp.concatenate([jnp.zeros((PAD, D), F32), gathered(0, 'meta_tokens', started.reshape(1, 1)), x[0]], axis=0)
    s5_flat3, s5_layers = _ssm_prepare(W)
    layer_p = []
    for l in range(depth):
        p = {n: W[n][l][None, :] for n in ('norm_mix_g', 'q_norm_g', 'k_norm_g', 'attn_sinks', 'ssm_d', 'b_glu', 'attn_out_g',
                                             'ssm_out_g', 'norm_mlp_g')}
        p['s5'] = s5_layers[l]
        layer_p.append(p)
    saved = []
    for l in range(depth):
        h_res, sv = _layer_fwd(h_res, layer_p[l], functools.partial(gathered, l), dims)
        saved.append(sv)
    loss_local, dx, dx16 = _loss(h_res, loss_target[0])
    loss = lax.psum(loss_local, ("x", "y", "c"))

    small_grads = [None] * depth
    shard_grads = {}
    pending = []

    def reduce_grads(l, names, grads):
        st = _rs_start("%d_%s" % (l, names[0]), list(grads), [COL_SHARDED[n] for n in names])
        st.update(layer=l, names=names, fresh=True)
        pending.append(st)
        return st['token'][0, 0]

    def tick(after):
        token = jnp.zeros((), F32)
        for st in list(pending):
            if st['fresh']:
                st['fresh'] = False
                continue
            _rs_advance(st, after)
            if 'result' in st:
                pending.remove(st)
                for a, n in enumerate(st['names']):
                    shard_grads[(st['layer'], n)] = (st['result'][0][a], st['result'][1][a])
            else:
                token = token + st['token'][0, 0]
        return token

    token = jnp.zeros((), F32)
    for l in reversed(range(depth)):
        dx, dx16, gs = _layer_bwd(dx, dx16, saved[l], layer_p[l], dims, functools.partial(reduce_grads, l), tick, token)
        saved[l] = None
        small_grads[l] = gs
        token = tick(dx)
    grad_x = dx[BLOCK:].reshape(x.shape)

    g_small = _ssm_param_grads(W, s5_flat3, [small_grads[l]['s5_raw'] for l in range(depth)])
    for n in small_names:
        if n not in g_small:
            g_small[n] = jnp.stack([small_grads[l][n].reshape(W[n].shape[1:]) for l in range(depth)])
    g_shapes = [(N_META, D)] + [W[n].shape for n in small_names]
    rows = _pack_rows(g_shapes, 8 * 2 * N_CHIPS)
    packed = _pack([dx[PAD:BLOCK]] + [g_small[n] for n in small_names], rows)
    out = {}

    def adamw_big(n):
        return _adamw_big("adamw_" + n, [shard_grads[(l, n)][0] for l in range(depth)],
                          [shard_grads[(l, n)][1] for l in range(depth)], W[n], M[n], V[n])

    early = ('w_up', 'w_down')
    tick(packed)
    while any((l, n) not in shard_grads for l in range(depth) for n in early):
        tick(packed)
    for n in early:
        out[n] = adamw_big(n)
    tick(out[early[-1]][1])
    red, = _reduce_scatter("small", [packed], [False], F32)
    red_full, = _all_gather("ag_small", [red], [False])
    while pending:
        tick(red_full)
    g_list = _unpack(red_full, g_shapes)
    g_meta = lax.dynamic_slice_in_dim(g_list[0], chip * meta_tokens.shape[1], meta_tokens.shape[1], axis=1)
    G = dict(zip(small_names, g_list[1:]))
    G['meta_tokens'] = g_meta

    for n in BIG:
        if n not in out:
            out[n] = adamw_big(n)
    for n in ['meta_tokens'] + small_names:
        rows2d = lambda a: a.reshape(-1, a.shape[-1])
        upd = _ew("adamw_" + n, _adam_math, [rows2d(W[n]), rows2d(G[n]), rows2d(M[n]), rows2d(V[n])], [F32] * 3)
        out[n] = (G[n], *[u.reshape(W[n].shape) for u in upd])
    return (loss, grad_x, *[out[n][0] for n in WEIGHTS], *[out[n][1] for n in WEIGHTS],
            *[out[n][2] for n in WEIGHTS], *[out[n][3] for n in WEIGHTS])
```

```python
import functools
import math

import numpy as np
import jax
import jax.numpy as jnp
from jax import lax
from jax.experimental import pallas as pl
from jax.experimental.pallas import tpu as pltpu

F32 = jnp.float32
BF16 = jnp.bfloat16
MESH = pl.DeviceIdType.MESH

N_META = 16
HEAD_DIM = 64
KV_GROUP = 4
GROUP_CH = 16
STATE = 64
BLOCK = 128
PAD = BLOCK - N_META
NORM_EPS = 1e-6
NEG_INF = -1e30
LANES = 128
V7X_VMEM_LIMIT_BYTES = 56 * 1024 * 1024
MM_VMEM_BUDGET_BYTES = 44 * 1024 * 1024

ADAM_LR, ADAM_B1, ADAM_B2, ADAM_EPS, ADAM_WD, ADAM_STEP = 0.001, 0.9, 0.999, 1e-08, 0.01, 10

WEIGHTS = ['meta_tokens', 'norm_mix_g', 'w_in', 'q_norm_g', 'k_norm_g', 'attn_sinks', 'ssm_lambda_re',
           'ssm_lambda_im', 'ssm_log_step', 'ssm_b_re', 'ssm_b_im', 'ssm_c_re', 'ssm_c_im', 'ssm_d', 'w_glu',
           'b_glu', 'attn_out_g', 'ssm_out_g', 'w_out', 'norm_mlp_g', 'w_up', 'w_down']
BIG = ['w_in', 'w_glu', 'w_out', 'w_up', 'w_down']
COL_SHARDED = {'w_in': True, 'w_glu': False, 'w_out': False, 'w_up': True, 'w_down': False}
N_CHIPS = 4


def _pick(n, cands):
    for c in cands:
        if c <= n and n % c == 0:
            return c
    return n


def _params(sem):
    return pltpu.CompilerParams(dimension_semantics=sem, vmem_limit_bytes=V7X_VMEM_LIMIT_BYTES)


def _pcall(body, **kw):
    return pl.pallas_call(body, **kw)


def _sds(shape, dtype):
    return jax.ShapeDtypeStruct(shape, dtype)


_DIMS = {'nn': ((1,), (0,)), 'nt': ((1,), (1,)), 'tn': ((0,), (0,))}


def _mm(name, a, b, mode, out_dtypes, extras=(), epi=None):
    if mode == 'nn':
        (M, K), (_, N) = a.shape, b.shape
    elif mode == 'nt':
        (M, K), (N, _) = a.shape, b.shape
    else:
        (K, M), (_, N) = a.shape, b.shape
    tile_bytes = 4 * len([k for k, _ in extras if k == 'tile']) + sum(jnp.dtype(d).itemsize for d in out_dtypes)

    def fits(tm, tn, tk):
        need = 2 * tm * tk * a.dtype.itemsize + 2 * tk * tn * b.dtype.itemsize + 4 * tm * tn + 2 * tm * tn * tile_bytes
        return need <= MM_VMEM_BUDGET_BYTES

    if mode == 'tn':
        tm, tk_cands = _pick(M, (1024, 512, 256, 128)), (1408, 704, 384, 128)
    else:
        tm, tk_cands = _pick(M, (1408, 704, 384, 128)), (2048, 1024, 512, 256, 128)
    tk_cands = [t for t in tk_cands if t <= K and K % t == 0] or [K]
    tn_cands = [t for t in (2048, 1280, 1024, 640, 512, 256, 128) if t <= N and N % t == 0] or [N]
    if mode != 'tn' and tk_cands[0] == K and a.dtype == BF16:
        tk_cands = tk_cands[:1]
    tn, tk = next(((tn_, tk_) for tn_ in tn_cands for tk_ in tk_cands if fits(tm, tn_, tk_)), (tn_cands[-1], tk_cands[-1]))
    nk = K // tk
    a_spec = pl.BlockSpec((tk, tm), lambda i, j, k: (k, i)) if mode == 'tn' else pl.BlockSpec((tm, tk), lambda i, j, k: (i, k))
    b_spec = pl.BlockSpec((tn, tk), lambda i, j, k: (j, k)) if mode == 'nt' else pl.BlockSpec((tk, tn), lambda i, j, k: (k, j))
    ex_specs = [pl.BlockSpec((tm, tn), lambda i, j, k: (i, j)) if kind == 'tile' else pl.BlockSpec((1, tn), lambda i, j, k: (0, j))
                for kind, _ in extras]
    ne, no = len(extras), len(out_dtypes)
    dims = (_DIMS[mode], ((), ()))

    def body(a_ref, b_ref, *rest):
        ex, outs, acc = rest[:ne], rest[ne:ne + no], rest[ne + no]
        k = pl.program_id(2)

        @pl.when(k == 0)
        def _():
            acc[...] = jnp.zeros_like(acc)

        acc[...] += lax.dot_general(a_ref[...].astype(BF16), b_ref[...].astype(BF16), dims, preferred_element_type=F32)

        @pl.when(k == nk - 1)
        def _():
            r = acc[...]
            res = epi(r, *[e[...] for e in ex]) if epi is not None else (r,)
            for o, v in zip(outs, res):
                o[...] = v.astype(o.dtype)

    outs = _pcall(
        body, name=name, grid=(M // tm, N // tn, nk),
        in_specs=[a_spec, b_spec] + ex_specs,
        out_specs=[pl.BlockSpec((tm, tn), lambda i, j, k: (i, j)) for _ in out_dtypes],
        out_shape=[_sds((M, N), d) for d in out_dtypes],
        scratch_shapes=[pltpu.VMEM((tm, tn), F32)],
        compiler_params=_params(("parallel", "parallel", "arbitrary")),
    )(a, b, *[e for _, e in extras])
    return outs


def _ew(name, fn, ins, out_dtypes):
    R, C = ins[0].shape
    tr = _pick(R, tuple(t for t in (1024, 512, 256, 128, 64, 32, 16, 8) if t * C <= 512 * 1024) or (8,))
    n_in = len(ins)

    def body(*refs):
        res = fn(*[r[...] for r in refs[:n_in]])
        for o, v in zip(refs[n_in:], res):
            o[...] = v.astype(o.dtype)

    spec = pl.BlockSpec((tr, C), lambda i: (i, 0))
    return _pcall(body, name=name, grid=(R // tr,), in_specs=[spec] * n_in, out_specs=[spec] * len(out_dtypes),
                  out_shape=[_sds((R, C), d) for d in out_dtypes], compiler_params=_params(("parallel",)))(*ins)


def _adam_math(w, g, m, v):
    m = ADAM_B1 * m + (1.0 - ADAM_B1) * g
    v = ADAM_B2 * v + (1.0 - ADAM_B2) * (g * g)
    m_hat = m / (1.0 - ADAM_B1 ** ADAM_STEP)
    v_hat = v / (1.0 - ADAM_B2 ** ADAM_STEP)
    delta = -ADAM_LR * (m_hat / (jnp.sqrt(v_hat) + ADAM_EPS) + ADAM_WD * w)
    return delta, m, v


def _rms_fwd(name, xs, gs, out_dtype):
    L = xs[0].shape[0]
    ws = [x.shape[1] for x in xs]
    n = len(xs)
    tr = _pick(L, (384, 256, 128))

    def body(*refs):
        o = refs[2 * n]
        off = 0
        for i in range(n):
            x = refs[i][...]
            r = lax.rsqrt(jnp.mean(x * x, axis=-1, keepdims=True) + NORM_EPS)
            o[:, off:off + ws[i]] = ((x * r) * refs[n + i][...]).astype(o.dtype)
            off += ws[i]

    return _pcall(body, name=name, grid=(L // tr,),
                  in_specs=[pl.BlockSpec((tr, w), lambda i: (i, 0)) for w in ws] + [pl.BlockSpec((1, w), lambda i: (0, 0)) for w in ws],
                  out_specs=pl.BlockSpec((tr, sum(ws)), lambda i: (i, 0)), out_shape=_sds((L, sum(ws)), out_dtype),
                  compiler_params=_params(("parallel",)))(*xs, *gs)


def _rms_bwd(name, xs, gs, dy, resid=None):
    L = xs[0].shape[0]
    ws = [x.shape[1] for x in xs]
    n = len(xs)
    tr = _pick(L, (384, 256, 128))
    has_res = resid is not None

    def body(*refs):
        x_refs, g_refs, dy_ref = refs[:n], refs[n:2 * n], refs[2 * n]
        p = 2 * n + 1
        res_ref = refs[p] if has_res else None
        p += 1 if has_res else 0
        dx_refs, dg_refs = refs[p:p + n], refs[p + n:p + 2 * n]
        dx16_ref = refs[p + 2 * n] if has_res else None
        first = pl.program_id(0) == 0
        off = 0
        for i in range(n):
            x = x_refs[i][...]
            d = dy_ref[:, off:off + ws[i]]
            r = lax.rsqrt(jnp.mean(x * x, axis=-1, keepdims=True) + NORM_EPS)
            xh = x * r
            dg = jnp.sum(d * xh, axis=0, keepdims=True)

            @pl.when(first)
            def _(i=i):
                dg_refs[i][...] = jnp.zeros_like(dg_refs[i])

            dg_refs[i][...] += dg
            dyg = d * g_refs[i][...]
            dx = r * (dyg - xh * jnp.mean(dyg * xh, axis=-1, keepdims=True))
            if has_res:
                dx = dx + res_ref[...]
                dx16_ref[...] = dx.astype(BF16)
            dx_refs[i][...] = dx
            off += ws[i]

    in_specs = ([pl.BlockSpec((tr, w), lambda i: (i, 0)) for w in ws] + [pl.BlockSpec((1, w), lambda i: (0, 0)) for w in ws]
                + [pl.BlockSpec((tr, sum(ws)), lambda i: (i, 0))])
    ins = list(xs) + list(gs) + [dy]
    if has_res:
        in_specs.append(pl.BlockSpec((tr, ws[0]), lambda i: (i, 0)))
        ins.append(resid)
    out_specs = [pl.BlockSpec((tr, w), lambda i: (i, 0)) for w in ws] + [pl.BlockSpec((1, w), lambda i: (0, 0)) for w in ws]
    out_shape = [_sds((L, w), F32) for w in ws] + [_sds((1, w), F32) for w in ws]
    if has_res:
        out_specs.append(pl.BlockSpec((tr, ws[0]), lambda i: (i, 0)))
        out_shape.append(_sds((L, ws[0]), BF16))
    outs = _pcall(body, name=name, grid=(L // tr,), in_specs=in_specs, out_specs=out_specs, out_shape=out_shape,
                  compiler_params=_params(("arbitrary",)))(*ins)
    return (outs[:n], outs[n:2 * n], outs[2 * n]) if has_res else (outs[:n], outs[n:])


def _loss(xl, target):
    Lp, D = xl.shape

    def body(x_ref, t_ref, dy_ref, dy16_ref, loss_ref):
        n = pl.program_id(0)

        @pl.when(n == 0)
        def _():
            loss_ref[...] = jnp.zeros_like(loss_ref)
            dy_ref[...] = jnp.zeros_like(dy_ref)
            dy16_ref[...] = jnp.zeros_like(dy16_ref)

        @pl.when(n > 0)
        def _():
            err = x_ref[...] - t_ref[...]
            dy = err * (1.0 / D)
            dy_ref[...] = dy
            dy16_ref[...] = dy.astype(BF16)
            loss_ref[...] += jnp.sum(err * err) * (0.5 / D)

    blk = pl.BlockSpec((BLOCK, D), lambda n: (n, 0))
    dy, dy16, loss = _pcall(body, name="loss_head", grid=(Lp // BLOCK,),
                            in_specs=[blk, pl.BlockSpec((BLOCK, D), lambda n: (jnp.maximum(n - 1, 0), 0))],
                            out_specs=[blk, blk, pl.BlockSpec((8, LANES), lambda n: (0, 0))],
                            out_shape=[_sds((Lp, D), F32), _sds((Lp, D), BF16), _sds((8, LANES), F32)],
                            compiler_params=_params(("arbitrary",)))(xl, target)
    return loss[0, 0], dy, dy16


def _attn_mask_dist(n):
    i = lax.broadcasted_iota(jnp.int32, (BLOCK, 3 * BLOCK), 0)
    j = lax.broadcasted_iota(jnp.int32, (BLOCK, 3 * BLOCK), 1)
    in_band = j < 2 * BLOCK
    band = in_band & (j > i) & (j <= i + BLOCK) & (j >= 2 * BLOCK - BLOCK * n)
    jm = j - 2 * BLOCK
    meta = (~in_band) & (jm >= PAD) & (jm <= BLOCK * n + i)
    hidden = jnp.where(band | meta, 0.0, NEG_INF)
    dist = jnp.where(in_band, BLOCK + i - j, BLOCK * n + i - jm).astype(F32)
    return jnp.concatenate([hidden] * KV_GROUP, axis=0), jnp.concatenate([dist] * KV_GROUP, axis=0)


def _head_norm(x, g):
    r = lax.rsqrt(jnp.mean(x * x, axis=-1, keepdims=True) + NORM_EPS)
    return (x * r) * g, r


def _attn_specs(attn_w, kv_w):
    kb = attn_w // kv_w
    q_spec = pl.BlockSpec((BLOCK, attn_w), lambda n: (n, 0))

    def kv(col):
        return [pl.BlockSpec((BLOCK, kv_w), lambda n: (jnp.maximum(n - 1, 0), col)),
                pl.BlockSpec((BLOCK, kv_w), lambda n: (n, col)),
                pl.BlockSpec((BLOCK, kv_w), lambda n: (0, col))]

    return q_spec, kv(kb), kv(kb + 1)


def _slopes(n_heads):
    return [2.0 ** (-8.0 * (h + 1) / n_heads) for h in range(n_heads)]


def _head_slice(h):
    return slice(h * HEAD_DIM, (h + 1) * HEAD_DIM)


def _stack_heads(ref, kh):
    return jnp.concatenate([ref[:, _head_slice(kh * KV_GROUP + g)] for g in range(KV_GROUP)], axis=0)


def _group_column(vals):
    return jnp.concatenate([jnp.broadcast_to(v, (BLOCK, 1)) for v in vals], axis=0)


def _group_inputs(kh, slopes, q_ref, kp, kc, km, vp, vc, vm, gq_ref, gk_ref, sk_ref):
    cs = _head_slice(kh)
    kn, _ = _head_norm(jnp.concatenate([kp[:, cs], kc[:, cs], km[:, cs]], axis=0), gk_ref[...])
    vcat = jnp.concatenate([vp[:, cs], vc[:, cs], vm[:, cs]], axis=0).astype(BF16)
    q = _stack_heads(q_ref, kh)
    qn, rq = _head_norm(q, gq_ref[...])
    heads = range(kh * KV_GROUP, (kh + 1) * KV_GROUP)
    slope = _group_column([jnp.full((1, 1), slopes[h], F32) for h in heads])
    sink = _group_column([sk_ref[0:1, h:h + 1] for h in heads])
    return q, qn, rq, kn, vcat, slope, sink


def _scores(qn, kn, slope, sink, hidden, dist):
    s = lax.dot_general(qn.astype(BF16), kn.astype(BF16), (((1,), (1,)), ((), ())), preferred_element_type=F32)
    s = (s * (1.0 / math.sqrt(HEAD_DIM)) - slope * dist) + hidden
    m = jnp.maximum(jnp.max(s, axis=-1, keepdims=True), sink)
    p = jnp.exp(s - m)
    ps = jnp.exp(sink - m)
    inv = 1.0 / (jnp.sum(p, axis=-1, keepdims=True) + ps)
    return p * inv, ps * inv


def _attn_fwd(proj, gq, gk, sinks, attn_w, kv_w):
    Lp = proj.shape[0]
    n_heads, n_kv = attn_w // HEAD_DIM, kv_w // HEAD_DIM
    slopes = _slopes(n_heads)
    q_spec, k_specs, v_specs = _attn_specs(attn_w, kv_w)

    def body(q_ref, kp, kc, km, vp, vc, vm, gq_ref, gk_ref, sk_ref, o_ref):
        mask, dist = _attn_mask_dist(pl.program_id(0))
        for kh in range(n_kv):
            _, qn, _, kn, vcat, slope, sink = _group_inputs(kh, slopes, q_ref, kp, kc, km, vp, vc, vm, gq_ref, gk_ref, sk_ref)
            p, _ = _scores(qn, kn, slope, sink, mask, dist)
            o = jnp.dot(p.astype(BF16), vcat, preferred_element_type=F32)
            for g in range(KV_GROUP):
                o_ref[:, _head_slice(kh * KV_GROUP + g)] = o[g * BLOCK:(g + 1) * BLOCK]

    small = lambda w: pl.BlockSpec((1, w), lambda n: (0, 0))
    return _pcall(body, name="attn_fwd", grid=(Lp // BLOCK,),
                  in_specs=[q_spec] + k_specs + v_specs + [small(HEAD_DIM), small(HEAD_DIM), small(n_heads)],
                  out_specs=pl.BlockSpec((BLOCK, attn_w), lambda n: (n, 0)), out_shape=_sds((Lp, attn_w), F32),
                  compiler_params=_params(("parallel",)))(proj, proj, proj, proj, proj, proj, proj, gq, gk, sinks)


def _attn_bwd(proj, attn, dattn, gq, gk, sinks, attn_w, kv_w):
    Lp = proj.shape[0]
    n_heads, n_kv = attn_w // HEAD_DIM, kv_w // HEAD_DIM
    slopes = _slopes(n_heads)
    q_spec, k_specs, v_specs = _attn_specs(attn_w, kv_w)
    scale = 1.0 / math.sqrt(HEAD_DIM)
    tn_dims = (((0,), (0,)), ((), ()))

    def body(q_ref, kp, kc, km, vp, vc, vm, o_ref, do_ref, gq_ref, gk_ref, sk_ref, dq_ref, dk_ref, dv_ref, dgq_ref, dsk_ref):
        n = pl.program_id(0)

        @pl.when(n == 0)
        def _():
            dk_ref[...] = jnp.zeros_like(dk_ref)
            dv_ref[...] = jnp.zeros_like(dv_ref)
            dgq_ref[...] = jnp.zeros_like(dgq_ref)
            dsk_ref[...] = jnp.zeros_like(dsk_ref)

        mask, dist = _attn_mask_dist(n)
        lane = lax.broadcasted_iota(jnp.int32, (1, n_heads), 1)
        rows_prev = pl.ds(pl.multiple_of(jnp.maximum(n - 1, 0) * BLOCK, BLOCK), BLOCK)
        rows_cur = pl.ds(pl.multiple_of(n * BLOCK, BLOCK), BLOCK)
        rows_meta = pl.ds(0, BLOCK)
        dgq = jnp.zeros((1, HEAD_DIM), F32)
        dsk = jnp.zeros((1, n_heads), F32)
        for kh in range(n_kv):
            cs = _head_slice(kh)
            q, qn, rq, kn, vcat, slope, sink = _group_inputs(kh, slopes, q_ref, kp, kc, km, vp, vc, vm, gq_ref, gk_ref, sk_ref)
            p, ps = _scores(qn, kn, slope, sink, mask, dist)
            do = _stack_heads(do_ref, kh)
            dd = jnp.sum(do * _stack_heads(o_ref, kh), axis=-1, keepdims=True)
            do16 = do.astype(BF16)
            dp = lax.dot_general(do16, vcat, (((1,), (1,)), ((), ())), preferred_element_type=F32)
            ds16 = (p * (dp - dd)).astype(BF16)
            dsink = -ps * dd
            for g in range(KV_GROUP):
                dsk = dsk + jnp.where(lane == kh * KV_GROUP + g, jnp.sum(dsink[g * BLOCK:(g + 1) * BLOCK]), 0.0)
            dqn = jnp.dot(ds16, kn.astype(BF16), preferred_element_type=F32) * scale
            dkn = lax.dot_general(ds16, qn.astype(BF16), tn_dims, preferred_element_type=F32) * scale
            dvc = lax.dot_general(p.astype(BF16), do16, tn_dims, preferred_element_type=F32)
            xh = q * rq
            dgq = dgq + jnp.sum(dqn * xh, axis=0, keepdims=True)
            dyg = dqn * gq_ref[...]
            dq = rq * (dyg - xh * jnp.mean(dyg * xh, axis=-1, keepdims=True))
            for g in range(KV_GROUP):
                dq_ref[:, _head_slice(kh * KV_GROUP + g)] = dq[g * BLOCK:(g + 1) * BLOCK].astype(dq_ref.dtype)
            for part, rows in enumerate((rows_prev, rows_cur, rows_meta)):
                ps_ = slice(part * BLOCK, (part + 1) * BLOCK)
                dk_ref[rows, cs] += dkn[ps_]
                dv_ref[rows, cs] += dvc[ps_]
        dgq_ref[...] += dgq
        dsk_ref[...] += dsk

    small = lambda w: pl.BlockSpec((1, w), lambda n: (0, 0))
    blk = pl.BlockSpec((BLOCK, attn_w), lambda n: (n, 0))
    whole = pl.BlockSpec((Lp, kv_w), lambda n: (0, 0))
    return _pcall(body, name="attn_bwd", grid=(Lp // BLOCK,),
                  in_specs=[q_spec] + k_specs + v_specs + [blk, blk, small(HEAD_DIM), small(HEAD_DIM), small(n_heads)],
                  out_specs=[blk, whole, whole, small(HEAD_DIM), small(n_heads)],
                  out_shape=[_sds(proj.shape, BF16), _sds((Lp, kv_w), F32), _sds((Lp, kv_w), F32),
                             _sds((1, HEAD_DIM), F32), _sds((1, n_heads), F32)],
                  compiler_params=_params(("arbitrary",)))(proj, proj, proj, proj, proj, proj, proj, attn, dattn, gq, gk, sinks)


def _knorm_bwd(proj, dkn, dv, gk, dproj, attn_w, kv_w):
    Lp = proj.shape[0]
    n_kv = kv_w // HEAD_DIM
    tr = _pick(Lp, (384, 256, 128))

    def body(k_ref, d_ref, dv_ref, g_ref, buf_ref, out_ref, dg_ref):
        @pl.when(pl.program_id(0) == 0)
        def _():
            dg_ref[...] = jnp.zeros_like(dg_ref)

        dg = jnp.zeros((1, HEAD_DIM), F32)
        for kh in range(n_kv):
            cs = slice(kh * HEAD_DIM, (kh + 1) * HEAD_DIM)
            x = k_ref[:, cs]
            d = d_ref[:, cs]
            r = lax.rsqrt(jnp.mean(x * x, axis=-1, keepdims=True) + NORM_EPS)
            xh = x * r
            dg = dg + jnp.sum(d * xh, axis=0, keepdims=True)
            dyg = d * g_ref[...]
            out_ref[:, cs] = (r * (dyg - xh * jnp.mean(dyg * xh, axis=-1, keepdims=True))).astype(out_ref.dtype)
        out_ref[:, kv_w:] = dv_ref[...].astype(out_ref.dtype)
        dg_ref[...] += dg

    kv_blk = pl.BlockSpec((tr, kv_w), lambda i: (i, 0))
    return _pcall(body, name="knorm_bwd", grid=(Lp // tr,),
                  in_specs=[pl.BlockSpec((tr, kv_w), lambda i: (i, attn_w // kv_w)), kv_blk, kv_blk,
                            pl.BlockSpec((1, HEAD_DIM), lambda i: (0, 0)), pl.BlockSpec(memory_space=pl.ANY)],
                  out_specs=[pl.BlockSpec((tr, 2 * kv_w), lambda i: (i, attn_w // (2 * kv_w))), pl.BlockSpec((1, HEAD_DIM), lambda i: (0, 0))],
                  out_shape=[_sds(dproj.shape, dproj.dtype), _sds((1, HEAD_DIM), F32)],
                  input_output_aliases={4: 0},
                  compiler_params=_params(("arbitrary",)))(proj, dkn, dv, gk, dproj)


def _ssm_bbar(lr, li, ls, br, bi):
    def fn(lr, li, ls, br, bi):
        fr, fi = _zoh_factor(lr, li, ls)
        return fr * br - fi * bi, fr * bi + fi * br

    return _ew("ssm_bbar", fn, [lr, li, ls, br, bi], [F32, F32])


def _lam_bar(lr, li, ls):
    dl = jnp.exp(ls)
    e = jnp.exp(lr * dl)
    return e * jnp.cos(li * dl), e * jnp.sin(li * dl), dl


def _zoh_factor(lr, li, ls):
    ar, ai, _ = _lam_bar(lr, li, ls)
    n2 = lr * lr + li * li
    ivr, ivi = lr / n2, -li / n2
    return (ar - 1.0) * ivr - ai * ivi, (ar - 1.0) * ivi + ai * ivr


SCAN_SHIFTS = (1, 2, 4)


def _ssm_tables(lr, li, ls):
    Wx = lr.shape[1]

    def body(lr_ref, li_ref, ls_ref, tf_ref, tr_ref):
        dl = jnp.exp(ls_ref[...])
        zr, zi = lr_ref[...] * dl, li_ref[...] * dl
        row = lax.broadcasted_iota(jnp.int32, (8, Wx), 0)

        def power(kf):
            e = jnp.exp(kf * zr)
            return e * jnp.cos(kf * zi), e * jnp.sin(kf * zi)

        for ref, rev in ((tf_ref, False), (tr_ref, True)):
            sgn = -1.0 if rev else 1.0
            for k, d in enumerate(SCAN_SHIFTS):
                ar, ai = power(jnp.full((8, Wx), float(d), F32))
                keep = (row < 8 - d) if rev else (row >= d)
                ref[k] = jnp.where(keep, ar, 0.0)
                ref[4 + k] = jnp.where(keep, sgn * ai, 0.0)
            pr, pi = power(((8 - row) if rev else (row + 1)).astype(F32))
            ref[3] = pr
            ref[7] = sgn * pi

    full = pl.BlockSpec((1, Wx), lambda: (0, 0))
    tab = pl.BlockSpec((8, 8, Wx), lambda: (0, 0, 0))
    return _pcall(body, name="ssm_tables", in_specs=[full] * 3, out_specs=[tab, tab],
                  out_shape=[_sds((8, 8, Wx), F32)] * 2,
                  compiler_params=pltpu.CompilerParams(vmem_limit_bytes=V7X_VMEM_LIMIT_BYTES))(lr, li, ls)


def _scan(name, br, bi, tab, layer, reverse, states=None):
    L, Wx = br.shape
    TB = _pick(L, (384, 256, 128))
    CW = _pick(Wx, (1024, 512, 256, 128))
    nT, nG = L // TB, TB // 8

    def body(*refs):
        if reverse:
            br_ref, bi_ref, xr_ref, xi_ref, tab_ref, or_ref, oi_ref, s1_ref, s2_ref, cr_ref, ci_ref = refs
        else:
            br_ref, bi_ref, tab_ref, or_ref, oi_ref, cr_ref, ci_ref = refs

        @pl.when(pl.program_id(1) == 0)
        def _():
            cr_ref[...] = jnp.zeros_like(cr_ref)
            ci_ref[...] = jnp.zeros_like(ci_ref)
            if reverse:
                s1_ref[...] = jnp.zeros_like(s1_ref)
                s2_ref[...] = jnp.zeros_like(s2_ref)

        def step(q, carry):
            cr, ci = carry[0], carry[1]
            g = (nG - 1 - q) if reverse else q
            rows = pl.ds(pl.multiple_of(g * 8, 8), 8)
            b_r, b_i = br_ref[rows, :], bi_ref[rows, :]
            sr, si = b_r, b_i
            for k, d in enumerate(SCAN_SHIFTS):
                mr, mi = tab_ref[k], tab_ref[4 + k]
                sh = (8 - d) if reverse else d
                pr, pi = pltpu.roll(sr, sh, 0), pltpu.roll(si, sh, 0)
                sr, si = sr + mr * pr - mi * pi, si + mr * pi + mi * pr
            pwr, pwi = tab_ref[3], tab_ref[7]
            xr = sr + pwr * cr - pwi * ci
            xi = si + pwr * ci + pwi * cr
            or_ref[rows, :] = xr
            oi_ref[rows, :] = xi
            row = 0 if reverse else 7
            out = (jnp.broadcast_to(xr[row:row + 1, :], xr.shape), jnp.broadcast_to(xi[row:row + 1, :], xi.shape))
            if reverse:
                hr, hi = xr - b_r, xi - b_i
                st_r, st_i = xr_ref[rows, :], xi_ref[rows, :]
                out = out + (carry[2] + hr * st_r + hi * st_i, carry[3] + hi * st_r - hr * st_i)
            return out

        init = (cr_ref[...], ci_ref[...])
        if reverse:
            init = init + (jnp.zeros((8, CW), F32), jnp.zeros((8, CW), F32))
        fin = lax.fori_loop(0, nG, step, init, unroll=2)
        cr_ref[...] = fin[0]
        ci_ref[...] = fin[1]
        if reverse:
            s1_ref[...] += fin[2]
            s2_ref[...] += fin[3]

    tmap = (lambda j, t: (nT - 1 - t, j)) if reverse else (lambda j, t: (t, j))
    blk = pl.BlockSpec((TB, CW), tmap)
    tab_spec = pl.BlockSpec((8, 8, CW), lambda j, t: (0, 0, layer * (Wx // CW) + j))
    sum_spec = pl.BlockSpec((8, CW), lambda j, t: (0, j))
    ins = [br, bi] + (list(states) if reverse else []) + [tab]
    in_specs = [blk, blk] + ([blk, blk] if reverse else []) + [tab_spec]
    out_specs = [blk, blk] + ([sum_spec, sum_spec] if reverse else [])
    out_shape = [_sds((L, Wx), F32)] * 2 + ([_sds((8, Wx), F32)] * 2 if reverse else [])
    return _pcall(body, name=name, grid=(Wx // CW, nT), in_specs=in_specs, out_specs=out_specs, out_shape=out_shape,
                  scratch_shapes=[pltpu.VMEM((8, CW), F32), pltpu.VMEM((8, CW), F32)],
                  compiler_params=_params(("parallel", "arbitrary")))(*ins)


def _row_tile(L):
    return _pick(L, (1408, 704, 384, 128))


TILES_PER_BLOCK = 4


def _blockproj(name, src, off, w_r, w_i, layer, depth):
    L = src.shape[0]
    T = w_r.shape[0] // depth
    tm = _row_tile(L)
    wide = TILES_PER_BLOCK * LANES

    def body(s_ref, wr_ref, wi_ref, or_ref, oi_ref):
        s = s_ref[...].astype(BF16)
        for k in range(TILES_PER_BLOCK):
            cols = slice(k * LANES, (k + 1) * LANES)
            or_ref[:, cols] = jnp.dot(s, wr_ref[k], preferred_element_type=F32)
            oi_ref[:, cols] = jnp.dot(s, wi_ref[k], preferred_element_type=F32)

    w_spec = pl.BlockSpec((TILES_PER_BLOCK, LANES, LANES), lambda i, q: (layer * (T // TILES_PER_BLOCK) + q, 0, 0))
    o_spec = pl.BlockSpec((tm, wide), lambda i, q: (i, q))
    return _pcall(body, name=name, grid=(L // tm, T // TILES_PER_BLOCK),
                  in_specs=[pl.BlockSpec((tm, LANES), lambda i, q: (i, off + q)), w_spec, w_spec],
                  out_specs=[o_spec, o_spec], out_shape=[_sds((L, T * LANES), F32)] * 2,
                  compiler_params=_params(("parallel", "arbitrary")))(src, w_r, w_i)


def _blockproj_grad(name, src, off, gr, gi):
    L = src.shape[0]
    T = gr.shape[1] // LANES
    tm = _row_tile(L)
    wide = TILES_PER_BLOCK * LANES
    tn_dims = (((0,), (0,)), ((), ()))

    def body(s_ref, gr_ref, gi_ref, or_ref, oi_ref):
        @pl.when(pl.program_id(1) == 0)
        def _():
            or_ref[...] = jnp.zeros_like(or_ref)
            oi_ref[...] = jnp.zeros_like(oi_ref)

        s = s_ref[...].astype(BF16)
        for k in range(TILES_PER_BLOCK):
            cols = slice(k * LANES, (k + 1) * LANES)
            or_ref[k] += lax.dot_general(s, gr_ref[:, cols].astype(BF16), tn_dims, preferred_element_type=F32)
            oi_ref[k] += lax.dot_general(s, gi_ref[:, cols].astype(BF16), tn_dims, preferred_element_type=F32)

    g_spec = pl.BlockSpec((tm, wide), lambda q, i: (i, q))
    o_spec = pl.BlockSpec((TILES_PER_BLOCK, LANES, LANES), lambda q, i: (q, 0, 0))
    return _pcall(body, name=name, grid=(T // TILES_PER_BLOCK, L // tm),
                  in_specs=[pl.BlockSpec((tm, LANES), lambda q, i: (i, off + q)), g_spec, g_spec],
                  out_specs=[o_spec, o_spec], out_shape=[_sds((T, LANES, LANES), F32)] * 2,
                  compiler_params=_params(("parallel", "arbitrary")))(src, gr, gi)


def _gelu(y):
    k = math.sqrt(2.0 / math.pi)
    return 0.5 * y * (1.0 + jnp.tanh(k * (y + 0.044715 * (y * y * y))))


def _gelu_grad(y):
    k = math.sqrt(2.0 / math.pi)
    t = jnp.tanh(k * (y + 0.044715 * (y * y * y)))
    return 0.5 * (1.0 + t) + 0.5 * y * (1.0 - t * t) * (k * (1.0 + 3 * 0.044715 * (y * y)))


def _ssm_out(xr, xi, w_r, w_i, proj, u_off, dvec, layer, depth):
    L = xr.shape[0]
    J = w_r.shape[0] // depth
    SW = w_r.shape[1]
    tm = _row_tile(L)

    def body(xr_ref, xi_ref, wr_ref, wi_ref, u_ref, d_ref, y_ref, gl_ref):
        acc = jnp.dot(xr_ref[...].astype(BF16), wr_ref[...], preferred_element_type=F32)
        acc += jnp.dot(xi_ref[...].astype(BF16), wi_ref[...], preferred_element_type=F32)
        y = acc + d_ref[...] * u_ref[...]
        y_ref[...] = y
        gl_ref[...] = _gelu(y)

    x_spec = pl.BlockSpec((tm, SW), lambda j, i: (i, j))
    w_spec = pl.BlockSpec((None, SW, LANES), lambda j, i: (layer * J + j, 0, 0))
    o_spec = pl.BlockSpec((tm, LANES), lambda j, i: (i, j))
    return _pcall(body, name="ssm_out", grid=(J, L // tm),
                  in_specs=[x_spec, x_spec, w_spec, w_spec, pl.BlockSpec((tm, LANES), lambda j, i: (i, u_off + j)),
                            pl.BlockSpec((1, LANES), lambda j, i: (0, j))],
                  out_specs=[o_spec, o_spec], out_shape=[_sds((L, J * LANES), F32)] * 2,
                  compiler_params=_params(("parallel", "parallel")))(xr, xi, w_r, w_i, proj, dvec)


def _ssm_du(gr, gi, w_r, w_i, dy, proj, u_off, dvec, dproj, layer, depth):
    L = gr.shape[0]
    J = w_r.shape[0] // depth
    SW = w_r.shape[1]
    tm = _row_tile(L)

    def body(gr_ref, gi_ref, wr_ref, wi_ref, dy_ref, u_ref, d_ref, buf_ref, du_ref, dd_ref):
        i = pl.program_id(1)

        @pl.when(i == 0)
        def _():
            dd_ref[...] = jnp.zeros_like(dd_ref)

        acc = jnp.dot(gr_ref[...].astype(BF16), wr_ref[...], preferred_element_type=F32)
        acc += jnp.dot(gi_ref[...].astype(BF16), wi_ref[...], preferred_element_type=F32)
        dy = dy_ref[...]
        row = lax.broadcasted_iota(jnp.int32, (tm, LANES), 0) + i * tm
        du_ref[...] = jnp.where(row >= PAD, acc + d_ref[...] * dy, 0.0).astype(du_ref.dtype)
        dd_ref[...] += jnp.sum(dy * u_ref[...], axis=0, keepdims=True)

    x_spec = pl.BlockSpec((tm, SW), lambda j, i: (i, j))
    w_spec = pl.BlockSpec((None, SW, LANES), lambda j, i: (layer * J + j, 0, 0))
    o_spec = pl.BlockSpec((tm, LANES), lambda j, i: (i, j))
    u_spec = pl.BlockSpec((tm, LANES), lambda j, i: (i, u_off + j))
    vec = pl.BlockSpec((1, LANES), lambda j, i: (0, j))
    return _pcall(body, name="ssm_du", grid=(J, L // tm),
                  in_specs=[x_spec, x_spec, w_spec, w_spec, o_spec, u_spec, vec, pl.BlockSpec(memory_space=pl.ANY)],
                  out_specs=[u_spec, vec], out_shape=[_sds(dproj.shape, dproj.dtype), _sds((1, J * LANES), F32)],
                  input_output_aliases={7: 0},
                  compiler_params=_params(("parallel", "arbitrary")))(gr, gi, w_r, w_i, dy, proj, dvec, dproj)


def _ssm_dc(xr, xi, dy, SW):
    L = xr.shape[0]
    J = dy.shape[1] // LANES
    tm = _row_tile(L)
    tn_dims = (((0,), (0,)), ((), ()))

    def body(xr_ref, xi_ref, dy_ref, or_ref, oi_ref):
        @pl.when(pl.program_id(1) == 0)
        def _():
            or_ref[...] = jnp.zeros_like(or_ref)
            oi_ref[...] = jnp.zeros_like(oi_ref)

        d = dy_ref[...].astype(BF16)
        or_ref[...] += lax.dot_general(xr_ref[...].astype(BF16), d, tn_dims, preferred_element_type=F32)
        oi_ref[...] += lax.dot_general(xi_ref[...].astype(BF16), d, tn_dims, preferred_element_type=F32)

    x_spec = pl.BlockSpec((tm, SW), lambda j, i: (i, j))
    o_spec = pl.BlockSpec((None, SW, LANES), lambda j, i: (j, 0, 0))
    return _pcall(body, name="ssm_dc", grid=(J, L // tm),
                  in_specs=[x_spec, x_spec, pl.BlockSpec((tm, LANES), lambda j, i: (i, j))],
                  out_specs=[o_spec, o_spec], out_shape=[_sds((J, SW, LANES), F32)] * 2,
                  compiler_params=_params(("parallel", "arbitrary")))(xr, xi, dy)


SCAN_COLS = 1024
SCAN_TILES = SCAN_COLS // LANES
SCAN_CH = SCAN_COLS // STATE * GROUP_CH
SCAN_SLAB = 8 * STATE


def _scan_rows(b_r, b_i, tab_ref, carry, reverse):
    sr, si = b_r, b_i
    for k, d in enumerate(SCAN_SHIFTS):
        mr, mi = tab_ref[k], tab_ref[4 + k]
        sh = (8 - d) if reverse else d
        pr, pi = pltpu.roll(sr, sh, 0), pltpu.roll(si, sh, 0)
        sr, si = sr + mr * pr - mi * pi, si + mr * pi + mi * pr
    pwr, pwi = tab_ref[3], tab_ref[7]
    xr = sr + pwr * carry[0] - pwi * carry[1]
    xi = si + pwr * carry[1] + pwi * carry[0]
    row = 0 if reverse else 7
    return xr, xi, (jnp.broadcast_to(xr[row:row + 1, :], xr.shape), jnp.broadcast_to(xi[row:row + 1, :], xi.shape))


def _ssm_fwd(proj, u_off, s, dvec):
    L = proj.shape[0]
    layer, depth = s['layer'], s['depth']
    T, J = s['wb'][0].shape[0] // depth, s['wc'][0].shape[0] // depth
    Wx, nC = T * LANES, T * LANES // SCAN_COLS
    TB = _pick(L, (384, 256, 128))
    nT, nG = L // TB, TB // 8

    def body(u_ref, wbr_ref, wbi_ref, wcr_ref, wci_ref, d_ref, tab_ref, xr_ref, xi_ref, y_ref, gl_ref, cr_ref, ci_ref):
        @pl.when(pl.program_id(1) == 0)
        def _():
            cr_ref[...] = jnp.zeros_like(cr_ref)
            ci_ref[...] = jnp.zeros_like(ci_ref)

        u = u_ref[...]
        u16 = u.astype(BF16)
        for k in range(SCAN_TILES):
            blk = u16[:, (k // TILES_PER_BLOCK) * LANES:(k // TILES_PER_BLOCK + 1) * LANES]
            cols = slice(k * LANES, (k + 1) * LANES)
            xr_ref[:, cols] = jnp.dot(blk, wbr_ref[k], preferred_element_type=F32)
            xi_ref[:, cols] = jnp.dot(blk, wbi_ref[k], preferred_element_type=F32)

        def step(q, carry):
            rows = pl.ds(pl.multiple_of(q * 8, 8), 8)
            xr, xi, carry = _scan_rows(xr_ref[rows, :], xi_ref[rows, :], tab_ref, carry, False)
            xr_ref[rows, :] = xr
            xi_ref[rows, :] = xi
            return carry

        cr, ci = lax.fori_loop(0, nG, step, (cr_ref[...], ci_ref[...]), unroll=2)
        cr_ref[...] = cr
        ci_ref[...] = ci
        for c in range(SCAN_CH // LANES):
            slab, ch = slice(c * SCAN_SLAB, (c + 1) * SCAN_SLAB), slice(c * LANES, (c + 1) * LANES)
            acc = jnp.dot(xr_ref[:, slab].astype(BF16), wcr_ref[c], preferred_element_type=F32)
            acc += jnp.dot(xi_ref[:, slab].astype(BF16), wci_ref[c], preferred_element_type=F32)
            y = acc + d_ref[:, ch] * u[:, ch]
            y_ref[:, ch] = y
            gl_ref[:, ch] = _gelu(y)

    x_spec = pl.BlockSpec((TB, SCAN_COLS), lambda j, t: (t, j))
    y_spec = pl.BlockSpec((TB, SCAN_CH), lambda j, t: (t, j))
    tile_w = pl.BlockSpec((SCAN_TILES, LANES, LANES), lambda j, t: (layer * nC + j, 0, 0))
    slab_w = pl.BlockSpec((SCAN_CH // LANES, SCAN_SLAB, LANES), lambda j, t: (layer * nC + j, 0, 0))
    return _pcall(body, name="ssm_fwd", grid=(nC, nT),
                  in_specs=[pl.BlockSpec((TB, SCAN_CH), lambda j, t: (t, u_off * LANES // SCAN_CH + j)), tile_w, tile_w, slab_w, slab_w,
                            pl.BlockSpec((1, SCAN_CH), lambda j, t: (0, j)),
                            pl.BlockSpec((8, 8, SCAN_COLS), lambda j, t: (0, 0, layer * nC + j))],
                  out_specs=[x_spec, x_spec, y_spec, y_spec],
                  out_shape=[_sds((L, Wx), F32)] * 2 + [_sds((L, J * LANES), F32)] * 2,
                  scratch_shapes=[pltpu.VMEM((8, SCAN_COLS), F32)] * 2,
                  compiler_params=_params(("parallel", "arbitrary")))(proj, *s['wb'], *s['wc'], dvec, s['tf'])


def _ssm_bwd(dy, proj, u_off, xr, xi, s, dvec, dproj):
    L = dy.shape[0]
    layer, depth = s['layer'], s['depth']
    T, J = s['wb'][0].shape[0] // depth, s['wc'][0].shape[0] // depth
    Wx, nC = T * LANES, T * LANES // SCAN_COLS
    TB = _pick(L, (384, 256, 128))
    nT, nG = L // TB, TB // 8
    n_ch = SCAN_CH // LANES
    tn_dims = (((0,), (0,)), ((), ()))

    def body(dy_ref, u_ref, xr_ref, xi_ref, wcr_ref, wci_ref, wbr_ref, wbi_ref, d_ref, tab_ref, buf_ref,
             du_ref, dd_ref, dwbr_ref, dwbi_ref, dwcr_ref, dwci_ref, s1_ref, s2_ref, gr_ref, gi_ref, cr_ref, ci_ref):
        t = pl.program_id(1)

        @pl.when(t == 0)
        def _():
            for ref in (cr_ref, ci_ref, dd_ref, dwbr_ref, dwbi_ref, dwcr_ref, dwci_ref, s1_ref, s2_ref):
                ref[...] = jnp.zeros_like(ref)

        dy = dy_ref[...]
        dy16 = dy.astype(BF16)
        u = u_ref[...]
        u16 = u.astype(BF16)
        for k in range(SCAN_TILES):
            blk = dy16[:, (k // TILES_PER_BLOCK) * LANES:(k // TILES_PER_BLOCK + 1) * LANES]
            cols = slice(k * LANES, (k + 1) * LANES)
            gr_ref[:, cols] = jnp.dot(blk, wcr_ref[k], preferred_element_type=F32)
            gi_ref[:, cols] = jnp.dot(blk, wci_ref[k], preferred_element_type=F32)

        def step(q, carry):
            rows = pl.ds(pl.multiple_of((nG - 1 - q) * 8, 8), 8)
            b_r, b_i = gr_ref[rows, :], gi_ref[rows, :]
            g_r, g_i, edge = _scan_rows(b_r, b_i, tab_ref, carry[:2], True)
            gr_ref[rows, :] = g_r
            gi_ref[rows, :] = g_i
            hr, hi = g_r - b_r, g_i - b_i
            st_r, st_i = xr_ref[rows, :], xi_ref[rows, :]
            return edge + (carry[2] + hr * st_r + hi * st_i, carry[3] + hi * st_r - hr * st_i)

        zero = jnp.zeros((8, SCAN_COLS), F32)
        fin = lax.fori_loop(0, nG, step, (cr_ref[...], ci_ref[...], zero, zero), unroll=2)
        cr_ref[...] = fin[0]
        ci_ref[...] = fin[1]
        s1_ref[...] += fin[2]
        s2_ref[...] += fin[3]

        row = lax.broadcasted_iota(jnp.int32, (TB, LANES), 0) + (nT - 1 - t) * TB
        for c in range(n_ch):
            slab, ch = slice(c * SCAN_SLAB, (c + 1) * SCAN_SLAB), slice(c * LANES, (c + 1) * LANES)
            g16r, g16i = gr_ref[:, slab].astype(BF16), gi_ref[:, slab].astype(BF16)
            acc = jnp.dot(g16r, wbr_ref[c], preferred_element_type=F32) + jnp.dot(g16i, wbi_ref[c], preferred_element_type=F32)
            du_ref[:, ch] = jnp.where(row >= PAD, acc + d_ref[:, ch] * dy[:, ch], 0.0).astype(du_ref.dtype)
            dwcr_ref[c] += lax.dot_general(xr_ref[:, slab].astype(BF16), dy16[:, ch], tn_dims, preferred_element_type=F32)
            dwci_ref[c] += lax.dot_general(xi_ref[:, slab].astype(BF16), dy16[:, ch], tn_dims, preferred_element_type=F32)
            for kk in range(TILES_PER_BLOCK):
                k = c * TILES_PER_BLOCK + kk
                cols = slice(kk * LANES, (kk + 1) * LANES)
                dwbr_ref[k] += lax.dot_general(u16[:, ch], g16r[:, cols], tn_dims, preferred_element_type=F32)
                dwbi_ref[k] += lax.dot_general(u16[:, ch], g16i[:, cols], tn_dims, preferred_element_type=F32)
        dd_ref[...] += jnp.sum(dy * u, axis=0, keepdims=True)

    rev = lambda j, t: (nT - 1 - t, j)
    x_spec = pl.BlockSpec((TB, SCAN_COLS), rev)
    u_spec = pl.BlockSpec((TB, SCAN_CH), lambda j, t: (nT - 1 - t, u_off * LANES // SCAN_CH + j))
    tile_w = pl.BlockSpec((SCAN_TILES, LANES, LANES), lambda j, t: (layer * nC + j, 0, 0))
    slab_w = pl.BlockSpec((n_ch, SCAN_SLAB, LANES), lambda j, t: (layer * nC + j, 0, 0))
    tile_g = pl.BlockSpec((SCAN_TILES, LANES, LANES), lambda j, t: (j, 0, 0))
    slab_g = pl.BlockSpec((n_ch, SCAN_SLAB, LANES), lambda j, t: (j, 0, 0))
    vec = pl.BlockSpec((1, SCAN_CH), lambda j, t: (0, j))
    sums = pl.BlockSpec((8, SCAN_COLS), lambda j, t: (0, j))
    return _pcall(body, name="ssm_bwd", grid=(nC, nT),
                  in_specs=[pl.BlockSpec((TB, SCAN_CH), rev), u_spec, x_spec, x_spec, tile_w, tile_w, slab_w, slab_w, vec,
                            pl.BlockSpec((8, 8, SCAN_COLS), lambda j, t: (0, 0, layer * nC + j)), pl.BlockSpec(memory_space=pl.ANY)],
                  out_specs=[u_spec, vec, tile_g, tile_g, slab_g, slab_g, sums, sums],
                  out_shape=[_sds(dproj.shape, dproj.dtype), _sds((1, J * LANES), F32), _sds((T, LANES, LANES), F32),
                             _sds((T, LANES, LANES), F32), _sds((J, SCAN_SLAB, LANES), F32), _sds((J, SCAN_SLAB, LANES), F32),
                             _sds((8, Wx), F32), _sds((8, Wx), F32)],
                  input_output_aliases={10: 0},
                  scratch_shapes=[pltpu.VMEM((TB, SCAN_COLS), F32)] * 2 + [pltpu.VMEM((8, SCAN_COLS), F32)] * 2,
                  compiler_params=_params(("parallel", "arbitrary")))(dy, proj, xr, xi, *s['wcT'], *s['wbT'], dvec, s['tr'], dproj)


def _glu_dz(ds, gl, z):
    L, W = ds.shape
    tr = _pick(L, (384, 256, 128))

    def body(ds_ref, gl_ref, z_ref, dz_ref, db_ref):
        @pl.when(pl.program_id(0) == 0)
        def _():
            db_ref[...] = jnp.zeros_like(db_ref)

        sg = jax.nn.sigmoid(z_ref[...])
        dz = ds_ref[...] * gl_ref[...] * (sg * (1.0 - sg))
        dz_ref[...] = dz.astype(BF16)
        db_ref[...] += jnp.sum(dz, axis=0, keepdims=True)

    spec = pl.BlockSpec((tr, W), lambda i: (i, 0))
    vec = pl.BlockSpec((1, W), lambda i: (0, 0))
    return _pcall(body, name="glu_dz", grid=(L // tr,), in_specs=[spec] * 3, out_specs=[spec, vec],
                  out_shape=[_sds((L, W), BF16), _sds((1, W), F32)], compiler_params=_params(("arbitrary",)))(ds, gl, z)


def _ssm_param_bwd_flat(lr, li, ls, br, bi, dbbr, dbbi):
    def seg_sum(x):
        for s in (8, 4, 2, 1):
            x = x + pltpu.roll(x, LANES - s, 1)
        return x

    def fn(lr, li, ls, br, bi, dbbr, dbbi):
        fr, fi = _zoh_factor(lr, li, ls)
        return (fr * dbbr + fi * dbbi, fr * dbbi - fi * dbbr,
                seg_sum(br * dbbr + bi * dbbi), seg_sum(br * dbbi - bi * dbbr))

    return _ew("ssm_param_bwd_flat", fn, [lr, li, ls, br, bi, dbbr, dbbi], [F32] * 4)


def _ssm_param_bwd(lr, li, ls, dfr, dfi, s1, s2):
    G, P = lr.shape

    def body(lr_ref, li_ref, ls_ref, dfr_ref, dfi_ref, s1_ref, s2_ref, dlr_ref, dli_ref, dls_ref):
        lr, li = lr_ref[...], li_ref[...]
        ar, ai, dl = _lam_bar(lr, li, ls_ref[...])
        sr, si = s1_ref[0], s2_ref[0]
        for k in range(1, 8):
            sr = sr + s1_ref[k]
            si = si + s2_ref[k]
        a2 = ar * ar + ai * ai
        gar, gai = (sr * ar - si * ai) / a2, (sr * ai + si * ar) / a2
        n2 = lr * lr + li * li
        ivr, ivi = lr / n2, -li / n2
        fr = (ar - 1.0) * ivr - ai * ivi
        fi = (ar - 1.0) * ivi + ai * ivr
        dfr, dfi = dfr_ref[...], dfi_ref[...]
        gar = gar + ivr * dfr + ivi * dfi
        gai = gai + ivr * dfi - ivi * dfr
        wr, wi = -(fr * ivr - fi * ivi), -(fr * ivi + fi * ivr)
        glr, gli = wr * dfr + wi * dfi, wr * dfi - wi * dfr
        gzr, gzi = ar * gar + ai * gai, ar * gai - ai * gar
        dlr_ref[...] = glr + dl * gzr
        dli_ref[...] = gli + dl * gzi
        dls_ref[...] = dl * jnp.sum(lr * gzr + li * gzi, axis=-1, keepdims=True)

    m = pl.BlockSpec((G, P), lambda: (0, 0))
    v = pl.BlockSpec((G, 1), lambda: (0, 0))
    s = pl.BlockSpec((8, G, P), lambda: (0, 0, 0))
    return _pcall(body, name="ssm_param_bwd", in_specs=[m, m, v, m, m, s, s], out_specs=[m, m, v],
                  out_shape=[_sds((G, P), F32), _sds((G, P), F32), _sds((G, 1), F32)])(lr, li, ls, dfr, dfi, s1, s2)


def _tile_mask(G):
    T = G // 2
    e = np.zeros((T, 8, 1, 2, 1), np.float32)
    for t in range(T):
        for c in range(2):
            e[t, (2 * t + c) % 8, 0, c, 0] = 1.0
    return e


def _tile_w(arr):
    T = arr.shape[0] // 2
    a = arr.reshape(T, 2, STATE, GROUP_CH).transpose(0, 3, 1, 2).reshape(T, 1, GROUP_CH, LANES).astype(BF16)
    shape = (T, 8, 1, LANES)
    t, e, lane = (lax.broadcasted_iota(jnp.int32, shape, d) for d in (0, 1, 3))
    return jnp.where(e == (2 * t + lane // STATE) % 8, a, 0).reshape(T, LANES, LANES)


def _tile_w_grad(dw):
    G = dw.shape[0] * 2
    d = dw.reshape(G // 2, 8, GROUP_CH, 2, STATE) * _tile_mask(G)
    return d.sum(axis=1).transpose(0, 2, 3, 1).reshape(G, STATE, GROUP_CH)


def _slab_w(arr):
    J = arr.shape[0] // 8
    a = jnp.tile(arr.reshape(J, 8, STATE, GROUP_CH).astype(BF16), (1, 1, 1, 8))
    g, lane = (lax.broadcasted_iota(jnp.int32, (1, 8, 1, LANES), d) for d in (1, 3))
    return jnp.where(g == lane // GROUP_CH, a, 0).reshape(J, 8 * STATE, LANES)


def _slab_w_grad(dw):
    J = dw.shape[0]
    eye = np.eye(8, dtype=np.float32).reshape(1, 8, 1, 8, 1)
    return (dw.reshape(J, 8, STATE, 8, GROUP_CH) * eye).sum(axis=3).reshape(J * 8, STATE, GROUP_CH)


def _exchange(name, ins, out_sds, remote, local, aliases=None):
    n_in, n_out, n_r, n_l = len(ins), len(out_sds), len(remote), len(local)

    def body(*refs):
        in_refs, out_refs = refs[:n_in], refs[n_in:n_in + n_out]
        send_sems, recv_sems, local_sems = refs[n_in + n_out:]
        x, y, c = lax.axis_index("x"), lax.axis_index("y"), lax.axis_index("c")

        def place(px, py, pc):
            return dict(x=px, y=py, c=pc, chip=2 * px + py)

        def flip(mask):
            mx, my, mc = mask
            return ((1 - x) if mx else x, (1 - y) if my else y, (1 - c) if mc else c)

        me = place(x, y, c)
        sends = []
        for k, (ii, src, oi, dst, mask) in enumerate(remote):
            cp = pltpu.make_async_remote_copy(src_ref=src(in_refs[ii], me), dst_ref=dst(out_refs[oi], me),
                                              send_sem=send_sems.at[k], recv_sem=recv_sems.at[k],
                                              device_id=flip(mask), device_id_type=MESH)
            cp.start()
            sends.append(cp)
        locals_ = []
        for k, (ii, src, oi, dst) in enumerate(local):
            cp = pltpu.make_async_copy(src(in_refs[ii], me), dst(out_refs[oi], me), local_sems.at[k])
            cp.start()
            locals_.append(cp)
        for k, (ii, src, oi, dst, mask) in enumerate(remote):
            sends[k].wait_send()
            peer = flip(mask)
            pltpu.make_async_remote_copy(src_ref=src(in_refs[ii], me), dst_ref=dst(out_refs[oi], place(*peer)),
                                         send_sem=send_sems.at[k], recv_sem=recv_sems.at[k],
                                         device_id=peer, device_id_type=MESH).wait_recv()
        for cp in locals_:
            cp.wait()

    any_spec = pl.BlockSpec(memory_space=pl.ANY)
    return _pcall(body, name=name, in_specs=[any_spec] * n_in, out_specs=[any_spec] * n_out, out_shape=list(out_sds),
                  input_output_aliases=aliases or {},
                  scratch_shapes=[pltpu.SemaphoreType.DMA((n_r,)), pltpu.SemaphoreType.DMA((n_r,)),
                                  pltpu.SemaphoreType.DMA((max(n_l, 1),))])(*ins)


def _mesh_place():
    x, y, c = lax.axis_index("x"), lax.axis_index("y"), lax.axis_index("c")

    def place(px, py, pc):
        return dict(x=px, y=py, c=pc, chip=2 * px + py)

    def flip(mask):
        mx, my, mc = mask
        return ((1 - x) if mx else x, (1 - y) if my else y, (1 - c) if mc else c)

    return place(x, y, c), place, flip


_HBM = pl.BlockSpec(memory_space=pltpu.HBM)
_SEM = pl.BlockSpec(memory_space=pltpu.SEMAPHORE)
_EFFECT = pltpu.SideEffectType.DATAFLOW_SIDE_EFFECTING


def _split_start(name, bufs, groups):
    n, ng = len(bufs), len(groups)

    def body(*refs):
        in_refs, sems, token = refs[:n], refs[n:n + 2 * ng], refs[-1]
        me, _, flip = _mesh_place()
        for g, copies in enumerate(groups):
            for k, (si, src, di, dst, mask) in enumerate(copies):
                pltpu.make_async_remote_copy(src_ref=src(in_refs[si], me), dst_ref=dst(in_refs[di], me),
                                             send_sem=sems[2 * g].at[k], recv_sem=sems[2 * g + 1].at[k],
                                             device_id=flip(mask), device_id_type=MESH).start()
        token[...] = jnp.zeros_like(token)

    outs = _pcall(body, name=name,
                  out_shape=(*[pltpu.SemaphoreType.DMA((len(g),)) for g in groups for _ in range(2)],
                             *[pltpu.HBM(b.shape, b.dtype) for b in bufs], _sds((8, LANES), F32)),
                  in_specs=[_HBM] * n, out_specs=(*[_SEM] * (2 * ng), *[_HBM] * n, pl.BlockSpec(memory_space=pltpu.VMEM)),
                  input_output_aliases={i: 2 * ng + i for i in range(n)},
                  compiler_params=pltpu.CompilerParams(has_side_effects=_EFFECT),
                  )(*[pltpu.with_memory_space_constraint(b, pltpu.HBM) for b in bufs])
    return [(outs[2 * g], outs[2 * g + 1]) for g in range(ng)], list(outs[2 * ng:2 * ng + n]), outs[-1]


def _split_wait(name, bufs, sems, after, remote):
    n = len(bufs)
    send_sems, recv_sems = sems

    def body(*refs):
        in_refs, ssem, rsem = refs[:n], refs[n], refs[n + 1]
        me, place, flip = _mesh_place()
        for k, (si, src, di, dst, mask) in enumerate(remote):
            peer = flip(mask)
            cp = pltpu.make_async_remote_copy(src_ref=src(in_refs[si], me), dst_ref=dst(in_refs[di], place(*peer)),
                                              send_sem=ssem.at[k], recv_sem=rsem.at[k], device_id=peer, device_id_type=MESH)
            cp.wait_send()
            cp.wait_recv()

    return list(_pcall(body, name=name, out_shape=tuple(pltpu.HBM(b.shape, b.dtype) for b in bufs),
                       in_specs=[_HBM] * n + [_SEM, _SEM, pl.BlockSpec(memory_space=pl.ANY)], out_specs=tuple([_HBM] * n),
                       input_output_aliases={i: i for i in range(n)},
                       compiler_params=pltpu.CompilerParams(has_side_effects=_EFFECT))(*bufs, send_sems, recv_sems, after))


CHIP_MASKS = ((0, 1, 0), (1, 0, 0), (1, 1, 0))
SIBLING = (0, 0, 1)


def _whole(ref, p):
    return ref


def _all_gather(name, shards, col_sharded):
    def dst_view(col):
        def view(ref, p):
            r, cdim = ref.shape[0] // (1 if col else N_CHIPS), ref.shape[1] // (N_CHIPS if col else 1)
            if col:
                return ref.at[:, pl.ds(pl.multiple_of(p["chip"] * cdim, LANES), cdim)]
            return ref.at[pl.ds(pl.multiple_of(p["chip"] * r, 8), r), :]
        return view

    out_sds = [_sds((s.shape[0], s.shape[1] * N_CHIPS) if col else (s.shape[0] * N_CHIPS, s.shape[1]), s.dtype)
               for s, col in zip(shards, col_sharded)]
    remote = [(a, _whole, a, dst_view(col), m) for a, col in enumerate(col_sharded) for m in CHIP_MASKS]
    local = [(a, _whole, a, dst_view(col)) for a, col in enumerate(col_sharded)]
    return _exchange(name, shards, out_sds, remote, local)


class _Place:
    def __getitem__(self, k):
        return lax.axis_index("c") if k == 0 else 2 * lax.axis_index("x") + lax.axis_index("y")


def _placed_call(body, name, grid, in_specs, out_specs, out_shape, sem, ins):
    def wrap(spec):
        return pl.BlockSpec(spec.block_shape, lambda *idx: spec.index_map(*idx, _Place()))

    outs = [wrap(s) for s in out_specs] if isinstance(out_specs, (list, tuple)) else wrap(out_specs)
    return _pcall(body, name=name, grid=grid, in_specs=[wrap(s) for s in in_specs], out_specs=outs, out_shape=out_shape,
                  compiler_params=_params(sem))(*ins)


def _rows_within(n, width, limit=512 * 1024):
    return _pick(n, tuple(t for t in (1024, 512, 256, 128, 64, 32, 16) if t * width <= limit) or (16,))


def _region_view(col):
    def view(ref, p):
        if col:
            cdim = ref.shape[1] // N_CHIPS
            return ref.at[:, pl.ds(pl.multiple_of(p["chip"] * cdim, LANES), cdim)]
        r = ref.shape[0] // N_CHIPS
        return ref.at[pl.ds(pl.multiple_of(p["chip"] * r, 16), r), :]
    return view


def _ag_place(name, w, layer, col, dtype):
    _, r, cdim = w.shape
    tr = _rows_within(r, cdim)
    nb = r // tr

    def body(w_ref, o_ref):
        o_ref[...] = w_ref[...].astype(dtype)

    if col:
        out_shape, out_spec = (r, N_CHIPS * cdim), pl.BlockSpec((tr, cdim), lambda i, pr: (i, pr[1]))
    else:
        out_shape, out_spec = (N_CHIPS * r, cdim), pl.BlockSpec((tr, cdim), lambda i, pr: (pr[1] * nb + i, 0))
    return _placed_call(body, name, (nb,), [pl.BlockSpec((None, tr, cdim), lambda i, pr: (layer, i, 0))], out_spec,
                        _sds(out_shape, dtype), ("parallel",), [w])


def _ag_copies(a, col):
    return [(a, _region_view(col), a, _region_view(col), m) for m in CHIP_MASKS]


def _rs_add2(name, g4, a4, out_dtype):
    J, _, h, C = g4.shape
    tr = _rows_within(h, C)

    def body(g_ref, a_ref, o_ref):
        o_ref[...] = (g_ref[...].astype(F32) + a_ref[...].astype(F32)).astype(o_ref.dtype)

    return _placed_call(body, name, (J, h // tr),
                        [pl.BlockSpec((None, None, tr, C), lambda j, i, pr: (j, pr[0], i, 0)),
                         pl.BlockSpec((None, None, tr, C), lambda j, i, pr: (j, 0, i, 0))],
                        pl.BlockSpec((None, tr, C), lambda j, i, pr: (j, i, 0)), _sds((J, h, C), out_dtype),
                        ("parallel", "parallel"), [g4, a4])


def _rs_add4(name, p3, landed, col):
    _, h, w = landed.shape
    tr = _rows_within(h, w)

    def body(p_ref, a_ref, b_ref, c_ref, o_ref):
        o_ref[...] = ((p_ref[...].astype(F32) + a_ref[...].astype(F32)) + b_ref[...].astype(F32)) + c_ref[...].astype(F32)

    own = (pl.BlockSpec((None, tr, w), lambda i, pr: (0, i, pr[1])) if col else pl.BlockSpec((None, tr, w), lambda i, pr: (pr[1], i, 0)))
    slot = lambda k: pl.BlockSpec((None, tr, w), lambda i, pr: (k, i, 0))
    return _placed_call(body, name, (h // tr,), [own, slot(0), slot(1), slot(2)], pl.BlockSpec((tr, w), lambda i, pr: (i, 0)),
                        _sds((h, w), F32), ("parallel",), [p3, landed, landed, landed])


def _rs_start(tag, grads, col_sharded):
    n = len(grads)
    g4 = [g.reshape((1, 2, g.shape[0] // 2, g.shape[1]) if col else (N_CHIPS, 2, g.shape[0] // (2 * N_CHIPS), g.shape[1]))
          for g, col in zip(grads, col_sharded)]
    other_half = lambda ref, p: ref.at[:, pl.ds(1 - p["c"], 1)]
    landing = [lax.empty((g.shape[0], 1) + g.shape[2:], g.dtype) for g in g4]
    copies = [(a, other_half, n + a, _whole, SIBLING) for a in range(n)]
    (sems,), bufs, token = _split_start("rs_sibling_start_" + tag, g4 + landing, [copies])
    return dict(tag=tag, stage=0, sems=sems, bufs=bufs, copies=copies, token=token, col_sharded=list(col_sharded))


def _rs_advance(st, after):
    col_sharded, tag = st['col_sharded'], st['tag']
    n = len(col_sharded)
    stage = st['stage']
    st['stage'] = stage + 1
    if stage == 0:
        bufs = _split_wait("rs_sibling_wait_" + tag, st['bufs'], st['sems'], after, st['copies'])
        st.update(_rs_chips_start(tag, [_rs_add2("rs_add2_w", bufs[a], bufs[n + a], BF16) for a in range(n)], col_sharded))
    elif stage == 1:
        bufs = _split_wait("rs_chips_wait_" + tag, st['bufs'], st['sems'], after, st['copies'])
        mine = [_rs_add4("rs_add4_w", bufs[a], bufs[n + a], col_sharded[a]) for a in range(n)]
        copies = [(a, _whole, n + a, _whole, SIBLING) for a in range(n)]
        (sems,), bufs, token = _split_start("rs_halves_start_" + tag, mine + [lax.empty(m.shape, F32) for m in mine], [copies])
        st.update(sems=sems, bufs=bufs, copies=copies, token=token)
    else:
        bufs = _split_wait("rs_halves_wait_" + tag, st['bufs'], st['sems'], after, st['copies'])
        st['result'] = (bufs[:n], bufs[n:])


def _rs_chips_start(tag, chip_sum, col_sharded):
    n = len(chip_sum)

    def send_view(col, mask):
        def view(ref, p):
            t = 2 * ((1 - p["x"]) if mask[0] else p["x"]) + ((1 - p["y"]) if mask[1] else p["y"])
            if col:
                sc = ref.shape[2] // N_CHIPS
                return ref.at[0, :, pl.ds(pl.multiple_of(t * sc, LANES), sc)]
            return ref.at[t]
        return view
    slot = lambda k: (lambda ref, p: ref.at[k])
    piece = [(s.shape[1], s.shape[2] // N_CHIPS if col else s.shape[2]) for s, col in zip(chip_sum, col_sharded)]
    landing = [lax.empty((len(CHIP_MASKS),) + s, BF16) for s in piece]
    copies = [(a, send_view(col_sharded[a], m), n + a, slot(k), m) for a in range(n) for k, m in enumerate(CHIP_MASKS)]
    (sems,), bufs, token = _split_start("rs_chips_start_" + tag, chip_sum + landing, [copies])
    return dict(sems=sems, bufs=bufs, copies=copies, token=token)


def _adamw_big(name, mine, other, w, m, v):
    depth, R, C = w.shape
    h = R // 2
    tr = _pick(h, tuple(t for t in (512, 256, 128, 64, 32, 16, 8) if t * C <= 256 * 1024) or (8,))
    nb = h // tr

    def g_spec(kk, hh):
        def imap(l, s, i, pr):
            before = (l < kk) | ((l == kk) & (s < hh))
            return (jnp.where((l == kk) & (s == hh), i, jnp.where(before, 0, nb - 1)), 0)
        return pl.BlockSpec((tr, C), imap)

    st_spec = pl.BlockSpec((None, tr, C), lambda l, s, i, pr: (l, jnp.where(s == 0, pr[0], 1 - pr[0]) * nb + i, 0))

    def body(*refs):
        g_refs = refs[:2 * depth]
        w_ref, m_ref, v_ref, go_ref, d_ref, mo_ref, vo_ref = refs[2 * depth:]
        l, s = pl.program_id(0), pl.program_id(1)
        for kk in range(depth):
            for hh in range(2):
                @pl.when((l == kk) & (s == hh))
                def _(kk=kk, hh=hh):
                    g = g_refs[2 * kk + hh][...]
                    d, mn, vn = _adam_math(w_ref[...], g, m_ref[...], v_ref[...])
                    go_ref[...] = g
                    d_ref[...] = d
                    mo_ref[...] = mn
                    vo_ref[...] = vn

    gs, g_specs = [], []
    for kk in range(depth):
        gs += [mine[kk], other[kk]]
        g_specs += [g_spec(kk, 0), g_spec(kk, 1)]
    return _placed_call(body, name, (depth, 2, nb), g_specs + [st_spec] * 3, [st_spec] * 4, [_sds(w.shape, F32)] * 4,
                        ("arbitrary", "arbitrary", "arbitrary"), gs + [w, m, v])


def _piece_view(col, j, other):
    def view(ref, p):
        R, C = ref.shape
        cc = (1 - p["c"]) if other else p["c"]
        if col:
            hr, sc = R // 2, C // N_CHIPS
            return ref.at[pl.ds(pl.multiple_of(cc * hr, 16), hr), pl.ds(j * sc, sc)]
        hr = R // (2 * N_CHIPS)
        return ref.at[pl.ds(pl.multiple_of((2 * j + cc) * hr, 8), hr), :]
    return view


def _piece_shape(shape, col):
    R, C = shape
    return (R // 2, C // N_CHIPS) if col else (R // (2 * N_CHIPS), C)


def _reduce_scatter(tag, grads, col_sharded, wire_dtype):
    n = len(grads)
    shapes = [_piece_shape(g.shape, col) for g, col in zip(grads, col_sharded)]

    slot = lambda j: (lambda ref, p: ref.at[j])
    remote = [(a, _piece_view(col_sharded[a], j, True), a, slot(j), SIBLING) for a in range(n) for j in range(N_CHIPS)]
    local = [(a, _piece_view(col_sharded[a], j, False), n + a, slot(j)) for a in range(n) for j in range(N_CHIPS)]
    got = _exchange("rs_sibling_" + tag, grads, [_sds((N_CHIPS,) + s, g.dtype) for s, g in zip(shapes, grads)] * 2, remote, local)
    theirs, mine = got[:n], got[n:]
    chip_sum = [_ew("rs_add2_" + tag, lambda a, b: (a.astype(F32) + b.astype(F32),),
                    [m.reshape(-1, m.shape[-1]), t.reshape(-1, t.shape[-1])], [wire_dtype])[0].reshape(m.shape)
                for m, t in zip(mine, theirs)]

    def send_view(mask):
        return lambda ref, p: ref.at[2 * ((1 - p["x"]) if mask[0] else p["x"]) + ((1 - p["y"]) if mask[1] else p["y"])]
    remote = [(a, send_view(m), a, slot(k), m) for a in range(n) for k, m in enumerate(CHIP_MASKS)]
    local = [(a, lambda ref, p: ref.at[p["chip"]], n + a, _whole) for a in range(n)]
    got = _exchange("rs_chips_" + tag, chip_sum,
                    [_sds((len(CHIP_MASKS),) + s, wire_dtype) for s in shapes] + [_sds(s, wire_dtype) for s in shapes], remote, local)
    landed, own = got[:n], got[n:]
    half = [_ew("rs_add4_" + tag, lambda o, a, b, c: (((o.astype(F32) + a.astype(F32)) + b.astype(F32)) + c.astype(F32),),
                [o, l[0], l[1], l[2]], [F32])[0] for o, l in zip(own, landed)]

    def half_rows(ref, p):
        hr = ref.shape[0] // 2
        return ref.at[pl.ds(pl.multiple_of(p["c"] * hr, 8), hr), :]
    remote = [(a, _whole, a, half_rows, SIBLING) for a in range(n)]
    local = [(a, _whole, a, half_rows) for a in range(n)]
    return _exchange("rs_halves_" + tag, half, [_sds((2 * s[0], s[1]), F32) for s in shapes], remote, local)


def _ssm_prepare(W):
    lr, li, ls = W['ssm_lambda_re'], W['ssm_lambda_im'], W['ssm_log_step']
    depth, G = ls.shape
    GG = depth * G
    flat = lambda a: a.reshape(-1, LANES)
    bc = lambda a: flat(jnp.broadcast_to(a, (depth, G, STATE, GROUP_CH)))
    flat3 = (bc(lr[..., None]), bc(li[..., None]), bc(ls[:, :, None, None]))
    bbr, bbi = _ssm_bbar(*flat3, flat(W['ssm_b_re']), flat(W['ssm_b_im']))
    bbr, bbi = bbr.reshape(GG, STATE, GROUP_CH), bbi.reshape(GG, STATE, GROUP_CH)
    row = lambda a: a.reshape(1, GG * STATE)
    tf, tr = _ssm_tables(row(lr), row(li), row(jnp.broadcast_to(ls[..., None], (depth, G, STATE))))
    cr = W['ssm_c_re'].reshape(GG, GROUP_CH, STATE).transpose(0, 2, 1)
    ci = -W['ssm_c_im'].reshape(GG, GROUP_CH, STATE).transpose(0, 2, 1)
    stacked = dict(wb=(_tile_w(bbr), _tile_w(bbi)), wbT=(_slab_w(bbr), _slab_w(bbi)),
                   wc=(_slab_w(cr), _slab_w(ci)), wcT=(_tile_w(cr), _tile_w(ci)))
    return flat3, [dict(stacked, tf=tf, tr=tr, layer=l, depth=depth) for l in range(depth)]


def _ssm_param_grads(W, flat3, raw):
    depth, G = W['ssm_log_step'].shape
    GG = depth * G
    cat = lambda k: jnp.concatenate([r[k] for r in raw], axis=0)
    flat = lambda a: a.reshape(-1, LANES)
    out = {}
    out['ssm_c_re'] = _slab_w_grad(cat(2)).transpose(0, 2, 1).reshape(W['ssm_c_re'].shape)
    out['ssm_c_im'] = -_slab_w_grad(cat(3)).transpose(0, 2, 1).reshape(W['ssm_c_im'].shape)
    dbr, dbi, qr, qi = _ssm_param_bwd_flat(*flat3, flat(W['ssm_b_re']), flat(W['ssm_b_im']),
                                           flat(_tile_w_grad(cat(0))), flat(_tile_w_grad(cat(1))))
    out['ssm_b_re'], out['ssm_b_im'] = dbr.reshape(W['ssm_b_re'].shape), dbi.reshape(W['ssm_b_im'].shape)
    pick = lambda q: q[:, ::GROUP_CH].reshape(GG, STATE)
    sums = lambda k: jnp.concatenate([r[k].reshape(8, G, STATE) for r in raw], axis=1)
    dlr, dli, dls = _ssm_param_bwd(W['ssm_lambda_re'].reshape(GG, STATE), W['ssm_lambda_im'].reshape(GG, STATE),
                                   W['ssm_log_step'].reshape(GG, 1), pick(qr), pick(qi), sums(4), sums(5))
    out['ssm_lambda_re'], out['ssm_lambda_im'] = dlr.reshape(depth, G, STATE), dli.reshape(depth, G, STATE)
    out['ssm_log_step'] = dls.reshape(depth, G)
    return out


def _layer_fwd(x, p, weight, dims):
    attn_w, kv_w, u_off = dims['attn_w'], dims['kv_w'], dims['u_off']
    s = p['s5']
    h = _rms_fwd("norm_mix", [x], [p['norm_mix_g']], BF16)
    w = {'w_in': weight('w_in', h)}
    proj, = _mm("mm_in", h, w['w_in'], 'nn', [F32])
    attn = _attn_fwd(proj, p['q_norm_g'], p['k_norm_g'], p['attn_sinks'], attn_w, kv_w)
    xr, xi, y, gl = _ssm_fwd(proj, u_off, s, p['ssm_d'])
    w['w_glu'] = weight('w_glu', gl)
    ssm, z = _mm("mm_glu", gl, w['w_glu'], 'nn', [F32, F32], extras=[('row', p['b_glu']), ('tile', gl)],
                 epi=lambda acc, b, g: ((lambda zz: (g * jax.nn.sigmoid(zz), zz))(acc + b)))
    mix = _rms_fwd("norm_heads", [attn, ssm], [p['attn_out_g'], p['ssm_out_g']], BF16)
    w['w_out'] = weight('w_out', mix)
    x_mid, = _mm("mm_out", mix, w['w_out'], 'nn', [F32], extras=[('tile', x)], epi=lambda acc, r: (acc + r,))
    h2 = _rms_fwd("norm_mlp", [x_mid], [p['norm_mlp_g']], BF16)
    w['w_up'] = weight('w_up', h2)
    a, r = _mm("mm_up", h2, w['w_up'], 'nn', [F32, BF16],
               epi=lambda acc: (acc, jnp.square(jnp.maximum(acc, 0.0))))
    w['w_down'] = weight('w_down', r)
    x_out, = _mm("mm_down", r, w['w_down'], 'nn', [F32], extras=[('tile', x_mid)], epi=lambda acc, rr: (acc + rr,))
    saved = dict(x=x, h=h, proj=proj, attn=attn, xr=xr, xi=xi, y=y, gl=gl, z=z, ssm=ssm, mix=mix, x_mid=x_mid, h2=h2, a=a, r=r, w=w)
    return x_out, saved


def _layer_bwd(dx, dx16, sv, p, dims, reduce_grads, tick, token_in):
    attn_w, kv_w, u_off = dims['attn_w'], dims['kv_w'], dims['u_off']
    s, w = p['s5'], sv['w']
    gb, gs = {}, {}
    da, = _mm("mm_down_dx", dx16, w['w_down'], 'nt', [BF16], extras=[('tile', sv['a'])],
              epi=lambda acc, a: (acc * (2.0 * jnp.maximum(a, 0.0)),))
    gb['w_down'], = _mm("mm_down_dw", sv['r'], dx16, 'tn', [BF16])
    dh2, = _mm("mm_up_dx", da, w['w_up'], 'nt', [F32])
    gb['w_up'], = _mm("mm_up_dw", sv['h2'], da, 'tn', [BF16])
    token = reduce_grads(('w_up', 'w_down'), [gb['w_up'], gb['w_down']]) + token_in
    (dx_mid,), (gs['norm_mlp_g'],), dx_mid16 = _rms_bwd("norm_mlp_bwd", [sv['x_mid']], [p['norm_mlp_g'] + token], dh2, resid=dx)
    dmix, = _mm("mm_out_dx", dx_mid16, w['w_out'], 'nt', [F32])
    gb['w_out'], = _mm("mm_out_dw", sv['mix'], dx_mid16, 'tn', [BF16])
    token = tick(dmix)
    (dattn, dssm), (gs['attn_out_g'], gs['ssm_out_g']) = _rms_bwd(
        "norm_heads_bwd", [sv['attn'], sv['ssm']], [p['attn_out_g'] + token, p['ssm_out_g']], dmix)
    dz, gs['b_glu'] = _glu_dz(dssm, sv['gl'], sv['z'])
    dy, = _mm("mm_glu_dx", dz, w['w_glu'], 'nt', [F32], extras=[('tile', dssm), ('tile', sv['z']), ('tile', sv['y'])],
              epi=lambda acc, ds, z, y: ((acc + ds * jax.nn.sigmoid(z)) * _gelu_grad(y),))
    gb['w_glu'], = _mm("mm_glu_dw", sv['gl'], dz, 'tn', [BF16])
    token = tick(dy)
    dproj, dkn, dv, gs['q_norm_g'], gs['attn_sinks'] = _attn_bwd(sv['proj'], sv['attn'], dattn, p['q_norm_g'] + token, p['k_norm_g'],
                                                                  p['attn_sinks'], attn_w, kv_w)
    dproj, gs['k_norm_g'] = _knorm_bwd(sv['proj'], dkn, dv, p['k_norm_g'], dproj, attn_w, kv_w)
    dproj, gs['ssm_d'], *gs['s5_raw'] = _ssm_bwd(dy, sv['proj'], u_off, sv['xr'], sv['xi'], s, p['ssm_d'], dproj)
    dh, = _mm("mm_in_dx", dproj, w['w_in'], 'nt', [F32])
    gb['w_in'], = _mm("mm_in_dw", sv['h'], dproj, 'tn', [BF16])
    token = reduce_grads(('w_in', 'w_glu', 'w_out'), [gb['w_in'], gb['w_glu'], gb['w_out']])
    (dx_in,), (gs['norm_mix_g'],), dx_in16 = _rms_bwd("norm_mix_bwd", [sv['x']], [p['norm_mix_g'] + token], dh, resid=dx_mid)
    return dx_in, dx_in16, gs


PACK_COLS = 1024


def _pack(arrs, rows):
    flat = jnp.concatenate([a.reshape(-1).astype(F32) for a in arrs])
    return jnp.pad(flat, (0, rows * PACK_COLS - flat.shape[0])).reshape(rows, PACK_COLS)


def _unpack(packed, shapes):
    flat = packed.reshape(-1)
    out, off = [], 0
    for s in shapes:
        n = int(np.prod(s))
        out.append(flat[off:off + n].reshape(s))
        off += n
    return out


def _pack_rows(shapes, multiple):
    n = sum(int(np.prod(s)) for s in shapes)
    rows = -(-n // PACK_COLS)
    return -(-rows // multiple) * multiple


def kernel(x, meta_tokens, norm_mix_g, w_in, q_norm_g, k_norm_g, attn_sinks, ssm_lambda_re, ssm_lambda_im, ssm_log_step, ssm_b_re, ssm_b_im, ssm_c_re, ssm_c_im, ssm_d, w_glu, b_glu, attn_out_g, ssm_out_g, w_out, norm_mlp_g, w_up, w_down, loss_target, m_meta_tokens, m_norm_mix_g, m_w_in, m_q_norm_g, m_k_norm_g, m_attn_sinks, m_ssm_lambda_re, m_ssm_lambda_im, m_ssm_log_step, m_ssm_b_re, m_ssm_b_im, m_ssm_c_re, m_ssm_c_im, m_ssm_d, m_w_glu, m_b_glu, m_attn_out_g, m_ssm_out_g, m_w_out, m_norm_mlp_g, m_w_up, m_w_down, v_meta_tokens, v_norm_mix_g, v_w_in, v_q_norm_g, v_k_norm_g, v_attn_sinks, v_ssm_lambda_re, v_ssm_lambda_im, v_ssm_log_step, v_ssm_b_re, v_ssm_b_im, v_ssm_c_re, v_ssm_c_im, v_ssm_d, v_w_glu, v_b_glu, v_attn_out_g, v_ssm_out_g, v_w_out, v_norm_mlp_g, v_w_up, v_w_down):
    args = (meta_tokens, norm_mix_g, w_in, q_norm_g, k_norm_g, attn_sinks, ssm_lambda_re, ssm_lambda_im, ssm_log_step, ssm_b_re, ssm_b_im, ssm_c_re, ssm_c_im, ssm_d, w_glu, b_glu, attn_out_g, ssm_out_g, w_out, norm_mlp_g, w_up, w_down)
    ms = (m_meta_tokens, m_norm_mix_g, m_w_in, m_q_norm_g, m_k_norm_g, m_attn_sinks, m_ssm_lambda_re, m_ssm_lambda_im, m_ssm_log_step, m_ssm_b_re, m_ssm_b_im, m_ssm_c_re, m_ssm_c_im, m_ssm_d, m_w_glu, m_b_glu, m_attn_out_g, m_ssm_out_g, m_w_out, m_norm_mlp_g, m_w_up, m_w_down)
    vs = (v_meta_tokens, v_norm_mix_g, v_w_in, v_q_norm_g, v_k_norm_g, v_attn_sinks, v_ssm_lambda_re, v_ssm_lambda_im, v_ssm_log_step, v_ssm_b_re, v_ssm_b_im, v_ssm_c_re, v_ssm_c_im, v_ssm_d, v_w_glu, v_b_glu, v_attn_out_g, v_ssm_out_g, v_w_out, v_norm_mlp_g, v_w_up, v_w_down)
    W = dict(zip(WEIGHTS, args))
    M = dict(zip(WEIGHTS, ms))
    V = dict(zip(WEIGHTS, vs))
    depth = norm_mix_g.shape[0]
    seq, D = x.shape[1], x.shape[2]
    attn_w = D // 2
    kv_w = attn_w // KV_GROUP
    dims = dict(attn_w=attn_w, kv_w=kv_w, u_off=(attn_w + 2 * kv_w) // LANES)
    small_names = [n for n in WEIGHTS if n not in BIG and n != 'meta_tokens']
    chip = 2 * lax.axis_index("x") + lax.axis_index("y")

    gathers, started = [], jnp.zeros((), F32)
    for l in range(depth):
        placed = [_ag_place("ag_place_" + n, W[n], l, COL_SHARDED[n], BF16) for n in BIG]
        groups = [_ag_copies(a, COL_SHARDED[n]) for a, n in enumerate(BIG)]
        if l == 0:
            placed = [_ag_place("ag_place_meta", meta_tokens[None], 0, True, F32)] + placed
            groups = [_ag_copies(0, True)] + [_ag_copies(a + 1, COL_SHARDED[n]) for a, n in enumerate(BIG)]
        sems, bufs, token = _split_start("ag_start_%d" % l, placed, groups)
        gathers.append(dict(zip((['meta_tokens'] if l == 0 else []) + BIG, zip(sems, bufs))))
        started = started + token[0, 0]

    def gathered(l, n, after):
        sems, buf = gathers[l][n]
        return _split_wait("ag_wait_%d_%s" % (l, n), [buf], sems, after, _ag_copies(0, n == 'meta_tokens' or COL_SHARDED[n]))[0]

    h_res = jnp.concatenate([jnp.zeros((PAD, D), F32), gathered(0, 'meta_tokens', started.reshape(1, 1)), x[0]], axis=0)
    s5_flat3, s5_layers = _ssm_prepare(W)
    layer_p = []
    for l in range(depth):
        p = {n: W[n][l][None, :] for n in ('norm_mix_g', 'q_norm_g', 'k_norm_g', 'attn_sinks', 'ssm_d', 'b_glu', 'attn_out_g',
                                             'ssm_out_g', 'norm_mlp_g')}
        p['s5'] = s5_layers[l]
        layer_p.append(p)
    saved = []
    for l in range(depth):
        h_res, sv = _layer_fwd(h_res, layer_p[l], functools.partial(gathered, l), dims)
        saved.append(sv)
    loss_local, dx, dx16 = _loss(h_res, loss_target[0])
    loss = lax.psum(loss_local, ("x", "y", "c"))

    small_grads = [None] * depth
    shard_grads = {}
    pending = []

    def reduce_grads(l, names, grads):
        st = _rs_start("%d_%s" % (l, names[0]), list(grads), [COL_SHARDED[n] for n in names])
        st.update(layer=l, names=names, fresh=True)
        pending.append(st)
        return st['token'][0, 0]

    def tick(after):
        token = jnp.zeros((), F32)
        for st in list(pending):
            if st['fresh']:
                st['fresh'] = False
                continue
            _rs_advance(st, after)
            if 'result' in st:
                pending.remove(st)
                for a, n in enumerate(st['names']):
                    shard_grads[(st['layer'], n)] = (st['result'][0][a], st['result'][1][a])
            else:
                token = token + st['token'][0, 0]
        return token

    token = jnp.zeros((), F32)
    for l in reversed(range(depth)):
        dx, dx16, gs = _layer_bwd(dx, dx16, saved[l], layer_p[l], dims, functools.partial(reduce_grads, l), tick, token)
        saved[l] = None
        small_grads[l] = gs
        token = tick(dx)
    grad_x = dx[BLOCK:].reshape(x.shape)

    g_small = _ssm_param_grads(W, s5_flat3, [small_grads[l]['s5_raw'] for l in range(depth)])
    for n in small_names:
        if n not in g_small:
            g_small[n] = jnp.stack([small_grads[l][n].reshape(W[n].shape[1:]) for l in range(depth)])
    g_shapes = [(N_META, D)] + [W[n].shape for n in small_names]
    rows = _pack_rows(g_shapes, 8 * 2 * N_CHIPS)
    packed = _pack([dx[PAD:BLOCK]] + [g_small[n] for n in small_names], rows)
    out = {}

    def adamw_big(n):
        return _adamw_big("adamw_" + n, [shard_grads[(l, n)][0] for l in range(depth)],
                          [shard_grads[(l, n)][1] for l in range(depth)], W[n], M[n], V[n])

    early = ('w_up', 'w_down')
    tick(packed)
    while any((l, n) not in shard_grads for l in range(depth) for n in early):
        tick(packed)
    for n in early:
        out[n] = adamw_big(n)
    tick(out[early[-1]][1])
    red, = _reduce_scatter("small", [packed], [False], F32)
    red_full, = _all_gather("ag_small", [red], [False])
    while pending:
        tick(red_full)
    g_list = _unpack(red_full, g_shapes)
    g_meta = lax.dynamic_slice_in_dim(g_list[0], chip * meta_tokens.shape[1], meta_tokens.shape[1], axis=1)
    G = dict(zip(small_names, g_list[1:]))
    G['meta_tokens'] = g_meta

    for n in BIG:
        if n not in out:
            out[n] = adamw_big(n)
    for n in ['meta_tokens'] + small_names:
        rows2d = lambda a: a.reshape(-1, a.shape[-1])
        upd = _ew("adamw_" + n, _adam_math, [rows2d(W[n]), rows2d(G[n]), rows2d(M[n]), rows2d(V[n])], [F32] * 3)
        out[n] = (G[n], *[u.reshape(W[n].shape) for u in upd])
    return (loss, grad_x, *[out[n][0] for n in WEIGHTS], *[out[n][1] for n in WEIGHTS],
            *[out[n][2] for n in WEIGHTS], *[out[n][3] for n in WEIGHTS])
```

```python
import functools
import math

import numpy as np
import jax
import jax.numpy as jnp
from jax import lax
from jax.experimental import pallas as pl
from jax.experimental.pallas import tpu as pltpu

F32 = jnp.float32
BF16 = jnp.bfloat16
MESH = pl.DeviceIdType.MESH

N_META = 16
HEAD_DIM = 64
KV_GROUP = 4
GROUP_CH = 16
STATE = 64
BLOCK = 128
PAD = BLOCK - N_META
NORM_EPS = 1e-6
NEG_INF = -1e30
LANES = 128
V7X_VMEM_LIMIT_BYTES = 56 * 1024 * 1024
MM_VMEM_BUDGET_BYTES = 44 * 1024 * 1024

ADAM_LR, ADAM_B1, ADAM_B2, ADAM_EPS, ADAM_WD, ADAM_STEP = 0.001, 0.9, 0.999, 1e-08, 0.01, 10

WEIGHTS = ['meta_tokens', 'norm_mix_g', 'w_in', 'q_norm_g', 'k_norm_g', 'attn_sinks', 'ssm_lambda_re',
           'ssm_lambda_im', 'ssm_log_step', 'ssm_b_re', 'ssm_b_im', 'ssm_c_re', 'ssm_c_im', 'ssm_d', 'w_glu',
           'b_glu', 'attn_out_g', 'ssm_out_g', 'w_out', 'norm_mlp_g', 'w_up', 'w_down']
BIG = ['w_in', 'w_glu', 'w_out', 'w_up', 'w_down']
COL_SHARDED = {'w_in': True, 'w_glu': False, 'w_out': False, 'w_up': True, 'w_down': False}
N_CHIPS = 4


def _pick(n, cands):
    for c in cands:
        if c <= n and n % c == 0:
            return c
    return n


def _params(sem):
    return pltpu.CompilerParams(dimension_semantics=sem, vmem_limit_bytes=V7X_VMEM_LIMIT_BYTES)


def _pcall(body, **kw):
    return pl.pallas_call(body, **kw)


def _sds(shape, dtype):
    return jax.ShapeDtypeStruct(shape, dtype)


_DIMS = {'nn': ((1,), (0,)), 'nt': ((1,), (1,)), 'tn': ((0,), (0,))}


def _mm(name, a, b, mode, out_dtypes, extras=(), epi=None):
    if mode == 'nn':
        (M, K), (_, N) = a.shape, b.shape
    elif mode == 'nt':
        (M, K), (N, _) = a.shape, b.shape
    else:
        (K, M), (_, N) = a.shape, b.shape
    tile_bytes = 4 * len([k for k, _ in extras if k == 'tile']) + sum(jnp.dtype(d).itemsize for d in out_dtypes)

    def fits(tm, tn, tk):
        need = 2 * tm * tk * a.dtype.itemsize + 2 * tk * tn * b.dtype.itemsize + 4 * tm * tn + 2 * tm * tn * tile_bytes
        return need <= MM_VMEM_BUDGET_BYTES

    if mode == 'tn':
        tm, tk_cands = _pick(M, (1024, 512, 256, 128)), (1408, 704, 384, 128)
    else:
        tm, tk_cands = _pick(M, (1408, 704, 384, 128)), (2048, 1024, 512, 256, 128)
    tk_cands = [t for t in tk_cands if t <= K and K % t == 0] or [K]
    tn_cands = [t for t in (2048, 1280, 1024, 640, 512, 256, 128) if t <= N and N % t == 0] or [N]
    if mode != 'tn' and tk_cands[0] == K and a.dtype == BF16:
        tk_cands = tk_cands[:1]
    tn, tk = next(((tn_, tk_) for tn_ in tn_cands for tk_ in tk_cands if fits(tm, tn_, tk_)), (tn_cands[-1], tk_cands[-1]))
    nk = K // tk
    a_spec = pl.BlockSpec((tk, tm), lambda i, j, k: (k, i)) if mode == 'tn' else pl.BlockSpec((tm, tk), lambda i, j, k: (i, k))
    b_spec = pl.BlockSpec((tn, tk), lambda i, j, k: (j, k)) if mode == 'nt' else pl.BlockSpec((tk, tn), lambda i, j, k: (k, j))
    ex_specs = [pl.BlockSpec((tm, tn), lambda i, j, k: (i, j)) if kind == 'tile' else pl.BlockSpec((1, tn), lambda i, j, k: (0, j))
                for kind, _ in extras]
    ne, no = len(extras), len(out_dtypes)
    dims = (_DIMS[mode], ((), ()))

    def body(a_ref, b_ref, *rest):
        ex, outs, acc = rest[:ne], rest[ne:ne + no], rest[ne + no]
        k = pl.program_id(2)

        @pl.when(k == 0)
        def _():
            acc[...] = jnp.zeros_like(acc)

        acc[...] += lax.dot_general(a_ref[...].astype(BF16), b_ref[...].astype(BF16), dims, preferred_element_type=F32)

        @pl.when(k == nk - 1)
        def _():
            r = acc[...]
            res = epi(r, *[e[...] for e in ex]) if epi is not None else (r,)
            for o, v in zip(outs, res):
                o[...] = v.astype(o.dtype)

    outs = _pcall(
        body, name=name, grid=(M // tm, N // tn, nk),
        in_specs=[a_spec, b_spec] + ex_specs,
        out_specs=[pl.BlockSpec((tm, tn), lambda i, j, k: (i, j)) for _ in out_dtypes],
        out_shape=[_sds((M, N), d) for d in out_dtypes],
        scratch_shapes=[pltpu.VMEM((tm, tn), F32)],
        compiler_params=_params(("parallel", "parallel", "arbitrary")),
    )(a, b, *[e for _, e in extras])
    return outs


def _ew(name, fn, ins, out_dtypes):
    R, C = ins[0].shape
    tr = _pick(R, tuple(t for t in (1024, 512, 256, 128, 64, 32, 16, 8) if t * C <= 512 * 1024) or (8,))
    n_in = len(ins)

    def body(*refs):
        res = fn(*[r[...] for r in refs[:n_in]])
        for o, v in zip(refs[n_in:], res):
            o[...] = v.astype(o.dtype)

    spec = pl.BlockSpec((tr, C), lambda i: (i, 0))
    return _pcall(body, name=name, grid=(R // tr,), in_specs=[spec] * n_in, out_specs=[spec] * len(out_dtypes),
                  out_shape=[_sds((R, C), d) for d in out_dtypes], compiler_params=_params(("parallel",)))(*ins)


def _adam_math(w, g, m, v):
    m = ADAM_B1 * m + (1.0 - ADAM_B1) * g
    v = ADAM_B2 * v + (1.0 - ADAM_B2) * (g * g)
    m_hat = m / (1.0 - ADAM_B1 ** ADAM_STEP)
    v_hat = v / (1.0 - ADAM_B2 ** ADAM_STEP)
    delta = -ADAM_LR * (m_hat / (jnp.sqrt(v_hat) + ADAM_EPS) + ADAM_WD * w)
    return delta, m, v


def _rms_fwd(name, xs, gs, out_dtype):
    L = xs[0].shape[0]
    ws = [x.shape[1] for x in xs]
    n = len(xs)
    tr = _pick(L, (384, 256, 128))

    def body(*refs):
        o = refs[2 * n]
        off = 0
        for i in range(n):
            x = refs[i][...]
            r = lax.rsqrt(jnp.mean(x * x, axis=-1, keepdims=True) + NORM_EPS)
            o[:, off:off + ws[i]] = ((x * r) * refs[n + i][...]).astype(o.dtype)
            off += ws[i]

    return _pcall(body, name=name, grid=(L // tr,),
                  in_specs=[pl.BlockSpec((tr, w), lambda i: (i, 0)) for w in ws] + [pl.BlockSpec((1, w), lambda i: (0, 0)) for w in ws],
                  out_specs=pl.BlockSpec((tr, sum(ws)), lambda i: (i, 0)), out_shape=_sds((L, sum(ws)), out_dtype),
                  compiler_params=_params(("parallel",)))(*xs, *gs)


def _rms_bwd(name, xs, gs, dy, resid=None):
    L = xs[0].shape[0]
    ws = [x.shape[1] for x in xs]
    n = len(xs)
    tr = _pick(L, (384, 256, 128))
    has_res = resid is not None

    def body(*refs):
        x_refs, g_refs, dy_ref = refs[:n], refs[n:2 * n], refs[2 * n]
        p = 2 * n + 1
        res_ref = refs[p] if has_res else None
        p += 1 if has_res else 0
        dx_refs, dg_refs = refs[p:p + n], refs[p + n:p + 2 * n]
        dx16_ref = refs[p + 2 * n] if has_res else None
        first = pl.program_id(0) == 0
        off = 0
        for i in range(n):
            x = x_refs[i][...]
            d = dy_ref[:, off:off + ws[i]]
            r = lax.rsqrt(jnp.mean(x * x, axis=-1, keepdims=True) + NORM_EPS)
            xh = x * r
            dg = jnp.sum(d * xh, axis=0, keepdims=True)

            @pl.when(first)
            def _(i=i):
                dg_refs[i][...] = jnp.zeros_like(dg_refs[i])

            dg_refs[i][...] += dg
            dyg = d * g_refs[i][...]
            dx = r * (dyg - xh * jnp.mean(dyg * xh, axis=-1, keepdims=True))
            if has_res:
                dx = dx + res_ref[...]
                dx16_ref[...] = dx.astype(BF16)
            dx_refs[i][...] = dx
            off += ws[i]

    in_specs = ([pl.BlockSpec((tr, w), lambda i: (i, 0)) for w in ws] + [pl.BlockSpec((1, w), lambda i: (0, 0)) for w in ws]
                + [pl.BlockSpec((tr, sum(ws)), lambda i: (i, 0))])
    ins = list(xs) + list(gs) + [dy]
    if has_res:
        in_specs.append(pl.BlockSpec((tr, ws[0]), lambda i: (i, 0)))
        ins.append(resid)
    out_specs = [pl.BlockSpec((tr, w), lambda i: (i, 0)) for w in ws] + [pl.BlockSpec((1, w), lambda i: (0, 0)) for w in ws]
    out_shape = [_sds((L, w), F32) for w in ws] + [_sds((1, w), F32) for w in ws]
    if has_res:
        out_specs.append(pl.BlockSpec((tr, ws[0]), lambda i: (i, 0)))
        out_shape.append(_sds((L, ws[0]), BF16))
    outs = _pcall(body, name=name, grid=(L // tr,), in_specs=in_specs, out_specs=out_specs, out_shape=out_shape,
                  compiler_params=_params(("arbitrary",)))(*ins)
    return (outs[:n], outs[n:2 * n], outs[2 * n]) if has_res else (outs[:n], outs[n:])


def _loss(xl, target):
    Lp, D = xl.shape

    def body(x_ref, t_ref, dy_ref, dy16_ref, loss_ref):
        n = pl.program_id(0)

        @pl.when(n == 0)
        def _():
            loss_ref[...] = jnp.zeros_like(loss_ref)
            dy_ref[...] = jnp.zeros_like(dy_ref)
            dy16_ref[...] = jnp.zeros_like(dy16_ref)

        @pl.when(n > 0)
        def _():
            err = x_ref[...] - t_ref[...]
            dy = err * (1.0 / D)
            dy_ref[...] = dy
            dy16_ref[...] = dy.astype(BF16)
            loss_ref[...] += jnp.sum(err * err) * (0.5 / D)

    blk = pl.BlockSpec((BLOCK, D), lambda n: (n, 0))
    dy, dy16, loss = _pcall(body, name="loss_head", grid=(Lp // BLOCK,),
                            in_specs=[blk, pl.BlockSpec((BLOCK, D), lambda n: (jnp.maximum(n - 1, 0), 0))],
                            out_specs=[blk, blk, pl.BlockSpec((8, LANES), lambda n: (0, 0))],
                            out_shape=[_sds((Lp, D), F32), _sds((Lp, D), BF16), _sds((8, LANES), F32)],
                            compiler_params=_params(("arbitrary",)))(xl, target)
    return loss[0, 0], dy, dy16


def _attn_mask_dist(n):
    i = lax.broadcasted_iota(jnp.int32, (BLOCK, 3 * BLOCK), 0)
    j = lax.broadcasted_iota(jnp.int32, (BLOCK, 3 * BLOCK), 1)
    in_band = j < 2 * BLOCK
    band = in_band & (j > i) & (j <= i + BLOCK) & (j >= 2 * BLOCK - BLOCK * n)
    jm = j - 2 * BLOCK
    meta = (~in_band) & (jm >= PAD) & (jm <= BLOCK * n + i)
    hidden = jnp.where(band | meta, 0.0, NEG_INF)
    dist = jnp.where(in_band, BLOCK + i - j, BLOCK * n + i - jm).astype(F32)
    return jnp.concatenate([hidden] * KV_GROUP, axis=0), jnp.concatenate([dist] * KV_GROUP, axis=0)


def _head_norm(x, g):
    r = lax.rsqrt(jnp.mean(x * x, axis=-1, keepdims=True) + NORM_EPS)
    return (x * r) * g, r


def _attn_specs(attn_w, kv_w):
    kb = attn_w // kv_w
    q_spec = pl.BlockSpec((BLOCK, attn_w), lambda n: (n, 0))

    def kv(col):
        return [pl.BlockSpec((BLOCK, kv_w), lambda n: (jnp.maximum(n - 1, 0), col)),
                pl.BlockSpec((BLOCK, kv_w), lambda n: (n, col)),
                pl.BlockSpec((BLOCK, kv_w), lambda n: (0, col))]

    return q_spec, kv(kb), kv(kb + 1)


def _slopes(n_heads):
    return [2.0 ** (-8.0 * (h + 1) / n_heads) for h in range(n_heads)]


def _head_slice(h):
    return slice(h * HEAD_DIM, (h + 1) * HEAD_DIM)


def _stack_heads(ref, kh):
    return jnp.concatenate([ref[:, _head_slice(kh * KV_GROUP + g)] for g in range(KV_GROUP)], axis=0)


def _group_column(vals):
    return jnp.concatenate([jnp.broadcast_to(v, (BLOCK, 1)) for v in vals], axis=0)


def _group_inputs(kh, slopes, q_ref, kp, kc, km, vp, vc, vm, gq_ref, gk_ref, sk_ref):
    cs = _head_slice(kh)
    kn, _ = _head_norm(jnp.concatenate([kp[:, cs], kc[:, cs], km[:, cs]], axis=0), gk_ref[...])
    vcat = jnp.concatenate([vp[:, cs], vc[:, cs], vm[:, cs]], axis=0).astype(BF16)
    q = _stack_heads(q_ref, kh)
    qn, rq = _head_norm(q, gq_ref[...])
    heads = range(kh * KV_GROUP, (kh + 1) * KV_GROUP)
    slope = _group_column([jnp.full((1, 1), slopes[h], F32) for h in heads])
    sink = _group_column([sk_ref[0:1, h:h + 1] for h in heads])
    return q, qn, rq, kn, vcat, slope, sink


def _scores(qn, kn, slope, sink, hidden, dist):
    s = lax.dot_general(qn.astype(BF16), kn.astype(BF16), (((1,), (1,)), ((), ())), preferred_element_type=F32)
    s = (s * (1.0 / math.sqrt(HEAD_DIM)) - slope * dist) + hidden
    m = jnp.maximum(jnp.max(s, axis=-1, keepdims=True), sink)
    p = jnp.exp(s - m)
    ps = jnp.exp(sink - m)
    inv = 1.0 / (jnp.sum(p, axis=-1, keepdims=True) + ps)
    return p * inv, ps * inv


def _attn_fwd(proj, gq, gk, sinks, attn_w, kv_w):
    Lp = proj.shape[0]
    n_heads, n_kv = attn_w // HEAD_DIM, kv_w // HEAD_DIM
    slopes = _slopes(n_heads)
    q_spec, k_specs, v_specs = _attn_specs(attn_w, kv_w)

    def body(q_ref, kp, kc, km, vp, vc, vm, gq_ref, gk_ref, sk_ref, o_ref):
        mask, dist = _attn_mask_dist(pl.program_id(0))
        for kh in range(n_kv):
            _, qn, _, kn, vcat, slope, sink = _group_inputs(kh, slopes, q_ref, kp, kc, km, vp, vc, vm, gq_ref, gk_ref, sk_ref)
            p, _ = _scores(qn, kn, slope, sink, mask, dist)
            o = jnp.dot(p.astype(BF16), vcat, preferred_element_type=F32)
            for g in range(KV_GROUP):
                o_ref[:, _head_slice(kh * KV_GROUP + g)] = o[g * BLOCK:(g + 1) * BLOCK]

    small = lambda w: pl.BlockSpec((1, w), lambda n: (0, 0))
    return _pcall(body, name="attn_fwd", grid=(Lp // BLOCK,),
                  in_specs=[q_spec] + k_specs + v_specs + [small(HEAD_DIM), small(HEAD_DIM), small(n_heads)],
                  out_specs=pl.BlockSpec((BLOCK, attn_w), lambda n: (n, 0)), out_shape=_sds((Lp, attn_w), F32),
                  compiler_params=_params(("parallel",)))(proj, proj, proj, proj, proj, proj, proj, gq, gk, sinks)


def _attn_bwd(proj, attn, dattn, gq, gk, sinks, attn_w, kv_w):
    Lp = proj.shape[0]
    n_heads, n_kv = attn_w // HEAD_DIM, kv_w // HEAD_DIM
    slopes = _slopes(n_heads)
    q_spec, k_specs, v_specs = _attn_specs(attn_w, kv_w)
    scale = 1.0 / math.sqrt(HEAD_DIM)
    tn_dims = (((0,), (0,)), ((), ()))

    def body(q_ref, kp, kc, km, vp, vc, vm, o_ref, do_ref, gq_ref, gk_ref, sk_ref, dq_ref, dk_ref, dv_ref, dgq_ref, dsk_ref):
        n = pl.program_id(0)

        @pl.when(n == 0)
        def _():
            dk_ref[...] = jnp.zeros_like(dk_ref)
            dv_ref[...] = jnp.zeros_like(dv_ref)
            dgq_ref[...] = jnp.zeros_like(dgq_ref)
            dsk_ref[...] = jnp.zeros_like(dsk_ref)

        mask, dist = _attn_mask_dist(n)
        lane = lax.broadcasted_iota(jnp.int32, (1, n_heads), 1)
        rows_prev = pl.ds(pl.multiple_of(jnp.maximum(n - 1, 0) * BLOCK, BLOCK), BLOCK)
        rows_cur = pl.ds(pl.multiple_of(n * BLOCK, BLOCK), BLOCK)
        rows_meta = pl.ds(0, BLOCK)
        dgq = jnp.zeros((1, HEAD_DIM), F32)
        dsk = jnp.zeros((1, n_heads), F32)
        for kh in range(n_kv):
            cs = _head_slice(kh)
            q, qn, rq, kn, vcat, slope, sink = _group_inputs(kh, slopes, q_ref, kp, kc, km, vp, vc, vm, gq_ref, gk_ref, sk_ref)
            p, ps = _scores(qn, kn, slope, sink, mask, dist)
            do = _stack_heads(do_ref, kh)
            dd = jnp.sum(do * _stack_heads(o_ref, kh), axis=-1, keepdims=True)
            do16 = do.astype(BF16)
            dp = lax.dot_general(do16, vcat, (((1,), (1,)), ((), ())), preferred_element_type=F32)
            ds16 = (p * (dp - dd)).astype(BF16)
            dsink = -ps * dd
            for g in range(KV_GROUP):
                dsk = dsk + jnp.where(lane == kh * KV_GROUP + g, jnp.sum(dsink[g * BLOCK:(g + 1) * BLOCK]), 0.0)
            dqn = jnp.dot(ds16, kn.astype(BF16), preferred_element_type=F32) * scale
            dkn = lax.dot_general(ds16, qn.astype(BF16), tn_dims, preferred_element_type=F32) * scale
            dvc = lax.dot_general(p.astype(BF16), do16, tn_dims, preferred_element_type=F32)
            xh = q * rq
            dgq = dgq + jnp.sum(dqn * xh, axis=0, keepdims=True)
            dyg = dqn * gq_ref[...]
            dq = rq * (dyg - xh * jnp.mean(dyg * xh, axis=-1, keepdims=True))
            for g in range(KV_GROUP):
                dq_ref[:, _head_slice(kh * KV_GROUP + g)] = dq[g * BLOCK:(g + 1) * BLOCK].astype(dq_ref.dtype)
            for part, rows in enumerate((rows_prev, rows_cur, rows_meta)):
                ps_ = slice(part * BLOCK, (part + 1) * BLOCK)
                dk_ref[rows, cs] += dkn[ps_]
                dv_ref[rows, cs] += dvc[ps_]
        dgq_ref[...] += dgq
        dsk_ref[...] += dsk

    small = lambda w: pl.BlockSpec((1, w), lambda n: (0, 0))
    blk = pl.BlockSpec((BLOCK, attn_w), lambda n: (n, 0))
    whole = pl.BlockSpec((Lp, kv_w), lambda n: (0, 0))
    return _pcall(body, name="attn_bwd", grid=(Lp // BLOCK,),
                  in_specs=[q_spec] + k_specs + v_specs + [blk, blk, small(HEAD_DIM), small(HEAD_DIM), small(n_heads)],
                  out_specs=[blk, whole, whole, small(HEAD_DIM), small(n_heads)],
                  out_shape=[_sds(proj.shape, BF16), _sds((Lp, kv_w), F32), _sds((Lp, kv_w), F32),
                             _sds((1, HEAD_DIM), F32), _sds((1, n_heads), F32)],
                  compiler_params=_params(("arbitrary",)))(proj, proj, proj, proj, proj, proj, proj, attn, dattn, gq, gk, sinks)


def _knorm_bwd(proj, dkn, dv, gk, dproj, attn_w, kv_w):
    Lp = proj.shape[0]
    n_kv = kv_w // HEAD_DIM
    tr = _pick(Lp, (384, 256, 128))

    def body(k_ref, d_ref, dv_ref, g_ref, buf_ref, out_ref, dg_ref):
        @pl.when(pl.program_id(0) == 0)
        def _():
            dg_ref[...] = jnp.zeros_like(dg_ref)

        dg = jnp.zeros((1, HEAD_DIM), F32)
        for kh in range(n_kv):
            cs = slice(kh * HEAD_DIM, (kh + 1) * HEAD_DIM)
            x = k_ref[:, cs]
            d = d_ref[:, cs]
            r = lax.rsqrt(jnp.mean(x * x, axis=-1, keepdims=True) + NORM_EPS)
            xh = x * r
            dg = dg + jnp.sum(d * xh, axis=0, keepdims=True)
            dyg = d * g_ref[...]
            out_ref[:, cs] = (r * (dyg - xh * jnp.mean(dyg * xh, axis=-1, keepdims=True))).astype(out_ref.dtype)
        out_ref[:, kv_w:] = dv_ref[...].astype(out_ref.dtype)
        dg_ref[...] += dg

    kv_blk = pl.BlockSpec((tr, kv_w), lambda i: (i, 0))
    return _pcall(body, name="knorm_bwd", grid=(Lp // tr,),
                  in_specs=[pl.BlockSpec((tr, kv_w), lambda i: (i, attn_w // kv_w)), kv_blk, kv_blk,
                            pl.BlockSpec((1, HEAD_DIM), lambda i: (0, 0)), pl.BlockSpec(memory_space=pl.ANY)],
                  out_specs=[pl.BlockSpec((tr, 2 * kv_w), lambda i: (i, attn_w // (2 * kv_w))), pl.BlockSpec((1, HEAD_DIM), lambda i: (0, 0))],
                  out_shape=[_sds(dproj.shape, dproj.dtype), _sds((1, HEAD_DIM), F32)],
                  input_output_aliases={4: 0},
                  compiler_params=_params(("arbitrary",)))(proj, dkn, dv, gk, dproj)


def _ssm_bbar(lr, li, ls, br, bi):
    def fn(lr, li, ls, br, bi):
        fr, fi = _zoh_factor(lr, li, ls)
        return fr * br - fi * bi, fr * bi + fi * br

    return _ew("ssm_bbar", fn, [lr, li, ls, br, bi], [F32, F32])


def _lam_bar(lr, li, ls):
    dl = jnp.exp(ls)
    e = jnp.exp(lr * dl)
    return e * jnp.cos(li * dl), e * jnp.sin(li * dl), dl


def _zoh_factor(lr, li, ls):
    ar, ai, _ = _lam_bar(lr, li, ls)
    n2 = lr * lr + li * li
    ivr, ivi = lr / n2, -li / n2
    return (ar - 1.0) * ivr - ai * ivi, (ar - 1.0) * ivi + ai * ivr


SCAN_SHIFTS = (1, 2, 4)


def _ssm_tables(lr, li, ls):
    Wx = lr.shape[1]

    def body(lr_ref, li_ref, ls_ref, tf_ref, tr_ref):
        dl = jnp.exp(ls_ref[...])
        zr, zi = lr_ref[...] * dl, li_ref[...] * dl
        row = lax.broadcasted_iota(jnp.int32, (8, Wx), 0)

        def power(kf):
            e = jnp.exp(kf * zr)
            return e * jnp.cos(kf * zi), e * jnp.sin(kf * zi)

        for ref, rev in ((tf_ref, False), (tr_ref, True)):
            sgn = -1.0 if rev else 1.0
            for k, d in enumerate(SCAN_SHIFTS):
                ar, ai = power(jnp.full((8, Wx), float(d), F32))
                keep = (row < 8 - d) if rev else (row >= d)
                ref[k] = jnp.where(keep, ar, 0.0)
                ref[4 + k] = jnp.where(keep, sgn * ai, 0.0)
            pr, pi = power(((8 - row) if rev else (row + 1)).astype(F32))
            ref[3] = pr
            ref[7] = sgn * pi

    full = pl.BlockSpec((1, Wx), lambda: (0, 0))
    tab = pl.BlockSpec((8, 8, Wx), lambda: (0, 0, 0))
    return _pcall(body, name="ssm_tables", in_specs=[full] * 3, out_specs=[tab, tab],
                  out_shape=[_sds((8, 8, Wx), F32)] * 2,
                  compiler_params=pltpu.CompilerParams(vmem_limit_bytes=V7X_VMEM_LIMIT_BYTES))(lr, li, ls)


def _scan(name, br, bi, tab, layer, reverse, states=None):
    L, Wx = br.shape
    TB = _pick(L, (384, 256, 128))
    CW = _pick(Wx, (1024, 512, 256, 128))
    nT, nG = L // TB, TB // 8

    def body(*refs):
        if reverse:
            br_ref, bi_ref, xr_ref, xi_ref, tab_ref, or_ref, oi_ref, s1_ref, s2_ref, cr_ref, ci_ref = refs
        else:
            br_ref, bi_ref, tab_ref, or_ref, oi_ref, cr_ref, ci_ref = refs

        @pl.when(pl.program_id(1) == 0)
        def _():
            cr_ref[...] = jnp.zeros_like(cr_ref)
            ci_ref[...] = jnp.zeros_like(ci_ref)
            if reverse:
                s1_ref[...] = jnp.zeros_like(s1_ref)
                s2_ref[...] = jnp.zeros_like(s2_ref)

        def step(q, carry):
            cr, ci = carry[0], carry[1]
            g = (nG - 1 - q) if reverse else q
            rows = pl.ds(pl.multiple_of(g * 8, 8), 8)
            b_r, b_i = br_ref[rows, :], bi_ref[rows, :]
            sr, si = b_r, b_i
            for k, d in enumerate(SCAN_SHIFTS):
                mr, mi = tab_ref[k], tab_ref[4 + k]
                sh = (8 - d) if reverse else d
                pr, pi = pltpu.roll(sr, sh, 0), pltpu.roll(si, sh, 0)
                sr, si = sr + mr * pr - mi * pi, si + mr * pi + mi * pr
            pwr, pwi = tab_ref[3], tab_ref[7]
            xr = sr + pwr * cr - pwi * ci
            xi = si + pwr * ci + pwi * cr
            or_ref[rows, :] = xr
            oi_ref[rows, :] = xi
            row = 0 if reverse else 7
            out = (jnp.broadcast_to(xr[row:row + 1, :], xr.shape), jnp.broadcast_to(xi[row:row + 1, :], xi.shape))
            if reverse:
                hr, hi = xr - b_r, xi - b_i
                st_r, st_i = xr_ref[rows, :], xi_ref[rows, :]
                out = out + (carry[2] + hr * st_r + hi * st_i, carry[3] + hi * st_r - hr * st_i)
            return out

        init = (cr_ref[...], ci_ref[...])
        if reverse:
            init = init + (jnp.zeros((8, CW), F32), jnp.zeros((8, CW), F32))
        fin = lax.fori_loop(0, nG, step, init, unroll=2)
        cr_ref[...] = fin[0]
        ci_ref[...] = fin[1]
        if reverse:
            s1_ref[...] += fin[2]
            s2_ref[...] += fin[3]

    tmap = (lambda j, t: (nT - 1 - t, j)) if reverse else (lambda j, t: (t, j))
    blk = pl.BlockSpec((TB, CW), tmap)
    tab_spec = pl.BlockSpec((8, 8, CW), lambda j, t: (0, 0, layer * (Wx // CW) + j))
    sum_spec = pl.BlockSpec((8, CW), lambda j, t: (0, j))
    ins = [br, bi] + (list(states) if reverse else []) + [tab]
    in_specs = [blk, blk] + ([blk, blk] if reverse else []) + [tab_spec]
    out_specs = [blk, blk] + ([sum_spec, sum_spec] if reverse else [])
    out_shape = [_sds((L, Wx), F32)] * 2 + ([_sds((8, Wx), F32)] * 2 if reverse else [])
    return _pcall(body, name=name, grid=(Wx // CW, nT), in_specs=in_specs, out_specs=out_specs, out_shape=out_shape,
                  scratch_shapes=[pltpu.VMEM((8, CW), F32), pltpu.VMEM((8, CW), F32)],
                  compiler_params=_params(("parallel", "arbitrary")))(*ins)


def _row_tile(L):
    return _pick(L, (1408, 704, 384, 128))


TILES_PER_BLOCK = 4


def _blockproj(name, src, off, w_r, w_i, layer, depth):
    L = src.shape[0]
    T = w_r.shape[0] // depth
    tm = _row_tile(L)
    wide = TILES_PER_BLOCK * LANES

    def body(s_ref, wr_ref, wi_ref, or_ref, oi_ref):
        s = s_ref[...].astype(BF16)
        for k in range(TILES_PER_BLOCK):
            cols = slice(k * LANES, (k + 1) * LANES)
            or_ref[:, cols] = jnp.dot(s, wr_ref[k], preferred_element_type=F32)
            oi_ref[:, cols] = jnp.dot(s, wi_ref[k], preferred_element_type=F32)

    w_spec = pl.BlockSpec((TILES_PER_BLOCK, LANES, LANES), lambda i, q: (layer * (T // TILES_PER_BLOCK) + q, 0, 0))
    o_spec = pl.BlockSpec((tm, wide), lambda i, q: (i, q))
    return _pcall(body, name=name, grid=(L // tm, T // TILES_PER_BLOCK),
                  in_specs=[pl.BlockSpec((tm, LANES), lambda i, q: (i, off + q)), w_spec, w_spec],
                  out_specs=[o_spec, o_spec], out_shape=[_sds((L, T * LANES), F32)] * 2,
                  compiler_params=_params(("parallel", "arbitrary")))(src, w_r, w_i)


def _blockproj_grad(name, src, off, gr, gi):
    L = src.shape[0]
    T = gr.shape[1] // LANES
    tm = _row_tile(L)
    wide = TILES_PER_BLOCK * LANES
    tn_dims = (((0,), (0,)), ((), ()))

    def body(s_ref, gr_ref, gi_ref, or_ref, oi_ref):
        @pl.when(pl.program_id(1) == 0)
        def _():
            or_ref[...] = jnp.zeros_like(or_ref)
            oi_ref[...] = jnp.zeros_like(oi_ref)

        s = s_ref[...].astype(BF16)
        for k in range(TILES_PER_BLOCK):
            cols = slice(k * LANES, (k + 1) * LANES)
            or_ref[k] += lax.dot_general(s, gr_ref[:, cols].astype(BF16), tn_dims, preferred_element_type=F32)
            oi_ref[k] += lax.dot_general(s, gi_ref[:, cols].astype(BF16), tn_dims, preferred_element_type=F32)

    g_spec = pl.BlockSpec((tm, wide), lambda q, i: (i, q))
    o_spec = pl.BlockSpec((TILES_PER_BLOCK, LANES, LANES), lambda q, i: (q, 0, 0))
    return _pcall(body, name=name, grid=(T // TILES_PER_BLOCK, L // tm),
                  in_specs=[pl.BlockSpec((tm, LANES), lambda q, i: (i, off + q)), g_spec, g_spec],
                  out_specs=[o_spec, o_spec], out_shape=[_sds((T, LANES, LANES), F32)] * 2,
                  compiler_params=_params(("parallel", "arbitrary")))(src, gr, gi)


def _gelu(y):
    k = math.sqrt(2.0 / math.pi)
    return 0.5 * y * (1.0 + jnp.tanh(k * (y + 0.044715 * (y * y * y))))


def _gelu_grad(y):
    k = math.sqrt(2.0 / math.pi)
    t = jnp.tanh(k * (y + 0.044715 * (y * y * y)))
    return 0.5 * (1.0 + t) + 0.5 * y * (1.0 - t * t) * (k * (1.0 + 3 * 0.044715 * (y * y)))


def _ssm_out(xr, xi, w_r, w_i, proj, u_off, dvec, layer, depth):
    L = xr.shape[0]
    J = w_r.shape[0] // depth
    SW = w_r.shape[1]
    tm = _row_tile(L)

    def body(xr_ref, xi_ref, wr_ref, wi_ref, u_ref, d_ref, y_ref, gl_ref):
        acc = jnp.dot(xr_ref[...].astype(BF16), wr_ref[...], preferred_element_type=F32)
        acc += jnp.dot(xi_ref[...].astype(BF16), wi_ref[...], preferred_element_type=F32)
        y = acc + d_ref[...] * u_ref[...]
        y_ref[...] = y
        gl_ref[...] = _gelu(y)

    x_spec = pl.BlockSpec((tm, SW), lambda j, i: (i, j))
    w_spec = pl.BlockSpec((None, SW, LANES), lambda j, i: (layer * J + j, 0, 0))
    o_spec = pl.BlockSpec((tm, LANES), lambda j, i: (i, j))
    return _pcall(body, name="ssm_out", grid=(J, L // tm),
                  in_specs=[x_spec, x_spec, w_spec, w_spec, pl.BlockSpec((tm, LANES), lambda j, i: (i, u_off + j)),
                            pl.BlockSpec((1, LANES), lambda j, i: (0, j))],
                  out_specs=[o_spec, o_spec], out_shape=[_sds((L, J * LANES), F32)] * 2,
                  compiler_params=_params(("parallel", "parallel")))(xr, xi, w_r, w_i, proj, dvec)


def _ssm_du(gr, gi, w_r, w_i, dy, proj, u_off, dvec, dproj, layer, depth):
    L = gr.shape[0]
    J = w_r.shape[0] // depth
    SW = w_r.shape[1]
    tm = _row_tile(L)

    def body(gr_ref, gi_ref, wr_ref, wi_ref, dy_ref, u_ref, d_ref, buf_ref, du_ref, dd_ref):
        i = pl.program_id(1)

        @pl.when(i == 0)
        def _():
            dd_ref[...] = jnp.zeros_like(dd_ref)

        acc = jnp.dot(gr_ref[...].astype(BF16), wr_ref[...], preferred_element_type=F32)
        acc += jnp.dot(gi_ref[...].astype(BF16), wi_ref[...], preferred_element_type=F32)
        dy = dy_ref[...]
        row = lax.broadcasted_iota(jnp.int32, (tm, LANES), 0) + i * tm
        du_ref[...] = jnp.where(row >= PAD, acc + d_ref[...] * dy, 0.0).astype(du_ref.dtype)
        dd_ref[...] += jnp.sum(dy * u_ref[...], axis=0, keepdims=True)

    x_spec = pl.BlockSpec((tm, SW), lambda j, i: (i, j))
    w_spec = pl.BlockSpec((None, SW, LANES), lambda j, i: (layer * J + j, 0, 0))
    o_spec = pl.BlockSpec((tm, LANES), lambda j, i: (i, j))
    u_spec = pl.BlockSpec((tm, LANES), lambda j, i: (i, u_off + j))
    vec = pl.BlockSpec((1, LANES), lambda j, i: (0, j))
    return _pcall(body, name="ssm_du", grid=(J, L // tm),
                  in_specs=[x_spec, x_spec, w_spec, w_spec, o_spec, u_spec, vec, pl.BlockSpec(memory_space=pl.ANY)],
                  out_specs=[u_spec, vec], out_shape=[_sds(dproj.shape, dproj.dtype), _sds((1, J * LANES), F32)],
                  input_output_aliases={7: 0},
                  compiler_params=_params(("parallel", "arbitrary")))(gr, gi, w_r, w_i, dy, proj, dvec, dproj)


def _ssm_dc(xr, xi, dy, SW):
    L = xr.shape[0]
    J = dy.shape[1] // LANES
    tm = _row_tile(L)
    tn_dims = (((0,), (0,)), ((), ()))

    def body(xr_ref, xi_ref, dy_ref, or_ref, oi_ref):
        @pl.when(pl.program_id(1) == 0)
        def _():
            or_ref[...] = jnp.zeros_like(or_ref)
            oi_ref[...] = jnp.zeros_like(oi_ref)

        d = dy_ref[...].astype(BF16)
        or_ref[...] += lax.dot_general(xr_ref[...].astype(BF16), d, tn_dims, preferred_element_type=F32)
        oi_ref[...] += lax.dot_general(xi_ref[...].astype(BF16), d, tn_dims, preferred_element_type=F32)

    x_spec = pl.BlockSpec((tm, SW), lambda j, i: (i, j))
    o_spec = pl.BlockSpec((None, SW, LANES), lambda j, i: (j, 0, 0))
    return _pcall(body, name="ssm_dc", grid=(J, L // tm),
                  in_specs=[x_spec, x_spec, pl.BlockSpec((tm, LANES), lambda j, i: (i, j))],
                  out_specs=[o_spec, o_spec], out_shape=[_sds((J, SW, LANES), F32)] * 2,
                  compiler_params=_params(("parallel", "arbitrary")))(xr, xi, dy)


SCAN_COLS = 1024
SCAN_TILES = SCAN_COLS // LANES
SCAN_CH = SCAN_COLS // STATE * GROUP_CH
SCAN_SLAB = 8 * STATE


def _scan_rows(b_r, b_i, tab_ref, carry, reverse):
    sr, si = b_r, b_i
    for k, d in enumerate(SCAN_SHIFTS):
        mr, mi = tab_ref[k], tab_ref[4 + k]
        sh = (8 - d) if reverse else d
        pr, pi = pltpu.roll(sr, sh, 0), pltpu.roll(si, sh, 0)
        sr, si = sr + mr * pr - mi * pi, si + mr * pi + mi * pr
    pwr, pwi = tab_ref[3], tab_ref[7]
    xr = sr + pwr * carry[0] - pwi * carry[1]
    xi = si + pwr * carry[1] + pwi * carry[0]
    row = 0 if reverse else 7
    return xr, xi, (jnp.broadcast_to(xr[row:row + 1, :], xr.shape), jnp.broadcast_to(xi[row:row + 1, :], xi.shape))


def _ssm_fwd(proj, u_off, s, dvec):
    L = proj.shape[0]
    layer, depth = s['layer'], s['depth']
    T, J = s['wb'][0].shape[0] // depth, s['wc'][0].shape[0] // depth
    Wx, nC = T * LANES, T * LANES // SCAN_COLS
    TB = _pick(L, (704, 384, 256, 128))
    nT, nG = L // TB, TB // 8

    def body(u_ref, wbr_ref, wbi_ref, wcr_ref, wci_ref, d_ref, tab_ref, xr_ref, xi_ref, y_ref, gl_ref, cr_ref, ci_ref):
        @pl.when(pl.program_id(1) == 0)
        def _():
            cr_ref[...] = jnp.zeros_like(cr_ref)
            ci_ref[...] = jnp.zeros_like(ci_ref)

        u = u_ref[...]
        u16 = u.astype(BF16)
        for k in range(SCAN_TILES):
            blk = u16[:, (k // TILES_PER_BLOCK) * LANES:(k // TILES_PER_BLOCK + 1) * LANES]
            cols = slice(k * LANES, (k + 1) * LANES)
            xr_ref[:, cols] = jnp.dot(blk, wbr_ref[k], preferred_element_type=F32)
            xi_ref[:, cols] = jnp.dot(blk, wbi_ref[k], preferred_element_type=F32)

        def step(q, carry):
            rows = pl.ds(pl.multiple_of(q * 8, 8), 8)
            xr, xi, carry = _scan_rows(xr_ref[rows, :], xi_ref[rows, :], tab_ref, carry, False)
            xr_ref[rows, :] = xr
            xi_ref[rows, :] = xi
            return carry

        cr, ci = lax.fori_loop(0, nG, step, (cr_ref[...], ci_ref[...]), unroll=2)
        cr_ref[...] = cr
        ci_ref[...] = ci
        for c in range(SCAN_CH // LANES):
            slab, ch = slice(c * SCAN_SLAB, (c + 1) * SCAN_SLAB), slice(c * LANES, (c + 1) * LANES)
            acc = jnp.dot(xr_ref[:, slab].astype(BF16), wcr_ref[c], preferred_element_type=F32)
            acc += jnp.dot(xi_ref[:, slab].astype(BF16), wci_ref[c], preferred_element_type=F32)
            y = acc + d_ref[:, ch] * u[:, ch]
            y_ref[:, ch] = y
            gl_ref[:, ch] = _gelu(y)

    x_spec = pl.BlockSpec((TB, SCAN_COLS), lambda j, t: (t, j))
    y_spec = pl.BlockSpec((TB, SCAN_CH), lambda j, t: (t, j))
    tile_w = pl.BlockSpec((SCAN_TILES, LANES, LANES), lambda j, t: (layer * nC + j, 0, 0))
    slab_w = pl.BlockSpec((SCAN_CH // LANES, SCAN_SLAB, LANES), lambda j, t: (layer * nC + j, 0, 0))
    return _pcall(body, name="ssm_fwd", grid=(nC, nT),
                  in_specs=[pl.BlockSpec((TB, SCAN_CH), lambda j, t: (t, u_off * LANES // SCAN_CH + j)), tile_w, tile_w, slab_w, slab_w,
                            pl.BlockSpec((1, SCAN_CH), lambda j, t: (0, j)),
                            pl.BlockSpec((8, 8, SCAN_COLS), lambda j, t: (0, 0, layer * nC + j))],
                  out_specs=[x_spec, x_spec, y_spec, y_spec],
                  out_shape=[_sds((L, Wx), F32)] * 2 + [_sds((L, J * LANES), F32)] * 2,
                  scratch_shapes=[pltpu.VMEM((8, SCAN_COLS), F32)] * 2,
                  compiler_params=_params(("parallel", "arbitrary")))(proj, *s['wb'], *s['wc'], dvec, s['tf'])


def _ssm_bwd(dy, proj, u_off, xr, xi, s, dvec, dproj):
    L = dy.shape[0]
    layer, depth = s['layer'], s['depth']
    T, J = s['wb'][0].shape[0] // depth, s['wc'][0].shape[0] // depth
    Wx, nC = T * LANES, T * LANES // SCAN_COLS
    TB = _pick(L, (704, 384, 256, 128))
    nT, nG = L // TB, TB // 8
    n_ch = SCAN_CH // LANES
    tn_dims = (((0,), (0,)), ((), ()))

    def body(dy_ref, u_ref, xr_ref, xi_ref, wcr_ref, wci_ref, wbr_ref, wbi_ref, d_ref, tab_ref, buf_ref,
             du_ref, dd_ref, dwbr_ref, dwbi_ref, dwcr_ref, dwci_ref, s1_ref, s2_ref, gr_ref, gi_ref, cr_ref, ci_ref):
        t = pl.program_id(1)

        @pl.when(t == 0)
        def _():
            for ref in (cr_ref, ci_ref, dd_ref, dwbr_ref, dwbi_ref, dwcr_ref, dwci_ref, s1_ref, s2_ref):
                ref[...] = jnp.zeros_like(ref)

        dy = dy_ref[...]
        dy16 = dy.astype(BF16)
        u = u_ref[...]
        u16 = u.astype(BF16)
        for k in range(SCAN_TILES):
            blk = dy16[:, (k // TILES_PER_BLOCK) * LANES:(k // TILES_PER_BLOCK + 1) * LANES]
            cols = slice(k * LANES, (k + 1) * LANES)
            gr_ref[:, cols] = jnp.dot(blk, wcr_ref[k], preferred_element_type=F32)
            gi_ref[:, cols] = jnp.dot(blk, wci_ref[k], preferred_element_type=F32)

        def step(q, carry):
            rows = pl.ds(pl.multiple_of((nG - 1 - q) * 8, 8), 8)
            b_r, b_i = gr_ref[rows, :], gi_ref[rows, :]
            g_r, g_i, edge = _scan_rows(b_r, b_i, tab_ref, carry[:2], True)
            gr_ref[rows, :] = g_r
            gi_ref[rows, :] = g_i
            hr, hi = g_r - b_r, g_i - b_i
            st_r, st_i = xr_ref[rows, :], xi_ref[rows, :]
            return edge + (carry[2] + hr * st_r + hi * st_i, carry[3] + hi * st_r - hr * st_i)

        zero = jnp.zeros((8, SCAN_COLS), F32)
        fin = lax.fori_loop(0, nG, step, (cr_ref[...], ci_ref[...], zero, zero), unroll=2)
        cr_ref[...] = fin[0]
        ci_ref[...] = fin[1]
        s1_ref[...] += fin[2]
        s2_ref[...] += fin[3]

        row = lax.broadcasted_iota(jnp.int32, (TB, LANES), 0) + (nT - 1 - t) * TB
        for c in range(n_ch):
            slab, ch = slice(c * SCAN_SLAB, (c + 1) * SCAN_SLAB), slice(c * LANES, (c + 1) * LANES)
            g16r, g16i = gr_ref[:, slab].astype(BF16), gi_ref[:, slab].astype(BF16)
            acc = jnp.dot(g16r, wbr_ref[c], preferred_element_type=F32) + jnp.dot(g16i, wbi_ref[c], preferred_element_type=F32)
            du_ref[:, ch] = jnp.where(row >= PAD, acc + d_ref[:, ch] * dy[:, ch], 0.0).astype(du_ref.dtype)
            dwcr_ref[c] += lax.dot_general(xr_ref[:, slab].astype(BF16), dy16[:, ch], tn_dims, preferred_element_type=F32)
            dwci_ref[c] += lax.dot_general(xi_ref[:, slab].astype(BF16), dy16[:, ch], tn_dims, preferred_element_type=F32)
            for kk in range(TILES_PER_BLOCK):
                k = c * TILES_PER_BLOCK + kk
                cols = slice(kk * LANES, (kk + 1) * LANES)
                dwbr_ref[k] += lax.dot_general(u16[:, ch], g16r[:, cols], tn_dims, preferred_element_type=F32)
                dwbi_ref[k] += lax.dot_general(u16[:, ch], g16i[:, cols], tn_dims, preferred_element_type=F32)
        dd_ref[...] += jnp.sum(dy * u, axis=0, keepdims=True)

    rev = lambda j, t: (nT - 1 - t, j)
    x_spec = pl.BlockSpec((TB, SCAN_COLS), rev)
    u_spec = pl.BlockSpec((TB, SCAN_CH), lambda j, t: (nT - 1 - t, u_off * LANES // SCAN_CH + j))
    tile_w = pl.BlockSpec((SCAN_TILES, LANES, LANES), lambda j, t: (layer * nC + j, 0, 0))
    slab_w = pl.BlockSpec((n_ch, SCAN_SLAB, LANES), lambda j, t: (layer * nC + j, 0, 0))
    tile_g = pl.BlockSpec((SCAN_TILES, LANES, LANES), lambda j, t: (j, 0, 0))
    slab_g = pl.BlockSpec((n_ch, SCAN_SLAB, LANES), lambda j, t: (j, 0, 0))
    vec = pl.BlockSpec((1, SCAN_CH), lambda j, t: (0, j))
    sums = pl.BlockSpec((8, SCAN_COLS), lambda j, t: (0, j))
    return _pcall(body, name="ssm_bwd", grid=(nC, nT),
                  in_specs=[pl.BlockSpec((TB, SCAN_CH), rev), u_spec, x_spec, x_spec, tile_w, tile_w, slab_w, slab_w, vec,
                            pl.BlockSpec((8, 8, SCAN_COLS), lambda j, t: (0, 0, layer * nC + j)), pl.BlockSpec(memory_space=pl.ANY)],
                  out_specs=[u_spec, vec, tile_g, tile_g, slab_g, slab_g, sums, sums],
                  out_shape=[_sds(dproj.shape, dproj.dtype), _sds((1, J * LANES), F32), _sds((T, LANES, LANES), F32),
                             _sds((T, LANES, LANES), F32), _sds((J, SCAN_SLAB, LANES), F32), _sds((J, SCAN_SLAB, LANES), F32),
                             _sds((8, Wx), F32), _sds((8, Wx), F32)],
                  input_output_aliases={10: 0},
                  scratch_shapes=[pltpu.VMEM((TB, SCAN_COLS), F32)] * 2 + [pltpu.VMEM((8, SCAN_COLS), F32)] * 2,
                  compiler_params=_params(("parallel", "arbitrary")))(dy, proj, xr, xi, *s['wcT'], *s['wbT'], dvec, s['tr'], dproj)


def _glu_dz(ds, gl, z):
    L, W = ds.shape
    tr = _pick(L, (384, 256, 128))

    def body(ds_ref, gl_ref, z_ref, dz_ref, db_ref):
        @pl.when(pl.program_id(0) == 0)
        def _():
            db_ref[...] = jnp.zeros_like(db_ref)

        sg = jax.nn.sigmoid(z_ref[...])
        dz = ds_ref[...] * gl_ref[...] * (sg * (1.0 - sg))
        dz_ref[...] = dz.astype(BF16)
        db_ref[...] += jnp.sum(dz, axis=0, keepdims=True)

    spec = pl.BlockSpec((tr, W), lambda i: (i, 0))
    vec = pl.BlockSpec((1, W), lambda i: (0, 0))
    return _pcall(body, name="glu_dz", grid=(L // tr,), in_specs=[spec] * 3, out_specs=[spec, vec],
                  out_shape=[_sds((L, W), BF16), _sds((1, W), F32)], compiler_params=_params(("arbitrary",)))(ds, gl, z)


def _ssm_param_bwd_flat(lr, li, ls, br, bi, dbbr, dbbi):
    def seg_sum(x):
        for s in (8, 4, 2, 1):
            x = x + pltpu.roll(x, LANES - s, 1)
        return x

    def fn(lr, li, ls, br, bi, dbbr, dbbi):
        fr, fi = _zoh_factor(lr, li, ls)
        return (fr * dbbr + fi * dbbi, fr * dbbi - fi * dbbr,
                seg_sum(br * dbbr + bi * dbbi), seg_sum(br * dbbi - bi * dbbr))

    return _ew("ssm_param_bwd_flat", fn, [lr, li, ls, br, bi, dbbr, dbbi], [F32] * 4)


def _ssm_param_bwd(lr, li, ls, dfr, dfi, s1, s2):
    G, P = lr.shape

    def body(lr_ref, li_ref, ls_ref, dfr_ref, dfi_ref, s1_ref, s2_ref, dlr_ref, dli_ref, dls_ref):
        lr, li = lr_ref[...], li_ref[...]
        ar, ai, dl = _lam_bar(lr, li, ls_ref[...])
        sr, si = s1_ref[0], s2_ref[0]
        for k in range(1, 8):
            sr = sr + s1_ref[k]
            si = si + s2_ref[k]
        a2 = ar * ar + ai * ai
        gar, gai = (sr * ar - si * ai) / a2, (sr * ai + si * ar) / a2
        n2 = lr * lr + li * li
        ivr, ivi = lr / n2, -li / n2
        fr = (ar - 1.0) * ivr - ai * ivi
        fi = (ar - 1.0) * ivi + ai * ivr
        dfr, dfi = dfr_ref[...], dfi_ref[...]
        gar = gar + ivr * dfr + ivi * dfi
        gai = gai + ivr * dfi - ivi * dfr
        wr, wi = -(fr * ivr - fi * ivi), -(fr * ivi + fi * ivr)
        glr, gli = wr * dfr + wi * dfi, wr * dfi - wi * dfr
        gzr, gzi = ar * gar + ai * gai, ar * gai - ai * gar
        dlr_ref[...] = glr + dl * gzr
        dli_ref[...] = gli + dl * gzi
        dls_ref[...] = dl * jnp.sum(lr * gzr + li * gzi, axis=-1, keepdims=True)

    m = pl.BlockSpec((G, P), lambda: (0, 0))
    v = pl.BlockSpec((G, 1), lambda: (0, 0))
    s = pl.BlockSpec((8, G, P), lambda: (0, 0, 0))
    return _pcall(body, name="ssm_param_bwd", in_specs=[m, m, v, m, m, s, s], out_specs=[m, m, v],
                  out_shape=[_sds((G, P), F32), _sds((G, P), F32), _sds((G, 1), F32)])(lr, li, ls, dfr, dfi, s1, s2)


def _tile_mask(G):
    T = G // 2
    e = np.zeros((T, 8, 1, 2, 1), np.float32)
    for t in range(T):
        for c in range(2):
            e[t, (2 * t + c) % 8, 0, c, 0] = 1.0
    return e


def _tile_w(arr):
    T = arr.shape[0] // 2
    a = arr.reshape(T, 2, STATE, GROUP_CH).transpose(0, 3, 1, 2).reshape(T, 1, GROUP_CH, LANES).astype(BF16)
    shape = (T, 8, 1, LANES)
    t, e, lane = (lax.broadcasted_iota(jnp.int32, shape, d) for d in (0, 1, 3))
    return jnp.where(e == (2 * t + lane // STATE) % 8, a, 0).reshape(T, LANES, LANES)


def _tile_w_grad(dw):
    G = dw.shape[0] * 2
    d = dw.reshape(G // 2, 8, GROUP_CH, 2, STATE) * _tile_mask(G)
    return d.sum(axis=1).transpose(0, 2, 3, 1).reshape(G, STATE, GROUP_CH)


def _slab_w(arr):
    J = arr.shape[0] // 8
    a = jnp.tile(arr.reshape(J, 8, STATE, GROUP_CH).astype(BF16), (1, 1, 1, 8))
    g, lane = (lax.broadcasted_iota(jnp.int32, (1, 8, 1, LANES), d) for d in (1, 3))
    return jnp.where(g == lane // GROUP_CH, a, 0).reshape(J, 8 * STATE, LANES)


def _slab_w_grad(dw):
    J = dw.shape[0]
    eye = np.eye(8, dtype=np.float32).reshape(1, 8, 1, 8, 1)
    return (dw.reshape(J, 8, STATE, 8, GROUP_CH) * eye).sum(axis=3).reshape(J * 8, STATE, GROUP_CH)


def _exchange(name, ins, out_sds, remote, local, aliases=None):
    n_in, n_out, n_r, n_l = len(ins), len(out_sds), len(remote), len(local)

    def body(*refs):
        in_refs, out_refs = refs[:n_in], refs[n_in:n_in + n_out]
        send_sems, recv_sems, local_sems = refs[n_in + n_out:]
        x, y, c = lax.axis_index("x"), lax.axis_index("y"), lax.axis_index("c")

        def place(px, py, pc):
            return dict(x=px, y=py, c=pc, chip=2 * px + py)

        def flip(mask):
            mx, my, mc = mask
            return ((1 - x) if mx else x, (1 - y) if my else y, (1 - c) if mc else c)

        me = place(x, y, c)
        sends = []
        for k, (ii, src, oi, dst, mask) in enumerate(remote):
            cp = pltpu.make_async_remote_copy(src_ref=src(in_refs[ii], me), dst_ref=dst(out_refs[oi], me),
                                              send_sem=send_sems.at[k], recv_sem=recv_sems.at[k],
                                              device_id=flip(mask), device_id_type=MESH)
            cp.start()
            sends.append(cp)
        locals_ = []
        for k, (ii, src, oi, dst) in enumerate(local):
            cp = pltpu.make_async_copy(src(in_refs[ii], me), dst(out_refs[oi], me), local_sems.at[k])
            cp.start()
            locals_.append(cp)
        for k, (ii, src, oi, dst, mask) in enumerate(remote):
            sends[k].wait_send()
            peer = flip(mask)
            pltpu.make_async_remote_copy(src_ref=src(in_refs[ii], me), dst_ref=dst(out_refs[oi], place(*peer)),
                                         send_sem=send_sems.at[k], recv_sem=recv_sems.at[k],
                                         device_id=peer, device_id_type=MESH).wait_recv()
        for cp in locals_:
            cp.wait()

    any_spec = pl.BlockSpec(memory_space=pl.ANY)
    return _pcall(body, name=name, in_specs=[any_spec] * n_in, out_specs=[any_spec] * n_out, out_shape=list(out_sds),
                  input_output_aliases=aliases or {},
                  scratch_shapes=[pltpu.SemaphoreType.DMA((n_r,)), pltpu.SemaphoreType.DMA((n_r,)),
                                  pltpu.SemaphoreType.DMA((max(n_l, 1),))])(*ins)


def _mesh_place():
    x, y, c = lax.axis_index("x"), lax.axis_index("y"), lax.axis_index("c")

    def place(px, py, pc):
        return dict(x=px, y=py, c=pc, chip=2 * px + py)

    def flip(mask):
        mx, my, mc = mask
        return ((1 - x) if mx else x, (1 - y) if my else y, (1 - c) if mc else c)

    return place(x, y, c), place, flip


_HBM = pl.BlockSpec(memory_space=pltpu.HBM)
_SEM = pl.BlockSpec(memory_space=pltpu.SEMAPHORE)
_EFFECT = pltpu.SideEffectType.DATAFLOW_SIDE_EFFECTING


def _split_start(name, bufs, groups):
    n, ng = len(bufs), len(groups)

    def body(*refs):
        in_refs, sems, token = refs[:n], refs[n:n + 2 * ng], refs[-1]
        me, _, flip = _mesh_place()
        for g, copies in enumerate(groups):
            for k, (si, src, di, dst, mask) in enumerate(copies):
                pltpu.make_async_remote_copy(src_ref=src(in_refs[si], me), dst_ref=dst(in_refs[di], me),
                                             send_sem=sems[2 * g].at[k], recv_sem=sems[2 * g + 1].at[k],
                                             device_id=flip(mask), device_id_type=MESH).start()
        token[...] = jnp.zeros_like(token)

    outs = _pcall(body, name=name,
                  out_shape=(*[pltpu.SemaphoreType.DMA((len(g),)) for g in groups for _ in range(2)],
                             *[pltpu.HBM(b.shape, b.dtype) for b in bufs], _sds((8, LANES), F32)),
                  in_specs=[_HBM] * n, out_specs=(*[_SEM] * (2 * ng), *[_HBM] * n, pl.BlockSpec(memory_space=pltpu.VMEM)),
                  input_output_aliases={i: 2 * ng + i for i in range(n)},
                  compiler_params=pltpu.CompilerParams(has_side_effects=_EFFECT),
                  )(*[pltpu.with_memory_space_constraint(b, pltpu.HBM) for b in bufs])
    return [(outs[2 * g], outs[2 * g + 1]) for g in range(ng)], list(outs[2 * ng:2 * ng + n]), outs[-1]


def _split_wait(name, bufs, sems, after, remote):
    n = len(bufs)
    send_sems, recv_sems = sems

    def body(*refs):
        in_refs, ssem, rsem = refs[:n], refs[n], refs[n + 1]
        me, place, flip = _mesh_place()
        for k, (si, src, di, dst, mask) in enumerate(remote):
            peer = flip(mask)
            cp = pltpu.make_async_remote_copy(src_ref=src(in_refs[si], me), dst_ref=dst(in_refs[di], place(*peer)),
                                              send_sem=ssem.at[k], recv_sem=rsem.at[k], device_id=peer, device_id_type=MESH)
            cp.wait_send()
            cp.wait_recv()

    return list(_pcall(body, name=name, out_shape=tuple(pltpu.HBM(b.shape, b.dtype) for b in bufs),
                       in_specs=[_HBM] * n + [_SEM, _SEM, pl.BlockSpec(memory_space=pl.ANY)], out_specs=tuple([_HBM] * n),
                       input_output_aliases={i: i for i in range(n)},
                       compiler_params=pltpu.CompilerParams(has_side_effects=_EFFECT))(*bufs, send_sems, recv_sems, after))


CHIP_MASKS = ((0, 1, 0), (1, 0, 0), (1, 1, 0))
SIBLING = (0, 0, 1)


def _whole(ref, p):
    return ref


def _all_gather(name, shards, col_sharded):
    def dst_view(col):
        def view(ref, p):
            r, cdim = ref.shape[0] // (1 if col else N_CHIPS), ref.shape[1] // (N_CHIPS if col else 1)
            if col:
                return ref.at[:, pl.ds(pl.multiple_of(p["chip"] * cdim, LANES), cdim)]
            return ref.at[pl.ds(pl.multiple_of(p["chip"] * r, 8), r), :]
        return view

    out_sds = [_sds((s.shape[0], s.shape[1] * N_CHIPS) if col else (s.shape[0] * N_CHIPS, s.shape[1]), s.dtype)
               for s, col in zip(shards, col_sharded)]
    remote = [(a, _whole, a, dst_view(col), m) for a, col in enumerate(col_sharded) for m in CHIP_MASKS]
    local = [(a, _whole, a, dst_view(col)) for a, col in enumerate(col_sharded)]
    return _exchange(name, shards, out_sds, remote, local)


class _Place:
    def __getitem__(self, k):
        return lax.axis_index("c") if k == 0 else 2 * lax.axis_index("x") + lax.axis_index("y")


def _placed_call(body, name, grid, in_specs, out_specs, out_shape, sem, ins):
    def wrap(spec):
        return pl.BlockSpec(spec.block_shape, lambda *idx: spec.index_map(*idx, _Place()))

    outs = [wrap(s) for s in out_specs] if isinstance(out_specs, (list, tuple)) else wrap(out_specs)
    return _pcall(body, name=name, grid=grid, in_specs=[wrap(s) for s in in_specs], out_specs=outs, out_shape=out_shape,
                  compiler_params=_params(sem))(*ins)


def _rows_within(n, width, limit=512 * 1024):
    return _pick(n, tuple(t for t in (1024, 512, 256, 128, 64, 32, 16) if t * width <= limit) or (16,))


def _region_view(col):
    def view(ref, p):
        if col:
            cdim = ref.shape[1] // N_CHIPS
            return ref.at[:, pl.ds(pl.multiple_of(p["chip"] * cdim, LANES), cdim)]
        r = ref.shape[0] // N_CHIPS
        return ref.at[pl.ds(pl.multiple_of(p["chip"] * r, 16), r), :]
    return view


def _ag_place(name, w, layer, col, dtype):
    _, r, cdim = w.shape
    tr = _rows_within(r, cdim)
    nb = r // tr

    def body(w_ref, o_ref):
        o_ref[...] = w_ref[...].astype(dtype)

    if col:
        out_shape, out_spec = (r, N_CHIPS * cdim), pl.BlockSpec((tr, cdim), lambda i, pr: (i, pr[1]))
    else:
        out_shape, out_spec = (N_CHIPS * r, cdim), pl.BlockSpec((tr, cdim), lambda i, pr: (pr[1] * nb + i, 0))
    return _placed_call(body, name, (nb,), [pl.BlockSpec((None, tr, cdim), lambda i, pr: (layer, i, 0))], out_spec,
                        _sds(out_shape, dtype), ("parallel",), [w])


def _ag_copies(a, col):
    return [(a, _region_view(col), a, _region_view(col), m) for m in CHIP_MASKS]


def _rs_add2(name, g4, a4, out_dtype):
    J, _, h, C = g4.shape
    tr = _rows_within(h, C)

    def body(g_ref, a_ref, o_ref):
        o_ref[...] = (g_ref[...].astype(F32) + a_ref[...].astype(F32)).astype(o_ref.dtype)

    return _placed_call(body, name, (J, h // tr),
                        [pl.BlockSpec((None, None, tr, C), lambda j, i, pr: (j, pr[0], i, 0)),
                         pl.BlockSpec((None, None, tr, C), lambda j, i, pr: (j, 0, i, 0))],
                        pl.BlockSpec((None, tr, C), lambda j, i, pr: (j, i, 0)), _sds((J, h, C), out_dtype),
                        ("parallel", "parallel"), [g4, a4])


def _rs_add4(name, p3, landed, col):
    _, h, w = landed.shape
    tr = _rows_within(h, w)

    def body(p_ref, a_ref, b_ref, c_ref, o_ref):
        o_ref[...] = ((p_ref[...].astype(F32) + a_ref[...].astype(F32)) + b_ref[...].astype(F32)) + c_ref[...].astype(F32)

    own = (pl.BlockSpec((None, tr, w), lambda i, pr: (0, i, pr[1])) if col else pl.BlockSpec((None, tr, w), lambda i, pr: (pr[1], i, 0)))
    slot = lambda k: pl.BlockSpec((None, tr, w), lambda i, pr: (k, i, 0))
    return _placed_call(body, name, (h // tr,), [own, slot(0), slot(1), slot(2)], pl.BlockSpec((tr, w), lambda i, pr: (i, 0)),
                        _sds((h, w), F32), ("parallel",), [p3, landed, landed, landed])


def _rs_start(tag, grads, col_sharded):
    n = len(grads)
    g4 = [g.reshape((1, 2, g.shape[0] // 2, g.shape[1]) if col else (N_CHIPS, 2, g.shape[0] // (2 * N_CHIPS), g.shape[1]))
          for g, col in zip(grads, col_sharded)]
    other_half = lambda ref, p: ref.at[:, pl.ds(1 - p["c"], 1)]
    landing = [lax.empty((g.shape[0], 1) + g.shape[2:], g.dtype) for g in g4]
    copies = [(a, other_half, n + a, _whole, SIBLING) for a in range(n)]
    (sems,), bufs, token = _split_start("rs_sibling_start_" + tag, g4 + landing, [copies])
    return dict(tag=tag, stage=0, sems=sems, bufs=bufs, copies=copies, token=token, col_sharded=list(col_sharded))


def _rs_advance(st, after):
    col_sharded, tag = st['col_sharded'], st['tag']
    n = len(col_sharded)
    stage = st['stage']
    st['stage'] = stage + 1
    if stage == 0:
        bufs = _split_wait("rs_sibling_wait_" + tag, st['bufs'], st['sems'], after, st['copies'])
        st.update(_rs_chips_start(tag, [_rs_add2("rs_add2_w", bufs[a], bufs[n + a], BF16) for a in range(n)], col_sharded))
    elif stage == 1:
        bufs = _split_wait("rs_chips_wait_" + tag, st['bufs'], st['sems'], after, st['copies'])
        mine = [_rs_add4("rs_add4_w", bufs[a], bufs[n + a], col_sharded[a]) for a in range(n)]
        copies = [(a, _whole, n + a, _whole, SIBLING) for a in range(n)]
        (sems,), bufs, token = _split_start("rs_halves_start_" + tag, mine + [lax.empty(m.shape, F32) for m in mine], [copies])
        st.update(sems=sems, bufs=bufs, copies=copies, token=token)
    else:
        bufs = _split_wait("rs_halves_wait_" + tag, st['bufs'], st['sems'], after, st['copies'])
        st['result'] = (bufs[:n], bufs[n:])


def _rs_chips_start(tag, chip_sum, col_sharded):
    n = len(chip_sum)

    def send_view(col, mask):
        def view(ref, p):
            t = 2 * ((1 - p["x"]) if mask[0] else p["x"]) + ((1 - p["y"]) if mask[1] else p["y"])
            if col:
                sc = ref.shape[2] // N_CHIPS
                return ref.at[0, :, pl.ds(pl.multiple_of(t * sc, LANES), sc)]
            return ref.at[t]
        return view
    slot = lambda k: (lambda ref, p: ref.at[k])
    piece = [(s.shape[1], s.shape[2] // N_CHIPS if col else s.shape[2]) for s, col in zip(chip_sum, col_sharded)]
    landing = [lax.empty((len(CHIP_MASKS),) + s, BF16) for s in piece]
    copies = [(a, send_view(col_sharded[a], m), n + a, slot(k), m) for a in range(n) for k, m in enumerate(CHIP_MASKS)]
    (sems,), bufs, token = _split_start("rs_chips_start_" + tag, chip_sum + landing, [copies])
    return dict(sems=sems, bufs=bufs, copies=copies, token=token)


def _adamw_big(name, mine, other, w, m, v):
    depth, R, C = w.shape
    h = R // 2
    tr = _pick(h, tuple(t for t in (512, 256, 128, 64, 32, 16, 8) if t * C <= 256 * 1024) or (8,))
    nb = h // tr

    def g_spec(kk, hh):
        def imap(l, s, i, pr):
            before = (l < kk) | ((l == kk) & (s < hh))
            return (jnp.where((l == kk) & (s == hh), i, jnp.where(before, 0, nb - 1)), 0)
        return pl.BlockSpec((tr, C), imap)

    st_spec = pl.BlockSpec((None, tr, C), lambda l, s, i, pr: (l, jnp.where(s == 0, pr[0], 1 - pr[0]) * nb + i, 0))

    def body(*refs):
        g_refs = refs[:2 * depth]
        w_ref, m_ref, v_ref, go_ref, d_ref, mo_ref, vo_ref = refs[2 * depth:]
        l, s = pl.program_id(0), pl.program_id(1)
        for kk in range(depth):
            for hh in range(2):
                @pl.when((l == kk) & (s == hh))
                def _(kk=kk, hh=hh):
                    g = g_refs[2 * kk + hh][...]
                    d, mn, vn = _adam_math(w_ref[...], g, m_ref[...], v_ref[...])
                    go_ref[...] = g
                    d_ref[...] = d
                    mo_ref[...] = mn
                    vo_ref[...] = vn

    gs, g_specs = [], []
    for kk in range(depth):
        gs += [mine[kk], other[kk]]
        g_specs += [g_spec(kk, 0), g_spec(kk, 1)]
    return _placed_call(body, name, (depth, 2, nb), g_specs + [st_spec] * 3, [st_spec] * 4, [_sds(w.shape, F32)] * 4,
                        ("arbitrary", "arbitrary", "arbitrary"), gs + [w, m, v])


def _piece_view(col, j, other):
    def view(ref, p):
        R, C = ref.shape
        cc = (1 - p["c"]) if other else p["c"]
        if col:
            hr, sc = R // 2, C // N_CHIPS
            return ref.at[pl.ds(pl.multiple_of(cc * hr, 16), hr), pl.ds(j * sc, sc)]
        hr = R // (2 * N_CHIPS)
        return ref.at[pl.ds(pl.multiple_of((2 * j + cc) * hr, 8), hr), :]
    return view


def _piece_shape(shape, col):
    R, C = shape
    return (R // 2, C // N_CHIPS) if col else (R // (2 * N_CHIPS), C)


def _reduce_scatter(tag, grads, col_sharded, wire_dtype):
    n = len(grads)
    shapes = [_piece_shape(g.shape, col) for g, col in zip(grads, col_sharded)]

    slot = lambda j: (lambda ref, p: ref.at[j])
    remote = [(a, _piece_view(col_sharded[a], j, True), a, slot(j), SIBLING) for a in range(n) for j in range(N_CHIPS)]
    local = [(a, _piece_view(col_sharded[a], j, False), n + a, slot(j)) for a in range(n) for j in range(N_CHIPS)]
    got = _exchange("rs_sibling_" + tag, grads, [_sds((N_CHIPS,) + s, g.dtype) for s, g in zip(shapes, grads)] * 2, remote, local)
    theirs, mine = got[:n], got[n:]
    chip_sum = [_ew("rs_add2_" + tag, lambda a, b: (a.astype(F32) + b.astype(F32),),
                    [m.reshape(-1, m.shape[-1]), t.reshape(-1, t.shape[-1])], [wire_dtype])[0].reshape(m.shape)
                for m, t in zip(mine, theirs)]

    def send_view(mask):
        return lambda ref, p: ref.at[2 * ((1 - p["x"]) if mask[0] else p["x"]) + ((1 - p["y"]) if mask[1] else p["y"])]
    remote = [(a, send_view(m), a, slot(k), m) for a in range(n) for k, m in enumerate(CHIP_MASKS)]
    local = [(a, lambda ref, p: ref.at[p["chip"]], n + a, _whole) for a in range(n)]
    got = _exchange("rs_chips_" + tag, chip_sum,
                    [_sds((len(CHIP_MASKS),) + s, wire_dtype) for s in shapes] + [_sds(s, wire_dtype) for s in shapes], remote, local)
    landed, own = got[:n], got[n:]
    half = [_ew("rs_add4_" + tag, lambda o, a, b, c: (((o.astype(F32) + a.astype(F32)) + b.astype(F32)) + c.astype(F32),),
                [o, l[0], l[1], l[2]], [F32])[0] for o, l in zip(own, landed)]

    def half_rows(ref, p):
        hr = ref.shape[0] // 2
        return ref.at[pl.ds(pl.multiple_of(p["c"] * hr, 8), hr), :]
    remote = [(a, _whole, a, half_rows, SIBLING) for a in range(n)]
    local = [(a, _whole, a, half_rows) for a in range(n)]
    return _exchange("rs_halves_" + tag, half, [_sds((2 * s[0], s[1]), F32) for s in shapes], remote, local)


def _ssm_prepare(W):
    lr, li, ls = W['ssm_lambda_re'], W['ssm_lambda_im'], W['ssm_log_step']
    depth, G = ls.shape
    GG = depth * G
    flat = lambda a: a.reshape(-1, LANES)
    bc = lambda a: flat(jnp.broadcast_to(a, (depth, G, STATE, GROUP_CH)))
    flat3 = (bc(lr[..., None]), bc(li[..., None]), bc(ls[:, :, None, None]))
    bbr, bbi = _ssm_bbar(*flat3, flat(W['ssm_b_re']), flat(W['ssm_b_im']))
    bbr, bbi = bbr.reshape(GG, STATE, GROUP_CH), bbi.reshape(GG, STATE, GROUP_CH)
    row = lambda a: a.reshape(1, GG * STATE)
    tf, tr = _ssm_tables(row(lr), row(li), row(jnp.broadcast_to(ls[..., None], (depth, G, STATE))))
    cr = W['ssm_c_re'].reshape(GG, GROUP_CH, STATE).transpose(0, 2, 1)
    ci = -W['ssm_c_im'].reshape(GG, GROUP_CH, STATE).transpose(0, 2, 1)
    stacked = dict(wb=(_tile_w(bbr), _tile_w(bbi)), wbT=(_slab_w(bbr), _slab_w(bbi)),
                   wc=(_slab_w(cr), _slab_w(ci)), wcT=(_tile_w(cr), _tile_w(ci)))
    return flat3, [dict(stacked, tf=tf, tr=tr, layer=l, depth=depth) for l in range(depth)]


def _ssm_param_grads(W, flat3, raw):
    depth, G = W['ssm_log_step'].shape
    GG = depth * G
    cat = lambda k: jnp.concatenate([r[k] for r in raw], axis=0)
    flat = lambda a: a.reshape(-1, LANES)
    out = {}
    out['ssm_c_re'] = _slab_w_grad(cat(2)).transpose(0, 2, 1).reshape(W['ssm_c_re'].shape)
    out['ssm_c_im'] = -_slab_w_grad(cat(3)).transpose(0, 2, 1).reshape(W['ssm_c_im'].shape)
    dbr, dbi, qr, qi = _ssm_param_bwd_flat(*flat3, flat(W['ssm_b_re']), flat(W['ssm_b_im']),
                                           flat(_tile_w_grad(cat(0))), flat(_tile_w_grad(cat(1))))
    out['ssm_b_re'], out['ssm_b_im'] = dbr.reshape(W['ssm_b_re'].shape), dbi.reshape(W['ssm_b_im'].shape)
    pick = lambda q: q[:, ::GROUP_CH].reshape(GG, STATE)
    sums = lambda k: jnp.concatenate([r[k].reshape(8, G, STATE) for r in raw], axis=1)
    dlr, dli, dls = _ssm_param_bwd(W['ssm_lambda_re'].reshape(GG, STATE), W['ssm_lambda_im'].reshape(GG, STATE),
                                   W['ssm_log_step'].reshape(GG, 1), pick(qr), pick(qi), sums(4), sums(5))
    out['ssm_lambda_re'], out['ssm_lambda_im'] = dlr.reshape(depth, G, STATE), dli.reshape(depth, G, STATE)
    out['ssm_log_step'] = dls.reshape(depth, G)
    return out


def _layer_fwd(x, p, weight, dims):
    attn_w, kv_w, u_off = dims['attn_w'], dims['kv_w'], dims['u_off']
    s = p['s5']
    h = _rms_fwd("norm_mix", [x], [p['norm_mix_g']], BF16)
    w = {'w_in': weight('w_in', h)}
    proj, = _mm("mm_in", h, w['w_in'], 'nn', [F32])
    attn = _attn_fwd(proj, p['q_norm_g'], p['k_norm_g'], p['attn_sinks'], attn_w, kv_w)
    xr, xi, y, gl = _ssm_fwd(proj, u_off, s, p['ssm_d'])
    w['w_glu'] = weight('w_glu', gl)
    ssm, z = _mm("mm_glu", gl, w['w_glu'], 'nn', [F32, F32], extras=[('row', p['b_glu']), ('tile', gl)],
                 epi=lambda acc, b, g: ((lambda zz: (g * jax.nn.sigmoid(zz), zz))(acc + b)))
    mix = _rms_fwd("norm_heads", [attn, ssm], [p['attn_out_g'], p['ssm_out_g']], BF16)
    w['w_out'] = weight('w_out', mix)
    x_mid, = _mm("mm_out", mix, w['w_out'], 'nn', [F32], extras=[('tile', x)], epi=lambda acc, r: (acc + r,))
    h2 = _rms_fwd("norm_mlp", [x_mid], [p['norm_mlp_g']], BF16)
    w['w_up'] = weight('w_up', h2)
    a, r = _mm("mm_up", h2, w['w_up'], 'nn', [F32, BF16],
               epi=lambda acc: (acc, jnp.square(jnp.maximum(acc, 0.0))))
    w['w_down'] = weight('w_down', r)
    x_out, = _mm("mm_down", r, w['w_down'], 'nn', [F32], extras=[('tile', x_mid)], epi=lambda acc, rr: (acc + rr,))
    saved = dict(x=x, h=h, proj=proj, attn=attn, xr=xr, xi=xi, y=y, gl=gl, z=z, ssm=ssm, mix=mix, x_mid=x_mid, h2=h2, a=a, r=r, w=w)
    return x_out, saved


def _layer_bwd(dx, dx16, sv, p, dims, reduce_grads, tick, token_in):
    attn_w, kv_w, u_off = dims['attn_w'], dims['kv_w'], dims['u_off']
    s, w = p['s5'], sv['w']
    gb, gs = {}, {}
    da, = _mm("mm_down_dx", dx16, w['w_down'], 'nt', [BF16], extras=[('tile', sv['a'])],
              epi=lambda acc, a: (acc * (2.0 * jnp.maximum(a, 0.0)),))
    gb['w_down'], = _mm("mm_down_dw", sv['r'], dx16, 'tn', [BF16])
    dh2, = _mm("mm_up_dx", da, w['w_up'], 'nt', [F32])
    gb['w_up'], = _mm("mm_up_dw", sv['h2'], da, 'tn', [BF16])
    token = reduce_grads(('w_up', 'w_down'), [gb['w_up'], gb['w_down']]) + token_in
    (dx_mid,), (gs['norm_mlp_g'],), dx_mid16 = _rms_bwd("norm_mlp_bwd", [sv['x_mid']], [p['norm_mlp_g'] + token], dh2, resid=dx)
    dmix, = _mm("mm_out_dx", dx_mid16, w['w_out'], 'nt', [F32])
    gb['w_out'], = _mm("mm_out_dw", sv['mix'], dx_mid16, 'tn', [BF16])
    token = tick(dmix)
    (dattn, dssm), (gs['attn_out_g'], gs['ssm_out_g']) = _rms_bwd(
        "norm_heads_bwd", [sv['attn'], sv['ssm']], [p['attn_out_g'] + token, p['ssm_out_g']], dmix)
    dz, gs['b_glu'] = _glu_dz(dssm, sv['gl'], sv['z'])
    dy, = _mm("mm_glu_dx", dz, w['w_glu'], 'nt', [F32], extras=[('tile', dssm), ('tile', sv['z']), ('tile', sv['y'])],
              epi=lambda acc, ds, z, y: ((acc + ds * jax.nn.sigmoid(z)) * _gelu_grad(y),))
    gb['w_glu'], = _mm("mm_glu_dw", sv['gl'], dz, 'tn', [BF16])
    token = tick(dy)
    dproj, dkn, dv, gs['q_norm_g'], gs['attn_sinks'] = _attn_bwd(sv['proj'], sv['attn'], dattn, p['q_norm_g'] + token, p['k_norm_g'],
                                                                  p['attn_sinks'], attn_w, kv_w)
    dproj, gs['k_norm_g'] = _knorm_bwd(sv['proj'], dkn, dv, p['k_norm_g'], dproj, attn_w, kv_w)
    dproj, gs['ssm_d'], *gs['s5_raw'] = _ssm_bwd(dy, sv['proj'], u_off, sv['xr'], sv['xi'], s, p['ssm_d'], dproj)
    dh, = _mm("mm_in_dx", dproj, w['w_in'], 'nt', [F32])
    gb['w_in'], = _mm("mm_in_dw", sv['h'], dproj, 'tn', [BF16])
    token = reduce_grads(('w_in', 'w_glu', 'w_out'), [gb['w_in'], gb['w_glu'], gb['w_out']])
    (dx_in,), (gs['norm_mix_g'],), dx_in16 = _rms_bwd("norm_mix_bwd", [sv['x']], [p['norm_mix_g'] + token], dh, resid=dx_mid)
    return dx_in, dx_in16, gs


PACK_COLS = 1024


def _pack(arrs, rows):
    flat = jnp.concatenate([a.reshape(-1).astype(F32) for a in arrs])
    return jnp.pad(flat, (0, rows * PACK_COLS - flat.shape[0])).reshape(rows, PACK_COLS)


def _unpack(packed, shapes):
    flat = packed.reshape(-1)
    out, off = [], 0
    for s in shapes:
        n = int(np.prod(s))
        out.append(flat[off:off + n].reshape(s))
        off += n
    return out


def _pack_rows(shapes, multiple):
    n = sum(int(np.prod(s)) for s in shapes)
    rows = -(-n // PACK_COLS)
    return -(-rows // multiple) * multiple


def kernel(x, meta_tokens, norm_mix_g, w_in, q_norm_g, k_norm_g, attn_sinks, ssm_lambda_re, ssm_lambda_im, ssm_log_step, ssm_b_re, ssm_b_im, ssm_c_re, ssm_c_im, ssm_d, w_glu, b_glu, attn_out_g, ssm_out_g, w_out, norm_mlp_g, w_up, w_down, loss_target, m_meta_tokens, m_norm_mix_g, m_w_in, m_q_norm_g, m_k_norm_g, m_attn_sinks, m_ssm_lambda_re, m_ssm_lambda_im, m_ssm_log_step, m_ssm_b_re, m_ssm_b_im, m_ssm_c_re, m_ssm_c_im, m_ssm_d, m_w_glu, m_b_glu, m_attn_out_g, m_ssm_out_g, m_w_out, m_norm_mlp_g, m_w_up, m_w_down, v_meta_tokens, v_norm_mix_g, v_w_in, v_q_norm_g, v_k_norm_g, v_attn_sinks, v_ssm_lambda_re, v_ssm_lambda_im, v_ssm_log_step, v_ssm_b_re, v_ssm_b_im, v_ssm_c_re, v_ssm_c_im, v_ssm_d, v_w_glu, v_b_glu, v_attn_out_g, v_ssm_out_g, v_w_out, v_norm_mlp_g, v_w_up, v_w_down):
    args = (meta_tokens, norm_mix_g, w_in, q_norm_g, k_norm_g, attn_sinks, ssm_lambda_re, ssm_lambda_im, ssm_log_step, ssm_b_re, ssm_b_im, ssm_c_re, ssm_c_im, ssm_d, w_glu, b_glu, attn_out_g, ssm_out_g, w_out, norm_mlp_g, w_up, w_down)
    ms = (m_meta_tokens, m_norm_mix_g, m_w_in, m_q_norm_g, m_k_norm_g, m_attn_sinks, m_ssm_lambda_re, m_ssm_lambda_im, m_ssm_log_step, m_ssm_b_re, m_ssm_b_im, m_ssm_c_re, m_ssm_c_im, m_ssm_d, m_w_glu, m_b_glu, m_attn_out_g, m_ssm_out_g, m_w_out, m_norm_mlp_g, m_w_up, m_w_down)
    vs = (v_meta_tokens, v_norm_mix_g, v_w_in, v_q_norm_g, v_k_norm_g, v_attn_sinks, v_ssm_lambda_re, v_ssm_lambda_im, v_ssm_log_step, v_ssm_b_re, v_ssm_b_im, v_ssm_c_re, v_ssm_c_im, v_ssm_d, v_w_glu, v_b_glu, v_attn_out_g, v_ssm_out_g, v_w_out, v_norm_mlp_g, v_w_up, v_w_down)
    W = dict(zip(WEIGHTS, args))
    M = dict(zip(WEIGHTS, ms))
    V = dict(zip(WEIGHTS, vs))
    depth = norm_mix_g.shape[0]
    seq, D = x.shape[1], x.shape[2]
    attn_w = D // 2
    kv_w = attn_w // KV_GROUP
    dims = dict(attn_w=attn_w, kv_w=kv_w, u_off=(attn_w + 2 * kv_w) // LANES)
    small_names = [n for n in WEIGHTS if n not in BIG and n != 'meta_tokens']
    chip = 2 * lax.axis_index("x") + lax.axis_index("y")

    gathers, started = [], jnp.zeros((), F32)
    for l in range(depth):
        placed = [_ag_place("ag_place_" + n, W[n], l, COL_SHARDED[n], BF16) for n in BIG]
        groups = [_ag_copies(a, COL_SHARDED[n]) for a, n in enumerate(BIG)]
        if l == 0:
            placed = [_ag_place("ag_place_meta", meta_tokens[None], 0, True, F32)] + placed
            groups = [_ag_copies(0, True)] + [_ag_copies(a + 1, COL_SHARDED[n]) for a, n in enumerate(BIG)]
        sems, bufs, token = _split_start("ag_start_%d" % l, placed, groups)
        gathers.append(dict(zip((['meta_tokens'] if l == 0 else []) + BIG, zip(sems, bufs))))
        started = started + token[0, 0]

    def gathered(l, n, after):
        sems, buf = gathers[l][n]
        return _split_wait("ag_wait_%d_%s" % (l, n), [buf], sems, after, _ag_copies(0, n == 'meta_tokens' or COL_SHARDED[n]))[0]

    h_res = jnp.concatenate([jnp.zeros((PAD, D), F32), gathered(0, 'meta_tokens', started.reshape(1, 1)), x[0]], axis=0)
    s5_flat3, s5_layers = _ssm_prepare(W)
    layer_p = []
    for l in range(depth):
        p = {n: W[n][l][None, :] for n in ('norm_mix_g', 'q_norm_g', 'k_norm_g', 'attn_sinks', 'ssm_d', 'b_glu', 'attn_out_g',
                                             'ssm_out_g', 'norm_mlp_g')}
        p['s5'] = s5_layers[l]
        layer_p.append(p)
    saved = []
    for l in range(depth):
        h_res, sv = _layer_fwd(h_res, layer_p[l], functools.partial(gathered, l), dims)
        saved.append(sv)
    loss_local, dx, dx16 = _loss(h_res, loss_target[0])
    loss = lax.psum(loss_local, ("x", "y", "c"))

    small_grads = [None] * depth
    shard_grads = {}
    pending = []

    def reduce_grads(l, names, grads):
        st = _rs_start("%d_%s" % (l, names[0]), list(grads), [COL_SHARDED[n] for n in names])
        st.update(layer=l, names=names, fresh=True)
        pending.append(st)
        return st['token'][0, 0]

    def tick(after):
        token = jnp.zeros((), F32)
        for st in list(pending):
            if st['fresh']:
                st['fresh'] = False
                continue
            _rs_advance(st, after)
            if 'result' in st:
                pending.remove(st)
                for a, n in enumerate(st['names']):
                    shard_grads[(st['layer'], n)] = (st['result'][0][a], st['result'][1][a])
            else:
                token = token + st['token'][0, 0]
        return token

    token = jnp.zeros((), F32)
    for l in reversed(range(depth)):
        dx, dx16, gs = _layer_bwd(dx, dx16, saved[l], layer_p[l], dims, functools.partial(reduce_grads, l), tick, token)
        saved[l] = None
        small_grads[l] = gs
        token = tick(dx)
    grad_x = dx[BLOCK:].reshape(x.shape)

    g_small = _ssm_param_grads(W, s5_flat3, [small_grads[l]['s5_raw'] for l in range(depth)])
    for n in small_names:
        if n not in g_small:
            g_small[n] = jnp.stack([small_grads[l][n].reshape(W[n].shape[1:]) for l in range(depth)])
    g_shapes = [(N_META, D)] + [W[n].shape for n in small_names]
    rows = _pack_rows(g_shapes, 8 * 2 * N_CHIPS)
    packed = _pack([dx[PAD:BLOCK]] + [g_small[n] for n in small_names], rows)
    out = {}

    def adamw_big(n):
        return _adamw_big("adamw_" + n, [shard_grads[(l, n)][0] for l in range(depth)],
                          [shard_grads[(l, n)][1] for l in range(depth)], W[n], M[n], V[n])

    early = ('w_up', 'w_down')
    tick(packed)
    while any((l, n) not in shard_grads for l in range(depth) for n in early):
        tick(packed)
    for n in early:
        out[n] = adamw_big(n)
    tick(out[early[-1]][1])
    red, = _reduce_scatter("small", [packed], [False], F32)
    red_full, = _all_gather("ag_small", [red], [False])
    while pending:
        tick(red_full)
    g_list = _unpack(red_full, g_shapes)
    g_meta = lax.dynamic_slice_in_dim(g_list[0], chip * meta_tokens.shape[1], meta_tokens.shape[1], axis=1)
    G = dict(zip(small_names, g_list[1:]))
    G['meta_tokens'] = g_meta

    for n in BIG:
        if n not in out:
            out[n] = adamw_big(n)
    for n in ['meta_tokens'] + small_names:
        rows2d = lambda a: a.reshape(-1, a.shape[-1])
        upd = _ew("adamw_" + n, _adam_math, [rows2d(W[n]), rows2d(G[n]), rows2d(M[n]), rows2d(V[n])], [F32] * 3)
        out[n] = (G[n], *[u.reshape(W[n].shape) for u in upd])
    return (loss, grad_x, *[out[n][0] for n in WEIGHTS], *[out[n][1] for n in WEIGHTS],
            *[out[n][2] for n in WEIGHTS], *[out[n][3] for n in WEIGHTS])
```
